```python
import jax, jax.numpy as jnp
from jax import lax
import numpy as np

D_MODEL = 1024
BATCH = 8
SEQ = 16384
DEPTH = 4

MLA_HEADS = 8
QK_NOPE_DIM = 64
QK_ROPE_DIM = 32
V_HEAD_DIM = 64
Q_LORA_RANK = 384
KV_LORA_RANK = 256
ROPE_THETA = 10000.0
Q_BLOCK = 128
SG_GROUPS = 8
SG_GROUP_DIM = 64
SG_WIDTH = SG_GROUPS * SG_GROUP_DIM
SG_CHUNK = 128
CONV_WIDTH = D_MODEL
CONV_K = 3
D_FF = 2816
NORM_EPS = 1e-6

MLA_OUT = MLA_HEADS * V_HEAD_DIM
MIX_WIDTH = MLA_OUT + SG_WIDTH
QK_HEAD_DIM = QK_NOPE_DIM + QK_ROPE_DIM
EVEN_IN = Q_LORA_RANK + KV_LORA_RANK + QK_ROPE_DIM + 2 * SG_WIDTH
N_EVEN = (DEPTH + 1) // 2
N_ODD = DEPTH // 2

kernel_name = "macaron_mla_sgu_shortconv_hybrid"


def rms_norm(x, g):
    x32 = x.astype(jnp.float32)
    y = x32 * lax.rsqrt(jnp.mean(x32 * x32, axis=-1, keepdims=True) + NORM_EPS)
    return (y * g.astype(jnp.float32)).astype(x.dtype)


def swiglu(h, w_gate, w_up, w_down):
    return (jax.nn.silu(h @ w_gate) * (h @ w_up)) @ w_down


def rope_tables(positions):
    inv_freq = ROPE_THETA ** (-jnp.arange(0, QK_ROPE_DIM, 2, dtype=jnp.float32) / QK_ROPE_DIM)
    ang = positions.astype(jnp.float32)[..., None] * inv_freq
    return jnp.cos(ang), jnp.sin(ang)


def apply_rope(t, cos, sin):
    t32 = t.astype(jnp.float32)
    t1, t2 = jnp.split(t32, 2, axis=-1)
    out = jnp.concatenate([t1 * cos - t2 * sin, t1 * sin + t2 * cos], axis=-1)
    return out.astype(t.dtype)


def mla_attention(q_nope, q_rope, k_nope, k_rope, v):
    B, S, H, _ = q_nope.shape
    nb = S // Q_BLOCK
    scale = QK_HEAD_DIM ** -0.5
    k_idx = jnp.arange(S)

    def to_blocks(t):
        return jnp.moveaxis(t.reshape(B, nb, Q_BLOCK, *t.shape[2:]), 1, 0)

    def one_block(args):
        qn, qr, i = args
        s = (jnp.einsum('bqhd,bkhd->bhqk', qn, k_nope, preferred_element_type=jnp.float32)
             + jnp.einsum('bqhr,bkr->bhqk', qr, k_rope, preferred_element_type=jnp.float32)) * scale
        q_idx = i * Q_BLOCK + jnp.arange(Q_BLOCK)
        s = jnp.where(k_idx[None, :] <= q_idx[:, None], s, -jnp.inf)
        p = jax.nn.softmax(s, axis=-1).astype(v.dtype)
        return jnp.einsum('bhqk,bkhd->bqhd', p, v)

    out = lax.map(one_block, (to_blocks(q_nope), to_blocks(q_rope), jnp.arange(nb)))
    return jnp.moveaxis(out, 0, 1).reshape(B, S, H * V_HEAD_DIM)


def spatial_gating(z, sg_norm, sg_w, sg_b):
    u, v = jnp.split(z, 2, axis=-1)
    v = rms_norm(v, sg_norm)
    B, S, _ = v.shape
    nc = S // SG_CHUNK
    v = v.reshape(B, nc, SG_CHUNK, SG_GROUPS, SG_GROUP_DIM)
    w = sg_w * jnp.tril(jnp.ones((SG_CHUNK, SG_CHUNK), dtype=sg_w.dtype))
    mixed = jnp.einsum('gts,bnsgc->bntgc', w, v) + sg_b.T[None, None, :, :, None]
    return u * mixed.reshape(B, S, SG_WIDTH)


def mla_sgu_mixer(h, cos, sin, w_in, q_norm, w_uq, kv_norm, w_ukv, sg_norm, sg_w, sg_b, w_out):
    B, S, _ = h.shape
    proj = h @ w_in
    c_q, c_kv, k_rope, z = jnp.split(
        proj, [Q_LORA_RANK, Q_LORA_RANK + KV_LORA_RANK, Q_LORA_RANK + KV_LORA_RANK + QK_ROPE_DIM], axis=-1)
    q = (rms_norm(c_q, q_norm) @ w_uq).reshape(B, S, MLA_HEADS, QK_HEAD_DIM)
    q_nope = q[..., :QK_NOPE_DIM]
    q_rope = apply_rope(q[..., QK_NOPE_DIM:], cos[:, :, None, :], sin[:, :, None, :])
    k_rope = apply_rope(k_rope, cos, sin)
    kv = (rms_norm(c_kv, kv_norm) @ w_ukv).reshape(B, S, MLA_HEADS, QK_NOPE_DIM + V_HEAD_DIM)
    k_nope, v = kv[..., :QK_NOPE_DIM], kv[..., QK_NOPE_DIM:]
    attn = mla_attention(q_nope, q_rope, k_nope, k_rope, v)
    sg = spatial_gating(jax.nn.gelu(z, approximate=False), sg_norm, sg_w, sg_b)
    return jnp.concatenate([attn, sg], axis=-1) @ w_out


def short_conv_mixer(h, w_in, conv_w, w_out):
    b_gate, c_gate, z = jnp.split(h @ w_in, 3, axis=-1)
    y = lax.conv_general_dilated(
        c_gate * z, conv_w[:, None, :], window_strides=(1,), padding=[(CONV_K - 1, 0)],
        dimension_numbers=('NWC', 'WIO', 'NWC'), feature_group_count=CONV_WIDTH)
    return (b_gate * y) @ w_out


def _fwd_setup_inputs(seed: int = 0) -> dict:
    key = jax.random.key(seed)
    keys = iter(jax.random.split(key, 32))
    f32 = jnp.float32

    def dense(shape, fan_in):
        return jax.random.normal(next(keys), shape, f32) * (fan_in ** -0.5)

    def gain(shape):
        return 1.0 + 0.1 * jax.random.normal(next(keys), shape, f32)

    x = jax.random.normal(next(keys), (BATCH, SEQ, D_MODEL), f32)
    offset = jax.random.randint(next(keys), (BATCH, 1), 0, 1024, dtype=jnp.int32)
    positions = offset + jnp.arange(SEQ, dtype=jnp.int32)[None, :]
    return {
        "x": x,
        "positions": positions,
        "ffn_pre_norm": gain((DEPTH, D_MODEL)),
        "ffn_pre_w_gate": dense((DEPTH, D_MODEL, D_FF), D_MODEL),
        "ffn_pre_w_up": dense((DEPTH, D_MODEL, D_FF), D_MODEL),
        "ffn_pre_w_down": dense((DEPTH, D_FF, D_MODEL), D_FF),
        "mix_norm": gain((DEPTH, D_MODEL)),
        "ffn_post_norm": gain((DEPTH, D_MODEL)),
        "ffn_post_w_gate": dense((DEPTH, D_MODEL, D_FF), D_MODEL),
        "ffn_post_w_up": dense((DEPTH, D_MODEL, D_FF), D_MODEL),
        "ffn_post_w_down": dense((DEPTH, D_FF, D_MODEL), D_FF),
        "even_w_in": dense((N_EVEN, D_MODEL, EVEN_IN), D_MODEL),
        "q_norm": gain((N_EVEN, Q_LORA_RANK)),
        "w_uq": dense((N_EVEN, Q_LORA_RANK, MLA_HEADS * QK_HEAD_DIM), Q_LORA_RANK),
        "kv_norm": gain((N_EVEN, KV_LORA_RANK)),
        "w_ukv": dense((N_EVEN, KV_LORA_RANK, MLA_HEADS * (QK_NOPE_DIM + V_HEAD_DIM)), KV_LORA_RANK),
        "sg_norm": gain((N_EVEN, SG_WIDTH)),
        "sg_w": dense((N_EVEN, SG_GROUPS, SG_CHUNK, SG_CHUNK), SG_CHUNK),
        "sg_b": gain((N_EVEN, SG_GROUPS, SG_CHUNK)),
        "even_w_out": dense((N_EVEN, MIX_WIDTH, D_MODEL), MIX_WIDTH),
        "conv_w_in": dense((N_ODD, D_MODEL, 3 * CONV_WIDTH), D_MODEL),
        "conv_w": dense((N_ODD, CONV_K, CONV_WIDTH), CONV_K),
        "conv_w_out": dense((N_ODD, CONV_WIDTH, D_MODEL), CONV_WIDTH),
        "final_norm": gain((D_MODEL,)),
    }


def _fwd_reference(x, positions, ffn_pre_norm, ffn_pre_w_gate, ffn_pre_w_up, ffn_pre_w_down,
              mix_norm, ffn_post_norm, ffn_post_w_gate, ffn_post_w_up, ffn_post_w_down,
              even_w_in, q_norm, w_uq, kv_norm, w_ukv, sg_norm, sg_w, sg_b, even_w_out,
              conv_w_in, conv_w, conv_w_out, final_norm):
    cos, sin = rope_tables(positions)
    for layer in range(DEPTH):
        x = x + 0.5 * swiglu(rms_norm(x, ffn_pre_norm[layer]),
                             ffn_pre_w_gate[layer], ffn_pre_w_up[layer], ffn_pre_w_down[layer])
        h = rms_norm(x, mix_norm[layer])
        if layer % 2 == 0:
            e = layer // 2
            x = x + mla_sgu_mixer(h, cos, sin, even_w_in[e], q_norm[e], w_uq[e], kv_norm[e],
                                  w_ukv[e], sg_norm[e], sg_w[e], sg_b[e], even_w_out[e])
        else:
            o = layer // 2
            x = x + short_conv_mixer(h, conv_w_in[o], conv_w[o], conv_w_out[o])
        x = x + 0.5 * swiglu(rms_norm(x, ffn_post_norm[layer]),
                             ffn_post_w_gate[layer], ffn_post_w_up[layer], ffn_post_w_down[layer])
    return rms_norm(x, final_norm)


import jax as _jax
import jax.numpy as _jnp

TWIN_FORMAT = 'train_step'
FWD_PARAMS = ['x', 'positions', 'ffn_pre_norm', 'ffn_pre_w_gate', 'ffn_pre_w_up', 'ffn_pre_w_down', 'mix_norm', 'ffn_post_norm', 'ffn_post_w_gate', 'ffn_post_w_up', 'ffn_post_w_down', 'even_w_in', 'q_norm', 'w_uq', 'kv_norm', 'w_ukv', 'sg_norm', 'sg_w', 'sg_b', 'even_w_out', 'conv_w_in', 'conv_w', 'conv_w_out', 'final_norm']
TWIN_WEIGHTS = ['ffn_pre_norm', 'ffn_pre_w_gate', 'ffn_pre_w_up', 'ffn_pre_w_down', 'mix_norm', 'ffn_post_norm', 'ffn_post_w_gate', 'ffn_post_w_up', 'ffn_post_w_down', 'even_w_in', 'q_norm', 'w_uq', 'kv_norm', 'w_ukv', 'sg_norm', 'sg_w', 'sg_b', 'even_w_out', 'conv_w_in', 'conv_w', 'conv_w_out', 'final_norm']
TWIN_DIFF_INPUT = 'x'
TWIN_INPUTS = ['x', 'positions', 'ffn_pre_norm', 'ffn_pre_w_gate', 'ffn_pre_w_up', 'ffn_pre_w_down', 'mix_norm', 'ffn_post_norm', 'ffn_post_w_gate', 'ffn_post_w_up', 'ffn_post_w_down', 'even_w_in', 'q_norm', 'w_uq', 'kv_norm', 'w_ukv', 'sg_norm', 'sg_w', 'sg_b', 'even_w_out', 'conv_w_in', 'conv_w', 'conv_w_out', 'final_norm', 'loss_target', 'm_ffn_pre_norm', 'm_ffn_pre_w_gate', 'm_ffn_pre_w_up', 'm_ffn_pre_w_down', 'm_mix_norm', 'm_ffn_post_norm', 'm_ffn_post_w_gate', 'm_ffn_post_w_up', 'm_ffn_post_w_down', 'm_even_w_in', 'm_q_norm', 'm_w_uq', 'm_kv_norm', 'm_w_ukv', 'm_sg_norm', 'm_sg_w', 'm_sg_b', 'm_even_w_out', 'm_conv_w_in', 'm_conv_w', 'm_conv_w_out', 'm_final_norm', 'v_ffn_pre_norm', 'v_ffn_pre_w_gate', 'v_ffn_pre_w_up', 'v_ffn_pre_w_down', 'v_mix_norm', 'v_ffn_post_norm', 'v_ffn_post_w_gate', 'v_ffn_post_w_up', 'v_ffn_post_w_down', 'v_even_w_in', 'v_q_norm', 'v_w_uq', 'v_kv_norm', 'v_w_ukv', 'v_sg_norm', 'v_sg_w', 'v_sg_b', 'v_even_w_out', 'v_conv_w_in', 'v_conv_w', 'v_conv_w_out', 'v_final_norm']
TWIN_OUTPUTS = ['loss', 'grad_x', 'grad_ffn_pre_norm', 'grad_ffn_pre_w_gate', 'grad_ffn_pre_w_up', 'grad_ffn_pre_w_down', 'grad_mix_norm', 'grad_ffn_post_norm', 'grad_ffn_post_w_gate', 'grad_ffn_post_w_up', 'grad_ffn_post_w_down', 'grad_even_w_in', 'grad_q_norm', 'grad_w_uq', 'grad_kv_norm', 'grad_w_ukv', 'grad_sg_norm', 'grad_sg_w', 'grad_sg_b', 'grad_even_w_out', 'grad_conv_w_in', 'grad_conv_w', 'grad_conv_w_out', 'grad_final_norm', 'delta_ffn_pre_norm', 'delta_ffn_pre_w_gate', 'delta_ffn_pre_w_up', 'delta_ffn_pre_w_down', 'delta_mix_norm', 'delta_ffn_post_norm', 'delta_ffn_post_w_gate', 'delta_ffn_post_w_up', 'delta_ffn_post_w_down', 'delta_even_w_in', 'delta_q_norm', 'delta_w_uq', 'delta_kv_norm', 'delta_w_ukv', 'delta_sg_norm', 'delta_sg_w', 'delta_sg_b', 'delta_even_w_out', 'delta_conv_w_in', 'delta_conv_w', 'delta_conv_w_out', 'delta_final_norm', 'new_m_ffn_pre_norm', 'new_m_ffn_pre_w_gate', 'new_m_ffn_pre_w_up', 'new_m_ffn_pre_w_down', 'new_m_mix_norm', 'new_m_ffn_post_norm', 'new_m_ffn_post_w_gate', 'new_m_ffn_post_w_up', 'new_m_ffn_post_w_down', 'new_m_even_w_in', 'new_m_q_norm', 'new_m_w_uq', 'new_m_kv_norm', 'new_m_w_ukv', 'new_m_sg_norm', 'new_m_sg_w', 'new_m_sg_b', 'new_m_even_w_out', 'new_m_conv_w_in', 'new_m_conv_w', 'new_m_conv_w_out', 'new_m_final_norm', 'new_v_ffn_pre_norm', 'new_v_ffn_pre_w_gate', 'new_v_ffn_pre_w_up', 'new_v_ffn_pre_w_down', 'new_v_mix_norm', 'new_v_ffn_post_norm', 'new_v_ffn_post_w_gate', 'new_v_ffn_post_w_up', 'new_v_ffn_post_w_down', 'new_v_even_w_in', 'new_v_q_norm', 'new_v_w_uq', 'new_v_kv_norm', 'new_v_w_ukv', 'new_v_sg_norm', 'new_v_sg_w', 'new_v_sg_b', 'new_v_even_w_out', 'new_v_conv_w_in', 'new_v_conv_w', 'new_v_conv_w_out', 'new_v_final_norm']
TWIN_LEAF_KINDS = {'loss': 'loss', 'grad_x': 'grad_x', 'grad_ffn_pre_norm': 'grad_w', 'grad_ffn_pre_w_gate': 'grad_w', 'grad_ffn_pre_w_up': 'grad_w', 'grad_ffn_pre_w_down': 'grad_w', 'grad_mix_norm': 'grad_w', 'grad_ffn_post_norm': 'grad_w', 'grad_ffn_post_w_gate': 'grad_w', 'grad_ffn_post_w_up': 'grad_w', 'grad_ffn_post_w_down': 'grad_w', 'grad_even_w_in': 'grad_w', 'grad_q_norm': 'grad_w', 'grad_w_uq': 'grad_w', 'grad_kv_norm': 'grad_w', 'grad_w_ukv': 'grad_w', 'grad_sg_norm': 'grad_w', 'grad_sg_w': 'grad_w', 'grad_sg_b': 'grad_w', 'grad_even_w_out': 'grad_w', 'grad_conv_w_in': 'grad_w', 'grad_conv_w': 'grad_w', 'grad_conv_w_out': 'grad_w', 'grad_final_norm': 'grad_w', 'delta_ffn_pre_norm': 'delta_w', 'delta_ffn_pre_w_gate': 'delta_w', 'delta_ffn_pre_w_up': 'delta_w', 'delta_ffn_pre_w_down': 'delta_w', 'delta_mix_norm': 'delta_w', 'delta_ffn_post_norm': 'delta_w', 'delta_ffn_post_w_gate': 'delta_w', 'delta_ffn_post_w_up': 'delta_w', 'delta_ffn_post_w_down': 'delta_w', 'delta_even_w_in': 'delta_w', 'delta_q_norm': 'delta_w', 'delta_w_uq': 'delta_w', 'delta_kv_norm': 'delta_w', 'delta_w_ukv': 'delta_w', 'delta_sg_norm': 'delta_w', 'delta_sg_w': 'delta_w', 'delta_sg_b': 'delta_w', 'delta_even_w_out': 'delta_w', 'delta_conv_w_in': 'delta_w', 'delta_conv_w': 'delta_w', 'delta_conv_w_out': 'delta_w', 'delta_final_norm': 'delta_w', 'new_m_ffn_pre_norm': 'new_m', 'new_m_ffn_pre_w_gate': 'new_m', 'new_m_ffn_pre_w_up': 'new_m', 'new_m_ffn_pre_w_down': 'new_m', 'new_m_mix_norm': 'new_m', 'new_m_ffn_post_norm': 'new_m', 'new_m_ffn_post_w_gate': 'new_m', 'new_m_ffn_post_w_up': 'new_m', 'new_m_ffn_post_w_down': 'new_m', 'new_m_even_w_in': 'new_m', 'new_m_q_norm': 'new_m', 'new_m_w_uq': 'new_m', 'new_m_kv_norm': 'new_m', 'new_m_w_ukv': 'new_m', 'new_m_sg_norm': 'new_m', 'new_m_sg_w': 'new_m', 'new_m_sg_b': 'new_m', 'new_m_even_w_out': 'new_m', 'new_m_conv_w_in': 'new_m', 'new_m_conv_w': 'new_m', 'new_m_conv_w_out': 'new_m', 'new_m_final_norm': 'new_m', 'new_v_ffn_pre_norm': 'new_v', 'new_v_ffn_pre_w_gate': 'new_v', 'new_v_ffn_pre_w_up': 'new_v', 'new_v_ffn_pre_w_down': 'new_v', 'new_v_mix_norm': 'new_v', 'new_v_ffn_post_norm': 'new_v', 'new_v_ffn_post_w_gate': 'new_v', 'new_v_ffn_post_w_up': 'new_v', 'new_v_ffn_post_w_down': 'new_v', 'new_v_even_w_in': 'new_v', 'new_v_q_norm': 'new_v', 'new_v_w_uq': 'new_v', 'new_v_kv_norm': 'new_v', 'new_v_w_ukv': 'new_v', 'new_v_sg_norm': 'new_v', 'new_v_sg_w': 'new_v', 'new_v_sg_b': 'new_v', 'new_v_even_w_out': 'new_v', 'new_v_conv_w_in': 'new_v', 'new_v_conv_w': 'new_v', 'new_v_conv_w_out': 'new_v', 'new_v_final_norm': 'new_v'}


def _forward(args):
    return _fwd_reference(*[args[k] for k in FWD_PARAMS])


def _output_shape():
    def fwd():
        inp = _fwd_setup_inputs(0)
        return _fwd_reference(*[inp[k] for k in FWD_PARAMS])
    out = _jax.eval_shape(fwd)
    return out.shape, out.dtype

N_MICROBATCH = 1
ADAM_LR = 0.001
ADAM_B1 = 0.9
ADAM_B2 = 0.999
ADAM_EPS = 1e-08
ADAM_WD = 0.01
ADAM_STEP = 10
PER_EXAMPLE_BATCH_AXIS = {'x': 0, 'positions': 0, 'loss_target': 0}
SHARED_INPUTS = []
_WEIGHT_DTYPES = {'ffn_pre_norm': _jnp.float32, 'ffn_pre_w_gate': _jnp.float32, 'ffn_pre_w_up': _jnp.float32, 'ffn_pre_w_down': _jnp.float32, 'mix_norm': _jnp.float32, 'ffn_post_norm': _jnp.float32, 'ffn_post_w_gate': _jnp.float32, 'ffn_post_w_up': _jnp.float32, 'ffn_post_w_down': _jnp.float32, 'even_w_in': _jnp.float32, 'q_norm': _jnp.float32, 'w_uq': _jnp.float32, 'kv_norm': _jnp.float32, 'w_ukv': _jnp.float32, 'sg_norm': _jnp.float32, 'sg_w': _jnp.float32, 'sg_b': _jnp.float32, 'even_w_out': _jnp.float32, 'conv_w_in': _jnp.float32, 'conv_w': _jnp.float32, 'conv_w_out': _jnp.float32, 'final_norm': _jnp.float32}
MOMENT_SCALE = {'ffn_pre_norm': 2.147187e-01, 'ffn_pre_w_gate': 8.806557e-02, 'ffn_pre_w_up': 8.573040e-02, 'ffn_pre_w_down': 1.420003e-01, 'mix_norm': 3.858596e-01, 'ffn_post_norm': 1.588969e-01, 'ffn_post_w_gate': 6.492632e-02, 'ffn_post_w_up': 6.380738e-02, 'ffn_post_w_down': 1.061128e-01, 'even_w_in': 2.498283e-01, 'q_norm': 9.697520e-02, 'w_uq': 6.954832e-02, 'kv_norm': 1.768371e-01, 'w_ukv': 9.022990e-02, 'sg_norm': 2.015846e-01, 'sg_w': 1.444500e-01, 'sg_b': 2.090295e-01, 'even_w_out': 3.037458e-01, 'conv_w_in': 2.510908e-01, 'conv_w': 2.647472e-01, 'conv_w_out': 2.720563e-01, 'final_norm': 1.280790e+02}


def _to_microbatches(a, axis):
    t = _jnp.moveaxis(a, axis, 0)
    t = t.reshape((N_MICROBATCH, t.shape[0] // N_MICROBATCH) + t.shape[1:])
    return _jnp.moveaxis(t, 1, axis + 1)


def setup_inputs(seed: int = 0) -> dict:
    inp = _fwd_setup_inputs(seed)
    key = _jax.random.fold_in(_jax.random.key(seed), 7919)
    shape, _ = _output_shape()
    out = dict(inp)
    out["loss_target"] = _jax.random.normal(_jax.random.fold_in(key, 0), shape, _jnp.float32)
    for i, name in enumerate(TWIN_WEIGHTS):
        w = inp[name].astype(_jnp.float32)
        if MOMENT_SCALE is None:
            s = _jnp.sqrt(_jnp.mean(_jnp.square(w)) + 1e-30)
        else:
            s = MOMENT_SCALE[name]
        km, kv = _jax.random.split(_jax.random.fold_in(key, i + 1))
        out[name] = w
        out["m_" + name] = s * _jax.random.normal(km, w.shape, _jnp.float32)
        out["v_" + name] = (s * s) * _jax.random.uniform(kv, w.shape, _jnp.float32, 0.5, 1.5)
    if N_MICROBATCH > 1:
        for name, axis in PER_EXAMPLE_BATCH_AXIS.items():
            out[name] = _to_microbatches(out[name], axis)
    return {'x': out['x'], 'positions': out['positions'], 'ffn_pre_norm': out['ffn_pre_norm'], 'ffn_pre_w_gate': out['ffn_pre_w_gate'], 'ffn_pre_w_up': out['ffn_pre_w_up'], 'ffn_pre_w_down': out['ffn_pre_w_down'], 'mix_norm': out['mix_norm'], 'ffn_post_norm': out['ffn_post_norm'], 'ffn_post_w_gate': out['ffn_post_w_gate'], 'ffn_post_w_up': out['ffn_post_w_up'], 'ffn_post_w_down': out['ffn_post_w_down'], 'even_w_in': out['even_w_in'], 'q_norm': out['q_norm'], 'w_uq': out['w_uq'], 'kv_norm': out['kv_norm'], 'w_ukv': out['w_ukv'], 'sg_norm': out['sg_norm'], 'sg_w': out['sg_w'], 'sg_b': out['sg_b'], 'even_w_out': out['even_w_out'], 'conv_w_in': out['conv_w_in'], 'conv_w': out['conv_w'], 'conv_w_out': out['conv_w_out'], 'final_norm': out['final_norm'], 'loss_target': out['loss_target'], 'm_ffn_pre_norm': out['m_ffn_pre_norm'], 'm_ffn_pre_w_gate': out['m_ffn_pre_w_gate'], 'm_ffn_pre_w_up': out['m_ffn_pre_w_up'], 'm_ffn_pre_w_down': out['m_ffn_pre_w_down'], 'm_mix_norm': out['m_mix_norm'], 'm_ffn_post_norm': out['m_ffn_post_norm'], 'm_ffn_post_w_gate': out['m_ffn_post_w_gate'], 'm_ffn_post_w_up': out['m_ffn_post_w_up'], 'm_ffn_post_w_down': out['m_ffn_post_w_down'], 'm_even_w_in': out['m_even_w_in'], 'm_q_norm': out['m_q_norm'], 'm_w_uq': out['m_w_uq'], 'm_kv_norm': out['m_kv_norm'], 'm_w_ukv': out['m_w_ukv'], 'm_sg_norm': out['m_sg_norm'], 'm_sg_w': out['m_sg_w'], 'm_sg_b': out['m_sg_b'], 'm_even_w_out': out['m_even_w_out'], 'm_conv_w_in': out['m_conv_w_in'], 'm_conv_w': out['m_conv_w'], 'm_conv_w_out': out['m_conv_w_out'], 'm_final_norm': out['m_final_norm'], 'v_ffn_pre_norm': out['v_ffn_pre_norm'], 'v_ffn_pre_w_gate': out['v_ffn_pre_w_gate'], 'v_ffn_pre_w_up': out['v_ffn_pre_w_up'], 'v_ffn_pre_w_down': out['v_ffn_pre_w_down'], 'v_mix_norm': out['v_mix_norm'], 'v_ffn_post_norm': out['v_ffn_post_norm'], 'v_ffn_post_w_gate': out['v_ffn_post_w_gate'], 'v_ffn_post_w_up': out['v_ffn_post_w_up'], 'v_ffn_post_w_down': out['v_ffn_post_w_down'], 'v_even_w_in': out['v_even_w_in'], 'v_q_norm': out['v_q_norm'], 'v_w_uq': out['v_w_uq'], 'v_kv_norm': out['v_kv_norm'], 'v_w_ukv': out['v_w_ukv'], 'v_sg_norm': out['v_sg_norm'], 'v_sg_w': out['v_sg_w'], 'v_sg_b': out['v_sg_b'], 'v_even_w_out': out['v_even_w_out'], 'v_conv_w_in': out['v_conv_w_in'], 'v_conv_w': out['v_conv_w'], 'v_conv_w_out': out['v_conv_w_out'], 'v_final_norm': out['v_final_norm']}


def _loss(weights, diff, rest, loss_target):
    with _jax.named_scope("forward"):
        args = {**rest, TWIN_DIFF_INPUT: diff, **{k: w.astype(_WEIGHT_DTYPES[k]) for k, w in weights.items()}}
        y = _forward(args)
    with _jax.named_scope("loss_head"):
        err = _jnp.square(y.astype(_jnp.float32) - loss_target)
        return 0.5 * _jnp.sum(_jnp.mean(err, axis=-1)) if err.ndim else 0.5 * err


def _adamw(w, g, m, v):
    m = ADAM_B1 * m + (1.0 - ADAM_B1) * g
    v = ADAM_B2 * v + (1.0 - ADAM_B2) * _jnp.square(g)
    m_hat = m / (1.0 - ADAM_B1 ** ADAM_STEP)
    v_hat = v / (1.0 - ADAM_B2 ** ADAM_STEP)
    delta = -ADAM_LR * (m_hat / (_jnp.sqrt(v_hat) + ADAM_EPS) + ADAM_WD * w)
    return delta, m, v


def reference(x, positions, ffn_pre_norm, ffn_pre_w_gate, ffn_pre_w_up, ffn_pre_w_down, mix_norm, ffn_post_norm, ffn_post_w_gate, ffn_post_w_up, ffn_post_w_down, even_w_in, q_norm, w_uq, kv_norm, w_ukv, sg_norm, sg_w, sg_b, even_w_out, conv_w_in, conv_w, conv_w_out, final_norm, loss_target, m_ffn_pre_norm, m_ffn_pre_w_gate, m_ffn_pre_w_up, m_ffn_pre_w_down, m_mix_norm, m_ffn_post_norm, m_ffn_post_w_gate, m_ffn_post_w_up, m_ffn_post_w_down, m_even_w_in, m_q_norm, m_w_uq, m_kv_norm, m_w_ukv, m_sg_norm, m_sg_w, m_sg_b, m_even_w_out, m_conv_w_in, m_conv_w, m_conv_w_out, m_final_norm, v_ffn_pre_norm, v_ffn_pre_w_gate, v_ffn_pre_w_up, v_ffn_pre_w_down, v_mix_norm, v_ffn_post_norm, v_ffn_post_w_gate, v_ffn_post_w_up, v_ffn_post_w_down, v_even_w_in, v_q_norm, v_w_uq, v_kv_norm, v_w_ukv, v_sg_norm, v_sg_w, v_sg_b, v_even_w_out, v_conv_w_in, v_conv_w, v_conv_w_out, v_final_norm):
    given = dict(x=x, positions=positions, ffn_pre_norm=ffn_pre_norm, ffn_pre_w_gate=ffn_pre_w_gate, ffn_pre_w_up=ffn_pre_w_up, ffn_pre_w_down=ffn_pre_w_down, mix_norm=mix_norm, ffn_post_norm=ffn_post_norm, ffn_post_w_gate=ffn_post_w_gate, ffn_post_w_up=ffn_post_w_up, ffn_post_w_down=ffn_post_w_down, even_w_in=even_w_in, q_norm=q_norm, w_uq=w_uq, kv_norm=kv_norm, w_ukv=w_ukv, sg_norm=sg_norm, sg_w=sg_w, sg_b=sg_b, even_w_out=even_w_out, conv_w_in=conv_w_in, conv_w=conv_w, conv_w_out=conv_w_out, final_norm=final_norm, loss_target=loss_target, m_ffn_pre_norm=m_ffn_pre_norm, m_ffn_pre_w_gate=m_ffn_pre_w_gate, m_ffn_pre_w_up=m_ffn_pre_w_up, m_ffn_pre_w_down=m_ffn_pre_w_down, m_mix_norm=m_mix_norm, m_ffn_post_norm=m_ffn_post_norm, m_ffn_post_w_gate=m_ffn_post_w_gate, m_ffn_post_w_up=m_ffn_post_w_up, m_ffn_post_w_down=m_ffn_post_w_down, m_even_w_in=m_even_w_in, m_q_norm=m_q_norm, m_w_uq=m_w_uq, m_kv_norm=m_kv_norm, m_w_ukv=m_w_ukv, m_sg_norm=m_sg_norm, m_sg_w=m_sg_w, m_sg_b=m_sg_b, m_even_w_out=m_even_w_out, m_conv_w_in=m_conv_w_in, m_conv_w=m_conv_w, m_conv_w_out=m_conv_w_out, m_final_norm=m_final_norm, v_ffn_pre_norm=v_ffn_pre_norm, v_ffn_pre_w_gate=v_ffn_pre_w_gate, v_ffn_pre_w_up=v_ffn_pre_w_up, v_ffn_pre_w_down=v_ffn_pre_w_down, v_mix_norm=v_mix_norm, v_ffn_post_norm=v_ffn_post_norm, v_ffn_post_w_gate=v_ffn_post_w_gate, v_ffn_post_w_up=v_ffn_post_w_up, v_ffn_post_w_down=v_ffn_post_w_down, v_even_w_in=v_even_w_in, v_q_norm=v_q_norm, v_w_uq=v_w_uq, v_kv_norm=v_kv_norm, v_w_ukv=v_w_ukv, v_sg_norm=v_sg_norm, v_sg_w=v_sg_w, v_sg_b=v_sg_b, v_even_w_out=v_even_w_out, v_conv_w_in=v_conv_w_in, v_conv_w=v_conv_w, v_conv_w_out=v_conv_w_out, v_final_norm=v_final_norm)
    weights = {n: given[n] for n in TWIN_WEIGHTS}
    shared = {n: given[n] for n in SHARED_INPUTS}
    per_example = {n: given[n] for n in ['x', 'positions']}
    grad_fn = _jax.value_and_grad(_loss, argnums=(0, 1))

    def one_microbatch(ex, loss_target):
        ex = dict(ex)
        diff = ex.pop(TWIN_DIFF_INPUT)
        return grad_fn(weights, diff, {**shared, **ex}, loss_target)

    if N_MICROBATCH == 1:
        loss, (grad_w, grad_x) = one_microbatch(per_example, given["loss_target"])
    else:
        def body(carry, xs):
            loss_sum, grad_sum = carry
            l_k, (gw_k, gx_k) = one_microbatch(xs[0], xs[1])
            with _jax.named_scope("update"):
                return (loss_sum + l_k, _jax.tree.map(_jnp.add, grad_sum, gw_k)), gx_k

        init = (_jnp.zeros((), _jnp.float32), _jax.tree.map(_jnp.zeros_like, weights))
        (loss, grad_w), grad_x = _jax.lax.scan(body, init, (per_example, given["loss_target"]))
    with _jax.named_scope("update"):
        delta_w, new_m, new_v = {}, {}, {}
        for n in TWIN_WEIGHTS:
            delta_w[n], new_m[n], new_v[n] = _adamw(weights[n], grad_w[n], given["m_" + n], given["v_" + n])
    return (loss, grad_x, *[grad_w[n] for n in TWIN_WEIGHTS], *[delta_w[n] for n in TWIN_WEIGHTS],
            *[new_m[n] for n in TWIN_WEIGHTS], *[new_v[n] for n in TWIN_WEIGHTS])
```

```python
import functools

import numpy as np
import jax
import jax.numpy as jnp
from jax import lax
from jax.experimental import pallas as pl
from jax.experimental.pallas import tpu as pltpu

F32 = jnp.float32
BF16 = jnp.bfloat16
_MXU_DTYPE = jnp.bfloat16
_WIRE_DTYPE = jnp.bfloat16
_VMEM_LIMIT = 52 * 1024 * 1024
_LANES = 128
_ATT_BLOCK = 512
_ROW_TILE = 512
_SG_TILE = 1024

NORM_EPS = 1e-6
HEADS = 8
NOPE = 64
ROPE = 32
VDIM = 64
QK_DIM = NOPE + ROPE
HP = 128
Q_LORA = 384
KV_LORA = 256
SG_WIDTH = 512
SG_GROUPS = 8
SG_GDIM = 64
SG_CHUNK = 128
ROPE_THETA = 10000.0
PROJ_W = Q_LORA + KV_LORA + HP + 2 * SG_WIDTH
ADAM_LR = 0.001
ADAM_B1 = 0.9
ADAM_B2 = 0.999
ADAM_EPS = 1e-08
ADAM_WD = 0.01
ADAM_STEP = 10
MESH = pl.DeviceIdType.MESH
PACK_COLS = 1024
PACK_ROW_MULT = 256

SHARDED = ['ffn_pre_w_gate', 'ffn_pre_w_up', 'ffn_pre_w_down', 'ffn_post_w_gate', 'ffn_post_w_up',
           'ffn_post_w_down', 'even_w_in', 'w_uq', 'w_ukv', 'even_w_out', 'conv_w_in', 'conv_w', 'conv_w_out']
SHARD_AXIS = {'ffn_pre_w_gate': 2, 'ffn_pre_w_up': 2, 'ffn_pre_w_down': 1, 'ffn_post_w_gate': 2,
              'ffn_post_w_up': 2, 'ffn_post_w_down': 1, 'even_w_in': 2, 'w_uq': 2, 'w_ukv': 2,
              'even_w_out': 1, 'conv_w_in': 2, 'conv_w': 2, 'conv_w_out': 1}
REPLICATED = ['ffn_pre_norm', 'mix_norm', 'ffn_post_norm', 'q_norm', 'kv_norm', 'sg_norm', 'sg_w', 'sg_b',
              'final_norm']
WEIGHTS = ['ffn_pre_norm', 'ffn_pre_w_gate', 'ffn_pre_w_up', 'ffn_pre_w_down', 'mix_norm', 'ffn_post_norm',
           'ffn_post_w_gate', 'ffn_post_w_up', 'ffn_post_w_down', 'even_w_in', 'q_norm', 'w_uq', 'kv_norm',
           'w_ukv', 'sg_norm', 'sg_w', 'sg_b', 'even_w_out', 'conv_w_in', 'conv_w', 'conv_w_out', 'final_norm']


def _params(sem=None):
    return pltpu.CompilerParams(vmem_limit_bytes=_VMEM_LIMIT,
                                **({} if sem is None else {'dimension_semantics': sem}))


def _pick(n, pref, mult):
    best = None
    t = mult
    while t <= min(n, pref):
        if n % t == 0:
            best = t
        t += mult
    return n if best is None else best


def _mx(v):
    return v if v.dtype == _MXU_DTYPE else v.astype(_MXU_DTYPE)


def _sigmoid(a):
    return 1.0 / (1.0 + jnp.exp(-a))


def _rms_stats(x):
    rstd = lax.rsqrt(jnp.mean(x * x, axis=-1, keepdims=True) + NORM_EPS)
    return x * rstd, rstd


def _rms_bwd(x, g, dh):
    xhat, rstd = _rms_stats(x)
    gdh = g * dh
    dx = rstd * (gdh - xhat * jnp.mean(gdh * xhat, axis=-1, keepdims=True))
    return dx, dh * xhat


def _fused_matmul(name, mode, lhs, rhs, prods, n_acc, epilogue, out_dtypes, M, N, K, tm, tn, tk,
                  tile_extras=(), row_extras=(), mrow_extras=(), n_colsum=0):
    gj, gi, gk = N // tn, M // tm, K // tk
    assert gj * tn == N and gi * tm == M and gk * tk == K, (name, M, N, K, tm, tn, tk)
    dims = {'nn': (((1,), (0,)), ((), ())), 'nt': (((1,), (1,)), ((), ())), 'tn': (((0,), (0,)), ((), ()))}[mode]

    def lhs_spec(roff, coff, kb):
        kb = tk if kb is None else kb
        if mode == 'tn':
            return pl.BlockSpec((kb, tm), lambda j, i, k: (k + roff, i + coff))
        return pl.BlockSpec((tm, kb), lambda j, i, k: (i + roff, k + coff))

    def rhs_spec(roff, coff, kb):
        kb = tk if kb is None else kb
        if mode == 'nt':
            return pl.BlockSpec((tn, kb), lambda j, i, k: (j + roff, k + coff))
        return pl.BlockSpec((kb, tn), lambda j, i, k: (k + roff, j + coff))

    in_specs = [lhs_spec(*a[1:]) for a in lhs] + [rhs_spec(*a[1:]) for a in rhs]
    in_specs += [pl.BlockSpec((tm, tn), lambda j, i, k: (i, j)) for _ in tile_extras]
    in_specs += [pl.BlockSpec((1, tn), lambda j, i, k: (0, j)) for _ in row_extras]
    in_specs += [pl.BlockSpec((tm, a.shape[1]), lambda j, i, k: (i, 0)) for a in mrow_extras]
    n_out = len(out_dtypes)
    out_shape = [jax.ShapeDtypeStruct((M, N), d) for d in out_dtypes]
    out_specs = [pl.BlockSpec((tm, tn), lambda j, i, k: (i, j)) for _ in out_dtypes]
    out_shape += [jax.ShapeDtypeStruct((1, N), F32) for _ in range(n_colsum)]
    out_specs += [pl.BlockSpec((1, tn), lambda j, i, k: (0, j)) for _ in range(n_colsum)]
    scratch = [pltpu.VMEM((tm, tn), F32) for _ in range(n_acc)] if gk > 1 else []
    nl, nr, nt, nrw, nm = len(lhs), len(rhs), len(tile_extras), len(row_extras), len(mrow_extras)

    def body(*refs):
        pos = 0
        lhs_refs = refs[pos:pos + nl]; pos += nl
        rhs_refs = refs[pos:pos + nr]; pos += nr
        tile_refs = refs[pos:pos + nt]; pos += nt
        row_refs = refs[pos:pos + nrw]; pos += nrw
        mrow_refs = refs[pos:pos + nm]; pos += nm
        out_refs = refs[pos:pos + n_out]; pos += n_out
        cs_refs = refs[pos:pos + n_colsum]; pos += n_colsum
        acc_refs = refs[pos:]
        i = pl.program_id(1)
        k = pl.program_id(2)

        def partials():
            res = [None] * n_acc
            for (li, ri, ai) in prods:
                d = lax.dot_general(_mx(lhs_refs[li][...]), _mx(rhs_refs[ri][...]), dims,
                                    preferred_element_type=F32)
                res[ai] = d if res[ai] is None else res[ai] + d
            return res

        def finish(accs):
            outs = epilogue(accs, [r[...] for r in tile_refs], [r[...] for r in row_refs],
                            [r[...] for r in mrow_refs])
            for r, o in zip(out_refs, outs[:n_out]):
                r[...] = o.astype(r.dtype)
            for r, c in zip(cs_refs, outs[n_out:]):
                c = jnp.sum(c, axis=0, keepdims=True)

                @pl.when(i == 0)
                def _():
                    r[...] = c

                @pl.when(i != 0)
                def _():
                    r[...] += c

        if gk == 1:
            finish(partials())
        else:
            p = partials()

            @pl.when(k == 0)
            def _():
                for r, v in zip(acc_refs, p):
                    r[...] = v

            @pl.when(k != 0)
            def _():
                for r, v in zip(acc_refs, p):
                    r[...] += v

            @pl.when(k == gk - 1)
            def _():
                finish([r[...] for r in acc_refs])

    res = pl.pallas_call(
        body, name=name, grid=(gj, gi, gk), in_specs=in_specs, out_specs=out_specs, out_shape=out_shape,
        scratch_shapes=scratch, compiler_params=_params(("arbitrary", "arbitrary", "arbitrary")),
    )(*[a[0] for a in lhs], *[a[0] for a in rhs], *tile_extras, *row_extras, *mrow_extras)
    return res


def _op(a, roff=0, coff=0, kb=None):
    return (a, roff, coff, kb)


def _ident_epi(scale=None):
    def epi(accs, tiles, rows, mrows):
        return [a if scale is None else a * scale for a in accs]
    return epi


def _resid_norm_epi(scale):
    def epi(accs, tiles, rows, mrows):
        x_new = tiles[0] + scale * accs[0]
        xhat, _ = _rms_stats(x_new)
        return [x_new, xhat * rows[0]]
    return epi


def _norm_bwd_epi(accs, tiles, rows, mrows):
    dx_n, dg = _rms_bwd(tiles[0], rows[0], accs[0])
    dx = tiles[1] + dx_n
    return [dx, dx, dg]


def _rmsnorm_call(name, x, g):
    S, D = x.shape
    tm = _pick(S, _ROW_TILE, 8)

    def body(x_ref, g_ref, h_ref):
        xhat, _ = _rms_stats(x_ref[...])
        h_ref[...] = (xhat * g_ref[...]).astype(h_ref.dtype)

    return pl.pallas_call(
        body, name=name, grid=(S // tm,),
        in_specs=[pl.BlockSpec((tm, D), lambda i: (i, 0)), pl.BlockSpec((1, D), lambda i: (0, 0))],
        out_specs=pl.BlockSpec((tm, D), lambda i: (i, 0)),
        out_shape=jax.ShapeDtypeStruct((S, D), BF16), compiler_params=_params(("arbitrary",)),
    )(x, g)


def _loss_call(x, target, g):
    S, D = x.shape
    tm = _pick(S, _ROW_TILE, 8)

    def body(x_ref, t_ref, g_ref, dx_ref, dxb_ref, dg_ref, loss_ref):
        i = pl.program_id(0)
        x_t = x_ref[...]
        gain = g_ref[...]
        xhat, _ = _rms_stats(x_t)
        diff = xhat * gain - t_ref[...]
        dy = diff * (1.0 / D)
        dx, dg = _rms_bwd(x_t, gain, dy)
        dx_ref[...] = dx
        dxb_ref[...] = dx.astype(BF16)
        dg = jnp.sum(dg, axis=0, keepdims=True)
        part = 0.5 * jnp.sum(jnp.sum(diff * diff, axis=1, keepdims=True), axis=0, keepdims=True) * (1.0 / D)
        part = jnp.broadcast_to(part, (1, _LANES))

        @pl.when(i == 0)
        def _():
            dg_ref[...] = dg
            loss_ref[...] = part

        @pl.when(i != 0)
        def _():
            dg_ref[...] += dg
            loss_ref[...] += part

    row = lambda i: (i, 0)
    fixed = lambda i: (0, 0)
    return pl.pallas_call(
        body, name="loss_head", grid=(S // tm,),
        in_specs=[pl.BlockSpec((tm, D), row), pl.BlockSpec((tm, D), row), pl.BlockSpec((1, D), fixed)],
        out_specs=[pl.BlockSpec((tm, D), row), pl.BlockSpec((tm, D), row), pl.BlockSpec((1, D), fixed),
                   pl.BlockSpec((1, _LANES), fixed)],
        out_shape=[jax.ShapeDtypeStruct((S, D), F32), jax.ShapeDtypeStruct((S, D), BF16),
                   jax.ShapeDtypeStruct((1, D), F32), jax.ShapeDtypeStruct((1, _LANES), F32)],
        compiler_params=_params(("arbitrary",)),
    )(x, target, g)


def _rope_tables_call(pos_col, invf, mask_a, mask_b):
    S = pos_col.shape[0]
    tm = _pick(S, _ROW_TILE, 8)

    def body(p_ref, f_ref, a_ref, b_ref, cos_ref, sa_ref, sb_ref):
        ang = p_ref[...].astype(F32) * f_ref[...]
        sn = jnp.sin(ang)
        cos_ref[...] = jnp.cos(ang)
        sa_ref[...] = sn * a_ref[...]
        sb_ref[...] = sn * b_ref[...]

    row = lambda i: (i, 0)
    fixed = lambda i: (0, 0)
    return pl.pallas_call(
        body, name="rope_tables", grid=(S // tm,),
        in_specs=[pl.BlockSpec((tm, 1), row)] + [pl.BlockSpec((1, HP), fixed)] * 3,
        out_specs=[pl.BlockSpec((tm, HP), row)] * 3,
        out_shape=[jax.ShapeDtypeStruct((S, HP), F32)] * 3, compiler_params=_params(("arbitrary",)),
    )(pos_col, invf, mask_a, mask_b)


def _rope(t, cos, sa, sb):
    return t * cos + pltpu.roll(t, HP - ROPE // 2, 1) * sa + pltpu.roll(t, ROPE // 2, 1) * sb


def _rope_t(d, cos, sa, sb):
    return d * cos + pltpu.roll(d * sa, ROPE // 2, 1) + pltpu.roll(d * sb, HP - ROPE // 2, 1)


def _gelu(z):
    return 0.5 * z * (1.0 + lax.erf(z * np.float32(1.0 / np.sqrt(2.0))))


def _gelu_grad(z):
    cdf = 0.5 * (1.0 + lax.erf(z * np.float32(1.0 / np.sqrt(2.0))))
    pdf = np.float32(1.0 / np.sqrt(2.0 * np.pi)) * jnp.exp(-0.5 * z * z)
    return cdf + z * pdf


_CQ0, _CKV0, _KR0, _Z0 = 0, Q_LORA, Q_LORA + KV_LORA, Q_LORA + KV_LORA + HP


def _even_prep_call(proj, qn, kvn, sgn, cos, sa, sb):
    S = proj.shape[0]
    tm = _pick(S, 256, 8)

    def body(p_ref, qn_ref, kvn_ref, sgn_ref, cos_ref, sa_ref, sb_ref, cq_ref, ckv_ref, kr_ref, u_ref, v_ref):
        cq = p_ref[:, _CQ0:_CQ0 + Q_LORA]
        cq_ref[...] = (_rms_stats(cq)[0] * qn_ref[...]).astype(BF16)
        ckv = p_ref[:, _CKV0:_CKV0 + KV_LORA]
        ckv_ref[...] = (_rms_stats(ckv)[0] * kvn_ref[...]).astype(BF16)
        kr = p_ref[:, _KR0:_KR0 + HP]
        kr_ref[...] = _rope(kr, cos_ref[...], sa_ref[...], sb_ref[...]).astype(BF16)
        u_ref[...] = _gelu(p_ref[:, _Z0:_Z0 + SG_WIDTH]).astype(BF16)
        zv = _gelu(p_ref[:, _Z0 + SG_WIDTH:_Z0 + 2 * SG_WIDTH])
        v_ref[...] = (_rms_stats(zv)[0] * sgn_ref[...]).astype(BF16)

    row = lambda i: (i, 0)
    fixed = lambda i: (0, 0)
    widths = [Q_LORA, KV_LORA, HP, SG_WIDTH, SG_WIDTH]
    return pl.pallas_call(
        body, name="even_prep", grid=(S // tm,),
        in_specs=[pl.BlockSpec((tm, PROJ_W), row), pl.BlockSpec((1, Q_LORA), fixed),
                  pl.BlockSpec((1, KV_LORA), fixed), pl.BlockSpec((1, SG_WIDTH), fixed)]
        + [pl.BlockSpec((tm, HP), row)] * 3,
        out_specs=[pl.BlockSpec((tm, w), row) for w in widths],
        out_shape=[jax.ShapeDtypeStruct((S, w), BF16) for w in widths],
        compiler_params=_params(("arbitrary",)),
    )(proj, qn, kvn, sgn, cos, sa, sb)


def _even_prep_bwd_call(proj, qn, kvn, sgn, cos, sa, sb, dcqn, dckvn, dk, du, dvn):
    S = proj.shape[0]
    tm = _pick(S, 256, 8)

    def body(p_ref, qn_ref, kvn_ref, sgn_ref, cos_ref, sa_ref, sb_ref, dcq_ref, dckv_ref, dk_ref, du_ref,
             dvn_ref, dp_ref, dqn_ref, dkvn_ref, dsgn_ref):
        i = pl.program_id(0)
        dcq, gq = _rms_bwd(p_ref[:, _CQ0:_CQ0 + Q_LORA], qn_ref[...], dcq_ref[...])
        dp_ref[:, _CQ0:_CQ0 + Q_LORA] = dcq.astype(BF16)
        dckv, gkv = _rms_bwd(p_ref[:, _CKV0:_CKV0 + KV_LORA], kvn_ref[...], dckv_ref[...])
        dp_ref[:, _CKV0:_CKV0 + KV_LORA] = dckv.astype(BF16)
        dkr = dk_ref[:, 0:HP].astype(F32)
        for h in range(1, HEADS):
            dkr = dkr + dk_ref[:, h * HP:(h + 1) * HP].astype(F32)
        lane = lax.broadcasted_iota(jnp.int32, dkr.shape, 1)
        dkr = jnp.where((lane >= NOPE) & (lane < QK_DIM), dkr, 0.0)
        dp_ref[:, _KR0:_KR0 + HP] = _rope_t(dkr, cos_ref[...], sa_ref[...], sb_ref[...]).astype(BF16)
        zu = p_ref[:, _Z0:_Z0 + SG_WIDTH]
        dp_ref[:, _Z0:_Z0 + SG_WIDTH] = (du_ref[...].astype(F32) * _gelu_grad(zu)).astype(BF16)
        zv = p_ref[:, _Z0 + SG_WIDTH:_Z0 + 2 * SG_WIDTH]
        dgv, gsg = _rms_bwd(_gelu(zv), sgn_ref[...], dvn_ref[...].astype(F32))
        dp_ref[:, _Z0 + SG_WIDTH:_Z0 + 2 * SG_WIDTH] = (dgv * _gelu_grad(zv)).astype(BF16)
        sums = [jnp.sum(t, axis=0, keepdims=True) for t in (gq, gkv, gsg)]

        @pl.when(i == 0)
        def _():
            for r, s in zip((dqn_ref, dkvn_ref, dsgn_ref), sums):
                r[...] = s

        @pl.when(i != 0)
        def _():
            for r, s in zip((dqn_ref, dkvn_ref, dsgn_ref), sums):
                r[...] += s

    row = lambda i: (i, 0)
    fixed = lambda i: (0, 0)
    return pl.pallas_call(
        body, name="even_prep_bwd", grid=(S // tm,),
        in_specs=[pl.BlockSpec((tm, PROJ_W), row), pl.BlockSpec((1, Q_LORA), fixed),
                  pl.BlockSpec((1, KV_LORA), fixed), pl.BlockSpec((1, SG_WIDTH), fixed)]
        + [pl.BlockSpec((tm, HP), row)] * 3
        + [pl.BlockSpec((tm, Q_LORA), row), pl.BlockSpec((tm, KV_LORA), row),
           pl.BlockSpec((tm, HEADS * HP), row), pl.BlockSpec((tm, SG_WIDTH), row),
           pl.BlockSpec((tm, SG_WIDTH), row)],
        out_specs=[pl.BlockSpec((tm, PROJ_W), row), pl.BlockSpec((1, Q_LORA), fixed),
                   pl.BlockSpec((1, KV_LORA), fixed), pl.BlockSpec((1, SG_WIDTH), fixed)],
        out_shape=[jax.ShapeDtypeStruct((S, PROJ_W), BF16), jax.ShapeDtypeStruct((1, Q_LORA), F32),
                   jax.ShapeDtypeStruct((1, KV_LORA), F32), jax.ShapeDtypeStruct((1, SG_WIDTH), F32)],
        compiler_params=_params(("arbitrary",)),
    )(proj, qn, kvn, sgn, cos, sa, sb, dcqn, dckvn, dk, du, dvn)


def _causal_mask(rows, cols):
    r = lax.broadcasted_iota(jnp.int32, (rows, cols), 0)
    c = lax.broadcasted_iota(jnp.int32, (rows, cols), 1)
    return c <= r


def _flash_fwd_call(q, k, v):
    S = q.shape[0]
    tb = _pick(S, _ATT_BLOCK, 128)
    nq = S // tb
    nt_dims = (((1,), (1,)), ((), ()))

    def body(q_ref, k_ref, v_ref, o_ref, lse_ref):
        i = pl.program_id(1)
        q_t = q_ref[...]

        def step(j, carry, masked):
            m, l, acc = carry
            off = pl.multiple_of(j * tb, tb)
            k_t = k_ref[pl.ds(off, tb), :]
            v_t = v_ref[pl.ds(off, tb), :]
            s = lax.dot_general(q_t, k_t, nt_dims, preferred_element_type=F32)
            if masked:
                s = jnp.where(_causal_mask(tb, tb), s, -1e30)
            m_new = jnp.maximum(m, jnp.max(s, axis=1, keepdims=True))
            alpha = jnp.exp(m - m_new)
            p = jnp.exp(s - m_new)
            l = alpha * l + jnp.sum(p, axis=1, keepdims=True)
            acc = alpha * acc + jnp.dot(p.astype(v_t.dtype), v_t, preferred_element_type=F32)
            return m_new, l, acc

        init = (jnp.full((tb, 1), -1e30, F32), jnp.zeros((tb, 1), F32), jnp.zeros((tb, HP), F32))
        carry = lax.fori_loop(0, i, lambda j, c: step(j, c, False), init)
        m, l, acc = step(i, carry, True)
        o_ref[...] = (acc / l).astype(o_ref.dtype)
        lse = jnp.broadcast_to(m + jnp.log(l), (tb, HP))
        lse_ref[0, 0] = jnp.transpose(lse)[0:8, :]

    return pl.pallas_call(
        body, name="flash_fwd", grid=(HEADS, nq),
        in_specs=[pl.BlockSpec((tb, HP), lambda h, i: (i, h)), pl.BlockSpec((S, HP), lambda h, i: (0, h)),
                  pl.BlockSpec((S, HP), lambda h, i: (0, h))],
        out_specs=[pl.BlockSpec((tb, HP), lambda h, i: (i, h)),
                   pl.BlockSpec((1, 1, 8, tb), lambda h, i: (h, i, 0, 0))],
        out_shape=[jax.ShapeDtypeStruct((S, HEADS * HP), q.dtype), jax.ShapeDtypeStruct((HEADS, nq, 8, tb), F32)],
        compiler_params=_params(("arbitrary", "arbitrary")),
    )(q, k, v)


def _attn_delta_call(o, do):
    S = o.shape[0]
    tb = _pick(S, _ATT_BLOCK, 128)

    def body(o_ref, do_ref, d_ref):
        d = jnp.sum(o_ref[...].astype(F32) * do_ref[...].astype(F32), axis=1, keepdims=True)
        d_ref[0, 0] = jnp.transpose(jnp.broadcast_to(d, (tb, HP)))[0:8, :]

    return pl.pallas_call(
        body, name="attn_delta", grid=(HEADS, S // tb),
        in_specs=[pl.BlockSpec((tb, HP), lambda h, i: (i, h))] * 2,
        out_specs=pl.BlockSpec((1, 1, 8, tb), lambda h, i: (h, i, 0, 0)),
        out_shape=jax.ShapeDtypeStruct((HEADS, S // tb, 8, tb), F32), compiler_params=_params(("arbitrary", "arbitrary")),
    )(o, do)


def _flash_bwd_call(q, k, v, do, lse, delta):
    S = q.shape[0]
    tb = _pick(S, _ATT_BLOCK, 128)
    nq = S // tb
    nt_dims = (((1,), (1,)), ((), ()))
    tn_dims = (((0,), (0,)), ((), ()))

    def body(q_ref, do_ref, lse_ref, dl_ref, k_ref, v_ref, dq_ref, dk_ref, dv_ref):
        j = pl.program_id(1)
        k_t = k_ref[...]
        v_t = v_ref[...]

        @pl.when(j == 0)
        def _():
            dq_ref[...] = jnp.zeros_like(dq_ref)

        def step(i, carry, masked):
            dk, dv = carry
            off = pl.multiple_of(i * tb, tb)
            q_t = q_ref[pl.ds(off, tb), :]
            do_t = do_ref[pl.ds(off, tb), :]
            lse_row = lse_ref[0, i, 0:1, :]
            dl_row = dl_ref[0, i, 0:1, :]
            st = lax.dot_general(k_t, q_t, nt_dims, preferred_element_type=F32)
            pt = jnp.exp(st - lse_row)
            if masked:
                pt = jnp.where(jnp.transpose(_causal_mask(tb, tb)), pt, 0.0)
            dpt = lax.dot_general(v_t, do_t, nt_dims, preferred_element_type=F32)
            dst = (pt * (dpt - dl_row)).astype(q_t.dtype)
            dv = dv + jnp.dot(pt.astype(do_t.dtype), do_t, preferred_element_type=F32)
            dk = dk + jnp.dot(dst, q_t, preferred_element_type=F32)
            dq_ref[pl.ds(off, tb), :] += lax.dot_general(dst, k_t, tn_dims, preferred_element_type=F32)
            return dk, dv

        zero = jnp.zeros((tb, HP), F32)
        carry = step(j, (zero, zero), True)
        dk, dv = lax.fori_loop(j + 1, nq, lambda i, c: step(i, c, False), carry)
        dk_ref[...] = dk.astype(dk_ref.dtype)
        dv_ref[...] = dv.astype(dv_ref.dtype)

    head = lambda h, j: (0, h)
    blk = lambda h, j: (j, h)
    rows = lambda h, j: (h, 0, 0, 0)
    return pl.pallas_call(
        body, name="flash_bwd", grid=(HEADS, nq),
        in_specs=[pl.BlockSpec((S, HP), head), pl.BlockSpec((S, HP), head), pl.BlockSpec((1, nq, 8, tb), rows),
                  pl.BlockSpec((1, nq, 8, tb), rows), pl.BlockSpec((tb, HP), blk), pl.BlockSpec((tb, HP), blk)],
        out_specs=[pl.BlockSpec((S, HP), head), pl.BlockSpec((tb, HP), blk), pl.BlockSpec((tb, HP), blk)],
        out_shape=[jax.ShapeDtypeStruct((S, HEADS * HP), F32), jax.ShapeDtypeStruct((S, HEADS * HP), BF16),
                   jax.ShapeDtypeStruct((S, HEADS * HP), BF16)],
        compiler_params=_params(("arbitrary", "arbitrary")),
    )(q, do, lse, delta, k, v)


def _sg_mixed(w_ref, vch, lane_lo):
    blocks = []
    for jb in range(SG_WIDTH // _LANES):
        r = jnp.dot(w_ref[jb], vch[:, jb * _LANES:(jb + 1) * _LANES], preferred_element_type=F32)
        blocks.append(jnp.where(lane_lo, r[0:SG_CHUNK], r[SG_CHUNK:2 * SG_CHUNK]))
    return jnp.concatenate(blocks, axis=1)


def _sgu_fwd_call(vn, u, attn, wst, bexp):
    S = vn.shape[0]
    tm = _pick(S, _SG_TILE, SG_CHUNK)
    AW = HEADS * HP

    def body(v_ref, u_ref, a_ref, w_ref, b_ref, mix_ref):
        lane_lo = lax.broadcasted_iota(jnp.int32, (SG_CHUNK, _LANES), 1) < SG_GDIM
        mix_ref[:, 0:AW] = a_ref[...]
        for c in range(tm // SG_CHUNK):
            rs = slice(c * SG_CHUNK, (c + 1) * SG_CHUNK)
            mixed = _sg_mixed(w_ref, v_ref[rs, :], lane_lo) + b_ref[...]
            mix_ref[rs, AW:AW + SG_WIDTH] = (u_ref[rs, :].astype(F32) * mixed).astype(mix_ref.dtype)

    row = lambda i: (i, 0)
    return pl.pallas_call(
        body, name="sgu_fwd", grid=(S // tm,),
        in_specs=[pl.BlockSpec((tm, SG_WIDTH), row), pl.BlockSpec((tm, SG_WIDTH), row), pl.BlockSpec((tm, AW), row),
                  pl.BlockSpec((SG_WIDTH // _LANES, 2 * SG_CHUNK, SG_CHUNK), lambda i: (0, 0, 0)),
                  pl.BlockSpec((SG_CHUNK, SG_WIDTH), lambda i: (0, 0))],
        out_specs=pl.BlockSpec((tm, AW + SG_WIDTH), row),
        out_shape=jax.ShapeDtypeStruct((S, AW + SG_WIDTH), BF16), compiler_params=_params(("arbitrary",)),
    )(vn, u, attn, wst, bexp)


def _sgu_bwd_call(dmix, vn, u, wst, wst_t, bexp):
    S = vn.shape[0]
    tm = _pick(S, _SG_TILE, SG_CHUNK)
    nblk = SG_WIDTH // _LANES
    col0 = (HEADS * HP) // SG_WIDTH
    nt_dims = (((1,), (1,)), ((), ()))

    def body(d_ref, v_ref, u_ref, w_ref, wt_ref, b_ref, du_ref, dv_ref, dw_ref, db_ref, dbacc_ref):
        i = pl.program_id(0)
        lane_lo = lax.broadcasted_iota(jnp.int32, (SG_CHUNK, _LANES), 1) < SG_GDIM

        @pl.when(i == 0)
        def _():
            dw_ref[...] = jnp.zeros_like(dw_ref)
            dbacc_ref[...] = jnp.zeros_like(dbacc_ref)

        for c in range(tm // SG_CHUNK):
            rs = slice(c * SG_CHUNK, (c + 1) * SG_CHUNK)
            vch = v_ref[rs, :]
            dsg = d_ref[rs, :].astype(F32)
            mixed = _sg_mixed(w_ref, vch, lane_lo) + b_ref[...]
            du_ref[rs, :] = (dsg * mixed).astype(du_ref.dtype)
            dmixed = dsg * u_ref[rs, :].astype(F32)
            dbacc_ref[...] += dmixed
            dmx = dmixed.astype(vch.dtype)
            dv_ref[rs, :] = _sg_mixed(wt_ref, dmx, lane_lo).astype(dv_ref.dtype)
            for jb in range(nblk):
                dblk = dmx[:, jb * _LANES:(jb + 1) * _LANES]
                vblk = vch[:, jb * _LANES:(jb + 1) * _LANES]
                zero = jnp.zeros_like(dblk)
                dw_ref[2 * jb] += lax.dot_general(jnp.where(lane_lo, dblk, zero), vblk, nt_dims,
                                                  preferred_element_type=F32)
                dw_ref[2 * jb + 1] += lax.dot_general(jnp.where(lane_lo, zero, dblk), vblk, nt_dims,
                                                      preferred_element_type=F32)

        @pl.when(i == pl.num_programs(0) - 1)
        def _():
            tri = _causal_mask(SG_CHUNK, SG_CHUNK)
            for g in range(SG_GROUPS):
                dw_ref[g] = jnp.where(tri, dw_ref[g], 0.0)
            lane = lax.broadcasted_iota(jnp.int32, (SG_CHUNK, _LANES), 1)
            out = jnp.zeros((SG_CHUNK, _LANES), F32)
            for g in range(SG_GROUPS):
                blk = dbacc_ref[:, (g // 2) * _LANES:(g // 2 + 1) * _LANES]
                sel = lane_lo if g % 2 == 0 else jnp.logical_not(lane_lo)
                s = jnp.sum(jnp.where(sel, blk, 0.0), axis=1, keepdims=True)
                out = jnp.where(lane == g, s, out)
            db_ref[...] = out

    row = lambda i: (i, 0)
    wspec = pl.BlockSpec((nblk, 2 * SG_CHUNK, SG_CHUNK), lambda i: (0, 0, 0))
    return pl.pallas_call(
        body, name="sgu_bwd", grid=(S // tm,),
        in_specs=[pl.BlockSpec((tm, SG_WIDTH), lambda i: (i, col0)), pl.BlockSpec((tm, SG_WIDTH), row),
                  pl.BlockSpec((tm, SG_WIDTH), row), wspec, wspec,
                  pl.BlockSpec((SG_CHUNK, SG_WIDTH), lambda i: (0, 0))],
        out_specs=[pl.BlockSpec((tm, SG_WIDTH), row), pl.BlockSpec((tm, SG_WIDTH), row),
                   pl.BlockSpec((SG_GROUPS, SG_CHUNK, SG_CHUNK), lambda i: (0, 0, 0)),
                   pl.BlockSpec((SG_CHUNK, _LANES), lambda i: (0, 0))],
        out_shape=[jax.ShapeDtypeStruct((S, SG_WIDTH), BF16), jax.ShapeDtypeStruct((S, SG_WIDTH), BF16),
                   jax.ShapeDtypeStruct((SG_GROUPS, SG_CHUNK, SG_CHUNK), F32),
                   jax.ShapeDtypeStruct((SG_CHUNK, _LANES), F32)],
        scratch_shapes=[pltpu.VMEM((SG_CHUNK, SG_WIDTH), F32)],
        compiler_params=_params(("arbitrary",)),
    )(dmix, vn, u, wst, wst_t, bexp)


def _shift_down(t, halo, n):
    rows = lax.broadcasted_iota(jnp.int32, t.shape, 0)
    out = pltpu.roll(t, n, 0)
    for r in range(n):
        out = jnp.where(rows == r, halo[8 - n + r:8 - n + r + 1, :], out)
    return out


def _shift_up(t, halo, n):
    tm = t.shape[0]
    rows = lax.broadcasted_iota(jnp.int32, t.shape, 0)
    out = pltpu.roll(t, tm - n, 0)
    for r in range(n):
        out = jnp.where(rows == tm - n + r, halo[r:r + 1, :], out)
    return out


def _conv_fwd_call(p, w):
    S, C3 = p.shape
    C = C3 // 3
    tm = _pick(S, _ROW_TILE, 8)
    hb = tm // 8

    def body(p_ref, c_prev, z_prev, w_ref, m_ref):
        i = pl.program_id(0)
        cz = p_ref[:, C:2 * C] * p_ref[:, 2 * C:3 * C]
        czp = jnp.where(i > 0, c_prev[...] * z_prev[...], 0.0)
        y = w_ref[2:3, :] * cz + w_ref[1:2, :] * _shift_down(cz, czp, 1) + w_ref[0:1, :] * _shift_down(cz, czp, 2)
        m_ref[...] = (p_ref[:, 0:C] * y).astype(m_ref.dtype)

    prev = lambda col: (lambda i: (jnp.maximum(i * hb - 1, 0), col))
    return pl.pallas_call(
        body, name="conv_fwd", grid=(S // tm,),
        in_specs=[pl.BlockSpec((tm, C3), lambda i: (i, 0)), pl.BlockSpec((8, C), prev(1)),
                  pl.BlockSpec((8, C), prev(2)), pl.BlockSpec((3, C), lambda i: (0, 0))],
        out_specs=pl.BlockSpec((tm, C), lambda i: (i, 0)),
        out_shape=jax.ShapeDtypeStruct((S, C), BF16), compiler_params=_params(("arbitrary",)),
    )(p, p, p, w)


def _conv_bwd_call(p, w, dm):
    S, C3 = p.shape
    C = C3 // 3
    tm = _pick(S, 256, 8)
    hb = tm // 8
    n_tiles = S // tm

    def body(p_ref, c_prev, z_prev, b_next, dm_ref, dm_next, w_ref, dp_ref, dw_ref):
        i = pl.program_id(0)
        b = p_ref[:, 0:C]
        c = p_ref[:, C:2 * C]
        z = p_ref[:, 2 * C:3 * C]
        cz = c * z
        czp = jnp.where(i > 0, c_prev[...] * z_prev[...], 0.0)
        s1 = _shift_down(cz, czp, 1)
        s2 = _shift_down(cz, czp, 2)
        w0, w1, w2 = w_ref[0:1, :], w_ref[1:2, :], w_ref[2:3, :]
        y = w2 * cz + w1 * s1 + w0 * s2
        dm_t = dm_ref[...]
        dy = dm_t * b
        dyn = jnp.where(i < n_tiles - 1, dm_next[...] * b_next[...], 0.0)
        dcz = w2 * dy + w1 * _shift_up(dy, dyn, 1) + w0 * _shift_up(dy, dyn, 2)
        dp_ref[:, 0:C] = (dm_t * y).astype(dp_ref.dtype)
        dp_ref[:, C:2 * C] = (dcz * z).astype(dp_ref.dtype)
        dp_ref[:, 2 * C:3 * C] = (dcz * c).astype(dp_ref.dtype)
        dw = jnp.concatenate([jnp.sum(dy * s2, axis=0, keepdims=True), jnp.sum(dy * s1, axis=0, keepdims=True),
                              jnp.sum(dy * cz, axis=0, keepdims=True)], axis=0)

        @pl.when(i == 0)
        def _():
            dw_ref[...] = dw

        @pl.when(i != 0)
        def _():
            dw_ref[...] += dw

    prev = lambda col: (lambda i: (jnp.maximum(i * hb - 1, 0), col))
    nxt = lambda col: (lambda i: (jnp.minimum((i + 1) * hb, S // 8 - 1), col))
    return pl.pallas_call(
        body, name="conv_bwd", grid=(n_tiles,),
        in_specs=[pl.BlockSpec((tm, C3), lambda i: (i, 0)), pl.BlockSpec((8, C), prev(1)),
                  pl.BlockSpec((8, C), prev(2)), pl.BlockSpec((8, C), nxt(0)),
                  pl.BlockSpec((tm, C), lambda i: (i, 0)), pl.BlockSpec((8, C), nxt(0)),
                  pl.BlockSpec((3, C), lambda i: (0, 0))],
        out_specs=[pl.BlockSpec((tm, C3), lambda i: (i, 0)), pl.BlockSpec((3, C), lambda i: (0, 0))],
        out_shape=[jax.ShapeDtypeStruct((S, C3), BF16), jax.ShapeDtypeStruct((3, C), F32)],
        compiler_params=_params(("arbitrary",)),
    )(p, p, p, p, dm, dm, w)


def _my_place():
    return lax.axis_index("x"), lax.axis_index("y"), lax.axis_index("c")


def _gather_weights_call(name, shard):
    R, C = shard.shape

    def body(s_ref, o_ref, send_sems, recv_sems, local_sem):
        x, y, c = _my_place()
        mine = 2 * x + y
        local = pltpu.make_async_copy(s_ref, o_ref.at[mine], local_sem)
        local.start()
        peers = [(1 - x, y), (x, 1 - y), (1 - x, 1 - y)]
        copies = []
        for k, (px, py) in enumerate(peers):
            cp = pltpu.make_async_remote_copy(src_ref=s_ref, dst_ref=o_ref.at[mine], send_sem=send_sems.at[k],
                                              recv_sem=recv_sems.at[k], device_id=(px, py, c), device_id_type=MESH)
            cp.start()
            copies.append(cp)
        for k, (px, py) in enumerate(peers):
            pltpu.make_async_remote_copy(src_ref=s_ref, dst_ref=o_ref.at[2 * px + py], send_sem=send_sems.at[k],
                                         recv_sem=recv_sems.at[k], device_id=(px, py, c),
                                         device_id_type=MESH).wait_recv()
        for cp in copies:
            cp.wait_send()
        local.wait()

    any_spec = pl.BlockSpec(memory_space=pl.ANY)
    return pl.pallas_call(
        body, name=name, in_specs=[any_spec], out_specs=any_spec,
        out_shape=jax.ShapeDtypeStruct((4, R, C), shard.dtype),
        scratch_shapes=[pltpu.SemaphoreType.DMA((3,)), pltpu.SemaphoreType.DMA((3,)), pltpu.SemaphoreType.DMA],
        compiler_params=pltpu.CompilerParams(has_side_effects=True),
    )(shard)


def _scatter_grads_call(packed):
    n, R, C = packed.shape

    def body(p_ref, o_ref, send_sems, recv_sems, local_sem):
        x, y, c = _my_place()
        me = 4 * x + 2 * y + c
        local = pltpu.make_async_copy(p_ref.at[me], o_ref.at[me], local_sem)
        local.start()
        peers = []
        for k in range(1, 8):
            px = x ^ (k >> 2) if (k >> 2) else x
            py = y ^ ((k >> 1) & 1) if ((k >> 1) & 1) else y
            pc = c ^ (k & 1) if (k & 1) else c
            peers.append((px, py, pc))
        copies = []
        for k, (px, py, pc) in enumerate(peers):
            cp = pltpu.make_async_remote_copy(src_ref=p_ref.at[4 * px + 2 * py + pc], dst_ref=o_ref.at[me],
                                              send_sem=send_sems.at[k], recv_sem=recv_sems.at[k],
                                              device_id=(px, py, pc), device_id_type=MESH)
            cp.start()
            copies.append(cp)
        for k, (px, py, pc) in enumerate(peers):
            pltpu.make_async_remote_copy(src_ref=p_ref.at[me], dst_ref=o_ref.at[4 * px + 2 * py + pc],
                                         send_sem=send_sems.at[k], recv_sem=recv_sems.at[k],
                                         device_id=(px, py, pc), device_id_type=MESH).wait_recv()
        for cp in copies:
            cp.wait_send()
        local.wait()

    any_spec = pl.BlockSpec(memory_space=pl.ANY)
    return pl.pallas_call(
        body, name="scatter_grads", in_specs=[any_spec], out_specs=any_spec,
        out_shape=jax.ShapeDtypeStruct((n, R, C), packed.dtype),
        scratch_shapes=[pltpu.SemaphoreType.DMA((7,)), pltpu.SemaphoreType.DMA((7,)), pltpu.SemaphoreType.DMA],
        compiler_params=pltpu.CompilerParams(has_side_effects=True),
    )(packed)


def _sum_slots_call(name, parts):
    n, R, C = parts.shape
    tr = _pick(R, 256, 8)

    def body(p_ref, o_ref):
        acc = p_ref[0].astype(F32)
        for s in range(1, n):
            acc = acc + p_ref[s].astype(F32)
        o_ref[...] = acc

    return pl.pallas_call(
        body, name=name, grid=(R // tr,),
        in_specs=[pl.BlockSpec((n, tr, C), lambda i: (0, i, 0))], out_specs=pl.BlockSpec((tr, C), lambda i: (i, 0)),
        out_shape=jax.ShapeDtypeStruct((R, C), F32), compiler_params=_params(("arbitrary",)),
    )(parts)


def _sibling_share_call(half):
    R, C = half.shape

    def body(h_ref, o_ref, send_sem, recv_sem, local_sem):
        x, y, c = _my_place()
        local = pltpu.make_async_copy(h_ref, o_ref.at[c], local_sem)
        local.start()
        cp = pltpu.make_async_remote_copy(src_ref=h_ref, dst_ref=o_ref.at[c], send_sem=send_sem, recv_sem=recv_sem,
                                          device_id=(x, y, 1 - c), device_id_type=MESH)
        cp.start()
        pltpu.make_async_remote_copy(src_ref=h_ref, dst_ref=o_ref.at[1 - c], send_sem=send_sem, recv_sem=recv_sem,
                                     device_id=(x, y, 1 - c), device_id_type=MESH).wait_recv()
        cp.wait_send()
        local.wait()

    any_spec = pl.BlockSpec(memory_space=pl.ANY)
    return pl.pallas_call(
        body, name="sibling_share", in_specs=[any_spec], out_specs=any_spec,
        out_shape=jax.ShapeDtypeStruct((2, R, C), half.dtype),
        scratch_shapes=[pltpu.SemaphoreType.DMA, pltpu.SemaphoreType.DMA, pltpu.SemaphoreType.DMA],
        compiler_params=pltpu.CompilerParams(has_side_effects=True),
    )(half)


def _allreduce_small_call(part):
    R, C = part.shape

    def body(p_ref, o_ref, slots, send_sems, recv_sems):
        x, y, c = _my_place()
        me = 4 * x + 2 * y + c
        peers = []
        for k in range(1, 8):
            px = x ^ (k >> 2) if (k >> 2) else x
            py = y ^ ((k >> 1) & 1) if ((k >> 1) & 1) else y
            pc = c ^ (k & 1) if (k & 1) else c
            peers.append((px, py, pc))
        copies = []
        for k, (px, py, pc) in enumerate(peers):
            cp = pltpu.make_async_remote_copy(src_ref=p_ref, dst_ref=slots.at[me], send_sem=send_sems.at[k],
                                              recv_sem=recv_sems.at[k], device_id=(px, py, pc), device_id_type=MESH)
            cp.start()
            copies.append(cp)
        slots[me] = p_ref[...]
        for k, (px, py, pc) in enumerate(peers):
            pltpu.make_async_remote_copy(src_ref=p_ref, dst_ref=slots.at[4 * px + 2 * py + pc],
                                         send_sem=send_sems.at[k], recv_sem=recv_sems.at[k],
                                         device_id=(px, py, pc), device_id_type=MESH).wait_recv()
        for cp in copies:
            cp.wait_send()
        acc = slots[0]
        for s in range(1, 8):
            acc = acc + slots[s]
        o_ref[...] = acc

    vm = pl.BlockSpec(memory_space=pltpu.VMEM)
    return pl.pallas_call(
        body, name="allreduce_small", in_specs=[vm], out_specs=vm,
        out_shape=jax.ShapeDtypeStruct((R, C), F32),
        scratch_shapes=[pltpu.VMEM((8, R, C), F32), pltpu.SemaphoreType.DMA((7,)), pltpu.SemaphoreType.DMA((7,))],
        compiler_params=pltpu.CompilerParams(has_side_effects=True, vmem_limit_bytes=_VMEM_LIMIT),
    )(part)


def _adamw_call(name, w, g, m, v):
    shape = w.shape
    cols = shape[-1]
    rows = int(np.prod(shape[:-1])) if len(shape) > 1 else 1
    w2, g2, m2, v2 = (t.reshape(rows, cols) for t in (w, g, m, v))
    tr = _pick(rows, 256, 8)
    c1 = 1.0 / (1.0 - ADAM_B1 ** ADAM_STEP)
    c2 = 1.0 / (1.0 - ADAM_B2 ** ADAM_STEP)

    def body(w_ref, g_ref, m_ref, v_ref, d_ref, nm_ref, nv_ref):
        gr = g_ref[...]
        m_new = ADAM_B1 * m_ref[...] + (1.0 - ADAM_B1) * gr
        v_new = ADAM_B2 * v_ref[...] + (1.0 - ADAM_B2) * (gr * gr)
        m_hat = m_new / (1.0 - ADAM_B1 ** ADAM_STEP)
        v_hat = v_new / (1.0 - ADAM_B2 ** ADAM_STEP)
        d_ref[...] = -ADAM_LR * (m_hat / (jnp.sqrt(v_hat) + ADAM_EPS) + ADAM_WD * w_ref[...])
        nm_ref[...] = m_new
        nv_ref[...] = v_new

    spec = pl.BlockSpec((tr, cols), lambda i: (i, 0))
    d, nm, nv = pl.pallas_call(
        body, name=name, grid=(rows // tr,), in_specs=[spec] * 4, out_specs=[spec] * 3,
        out_shape=[jax.ShapeDtypeStruct((rows, cols), F32)] * 3, compiler_params=_params(("arbitrary",)),
    )(w2, g2, m2, v2)
    return d.reshape(shape), nm.reshape(shape), nv.reshape(shape)


def _pad_rows(flat, mult):
    n = flat.shape[0]
    unit = PACK_COLS * mult
    total = -(-n // unit) * unit
    return jnp.pad(flat, (0, total - n)).reshape(total // PACK_COLS, PACK_COLS)


def _shard_slice(arr, axis, blk, nblk=4):
    w = arr.shape[axis] // nblk
    return lax.slice_in_dim(arr, blk * w, (blk + 1) * w, axis=axis)


def _half(arr, c):
    n = arr.shape[0] // 2
    return arr[c * n:(c + 1) * n]


def _ffn_fwd(tag, x, h, wg, wu, wd, g_next):
    S, D = x.shape
    FF = wg.shape[1]
    tm = _pick(S, 512, 8)
    tn = _pick(FF, 704, 128)

    def epi(accs, tiles, rows, mrows):
        a, u = accs
        return [a, u, a * _sigmoid(a) * u]

    a, u, s = _fused_matmul(tag + "_gate_up", 'nn', [_op(h)], [_op(wg), _op(wu)], [(0, 0, 0), (0, 1, 1)], 2, epi,
                            [BF16, BF16, BF16], S, FF, D, tm, tn, D)
    x_new, h_next = _fused_matmul(tag + "_down", 'nn', [_op(s)], [_op(wd)], [(0, 0, 0)], 1, _resid_norm_epi(0.5),
                                  [F32, BF16], S, D, FF, tm, D, FF, tile_extras=[x], row_extras=[g_next])
    return x_new, h_next, (x, h, a, u, s)


def _ffn_bwd(tag, saved, dx_out, dxb, wg, wu, wd, gain):
    x, h, a, u, s = saved
    S, D = x.shape
    FF = wg.shape[1]
    tm = _pick(S, 512, 8)
    tn = _pick(FF, 704, 128)
    tk = _pick(S, 512, 128)

    def epi(accs, tiles, rows, mrows):
        ds = 0.5 * accs[0]
        a_t = tiles[0].astype(F32)
        u_t = tiles[1].astype(F32)
        sig = _sigmoid(a_t)
        return [ds * u_t * (sig * (1.0 + a_t * (1.0 - sig))), ds * (a_t * sig)]

    da, du = _fused_matmul(tag + "_dgate_up", 'nt', [_op(dxb)], [_op(wd)], [(0, 0, 0)], 1, epi, [BF16, BF16],
                           S, FF, D, tm, tn, D, tile_extras=[a, u])
    dwd, = _fused_matmul(tag + "_dw_down", 'tn', [_op(s)], [_op(dxb)], [(0, 0, 0)], 1, _ident_epi(0.5), [F32],
                         FF, D, S, _pick(FF, 1408, 128), D, tk)
    dwg, dwu = _fused_matmul(tag + "_dw_gate_up", 'tn', [_op(h)], [_op(da), _op(du)], [(0, 0, 0), (0, 1, 1)], 2,
                             _ident_epi(), [F32, F32], D, FF, S, D, tn, tk)
    dx, dxb_new, dgain = _fused_matmul(tag + "_dx", 'nt', [_op(da), _op(du)], [_op(wg), _op(wu)],
                                       [(0, 0, 0), (1, 1, 0)], 1, _norm_bwd_epi, [F32, BF16], S, D, FF, tm, D, tn,
                                       tile_extras=[x, dx_out], row_extras=[gain], n_colsum=1)
    return dx, dxb_new, dgain, dwg, dwu, dwd


def _conv_mixer_fwd(tag, x, h, w_in, w_taps, w_out, g_next):
    S, D = x.shape
    C3 = w_in.shape[1]
    tm = _pick(S, 512, 8)
    p, = _fused_matmul(tag + "_in", 'nn', [_op(h)], [_op(w_in)], [(0, 0, 0)], 1, _ident_epi(), [F32],
                       S, C3, D, tm, _pick(C3, 1024, 128), D)
    m = _conv_fwd_call(p, w_taps)
    x_new, h_next = _fused_matmul(tag + "_out", 'nn', [_op(m)], [_op(w_out)], [(0, 0, 0)], 1, _resid_norm_epi(1.0),
                                  [F32, BF16], S, D, D, tm, D, D, tile_extras=[x], row_extras=[g_next])
    return x_new, h_next, (x, h, p, m)


def _conv_mixer_bwd(tag, saved, dx_out, dxb, w_in, w_taps, w_out, gain):
    x, h, p, m = saved
    S, D = x.shape
    C3 = w_in.shape[1]
    tm = _pick(S, 512, 8)
    tk = _pick(S, 512, 128)
    dm, = _fused_matmul(tag + "_dm", 'nt', [_op(dxb)], [_op(w_out)], [(0, 0, 0)], 1, _ident_epi(), [F32],
                        S, D, D, tm, D, D)
    dw_out, = _fused_matmul(tag + "_dw_out", 'tn', [_op(m)], [_op(dxb)], [(0, 0, 0)], 1, _ident_epi(), [F32],
                            D, D, S, D, D, tk)
    dp, dtaps = _conv_bwd_call(p, w_taps, dm)
    dw_in, = _fused_matmul(tag + "_dw_in", 'tn', [_op(h)], [_op(dp)], [(0, 0, 0)], 1, _ident_epi(), [F32],
                           D, C3, S, D, _pick(C3, 1024, 128), tk)
    dx, dxb_new, dgain = _fused_matmul(tag + "_dx", 'nt', [_op(dp)], [_op(w_in)], [(0, 0, 0)], 1, _norm_bwd_epi,
                                       [F32, BF16], S, D, C3, tm, D, _pick(C3, 1024, 128),
                                       tile_extras=[x, dx_out], row_extras=[gain], n_colsum=1)
    return dx, dxb_new, dgain, dw_in, dtaps, dw_out


def _attn_scale():
    return np.float32(QK_DIM ** -0.5)


def _even_mixer_fwd(tag, x, h, wts, tables, g_next):
    S, D = x.shape
    cos, sa, sb = tables
    tm = _pick(S, 512, 8)
    AW = HEADS * HP
    proj, = _fused_matmul(tag + "_in", 'nn', [_op(h)], [_op(wts['w_in'])], [(0, 0, 0)], 1, _ident_epi(), [F32],
                          S, PROJ_W, D, tm, _pick(PROJ_W, 896, 128), D)
    cqn, ckvn, kr, u, vn = _even_prep_call(proj, wts['q_norm'], wts['kv_norm'], wts['sg_norm'], cos, sa, sb)
    scale = _attn_scale()

    def q_epi(accs, tiles, rows, mrows):
        c_t, a_t, b_t = mrows
        heads = [_rope(accs[0][:, hh * HP:(hh + 1) * HP], c_t, a_t, b_t) * scale for hh in range(HEADS)]
        return [jnp.concatenate(heads, axis=1)]

    q, = _fused_matmul(tag + "_q", 'nn', [_op(cqn)], [_op(wts['w_q'])], [(0, 0, 0)], 1, q_epi, [BF16],
                       S, AW, Q_LORA, tm, AW, Q_LORA, mrow_extras=[cos, sa, sb])

    def kv_epi(accs, tiles, rows, mrows):
        return [accs[0] + jnp.concatenate([mrows[0].astype(F32)] * HEADS, axis=1), accs[1]]

    k, v = _fused_matmul(tag + "_kv", 'nn', [_op(ckvn)], [_op(wts['w_k']), _op(wts['w_v'])],
                         [(0, 0, 0), (0, 1, 1)], 2, kv_epi, [BF16, BF16], S, AW, KV_LORA, tm, AW, KV_LORA,
                         mrow_extras=[kr])
    o, lse = _flash_fwd_call(q, k, v)
    mix = _sgu_fwd_call(vn, u, o, wts['sg_wst'], wts['sg_bexp'])
    x_new, h_next = _fused_matmul(tag + "_out", 'nn', [_op(mix)], [_op(wts['w_out'])], [(0, 0, 0)], 1,
                                  _resid_norm_epi(1.0), [F32, BF16], S, D, AW + SG_WIDTH, tm, D, AW + SG_WIDTH,
                                  tile_extras=[x], row_extras=[g_next])
    return x_new, h_next, (x, h, proj, cqn, ckvn, u, vn, q, k, v, o, lse, mix)


def _even_mixer_bwd(tag, saved, dx_out, dxb, wts, tables, gain):
    x, h, proj, cqn, ckvn, u, vn, q, k, v, o, lse, mix = saved
    S, D = x.shape
    cos, sa, sb = tables
    tm = _pick(S, 512, 8)
    tk = _pick(S, 512, 128)
    AW = HEADS * HP
    MW = AW + SG_WIDTH
    dmix, = _fused_matmul(tag + "_dmix", 'nt', [_op(dxb)], [_op(wts['w_out'])], [(0, 0, 0)], 1, _ident_epi(), [BF16],
                          S, MW, D, tm, _pick(MW, 768, 128), D)
    dw_out, = _fused_matmul(tag + "_dw_out", 'tn', [_op(mix)], [_op(dxb)], [(0, 0, 0)], 1, _ident_epi(), [F32],
                            MW, D, S, _pick(MW, 768, 128), D, tk)
    du, dvn, dsg_w, dsg_b = _sgu_bwd_call(dmix, vn, u, wts['sg_wst'], wts['sg_wst_t'], wts['sg_bexp'])
    delta = _attn_delta_call(o, dmix)
    dq, dk, dv = _flash_bwd_call(q, k, v, dmix, lse, delta)
    scale = _attn_scale()

    def dq_epi(accs, tiles, rows, mrows):
        return accs

    def dq_pre_call():
        tr = _pick(S, 256, 8)

        def body(d_ref, c_ref, a_ref, b_ref, o_ref):
            for hh in range(HEADS):
                t = _rope_t(d_ref[:, hh * HP:(hh + 1) * HP], c_ref[...], a_ref[...], b_ref[...]) * scale
                o_ref[:, hh * HP:(hh + 1) * HP] = t.astype(o_ref.dtype)

        row = lambda i: (i, 0)
        return pl.pallas_call(
            body, name=tag + "_dq_unrope", grid=(S // tr,),
            in_specs=[pl.BlockSpec((tr, AW), row)] + [pl.BlockSpec((tr, HP), row)] * 3,
            out_specs=pl.BlockSpec((tr, AW), row), out_shape=jax.ShapeDtypeStruct((S, AW), BF16),
            compiler_params=_params(("arbitrary",)),
        )(dq, cos, sa, sb)

    dqp = dq_pre_call()
    dw_q, = _fused_matmul(tag + "_dw_q", 'tn', [_op(cqn)], [_op(dqp)], [(0, 0, 0)], 1, _ident_epi(), [F32],
                          Q_LORA, AW, S, Q_LORA, AW, tk)
    dcqn, = _fused_matmul(tag + "_dcq", 'nt', [_op(dqp)], [_op(wts['w_q'])], [(0, 0, 0)], 1, dq_epi, [F32],
                          S, Q_LORA, AW, tm, Q_LORA, AW)
    dw_k, dw_v = _fused_matmul(tag + "_dw_kv", 'tn', [_op(ckvn)], [_op(dk), _op(dv)], [(0, 0, 0), (0, 1, 1)], 2,
                               _ident_epi(), [F32, F32], KV_LORA, AW, S, KV_LORA, AW, tk)
    dckvn, = _fused_matmul(tag + "_dckv", 'nt', [_op(dk), _op(dv)], [_op(wts['w_k']), _op(wts['w_v'])],
                           [(0, 0, 0), (1, 1, 0)], 1, dq_epi, [F32], S, KV_LORA, AW, tm, KV_LORA, AW)
    dproj, dqn, dkvn, dsgn = _even_prep_bwd_call(proj, wts['q_norm'], wts['kv_norm'], wts['sg_norm'], cos, sa, sb,
                                                 dcqn, dckvn, dk, du, dvn)
    dw_in, = _fused_matmul(tag + "_dw_in", 'tn', [_op(h)], [_op(dproj)], [(0, 0, 0)], 1, _ident_epi(), [F32],
                           D, PROJ_W, S, D, _pick(PROJ_W, 896, 128), tk)
    dx, dxb_new, dgain = _fused_matmul(tag + "_dx", 'nt', [_op(dproj)], [_op(wts['w_in'])], [(0, 0, 0)], 1,
                                       _norm_bwd_epi, [F32, BF16], S, D, PROJ_W, tm, D, _pick(PROJ_W, 896, 128),
                                       tile_extras=[x, dx_out], row_extras=[gain], n_colsum=1)
    grads = dict(w_in=dw_in, w_q=dw_q, w_k=dw_k, w_v=dw_v, w_out=dw_out, q_norm=dqn, kv_norm=dkvn, sg_norm=dsgn,
                 sg_w=dsg_w, sg_b=dsg_b)
    return dx, dxb_new, dgain, grads


def _even_weights(w_in, w_uq, w_ukv, w_out, q_norm, kv_norm, sg_norm, sg_w, sg_b):
    D = w_in.shape[0]
    dt = w_in.dtype
    kr_cols = jnp.zeros((D, HP), dt).at[:, NOPE:QK_DIM].set(w_in[:, Q_LORA + KV_LORA:Q_LORA + KV_LORA + ROPE])
    w_in_p = jnp.concatenate([w_in[:, :Q_LORA + KV_LORA], kr_cols, w_in[:, Q_LORA + KV_LORA + ROPE:]], axis=1)
    wq = w_uq.reshape(Q_LORA, HEADS, QK_DIM)
    w_q = jnp.pad(wq, ((0, 0), (0, 0), (0, HP - QK_DIM))).reshape(Q_LORA, HEADS * HP)
    wkv = w_ukv.reshape(KV_LORA, HEADS, NOPE + VDIM)
    w_k = jnp.pad(wkv[:, :, :NOPE], ((0, 0), (0, 0), (0, HP - NOPE))).reshape(KV_LORA, HEADS * HP)
    w_v = jnp.pad(wkv[:, :, NOPE:], ((0, 0), (0, 0), (0, HP - VDIM))).reshape(KV_LORA, HEADS * HP)
    wo_a = w_out[:HEADS * VDIM].reshape(HEADS, VDIM, D)
    wo_a = jnp.pad(wo_a, ((0, 0), (0, HP - VDIM), (0, 0))).reshape(HEADS * HP, D)
    w_out_p = jnp.concatenate([wo_a, w_out[HEADS * VDIM:]], axis=0)
    tri = jnp.tril(jnp.ones((SG_CHUNK, SG_CHUNK), F32))
    wm = sg_w * tri
    wst = wm.reshape(SG_GROUPS // 2, 2 * SG_CHUNK, SG_CHUNK).astype(_MXU_DTYPE)
    wst_t = jnp.swapaxes(wm, 1, 2).reshape(SG_GROUPS // 2, 2 * SG_CHUNK, SG_CHUNK).astype(_MXU_DTYPE)
    bexp = jnp.repeat(sg_b.T, SG_GDIM, axis=1)
    return dict(w_in=w_in_p, w_q=w_q, w_k=w_k, w_v=w_v, w_out=w_out_p, sg_wst=wst, sg_wst_t=wst_t, sg_bexp=bexp,
                q_norm=q_norm.reshape(1, -1), kv_norm=kv_norm.reshape(1, -1), sg_norm=sg_norm.reshape(1, -1))


def _even_grads_unpad(g):
    d_in = g['w_in']
    kr0 = Q_LORA + KV_LORA
    dw_in = jnp.concatenate([d_in[:, :kr0], d_in[:, kr0 + NOPE:kr0 + QK_DIM], d_in[:, kr0 + HP:]], axis=1)
    dw_uq = g['w_q'].reshape(Q_LORA, HEADS, HP)[:, :, :QK_DIM].reshape(Q_LORA, HEADS * QK_DIM)
    dk = g['w_k'].reshape(KV_LORA, HEADS, HP)[:, :, :NOPE]
    dv = g['w_v'].reshape(KV_LORA, HEADS, HP)[:, :, :VDIM]
    dw_ukv = jnp.concatenate([dk, dv], axis=2).reshape(KV_LORA, HEADS * (NOPE + VDIM))
    D = d_in.shape[0]
    wo = g['w_out']
    wo_a = wo[:HEADS * HP].reshape(HEADS, HP, D)[:, :VDIM].reshape(HEADS * VDIM, D)
    dw_out = jnp.concatenate([wo_a, wo[HEADS * HP:]], axis=0)
    dsg_b = g['sg_b'][:, :SG_GROUPS].T
    return dict(even_w_in=dw_in, w_uq=dw_uq, w_ukv=dw_ukv, even_w_out=dw_out, q_norm=g['q_norm'][0],
                kv_norm=g['kv_norm'][0], sg_norm=g['sg_norm'][0], sg_w=g['sg_w'], sg_b=dsg_b)


def kernel(x, positions, ffn_pre_norm, ffn_pre_w_gate, ffn_pre_w_up, ffn_pre_w_down, mix_norm, ffn_post_norm, ffn_post_w_gate, ffn_post_w_up, ffn_post_w_down, even_w_in, q_norm, w_uq, kv_norm, w_ukv, sg_norm, sg_w, sg_b, even_w_out, conv_w_in, conv_w, conv_w_out, final_norm, loss_target, m_ffn_pre_norm, m_ffn_pre_w_gate, m_ffn_pre_w_up, m_ffn_pre_w_down, m_mix_norm, m_ffn_post_norm, m_ffn_post_w_gate, m_ffn_post_w_up, m_ffn_post_w_down, m_even_w_in, m_q_norm, m_w_uq, m_kv_norm, m_w_ukv, m_sg_norm, m_sg_w, m_sg_b, m_even_w_out, m_conv_w_in, m_conv_w, m_conv_w_out, m_final_norm, v_ffn_pre_norm, v_ffn_pre_w_gate, v_ffn_pre_w_up, v_ffn_pre_w_down, v_mix_norm, v_ffn_post_norm, v_ffn_post_w_gate, v_ffn_post_w_up, v_ffn_post_w_down, v_even_w_in, v_q_norm, v_w_uq, v_kv_norm, v_w_ukv, v_sg_norm, v_sg_w, v_sg_b, v_even_w_out, v_conv_w_in, v_conv_w, v_conv_w_out, v_final_norm):
    env = dict(locals())
    w_loc = {n: env[n] for n in WEIGHTS}
    m_loc = {n: env['m_' + n] for n in WEIGHTS}
    v_loc = {n: env['v_' + n] for n in WEIGHTS}
    S, D = x.shape[1], x.shape[2]
    depth = ffn_pre_norm.shape[0]
    xs = x.reshape(S, D)
    target = loss_target.reshape(S, D)

    flat = jnp.concatenate([w_loc[n].astype(_WIRE_DTYPE).reshape(-1) for n in SHARDED])
    gathered = _gather_weights_call("gather_weights", _pad_rows(flat, 16))
    taps = _gather_weights_call("gather_taps", _pad_rows(conv_w.reshape(-1), 8)).reshape(4, -1)
    taps = jnp.concatenate([taps[b, :conv_w.size].reshape(conv_w.shape) for b in range(4)], axis=2)
    gflat = gathered.reshape(4, -1)
    full = {}
    off = 0
    for n in SHARDED:
        shp = w_loc[n].shape
        size = int(np.prod(shp))
        blocks = [gflat[b, off:off + size].reshape(shp) for b in range(4)]
        full[n] = jnp.concatenate(blocks, axis=SHARD_AXIS[n])
        off += size

    inv_freq = ROPE_THETA ** (-jnp.arange(0, ROPE, 2, dtype=F32) / ROPE)
    half = ROPE // 2
    invf = jnp.zeros((1, HP), F32).at[0, NOPE:NOPE + half].set(inv_freq).at[0, NOPE + half:QK_DIM].set(inv_freq)
    mask_a = jnp.zeros((1, HP), F32).at[0, NOPE:NOPE + half].set(-1.0)
    mask_b = jnp.zeros((1, HP), F32).at[0, NOPE + half:QK_DIM].set(1.0)
    tables = _rope_tables_call(positions.reshape(S, 1), invf, mask_a, mask_b)

    even_w = []
    for e in range((depth + 1) // 2):
        even_w.append(_even_weights(full['even_w_in'][e], full['w_uq'][e], full['w_ukv'][e], full['even_w_out'][e],
                                    q_norm[e], kv_norm[e], sg_norm[e], sg_w[e], sg_b[e]))

    def gain_row(arr, l):
        return arr[l].reshape(1, D)

    saved = []
    h = _rmsnorm_call("first_norm", xs, gain_row(ffn_pre_norm, 0))
    xc = xs
    for l in range(depth):
        xc, h, s_pre = _ffn_fwd(f"l{l}_pre", xc, h, full['ffn_pre_w_gate'][l], full['ffn_pre_w_up'][l],
                                full['ffn_pre_w_down'][l], gain_row(mix_norm, l))
        if l % 2 == 0:
            xc, h, s_mix = _even_mixer_fwd(f"l{l}_mix", xc, h, even_w[l // 2], tables, gain_row(ffn_post_norm, l))
        else:
            o = l // 2
            xc, h, s_mix = _conv_mixer_fwd(f"l{l}_mix", xc, h, full['conv_w_in'][o], taps[o],
                                           full['conv_w_out'][o], gain_row(ffn_post_norm, l))
        g_next = gain_row(ffn_pre_norm, l + 1) if l + 1 < depth else final_norm.reshape(1, D)
        xc, h, s_post = _ffn_fwd(f"l{l}_post", xc, h, full['ffn_post_w_gate'][l], full['ffn_post_w_up'][l],
                                 full['ffn_post_w_down'][l], g_next)
        saved.append((s_pre, s_mix, s_post))

    dx, dxb, d_final, loss_part = _loss_call(xc, target, final_norm.reshape(1, D))
    loss = lax.psum(loss_part[0, 0], ("x", "y", "c"))

    gl = {n: [None] * w_loc[n].shape[0] for n in WEIGHTS if n != 'final_norm'}
    for l in reversed(range(depth)):
        s_pre, s_mix, s_post = saved[l]
        dx, dxb, dgain, dwg, dwu, dwd = _ffn_bwd(f"l{l}_post", s_post, dx, dxb, full['ffn_post_w_gate'][l],
                                                 full['ffn_post_w_up'][l], full['ffn_post_w_down'][l],
                                                 gain_row(ffn_post_norm, l))
        gl['ffn_post_norm'][l] = dgain[0]
        gl['ffn_post_w_gate'][l], gl['ffn_post_w_up'][l], gl['ffn_post_w_down'][l] = dwg, dwu, dwd
        if l % 2 == 0:
            e = l // 2
            dx, dxb, dgain, eg = _even_mixer_bwd(f"l{l}_mix", s_mix, dx, dxb, even_w[e], tables, gain_row(mix_norm, l))
            for n, val in _even_grads_unpad(eg).items():
                gl[n][e] = val
        else:
            o = l // 2
            dx, dxb, dgain, dw_in, dtaps, dw_out = _conv_mixer_bwd(f"l{l}_mix", s_mix, dx, dxb, full['conv_w_in'][o],
                                                                   taps[o], full['conv_w_out'][o],
                                                                   gain_row(mix_norm, l))
            gl['conv_w_in'][o], gl['conv_w'][o], gl['conv_w_out'][o] = dw_in, dtaps, dw_out
        gl['mix_norm'][l] = dgain[0]
        dx, dxb, dgain, dwg, dwu, dwd = _ffn_bwd(f"l{l}_pre", s_pre, dx, dxb, full['ffn_pre_w_gate'][l],
                                                 full['ffn_pre_w_up'][l], full['ffn_pre_w_down'][l],
                                                 gain_row(ffn_pre_norm, l))
        gl['ffn_pre_norm'][l] = dgain[0]
        gl['ffn_pre_w_gate'][l], gl['ffn_pre_w_up'][l], gl['ffn_pre_w_down'][l] = dwg, dwu, dwd
    grad_x = dx.reshape(x.shape)
    part = {n: jnp.stack(gl[n]) for n in gl}
    part['final_norm'] = d_final[0]

    segs = []
    for b in range(4):
        for cc in range(2):
            segs.append(_pad_rows(jnp.concatenate(
                [_half(_shard_slice(part[n], SHARD_AXIS[n], b), cc).astype(_WIRE_DTYPE).reshape(-1) for n in SHARDED]),
                PACK_ROW_MULT))
    received = _scatter_grads_call(jnp.stack(segs))
    mine = _sum_slots_call("sum_grad_slots", received)
    both = _sibling_share_call(mine).reshape(2, -1)
    grads = {}
    off = 0
    for n in SHARDED:
        shp = w_loc[n].shape
        hshp = (shp[0] // 2,) + tuple(shp[1:])
        size = int(np.prod(hshp))
        grads[n] = jnp.concatenate([both[cc, off:off + size].reshape(hshp) for cc in range(2)], axis=0)
        off += size

    small = _pad_rows(jnp.concatenate([part[n].reshape(-1) for n in REPLICATED]), 8)
    small_sum = _allreduce_small_call(small).reshape(-1)
    off = 0
    for n in REPLICATED:
        size = int(np.prod(w_loc[n].shape))
        grads[n] = small_sum[off:off + size].reshape(w_loc[n].shape)
        off += size

    deltas, new_m, new_v = {}, {}, {}
    for n in WEIGHTS:
        deltas[n], new_m[n], new_v[n] = _adamw_call("adamw_" + n, w_loc[n], grads[n], m_loc[n], v_loc[n])
    return (loss, grad_x, *[grads[n] for n in WEIGHTS], *[deltas[n] for n in WEIGHTS],
            *[new_m[n] for n in WEIGHTS], *[new_v[n] for n in WEIGHTS])
```

```python
import functools

import numpy as np
import jax
import jax.numpy as jnp
from jax import lax
from jax.experimental import pallas as pl
from jax.experimental.pallas import tpu as pltpu

F32 = jnp.float32
BF16 = jnp.bfloat16
_MXU_DTYPE = jnp.bfloat16
_WIRE_DTYPE = jnp.bfloat16
_VMEM_LIMIT = 52 * 1024 * 1024
_LANES = 128
_ATT_BLOCK = 512
_ROW_TILE = 512
_SG_TILE = 1024

NORM_EPS = 1e-6
HEADS = 8
NOPE = 64
ROPE = 32
VDIM = 64
QK_DIM = NOPE + ROPE
HP = 128
Q_LORA = 384
KV_LORA = 256
SG_WIDTH = 512
SG_GROUPS = 8
SG_GDIM = 64
SG_CHUNK = 128
ROPE_THETA = 10000.0
PROJ_W = Q_LORA + KV_LORA + HP + 2 * SG_WIDTH
ADAM_LR = 0.001
ADAM_B1 = 0.9
ADAM_B2 = 0.999
ADAM_EPS = 1e-08
ADAM_WD = 0.01
ADAM_STEP = 10
MESH = pl.DeviceIdType.MESH
PACK_COLS = 1024
PACK_ROW_MULT = 256

SHARDED = ['ffn_pre_w_gate', 'ffn_pre_w_up', 'ffn_pre_w_down', 'ffn_post_w_gate', 'ffn_post_w_up',
           'ffn_post_w_down', 'even_w_in', 'w_uq', 'w_ukv', 'even_w_out', 'conv_w_in', 'conv_w', 'conv_w_out']
SHARD_AXIS = {'ffn_pre_w_gate': 2, 'ffn_pre_w_up': 2, 'ffn_pre_w_down': 1, 'ffn_post_w_gate': 2,
              'ffn_post_w_up': 2, 'ffn_post_w_down': 1, 'even_w_in': 2, 'w_uq': 2, 'w_ukv': 2,
              'even_w_out': 1, 'conv_w_in': 2, 'conv_w': 2, 'conv_w_out': 1}
FFN_WEIGHTS = ['ffn_pre_w_gate', 'ffn_pre_w_up', 'ffn_pre_w_down', 'ffn_post_w_gate', 'ffn_post_w_up',
               'ffn_post_w_down']
REPLICATED = ['ffn_pre_norm', 'mix_norm', 'ffn_post_norm', 'q_norm', 'kv_norm', 'sg_norm', 'sg_w', 'sg_b',
              'final_norm']
WEIGHTS = ['ffn_pre_norm', 'ffn_pre_w_gate', 'ffn_pre_w_up', 'ffn_pre_w_down', 'mix_norm', 'ffn_post_norm',
           'ffn_post_w_gate', 'ffn_post_w_up', 'ffn_post_w_down', 'even_w_in', 'q_norm', 'w_uq', 'kv_norm',
           'w_ukv', 'sg_norm', 'sg_w', 'sg_b', 'even_w_out', 'conv_w_in', 'conv_w', 'conv_w_out', 'final_norm']


def _params(sem=None):
    return pltpu.CompilerParams(vmem_limit_bytes=_VMEM_LIMIT,
                                **({} if sem is None else {'dimension_semantics': sem}))


def _pick(n, pref, mult):
    best = None
    t = mult
    while t <= min(n, pref):
        if n % t == 0:
            best = t
        t += mult
    return n if best is None else best


def _mx(v):
    return v if v.dtype == _MXU_DTYPE else v.astype(_MXU_DTYPE)


def _sigmoid(a):
    return 1.0 / (1.0 + jnp.exp(-a))


def _rms_stats(x):
    rstd = lax.rsqrt(jnp.mean(x * x, axis=-1, keepdims=True) + NORM_EPS)
    return x * rstd, rstd


def _rms_bwd(x, g, dh):
    xhat, rstd = _rms_stats(x)
    gdh = g * dh
    dx = rstd * (gdh - xhat * jnp.mean(gdh * xhat, axis=-1, keepdims=True))
    return dx, dh * xhat


def _fused_matmul(name, mode, lhs, rhs, prods, n_acc, epilogue, out_dtypes, M, N, K, tm, tn, tk,
                  tile_extras=(), row_extras=(), mrow_extras=(), n_colsum=0):
    gj, gi, gk = N // tn, M // tm, K // tk
    assert gj * tn == N and gi * tm == M and gk * tk == K, (name, M, N, K, tm, tn, tk)
    dims = {'nn': (((1,), (0,)), ((), ())), 'nt': (((1,), (1,)), ((), ())), 'tn': (((0,), (0,)), ((), ()))}[mode]

    def lhs_spec(roff, coff, kb):
        kb = tk if kb is None else kb
        if mode == 'tn':
            return pl.BlockSpec((kb, tm), lambda j, i, k: (k + roff, i + coff))
        return pl.BlockSpec((tm, kb), lambda j, i, k: (i + roff, k + coff))

    def rhs_spec(roff, coff, kb):
        kb = tk if kb is None else kb
        if mode == 'nt':
            return pl.BlockSpec((tn, kb), lambda j, i, k: (j + roff, k + coff))
        return pl.BlockSpec((kb, tn), lambda j, i, k: (k + roff, j + coff))

    in_specs = [lhs_spec(*a[1:]) for a in lhs] + [rhs_spec(*a[1:]) for a in rhs]
    in_specs += [pl.BlockSpec((tm, tn), lambda j, i, k: (i, j)) for _ in tile_extras]
    in_specs += [pl.BlockSpec((1, tn), lambda j, i, k: (0, j)) for _ in row_extras]
    in_specs += [pl.BlockSpec((tm, a.shape[1]), lambda j, i, k: (i, 0)) for a in mrow_extras]
    n_out = len(out_dtypes)
    out_shape = [jax.ShapeDtypeStruct((M, N), d) for d in out_dtypes]
    out_specs = [pl.BlockSpec((tm, tn), lambda j, i, k: (i, j)) for _ in out_dtypes]
    out_shape += [jax.ShapeDtypeStruct((1, N), F32) for _ in range(n_colsum)]
    out_specs += [pl.BlockSpec((1, tn), lambda j, i, k: (0, j)) for _ in range(n_colsum)]
    scratch = [pltpu.VMEM((tm, tn), F32) for _ in range(n_acc)] if gk > 1 else []
    nl, nr, nt, nrw, nm = len(lhs), len(rhs), len(tile_extras), len(row_extras), len(mrow_extras)

    def body(*refs):
        pos = 0
        lhs_refs = refs[pos:pos + nl]; pos += nl
        rhs_refs = refs[pos:pos + nr]; pos += nr
        tile_refs = refs[pos:pos + nt]; pos += nt
        row_refs = refs[pos:pos + nrw]; pos += nrw
        mrow_refs = refs[pos:pos + nm]; pos += nm
        out_refs = refs[pos:pos + n_out]; pos += n_out
        cs_refs = refs[pos:pos + n_colsum]; pos += n_colsum
        acc_refs = refs[pos:]
        i = pl.program_id(1)
        k = pl.program_id(2)

        def partials():
            res = [None] * n_acc
            for (li, ri, ai) in prods:
                d = lax.dot_general(_mx(lhs_refs[li][...]), _mx(rhs_refs[ri][...]), dims,
                                    preferred_element_type=F32)
                res[ai] = d if res[ai] is None else res[ai] + d
            return res

        def finish(accs):
            outs = epilogue(accs, [r[...] for r in tile_refs], [r[...] for r in row_refs],
                            [r[...] for r in mrow_refs])
            for r, o in zip(out_refs, outs[:n_out]):
                r[...] = o.astype(r.dtype)
            for r, c in zip(cs_refs, outs[n_out:]):
                c = jnp.sum(c, axis=0, keepdims=True)

                @pl.when(i == 0)
                def _():
                    r[...] = c

                @pl.when(i != 0)
                def _():
                    r[...] += c

        if gk == 1:
            finish(partials())
        else:
            p = partials()

            @pl.when(k == 0)
            def _():
                for r, v in zip(acc_refs, p):
                    r[...] = v

            @pl.when(k != 0)
            def _():
                for r, v in zip(acc_refs, p):
                    r[...] += v

            @pl.when(k == gk - 1)
            def _():
                finish([r[...] for r in acc_refs])

    res = pl.pallas_call(
        body, name=name, grid=(gj, gi, gk), in_specs=in_specs, out_specs=out_specs, out_shape=out_shape,
        scratch_shapes=scratch, compiler_params=_params(("arbitrary", "arbitrary", "arbitrary")),
    )(*[a[0] for a in lhs], *[a[0] for a in rhs], *tile_extras, *row_extras, *mrow_extras)
    return res


def _op(a, roff=0, coff=0, kb=None):
    return (a, roff, coff, kb)


def _ident_epi(scale=None):
    def epi(accs, tiles, rows, mrows):
        return [a if scale is None else a * scale for a in accs]
    return epi


def _resid_norm_epi(scale):
    def epi(accs, tiles, rows, mrows):
        x_new = tiles[0] + scale * accs[0]
        xhat, _ = _rms_stats(x_new)
        return [x_new, xhat * rows[0]]
    return epi


def _norm_bwd_epi(accs, tiles, rows, mrows):
    dx_n, dg = _rms_bwd(tiles[0], rows[0], accs[0])
    dx = tiles[1] + dx_n
    return [dx, dx, dg]


def _rmsnorm_call(name, x, g):
    S, D = x.shape
    tm = _pick(S, _ROW_TILE, 8)

    def body(x_ref, g_ref, h_ref):
        xhat, _ = _rms_stats(x_ref[...])
        h_ref[...] = (xhat * g_ref[...]).astype(h_ref.dtype)

    return pl.pallas_call(
        body, name=name, grid=(S // tm,),
        in_specs=[pl.BlockSpec((tm, D), lambda i: (i, 0)), pl.BlockSpec((1, D), lambda i: (0, 0))],
        out_specs=pl.BlockSpec((tm, D), lambda i: (i, 0)),
        out_shape=jax.ShapeDtypeStruct((S, D), BF16), compiler_params=_params(("arbitrary",)),
    )(x, g)


def _loss_call(x, target, g):
    S, D = x.shape
    tm = _pick(S, _ROW_TILE, 8)

    def body(x_ref, t_ref, g_ref, dx_ref, dxb_ref, dg_ref, loss_ref):
        i = pl.program_id(0)
        x_t = x_ref[...]
        gain = g_ref[...]
        xhat, _ = _rms_stats(x_t)
        diff = xhat * gain - t_ref[...]
        dy = diff * (1.0 / D)
        dx, dg = _rms_bwd(x_t, gain, dy)
        dx_ref[...] = dx
        dxb_ref[...] = dx.astype(BF16)
        dg = jnp.sum(dg, axis=0, keepdims=True)
        part = 0.5 * jnp.sum(jnp.sum(diff * diff, axis=1, keepdims=True), axis=0, keepdims=True) * (1.0 / D)
        part = jnp.broadcast_to(part, (1, _LANES))

        @pl.when(i == 0)
        def _():
            dg_ref[...] = dg
            loss_ref[...] = part

        @pl.when(i != 0)
        def _():
            dg_ref[...] += dg
            loss_ref[...] += part

    row = lambda i: (i, 0)
    fixed = lambda i: (0, 0)
    return pl.pallas_call(
        body, name="loss_head", grid=(S // tm,),
        in_specs=[pl.BlockSpec((tm, D), row), pl.BlockSpec((tm, D), row), pl.BlockSpec((1, D), fixed)],
        out_specs=[pl.BlockSpec((tm, D), row), pl.BlockSpec((tm, D), row), pl.BlockSpec((1, D), fixed),
                   pl.BlockSpec((1, _LANES), fixed)],
        out_shape=[jax.ShapeDtypeStruct((S, D), F32), jax.ShapeDtypeStruct((S, D), BF16),
                   jax.ShapeDtypeStruct((1, D), F32), jax.ShapeDtypeStruct((1, _LANES), F32)],
        compiler_params=_params(("arbitrary",)),
    )(x, target, g)


def _rope_tables_call(pos_col, invf, mask_a, mask_b):
    S = pos_col.shape[0]
    tm = _pick(S, _ROW_TILE, 8)

    def body(p_ref, f_ref, a_ref, b_ref, cos_ref, sa_ref, sb_ref):
        ang = p_ref[...].astype(F32) * f_ref[...]
        sn = jnp.sin(ang)
        cos_ref[...] = jnp.cos(ang)
        sa_ref[...] = sn * a_ref[...]
        sb_ref[...] = sn * b_ref[...]

    row = lambda i: (i, 0)
    fixed = lambda i: (0, 0)
    return pl.pallas_call(
        body, name="rope_tables", grid=(S // tm,),
        in_specs=[pl.BlockSpec((tm, 1), row)] + [pl.BlockSpec((1, HP), fixed)] * 3,
        out_specs=[pl.BlockSpec((tm, HP), row)] * 3,
        out_shape=[jax.ShapeDtypeStruct((S, HP), F32)] * 3, compiler_params=_params(("arbitrary",)),
    )(pos_col, invf, mask_a, mask_b)


def _rope(t, cos, sa, sb):
    return t * cos + pltpu.roll(t, HP - ROPE // 2, 1) * sa + pltpu.roll(t, ROPE // 2, 1) * sb


def _rope_t(d, cos, sa, sb):
    return d * cos + pltpu.roll(d * sa, ROPE // 2, 1) + pltpu.roll(d * sb, HP - ROPE // 2, 1)


def _gelu(z):
    return 0.5 * z * (1.0 + lax.erf(z * np.float32(1.0 / np.sqrt(2.0))))


def _gelu_grad(z):
    cdf = 0.5 * (1.0 + lax.erf(z * np.float32(1.0 / np.sqrt(2.0))))
    pdf = np.float32(1.0 / np.sqrt(2.0 * np.pi)) * jnp.exp(-0.5 * z * z)
    return cdf + z * pdf


_CQ0, _CKV0, _KR0, _Z0 = 0, Q_LORA, Q_LORA + KV_LORA, Q_LORA + KV_LORA + HP


def _even_prep_call(proj, qn, kvn, sgn, cos, sa, sb):
    S = proj.shape[0]
    tm = _pick(S, 256, 8)

    def body(p_ref, qn_ref, kvn_ref, sgn_ref, cos_ref, sa_ref, sb_ref, cq_ref, ckv_ref, kr_ref, u_ref, v_ref):
        cq = p_ref[:, _CQ0:_CQ0 + Q_LORA]
        cq_ref[...] = (_rms_stats(cq)[0] * qn_ref[...]).astype(BF16)
        ckv = p_ref[:, _CKV0:_CKV0 + KV_LORA]
        ckv_ref[...] = (_rms_stats(ckv)[0] * kvn_ref[...]).astype(BF16)
        kr = p_ref[:, _KR0:_KR0 + HP]
        kr_ref[...] = _rope(kr, cos_ref[...], sa_ref[...], sb_ref[...]).astype(BF16)
        u_ref[...] = _gelu(p_ref[:, _Z0:_Z0 + SG_WIDTH]).astype(BF16)
        zv = _gelu(p_ref[:, _Z0 + SG_WIDTH:_Z0 + 2 * SG_WIDTH])
        v_ref[...] = (_rms_stats(zv)[0] * sgn_ref[...]).astype(BF16)

    row = lambda i: (i, 0)
    fixed = lambda i: (0, 0)
    widths = [Q_LORA, KV_LORA, HP, SG_WIDTH, SG_WIDTH]
    return pl.pallas_call(
        body, name="even_prep", grid=(S // tm,),
        in_specs=[pl.BlockSpec((tm, PROJ_W), row), pl.BlockSpec((1, Q_LORA), fixed),
                  pl.BlockSpec((1, KV_LORA), fixed), pl.BlockSpec((1, SG_WIDTH), fixed)]
        + [pl.BlockSpec((tm, HP), row)] * 3,
        out_specs=[pl.BlockSpec((tm, w), row) for w in widths],
        out_shape=[jax.ShapeDtypeStruct((S, w), BF16) for w in widths],
        compiler_params=_params(("arbitrary",)),
    )(proj, qn, kvn, sgn, cos, sa, sb)


def _even_prep_bwd_call(proj, qn, kvn, sgn, cos, sa, sb, dcqn, dckvn, dk, du, dvn):
    S = proj.shape[0]
    tm = _pick(S, 256, 8)

    def body(p_ref, qn_ref, kvn_ref, sgn_ref, cos_ref, sa_ref, sb_ref, dcq_ref, dckv_ref, dk_ref, du_ref,
             dvn_ref, dp_ref, dqn_ref, dkvn_ref, dsgn_ref):
        i = pl.program_id(0)
        dcq, gq = _rms_bwd(p_ref[:, _CQ0:_CQ0 + Q_LORA], qn_ref[...], dcq_ref[...])
        dp_ref[:, _CQ0:_CQ0 + Q_LORA] = dcq.astype(BF16)
        dckv, gkv = _rms_bwd(p_ref[:, _CKV0:_CKV0 + KV_LORA], kvn_ref[...], dckv_ref[...])
        dp_ref[:, _CKV0:_CKV0 + KV_LORA] = dckv.astype(BF16)
        dkr = dk_ref[:, 0:HP].astype(F32)
        for h in range(1, HEADS):
            dkr = dkr + dk_ref[:, h * HP:(h + 1) * HP].astype(F32)
        lane = lax.broadcasted_iota(jnp.int32, dkr.shape, 1)
        dkr = jnp.where((lane >= NOPE) & (lane < QK_DIM), dkr, 0.0)
        dp_ref[:, _KR0:_KR0 + HP] = _rope_t(dkr, cos_ref[...], sa_ref[...], sb_ref[...]).astype(BF16)
        zu = p_ref[:, _Z0:_Z0 + SG_WIDTH]
        dp_ref[:, _Z0:_Z0 + SG_WIDTH] = (du_ref[...].astype(F32) * _gelu_grad(zu)).astype(BF16)
        zv = p_ref[:, _Z0 + SG_WIDTH:_Z0 + 2 * SG_WIDTH]
        dgv, gsg = _rms_bwd(_gelu(zv), sgn_ref[...], dvn_ref[...].astype(F32))
        dp_ref[:, _Z0 + SG_WIDTH:_Z0 + 2 * SG_WIDTH] = (dgv * _gelu_grad(zv)).astype(BF16)
        sums = [jnp.sum(t, axis=0, keepdims=True) for t in (gq, gkv, gsg)]

        @pl.when(i == 0)
        def _():
            for r, s in zip((dqn_ref, dkvn_ref, dsgn_ref), sums):
                r[...] = s

        @pl.when(i != 0)
        def _():
            for r, s in zip((dqn_ref, dkvn_ref, dsgn_ref), sums):
                r[...] += s

    row = lambda i: (i, 0)
    fixed = lambda i: (0, 0)
    return pl.pallas_call(
        body, name="even_prep_bwd", grid=(S // tm,),
        in_specs=[pl.BlockSpec((tm, PROJ_W), row), pl.BlockSpec((1, Q_LORA), fixed),
                  pl.BlockSpec((1, KV_LORA), fixed), pl.BlockSpec((1, SG_WIDTH), fixed)]
        + [pl.BlockSpec((tm, HP), row)] * 3
        + [pl.BlockSpec((tm, Q_LORA), row), pl.BlockSpec((tm, KV_LORA), row),
           pl.BlockSpec((tm, HEADS * HP), row), pl.BlockSpec((tm, SG_WIDTH), row),
           pl.BlockSpec((tm, SG_WIDTH), row)],
        out_specs=[pl.BlockSpec((tm, PROJ_W), row), pl.BlockSpec((1, Q_LORA), fixed),
                   pl.BlockSpec((1, KV_LORA), fixed), pl.BlockSpec((1, SG_WIDTH), fixed)],
        out_shape=[jax.ShapeDtypeStruct((S, PROJ_W), BF16), jax.ShapeDtypeStruct((1, Q_LORA), F32),
                   jax.ShapeDtypeStruct((1, KV_LORA), F32), jax.ShapeDtypeStruct((1, SG_WIDTH), F32)],
        compiler_params=_params(("arbitrary",)),
    )(proj, qn, kvn, sgn, cos, sa, sb, dcqn, dckvn, dk, du, dvn)


def _causal_mask(rows, cols):
    r = lax.broadcasted_iota(jnp.int32, (rows, cols), 0)
    c = lax.broadcasted_iota(jnp.int32, (rows, cols), 1)
    return c <= r


def _flash_fwd_call(q, k, v):
    S = q.shape[0]
    tb = _pick(S, _ATT_BLOCK, 128)
    nq = S // tb
    nt_dims = (((1,), (1,)), ((), ()))

    def body(q_ref, k_ref, v_ref, o_ref, lse_ref):
        i = pl.program_id(1)
        q_t = q_ref[...]

        def step(j, carry, masked):
            m, l, acc = carry
            off = pl.multiple_of(j * tb, tb)
            k_t = k_ref[pl.ds(off, tb), :]
            v_t = v_ref[pl.ds(off, tb), :]
            s = lax.dot_general(q_t, k_t, nt_dims, preferred_element_type=F32)
            if masked:
                s = jnp.where(_causal_mask(tb, tb), s, -1e30)
            m_new = jnp.maximum(m, jnp.max(s, axis=1, keepdims=True))
            alpha = jnp.exp(m - m_new)
            p = jnp.exp(s - m_new)
            l = alpha * l + jnp.sum(p, axis=1, keepdims=True)
            acc = alpha * acc + jnp.dot(p.astype(v_t.dtype), v_t, preferred_element_type=F32)
            return m_new, l, acc

        init = (jnp.full((tb, 1), -1e30, F32), jnp.zeros((tb, 1), F32), jnp.zeros((tb, HP), F32))
        carry = lax.fori_loop(0, i, lambda j, c: step(j, c, False), init)
        m, l, acc = step(i, carry, True)
        o_ref[...] = (acc / l).astype(o_ref.dtype)
        lse = jnp.broadcast_to(m + jnp.log(l), (tb, HP))
        lse_ref[0, 0] = jnp.transpose(lse)[0:8, :]

    return pl.pallas_call(
        body, name="flash_fwd", grid=(HEADS, nq),
        in_specs=[pl.BlockSpec((tb, HP), lambda h, i: (i, h)), pl.BlockSpec((S, HP), lambda h, i: (0, h)),
                  pl.BlockSpec((S, HP), lambda h, i: (0, h))],
        out_specs=[pl.BlockSpec((tb, HP), lambda h, i: (i, h)),
                   pl.BlockSpec((1, 1, 8, tb), lambda h, i: (h, i, 0, 0))],
        out_shape=[jax.ShapeDtypeStruct((S, HEADS * HP), q.dtype), jax.ShapeDtypeStruct((HEADS, nq, 8, tb), F32)],
        compiler_params=_params(("arbitrary", "arbitrary")),
    )(q, k, v)


def _attn_delta_call(o, do):
    S = o.shape[0]
    tb = _pick(S, _ATT_BLOCK, 128)

    def body(o_ref, do_ref, d_ref):
        d = jnp.sum(o_ref[...].astype(F32) * do_ref[...].astype(F32), axis=1, keepdims=True)
        d_ref[0, 0] = jnp.transpose(jnp.broadcast_to(d, (tb, HP)))[0:8, :]

    return pl.pallas_call(
        body, name="attn_delta", grid=(HEADS, S // tb),
        in_specs=[pl.BlockSpec((tb, HP), lambda h, i: (i, h))] * 2,
        out_specs=pl.BlockSpec((1, 1, 8, tb), lambda h, i: (h, i, 0, 0)),
        out_shape=jax.ShapeDtypeStruct((HEADS, S // tb, 8, tb), F32), compiler_params=_params(("arbitrary", "arbitrary")),
    )(o, do)


def _flash_bwd_call(q, k, v, do, lse, delta):
    S = q.shape[0]
    tb = _pick(S, _ATT_BLOCK, 128)
    nq = S // tb
    nt_dims = (((1,), (1,)), ((), ()))
    tn_dims = (((0,), (0,)), ((), ()))

    def body(q_ref, do_ref, lse_ref, dl_ref, k_ref, v_ref, dq_ref, dk_ref, dv_ref):
        j = pl.program_id(1)
        k_t = k_ref[...]
        v_t = v_ref[...]

        @pl.when(j == 0)
        def _():
            dq_ref[...] = jnp.zeros_like(dq_ref)

        def step(i, carry, masked):
            dk, dv = carry
            off = pl.multiple_of(i * tb, tb)
            q_t = q_ref[pl.ds(off, tb), :]
            do_t = do_ref[pl.ds(off, tb), :]
            lse_row = lse_ref[0, i, 0:1, :]
            dl_row = dl_ref[0, i, 0:1, :]
            st = lax.dot_general(k_t, q_t, nt_dims, preferred_element_type=F32)
            pt = jnp.exp(st - lse_row)
            if masked:
                pt = jnp.where(jnp.transpose(_causal_mask(tb, tb)), pt, 0.0)
            dpt = lax.dot_general(v_t, do_t, nt_dims, preferred_element_type=F32)
            dst = (pt * (dpt - dl_row)).astype(q_t.dtype)
            dv = dv + jnp.dot(pt.astype(do_t.dtype), do_t, preferred_element_type=F32)
            dk = dk + jnp.dot(dst, q_t, preferred_element_type=F32)
            dq_ref[pl.ds(off, tb), :] += lax.dot_general(dst, k_t, tn_dims, preferred_element_type=F32)
            return dk, dv

        zero = jnp.zeros((tb, HP), F32)
        carry = step(j, (zero, zero), True)
        dk, dv = lax.fori_loop(j + 1, nq, lambda i, c: step(i, c, False), carry)
        dk_ref[...] = dk.astype(dk_ref.dtype)
        dv_ref[...] = dv.astype(dv_ref.dtype)

    head = lambda h, j: (0, h)
    blk = lambda h, j: (j, h)
    rows = lambda h, j: (h, 0, 0, 0)
    return pl.pallas_call(
        body, name="flash_bwd", grid=(HEADS, nq),
        in_specs=[pl.BlockSpec((S, HP), head), pl.BlockSpec((S, HP), head), pl.BlockSpec((1, nq, 8, tb), rows),
                  pl.BlockSpec((1, nq, 8, tb), rows), pl.BlockSpec((tb, HP), blk), pl.BlockSpec((tb, HP), blk)],
        out_specs=[pl.BlockSpec((S, HP), head), pl.BlockSpec((tb, HP), blk), pl.BlockSpec((tb, HP), blk)],
        out_shape=[jax.ShapeDtypeStruct((S, HEADS * HP), F32), jax.ShapeDtypeStruct((S, HEADS * HP), BF16),
                   jax.ShapeDtypeStruct((S, HEADS * HP), BF16)],
        compiler_params=_params(("arbitrary", "arbitrary")),
    )(q, do, lse, delta, k, v)


def _sg_mixed(w_ref, vch, lane_lo):
    blocks = []
    for jb in range(SG_WIDTH // _LANES):
        r = jnp.dot(w_ref[jb], vch[:, jb * _LANES:(jb + 1) * _LANES], preferred_element_type=F32)
        blocks.append(jnp.where(lane_lo, r[0:SG_CHUNK], r[SG_CHUNK:2 * SG_CHUNK]))
    return jnp.concatenate(blocks, axis=1)


def _sgu_fwd_call(vn, u, attn, wst, bexp):
    S = vn.shape[0]
    tm = _pick(S, _SG_TILE, SG_CHUNK)
    AW = HEADS * HP

    def body(v_ref, u_ref, a_ref, w_ref, b_ref, mix_ref):
        lane_lo = lax.broadcasted_iota(jnp.int32, (SG_CHUNK, _LANES), 1) < SG_GDIM
        mix_ref[:, 0:AW] = a_ref[...]
        for c in range(tm // SG_CHUNK):
            rs = slice(c * SG_CHUNK, (c + 1) * SG_CHUNK)
            mixed = _sg_mixed(w_ref, v_ref[rs, :], lane_lo) + b_ref[...]
            mix_ref[rs, AW:AW + SG_WIDTH] = (u_ref[rs, :].astype(F32) * mixed).astype(mix_ref.dtype)

    row = lambda i: (i, 0)
    return pl.pallas_call(
        body, name="sgu_fwd", grid=(S // tm,),
        in_specs=[pl.BlockSpec((tm, SG_WIDTH), row), pl.BlockSpec((tm, SG_WIDTH), row), pl.BlockSpec((tm, AW), row),
                  pl.BlockSpec((SG_WIDTH // _LANES, 2 * SG_CHUNK, SG_CHUNK), lambda i: (0, 0, 0)),
                  pl.BlockSpec((SG_CHUNK, SG_WIDTH), lambda i: (0, 0))],
        out_specs=pl.BlockSpec((tm, AW + SG_WIDTH), row),
        out_shape=jax.ShapeDtypeStruct((S, AW + SG_WIDTH), BF16), compiler_params=_params(("arbitrary",)),
    )(vn, u, attn, wst, bexp)


def _sgu_bwd_call(dmix, vn, u, wst, wst_t, bexp):
    S = vn.shape[0]
    tm = _pick(S, _SG_TILE, SG_CHUNK)
    nblk = SG_WIDTH // _LANES
    col0 = (HEADS * HP) // SG_WIDTH
    nt_dims = (((1,), (1,)), ((), ()))

    def body(d_ref, v_ref, u_ref, w_ref, wt_ref, b_ref, du_ref, dv_ref, dw_ref, db_ref, dbacc_ref):
        i = pl.program_id(0)
        lane_lo = lax.broadcasted_iota(jnp.int32, (SG_CHUNK, _LANES), 1) < SG_GDIM

        @pl.when(i == 0)
        def _():
            dw_ref[...] = jnp.zeros_like(dw_ref)
            dbacc_ref[...] = jnp.zeros_like(dbacc_ref)

        for c in range(tm // SG_CHUNK):
            rs = slice(c * SG_CHUNK, (c + 1) * SG_CHUNK)
            vch = v_ref[rs, :]
            dsg = d_ref[rs, :].astype(F32)
            mixed = _sg_mixed(w_ref, vch, lane_lo) + b_ref[...]
            du_ref[rs, :] = (dsg * mixed).astype(du_ref.dtype)
            dmixed = dsg * u_ref[rs, :].astype(F32)
            dbacc_ref[...] += dmixed
            dmx = dmixed.astype(vch.dtype)
            dv_ref[rs, :] = _sg_mixed(wt_ref, dmx, lane_lo).astype(dv_ref.dtype)
            for jb in range(nblk):
                dblk = dmx[:, jb * _LANES:(jb + 1) * _LANES]
                vblk = vch[:, jb * _LANES:(jb + 1) * _LANES]
                zero = jnp.zeros_like(dblk)
                dw_ref[2 * jb] += lax.dot_general(jnp.where(lane_lo, dblk, zero), vblk, nt_dims,
                                                  preferred_element_type=F32)
                dw_ref[2 * jb + 1] += lax.dot_general(jnp.where(lane_lo, zero, dblk), vblk, nt_dims,
                                                      preferred_element_type=F32)

        @pl.when(i == pl.num_programs(0) - 1)
        def _():
            tri = _causal_mask(SG_CHUNK, SG_CHUNK)
            for g in range(SG_GROUPS):
                dw_ref[g] = jnp.where(tri, dw_ref[g], 0.0)
            lane = lax.broadcasted_iota(jnp.int32, (SG_CHUNK, _LANES), 1)
            out = jnp.zeros((SG_CHUNK, _LANES), F32)
            for g in range(SG_GROUPS):
                blk = dbacc_ref[:, (g // 2) * _LANES:(g // 2 + 1) * _LANES]
                sel = lane_lo if g % 2 == 0 else jnp.logical_not(lane_lo)
                s = jnp.sum(jnp.where(sel, blk, 0.0), axis=1, keepdims=True)
                out = jnp.where(lane == g, s, out)
            db_ref[...] = out

    row = lambda i: (i, 0)
    wspec = pl.BlockSpec((nblk, 2 * SG_CHUNK, SG_CHUNK), lambda i: (0, 0, 0))
    return pl.pallas_call(
        body, name="sgu_bwd", grid=(S // tm,),
        in_specs=[pl.BlockSpec((tm, SG_WIDTH), lambda i: (i, col0)), pl.BlockSpec((tm, SG_WIDTH), row),
                  pl.BlockSpec((tm, SG_WIDTH), row), wspec, wspec,
                  pl.BlockSpec((SG_CHUNK, SG_WIDTH), lambda i: (0, 0))],
        out_specs=[pl.BlockSpec((tm, SG_WIDTH), row), pl.BlockSpec((tm, SG_WIDTH), row),
                   pl.BlockSpec((SG_GROUPS, SG_CHUNK, SG_CHUNK), lambda i: (0, 0, 0)),
                   pl.BlockSpec((SG_CHUNK, _LANES), lambda i: (0, 0))],
        out_shape=[jax.ShapeDtypeStruct((S, SG_WIDTH), BF16), jax.ShapeDtypeStruct((S, SG_WIDTH), BF16),
                   jax.ShapeDtypeStruct((SG_GROUPS, SG_CHUNK, SG_CHUNK), F32),
                   jax.ShapeDtypeStruct((SG_CHUNK, _LANES), F32)],
        scratch_shapes=[pltpu.VMEM((SG_CHUNK, SG_WIDTH), F32)],
        compiler_params=_params(("arbitrary",)),
    )(dmix, vn, u, wst, wst_t, bexp)


def _shift_down(t, halo, n):
    rows = lax.broadcasted_iota(jnp.int32, t.shape, 0)
    out = pltpu.roll(t, n, 0)
    for r in range(n):
        out = jnp.where(rows == r, halo[8 - n + r:8 - n + r + 1, :], out)
    return out


def _shift_up(t, halo, n):
    tm = t.shape[0]
    rows = lax.broadcasted_iota(jnp.int32, t.shape, 0)
    out = pltpu.roll(t, tm - n, 0)
    for r in range(n):
        out = jnp.where(rows == tm - n + r, halo[r:r + 1, :], out)
    return out


def _conv_fwd_call(p, w):
    S, C3 = p.shape
    C = C3 // 3
    tm = _pick(S, _ROW_TILE, 8)
    hb = tm // 8

    def body(p_ref, c_prev, z_prev, w_ref, m_ref):
        i = pl.program_id(0)
        cz = p_ref[:, C:2 * C] * p_ref[:, 2 * C:3 * C]
        czp = jnp.where(i > 0, c_prev[...] * z_prev[...], 0.0)
        y = w_ref[2:3, :] * cz + w_ref[1:2, :] * _shift_down(cz, czp, 1) + w_ref[0:1, :] * _shift_down(cz, czp, 2)
        m_ref[...] = (p_ref[:, 0:C] * y).astype(m_ref.dtype)

    prev = lambda col: (lambda i: (jnp.maximum(i * hb - 1, 0), col))
    return pl.pallas_call(
        body, name="conv_fwd", grid=(S // tm,),
        in_specs=[pl.BlockSpec((tm, C3), lambda i: (i, 0)), pl.BlockSpec((8, C), prev(1)),
                  pl.BlockSpec((8, C), prev(2)), pl.BlockSpec((3, C), lambda i: (0, 0))],
        out_specs=pl.BlockSpec((tm, C), lambda i: (i, 0)),
        out_shape=jax.ShapeDtypeStruct((S, C), BF16), compiler_params=_params(("arbitrary",)),
    )(p, p, p, w)


def _conv_bwd_call(p, w, dm):
    S, C3 = p.shape
    C = C3 // 3
    tm = _pick(S, 256, 8)
    hb = tm // 8
    n_tiles = S // tm

    def body(p_ref, c_prev, z_prev, b_next, dm_ref, dm_next, w_ref, dp_ref, dw_ref):
        i = pl.program_id(0)
        b = p_ref[:, 0:C]
        c = p_ref[:, C:2 * C]
        z = p_ref[:, 2 * C:3 * C]
        cz = c * z
        czp = jnp.where(i > 0, c_prev[...] * z_prev[...], 0.0)
        s1 = _shift_down(cz, czp, 1)
        s2 = _shift_down(cz, czp, 2)
        w0, w1, w2 = w_ref[0:1, :], w_ref[1:2, :], w_ref[2:3, :]
        y = w2 * cz + w1 * s1 + w0 * s2
        dm_t = dm_ref[...]
        dy = dm_t * b
        dyn = jnp.where(i < n_tiles - 1, dm_next[...] * b_next[...], 0.0)
        dcz = w2 * dy + w1 * _shift_up(dy, dyn, 1) + w0 * _shift_up(dy, dyn, 2)
        dp_ref[:, 0:C] = (dm_t * y).astype(dp_ref.dtype)
        dp_ref[:, C:2 * C] = (dcz * z).astype(dp_ref.dtype)
        dp_ref[:, 2 * C:3 * C] = (dcz * c).astype(dp_ref.dtype)
        dw = jnp.concatenate([jnp.sum(dy * s2, axis=0, keepdims=True), jnp.sum(dy * s1, axis=0, keepdims=True),
                              jnp.sum(dy * cz, axis=0, keepdims=True)], axis=0)

        @pl.when(i == 0)
        def _():
            dw_ref[...] = dw

        @pl.when(i != 0)
        def _():
            dw_ref[...] += dw

    prev = lambda col: (lambda i: (jnp.maximum(i * hb - 1, 0), col))
    nxt = lambda col: (lambda i: (jnp.minimum((i + 1) * hb, S // 8 - 1), col))
    return pl.pallas_call(
        body, name="conv_bwd", grid=(n_tiles,),
        in_specs=[pl.BlockSpec((tm, C3), lambda i: (i, 0)), pl.BlockSpec((8, C), prev(1)),
                  pl.BlockSpec((8, C), prev(2)), pl.BlockSpec((8, C), nxt(0)),
                  pl.BlockSpec((tm, C), lambda i: (i, 0)), pl.BlockSpec((8, C), nxt(0)),
                  pl.BlockSpec((3, C), lambda i: (0, 0))],
        out_specs=[pl.BlockSpec((tm, C3), lambda i: (i, 0)), pl.BlockSpec((3, C), lambda i: (0, 0))],
        out_shape=[jax.ShapeDtypeStruct((S, C3), BF16), jax.ShapeDtypeStruct((3, C), F32)],
        compiler_params=_params(("arbitrary",)),
    )(p, p, p, p, dm, dm, w)


def _my_place():
    return lax.axis_index("x"), lax.axis_index("y"), lax.axis_index("c")


def _gather_weights_call(name, shard):
    R, C = shard.shape

    def body(s_ref, o_ref, send_sems, recv_sems, local_sem):
        x, y, c = _my_place()
        mine = 2 * x + y
        local = pltpu.make_async_copy(s_ref, o_ref.at[mine], local_sem)
        local.start()
        peers = [(1 - x, y), (x, 1 - y), (1 - x, 1 - y)]
        copies = []
        for k, (px, py) in enumerate(peers):
            cp = pltpu.make_async_remote_copy(src_ref=s_ref, dst_ref=o_ref.at[mine], send_sem=send_sems.at[k],
                                              recv_sem=recv_sems.at[k], device_id=(px, py, c), device_id_type=MESH)
            cp.start()
            copies.append(cp)
        for k, (px, py) in enumerate(peers):
            pltpu.make_async_remote_copy(src_ref=s_ref, dst_ref=o_ref.at[2 * px + py], send_sem=send_sems.at[k],
                                         recv_sem=recv_sems.at[k], device_id=(px, py, c),
                                         device_id_type=MESH).wait_recv()
        for cp in copies:
            cp.wait_send()
        local.wait()

    any_spec = pl.BlockSpec(memory_space=pl.ANY)
    return pl.pallas_call(
        body, name=name, in_specs=[any_spec], out_specs=any_spec,
        out_shape=jax.ShapeDtypeStruct((4, R, C), shard.dtype),
        scratch_shapes=[pltpu.SemaphoreType.DMA((3,)), pltpu.SemaphoreType.DMA((3,)), pltpu.SemaphoreType.DMA],
        compiler_params=pltpu.CompilerParams(has_side_effects=True),
    )(shard)


_D2D_CHUNKS = 4


def _gather_halves_call(shard):
    R, C = shard.shape
    Rh = R // 2
    rc = Rh // _D2D_CHUNKS
    assert rc * _D2D_CHUNKS * 2 == R and rc % 16 == 0, (R, rc)

    def body(s_ref, o_ref, ici_send, ici_recv, d2d_send, d2d_recv, local_sem):
        x, y, c = _my_place()
        mine = 2 * x + y
        local = pltpu.make_async_copy(s_ref, o_ref.at[mine], local_sem)
        local.start()
        chips = [(1 - x, y), (x, 1 - y), (1 - x, 1 - y)]
        my_half = pl.ds(pl.multiple_of(c * Rh, 16), Rh)

        def ici(k, chip, block):
            return pltpu.make_async_remote_copy(src_ref=s_ref.at[my_half], dst_ref=o_ref.at[block, my_half],
                                                send_sem=ici_send.at[k], recv_sem=ici_recv.at[k],
                                                device_id=(chip[0], chip[1], c), device_id_type=MESH)

        def d2d(k, j, block, half):
            rows = pl.ds(pl.multiple_of(half * Rh + j * rc, 16), rc)
            return pltpu.make_async_remote_copy(src_ref=o_ref.at[block, rows], dst_ref=o_ref.at[block, rows],
                                                send_sem=d2d_send.at[k * _D2D_CHUNKS + j],
                                                recv_sem=d2d_recv.at[k * _D2D_CHUNKS + j],
                                                device_id=(x, y, 1 - c), device_id_type=MESH)

        sends = [ici(k, chip, mine) for k, chip in enumerate(chips)]
        for cp in sends:
            cp.start()
        for k, chip in enumerate(chips):
            block = 2 * chip[0] + chip[1]
            ici(k, chip, block).wait_recv()
            for j in range(_D2D_CHUNKS):
                cp = d2d(k, j, block, c)
                cp.start()
                sends.append(cp)
        for k, chip in enumerate(chips):
            for j in range(_D2D_CHUNKS):
                d2d(k, j, 2 * chip[0] + chip[1], 1 - c).wait_recv()
        for cp in sends:
            cp.wait_send()
        local.wait()

    any_spec = pl.BlockSpec(memory_space=pl.ANY)
    n_d2d = 3 * _D2D_CHUNKS
    return pl.pallas_call(
        body, name="gather_weights", in_specs=[any_spec], out_specs=any_spec,
        out_shape=jax.ShapeDtypeStruct((4, R, C), shard.dtype),
        scratch_shapes=[pltpu.SemaphoreType.DMA((3,)), pltpu.SemaphoreType.DMA((3,)),
                        pltpu.SemaphoreType.DMA((n_d2d,)), pltpu.SemaphoreType.DMA((n_d2d,)),
                        pltpu.SemaphoreType.DMA],
        compiler_params=pltpu.CompilerParams(has_side_effects=True),
    )(shard)


def _pair_exchange_call(packed):
    nb, _, Rh, C = packed.shape
    rc = Rh // _D2D_CHUNKS
    assert rc * _D2D_CHUNKS == Rh and rc % 16 == 0

    def body(p_ref, o_ref, send_sems, recv_sems):
        x, y, c = _my_place()

        def cp(b, j):
            rows = pl.ds(j * rc, rc)
            return pltpu.make_async_remote_copy(src_ref=p_ref.at[b, 1 - c, rows], dst_ref=o_ref.at[b, rows],
                                                send_sem=send_sems.at[b * _D2D_CHUNKS + j],
                                                recv_sem=recv_sems.at[b * _D2D_CHUNKS + j],
                                                device_id=(x, y, 1 - c), device_id_type=MESH)

        copies = [cp(b, j) for b in range(nb) for j in range(_D2D_CHUNKS)]
        for t in copies:
            t.start()
        for t in copies:
            t.wait_recv()
        for t in copies:
            t.wait_send()

    any_spec = pl.BlockSpec(memory_space=pl.ANY)
    n = nb * _D2D_CHUNKS
    return pl.pallas_call(
        body, name="pair_exchange", in_specs=[any_spec], out_specs=any_spec,
        out_shape=jax.ShapeDtypeStruct((nb, Rh, C), packed.dtype),
        scratch_shapes=[pltpu.SemaphoreType.DMA((n,)), pltpu.SemaphoreType.DMA((n,))],
        compiler_params=pltpu.CompilerParams(has_side_effects=True),
    )(packed)


def _pair_add_call(packed, other, core):
    nb, _, Rh, C = packed.shape
    tr = _pick(Rh, 512, 16)

    def body(c_ref, p_ref, o_ref, q_ref):
        q_ref[...] = (p_ref[...].astype(F32) + o_ref[...].astype(F32)).astype(q_ref.dtype)

    grid_spec = pltpu.PrefetchScalarGridSpec(
        num_scalar_prefetch=1, grid=(nb, Rh // tr),
        in_specs=[pl.BlockSpec((None, None, tr, C), lambda b, r, c_ref: (b, c_ref[0], r, 0)),
                  pl.BlockSpec((None, tr, C), lambda b, r, c_ref: (b, r, 0))],
        out_specs=pl.BlockSpec((None, tr, C), lambda b, r, c_ref: (b, r, 0)))
    return pl.pallas_call(
        body, name="pair_add", grid_spec=grid_spec, out_shape=jax.ShapeDtypeStruct((nb, Rh, C), packed.dtype),
        compiler_params=_params(("arbitrary", "arbitrary")),
    )(core, packed, other)


def _chip_scatter_call(pairs):
    nb, Rh, C = pairs.shape

    def body(p_ref, o_ref, send_sems, recv_sems, local_sem):
        x, y, c = _my_place()
        mine = 2 * x + y
        local = pltpu.make_async_copy(p_ref.at[mine], o_ref.at[mine], local_sem)
        local.start()
        chips = [(1 - x, y), (x, 1 - y), (1 - x, 1 - y)]
        copies = []
        for k, (px, py) in enumerate(chips):
            t = pltpu.make_async_remote_copy(src_ref=p_ref.at[2 * px + py], dst_ref=o_ref.at[mine],
                                             send_sem=send_sems.at[k], recv_sem=recv_sems.at[k],
                                             device_id=(px, py, c), device_id_type=MESH)
            t.start()
            copies.append(t)
        for k, (px, py) in enumerate(chips):
            pltpu.make_async_remote_copy(src_ref=p_ref.at[mine], dst_ref=o_ref.at[2 * px + py],
                                         send_sem=send_sems.at[k], recv_sem=recv_sems.at[k],
                                         device_id=(px, py, c), device_id_type=MESH).wait_recv()
        for t in copies:
            t.wait_send()
        local.wait()

    any_spec = pl.BlockSpec(memory_space=pl.ANY)
    return pl.pallas_call(
        body, name="chip_scatter", in_specs=[any_spec], out_specs=any_spec,
        out_shape=jax.ShapeDtypeStruct((nb, Rh, C), pairs.dtype),
        scratch_shapes=[pltpu.SemaphoreType.DMA((3,)), pltpu.SemaphoreType.DMA((3,)), pltpu.SemaphoreType.DMA],
        compiler_params=pltpu.CompilerParams(has_side_effects=True),
    )(pairs)


def _sum_slots_call(name, parts):
    n, R, C = parts.shape
    tr = _pick(R, 256, 8)

    def body(p_ref, o_ref):
        acc = p_ref[0].astype(F32)
        for s in range(1, n):
            acc = acc + p_ref[s].astype(F32)
        o_ref[...] = acc

    return pl.pallas_call(
        body, name=name, grid=(R // tr,),
        in_specs=[pl.BlockSpec((n, tr, C), lambda i: (0, i, 0))], out_specs=pl.BlockSpec((tr, C), lambda i: (i, 0)),
        out_shape=jax.ShapeDtypeStruct((R, C), F32), compiler_params=_params(("arbitrary",)),
    )(parts)


def _sibling_share_call(half):
    R, C = half.shape
    n = 2 * _D2D_CHUNKS
    rc = R // n
    assert rc * n == R and rc % 8 == 0

    def body(h_ref, o_ref, send_sems, recv_sems, local_sem):
        x, y, c = _my_place()
        local = pltpu.make_async_copy(h_ref, o_ref.at[c], local_sem)
        local.start()

        def cp(j, slot):
            rows = pl.ds(j * rc, rc)
            return pltpu.make_async_remote_copy(src_ref=h_ref.at[rows], dst_ref=o_ref.at[slot, rows],
                                                send_sem=send_sems.at[j], recv_sem=recv_sems.at[j],
                                                device_id=(x, y, 1 - c), device_id_type=MESH)

        copies = [cp(j, c) for j in range(n)]
        for t in copies:
            t.start()
        for j in range(n):
            cp(j, 1 - c).wait_recv()
        for t in copies:
            t.wait_send()
        local.wait()

    any_spec = pl.BlockSpec(memory_space=pl.ANY)
    return pl.pallas_call(
        body, name="sibling_share", in_specs=[any_spec], out_specs=any_spec,
        out_shape=jax.ShapeDtypeStruct((2, R, C), half.dtype),
        scratch_shapes=[pltpu.SemaphoreType.DMA((n,)), pltpu.SemaphoreType.DMA((n,)), pltpu.SemaphoreType.DMA],
        compiler_params=pltpu.CompilerParams(has_side_effects=True),
    )(half)


def _allreduce_small_call(part):
    R, C = part.shape

    def body(p_ref, o_ref, slots, send_sems, recv_sems):
        x, y, c = _my_place()
        me = 4 * x + 2 * y + c
        peers = []
        for k in range(1, 8):
            px = x ^ (k >> 2) if (k >> 2) else x
            py = y ^ ((k >> 1) & 1) if ((k >> 1) & 1) else y
            pc = c ^ (k & 1) if (k & 1) else c
            peers.append((px, py, pc))
        copies = []
        for k, (px, py, pc) in enumerate(peers):
            cp = pltpu.make_async_remote_copy(src_ref=p_ref, dst_ref=slots.at[me], send_sem=send_sems.at[k],
                                              recv_sem=recv_sems.at[k], device_id=(px, py, pc), device_id_type=MESH)
            cp.start()
            copies.append(cp)
        slots[me] = p_ref[...]
        for k, (px, py, pc) in enumerate(peers):
            pltpu.make_async_remote_copy(src_ref=p_ref, dst_ref=slots.at[4 * px + 2 * py + pc],
                                         send_sem=send_sems.at[k], recv_sem=recv_sems.at[k],
                                         device_id=(px, py, pc), device_id_type=MESH).wait_recv()
        for cp in copies:
            cp.wait_send()
        acc = slots[0]
        for s in range(1, 8):
            acc = acc + slots[s]
        o_ref[...] = acc

    vm = pl.BlockSpec(memory_space=pltpu.VMEM)
    return pl.pallas_call(
        body, name="allreduce_small", in_specs=[vm], out_specs=vm,
        out_shape=jax.ShapeDtypeStruct((R, C), F32),
        scratch_shapes=[pltpu.VMEM((8, R, C), F32), pltpu.SemaphoreType.DMA((7,)), pltpu.SemaphoreType.DMA((7,))],
        compiler_params=pltpu.CompilerParams(has_side_effects=True, vmem_limit_bytes=_VMEM_LIMIT),
    )(part)


def _adamw_call(name, w, g, m, v):
    shape = w.shape
    cols = shape[-1]
    rows = int(np.prod(shape[:-1])) if len(shape) > 1 else 1
    w2, g2, m2, v2 = (t.reshape(rows, cols) for t in (w, g, m, v))
    tr = _pick(rows, 256, 8)
    c1 = 1.0 / (1.0 - ADAM_B1 ** ADAM_STEP)
    c2 = 1.0 / (1.0 - ADAM_B2 ** ADAM_STEP)

    def body(w_ref, g_ref, m_ref, v_ref, d_ref, nm_ref, nv_ref):
        gr = g_ref[...]
        m_new = ADAM_B1 * m_ref[...] + (1.0 - ADAM_B1) * gr
        v_new = ADAM_B2 * v_ref[...] + (1.0 - ADAM_B2) * (gr * gr)
        m_hat = m_new / (1.0 - ADAM_B1 ** ADAM_STEP)
        v_hat = v_new / (1.0 - ADAM_B2 ** ADAM_STEP)
        d_ref[...] = -ADAM_LR * (m_hat / (jnp.sqrt(v_hat) + ADAM_EPS) + ADAM_WD * w_ref[...])
        nm_ref[...] = m_new
        nv_ref[...] = v_new

    spec = pl.BlockSpec((tr, cols), lambda i: (i, 0))
    d, nm, nv = pl.pallas_call(
        body, name=name, grid=(rows // tr,), in_specs=[spec] * 4, out_specs=[spec] * 3,
        out_shape=[jax.ShapeDtypeStruct((rows, cols), F32)] * 3, compiler_params=_params(("arbitrary",)),
    )(w2, g2, m2, v2)
    return d.reshape(shape), nm.reshape(shape), nv.reshape(shape)


def _pad_rows(flat, mult):
    n = flat.shape[0]
    unit = PACK_COLS * mult
    total = -(-n // unit) * unit
    return jnp.pad(flat, (0, total - n)).reshape(total // PACK_COLS, PACK_COLS)


def _shard_slice(arr, axis, blk, nblk=4):
    w = arr.shape[axis] // nblk
    return lax.slice_in_dim(arr, blk * w, (blk + 1) * w, axis=axis)


_NT = (((1,), (1,)), ((), ()))
_TN = (((0,), (0,)), ((), ()))


def _ffn_fwd(tag, x, h, wg, wu, wd, g_next):
    S, D = x.shape
    NB, _, Fb = wg.shape
    tm = _pick(S, 1024, 8)

    def gate_up(h_ref, wg_ref, wu_ref, a_ref, u_ref, s_ref):
        h_t = _mx(h_ref[...])
        a = jnp.dot(h_t, _mx(wg_ref[...]), preferred_element_type=F32)
        u = jnp.dot(h_t, _mx(wu_ref[...]), preferred_element_type=F32)
        a_ref[...] = a.astype(a_ref.dtype)
        u_ref[...] = u.astype(u_ref.dtype)
        s_ref[...] = (a * _sigmoid(a) * u).astype(s_ref.dtype)

    wspec = pl.BlockSpec((None, D, Fb), lambda b, i: (b, 0, 0))
    hid = pl.BlockSpec((None, tm, Fb), lambda b, i: (b, i, 0))
    a, u, s = pl.pallas_call(
        gate_up, name=tag + "_gate_up", grid=(NB, S // tm),
        in_specs=[pl.BlockSpec((tm, D), lambda b, i: (i, 0)), wspec, wspec], out_specs=[hid] * 3,
        out_shape=[jax.ShapeDtypeStruct((NB, S, Fb), BF16)] * 3, compiler_params=_params(("arbitrary", "arbitrary")),
    )(h, wg, wu)

    tm2 = _pick(S, 512, 8)

    def down(s_ref, wd_ref, x_ref, g_ref, xo_ref, ho_ref):
        acc = jnp.dot(_mx(s_ref[0]), _mx(wd_ref[0]), preferred_element_type=F32)
        for b in range(1, NB):
            acc = acc + jnp.dot(_mx(s_ref[b]), _mx(wd_ref[b]), preferred_element_type=F32)
        x_new = x_ref[...] + 0.5 * acc
        xo_ref[...] = x_new
        ho_ref[...] = (_rms_stats(x_new)[0] * g_ref[...]).astype(ho_ref.dtype)

    row = pl.BlockSpec((tm2, D), lambda i: (i, 0))
    x_new, h_next = pl.pallas_call(
        down, name=tag + "_down", grid=(S // tm2,),
        in_specs=[pl.BlockSpec((NB, tm2, Fb), lambda i: (0, i, 0)), pl.BlockSpec((NB, Fb, D), lambda i: (0, 0, 0)),
                  row, pl.BlockSpec((1, D), lambda i: (0, 0))],
        out_specs=[row, row], out_shape=[jax.ShapeDtypeStruct((S, D), F32), jax.ShapeDtypeStruct((S, D), BF16)],
        compiler_params=_params(("arbitrary",)),
    )(s, wd, x, g_next)
    return x_new, h_next, (x, h, a, u, s)


def _ffn_bwd(tag, saved, dx_out, dxb, wg, wu, wd, gain):
    x, h, a, u, s = saved
    S, D = x.shape
    NB, _, Fb = wg.shape
    tm = _pick(S, 1024, 8)
    tk = _pick(S, 1024, 128)
    nk = S // tk

    def dgate_up(d_ref, wd_ref, a_ref, u_ref, da_ref, du_ref):
        ds = 0.5 * lax.dot_general(_mx(d_ref[...]), _mx(wd_ref[...]), _NT, preferred_element_type=F32)
        a_t = a_ref[...].astype(F32)
        u_t = u_ref[...].astype(F32)
        sig = _sigmoid(a_t)
        da_ref[...] = (ds * u_t * (sig * (1.0 + a_t * (1.0 - sig)))).astype(da_ref.dtype)
        du_ref[...] = (ds * (a_t * sig)).astype(du_ref.dtype)

    hid = pl.BlockSpec((None, tm, Fb), lambda b, i: (b, i, 0))
    da, du = pl.pallas_call(
        dgate_up, name=tag + "_dgate_up", grid=(NB, S // tm),
        in_specs=[pl.BlockSpec((tm, D), lambda b, i: (i, 0)), pl.BlockSpec((None, Fb, D), lambda b, i: (b, 0, 0)),
                  hid, hid],
        out_specs=[hid, hid], out_shape=[jax.ShapeDtypeStruct((NB, S, Fb), BF16)] * 2,
        compiler_params=_params(("arbitrary", "arbitrary")),
    )(dxb, wd, a, u)

    def dw_down(s_ref, d_ref, o_ref, acc_ref):
        k = pl.program_id(1)
        p = lax.dot_general(_mx(s_ref[...]), _mx(d_ref[...]), _TN, preferred_element_type=F32)

        @pl.when(k == 0)
        def _():
            acc_ref[...] = p

        @pl.when(k != 0)
        def _():
            acc_ref[...] += p

        @pl.when(k == nk - 1)
        def _():
            o_ref[...] = (0.5 * acc_ref[...]).astype(o_ref.dtype)

    hk = pl.BlockSpec((None, tk, Fb), lambda b, k: (b, k, 0))
    dwd = pl.pallas_call(
        dw_down, name=tag + "_dw_down", grid=(NB, nk),
        in_specs=[hk, pl.BlockSpec((tk, D), lambda b, k: (k, 0))],
        out_specs=pl.BlockSpec((None, Fb, D), lambda b, k: (b, 0, 0)),
        out_shape=jax.ShapeDtypeStruct((NB, Fb, D), _WIRE_DTYPE), scratch_shapes=[pltpu.VMEM((Fb, D), F32)],
        compiler_params=_params(("arbitrary", "arbitrary")),
    )(s, dxb)

    def dw_gate_up(h_ref, da_ref, du_ref, og_ref, ou_ref, accg_ref, accu_ref):
        k = pl.program_id(1)
        h_t = _mx(h_ref[...])
        pg = lax.dot_general(h_t, _mx(da_ref[...]), _TN, preferred_element_type=F32)
        pu = lax.dot_general(h_t, _mx(du_ref[...]), _TN, preferred_element_type=F32)

        @pl.when(k == 0)
        def _():
            accg_ref[...] = pg
            accu_ref[...] = pu

        @pl.when(k != 0)
        def _():
            accg_ref[...] += pg
            accu_ref[...] += pu

        @pl.when(k == nk - 1)
        def _():
            og_ref[...] = accg_ref[...].astype(og_ref.dtype)
            ou_ref[...] = accu_ref[...].astype(ou_ref.dtype)

    wout = pl.BlockSpec((None, D, Fb), lambda b, k: (b, 0, 0))
    dwg, dwu = pl.pallas_call(
        dw_gate_up, name=tag + "_dw_gate_up", grid=(NB, nk),
        in_specs=[pl.BlockSpec((tk, D), lambda b, k: (k, 0)), hk, hk], out_specs=[wout, wout],
        out_shape=[jax.ShapeDtypeStruct((NB, D, Fb), _WIRE_DTYPE)] * 2,
        scratch_shapes=[pltpu.VMEM((D, Fb), F32)] * 2, compiler_params=_params(("arbitrary", "arbitrary")),
    )(h, da, du)

    tm2 = _pick(S, 512, 8)

    def dx_body(da_ref, du_ref, wg_hbm, wu_hbm, x_ref, dxo_ref, g_ref, dx_ref, dxb_ref, dg_ref, wg_v, wu_v, sem):
        i = pl.program_id(0)

        @pl.when(i == 0)
        def _():
            cg = pltpu.make_async_copy(wg_hbm, wg_v, sem.at[0])
            cu = pltpu.make_async_copy(wu_hbm, wu_v, sem.at[1])
            cg.start()
            cu.start()
            cg.wait()
            cu.wait()

        dh = None
        for b in range(NB):
            t = lax.dot_general(_mx(da_ref[b]), wg_v[b], _NT, preferred_element_type=F32)
            t = t + lax.dot_general(_mx(du_ref[b]), wu_v[b], _NT, preferred_element_type=F32)
            dh = t if dh is None else dh + t
        dx_n, dg = _rms_bwd(x_ref[...], g_ref[...], dh)
        dx = dxo_ref[...] + dx_n
        dx_ref[...] = dx
        dxb_ref[...] = dx.astype(dxb_ref.dtype)
        dg = jnp.sum(dg, axis=0, keepdims=True)

        @pl.when(i == 0)
        def _():
            dg_ref[...] = dg

        @pl.when(i != 0)
        def _():
            dg_ref[...] += dg

    row = pl.BlockSpec((tm2, D), lambda i: (i, 0))
    hid2 = pl.BlockSpec((NB, tm2, Fb), lambda i: (0, i, 0))
    anyspec = pl.BlockSpec(memory_space=pl.ANY)
    fixed = pl.BlockSpec((1, D), lambda i: (0, 0))
    dx, dxb_new, dgain = pl.pallas_call(
        dx_body, name=tag + "_dx", grid=(S // tm2,),
        in_specs=[hid2, hid2, anyspec, anyspec, row, row, fixed], out_specs=[row, row, fixed],
        out_shape=[jax.ShapeDtypeStruct((S, D), F32), jax.ShapeDtypeStruct((S, D), BF16),
                   jax.ShapeDtypeStruct((1, D), F32)],
        scratch_shapes=[pltpu.VMEM((NB, D, Fb), wg.dtype), pltpu.VMEM((NB, D, Fb), wu.dtype),
                        pltpu.SemaphoreType.DMA((2,))],
        compiler_params=_params(("arbitrary",)),
    )(da, du, wg, wu, x, dx_out, gain)
    return dx, dxb_new, dgain, dwg, dwu, dwd


def _conv_mixer_fwd(tag, x, h, w_in, w_taps, w_out, g_next):
    S, D = x.shape
    C3 = w_in.shape[1]
    tm = _pick(S, 512, 8)
    p, = _fused_matmul(tag + "_in", 'nn', [_op(h)], [_op(w_in)], [(0, 0, 0)], 1, _ident_epi(), [F32],
                       S, C3, D, tm, _pick(C3, 1024, 128), D)
    m = _conv_fwd_call(p, w_taps)
    x_new, h_next = _fused_matmul(tag + "_out", 'nn', [_op(m)], [_op(w_out)], [(0, 0, 0)], 1, _resid_norm_epi(1.0),
                                  [F32, BF16], S, D, D, tm, D, D, tile_extras=[x], row_extras=[g_next])
    return x_new, h_next, (x, h, p, m)


def _conv_mixer_bwd(tag, saved, dx_out, dxb, w_in, w_taps, w_out, gain):
    x, h, p, m = saved
    S, D = x.shape
    C3 = w_in.shape[1]
    tm = _pick(S, 512, 8)
    tk = _pick(S, 512, 128)
    dm, = _fused_matmul(tag + "_dm", 'nt', [_op(dxb)], [_op(w_out)], [(0, 0, 0)], 1, _ident_epi(), [F32],
                        S, D, D, tm, D, D)
    dw_out, = _fused_matmul(tag + "_dw_out", 'tn', [_op(m)], [_op(dxb)], [(0, 0, 0)], 1, _ident_epi(), [F32],
                            D, D, S, D, D, tk)
    dp, dtaps = _conv_bwd_call(p, w_taps, dm)
    dw_in, = _fused_matmul(tag + "_dw_in", 'tn', [_op(h)], [_op(dp)], [(0, 0, 0)], 1, _ident_epi(), [F32],
                           D, C3, S, D, _pick(C3, 1024, 128), tk)
    dx, dxb_new, dgain = _fused_matmul(tag + "_dx", 'nt', [_op(dp)], [_op(w_in)], [(0, 0, 0)], 1, _norm_bwd_epi,
                                       [F32, BF16], S, D, C3, tm, D, _pick(C3, 1024, 128),
                                       tile_extras=[x, dx_out], row_extras=[gain], n_colsum=1)
    return dx, dxb_new, dgain, dw_in, dtaps, dw_out


def _attn_scale():
    return np.float32(QK_DIM ** -0.5)


def _even_mixer_fwd(tag, x, h, wts, tables, g_next):
    S, D = x.shape
    cos, sa, sb = tables
    tm = _pick(S, 512, 8)
    AW = HEADS * HP
    proj, = _fused_matmul(tag + "_in", 'nn', [_op(h)], [_op(wts['w_in'])], [(0, 0, 0)], 1, _ident_epi(), [F32],
                          S, PROJ_W, D, tm, _pick(PROJ_W, 896, 128), D)
    cqn, ckvn, kr, u, vn = _even_prep_call(proj, wts['q_norm'], wts['kv_norm'], wts['sg_norm'], cos, sa, sb)
    scale = _attn_scale()

    def q_epi(accs, tiles, rows, mrows):
        c_t, a_t, b_t = mrows
        heads = [_rope(accs[0][:, hh * HP:(hh + 1) * HP], c_t, a_t, b_t) * scale for hh in range(HEADS)]
        return [jnp.concatenate(heads, axis=1)]

    q, = _fused_matmul(tag + "_q", 'nn', [_op(cqn)], [_op(wts['w_q'])], [(0, 0, 0)], 1, q_epi, [BF16],
                       S, AW, Q_LORA, tm, AW, Q_LORA, mrow_extras=[cos, sa, sb])

    def kv_epi(accs, tiles, rows, mrows):
        return [accs[0] + jnp.concatenate([mrows[0].astype(F32)] * HEADS, axis=1), accs[1]]

    k, v = _fused_matmul(tag + "_kv", 'nn', [_op(ckvn)], [_op(wts['w_k']), _op(wts['w_v'])],
                         [(0, 0, 0), (0, 1, 1)], 2, kv_epi, [BF16, BF16], S, AW, KV_LORA, tm, AW, KV_LORA,
                         mrow_extras=[kr])
    o, lse = _flash_fwd_call(q, k, v)
    mix = _sgu_fwd_call(vn, u, o, wts['sg_wst'], wts['sg_bexp'])
    x_new, h_next = _fused_matmul(tag + "_out", 'nn', [_op(mix)], [_op(wts['w_out'])], [(0, 0, 0)], 1,
                                  _resid_norm_epi(1.0), [F32, BF16], S, D, AW + SG_WIDTH, tm, D, AW + SG_WIDTH,
                                  tile_extras=[x], row_extras=[g_next])
    return x_new, h_next, (x, h, proj, cqn, ckvn, u, vn, q, k, v, o, lse, mix)


def _even_mixer_bwd(tag, saved, dx_out, dxb, wts, tables, gain):
    x, h, proj, cqn, ckvn, u, vn, q, k, v, o, lse, mix = saved
    S, D = x.shape
    cos, sa, sb = tables
    tm = _pick(S, 512, 8)
    tk = _pick(S, 512, 128)
    AW = HEADS * HP
    MW = AW + SG_WIDTH
    dmix, = _fused_matmul(tag + "_dmix", 'nt', [_op(dxb)], [_op(wts['w_out'])], [(0, 0, 0)], 1, _ident_epi(), [BF16],
                          S, MW, D, tm, _pick(MW, 768, 128), D)
    dw_out, = _fused_matmul(tag + "_dw_out", 'tn', [_op(mix)], [_op(dxb)], [(0, 0, 0)], 1, _ident_epi(), [F32],
                            MW, D, S, _pick(MW, 768, 128), D, tk)
    du, dvn, dsg_w, dsg_b = _sgu_bwd_call(dmix, vn, u, wts['sg_wst'], wts['sg_wst_t'], wts['sg_bexp'])
    delta = _attn_delta_call(o, dmix)
    dq, dk, dv = _flash_bwd_call(q, k, v, dmix, lse, delta)
    scale = _attn_scale()

    def dq_epi(accs, tiles, rows, mrows):
        return accs

    def dq_pre_call():
        tr = _pick(S, 256, 8)

        def body(d_ref, c_ref, a_ref, b_ref, o_ref):
            for hh in range(HEADS):
                t = _rope_t(d_ref[:, hh * HP:(hh + 1) * HP], c_ref[...], a_ref[...], b_ref[...]) * scale
                o_ref[:, hh * HP:(hh + 1) * HP] = t.astype(o_ref.dtype)

        row = lambda i: (i, 0)
        return pl.pallas_call(
            body, name=tag + "_dq_unrope", grid=(S // tr,),
            in_specs=[pl.BlockSpec((tr, AW), row)] + [pl.BlockSpec((tr, HP), row)] * 3,
            out_specs=pl.BlockSpec((tr, AW), row), out_shape=jax.ShapeDtypeStruct((S, AW), BF16),
            compiler_params=_params(("arbitrary",)),
        )(dq, cos, sa, sb)

    dqp = dq_pre_call()
    dw_q, = _fused_matmul(tag + "_dw_q", 'tn', [_op(cqn)], [_op(dqp)], [(0, 0, 0)], 1, _ident_epi(), [F32],
                          Q_LORA, AW, S, Q_LORA, AW, tk)
    dcqn, = _fused_matmul(tag + "_dcq", 'nt', [_op(dqp)], [_op(wts['w_q'])], [(0, 0, 0)], 1, dq_epi, [F32],
                          S, Q_LORA, AW, tm, Q_LORA, AW)
    dw_k, dw_v = _fused_matmul(tag + "_dw_kv", 'tn', [_op(ckvn)], [_op(dk), _op(dv)], [(0, 0, 0), (0, 1, 1)], 2,
                               _ident_epi(), [F32, F32], KV_LORA, AW, S, KV_LORA, AW, tk)
    dckvn, = _fused_matmul(tag + "_dckv", 'nt', [_op(dk), _op(dv)], [_op(wts['w_k']), _op(wts['w_v'])],
                           [(0, 0, 0), (1, 1, 0)], 1, dq_epi, [F32], S, KV_LORA, AW, tm, KV_LORA, AW)
    dproj, dqn, dkvn, dsgn = _even_prep_bwd_call(proj, wts['q_norm'], wts['kv_norm'], wts['sg_norm'], cos, sa, sb,
                                                 dcqn, dckvn, dk, du, dvn)
    dw_in, = _fused_matmul(tag + "_dw_in", 'tn', [_op(h)], [_op(dproj)], [(0, 0, 0)], 1, _ident_epi(), [F32],
                           D, PROJ_W, S, D, _pick(PROJ_W, 896, 128), tk)
    dx, dxb_new, dgain = _fused_matmul(tag + "_dx", 'nt', [_op(dproj)], [_op(wts['w_in'])], [(0, 0, 0)], 1,
                                       _norm_bwd_epi, [F32, BF16], S, D, PROJ_W, tm, D, _pick(PROJ_W, 896, 128),
                                       tile_extras=[x, dx_out], row_extras=[gain], n_colsum=1)
    grads = dict(w_in=dw_in, w_q=dw_q, w_k=dw_k, w_v=dw_v, w_out=dw_out, q_norm=dqn, kv_norm=dkvn, sg_norm=dsgn,
                 sg_w=dsg_w, sg_b=dsg_b)
    return dx, dxb_new, dgain, grads


def _even_weights(w_in, w_uq, w_ukv, w_out, q_norm, kv_norm, sg_norm, sg_w, sg_b):
    D = w_in.shape[0]
    kr_cols = jnp.pad(w_in[:, Q_LORA + KV_LORA:Q_LORA + KV_LORA + ROPE], ((0, 0), (NOPE, HP - QK_DIM)))
    w_in_p = jnp.concatenate([w_in[:, :Q_LORA + KV_LORA], kr_cols, w_in[:, Q_LORA + KV_LORA + ROPE:]], axis=1)
    wq = w_uq.reshape(Q_LORA, HEADS, QK_DIM)
    w_q = jnp.pad(wq, ((0, 0), (0, 0), (0, HP - QK_DIM))).reshape(Q_LORA, HEADS * HP)
    wkv = w_ukv.reshape(KV_LORA, HEADS, NOPE + VDIM)
    w_k = jnp.pad(wkv[:, :, :NOPE], ((0, 0), (0, 0), (0, HP - NOPE))).reshape(KV_LORA, HEADS * HP)
    w_v = jnp.pad(wkv[:, :, NOPE:], ((0, 0), (0, 0), (0, HP - VDIM))).reshape(KV_LORA, HEADS * HP)
    wo_a = w_out[:HEADS * VDIM].reshape(HEADS, VDIM, D)
    wo_a = jnp.pad(wo_a, ((0, 0), (0, HP - VDIM), (0, 0))).reshape(HEADS * HP, D)
    w_out_p = jnp.concatenate([wo_a, w_out[HEADS * VDIM:]], axis=0)
    tri = jnp.tril(jnp.ones((SG_CHUNK, SG_CHUNK), F32))
    wm = sg_w * tri
    wst = wm.reshape(SG_GROUPS // 2, 2 * SG_CHUNK, SG_CHUNK).astype(_MXU_DTYPE)
    wst_t = jnp.swapaxes(wm, 1, 2).reshape(SG_GROUPS // 2, 2 * SG_CHUNK, SG_CHUNK).astype(_MXU_DTYPE)
    bexp = jnp.repeat(sg_b.T, SG_GDIM, axis=1)
    return dict(w_in=w_in_p, w_q=w_q, w_k=w_k, w_v=w_v, w_out=w_out_p, sg_wst=wst, sg_wst_t=wst_t, sg_bexp=bexp,
                q_norm=q_norm.reshape(1, -1), kv_norm=kv_norm.reshape(1, -1), sg_norm=sg_norm.reshape(1, -1))


def _even_grads_unpad(g):
    d_in = g['w_in']
    kr0 = Q_LORA + KV_LORA
    dw_in = jnp.concatenate([d_in[:, :kr0], d_in[:, kr0 + NOPE:kr0 + QK_DIM], d_in[:, kr0 + HP:]], axis=1)
    dw_uq = g['w_q'].reshape(Q_LORA, HEADS, HP)[:, :, :QK_DIM].reshape(Q_LORA, HEADS * QK_DIM)
    dk = g['w_k'].reshape(KV_LORA, HEADS, HP)[:, :, :NOPE]
    dv = g['w_v'].reshape(KV_LORA, HEADS, HP)[:, :, :VDIM]
    dw_ukv = jnp.concatenate([dk, dv], axis=2).reshape(KV_LORA, HEADS * (NOPE + VDIM))
    D = d_in.shape[0]
    wo = g['w_out']
    wo_a = wo[:HEADS * HP].reshape(HEADS, HP, D)[:, :VDIM].reshape(HEADS * VDIM, D)
    dw_out = jnp.concatenate([wo_a, wo[HEADS * HP:]], axis=0)
    dsg_b = g['sg_b'][:, :SG_GROUPS].T
    return dict(even_w_in=dw_in, w_uq=dw_uq, w_ukv=dw_ukv, even_w_out=dw_out, q_norm=g['q_norm'][0],
                kv_norm=g['kv_norm'][0], sg_norm=g['sg_norm'][0], sg_w=g['sg_w'], sg_b=dsg_b)


def kernel(x, positions, ffn_pre_norm, ffn_pre_w_gate, ffn_pre_w_up, ffn_pre_w_down, mix_norm, ffn_post_norm, ffn_post_w_gate, ffn_post_w_up, ffn_post_w_down, even_w_in, q_norm, w_uq, kv_norm, w_ukv, sg_norm, sg_w, sg_b, even_w_out, conv_w_in, conv_w, conv_w_out, final_norm, loss_target, m_ffn_pre_norm, m_ffn_pre_w_gate, m_ffn_pre_w_up, m_ffn_pre_w_down, m_mix_norm, m_ffn_post_norm, m_ffn_post_w_gate, m_ffn_post_w_up, m_ffn_post_w_down, m_even_w_in, m_q_norm, m_w_uq, m_kv_norm, m_w_ukv, m_sg_norm, m_sg_w, m_sg_b, m_even_w_out, m_conv_w_in, m_conv_w, m_conv_w_out, m_final_norm, v_ffn_pre_norm, v_ffn_pre_w_gate, v_ffn_pre_w_up, v_ffn_pre_w_down, v_mix_norm, v_ffn_post_norm, v_ffn_post_w_gate, v_ffn_post_w_up, v_ffn_post_w_down, v_even_w_in, v_q_norm, v_w_uq, v_kv_norm, v_w_ukv, v_sg_norm, v_sg_w, v_sg_b, v_even_w_out, v_conv_w_in, v_conv_w, v_conv_w_out, v_final_norm):
    env = dict(locals())
    w_loc = {n: env[n] for n in WEIGHTS}
    m_loc = {n: env['m_' + n] for n in WEIGHTS}
    v_loc = {n: env['v_' + n] for n in WEIGHTS}
    S, D = x.shape[1], x.shape[2]
    depth = ffn_pre_norm.shape[0]
    xs = x.reshape(S, D)
    target = loss_target.reshape(S, D)

    flat = jnp.concatenate([w_loc[n].astype(_WIRE_DTYPE).reshape(-1) for n in SHARDED])
    gathered = _gather_halves_call(_pad_rows(flat, PACK_ROW_MULT))
    taps = _gather_weights_call("gather_taps", _pad_rows(conv_w.reshape(-1), 8)).reshape(4, -1)
    taps = jnp.concatenate([taps[b, :conv_w.size].reshape(conv_w.shape) for b in range(4)], axis=2)
    gflat = gathered.reshape(4, -1)
    full = {}
    off = 0
    for n in SHARDED:
        shp = w_loc[n].shape
        size = int(np.prod(shp))
        seg = gflat[:, off:off + size].reshape((4,) + tuple(shp))
        if n in FFN_WEIGHTS:
            full[n] = [seg[:, l] for l in range(shp[0])]
        else:
            full[n] = jnp.concatenate([seg[b] for b in range(4)], axis=SHARD_AXIS[n])
        off += size

    inv_freq = ROPE_THETA ** (-jnp.arange(0, ROPE, 2, dtype=F32) / ROPE)
    half = ROPE // 2
    zeros = lambda n: jnp.zeros((n,), F32)
    ones = jnp.ones((half,), F32)
    invf = jnp.concatenate([zeros(NOPE), inv_freq, inv_freq, zeros(HP - QK_DIM)]).reshape(1, HP)
    mask_a = jnp.concatenate([zeros(NOPE), -ones, zeros(HP - NOPE - half)]).reshape(1, HP)
    mask_b = jnp.concatenate([zeros(NOPE + half), ones, zeros(HP - QK_DIM)]).reshape(1, HP)
    tables = _rope_tables_call(positions.reshape(S, 1), invf, mask_a, mask_b)

    even_w = []
    for e in range((depth + 1) // 2):
        even_w.append(_even_weights(full['even_w_in'][e], full['w_uq'][e], full['w_ukv'][e], full['even_w_out'][e],
                                    q_norm[e], kv_norm[e], sg_norm[e], sg_w[e], sg_b[e]))

    def gain_row(arr, l):
        return arr[l].reshape(1, D)

    saved = []
    h = _rmsnorm_call("first_norm", xs, gain_row(ffn_pre_norm, 0))
    xc = xs
    for l in range(depth):
        xc, h, s_pre = _ffn_fwd(f"l{l}_pre", xc, h, full['ffn_pre_w_gate'][l], full['ffn_pre_w_up'][l],
                                full['ffn_pre_w_down'][l], gain_row(mix_norm, l))
        if l % 2 == 0:
            xc, h, s_mix = _even_mixer_fwd(f"l{l}_mix", xc, h, even_w[l // 2], tables, gain_row(ffn_post_norm, l))
        else:
            o = l // 2
            xc, h, s_mix = _conv_mixer_fwd(f"l{l}_mix", xc, h, full['conv_w_in'][o], taps[o],
                                           full['conv_w_out'][o], gain_row(ffn_post_norm, l))
        g_next = gain_row(ffn_pre_norm, l + 1) if l + 1 < depth else final_norm.reshape(1, D)
        xc, h, s_post = _ffn_fwd(f"l{l}_post", xc, h, full['ffn_post_w_gate'][l], full['ffn_post_w_up'][l],
                                 full['ffn_post_w_down'][l], g_next)
        saved.append((s_pre, s_mix, s_post))

    dx, dxb, d_final, loss_part = _loss_call(xc, target, final_norm.reshape(1, D))
    loss = lax.psum(loss_part[0, 0], ("x", "y", "c"))

    gl = {n: [None] * w_loc[n].shape[0] for n in WEIGHTS if n != 'final_norm'}
    for l in reversed(range(depth)):
        s_pre, s_mix, s_post = saved[l]
        dx, dxb, dgain, dwg, dwu, dwd = _ffn_bwd(f"l{l}_post", s_post, dx, dxb, full['ffn_post_w_gate'][l],
                                                 full['ffn_post_w_up'][l], full['ffn_post_w_down'][l],
                                                 gain_row(ffn_post_norm, l))
        gl['ffn_post_norm'][l] = dgain[0]
        gl['ffn_post_w_gate'][l], gl['ffn_post_w_up'][l], gl['ffn_post_w_down'][l] = dwg, dwu, dwd
        if l % 2 == 0:
            e = l // 2
            dx, dxb, dgain, eg = _even_mixer_bwd(f"l{l}_mix", s_mix, dx, dxb, even_w[e], tables, gain_row(mix_norm, l))
            for n, val in _even_grads_unpad(eg).items():
                gl[n][e] = val
        else:
            o = l // 2
            dx, dxb, dgain, dw_in, dtaps, dw_out = _conv_mixer_bwd(f"l{l}_mix", s_mix, dx, dxb, full['conv_w_in'][o],
                                                                   taps[o], full['conv_w_out'][o],
                                                                   gain_row(mix_norm, l))
            gl['conv_w_in'][o], gl['conv_w'][o], gl['conv_w_out'][o] = dw_in, dtaps, dw_out
        gl['mix_norm'][l] = dgain[0]
        dx, dxb, dgain, dwg, dwu, dwd = _ffn_bwd(f"l{l}_pre", s_pre, dx, dxb, full['ffn_pre_w_gate'][l],
                                                 full['ffn_pre_w_up'][l], full['ffn_pre_w_down'][l],
                                                 gain_row(ffn_pre_norm, l))
        gl['ffn_pre_norm'][l] = dgain[0]
        gl['ffn_pre_w_gate'][l], gl['ffn_pre_w_up'][l], gl['ffn_pre_w_down'][l] = dwg, dwu, dwd
    grad_x = dx.reshape(x.shape)
    part = {n: jnp.stack(gl[n]) for n in gl}
    part['final_norm'] = d_final[0]

    segs = []
    for b in range(4):
        pieces = []
        for n in SHARDED:
            if n in FFN_WEIGHTS:
                pieces.append(part[n][:, b].reshape(-1))
            else:
                pieces.append(_shard_slice(part[n], SHARD_AXIS[n], b).astype(_WIRE_DTYPE).reshape(-1))
        segs.append(_pad_rows(jnp.concatenate(pieces), PACK_ROW_MULT))
    packed = jnp.stack(segs)
    packed = packed.reshape(4, 2, packed.shape[1] // 2, PACK_COLS)
    core = lax.axis_index("c").astype(jnp.int32).reshape(1)
    pairs = _pair_add_call(packed, _pair_exchange_call(packed), core)
    mine = _sum_slots_call("sum_grad_slots", _chip_scatter_call(pairs))
    both = _sibling_share_call(mine).reshape(-1)
    grads = {}
    off = 0
    for n in SHARDED:
        shp = w_loc[n].shape
        size = int(np.prod(shp))
        grads[n] = both[off:off + size].reshape(shp)
        off += size

    small = _pad_rows(jnp.concatenate([part[n].reshape(-1) for n in REPLICATED]), 8)
    small_sum = _allreduce_small_call(small).reshape(-1)
    off = 0
    for n in REPLICATED:
        size = int(np.prod(w_loc[n].shape))
        grads[n] = small_sum[off:off + size].reshape(w_loc[n].shape)
        off += size

    deltas, new_m, new_v = {}, {}, {}
    for n in WEIGHTS:
        deltas[n], new_m[n], new_v[n] = _adamw_call("adamw_" + n, w_loc[n], grads[n], m_loc[n], v_loc[n])
    return (loss, grad_x, *[grads[n] for n in WEIGHTS], *[deltas[n] for n in WEIGHTS],
            *[new_m[n] for n in WEIGHTS], *[new_v[n] for n in WEIGHTS])
```

```python
import functools

import numpy as np
import jax
import jax.numpy as jnp
from jax import lax
from jax.experimental import pallas as pl
from jax.experimental.pallas import tpu as pltpu

F32 = jnp.float32
BF16 = jnp.bfloat16
_MXU_DTYPE = jnp.bfloat16
_WIRE_DTYPE = jnp.bfloat16
_VMEM_LIMIT = 52 * 1024 * 1024
_LANES = 128
_ATT_BLOCK = 512
_ROW_TILE = 512
_SG_TILE = 1024

NORM_EPS = 1e-6
HEADS = 8
NOPE = 64
ROPE = 32
VDIM = 64
QK_DIM = NOPE + ROPE
HP = 128
Q_LORA = 384
KV_LORA = 256
SG_WIDTH = 512
SG_GROUPS = 8
SG_GDIM = 64
SG_CHUNK = 128
ROPE_THETA = 10000.0
PROJ_W = Q_LORA + KV_LORA + HP + 2 * SG_WIDTH
ADAM_LR = 0.001
ADAM_B1 = 0.9
ADAM_B2 = 0.999
ADAM_EPS = 1e-08
ADAM_WD = 0.01
ADAM_STEP = 10
MESH = pl.DeviceIdType.MESH
PACK_COLS = 1024
PACK_ROW_MULT = 256

SHARDED = ['ffn_pre_w_gate', 'ffn_pre_w_up', 'ffn_pre_w_down', 'ffn_post_w_gate', 'ffn_post_w_up',
           'ffn_post_w_down', 'even_w_in', 'w_uq', 'w_ukv', 'even_w_out', 'conv_w_in', 'conv_w', 'conv_w_out']
SHARD_AXIS = {'ffn_pre_w_gate': 2, 'ffn_pre_w_up': 2, 'ffn_pre_w_down': 1, 'ffn_post_w_gate': 2,
              'ffn_post_w_up': 2, 'ffn_post_w_down': 1, 'even_w_in': 2, 'w_uq': 2, 'w_ukv': 2,
              'even_w_out': 1, 'conv_w_in': 2, 'conv_w': 2, 'conv_w_out': 1}
FFN_WEIGHTS = ['ffn_pre_w_gate', 'ffn_pre_w_up', 'ffn_pre_w_down', 'ffn_post_w_gate', 'ffn_post_w_up',
               'ffn_post_w_down']
GROUP_A = ['ffn_pre_w_gate', 'ffn_pre_w_up', 'ffn_post_w_gate', 'ffn_post_w_up']
GROUP_B = ['ffn_pre_w_down', 'ffn_post_w_down', 'even_w_out', 'conv_w_out']
GROUP_C = ['even_w_in', 'w_uq', 'w_ukv', 'conv_w_in', 'conv_w']
REPLICATED = ['ffn_pre_norm', 'mix_norm', 'ffn_post_norm', 'q_norm', 'kv_norm', 'sg_norm', 'sg_w', 'sg_b',
              'final_norm']
WEIGHTS = ['ffn_pre_norm', 'ffn_pre_w_gate', 'ffn_pre_w_up', 'ffn_pre_w_down', 'mix_norm', 'ffn_post_norm',
           'ffn_post_w_gate', 'ffn_post_w_up', 'ffn_post_w_down', 'even_w_in', 'q_norm', 'w_uq', 'kv_norm',
           'w_ukv', 'sg_norm', 'sg_w', 'sg_b', 'even_w_out', 'conv_w_in', 'conv_w', 'conv_w_out', 'final_norm']


def _params(sem=None):
    return pltpu.CompilerParams(vmem_limit_bytes=_VMEM_LIMIT,
                                **({} if sem is None else {'dimension_semantics': sem}))


def _pick(n, pref, mult):
    best = None
    t = mult
    while t <= min(n, pref):
        if n % t == 0:
            best = t
        t += mult
    return n if best is None else best


def _mx(v):
    return v if v.dtype == _MXU_DTYPE else v.astype(_MXU_DTYPE)


def _sigmoid(a):
    return 1.0 / (1.0 + jnp.exp(-a))


def _rms_stats(x):
    rstd = lax.rsqrt(jnp.mean(x * x, axis=-1, keepdims=True) + NORM_EPS)
    return x * rstd, rstd


def _rms_bwd(x, g, dh):
    xhat, rstd = _rms_stats(x)
    gdh = g * dh
    dx = rstd * (gdh - xhat * jnp.mean(gdh * xhat, axis=-1, keepdims=True))
    return dx, dh * xhat


def _fused_matmul(name, mode, lhs, rhs, prods, n_acc, epilogue, out_dtypes, M, N, K, tm, tn, tk,
                  tile_extras=(), row_extras=(), mrow_extras=(), n_colsum=0):
    gj, gi, gk = N // tn, M // tm, K // tk
    assert gj * tn == N and gi * tm == M and gk * tk == K, (name, M, N, K, tm, tn, tk)
    dims = {'nn': (((1,), (0,)), ((), ())), 'nt': (((1,), (1,)), ((), ())), 'tn': (((0,), (0,)), ((), ()))}[mode]

    def lhs_spec(roff, coff, kb):
        kb = tk if kb is None else kb
        if mode == 'tn':
            return pl.BlockSpec((kb, tm), lambda j, i, k: (k + roff, i + coff))
        return pl.BlockSpec((tm, kb), lambda j, i, k: (i + roff, k + coff))

    def rhs_spec(roff, coff, kb):
        kb = tk if kb is None else kb
        if mode == 'nt':
            return pl.BlockSpec((tn, kb), lambda j, i, k: (j + roff, k + coff))
        return pl.BlockSpec((kb, tn), lambda j, i, k: (k + roff, j + coff))

    in_specs = [lhs_spec(*a[1:]) for a in lhs] + [rhs_spec(*a[1:]) for a in rhs]
    in_specs += [pl.BlockSpec((tm, tn), lambda j, i, k: (i, j)) for _ in tile_extras]
    in_specs += [pl.BlockSpec((1, tn), lambda j, i, k: (0, j)) for _ in row_extras]
    in_specs += [pl.BlockSpec((tm, a.shape[1]), lambda j, i, k: (i, 0)) for a in mrow_extras]
    n_out = len(out_dtypes)
    out_shape = [jax.ShapeDtypeStruct((M, N), d) for d in out_dtypes]
    out_specs = [pl.BlockSpec((tm, tn), lambda j, i, k: (i, j)) for _ in out_dtypes]
    out_shape += [jax.ShapeDtypeStruct((1, N), F32) for _ in range(n_colsum)]
    out_specs += [pl.BlockSpec((1, tn), lambda j, i, k: (0, j)) for _ in range(n_colsum)]
    scratch = [pltpu.VMEM((tm, tn), F32) for _ in range(n_acc)] if gk > 1 else []
    nl, nr, nt, nrw, nm = len(lhs), len(rhs), len(tile_extras), len(row_extras), len(mrow_extras)

    def body(*refs):
        pos = 0
        lhs_refs = refs[pos:pos + nl]; pos += nl
        rhs_refs = refs[pos:pos + nr]; pos += nr
        tile_refs = refs[pos:pos + nt]; pos += nt
        row_refs = refs[pos:pos + nrw]; pos += nrw
        mrow_refs = refs[pos:pos + nm]; pos += nm
        out_refs = refs[pos:pos + n_out]; pos += n_out
        cs_refs = refs[pos:pos + n_colsum]; pos += n_colsum
        acc_refs = refs[pos:]
        i = pl.program_id(1)
        k = pl.program_id(2)

        def partials():
            res = [None] * n_acc
            for (li, ri, ai) in prods:
                d = lax.dot_general(_mx(lhs_refs[li][...]), _mx(rhs_refs[ri][...]), dims,
                                    preferred_element_type=F32)
                res[ai] = d if res[ai] is None else res[ai] + d
            return res

        def finish(accs):
            outs = epilogue(accs, [r[...] for r in tile_refs], [r[...] for r in row_refs],
                            [r[...] for r in mrow_refs])
            for r, o in zip(out_refs, outs[:n_out]):
                r[...] = o.astype(r.dtype)
            for r, c in zip(cs_refs, outs[n_out:]):
                c = jnp.sum(c, axis=0, keepdims=True)

                @pl.when(i == 0)
                def _():
                    r[...] = c

                @pl.when(i != 0)
                def _():
                    r[...] += c

        if gk == 1:
            finish(partials())
        else:
            p = partials()

            @pl.when(k == 0)
            def _():
                for r, v in zip(acc_refs, p):
                    r[...] = v

            @pl.when(k != 0)
            def _():
                for r, v in zip(acc_refs, p):
                    r[...] += v

            @pl.when(k == gk - 1)
            def _():
                finish([r[...] for r in acc_refs])

    res = pl.pallas_call(
        body, name=name, grid=(gj, gi, gk), in_specs=in_specs, out_specs=out_specs, out_shape=out_shape,
        scratch_shapes=scratch, compiler_params=_params(("arbitrary", "arbitrary", "arbitrary")),
    )(*[a[0] for a in lhs], *[a[0] for a in rhs], *tile_extras, *row_extras, *mrow_extras)
    return res


def _op(a, roff=0, coff=0, kb=None):
    return (a, roff, coff, kb)


def _ident_epi(scale=None):
    def epi(accs, tiles, rows, mrows):
        return [a if scale is None else a * scale for a in accs]
    return epi


def _resid_norm_epi(scale):
    def epi(accs, tiles, rows, mrows):
        x_new = tiles[0] + scale * accs[0]
        xhat, _ = _rms_stats(x_new)
        return [x_new, xhat * rows[0]]
    return epi


def _norm_bwd_epi(accs, tiles, rows, mrows):
    dx_n, dg = _rms_bwd(tiles[0], rows[0], accs[0])
    dx = tiles[1] + dx_n
    return [dx, dx, dg]


def _rmsnorm_call(name, x, g):
    S, D = x.shape
    tm = _pick(S, _ROW_TILE, 8)

    def body(x_ref, g_ref, h_ref):
        xhat, _ = _rms_stats(x_ref[...])
        h_ref[...] = (xhat * g_ref[...]).astype(h_ref.dtype)

    return pl.pallas_call(
        body, name=name, grid=(S // tm,),
        in_specs=[pl.BlockSpec((tm, D), lambda i: (i, 0)), pl.BlockSpec((1, D), lambda i: (0, 0))],
        out_specs=pl.BlockSpec((tm, D), lambda i: (i, 0)),
        out_shape=jax.ShapeDtypeStruct((S, D), BF16), compiler_params=_params(("arbitrary",)),
    )(x, g)


def _loss_call(x, target, g):
    S, D = x.shape
    tm = _pick(S, _ROW_TILE, 8)

    def body(x_ref, t_ref, g_ref, dx_ref, dxb_ref, dg_ref, loss_ref):
        i = pl.program_id(0)
        x_t = x_ref[...]
        gain = g_ref[...]
        xhat, _ = _rms_stats(x_t)
        diff = xhat * gain - t_ref[...]
        dy = diff * (1.0 / D)
        dx, dg = _rms_bwd(x_t, gain, dy)
        dx_ref[...] = dx
        dxb_ref[...] = dx.astype(BF16)
        dg = jnp.sum(dg, axis=0, keepdims=True)
        part = 0.5 * jnp.sum(jnp.sum(diff * diff, axis=1, keepdims=True), axis=0, keepdims=True) * (1.0 / D)
        part = jnp.broadcast_to(part, (1, _LANES))

        @pl.when(i == 0)
        def _():
            dg_ref[...] = dg
            loss_ref[...] = part

        @pl.when(i != 0)
        def _():
            dg_ref[...] += dg
            loss_ref[...] += part

    row = lambda i: (i, 0)
    fixed = lambda i: (0, 0)
    return pl.pallas_call(
        body, name="loss_head", grid=(S // tm,),
        in_specs=[pl.BlockSpec((tm, D), row), pl.BlockSpec((tm, D), row), pl.BlockSpec((1, D), fixed)],
        out_specs=[pl.BlockSpec((tm, D), row), pl.BlockSpec((tm, D), row), pl.BlockSpec((1, D), fixed),
                   pl.BlockSpec((1, _LANES), fixed)],
        out_shape=[jax.ShapeDtypeStruct((S, D), F32), jax.ShapeDtypeStruct((S, D), BF16),
                   jax.ShapeDtypeStruct((1, D), F32), jax.ShapeDtypeStruct((1, _LANES), F32)],
        compiler_params=_params(("arbitrary",)),
    )(x, target, g)


def _rope_tables_call(pos_col, invf, mask_a, mask_b):
    S = pos_col.shape[0]
    tm = _pick(S, _ROW_TILE, 8)

    def body(p_ref, f_ref, a_ref, b_ref, cos_ref, sa_ref, sb_ref):
        ang = p_ref[...].astype(F32) * f_ref[...]
        sn = jnp.sin(ang)
        cos_ref[...] = jnp.cos(ang)
        sa_ref[...] = sn * a_ref[...]
        sb_ref[...] = sn * b_ref[...]

    row = lambda i: (i, 0)
    fixed = lambda i: (0, 0)
    return pl.pallas_call(
        body, name="rope_tables", grid=(S // tm,),
        in_specs=[pl.BlockSpec((tm, 1), row)] + [pl.BlockSpec((1, HP), fixed)] * 3,
        out_specs=[pl.BlockSpec((tm, HP), row)] * 3,
        out_shape=[jax.ShapeDtypeStruct((S, HP), F32)] * 3, compiler_params=_params(("arbitrary",)),
    )(pos_col, invf, mask_a, mask_b)


def _rope(t, cos, sa, sb):
    return t * cos + pltpu.roll(t, HP - ROPE // 2, 1) * sa + pltpu.roll(t, ROPE // 2, 1) * sb


def _rope_t(d, cos, sa, sb):
    return d * cos + pltpu.roll(d * sa, ROPE // 2, 1) + pltpu.roll(d * sb, HP - ROPE // 2, 1)


def _gelu(z):
    return 0.5 * z * (1.0 + lax.erf(z * np.float32(1.0 / np.sqrt(2.0))))


def _gelu_grad(z):
    cdf = 0.5 * (1.0 + lax.erf(z * np.float32(1.0 / np.sqrt(2.0))))
    pdf = np.float32(1.0 / np.sqrt(2.0 * np.pi)) * jnp.exp(-0.5 * z * z)
    return cdf + z * pdf


_CQ0, _CKV0, _KR0, _Z0 = 0, Q_LORA, Q_LORA + KV_LORA, Q_LORA + KV_LORA + HP


def _even_prep_call(proj, qn, kvn, sgn, cos, sa, sb):
    S = proj.shape[0]
    tm = _pick(S, 256, 8)

    def body(p_ref, qn_ref, kvn_ref, sgn_ref, cos_ref, sa_ref, sb_ref, cq_ref, ckv_ref, kr_ref, u_ref, v_ref):
        cq = p_ref[:, _CQ0:_CQ0 + Q_LORA]
        cq_ref[...] = (_rms_stats(cq)[0] * qn_ref[...]).astype(BF16)
        ckv = p_ref[:, _CKV0:_CKV0 + KV_LORA]
        ckv_ref[...] = (_rms_stats(ckv)[0] * kvn_ref[...]).astype(BF16)
        kr = p_ref[:, _KR0:_KR0 + HP]
        kr_ref[...] = _rope(kr, cos_ref[...], sa_ref[...], sb_ref[...]).astype(BF16)
        u_ref[...] = _gelu(p_ref[:, _Z0:_Z0 + SG_WIDTH]).astype(BF16)
        zv = _gelu(p_ref[:, _Z0 + SG_WIDTH:_Z0 + 2 * SG_WIDTH])
        v_ref[...] = (_rms_stats(zv)[0] * sgn_ref[...]).astype(BF16)

    row = lambda i: (i, 0)
    fixed = lambda i: (0, 0)
    widths = [Q_LORA, KV_LORA, HP, SG_WIDTH, SG_WIDTH]
    return pl.pallas_call(
        body, name="even_prep", grid=(S // tm,),
        in_specs=[pl.BlockSpec((tm, PROJ_W), row), pl.BlockSpec((1, Q_LORA), fixed),
                  pl.BlockSpec((1, KV_LORA), fixed), pl.BlockSpec((1, SG_WIDTH), fixed)]
        + [pl.BlockSpec((tm, HP), row)] * 3,
        out_specs=[pl.BlockSpec((tm, w), row) for w in widths],
        out_shape=[jax.ShapeDtypeStruct((S, w), BF16) for w in widths],
        compiler_params=_params(("arbitrary",)),
    )(proj, qn, kvn, sgn, cos, sa, sb)


def _even_prep_bwd_call(proj, qn, kvn, sgn, cos, sa, sb, dcqn, dckvn, dk, du, dvn):
    S = proj.shape[0]
    tm = _pick(S, 256, 8)

    def body(p_ref, qn_ref, kvn_ref, sgn_ref, cos_ref, sa_ref, sb_ref, dcq_ref, dckv_ref, dk_ref, du_ref,
             dvn_ref, dp_ref, dqn_ref, dkvn_ref, dsgn_ref):
        i = pl.program_id(0)
        dcq, gq = _rms_bwd(p_ref[:, _CQ0:_CQ0 + Q_LORA], qn_ref[...], dcq_ref[...])
        dp_ref[:, _CQ0:_CQ0 + Q_LORA] = dcq.astype(BF16)
        dckv, gkv = _rms_bwd(p_ref[:, _CKV0:_CKV0 + KV_LORA], kvn_ref[...], dckv_ref[...])
        dp_ref[:, _CKV0:_CKV0 + KV_LORA] = dckv.astype(BF16)
        dkr = dk_ref[:, 0:HP].astype(F32)
        for h in range(1, HEADS):
            dkr = dkr + dk_ref[:, h * HP:(h + 1) * HP].astype(F32)
        lane = lax.broadcasted_iota(jnp.int32, dkr.shape, 1)
        dkr = jnp.where((lane >= NOPE) & (lane < QK_DIM), dkr, 0.0)
        dp_ref[:, _KR0:_KR0 + HP] = _rope_t(dkr, cos_ref[...], sa_ref[...], sb_ref[...]).astype(BF16)
        zu = p_ref[:, _Z0:_Z0 + SG_WIDTH]
        dp_ref[:, _Z0:_Z0 + SG_WIDTH] = (du_ref[...].astype(F32) * _gelu_grad(zu)).astype(BF16)
        zv = p_ref[:, _Z0 + SG_WIDTH:_Z0 + 2 * SG_WIDTH]
        dgv, gsg = _rms_bwd(_gelu(zv), sgn_ref[...], dvn_ref[...].astype(F32))
        dp_ref[:, _Z0 + SG_WIDTH:_Z0 + 2 * SG_WIDTH] = (dgv * _gelu_grad(zv)).astype(BF16)
        sums = [jnp.sum(t, axis=0, keepdims=True) for t in (gq, gkv, gsg)]

        @pl.when(i == 0)
        def _():
            for r, s in zip((dqn_ref, dkvn_ref, dsgn_ref), sums):
                r[...] = s

        @pl.when(i != 0)
        def _():
            for r, s in zip((dqn_ref, dkvn_ref, dsgn_ref), sums):
                r[...] += s

    row = lambda i: (i, 0)
    fixed = lambda i: (0, 0)
    return pl.pallas_call(
        body, name="even_prep_bwd", grid=(S // tm,),
        in_specs=[pl.BlockSpec((tm, PROJ_W), row), pl.BlockSpec((1, Q_LORA), fixed),
                  pl.BlockSpec((1, KV_LORA), fixed), pl.BlockSpec((1, SG_WIDTH), fixed)]
        + [pl.BlockSpec((tm, HP), row)] * 3
        + [pl.BlockSpec((tm, Q_LORA), row), pl.BlockSpec((tm, KV_LORA), row),
           pl.BlockSpec((tm, HEADS * HP), row), pl.BlockSpec((tm, SG_WIDTH), row),
           pl.BlockSpec((tm, SG_WIDTH), row)],
        out_specs=[pl.BlockSpec((tm, PROJ_W), row), pl.BlockSpec((1, Q_LORA), fixed),
                   pl.BlockSpec((1, KV_LORA), fixed), pl.BlockSpec((1, SG_WIDTH), fixed)],
        out_shape=[jax.ShapeDtypeStruct((S, PROJ_W), BF16), jax.ShapeDtypeStruct((1, Q_LORA), F32),
                   jax.ShapeDtypeStruct((1, KV_LORA), F32), jax.ShapeDtypeStruct((1, SG_WIDTH), F32)],
        compiler_params=_params(("arbitrary",)),
    )(proj, qn, kvn, sgn, cos, sa, sb, dcqn, dckvn, dk, du, dvn)


def _causal_mask(rows, cols):
    r = lax.broadcasted_iota(jnp.int32, (rows, cols), 0)
    c = lax.broadcasted_iota(jnp.int32, (rows, cols), 1)
    return c <= r


def _flash_fwd_call(q, k, v):
    S = q.shape[0]
    tb = _pick(S, _ATT_BLOCK, 128)
    nq = S // tb
    nt_dims = (((1,), (1,)), ((), ()))

    def body(q_ref, k_ref, v_ref, o_ref, lse_ref):
        i = pl.program_id(1)
        q_t = q_ref[...]

        def step(j, carry, masked):
            m, l, acc = carry
            off = pl.multiple_of(j * tb, tb)
            k_t = k_ref[pl.ds(off, tb), :]
            v_t = v_ref[pl.ds(off, tb), :]
            s = lax.dot_general(q_t, k_t, nt_dims, preferred_element_type=F32)
            if masked:
                s = jnp.where(_causal_mask(tb, tb), s, -1e30)
            m_new = jnp.maximum(m, jnp.max(s, axis=1, keepdims=True))
            alpha = jnp.exp(m - m_new)
            p = jnp.exp(s - m_new)
            l = alpha * l + jnp.sum(p, axis=1, keepdims=True)
            acc = alpha * acc + jnp.dot(p.astype(v_t.dtype), v_t, preferred_element_type=F32)
            return m_new, l, acc

        init = (jnp.full((tb, 1), -1e30, F32), jnp.zeros((tb, 1), F32), jnp.zeros((tb, HP), F32))
        carry = lax.fori_loop(0, i, lambda j, c: step(j, c, False), init)
        m, l, acc = step(i, carry, True)
        o_ref[...] = (acc / l).astype(o_ref.dtype)
        lse = jnp.broadcast_to(m + jnp.log(l), (tb, HP))
        lse_ref[0, 0] = jnp.transpose(lse)[0:8, :]

    return pl.pallas_call(
        body, name="flash_fwd", grid=(HEADS, nq),
        in_specs=[pl.BlockSpec((tb, HP), lambda h, i: (i, h)), pl.BlockSpec((S, HP), lambda h, i: (0, h)),
                  pl.BlockSpec((S, HP), lambda h, i: (0, h))],
        out_specs=[pl.BlockSpec((tb, HP), lambda h, i: (i, h)),
                   pl.BlockSpec((1, 1, 8, tb), lambda h, i: (h, i, 0, 0))],
        out_shape=[jax.ShapeDtypeStruct((S, HEADS * HP), q.dtype), jax.ShapeDtypeStruct((HEADS, nq, 8, tb), F32)],
        compiler_params=_params(("arbitrary", "arbitrary")),
    )(q, k, v)


def _attn_delta_call(o, do):
    S = o.shape[0]
    tb = _pick(S, _ATT_BLOCK, 128)

    def body(o_ref, do_ref, d_ref):
        d = jnp.sum(o_ref[...].astype(F32) * do_ref[...].astype(F32), axis=1, keepdims=True)
        d_ref[0, 0] = jnp.transpose(jnp.broadcast_to(d, (tb, HP)))[0:8, :]

    return pl.pallas_call(
        body, name="attn_delta", grid=(HEADS, S // tb),
        in_specs=[pl.BlockSpec((tb, HP), lambda h, i: (i, h))] * 2,
        out_specs=pl.BlockSpec((1, 1, 8, tb), lambda h, i: (h, i, 0, 0)),
        out_shape=jax.ShapeDtypeStruct((HEADS, S // tb, 8, tb), F32), compiler_params=_params(("arbitrary", "arbitrary")),
    )(o, do)


def _flash_bwd_call(q, k, v, do, lse, delta):
    S = q.shape[0]
    tb = _pick(S, _ATT_BLOCK, 128)
    nq = S // tb
    nt_dims = (((1,), (1,)), ((), ()))
    tn_dims = (((0,), (0,)), ((), ()))

    def body(q_ref, do_ref, lse_ref, dl_ref, k_ref, v_ref, dq_ref, dk_ref, dv_ref):
        j = pl.program_id(1)
        k_t = k_ref[...]
        v_t = v_ref[...]

        @pl.when(j == 0)
        def _():
            dq_ref[...] = jnp.zeros_like(dq_ref)

        def step(i, carry, masked):
            dk, dv = carry
            off = pl.multiple_of(i * tb, tb)
            q_t = q_ref[pl.ds(off, tb), :]
            do_t = do_ref[pl.ds(off, tb), :]
            lse_row = lse_ref[0, i, 0:1, :]
            dl_row = dl_ref[0, i, 0:1, :]
            st = lax.dot_general(k_t, q_t, nt_dims, preferred_element_type=F32)
            pt = jnp.exp(st - lse_row)
            if masked:
                pt = jnp.where(jnp.transpose(_causal_mask(tb, tb)), pt, 0.0)
            dpt = lax.dot_general(v_t, do_t, nt_dims, preferred_element_type=F32)
            dst = (pt * (dpt - dl_row)).astype(q_t.dtype)
            dv = dv + jnp.dot(pt.astype(do_t.dtype), do_t, preferred_element_type=F32)
            dk = dk + jnp.dot(dst, q_t, preferred_element_type=F32)
            dq_ref[pl.ds(off, tb), :] += lax.dot_general(dst, k_t, tn_dims, preferred_element_type=F32)
            return dk, dv

        zero = jnp.zeros((tb, HP), F32)
        carry = step(j, (zero, zero), True)
        dk, dv = lax.fori_loop(j + 1, nq, lambda i, c: step(i, c, False), carry)
        dk_ref[...] = dk.astype(dk_ref.dtype)
        dv_ref[...] = dv.astype(dv_ref.dtype)

    head = lambda h, j: (0, h)
    blk = lambda h, j: (j, h)
    rows = lambda h, j: (h, 0, 0, 0)
    return pl.pallas_call(
        body, name="flash_bwd", grid=(HEADS, nq),
        in_specs=[pl.BlockSpec((S, HP), head), pl.BlockSpec((S, HP), head), pl.BlockSpec((1, nq, 8, tb), rows),
                  pl.BlockSpec((1, nq, 8, tb), rows), pl.BlockSpec((tb, HP), blk), pl.BlockSpec((tb, HP), blk)],
        out_specs=[pl.BlockSpec((S, HP), head), pl.BlockSpec((tb, HP), blk), pl.BlockSpec((tb, HP), blk)],
        out_shape=[jax.ShapeDtypeStruct((S, HEADS * HP), F32), jax.ShapeDtypeStruct((S, HEADS * HP), BF16),
                   jax.ShapeDtypeStruct((S, HEADS * HP), BF16)],
        compiler_params=_params(("arbitrary", "arbitrary")),
    )(q, do, lse, delta, k, v)


def _sg_mixed(w_ref, vch, lane_lo):
    blocks = []
    for jb in range(SG_WIDTH // _LANES):
        r = jnp.dot(w_ref[jb], vch[:, jb * _LANES:(jb + 1) * _LANES], preferred_element_type=F32)
        blocks.append(jnp.where(lane_lo, r[0:SG_CHUNK], r[SG_CHUNK:2 * SG_CHUNK]))
    return jnp.concatenate(blocks, axis=1)


def _sgu_fwd_call(vn, u, attn, wst, bexp):
    S = vn.shape[0]
    tm = _pick(S, _SG_TILE, SG_CHUNK)
    AW = HEADS * HP

    def body(v_ref, u_ref, a_ref, w_ref, b_ref, mix_ref):
        lane_lo = lax.broadcasted_iota(jnp.int32, (SG_CHUNK, _LANES), 1) < SG_GDIM
        mix_ref[:, 0:AW] = a_ref[...]
        for c in range(tm // SG_CHUNK):
            rs = slice(c * SG_CHUNK, (c + 1) * SG_CHUNK)
            mixed = _sg_mixed(w_ref, v_ref[rs, :], lane_lo) + b_ref[...]
            mix_ref[rs, AW:AW + SG_WIDTH] = (u_ref[rs, :].astype(F32) * mixed).astype(mix_ref.dtype)

    row = lambda i: (i, 0)
    return pl.pallas_call(
        body, name="sgu_fwd", grid=(S // tm,),
        in_specs=[pl.BlockSpec((tm, SG_WIDTH), row), pl.BlockSpec((tm, SG_WIDTH), row), pl.BlockSpec((tm, AW), row),
                  pl.BlockSpec((SG_WIDTH // _LANES, 2 * SG_CHUNK, SG_CHUNK), lambda i: (0, 0, 0)),
                  pl.BlockSpec((SG_CHUNK, SG_WIDTH), lambda i: (0, 0))],
        out_specs=pl.BlockSpec((tm, AW + SG_WIDTH), row),
        out_shape=jax.ShapeDtypeStruct((S, AW + SG_WIDTH), BF16), compiler_params=_params(("arbitrary",)),
    )(vn, u, attn, wst, bexp)


def _sgu_bwd_call(dmix, vn, u, wst, wst_t, bexp):
    S = vn.shape[0]
    tm = _pick(S, _SG_TILE, SG_CHUNK)
    nblk = SG_WIDTH // _LANES
    col0 = (HEADS * HP) // SG_WIDTH
    nt_dims = (((1,), (1,)), ((), ()))

    def body(d_ref, v_ref, u_ref, w_ref, wt_ref, b_ref, du_ref, dv_ref, dw_ref, db_ref, dbacc_ref):
        i = pl.program_id(0)
        lane_lo = lax.broadcasted_iota(jnp.int32, (SG_CHUNK, _LANES), 1) < SG_GDIM

        @pl.when(i == 0)
        def _():
            dw_ref[...] = jnp.zeros_like(dw_ref)
            dbacc_ref[...] = jnp.zeros_like(dbacc_ref)

        for c in range(tm // SG_CHUNK):
            rs = slice(c * SG_CHUNK, (c + 1) * SG_CHUNK)
            vch = v_ref[rs, :]
            dsg = d_ref[rs, :].astype(F32)
            mixed = _sg_mixed(w_ref, vch, lane_lo) + b_ref[...]
            du_ref[rs, :] = (dsg * mixed).astype(du_ref.dtype)
            dmixed = dsg * u_ref[rs, :].astype(F32)
            dbacc_ref[...] += dmixed
            dmx = dmixed.astype(vch.dtype)
            dv_ref[rs, :] = _sg_mixed(wt_ref, dmx, lane_lo).astype(dv_ref.dtype)
            for jb in range(nblk):
                dblk = dmx[:, jb * _LANES:(jb + 1) * _LANES]
                vblk = vch[:, jb * _LANES:(jb + 1) * _LANES]
                zero = jnp.zeros_like(dblk)
                dw_ref[2 * jb] += lax.dot_general(jnp.where(lane_lo, dblk, zero), vblk, nt_dims,
                                                  preferred_element_type=F32)
                dw_ref[2 * jb + 1] += lax.dot_general(jnp.where(lane_lo, zero, dblk), vblk, nt_dims,
                                                      preferred_element_type=F32)

        @pl.when(i == pl.num_programs(0) - 1)
        def _():
            tri = _causal_mask(SG_CHUNK, SG_CHUNK)
            for g in range(SG_GROUPS):
                dw_ref[g] = jnp.where(tri, dw_ref[g], 0.0)
            lane = lax.broadcasted_iota(jnp.int32, (SG_CHUNK, _LANES), 1)
            out = jnp.zeros((SG_CHUNK, _LANES), F32)
            for g in range(SG_GROUPS):
                blk = dbacc_ref[:, (g // 2) * _LANES:(g // 2 + 1) * _LANES]
                sel = lane_lo if g % 2 == 0 else jnp.logical_not(lane_lo)
                s = jnp.sum(jnp.where(sel, blk, 0.0), axis=1, keepdims=True)
                out = jnp.where(lane == g, s, out)
            db_ref[...] = out

    row = lambda i: (i, 0)
    wspec = pl.BlockSpec((nblk, 2 * SG_CHUNK, SG_CHUNK), lambda i: (0, 0, 0))
    return pl.pallas_call(
        body, name="sgu_bwd", grid=(S // tm,),
        in_specs=[pl.BlockSpec((tm, SG_WIDTH), lambda i: (i, col0)), pl.BlockSpec((tm, SG_WIDTH), row),
                  pl.BlockSpec((tm, SG_WIDTH), row), wspec, wspec,
                  pl.BlockSpec((SG_CHUNK, SG_WIDTH), lambda i: (0, 0))],
        out_specs=[pl.BlockSpec((tm, SG_WIDTH), row), pl.BlockSpec((tm, SG_WIDTH), row),
                   pl.BlockSpec((SG_GROUPS, SG_CHUNK, SG_CHUNK), lambda i: (0, 0, 0)),
                   pl.BlockSpec((SG_CHUNK, _LANES), lambda i: (0, 0))],
        out_shape=[jax.ShapeDtypeStruct((S, SG_WIDTH), BF16), jax.ShapeDtypeStruct((S, SG_WIDTH), BF16),
                   jax.ShapeDtypeStruct((SG_GROUPS, SG_CHUNK, SG_CHUNK), F32),
                   jax.ShapeDtypeStruct((SG_CHUNK, _LANES), F32)],
        scratch_shapes=[pltpu.VMEM((SG_CHUNK, SG_WIDTH), F32)],
        compiler_params=_params(("arbitrary",)),
    )(dmix, vn, u, wst, wst_t, bexp)


def _shift_down(t, halo, n):
    rows = lax.broadcasted_iota(jnp.int32, t.shape, 0)
    out = pltpu.roll(t, n, 0)
    for r in range(n):
        out = jnp.where(rows == r, halo[8 - n + r:8 - n + r + 1, :], out)
    return out


def _shift_up(t, halo, n):
    tm = t.shape[0]
    rows = lax.broadcasted_iota(jnp.int32, t.shape, 0)
    out = pltpu.roll(t, tm - n, 0)
    for r in range(n):
        out = jnp.where(rows == tm - n + r, halo[r:r + 1, :], out)
    return out


def _conv_fwd_call(p, w):
    S, C3 = p.shape
    C = C3 // 3
    tm = _pick(S, _ROW_TILE, 8)
    hb = tm // 8

    def body(p_ref, c_prev, z_prev, w_ref, m_ref):
        i = pl.program_id(0)
        cz = p_ref[:, C:2 * C] * p_ref[:, 2 * C:3 * C]
        czp = jnp.where(i > 0, c_prev[...] * z_prev[...], 0.0)
        y = w_ref[2:3, :] * cz + w_ref[1:2, :] * _shift_down(cz, czp, 1) + w_ref[0:1, :] * _shift_down(cz, czp, 2)
        m_ref[...] = (p_ref[:, 0:C] * y).astype(m_ref.dtype)

    prev = lambda col: (lambda i: (jnp.maximum(i * hb - 1, 0), col))
    return pl.pallas_call(
        body, name="conv_fwd", grid=(S // tm,),
        in_specs=[pl.BlockSpec((tm, C3), lambda i: (i, 0)), pl.BlockSpec((8, C), prev(1)),
                  pl.BlockSpec((8, C), prev(2)), pl.BlockSpec((3, C), lambda i: (0, 0))],
        out_specs=pl.BlockSpec((tm, C), lambda i: (i, 0)),
        out_shape=jax.ShapeDtypeStruct((S, C), BF16), compiler_params=_params(("arbitrary",)),
    )(p, p, p, w)


def _conv_bwd_call(p, w, dm):
    S, C3 = p.shape
    C = C3 // 3
    tm = _pick(S, 256, 8)
    hb = tm // 8
    n_tiles = S // tm

    def body(p_ref, c_prev, z_prev, b_next, dm_ref, dm_next, w_ref, dp_ref, dw_ref):
        i = pl.program_id(0)
        b = p_ref[:, 0:C]
        c = p_ref[:, C:2 * C]
        z = p_ref[:, 2 * C:3 * C]
        cz = c * z
        czp = jnp.where(i > 0, c_prev[...] * z_prev[...], 0.0)
        s1 = _shift_down(cz, czp, 1)
        s2 = _shift_down(cz, czp, 2)
        w0, w1, w2 = w_ref[0:1, :], w_ref[1:2, :], w_ref[2:3, :]
        y = w2 * cz + w1 * s1 + w0 * s2
        dm_t = dm_ref[...]
        dy = dm_t * b
        dyn = jnp.where(i < n_tiles - 1, dm_next[...] * b_next[...], 0.0)
        dcz = w2 * dy + w1 * _shift_up(dy, dyn, 1) + w0 * _shift_up(dy, dyn, 2)
        dp_ref[:, 0:C] = (dm_t * y).astype(dp_ref.dtype)
        dp_ref[:, C:2 * C] = (dcz * z).astype(dp_ref.dtype)
        dp_ref[:, 2 * C:3 * C] = (dcz * c).astype(dp_ref.dtype)
        dw = jnp.concatenate([jnp.sum(dy * s2, axis=0, keepdims=True), jnp.sum(dy * s1, axis=0, keepdims=True),
                              jnp.sum(dy * cz, axis=0, keepdims=True)], axis=0)

        @pl.when(i == 0)
        def _():
            dw_ref[...] = dw

        @pl.when(i != 0)
        def _():
            dw_ref[...] += dw

    prev = lambda col: (lambda i: (jnp.maximum(i * hb - 1, 0), col))
    nxt = lambda col: (lambda i: (jnp.minimum((i + 1) * hb, S // 8 - 1), col))
    return pl.pallas_call(
        body, name="conv_bwd", grid=(n_tiles,),
        in_specs=[pl.BlockSpec((tm, C3), lambda i: (i, 0)), pl.BlockSpec((8, C), prev(1)),
                  pl.BlockSpec((8, C), prev(2)), pl.BlockSpec((8, C), nxt(0)),
                  pl.BlockSpec((tm, C), lambda i: (i, 0)), pl.BlockSpec((8, C), nxt(0)),
                  pl.BlockSpec((3, C), lambda i: (0, 0))],
        out_specs=[pl.BlockSpec((tm, C3), lambda i: (i, 0)), pl.BlockSpec((3, C), lambda i: (0, 0))],
        out_shape=[jax.ShapeDtypeStruct((S, C3), BF16), jax.ShapeDtypeStruct((3, C), F32)],
        compiler_params=_params(("arbitrary",)),
    )(p, p, p, p, dm, dm, w)


def _my_place():
    return lax.axis_index("x"), lax.axis_index("y"), lax.axis_index("c")


def _gather_weights_call(name, shard):
    R, C = shard.shape

    def body(s_ref, o_ref, send_sems, recv_sems, local_sem):
        x, y, c = _my_place()
        mine = 2 * x + y
        local = pltpu.make_async_copy(s_ref, o_ref.at[mine], local_sem)
        local.start()
        peers = [(1 - x, y), (x, 1 - y), (1 - x, 1 - y)]
        copies = []
        for k, (px, py) in enumerate(peers):
            cp = pltpu.make_async_remote_copy(src_ref=s_ref, dst_ref=o_ref.at[mine], send_sem=send_sems.at[k],
                                              recv_sem=recv_sems.at[k], device_id=(px, py, c), device_id_type=MESH)
            cp.start()
            copies.append(cp)
        for k, (px, py) in enumerate(peers):
            pltpu.make_async_remote_copy(src_ref=s_ref, dst_ref=o_ref.at[2 * px + py], send_sem=send_sems.at[k],
                                         recv_sem=recv_sems.at[k], device_id=(px, py, c),
                                         device_id_type=MESH).wait_recv()
        for cp in copies:
            cp.wait_send()
        local.wait()

    any_spec = pl.BlockSpec(memory_space=pl.ANY)
    return pl.pallas_call(
        body, name=name, in_specs=[any_spec], out_specs=any_spec,
        out_shape=jax.ShapeDtypeStruct((4, R, C), shard.dtype),
        scratch_shapes=[pltpu.SemaphoreType.DMA((3,)), pltpu.SemaphoreType.DMA((3,)), pltpu.SemaphoreType.DMA],
        compiler_params=pltpu.CompilerParams(has_side_effects=True),
    )(shard)


_D2D_CHUNKS = 4


_LOCAL_CHUNKS = 8


def _local_copies(src_of, dst_of, rows, sems, base):
    rc = rows // _LOCAL_CHUNKS
    assert rc * _LOCAL_CHUNKS == rows and rc % 16 == 0, rows
    out = []
    for j in range(_LOCAL_CHUNKS):
        sl = pl.ds(j * rc, rc)
        out.append(pltpu.make_async_copy(src_of(sl), dst_of(sl), sems.at[base + j]))
    return out


def _comm_call(name, body, arrays, out_shapes, sem_counts):
    any_spec = pl.BlockSpec(memory_space=pl.ANY)
    return pl.pallas_call(
        body, name=name, in_specs=[any_spec] * len(arrays), out_specs=[any_spec] * len(out_shapes),
        out_shape=out_shapes, scratch_shapes=[pltpu.SemaphoreType.DMA((n,)) for n in sem_counts],
        compiler_params=pltpu.CompilerParams(has_side_effects=True),
    )(*arrays)


def _gather_halves_call(shards):
    na = len(shards)
    for s in shards:
        assert s.shape[0] % (2 * _D2D_CHUNKS * 16) == 0, s.shape

    def body(*refs):
        s_refs, o_refs = refs[:na], refs[na:2 * na]
        ici_send, ici_recv, d2d_send, d2d_recv, local_sems = refs[2 * na:]
        x, y, c = _my_place()
        mine = 2 * x + y
        chips = [(1 - x, y), (x, 1 - y), (1 - x, 1 - y)]
        pending = []
        for a in range(na):
            s_ref, o_ref = s_refs[a], o_refs[a]
            pending += _local_copies(lambda sl: s_ref.at[sl], lambda sl: o_ref.at[mine, sl], s_ref.shape[0],
                                     local_sems, a * _LOCAL_CHUNKS)
        for t in pending:
            t.start()

        def ici(a, k, chip, block):
            Rh = s_refs[a].shape[0] // 2
            my_half = pl.ds(pl.multiple_of(c * Rh, 16), Rh)
            return pltpu.make_async_remote_copy(src_ref=s_refs[a].at[my_half], dst_ref=o_refs[a].at[block, my_half],
                                                send_sem=ici_send.at[3 * a + k], recv_sem=ici_recv.at[3 * a + k],
                                                device_id=(chip[0], chip[1], c), device_id_type=MESH)

        def d2d(a, k, j, block, half):
            Rh = s_refs[a].shape[0] // 2
            rc = Rh // _D2D_CHUNKS
            rows = pl.ds(pl.multiple_of(half * Rh + j * rc, 16), rc)
            idx = (3 * a + k) * _D2D_CHUNKS + j
            return pltpu.make_async_remote_copy(src_ref=o_refs[a].at[block, rows], dst_ref=o_refs[a].at[block, rows],
                                                send_sem=d2d_send.at[idx], recv_sem=d2d_recv.at[idx],
                                                device_id=(x, y, 1 - c), device_id_type=MESH)

        sends = [ici(a, k, chip, mine) for a in range(na) for k, chip in enumerate(chips)]
        for cp in sends:
            cp.start()
        for a in range(na):
            for k, chip in enumerate(chips):
                block = 2 * chip[0] + chip[1]
                ici(a, k, chip, block).wait_recv()
                for j in range(_D2D_CHUNKS):
                    cp = d2d(a, k, j, block, c)
                    cp.start()
                    sends.append(cp)
        for a in range(na):
            for k, chip in enumerate(chips):
                for j in range(_D2D_CHUNKS):
                    d2d(a, k, j, 2 * chip[0] + chip[1], 1 - c).wait_recv()
        for cp in sends:
            cp.wait_send()
        for t in pending:
            t.wait()

    outs = [jax.ShapeDtypeStruct((4,) + tuple(s.shape), s.dtype) for s in shards]
    n_d2d = 3 * na * _D2D_CHUNKS
    return _comm_call("gather_weights", body, shards, outs, [3 * na, 3 * na, n_d2d, n_d2d, na * _LOCAL_CHUNKS])


def _pair_exchange_call(packed):
    na = len(packed)

    def body(*refs):
        p_refs, o_refs = refs[:na], refs[na:2 * na]
        send_sems, recv_sems = refs[2 * na:]
        x, y, c = _my_place()
        copies = []
        for a in range(na):
            nb, _, Rh, _ = p_refs[a].shape
            rc = Rh // _D2D_CHUNKS
            assert rc * _D2D_CHUNKS == Rh and rc % 16 == 0
            for b in range(nb):
                for j in range(_D2D_CHUNKS):
                    rows = pl.ds(j * rc, rc)
                    idx = (a * nb + b) * _D2D_CHUNKS + j
                    copies.append(pltpu.make_async_remote_copy(
                        src_ref=p_refs[a].at[b, 1 - c, rows], dst_ref=o_refs[a].at[b, rows],
                        send_sem=send_sems.at[idx], recv_sem=recv_sems.at[idx],
                        device_id=(x, y, 1 - c), device_id_type=MESH))
        for t in copies:
            t.start()
        for t in copies:
            t.wait_recv()
        for t in copies:
            t.wait_send()

    outs = [jax.ShapeDtypeStruct((p.shape[0], p.shape[2], p.shape[3]), p.dtype) for p in packed]
    n = sum(p.shape[0] for p in packed) * _D2D_CHUNKS
    return _comm_call("pair_exchange", body, packed, outs, [n, n])


def _pair_add_call(name, packed, other, core):
    nb, _, Rh, C = packed.shape
    tr = _pick(Rh, 512, 16)

    def body(c_ref, p_ref, o_ref, q_ref):
        q_ref[...] = (p_ref[...].astype(F32) + o_ref[...].astype(F32)).astype(q_ref.dtype)

    grid_spec = pltpu.PrefetchScalarGridSpec(
        num_scalar_prefetch=1, grid=(nb, Rh // tr),
        in_specs=[pl.BlockSpec((None, None, tr, C), lambda b, r, c_ref: (b, c_ref[0], r, 0)),
                  pl.BlockSpec((None, tr, C), lambda b, r, c_ref: (b, r, 0))],
        out_specs=pl.BlockSpec((None, tr, C), lambda b, r, c_ref: (b, r, 0)))
    return pl.pallas_call(
        body, name=name, grid_spec=grid_spec, out_shape=jax.ShapeDtypeStruct((nb, Rh, C), packed.dtype),
        compiler_params=_params(("arbitrary", "arbitrary")),
    )(core, packed, other)


def _chip_scatter_call(pairs):
    na = len(pairs)

    def body(*refs):
        p_refs, o_refs = refs[:na], refs[na:2 * na]
        send_sems, recv_sems, local_sems = refs[2 * na:]
        x, y, c = _my_place()
        mine = 2 * x + y
        chips = [(1 - x, y), (x, 1 - y), (1 - x, 1 - y)]
        pending = []
        for a in range(na):
            p_ref, o_ref = p_refs[a], o_refs[a]
            pending += _local_copies(lambda sl: p_ref.at[mine, sl], lambda sl: o_ref.at[mine, sl], p_ref.shape[1],
                                     local_sems, a * _LOCAL_CHUNKS)
        for t in pending:
            t.start()
        copies = []
        for a in range(na):
            for k, (px, py) in enumerate(chips):
                t = pltpu.make_async_remote_copy(src_ref=p_refs[a].at[2 * px + py], dst_ref=o_refs[a].at[mine],
                                                 send_sem=send_sems.at[3 * a + k], recv_sem=recv_sems.at[3 * a + k],
                                                 device_id=(px, py, c), device_id_type=MESH)
                t.start()
                copies.append(t)
        for a in range(na):
            for k, (px, py) in enumerate(chips):
                pltpu.make_async_remote_copy(src_ref=p_refs[a].at[mine], dst_ref=o_refs[a].at[2 * px + py],
                                             send_sem=send_sems.at[3 * a + k], recv_sem=recv_sems.at[3 * a + k],
                                             device_id=(px, py, c), device_id_type=MESH).wait_recv()
        for t in copies:
            t.wait_send()
        for t in pending:
            t.wait()

    outs = [jax.ShapeDtypeStruct(p.shape, p.dtype) for p in pairs]
    return _comm_call("chip_scatter", body, pairs, outs, [3 * na, 3 * na, na * _LOCAL_CHUNKS])


def _sum_slots_call(name, parts):
    n, R, C = parts.shape
    tr = _pick(R, 256, 8)

    def body(p_ref, o_ref):
        acc = p_ref[0].astype(F32)
        for s in range(1, n):
            acc = acc + p_ref[s].astype(F32)
        o_ref[...] = acc

    return pl.pallas_call(
        body, name=name, grid=(R // tr,),
        in_specs=[pl.BlockSpec((n, tr, C), lambda i: (0, i, 0))], out_specs=pl.BlockSpec((tr, C), lambda i: (i, 0)),
        out_shape=jax.ShapeDtypeStruct((R, C), F32), compiler_params=_params(("arbitrary",)),
    )(parts)


def _sibling_share_call(halves):
    na = len(halves)
    nch = 2 * _D2D_CHUNKS

    def body(*refs):
        h_refs, o_refs = refs[:na], refs[na:2 * na]
        send_sems, recv_sems, local_sems = refs[2 * na:]
        x, y, c = _my_place()
        pending = []
        for a in range(na):
            h_ref, o_ref = h_refs[a], o_refs[a]
            pending += _local_copies(lambda sl: h_ref.at[sl], lambda sl: o_ref.at[c, sl], h_ref.shape[0],
                                     local_sems, a * _LOCAL_CHUNKS)
        for t in pending:
            t.start()

        def cp(a, j, slot):
            rc = h_refs[a].shape[0] // nch
            rows = pl.ds(j * rc, rc)
            return pltpu.make_async_remote_copy(src_ref=h_refs[a].at[rows], dst_ref=o_refs[a].at[slot, rows],
                                                send_sem=send_sems.at[a * nch + j], recv_sem=recv_sems.at[a * nch + j],
                                                device_id=(x, y, 1 - c), device_id_type=MESH)

        copies = [cp(a, j, c) for a in range(na) for j in range(nch)]
        for t in copies:
            t.start()
        for a in range(na):
            for j in range(nch):
                cp(a, j, 1 - c).wait_recv()
        for t in copies:
            t.wait_send()
        for t in pending:
            t.wait()

    for h in halves:
        assert h.shape[0] % (nch * 8) == 0, h.shape
    outs = [jax.ShapeDtypeStruct((2,) + tuple(h.shape), h.dtype) for h in halves]
    return _comm_call("sibling_share", body, halves, outs, [na * nch, na * nch, na * _LOCAL_CHUNKS])


def _allreduce_small_call(part):
    R, C = part.shape

    def body(p_ref, o_ref, slots, send_sems, recv_sems):
        x, y, c = _my_place()
        me = 4 * x + 2 * y + c
        peers = []
        for k in range(1, 8):
            px = x ^ (k >> 2) if (k >> 2) else x
            py = y ^ ((k >> 1) & 1) if ((k >> 1) & 1) else y
            pc = c ^ (k & 1) if (k & 1) else c
            peers.append((px, py, pc))
        copies = []
        for k, (px, py, pc) in enumerate(peers):
            cp = pltpu.make_async_remote_copy(src_ref=p_ref, dst_ref=slots.at[me], send_sem=send_sems.at[k],
                                              recv_sem=recv_sems.at[k], device_id=(px, py, pc), device_id_type=MESH)
            cp.start()
            copies.append(cp)
        slots[me] = p_ref[...]
        for k, (px, py, pc) in enumerate(peers):
            pltpu.make_async_remote_copy(src_ref=p_ref, dst_ref=slots.at[4 * px + 2 * py + pc],
                                         send_sem=send_sems.at[k], recv_sem=recv_sems.at[k],
                                         device_id=(px, py, pc), device_id_type=MESH).wait_recv()
        for cp in copies:
            cp.wait_send()
        acc = slots[0]
        for s in range(1, 8):
            acc = acc + slots[s]
        o_ref[...] = acc

    vm = pl.BlockSpec(memory_space=pltpu.VMEM)
    return pl.pallas_call(
        body, name="allreduce_small", in_specs=[vm], out_specs=vm,
        out_shape=jax.ShapeDtypeStruct((R, C), F32),
        scratch_shapes=[pltpu.VMEM((8, R, C), F32), pltpu.SemaphoreType.DMA((7,)), pltpu.SemaphoreType.DMA((7,))],
        compiler_params=pltpu.CompilerParams(has_side_effects=True, vmem_limit_bytes=_VMEM_LIMIT),
    )(part)


def _adamw_call(name, w, g, m, v):
    shape = w.shape
    cols = shape[-1]
    rows = int(np.prod(shape[:-1])) if len(shape) > 1 else 1
    w2, g2, m2, v2 = (t.reshape(rows, cols) for t in (w, g, m, v))
    tr = _pick(rows, 256, 8)
    c1 = 1.0 / (1.0 - ADAM_B1 ** ADAM_STEP)
    c2 = 1.0 / (1.0 - ADAM_B2 ** ADAM_STEP)

    def body(w_ref, g_ref, m_ref, v_ref, d_ref, nm_ref, nv_ref):
        gr = g_ref[...]
        m_new = ADAM_B1 * m_ref[...] + (1.0 - ADAM_B1) * gr
        v_new = ADAM_B2 * v_ref[...] + (1.0 - ADAM_B2) * (gr * gr)
        m_hat = m_new / (1.0 - ADAM_B1 ** ADAM_STEP)
        v_hat = v_new / (1.0 - ADAM_B2 ** ADAM_STEP)
        d_ref[...] = -ADAM_LR * (m_hat / (jnp.sqrt(v_hat) + ADAM_EPS) + ADAM_WD * w_ref[...])
        nm_ref[...] = m_new
        nv_ref[...] = v_new

    spec = pl.BlockSpec((tr, cols), lambda i: (i, 0))
    d, nm, nv = pl.pallas_call(
        body, name=name, grid=(rows // tr,), in_specs=[spec] * 4, out_specs=[spec] * 3,
        out_shape=[jax.ShapeDtypeStruct((rows, cols), F32)] * 3, compiler_params=_params(("arbitrary",)),
    )(w2, g2, m2, v2)
    return d.reshape(shape), nm.reshape(shape), nv.reshape(shape)


def _pad_rows(flat, mult):
    n = flat.shape[0]
    unit = PACK_COLS * mult
    total = -(-n // unit) * unit
    return jnp.pad(flat, (0, total - n)).reshape(total // PACK_COLS, PACK_COLS)


def _pad_axis(arr, axis, mult):
    n = arr.shape[axis]
    total = -(-n // mult) * mult
    if total == n:
        return arr
    widths = [(0, 0)] * arr.ndim
    widths[axis] = (0, total - n)
    return jnp.pad(arr, widths)


def _shard_slice(arr, axis, blk, nblk=4):
    w = arr.shape[axis] // nblk
    return lax.slice_in_dim(arr, blk * w, (blk + 1) * w, axis=axis)


_NT = (((1,), (1,)), ((), ()))
_TN = (((0,), (0,)), ((), ()))


def _ffn_fwd(tag, x, h, wg, wu, wd, g_next):
    S, D = x.shape
    (wg, gi), (wu, ui), (wd, di) = wg, wu, wd
    NB, Fb = wg.shape[0], wg.shape[2]
    tm = _pick(S, 1024, 8)

    def gate_up(h_ref, wg_ref, wu_ref, a_ref, u_ref, s_ref):
        h_t = _mx(h_ref[...])
        a = jnp.dot(h_t, _mx(wg_ref[...]), preferred_element_type=F32)
        u = jnp.dot(h_t, _mx(wu_ref[...]), preferred_element_type=F32)
        a_ref[...] = a.astype(a_ref.dtype)
        u_ref[...] = u.astype(u_ref.dtype)
        s_ref[...] = (a * _sigmoid(a) * u).astype(s_ref.dtype)

    hid = pl.BlockSpec((None, tm, Fb), lambda b, i: (b, i, 0))
    a, u, s = pl.pallas_call(
        gate_up, name=tag + "_gate_up", grid=(NB, S // tm),
        in_specs=[pl.BlockSpec((tm, D), lambda b, i: (i, 0)), pl.BlockSpec((None, D, Fb), lambda b, i: (b, gi, 0)),
                  pl.BlockSpec((None, D, Fb), lambda b, i: (b, ui, 0))], out_specs=[hid] * 3,
        out_shape=[jax.ShapeDtypeStruct((NB, S, Fb), BF16)] * 3, compiler_params=_params(("arbitrary", "arbitrary")),
    )(h, wg, wu)

    tm2 = _pick(S, 512, 8)

    def down(s_ref, wd_ref, x_ref, g_ref, xo_ref, ho_ref):
        acc = jnp.dot(_mx(s_ref[0]), _mx(wd_ref[0]), preferred_element_type=F32)
        for b in range(1, NB):
            acc = acc + jnp.dot(_mx(s_ref[b]), _mx(wd_ref[b]), preferred_element_type=F32)
        x_new = x_ref[...] + 0.5 * acc
        xo_ref[...] = x_new
        ho_ref[...] = (_rms_stats(x_new)[0] * g_ref[...]).astype(ho_ref.dtype)

    row = pl.BlockSpec((tm2, D), lambda i: (i, 0))
    x_new, h_next = pl.pallas_call(
        down, name=tag + "_down", grid=(S // tm2,),
        in_specs=[pl.BlockSpec((NB, tm2, Fb), lambda i: (0, i, 0)), pl.BlockSpec((NB, Fb, D), lambda i: (0, di, 0)),
                  row, pl.BlockSpec((1, D), lambda i: (0, 0))],
        out_specs=[row, row], out_shape=[jax.ShapeDtypeStruct((S, D), F32), jax.ShapeDtypeStruct((S, D), BF16)],
        compiler_params=_params(("arbitrary",)),
    )(s, wd, x, g_next)
    return x_new, h_next, (x, h, a, u, s)


def _ffn_bwd(tag, saved, dx_out, dxb, wg, wu, wd, gain):
    x, h, a, u, s = saved
    S, D = x.shape
    (wg, gi), (wu, ui), (wd, di) = wg, wu, wd
    NB, Fb = wg.shape[0], wg.shape[2]
    tm = _pick(S, 1024, 8)
    tk = _pick(S, 1024, 128)
    nk = S // tk

    def dgate_up(d_ref, wd_ref, a_ref, u_ref, da_ref, du_ref):
        ds = 0.5 * lax.dot_general(_mx(d_ref[...]), _mx(wd_ref[...]), _NT, preferred_element_type=F32)
        a_t = a_ref[...].astype(F32)
        u_t = u_ref[...].astype(F32)
        sig = _sigmoid(a_t)
        da_ref[...] = (ds * u_t * (sig * (1.0 + a_t * (1.0 - sig)))).astype(da_ref.dtype)
        du_ref[...] = (ds * (a_t * sig)).astype(du_ref.dtype)

    hid = pl.BlockSpec((None, tm, Fb), lambda b, i: (b, i, 0))
    da, du = pl.pallas_call(
        dgate_up, name=tag + "_dgate_up", grid=(NB, S // tm),
        in_specs=[pl.BlockSpec((tm, D), lambda b, i: (i, 0)), pl.BlockSpec((None, Fb, D), lambda b, i: (b, di, 0)),
                  hid, hid],
        out_specs=[hid, hid], out_shape=[jax.ShapeDtypeStruct((NB, S, Fb), BF16)] * 2,
        compiler_params=_params(("arbitrary", "arbitrary")),
    )(dxb, wd, a, u)

    def dw_down(s_ref, d_ref, o_ref, acc_ref):
        k = pl.program_id(1)
        p = lax.dot_general(_mx(s_ref[...]), _mx(d_ref[...]), _TN, preferred_element_type=F32)

        @pl.when(k == 0)
        def _():
            acc_ref[...] = p

        @pl.when(k != 0)
        def _():
            acc_ref[...] += p

        @pl.when(k == nk - 1)
        def _():
            o_ref[...] = (0.5 * acc_ref[...]).astype(o_ref.dtype)

    hk = pl.BlockSpec((None, tk, Fb), lambda b, k: (b, k, 0))
    dwd = pl.pallas_call(
        dw_down, name=tag + "_dw_down", grid=(NB, nk),
        in_specs=[hk, pl.BlockSpec((tk, D), lambda b, k: (k, 0))],
        out_specs=pl.BlockSpec((None, Fb, D), lambda b, k: (b, 0, 0)),
        out_shape=jax.ShapeDtypeStruct((NB, Fb, D), _WIRE_DTYPE), scratch_shapes=[pltpu.VMEM((Fb, D), F32)],
        compiler_params=_params(("arbitrary", "arbitrary")),
    )(s, dxb)

    def dw_gate_up(h_ref, da_ref, du_ref, og_ref, ou_ref, accg_ref, accu_ref):
        k = pl.program_id(1)
        h_t = _mx(h_ref[...])
        pg = lax.dot_general(h_t, _mx(da_ref[...]), _TN, preferred_element_type=F32)
        pu = lax.dot_general(h_t, _mx(du_ref[...]), _TN, preferred_element_type=F32)

        @pl.when(k == 0)
        def _():
            accg_ref[...] = pg
            accu_ref[...] = pu

        @pl.when(k != 0)
        def _():
            accg_ref[...] += pg
            accu_ref[...] += pu

        @pl.when(k == nk - 1)
        def _():
            og_ref[...] = accg_ref[...].astype(og_ref.dtype)
            ou_ref[...] = accu_ref[...].astype(ou_ref.dtype)

    wout = pl.BlockSpec((None, D, Fb), lambda b, k: (b, 0, 0))
    dwg, dwu = pl.pallas_call(
        dw_gate_up, name=tag + "_dw_gate_up", grid=(NB, nk),
        in_specs=[pl.BlockSpec((tk, D), lambda b, k: (k, 0)), hk, hk], out_specs=[wout, wout],
        out_shape=[jax.ShapeDtypeStruct((NB, D, Fb), _WIRE_DTYPE)] * 2,
        scratch_shapes=[pltpu.VMEM((D, Fb), F32)] * 2, compiler_params=_params(("arbitrary", "arbitrary")),
    )(h, da, du)

    tm2 = _pick(S, 512, 8)

    def dx_body(da_ref, du_ref, wg_hbm, wu_hbm, x_ref, dxo_ref, g_ref, dx_ref, dxb_ref, dg_ref, wg_v, wu_v, sem):
        i = pl.program_id(0)

        @pl.when(i == 0)
        def _():
            cg = pltpu.make_async_copy(wg_hbm.at[:, pl.ds(gi * D, D), :], wg_v, sem.at[0])
            cu = pltpu.make_async_copy(wu_hbm.at[:, pl.ds(ui * D, D), :], wu_v, sem.at[1])
            cg.start()
            cu.start()
            cg.wait()
            cu.wait()

        dh = None
        for b in range(NB):
            t = lax.dot_general(_mx(da_ref[b]), wg_v[b], _NT, preferred_element_type=F32)
            t = t + lax.dot_general(_mx(du_ref[b]), wu_v[b], _NT, preferred_element_type=F32)
            dh = t if dh is None else dh + t
        dx_n, dg = _rms_bwd(x_ref[...], g_ref[...], dh)
        dx = dxo_ref[...] + dx_n
        dx_ref[...] = dx
        dxb_ref[...] = dx.astype(dxb_ref.dtype)
        dg = jnp.sum(dg, axis=0, keepdims=True)

        @pl.when(i == 0)
        def _():
            dg_ref[...] = dg

        @pl.when(i != 0)
        def _():
            dg_ref[...] += dg

    row = pl.BlockSpec((tm2, D), lambda i: (i, 0))
    hid2 = pl.BlockSpec((NB, tm2, Fb), lambda i: (0, i, 0))
    anyspec = pl.BlockSpec(memory_space=pl.ANY)
    fixed = pl.BlockSpec((1, D), lambda i: (0, 0))
    dx, dxb_new, dgain = pl.pallas_call(
        dx_body, name=tag + "_dx", grid=(S // tm2,),
        in_specs=[hid2, hid2, anyspec, anyspec, row, row, fixed], out_specs=[row, row, fixed],
        out_shape=[jax.ShapeDtypeStruct((S, D), F32), jax.ShapeDtypeStruct((S, D), BF16),
                   jax.ShapeDtypeStruct((1, D), F32)],
        scratch_shapes=[pltpu.VMEM((NB, D, Fb), wg.dtype), pltpu.VMEM((NB, D, Fb), wu.dtype),
                        pltpu.SemaphoreType.DMA((2,))],
        compiler_params=_params(("arbitrary",)),
    )(da, du, wg, wu, x, dx_out, gain)
    return dx, dxb_new, dgain, dwg, dwu, dwd


def _conv_mixer_fwd(tag, x, h, w_in, w_taps, w_out, g_next):
    S, D = x.shape
    C3 = w_in.shape[1]
    tm = _pick(S, 512, 8)
    p, = _fused_matmul(tag + "_in", 'nn', [_op(h)], [_op(w_in)], [(0, 0, 0)], 1, _ident_epi(), [F32],
                       S, C3, D, tm, _pick(C3, 1024, 128), D)
    m = _conv_fwd_call(p, w_taps)
    x_new, h_next = _fused_matmul(tag + "_out", 'nn', [_op(m)], [_op(w_out)], [(0, 0, 0)], 1, _resid_norm_epi(1.0),
                                  [F32, BF16], S, D, D, tm, D, D, tile_extras=[x], row_extras=[g_next])
    return x_new, h_next, (x, h, p, m)


def _conv_mixer_bwd(tag, saved, dx_out, dxb, w_in, w_taps, w_out, gain):
    x, h, p, m = saved
    S, D = x.shape
    C3 = w_in.shape[1]
    tm = _pick(S, 512, 8)
    tk = _pick(S, 512, 128)
    dm, = _fused_matmul(tag + "_dm", 'nt', [_op(dxb)], [_op(w_out)], [(0, 0, 0)], 1, _ident_epi(), [F32],
                        S, D, D, tm, D, D)
    dw_out, = _fused_matmul(tag + "_dw_out", 'tn', [_op(m)], [_op(dxb)], [(0, 0, 0)], 1, _ident_epi(), [F32],
                            D, D, S, D, D, tk)
    dp, dtaps = _conv_bwd_call(p, w_taps, dm)
    dw_in, = _fused_matmul(tag + "_dw_in", 'tn', [_op(h)], [_op(dp)], [(0, 0, 0)], 1, _ident_epi(), [F32],
                           D, C3, S, D, _pick(C3, 1024, 128), tk)
    dx, dxb_new, dgain = _fused_matmul(tag + "_dx", 'nt', [_op(dp)], [_op(w_in)], [(0, 0, 0)], 1, _norm_bwd_epi,
                                       [F32, BF16], S, D, C3, tm, D, _pick(C3, 1024, 128),
                                       tile_extras=[x, dx_out], row_extras=[gain], n_colsum=1)
    return dx, dxb_new, dgain, dw_in, dtaps, dw_out


def _attn_scale():
    return np.float32(QK_DIM ** -0.5)


def _even_mixer_fwd(tag, x, h, wts, tables, g_next):
    S, D = x.shape
    cos, sa, sb = tables
    tm = _pick(S, 512, 8)
    AW = HEADS * HP
    proj, = _fused_matmul(tag + "_in", 'nn', [_op(h)], [_op(wts['w_in'])], [(0, 0, 0)], 1, _ident_epi(), [F32],
                          S, PROJ_W, D, tm, _pick(PROJ_W, 896, 128), D)
    cqn, ckvn, kr, u, vn = _even_prep_call(proj, wts['q_norm'], wts['kv_norm'], wts['sg_norm'], cos, sa, sb)
    scale = _attn_scale()

    def q_epi(accs, tiles, rows, mrows):
        c_t, a_t, b_t = mrows
        heads = [_rope(accs[0][:, hh * HP:(hh + 1) * HP], c_t, a_t, b_t) * scale for hh in range(HEADS)]
        return [jnp.concatenate(heads, axis=1)]

    q, = _fused_matmul(tag + "_q", 'nn', [_op(cqn)], [_op(wts['w_q'])], [(0, 0, 0)], 1, q_epi, [BF16],
                       S, AW, Q_LORA, tm, AW, Q_LORA, mrow_extras=[cos, sa, sb])

    def kv_epi(accs, tiles, rows, mrows):
        return [accs[0] + jnp.concatenate([mrows[0].astype(F32)] * HEADS, axis=1), accs[1]]

    k, v = _fused_matmul(tag + "_kv", 'nn', [_op(ckvn)], [_op(wts['w_k']), _op(wts['w_v'])],
                         [(0, 0, 0), (0, 1, 1)], 2, kv_epi, [BF16, BF16], S, AW, KV_LORA, tm, AW, KV_LORA,
                         mrow_extras=[kr])
    o, lse = _flash_fwd_call(q, k, v)
    mix = _sgu_fwd_call(vn, u, o, wts['sg_wst'], wts['sg_bexp'])
    x_new, h_next = _fused_matmul(tag + "_out", 'nn', [_op(mix)], [_op(wts['w_out'])], [(0, 0, 0)], 1,
                                  _resid_norm_epi(1.0), [F32, BF16], S, D, AW + SG_WIDTH, tm, D, AW + SG_WIDTH,
                                  tile_extras=[x], row_extras=[g_next])
    return x_new, h_next, (x, h, proj, cqn, ckvn, u, vn, q, k, v, o, lse, mix)


def _even_mixer_bwd(tag, saved, dx_out, dxb, wts, tables, gain):
    x, h, proj, cqn, ckvn, u, vn, q, k, v, o, lse, mix = saved
    S, D = x.shape
    cos, sa, sb = tables
    tm = _pick(S, 512, 8)
    tk = _pick(S, 512, 128)
    AW = HEADS * HP
    MW = AW + SG_WIDTH
    dmix, = _fused_matmul(tag + "_dmix", 'nt', [_op(dxb)], [_op(wts['w_out'])], [(0, 0, 0)], 1, _ident_epi(), [BF16],
                          S, MW, D, tm, _pick(MW, 768, 128), D)
    dw_out, = _fused_matmul(tag + "_dw_out", 'tn', [_op(mix)], [_op(dxb)], [(0, 0, 0)], 1, _ident_epi(), [F32],
                            MW, D, S, _pick(MW, 768, 128), D, tk)
    du, dvn, dsg_w, dsg_b = _sgu_bwd_call(dmix, vn, u, wts['sg_wst'], wts['sg_wst_t'], wts['sg_bexp'])
    delta = _attn_delta_call(o, dmix)
    dq, dk, dv = _flash_bwd_call(q, k, v, dmix, lse, delta)
    scale = _attn_scale()

    def dq_epi(accs, tiles, rows, mrows):
        return accs

    def dq_pre_call():
        tr = _pick(S, 256, 8)

        def body(d_ref, c_ref, a_ref, b_ref, o_ref):
            for hh in range(HEADS):
                t = _rope_t(d_ref[:, hh * HP:(hh + 1) * HP], c_ref[...], a_ref[...], b_ref[...]) * scale
                o_ref[:, hh * HP:(hh + 1) * HP] = t.astype(o_ref.dtype)

        row = lambda i: (i, 0)
        return pl.pallas_call(
            body, name=tag + "_dq_unrope", grid=(S // tr,),
            in_specs=[pl.BlockSpec((tr, AW), row)] + [pl.BlockSpec((tr, HP), row)] * 3,
            out_specs=pl.BlockSpec((tr, AW), row), out_shape=jax.ShapeDtypeStruct((S, AW), BF16),
            compiler_params=_params(("arbitrary",)),
        )(dq, cos, sa, sb)

    dqp = dq_pre_call()
    dw_q, = _fused_matmul(tag + "_dw_q", 'tn', [_op(cqn)], [_op(dqp)], [(0, 0, 0)], 1, _ident_epi(), [F32],
                          Q_LORA, AW, S, Q_LORA, AW, tk)
    dcqn, = _fused_matmul(tag + "_dcq", 'nt', [_op(dqp)], [_op(wts['w_q'])], [(0, 0, 0)], 1, dq_epi, [F32],
                          S, Q_LORA, AW, tm, Q_LORA, AW)
    dw_k, dw_v = _fused_matmul(tag + "_dw_kv", 'tn', [_op(ckvn)], [_op(dk), _op(dv)], [(0, 0, 0), (0, 1, 1)], 2,
                               _ident_epi(), [F32, F32], KV_LORA, AW, S, KV_LORA, AW, tk)
    dckvn, = _fused_matmul(tag + "_dckv", 'nt', [_op(dk), _op(dv)], [_op(wts['w_k']), _op(wts['w_v'])],
                           [(0, 0, 0), (1, 1, 0)], 1, dq_epi, [F32], S, KV_LORA, AW, tm, KV_LORA, AW)
    dproj, dqn, dkvn, dsgn = _even_prep_bwd_call(proj, wts['q_norm'], wts['kv_norm'], wts['sg_norm'], cos, sa, sb,
                                                 dcqn, dckvn, dk, du, dvn)
    dw_in, = _fused_matmul(tag + "_dw_in", 'tn', [_op(h)], [_op(dproj)], [(0, 0, 0)], 1, _ident_epi(), [F32],
                           D, PROJ_W, S, D, _pick(PROJ_W, 896, 128), tk)
    dx, dxb_new, dgain = _fused_matmul(tag + "_dx", 'nt', [_op(dproj)], [_op(wts['w_in'])], [(0, 0, 0)], 1,
                                       _norm_bwd_epi, [F32, BF16], S, D, PROJ_W, tm, D, _pick(PROJ_W, 896, 128),
                                       tile_extras=[x, dx_out], row_extras=[gain], n_colsum=1)
    grads = dict(w_in=dw_in, w_q=dw_q, w_k=dw_k, w_v=dw_v, w_out=dw_out, q_norm=dqn, kv_norm=dkvn, sg_norm=dsgn,
                 sg_w=dsg_w, sg_b=dsg_b)
    return dx, dxb_new, dgain, grads


def _even_weights(w_in, w_uq, w_ukv, w_out, q_norm, kv_norm, sg_norm, sg_w, sg_b):
    D = w_in.shape[0]
    kr_cols = jnp.pad(w_in[:, Q_LORA + KV_LORA:Q_LORA + KV_LORA + ROPE], ((0, 0), (NOPE, HP - QK_DIM)))
    w_in_p = jnp.concatenate([w_in[:, :Q_LORA + KV_LORA], kr_cols, w_in[:, Q_LORA + KV_LORA + ROPE:]], axis=1)
    wq = w_uq.reshape(Q_LORA, HEADS, QK_DIM)
    w_q = jnp.pad(wq, ((0, 0), (0, 0), (0, HP - QK_DIM))).reshape(Q_LORA, HEADS * HP)
    wkv = w_ukv.reshape(KV_LORA, HEADS, NOPE + VDIM)
    w_k = jnp.pad(wkv[:, :, :NOPE], ((0, 0), (0, 0), (0, HP - NOPE))).reshape(KV_LORA, HEADS * HP)
    w_v = jnp.pad(wkv[:, :, NOPE:], ((0, 0), (0, 0), (0, HP - VDIM))).reshape(KV_LORA, HEADS * HP)
    wo_a = w_out[:HEADS * VDIM].reshape(HEADS, VDIM, D)
    wo_a = jnp.pad(wo_a, ((0, 0), (0, HP - VDIM), (0, 0))).reshape(HEADS * HP, D)
    w_out_p = jnp.concatenate([wo_a, w_out[HEADS * VDIM:]], axis=0)
    tri = jnp.tril(jnp.ones((SG_CHUNK, SG_CHUNK), F32))
    wm = sg_w * tri
    wst = wm.reshape(SG_GROUPS // 2, 2 * SG_CHUNK, SG_CHUNK).astype(_MXU_DTYPE)
    wst_t = jnp.swapaxes(wm, 1, 2).reshape(SG_GROUPS // 2, 2 * SG_CHUNK, SG_CHUNK).astype(_MXU_DTYPE)
    bexp = jnp.repeat(sg_b.T, SG_GDIM, axis=1)
    return dict(w_in=w_in_p, w_q=w_q, w_k=w_k, w_v=w_v, w_out=w_out_p, sg_wst=wst, sg_wst_t=wst_t, sg_bexp=bexp,
                q_norm=q_norm.reshape(1, -1), kv_norm=kv_norm.reshape(1, -1), sg_norm=sg_norm.reshape(1, -1))


def _even_grads_unpad(g):
    d_in = g['w_in']
    kr0 = Q_LORA + KV_LORA
    dw_in = jnp.concatenate([d_in[:, :kr0], d_in[:, kr0 + NOPE:kr0 + QK_DIM], d_in[:, kr0 + HP:]], axis=1)
    dw_uq = g['w_q'].reshape(Q_LORA, HEADS, HP)[:, :, :QK_DIM].reshape(Q_LORA, HEADS * QK_DIM)
    dk = g['w_k'].reshape(KV_LORA, HEADS, HP)[:, :, :NOPE]
    dv = g['w_v'].reshape(KV_LORA, HEADS, HP)[:, :, :VDIM]
    dw_ukv = jnp.concatenate([dk, dv], axis=2).reshape(KV_LORA, HEADS * (NOPE + VDIM))
    D = d_in.shape[0]
    wo = g['w_out']
    wo_a = wo[:HEADS * HP].reshape(HEADS, HP, D)[:, :VDIM].reshape(HEADS * VDIM, D)
    dw_out = jnp.concatenate([wo_a, wo[HEADS * HP:]], axis=0)
    dsg_b = g['sg_b'][:, :SG_GROUPS].T
    return dict(even_w_in=dw_in, w_uq=dw_uq, w_ukv=dw_ukv, even_w_out=dw_out, q_norm=g['q_norm'][0],
                kv_norm=g['kv_norm'][0], sg_norm=g['sg_norm'][0], sg_w=g['sg_w'], sg_b=dsg_b)


def kernel(x, positions, ffn_pre_norm, ffn_pre_w_gate, ffn_pre_w_up, ffn_pre_w_down, mix_norm, ffn_post_norm, ffn_post_w_gate, ffn_post_w_up, ffn_post_w_down, even_w_in, q_norm, w_uq, kv_norm, w_ukv, sg_norm, sg_w, sg_b, even_w_out, conv_w_in, conv_w, conv_w_out, final_norm, loss_target, m_ffn_pre_norm, m_ffn_pre_w_gate, m_ffn_pre_w_up, m_ffn_pre_w_down, m_mix_norm, m_ffn_post_norm, m_ffn_post_w_gate, m_ffn_post_w_up, m_ffn_post_w_down, m_even_w_in, m_q_norm, m_w_uq, m_kv_norm, m_w_ukv, m_sg_norm, m_sg_w, m_sg_b, m_even_w_out, m_conv_w_in, m_conv_w, m_conv_w_out, m_final_norm, v_ffn_pre_norm, v_ffn_pre_w_gate, v_ffn_pre_w_up, v_ffn_pre_w_down, v_mix_norm, v_ffn_post_norm, v_ffn_post_w_gate, v_ffn_post_w_up, v_ffn_post_w_down, v_even_w_in, v_q_norm, v_w_uq, v_kv_norm, v_w_ukv, v_sg_norm, v_sg_w, v_sg_b, v_even_w_out, v_conv_w_in, v_conv_w, v_conv_w_out, v_final_norm):
    env = dict(locals())
    w_loc = {n: env[n] for n in WEIGHTS}
    m_loc = {n: env['m_' + n] for n in WEIGHTS}
    v_loc = {n: env['v_' + n] for n in WEIGHTS}
    S, D = x.shape[1], x.shape[2]
    depth = ffn_pre_norm.shape[0]
    xs = x.reshape(S, D)
    target = loss_target.reshape(S, D)

    Fb = ffn_pre_w_gate.shape[2]
    wire2d = lambda n, cols: w_loc[n].astype(_WIRE_DTYPE).reshape(-1, cols)
    shard_a = _pad_axis(jnp.concatenate([wire2d(n, Fb) for n in GROUP_A], axis=0), 0, PACK_ROW_MULT)
    shard_b = _pad_axis(jnp.concatenate([wire2d(n, D) for n in GROUP_B], axis=0), 0, PACK_ROW_MULT)
    shard_c = _pad_rows(jnp.concatenate([w_loc[n].astype(_WIRE_DTYPE).reshape(-1) for n in GROUP_C]), PACK_ROW_MULT)
    gat_a, gat_b, gat_c = _gather_halves_call([shard_a, shard_b, shard_c])
    taps = _gather_weights_call("gather_taps", _pad_rows(conv_w.reshape(-1), 8)).reshape(4, -1)
    taps = jnp.concatenate([taps[b, :conv_w.size].reshape(conv_w.shape) for b in range(4)], axis=2)
    full = {}
    for ia, n in enumerate(GROUP_A):
        full[n] = [(gat_a, ia * depth + l) for l in range(depth)]
    row = 0
    for n in GROUP_B:
        shp = w_loc[n].shape
        rows = shp[0] * shp[1]
        if n in FFN_WEIGHTS:
            full[n] = [(gat_b, row // shp[1] + l) for l in range(depth)]
        else:
            full[n] = jnp.concatenate([gat_b[b, row:row + rows].reshape(shp) for b in range(4)], axis=1)
        row += rows
    gflat = gat_c.reshape(4, -1)
    off = 0
    for n in GROUP_C:
        shp = w_loc[n].shape
        size = int(np.prod(shp))
        full[n] = jnp.concatenate([gflat[b, off:off + size].reshape(shp) for b in range(4)], axis=SHARD_AXIS[n])
        off += size

    inv_freq = ROPE_THETA ** (-jnp.arange(0, ROPE, 2, dtype=F32) / ROPE)
    half = ROPE // 2
    zeros = lambda n: jnp.zeros((n,), F32)
    ones = jnp.ones((half,), F32)
    invf = jnp.concatenate([zeros(NOPE), inv_freq, inv_freq, zeros(HP - QK_DIM)]).reshape(1, HP)
    mask_a = jnp.concatenate([zeros(NOPE), -ones, zeros(HP - NOPE - half)]).reshape(1, HP)
    mask_b = jnp.concatenate([zeros(NOPE + half), ones, zeros(HP - QK_DIM)]).reshape(1, HP)
    tables = _rope_tables_call(positions.reshape(S, 1), invf, mask_a, mask_b)

    even_w = []
    for e in range((depth + 1) // 2):
        even_w.append(_even_weights(full['even_w_in'][e], full['w_uq'][e], full['w_ukv'][e], full['even_w_out'][e],
                                    q_norm[e], kv_norm[e], sg_norm[e], sg_w[e], sg_b[e]))

    def gain_row(arr, l):
        return arr[l].reshape(1, D)

    saved = []
    h = _rmsnorm_call("first_norm", xs, gain_row(ffn_pre_norm, 0))
    xc = xs
    for l in range(depth):
        xc, h, s_pre = _ffn_fwd(f"l{l}_pre", xc, h, full['ffn_pre_w_gate'][l], full['ffn_pre_w_up'][l],
                                full['ffn_pre_w_down'][l], gain_row(mix_norm, l))
        if l % 2 == 0:
            xc, h, s_mix = _even_mixer_fwd(f"l{l}_mix", xc, h, even_w[l // 2], tables, gain_row(ffn_post_norm, l))
        else:
            o = l // 2
            xc, h, s_mix = _conv_mixer_fwd(f"l{l}_mix", xc, h, full['conv_w_in'][o], taps[o],
                                           full['conv_w_out'][o], gain_row(ffn_post_norm, l))
        g_next = gain_row(ffn_pre_norm, l + 1) if l + 1 < depth else final_norm.reshape(1, D)
        xc, h, s_post = _ffn_fwd(f"l{l}_post", xc, h, full['ffn_post_w_gate'][l], full['ffn_post_w_up'][l],
                                 full['ffn_post_w_down'][l], g_next)
        saved.append((s_pre, s_mix, s_post))

    dx, dxb, d_final, loss_part = _loss_call(xc, target, final_norm.reshape(1, D))
    loss = lax.psum(loss_part[0, 0], ("x", "y", "c"))

    gl = {n: [None] * w_loc[n].shape[0] for n in WEIGHTS if n != 'final_norm'}
    for l in reversed(range(depth)):
        s_pre, s_mix, s_post = saved[l]
        dx, dxb, dgain, dwg, dwu, dwd = _ffn_bwd(f"l{l}_post", s_post, dx, dxb, full['ffn_post_w_gate'][l],
                                                 full['ffn_post_w_up'][l], full['ffn_post_w_down'][l],
                                                 gain_row(ffn_post_norm, l))
        gl['ffn_post_norm'][l] = dgain[0]
        gl['ffn_post_w_gate'][l], gl['ffn_post_w_up'][l], gl['ffn_post_w_down'][l] = dwg, dwu, dwd
        if l % 2 == 0:
            e = l // 2
            dx, dxb, dgain, eg = _even_mixer_bwd(f"l{l}_mix", s_mix, dx, dxb, even_w[e], tables, gain_row(mix_norm, l))
            for n, val in _even_grads_unpad(eg).items():
                gl[n][e] = val
        else:
            o = l // 2
            dx, dxb, dgain, dw_in, dtaps, dw_out = _conv_mixer_bwd(f"l{l}_mix", s_mix, dx, dxb, full['conv_w_in'][o],
                                                                   taps[o], full['conv_w_out'][o],
                                                                   gain_row(mix_norm, l))
            gl['conv_w_in'][o], gl['conv_w'][o], gl['conv_w_out'][o] = dw_in, dtaps, dw_out
        gl['mix_norm'][l] = dgain[0]
        dx, dxb, dgain, dwg, dwu, dwd = _ffn_bwd(f"l{l}_pre", s_pre, dx, dxb, full['ffn_pre_w_gate'][l],
                                                 full['ffn_pre_w_up'][l], full['ffn_pre_w_down'][l],
                                                 gain_row(ffn_pre_norm, l))
        gl['ffn_pre_norm'][l] = dgain[0]
        gl['ffn_pre_w_gate'][l], gl['ffn_pre_w_up'][l], gl['ffn_pre_w_down'][l] = dwg, dwu, dwd
    grad_x = dx.reshape(x.shape)
    part = {n: jnp.stack(gl[n]) for n in gl if n not in FFN_WEIGHTS}
    part['final_norm'] = d_final[0]

    def row_blocked(n):
        g = part[n]
        L, r4, cols = g.shape
        return jnp.swapaxes(g.reshape(L, 4, r4 // 4, cols), 0, 1).reshape(4, L * (r4 // 4), cols).astype(_WIRE_DTYPE)

    pack_a = jnp.concatenate([gl[n][l] for n in GROUP_A for l in range(depth)], axis=1)
    pack_b = jnp.concatenate([gl[n][l] for n in GROUP_B if n in FFN_WEIGHTS for l in range(depth)]
                             + [row_blocked(n) for n in GROUP_B if n not in FFN_WEIGHTS], axis=1)
    pack_c = jnp.stack([_pad_rows(jnp.concatenate(
        [_shard_slice(part[n], SHARD_AXIS[n], b).astype(_WIRE_DTYPE).reshape(-1) for n in GROUP_C]), PACK_ROW_MULT)
        for b in range(4)])
    packs = [_pad_axis(p, 1, PACK_ROW_MULT) for p in (pack_a, pack_b, pack_c)]
    packs = [p.reshape(4, 2, p.shape[1] // 2, p.shape[2]) for p in packs]
    core = lax.axis_index("c").astype(jnp.int32).reshape(1)
    theirs = _pair_exchange_call(packs)
    pairs = [_pair_add_call(f"pair_add_{i}", p, t, core) for i, (p, t) in enumerate(zip(packs, theirs))]
    arrived = _chip_scatter_call(pairs)
    mine = [_sum_slots_call(f"sum_grad_slots_{i}", r) for i, r in enumerate(arrived)]
    red_a, red_b, red_c = [t.reshape(-1, t.shape[2]) for t in _sibling_share_call(mine)]
    grads = {}
    for group, red in ((GROUP_A, red_a), (GROUP_B, red_b)):
        row = 0
        for n in group:
            shp = w_loc[n].shape
            rows = shp[0] * shp[1]
            grads[n] = red[row:row + rows].reshape(shp)
            row += rows
    red_c = red_c.reshape(-1)
    off = 0
    for n in GROUP_C:
        shp = w_loc[n].shape
        size = int(np.prod(shp))
        grads[n] = red_c[off:off + size].reshape(shp)
        off += size

    small = _pad_rows(jnp.concatenate([part[n].reshape(-1) for n in REPLICATED]), 8)
    small_sum = _allreduce_small_call(small).reshape(-1)
    off = 0
    for n in REPLICATED:
        size = int(np.prod(w_loc[n].shape))
        grads[n] = small_sum[off:off + size].reshape(w_loc[n].shape)
        off += size

    deltas, new_m, new_v = {}, {}, {}
    for n in WEIGHTS:
        deltas[n], new_m[n], new_v[n] = _adamw_call("adamw_" + n, w_loc[n], grads[n], m_loc[n], v_loc[n])
    return (loss, grad_x, *[grads[n] for n in WEIGHTS], *[deltas[n] for n in WEIGHTS],
            *[new_m[n] for n in WEIGHTS], *[new_v[n] for n in WEIGHTS])
```

```python
import functools

import numpy as np
import jax
import jax.numpy as jnp
from jax import lax
from jax.experimental import pallas as pl
from jax.experimental.pallas import tpu as pltpu

F32 = jnp.float32
BF16 = jnp.bfloat16
_MXU_DTYPE = jnp.bfloat16
_WIRE_DTYPE = jnp.bfloat16
_VMEM_LIMIT = 52 * 1024 * 1024
_LANES = 128
_ATT_BLOCK = 512
_ROW_TILE = 512
_SG_TILE = 1024

NORM_EPS = 1e-6
HEADS = 8
NOPE = 64
ROPE = 32
VDIM = 64
QK_DIM = NOPE + ROPE
HP = 128
Q_LORA = 384
KV_LORA = 256
SG_WIDTH = 512
SG_GROUPS = 8
SG_GDIM = 64
SG_CHUNK = 128
ROPE_THETA = 10000.0
PROJ_W = Q_LORA + KV_LORA + HP + 2 * SG_WIDTH
ADAM_LR = 0.001
ADAM_B1 = 0.9
ADAM_B2 = 0.999
ADAM_EPS = 1e-08
ADAM_WD = 0.01
ADAM_STEP = 10
MESH = pl.DeviceIdType.MESH
PACK_COLS = 1024
PACK_ROW_MULT = 256

SHARDED = ['ffn_pre_w_gate', 'ffn_pre_w_up', 'ffn_pre_w_down', 'ffn_post_w_gate', 'ffn_post_w_up',
           'ffn_post_w_down', 'even_w_in', 'w_uq', 'w_ukv', 'even_w_out', 'conv_w_in', 'conv_w', 'conv_w_out']
SHARD_AXIS = {'ffn_pre_w_gate': 2, 'ffn_pre_w_up': 2, 'ffn_pre_w_down': 1, 'ffn_post_w_gate': 2,
              'ffn_post_w_up': 2, 'ffn_post_w_down': 1, 'even_w_in': 2, 'w_uq': 2, 'w_ukv': 2,
              'even_w_out': 1, 'conv_w_in': 2, 'conv_w': 2, 'conv_w_out': 1}
FFN_WEIGHTS = ['ffn_pre_w_gate', 'ffn_pre_w_up', 'ffn_pre_w_down', 'ffn_post_w_gate', 'ffn_post_w_up',
               'ffn_post_w_down']
GROUP_A = ['ffn_pre_w_gate', 'ffn_pre_w_up', 'ffn_post_w_gate', 'ffn_post_w_up']
GROUP_B = ['ffn_pre_w_down', 'ffn_post_w_down', 'even_w_out', 'conv_w_out']
GROUP_C = ['even_w_in', 'w_uq', 'w_ukv', 'conv_w_in', 'conv_w']
REPLICATED = ['ffn_pre_norm', 'mix_norm', 'ffn_post_norm', 'q_norm', 'kv_norm', 'sg_norm', 'sg_w', 'sg_b',
              'final_norm']
WEIGHTS = ['ffn_pre_norm', 'ffn_pre_w_gate', 'ffn_pre_w_up', 'ffn_pre_w_down', 'mix_norm', 'ffn_post_norm',
           'ffn_post_w_gate', 'ffn_post_w_up', 'ffn_post_w_down', 'even_w_in', 'q_norm', 'w_uq', 'kv_norm',
           'w_ukv', 'sg_norm', 'sg_w', 'sg_b', 'even_w_out', 'conv_w_in', 'conv_w', 'conv_w_out', 'final_norm']


def _params(sem=None):
    return pltpu.CompilerParams(vmem_limit_bytes=_VMEM_LIMIT,
                                **({} if sem is None else {'dimension_semantics': sem}))


def _pick(n, pref, mult):
    best = None
    t = mult
    while t <= min(n, pref):
        if n % t == 0:
            best = t
        t += mult
    return n if best is None else best


def _mx(v):
    return v if v.dtype == _MXU_DTYPE else v.astype(_MXU_DTYPE)


def _sigmoid(a):
    return 1.0 / (1.0 + jnp.exp(-a))


def _rms_stats(x):
    rstd = lax.rsqrt(jnp.mean(x * x, axis=-1, keepdims=True) + NORM_EPS)
    return x * rstd, rstd


def _rms_bwd(x, g, dh):
    xhat, rstd = _rms_stats(x)
    gdh = g * dh
    dx = rstd * (gdh - xhat * jnp.mean(gdh * xhat, axis=-1, keepdims=True))
    return dx, dh * xhat


def _fused_matmul(name, mode, lhs, rhs, prods, n_acc, epilogue, out_dtypes, M, N, K, tm, tn, tk,
                  tile_extras=(), row_extras=(), mrow_extras=(), n_colsum=0):
    gj, gi, gk = N // tn, M // tm, K // tk
    assert gj * tn == N and gi * tm == M and gk * tk == K, (name, M, N, K, tm, tn, tk)
    dims = {'nn': (((1,), (0,)), ((), ())), 'nt': (((1,), (1,)), ((), ())), 'tn': (((0,), (0,)), ((), ()))}[mode]

    def lhs_spec(roff, coff, kb):
        kb = tk if kb is None else kb
        if mode == 'tn':
            return pl.BlockSpec((kb, tm), lambda j, i, k: (k + roff, i + coff))
        return pl.BlockSpec((tm, kb), lambda j, i, k: (i + roff, k + coff))

    def rhs_spec(roff, coff, kb):
        kb = tk if kb is None else kb
        if mode == 'nt':
            return pl.BlockSpec((tn, kb), lambda j, i, k: (j + roff, k + coff))
        return pl.BlockSpec((kb, tn), lambda j, i, k: (k + roff, j + coff))

    in_specs = [lhs_spec(*a[1:]) for a in lhs] + [rhs_spec(*a[1:]) for a in rhs]
    in_specs += [pl.BlockSpec((tm, tn), lambda j, i, k: (i, j)) for _ in tile_extras]
    in_specs += [pl.BlockSpec((1, tn), lambda j, i, k: (0, j)) for _ in row_extras]
    in_specs += [pl.BlockSpec((tm, a.shape[1]), lambda j, i, k: (i, 0)) for a in mrow_extras]
    n_out = len(out_dtypes)
    out_shape = [jax.ShapeDtypeStruct((M, N), d) for d in out_dtypes]
    out_specs = [pl.BlockSpec((tm, tn), lambda j, i, k: (i, j)) for _ in out_dtypes]
    out_shape += [jax.ShapeDtypeStruct((1, N), F32) for _ in range(n_colsum)]
    out_specs += [pl.BlockSpec((1, tn), lambda j, i, k: (0, j)) for _ in range(n_colsum)]
    scratch = [pltpu.VMEM((tm, tn), F32) for _ in range(n_acc)] if gk > 1 else []
    nl, nr, nt, nrw, nm = len(lhs), len(rhs), len(tile_extras), len(row_extras), len(mrow_extras)

    def body(*refs):
        pos = 0
        lhs_refs = refs[pos:pos + nl]; pos += nl
        rhs_refs = refs[pos:pos + nr]; pos += nr
        tile_refs = refs[pos:pos + nt]; pos += nt
        row_refs = refs[pos:pos + nrw]; pos += nrw
        mrow_refs = refs[pos:pos + nm]; pos += nm
        out_refs = refs[pos:pos + n_out]; pos += n_out
        cs_refs = refs[pos:pos + n_colsum]; pos += n_colsum
        acc_refs = refs[pos:]
        i = pl.program_id(1)
        k = pl.program_id(2)

        def partials():
            res = [None] * n_acc
            for (li, ri, ai) in prods:
                d = lax.dot_general(_mx(lhs_refs[li][...]), _mx(rhs_refs[ri][...]), dims,
                                    preferred_element_type=F32)
                res[ai] = d if res[ai] is None else res[ai] + d
            return res

        def finish(accs):
            outs = epilogue(accs, [r[...] for r in tile_refs], [r[...] for r in row_refs],
                            [r[...] for r in mrow_refs])
            for r, o in zip(out_refs, outs[:n_out]):
                r[...] = o.astype(r.dtype)
            for r, c in zip(cs_refs, outs[n_out:]):
                c = jnp.sum(c, axis=0, keepdims=True)

                @pl.when(i == 0)
                def _():
                    r[...] = c

                @pl.when(i != 0)
                def _():
                    r[...] += c

        if gk == 1:
            finish(partials())
        else:
            p = partials()

            @pl.when(k == 0)
            def _():
                for r, v in zip(acc_refs, p):
                    r[...] = v

            @pl.when(k != 0)
            def _():
                for r, v in zip(acc_refs, p):
                    r[...] += v

            @pl.when(k == gk - 1)
            def _():
                finish([r[...] for r in acc_refs])

    res = pl.pallas_call(
        body, name=name, grid=(gj, gi, gk), in_specs=in_specs, out_specs=out_specs, out_shape=out_shape,
        scratch_shapes=scratch, compiler_params=_params(("arbitrary", "arbitrary", "arbitrary")),
    )(*[a[0] for a in lhs], *[a[0] for a in rhs], *tile_extras, *row_extras, *mrow_extras)
    return res


def _op(a, roff=0, coff=0, kb=None):
    return (a, roff, coff, kb)


def _ident_epi(scale=None):
    def epi(accs, tiles, rows, mrows):
        return [a if scale is None else a * scale for a in accs]
    return epi


def _resid_norm_epi(scale):
    def epi(accs, tiles, rows, mrows):
        x_new = tiles[0] + scale * accs[0]
        xhat, _ = _rms_stats(x_new)
        return [x_new, xhat * rows[0]]
    return epi


def _norm_bwd_epi(accs, tiles, rows, mrows):
    dx_n, dg = _rms_bwd(tiles[0], rows[0], accs[0])
    dx = tiles[1] + dx_n
    return [dx, dx, dg]


def _rmsnorm_call(name, x, g):
    S, D = x.shape
    tm = _pick(S, _ROW_TILE, 8)

    def body(x_ref, g_ref, h_ref):
        xhat, _ = _rms_stats(x_ref[...])
        h_ref[...] = (xhat * g_ref[...]).astype(h_ref.dtype)

    return pl.pallas_call(
        body, name=name, grid=(S // tm,),
        in_specs=[pl.BlockSpec((tm, D), lambda i: (i, 0)), pl.BlockSpec((1, D), lambda i: (0, 0))],
        out_specs=pl.BlockSpec((tm, D), lambda i: (i, 0)),
        out_shape=jax.ShapeDtypeStruct((S, D), BF16), compiler_params=_params(("arbitrary",)),
    )(x, g)


def _loss_call(x, target, g):
    S, D = x.shape
    tm = _pick(S, _ROW_TILE, 8)

    def body(x_ref, t_ref, g_ref, dx_ref, dxb_ref, dg_ref, loss_ref):
        i = pl.program_id(0)
        x_t = x_ref[...]
        gain = g_ref[...]
        xhat, _ = _rms_stats(x_t)
        diff = xhat * gain - t_ref[...]
        dy = diff * (1.0 / D)
        dx, dg = _rms_bwd(x_t, gain, dy)
        dx_ref[...] = dx
        dxb_ref[...] = dx.astype(BF16)
        dg = jnp.sum(dg, axis=0, keepdims=True)
        part = 0.5 * jnp.sum(jnp.sum(diff * diff, axis=1, keepdims=True), axis=0, keepdims=True) * (1.0 / D)
        part = jnp.broadcast_to(part, (1, _LANES))

        @pl.when(i == 0)
        def _():
            dg_ref[...] = dg
            loss_ref[...] = part

        @pl.when(i != 0)
        def _():
            dg_ref[...] += dg
            loss_ref[...] += part

    row = lambda i: (i, 0)
    fixed = lambda i: (0, 0)
    return pl.pallas_call(
        body, name="loss_head", grid=(S // tm,),
        in_specs=[pl.BlockSpec((tm, D), row), pl.BlockSpec((tm, D), row), pl.BlockSpec((1, D), fixed)],
        out_specs=[pl.BlockSpec((tm, D), row), pl.BlockSpec((tm, D), row), pl.BlockSpec((1, D), fixed),
                   pl.BlockSpec((1, _LANES), fixed)],
        out_shape=[jax.ShapeDtypeStruct((S, D), F32), jax.ShapeDtypeStruct((S, D), BF16),
                   jax.ShapeDtypeStruct((1, D), F32), jax.ShapeDtypeStruct((1, _LANES), F32)],
        compiler_params=_params(("arbitrary",)),
    )(x, target, g)


def _rope_tables_call(pos_col, invf, mask_a, mask_b):
    S = pos_col.shape[0]
    tm = _pick(S, _ROW_TILE, 8)

    def body(p_ref, f_ref, a_ref, b_ref, cos_ref, sa_ref, sb_ref):
        ang = p_ref[...].astype(F32) * f_ref[...]
        sn = jnp.sin(ang)
        cos_ref[...] = jnp.cos(ang)
        sa_ref[...] = sn * a_ref[...]
        sb_ref[...] = sn * b_ref[...]

    row = lambda i: (i, 0)
    fixed = lambda i: (0, 0)
    return pl.pallas_call(
        body, name="rope_tables", grid=(S // tm,),
        in_specs=[pl.BlockSpec((tm, 1), row)] + [pl.BlockSpec((1, HP), fixed)] * 3,
        out_specs=[pl.BlockSpec((tm, HP), row)] * 3,
        out_shape=[jax.ShapeDtypeStruct((S, HP), F32)] * 3, compiler_params=_params(("arbitrary",)),
    )(pos_col, invf, mask_a, mask_b)


def _rope(t, cos, sa, sb):
    return t * cos + pltpu.roll(t, HP - ROPE // 2, 1) * sa + pltpu.roll(t, ROPE // 2, 1) * sb


def _rope_t(d, cos, sa, sb):
    return d * cos + pltpu.roll(d * sa, ROPE // 2, 1) + pltpu.roll(d * sb, HP - ROPE // 2, 1)


def _gelu(z):
    return 0.5 * z * (1.0 + lax.erf(z * np.float32(1.0 / np.sqrt(2.0))))


def _gelu_grad(z):
    cdf = 0.5 * (1.0 + lax.erf(z * np.float32(1.0 / np.sqrt(2.0))))
    pdf = np.float32(1.0 / np.sqrt(2.0 * np.pi)) * jnp.exp(-0.5 * z * z)
    return cdf + z * pdf


_CQ0, _CKV0, _KR0, _Z0 = 0, Q_LORA, Q_LORA + KV_LORA, Q_LORA + KV_LORA + HP


def _even_prep_call(proj, qn, kvn, sgn, cos, sa, sb):
    S = proj.shape[0]
    tm = _pick(S, 256, 8)

    def body(p_ref, qn_ref, kvn_ref, sgn_ref, cos_ref, sa_ref, sb_ref, cq_ref, ckv_ref, kr_ref, u_ref, v_ref):
        cq = p_ref[:, _CQ0:_CQ0 + Q_LORA]
        cq_ref[...] = (_rms_stats(cq)[0] * qn_ref[...]).astype(BF16)
        ckv = p_ref[:, _CKV0:_CKV0 + KV_LORA]
        ckv_ref[...] = (_rms_stats(ckv)[0] * kvn_ref[...]).astype(BF16)
        kr = p_ref[:, _KR0:_KR0 + HP]
        kr_ref[...] = _rope(kr, cos_ref[...], sa_ref[...], sb_ref[...]).astype(BF16)
        u_ref[...] = _gelu(p_ref[:, _Z0:_Z0 + SG_WIDTH]).astype(BF16)
        zv = _gelu(p_ref[:, _Z0 + SG_WIDTH:_Z0 + 2 * SG_WIDTH])
        v_ref[...] = (_rms_stats(zv)[0] * sgn_ref[...]).astype(BF16)

    row = lambda i: (i, 0)
    fixed = lambda i: (0, 0)
    widths = [Q_LORA, KV_LORA, HP, SG_WIDTH, SG_WIDTH]
    return pl.pallas_call(
        body, name="even_prep", grid=(S // tm,),
        in_specs=[pl.BlockSpec((tm, PROJ_W), row), pl.BlockSpec((1, Q_LORA), fixed),
                  pl.BlockSpec((1, KV_LORA), fixed), pl.BlockSpec((1, SG_WIDTH), fixed)]
        + [pl.BlockSpec((tm, HP), row)] * 3,
        out_specs=[pl.BlockSpec((tm, w), row) for w in widths],
        out_shape=[jax.ShapeDtypeStruct((S, w), BF16) for w in widths],
        compiler_params=_params(("arbitrary",)),
    )(proj, qn, kvn, sgn, cos, sa, sb)


def _even_prep_bwd_call(proj, qn, kvn, sgn, cos, sa, sb, dcqn, dckvn, dk, du, dvn):
    S = proj.shape[0]
    tm = _pick(S, 256, 8)

    def body(p_ref, qn_ref, kvn_ref, sgn_ref, cos_ref, sa_ref, sb_ref, dcq_ref, dckv_ref, dk_ref, du_ref,
             dvn_ref, dp_ref, dqn_ref, dkvn_ref, dsgn_ref):
        i = pl.program_id(0)
        dcq, gq = _rms_bwd(p_ref[:, _CQ0:_CQ0 + Q_LORA], qn_ref[...], dcq_ref[...])
        dp_ref[:, _CQ0:_CQ0 + Q_LORA] = dcq.astype(BF16)
        dckv, gkv = _rms_bwd(p_ref[:, _CKV0:_CKV0 + KV_LORA], kvn_ref[...], dckv_ref[...])
        dp_ref[:, _CKV0:_CKV0 + KV_LORA] = dckv.astype(BF16)
        dkr = dk_ref[:, 0:HP].astype(F32)
        for h in range(1, HEADS):
            dkr = dkr + dk_ref[:, h * HP:(h + 1) * HP].astype(F32)
        lane = lax.broadcasted_iota(jnp.int32, dkr.shape, 1)
        dkr = jnp.where((lane >= NOPE) & (lane < QK_DIM), dkr, 0.0)
        dp_ref[:, _KR0:_KR0 + HP] = _rope_t(dkr, cos_ref[...], sa_ref[...], sb_ref[...]).astype(BF16)
        zu = p_ref[:, _Z0:_Z0 + SG_WIDTH]
        dp_ref[:, _Z0:_Z0 + SG_WIDTH] = (du_ref[...].astype(F32) * _gelu_grad(zu)).astype(BF16)
        zv = p_ref[:, _Z0 + SG_WIDTH:_Z0 + 2 * SG_WIDTH]
        dgv, gsg = _rms_bwd(_gelu(zv), sgn_ref[...], dvn_ref[...].astype(F32))
        dp_ref[:, _Z0 + SG_WIDTH:_Z0 + 2 * SG_WIDTH] = (dgv * _gelu_grad(zv)).astype(BF16)
        sums = [jnp.sum(t, axis=0, keepdims=True) for t in (gq, gkv, gsg)]

        @pl.when(i == 0)
        def _():
            for r, s in zip((dqn_ref, dkvn_ref, dsgn_ref), sums):
                r[...] = s

        @pl.when(i != 0)
        def _():
            for r, s in zip((dqn_ref, dkvn_ref, dsgn_ref), sums):
                r[...] += s

    row = lambda i: (i, 0)
    fixed = lambda i: (0, 0)
    return pl.pallas_call(
        body, name="even_prep_bwd", grid=(S // tm,),
        in_specs=[pl.BlockSpec((tm, PROJ_W), row), pl.BlockSpec((1, Q_LORA), fixed),
                  pl.BlockSpec((1, KV_LORA), fixed), pl.BlockSpec((1, SG_WIDTH), fixed)]
        + [pl.BlockSpec((tm, HP), row)] * 3
        + [pl.BlockSpec((tm, Q_LORA), row), pl.BlockSpec((tm, KV_LORA), row),
           pl.BlockSpec((tm, HEADS * HP), row), pl.BlockSpec((tm, SG_WIDTH), row),
           pl.BlockSpec((tm, SG_WIDTH), row)],
        out_specs=[pl.BlockSpec((tm, PROJ_W), row), pl.BlockSpec((1, Q_LORA), fixed),
                   pl.BlockSpec((1, KV_LORA), fixed), pl.BlockSpec((1, SG_WIDTH), fixed)],
        out_shape=[jax.ShapeDtypeStruct((S, PROJ_W), BF16), jax.ShapeDtypeStruct((1, Q_LORA), F32),
                   jax.ShapeDtypeStruct((1, KV_LORA), F32), jax.ShapeDtypeStruct((1, SG_WIDTH), F32)],
        compiler_params=_params(("arbitrary",)),
    )(proj, qn, kvn, sgn, cos, sa, sb, dcqn, dckvn, dk, du, dvn)


def _causal_mask(rows, cols):
    r = lax.broadcasted_iota(jnp.int32, (rows, cols), 0)
    c = lax.broadcasted_iota(jnp.int32, (rows, cols), 1)
    return c <= r


def _flash_fwd_call(q, k, v):
    S = q.shape[0]
    tb = _pick(S, _ATT_BLOCK, 128)
    nq = S // tb
    nt_dims = (((1,), (1,)), ((), ()))

    def body(q_ref, k_ref, v_ref, o_ref, lse_ref):
        i = pl.program_id(1)
        q_t = q_ref[...]

        def step(j, carry, masked):
            m, l, acc = carry
            off = pl.multiple_of(j * tb, tb)
            k_t = k_ref[pl.ds(off, tb), :]
            v_t = v_ref[pl.ds(off, tb), :]
            s = lax.dot_general(q_t, k_t, nt_dims, preferred_element_type=F32)
            if masked:
                s = jnp.where(_causal_mask(tb, tb), s, -1e30)
            m_new = jnp.maximum(m, jnp.max(s, axis=1, keepdims=True))
            alpha = jnp.exp(m - m_new)
            p = jnp.exp(s - m_new)
            l = alpha * l + jnp.sum(p, axis=1, keepdims=True)
            acc = alpha * acc + jnp.dot(p.astype(v_t.dtype), v_t, preferred_element_type=F32)
            return m_new, l, acc

        init = (jnp.full((tb, 1), -1e30, F32), jnp.zeros((tb, 1), F32), jnp.zeros((tb, HP), F32))
        carry = lax.fori_loop(0, i, lambda j, c: step(j, c, False), init)
        m, l, acc = step(i, carry, True)
        o_ref[...] = (acc / l).astype(o_ref.dtype)
        lse = jnp.broadcast_to(m + jnp.log(l), (tb, HP))
        lse_ref[0, 0] = jnp.transpose(lse)[0:8, :]

    return pl.pallas_call(
        body, name="flash_fwd", grid=(HEADS, nq),
        in_specs=[pl.BlockSpec((tb, HP), lambda h, i: (i, h)), pl.BlockSpec((S, HP), lambda h, i: (0, h)),
                  pl.BlockSpec((S, HP), lambda h, i: (0, h))],
        out_specs=[pl.BlockSpec((tb, HP), lambda h, i: (i, h)),
                   pl.BlockSpec((1, 1, 8, tb), lambda h, i: (h, i, 0, 0))],
        out_shape=[jax.ShapeDtypeStruct((S, HEADS * HP), q.dtype), jax.ShapeDtypeStruct((HEADS, nq, 8, tb), F32)],
        compiler_params=_params(("arbitrary", "arbitrary")),
    )(q, k, v)


def _attn_delta_call(o, do):
    S = o.shape[0]
    tb = _pick(S, _ATT_BLOCK, 128)

    def body(o_ref, do_ref, d_ref):
        d = jnp.sum(o_ref[...].astype(F32) * do_ref[...].astype(F32), axis=1, keepdims=True)
        d_ref[0, 0] = jnp.transpose(jnp.broadcast_to(d, (tb, HP)))[0:8, :]

    return pl.pallas_call(
        body, name="attn_delta", grid=(HEADS, S // tb),
        in_specs=[pl.BlockSpec((tb, HP), lambda h, i: (i, h))] * 2,
        out_specs=pl.BlockSpec((1, 1, 8, tb), lambda h, i: (h, i, 0, 0)),
        out_shape=jax.ShapeDtypeStruct((HEADS, S // tb, 8, tb), F32), compiler_params=_params(("arbitrary", "arbitrary")),
    )(o, do)


def _flash_bwd_call(q, k, v, do, lse, delta):
    S = q.shape[0]
    tb = _pick(S, _ATT_BLOCK, 128)
    nq = S // tb
    nt_dims = (((1,), (1,)), ((), ()))
    tn_dims = (((0,), (0,)), ((), ()))

    def body(q_ref, do_ref, lse_ref, dl_ref, k_ref, v_ref, dq_ref, dk_ref, dv_ref):
        j = pl.program_id(1)
        k_t = k_ref[...]
        v_t = v_ref[...]

        @pl.when(j == 0)
        def _():
            dq_ref[...] = jnp.zeros_like(dq_ref)

        def step(i, carry, masked):
            dk, dv = carry
            off = pl.multiple_of(i * tb, tb)
            q_t = q_ref[pl.ds(off, tb), :]
            do_t = do_ref[pl.ds(off, tb), :]
            lse_row = lse_ref[0, i, 0:1, :]
            dl_row = dl_ref[0, i, 0:1, :]
            st = lax.dot_general(k_t, q_t, nt_dims, preferred_element_type=F32)
            pt = jnp.exp(st - lse_row)
            if masked:
                pt = jnp.where(jnp.transpose(_causal_mask(tb, tb)), pt, 0.0)
            dpt = lax.dot_general(v_t, do_t, nt_dims, preferred_element_type=F32)
            dst = (pt * (dpt - dl_row)).astype(q_t.dtype)
            dv = dv + jnp.dot(pt.astype(do_t.dtype), do_t, preferred_element_type=F32)
            dk = dk + jnp.dot(dst, q_t, preferred_element_type=F32)
            dq_ref[pl.ds(off, tb), :] += lax.dot_general(dst, k_t, tn_dims, preferred_element_type=F32)
            return dk, dv

        zero = jnp.zeros((tb, HP), F32)
        carry = step(j, (zero, zero), True)
        dk, dv = lax.fori_loop(j + 1, nq, lambda i, c: step(i, c, False), carry)
        dk_ref[...] = dk.astype(dk_ref.dtype)
        dv_ref[...] = dv.astype(dv_ref.dtype)

    head = lambda h, j: (0, h)
    blk = lambda h, j: (j, h)
    rows = lambda h, j: (h, 0, 0, 0)
    return pl.pallas_call(
        body, name="flash_bwd", grid=(HEADS, nq),
        in_specs=[pl.BlockSpec((S, HP), head), pl.BlockSpec((S, HP), head), pl.BlockSpec((1, nq, 8, tb), rows),
                  pl.BlockSpec((1, nq, 8, tb), rows), pl.BlockSpec((tb, HP), blk), pl.BlockSpec((tb, HP), blk)],
        out_specs=[pl.BlockSpec((S, HP), head), pl.BlockSpec((tb, HP), blk), pl.BlockSpec((tb, HP), blk)],
        out_shape=[jax.ShapeDtypeStruct((S, HEADS * HP), F32), jax.ShapeDtypeStruct((S, HEADS * HP), BF16),
                   jax.ShapeDtypeStruct((S, HEADS * HP), BF16)],
        compiler_params=_params(("arbitrary", "arbitrary")),
    )(q, do, lse, delta, k, v)


def _sg_mixed(w_ref, vch, lane_lo):
    blocks = []
    for jb in range(SG_WIDTH // _LANES):
        r = jnp.dot(w_ref[jb], vch[:, jb * _LANES:(jb + 1) * _LANES], preferred_element_type=F32)
        blocks.append(jnp.where(lane_lo, r[0:SG_CHUNK], r[SG_CHUNK:2 * SG_CHUNK]))
    return jnp.concatenate(blocks, axis=1)


def _sgu_fwd_call(vn, u, attn, wst, bexp):
    S = vn.shape[0]
    tm = _pick(S, _SG_TILE, SG_CHUNK)
    AW = HEADS * HP

    def body(v_ref, u_ref, a_ref, w_ref, b_ref, mix_ref):
        lane_lo = lax.broadcasted_iota(jnp.int32, (SG_CHUNK, _LANES), 1) < SG_GDIM
        mix_ref[:, 0:AW] = a_ref[...]
        for c in range(tm // SG_CHUNK):
            rs = slice(c * SG_CHUNK, (c + 1) * SG_CHUNK)
            mixed = _sg_mixed(w_ref, v_ref[rs, :], lane_lo) + b_ref[...]
            mix_ref[rs, AW:AW + SG_WIDTH] = (u_ref[rs, :].astype(F32) * mixed).astype(mix_ref.dtype)

    row = lambda i: (i, 0)
    return pl.pallas_call(
        body, name="sgu_fwd", grid=(S // tm,),
        in_specs=[pl.BlockSpec((tm, SG_WIDTH), row), pl.BlockSpec((tm, SG_WIDTH), row), pl.BlockSpec((tm, AW), row),
                  pl.BlockSpec((SG_WIDTH // _LANES, 2 * SG_CHUNK, SG_CHUNK), lambda i: (0, 0, 0)),
                  pl.BlockSpec((SG_CHUNK, SG_WIDTH), lambda i: (0, 0))],
        out_specs=pl.BlockSpec((tm, AW + SG_WIDTH), row),
        out_shape=jax.ShapeDtypeStruct((S, AW + SG_WIDTH), BF16), compiler_params=_params(("arbitrary",)),
    )(vn, u, attn, wst, bexp)


def _sgu_bwd_call(dmix, vn, u, wst, wst_t, bexp):
    S = vn.shape[0]
    tm = _pick(S, _SG_TILE, SG_CHUNK)
    nblk = SG_WIDTH // _LANES
    col0 = (HEADS * HP) // SG_WIDTH
    nt_dims = (((1,), (1,)), ((), ()))

    def body(d_ref, v_ref, u_ref, w_ref, wt_ref, b_ref, du_ref, dv_ref, dw_ref, db_ref, dbacc_ref):
        i = pl.program_id(0)
        lane_lo = lax.broadcasted_iota(jnp.int32, (SG_CHUNK, _LANES), 1) < SG_GDIM

        @pl.when(i == 0)
        def _():
            dw_ref[...] = jnp.zeros_like(dw_ref)
            dbacc_ref[...] = jnp.zeros_like(dbacc_ref)

        for c in range(tm // SG_CHUNK):
            rs = slice(c * SG_CHUNK, (c + 1) * SG_CHUNK)
            vch = v_ref[rs, :]
            dsg = d_ref[rs, :].astype(F32)
            mixed = _sg_mixed(w_ref, vch, lane_lo) + b_ref[...]
            du_ref[rs, :] = (dsg * mixed).astype(du_ref.dtype)
            dmixed = dsg * u_ref[rs, :].astype(F32)
            dbacc_ref[...] += dmixed
            dmx = dmixed.astype(vch.dtype)
            dv_ref[rs, :] = _sg_mixed(wt_ref, dmx, lane_lo).astype(dv_ref.dtype)
            for jb in range(nblk):
                dblk = dmx[:, jb * _LANES:(jb + 1) * _LANES]
                vblk = vch[:, jb * _LANES:(jb + 1) * _LANES]
                zero = jnp.zeros_like(dblk)
                dw_ref[2 * jb] += lax.dot_general(jnp.where(lane_lo, dblk, zero), vblk, nt_dims,
                                                  preferred_element_type=F32)
                dw_ref[2 * jb + 1] += lax.dot_general(jnp.where(lane_lo, zero, dblk), vblk, nt_dims,
                                                      preferred_element_type=F32)

        @pl.when(i == pl.num_programs(0) - 1)
        def _():
            tri = _causal_mask(SG_CHUNK, SG_CHUNK)
            for g in range(SG_GROUPS):
                dw_ref[g] = jnp.where(tri, dw_ref[g], 0.0)
            lane = lax.broadcasted_iota(jnp.int32, (SG_CHUNK, _LANES), 1)
            out = jnp.zeros((SG_CHUNK, _LANES), F32)
            for g in range(SG_GROUPS):
                blk = dbacc_ref[:, (g // 2) * _LANES:(g // 2 + 1) * _LANES]
                sel = lane_lo if g % 2 == 0 else jnp.logical_not(lane_lo)
                s = jnp.sum(jnp.where(sel, blk, 0.0), axis=1, keepdims=True)
                out = jnp.where(lane == g, s, out)
            db_ref[...] = out

    row = lambda i: (i, 0)
    wspec = pl.BlockSpec((nblk, 2 * SG_CHUNK, SG_CHUNK), lambda i: (0, 0, 0))
    return pl.pallas_call(
        body, name="sgu_bwd", grid=(S // tm,),
        in_specs=[pl.BlockSpec((tm, SG_WIDTH), lambda i: (i, col0)), pl.BlockSpec((tm, SG_WIDTH), row),
                  pl.BlockSpec((tm, SG_WIDTH), row), wspec, wspec,
                  pl.BlockSpec((SG_CHUNK, SG_WIDTH), lambda i: (0, 0))],
        out_specs=[pl.BlockSpec((tm, SG_WIDTH), row), pl.BlockSpec((tm, SG_WIDTH), row),
                   pl.BlockSpec((SG_GROUPS, SG_CHUNK, SG_CHUNK), lambda i: (0, 0, 0)),
                   pl.BlockSpec((SG_CHUNK, _LANES), lambda i: (0, 0))],
        out_shape=[jax.ShapeDtypeStruct((S, SG_WIDTH), BF16), jax.ShapeDtypeStruct((S, SG_WIDTH), BF16),
                   jax.ShapeDtypeStruct((SG_GROUPS, SG_CHUNK, SG_CHUNK), F32),
                   jax.ShapeDtypeStruct((SG_CHUNK, _LANES), F32)],
        scratch_shapes=[pltpu.VMEM((SG_CHUNK, SG_WIDTH), F32)],
        compiler_params=_params(("arbitrary",)),
    )(dmix, vn, u, wst, wst_t, bexp)


def _shift_down(t, halo, n):
    rows = lax.broadcasted_iota(jnp.int32, t.shape, 0)
    out = pltpu.roll(t, n, 0)
    for r in range(n):
        out = jnp.where(rows == r, halo[8 - n + r:8 - n + r + 1, :], out)
    return out


def _shift_up(t, halo, n):
    tm = t.shape[0]
    rows = lax.broadcasted_iota(jnp.int32, t.shape, 0)
    out = pltpu.roll(t, tm - n, 0)
    for r in range(n):
        out = jnp.where(rows == tm - n + r, halo[r:r + 1, :], out)
    return out


def _conv_fwd_call(p, w):
    S, C3 = p.shape
    C = C3 // 3
    tm = _pick(S, _ROW_TILE, 8)
    hb = tm // 8

    def body(p_ref, c_prev, z_prev, w_ref, m_ref):
        i = pl.program_id(0)
        cz = p_ref[:, C:2 * C] * p_ref[:, 2 * C:3 * C]
        czp = jnp.where(i > 0, c_prev[...] * z_prev[...], 0.0)
        y = w_ref[2:3, :] * cz + w_ref[1:2, :] * _shift_down(cz, czp, 1) + w_ref[0:1, :] * _shift_down(cz, czp, 2)
        m_ref[...] = (p_ref[:, 0:C] * y).astype(m_ref.dtype)

    prev = lambda col: (lambda i: (jnp.maximum(i * hb - 1, 0), col))
    return pl.pallas_call(
        body, name="conv_fwd", grid=(S // tm,),
        in_specs=[pl.BlockSpec((tm, C3), lambda i: (i, 0)), pl.BlockSpec((8, C), prev(1)),
                  pl.BlockSpec((8, C), prev(2)), pl.BlockSpec((3, C), lambda i: (0, 0))],
        out_specs=pl.BlockSpec((tm, C), lambda i: (i, 0)),
        out_shape=jax.ShapeDtypeStruct((S, C), BF16), compiler_params=_params(("arbitrary",)),
    )(p, p, p, w)


def _conv_bwd_call(p, w, dm):
    S, C3 = p.shape
    C = C3 // 3
    tm = _pick(S, 256, 8)
    hb = tm // 8
    n_tiles = S // tm

    def body(p_ref, c_prev, z_prev, b_next, dm_ref, dm_next, w_ref, dp_ref, dw_ref):
        i = pl.program_id(0)
        b = p_ref[:, 0:C]
        c = p_ref[:, C:2 * C]
        z = p_ref[:, 2 * C:3 * C]
        cz = c * z
        czp = jnp.where(i > 0, c_prev[...] * z_prev[...], 0.0)
        s1 = _shift_down(cz, czp, 1)
        s2 = _shift_down(cz, czp, 2)
        w0, w1, w2 = w_ref[0:1, :], w_ref[1:2, :], w_ref[2:3, :]
        y = w2 * cz + w1 * s1 + w0 * s2
        dm_t = dm_ref[...]
        dy = dm_t * b
        dyn = jnp.where(i < n_tiles - 1, dm_next[...] * b_next[...], 0.0)
        dcz = w2 * dy + w1 * _shift_up(dy, dyn, 1) + w0 * _shift_up(dy, dyn, 2)
        dp_ref[:, 0:C] = (dm_t * y).astype(dp_ref.dtype)
        dp_ref[:, C:2 * C] = (dcz * z).astype(dp_ref.dtype)
        dp_ref[:, 2 * C:3 * C] = (dcz * c).astype(dp_ref.dtype)
        dw = jnp.concatenate([jnp.sum(dy * s2, axis=0, keepdims=True), jnp.sum(dy * s1, axis=0, keepdims=True),
                              jnp.sum(dy * cz, axis=0, keepdims=True)], axis=0)

        @pl.when(i == 0)
        def _():
            dw_ref[...] = dw

        @pl.when(i != 0)
        def _():
            dw_ref[...] += dw

    prev = lambda col: (lambda i: (jnp.maximum(i * hb - 1, 0), col))
    nxt = lambda col: (lambda i: (jnp.minimum((i + 1) * hb, S // 8 - 1), col))
    return pl.pallas_call(
        body, name="conv_bwd", grid=(n_tiles,),
        in_specs=[pl.BlockSpec((tm, C3), lambda i: (i, 0)), pl.BlockSpec((8, C), prev(1)),
                  pl.BlockSpec((8, C), prev(2)), pl.BlockSpec((8, C), nxt(0)),
                  pl.BlockSpec((tm, C), lambda i: (i, 0)), pl.BlockSpec((8, C), nxt(0)),
                  pl.BlockSpec((3, C), lambda i: (0, 0))],
        out_specs=[pl.BlockSpec((tm, C3), lambda i: (i, 0)), pl.BlockSpec((3, C), lambda i: (0, 0))],
        out_shape=[jax.ShapeDtypeStruct((S, C3), BF16), jax.ShapeDtypeStruct((3, C), F32)],
        compiler_params=_params(("arbitrary",)),
    )(p, p, p, p, dm, dm, w)


def _my_place():
    return lax.axis_index("x"), lax.axis_index("y"), lax.axis_index("c")


def _gather_weights_call(name, shard):
    R, C = shard.shape

    def body(s_ref, o_ref, send_sems, recv_sems, local_sem):
        x, y, c = _my_place()
        mine = 2 * x + y
        local = pltpu.make_async_copy(s_ref, o_ref.at[mine], local_sem)
        local.start()
        peers = [(1 - x, y), (x, 1 - y), (1 - x, 1 - y)]
        copies = []
        for k, (px, py) in enumerate(peers):
            cp = pltpu.make_async_remote_copy(src_ref=s_ref, dst_ref=o_ref.at[mine], send_sem=send_sems.at[k],
                                              recv_sem=recv_sems.at[k], device_id=(px, py, c), device_id_type=MESH)
            cp.start()
            copies.append(cp)
        for k, (px, py) in enumerate(peers):
            pltpu.make_async_remote_copy(src_ref=s_ref, dst_ref=o_ref.at[2 * px + py], send_sem=send_sems.at[k],
                                         recv_sem=recv_sems.at[k], device_id=(px, py, c),
                                         device_id_type=MESH).wait_recv()
        for cp in copies:
            cp.wait_send()
        local.wait()

    any_spec = pl.BlockSpec(memory_space=pl.ANY)
    return pl.pallas_call(
        body, name=name, in_specs=[any_spec], out_specs=any_spec,
        out_shape=jax.ShapeDtypeStruct((4, R, C), shard.dtype),
        scratch_shapes=[pltpu.SemaphoreType.DMA((3,)), pltpu.SemaphoreType.DMA((3,)), pltpu.SemaphoreType.DMA],
        compiler_params=pltpu.CompilerParams(has_side_effects=True),
    )(shard)


_D2D_CHUNKS = 4


_LOCAL_CHUNKS = 8


def _local_copies(src_of, dst_of, rows, sems, base):
    rc = rows // _LOCAL_CHUNKS
    assert rc * _LOCAL_CHUNKS == rows and rc % 16 == 0, rows
    out = []
    for j in range(_LOCAL_CHUNKS):
        sl = pl.ds(j * rc, rc)
        out.append(pltpu.make_async_copy(src_of(sl), dst_of(sl), sems.at[base + j]))
    return out


def _comm_call(name, body, arrays, out_shapes, sem_counts, aliases=None):
    any_spec = pl.BlockSpec(memory_space=pl.ANY)
    return pl.pallas_call(
        body, name=name, in_specs=[any_spec] * len(arrays), out_specs=[any_spec] * len(out_shapes),
        out_shape=out_shapes, scratch_shapes=[pltpu.SemaphoreType.DMA((n,)) for n in sem_counts],
        input_output_aliases=aliases or {}, compiler_params=pltpu.CompilerParams(has_side_effects=True),
    )(*arrays)


def _gather_halves_call(shards):
    na = len(shards)
    for s in shards:
        assert s.shape[0] % (2 * _D2D_CHUNKS * 16) == 0, s.shape

    def body(*refs):
        s_refs, o_refs = refs[:na], refs[na:2 * na]
        ici_send, ici_recv, d2d_send, d2d_recv = refs[2 * na:]
        x, y, c = _my_place()
        mine = 2 * x + y
        chips = [(1 - x, y), (x, 1 - y), (1 - x, 1 - y)]

        def ici(a, k, chip, block):
            Rh = s_refs[a].shape[1] // 2
            my_half = pl.ds(pl.multiple_of(c * Rh, 16), Rh)
            return pltpu.make_async_remote_copy(src_ref=s_refs[a].at[mine, my_half],
                                                dst_ref=o_refs[a].at[block, my_half],
                                                send_sem=ici_send.at[3 * a + k], recv_sem=ici_recv.at[3 * a + k],
                                                device_id=(chip[0], chip[1], c), device_id_type=MESH)

        def d2d(a, k, j, block, half):
            Rh = s_refs[a].shape[1] // 2
            rc = Rh // _D2D_CHUNKS
            rows = pl.ds(pl.multiple_of(half * Rh + j * rc, 16), rc)
            idx = (3 * a + k) * _D2D_CHUNKS + j
            return pltpu.make_async_remote_copy(src_ref=o_refs[a].at[block, rows], dst_ref=o_refs[a].at[block, rows],
                                                send_sem=d2d_send.at[idx], recv_sem=d2d_recv.at[idx],
                                                device_id=(x, y, 1 - c), device_id_type=MESH)

        sends = [ici(a, k, chip, mine) for a in range(na) for k, chip in enumerate(chips)]
        for cp in sends:
            cp.start()
        for a in range(na):
            for k, chip in enumerate(chips):
                block = 2 * chip[0] + chip[1]
                ici(a, k, chip, block).wait_recv()
                for j in range(_D2D_CHUNKS):
                    cp = d2d(a, k, j, block, c)
                    cp.start()
                    sends.append(cp)
        for a in range(na):
            for k, chip in enumerate(chips):
                for j in range(_D2D_CHUNKS):
                    d2d(a, k, j, 2 * chip[0] + chip[1], 1 - c).wait_recv()
        for cp in sends:
            cp.wait_send()

    start = [jnp.broadcast_to(s[None], (4,) + tuple(s.shape)) for s in shards]
    outs = [jax.ShapeDtypeStruct(t.shape, t.dtype) for t in start]
    n_d2d = 3 * na * _D2D_CHUNKS
    return _comm_call("gather_weights", body, start, outs, [3 * na, 3 * na, n_d2d, n_d2d],
                      aliases={a: a for a in range(na)})


def _pair_exchange_call(packed):
    na = len(packed)

    def body(*refs):
        p_refs, o_refs = refs[:na], refs[na:2 * na]
        send_sems, recv_sems = refs[2 * na:]
        x, y, c = _my_place()
        copies = []
        for a in range(na):
            nb, _, Rh, _ = p_refs[a].shape
            rc = Rh // _D2D_CHUNKS
            assert rc * _D2D_CHUNKS == Rh and rc % 16 == 0
            for b in range(nb):
                for j in range(_D2D_CHUNKS):
                    rows = pl.ds(j * rc, rc)
                    idx = (a * nb + b) * _D2D_CHUNKS + j
                    copies.append(pltpu.make_async_remote_copy(
                        src_ref=p_refs[a].at[b, 1 - c, rows], dst_ref=o_refs[a].at[b, rows],
                        send_sem=send_sems.at[idx], recv_sem=recv_sems.at[idx],
                        device_id=(x, y, 1 - c), device_id_type=MESH))
        for t in copies:
            t.start()
        for t in copies:
            t.wait_recv()
        for t in copies:
            t.wait_send()

    outs = [jax.ShapeDtypeStruct((p.shape[0], p.shape[2], p.shape[3]), p.dtype) for p in packed]
    n = sum(p.shape[0] for p in packed) * _D2D_CHUNKS
    return _comm_call("pair_exchange", body, packed, outs, [n, n])


def _pair_add_call(name, packed, other, core):
    nb, _, Rh, C = packed.shape
    tr = _pick(Rh, 512, 16)

    def body(c_ref, p_ref, o_ref, q_ref):
        q_ref[...] = (p_ref[...].astype(F32) + o_ref[...].astype(F32)).astype(q_ref.dtype)

    grid_spec = pltpu.PrefetchScalarGridSpec(
        num_scalar_prefetch=1, grid=(nb, Rh // tr),
        in_specs=[pl.BlockSpec((None, None, tr, C), lambda b, r, c_ref: (b, c_ref[0], r, 0)),
                  pl.BlockSpec((None, tr, C), lambda b, r, c_ref: (b, r, 0))],
        out_specs=pl.BlockSpec((None, tr, C), lambda b, r, c_ref: (b, r, 0)))
    return pl.pallas_call(
        body, name=name, grid_spec=grid_spec, out_shape=jax.ShapeDtypeStruct((nb, Rh, C), packed.dtype),
        compiler_params=_params(("arbitrary", "arbitrary")),
    )(core, packed, other)


def _chip_scatter_call(pairs):
    na = len(pairs)

    def body(*refs):
        p_refs, o_refs = refs[:na], refs[na:2 * na]
        send_sems, recv_sems, local_sems = refs[2 * na:]
        x, y, c = _my_place()
        mine = 2 * x + y
        chips = [(1 - x, y), (x, 1 - y), (1 - x, 1 - y)]
        pending = []
        for a in range(na):
            p_ref, o_ref = p_refs[a], o_refs[a]
            pending += _local_copies(lambda sl: p_ref.at[mine, sl], lambda sl: o_ref.at[mine, sl], p_ref.shape[1],
                                     local_sems, a * _LOCAL_CHUNKS)
        for t in pending:
            t.start()
        copies = []
        for a in range(na):
            for k, (px, py) in enumerate(chips):
                t = pltpu.make_async_remote_copy(src_ref=p_refs[a].at[2 * px + py], dst_ref=o_refs[a].at[mine],
                                                 send_sem=send_sems.at[3 * a + k], recv_sem=recv_sems.at[3 * a + k],
                                                 device_id=(px, py, c), device_id_type=MESH)
                t.start()
                copies.append(t)
        for a in range(na):
            for k, (px, py) in enumerate(chips):
                pltpu.make_async_remote_copy(src_ref=p_refs[a].at[mine], dst_ref=o_refs[a].at[2 * px + py],
                                             send_sem=send_sems.at[3 * a + k], recv_sem=recv_sems.at[3 * a + k],
                                             device_id=(px, py, c), device_id_type=MESH).wait_recv()
        for t in copies:
            t.wait_send()
        for t in pending:
            t.wait()

    outs = [jax.ShapeDtypeStruct(p.shape, p.dtype) for p in pairs]
    return _comm_call("chip_scatter", body, pairs, outs, [3 * na, 3 * na, na * _LOCAL_CHUNKS])


def _sum_slots_call(name, parts, core):
    n, R, C = parts.shape
    tr = _pick(R, 256, 8)

    def body(c_ref, p_ref, o_ref):
        acc = p_ref[0].astype(F32)
        for s in range(1, n):
            acc = acc + p_ref[s].astype(F32)
        o_ref[...] = acc

    grid_spec = pltpu.PrefetchScalarGridSpec(
        num_scalar_prefetch=1, grid=(R // tr,),
        in_specs=[pl.BlockSpec((n, tr, C), lambda i, c_ref: (0, i, 0))],
        out_specs=pl.BlockSpec((None, tr, C), lambda i, c_ref: (c_ref[0], i, 0)))
    return pl.pallas_call(
        body, name=name, grid_spec=grid_spec, out_shape=jax.ShapeDtypeStruct((2, R, C), F32),
        compiler_params=_params(("arbitrary",)),
    )(core, parts)


def _sibling_share_call(halves):
    na = len(halves)
    nch = 2 * _D2D_CHUNKS

    def body(*refs):
        h_refs, o_refs = refs[:na], refs[na:2 * na]
        send_sems, recv_sems = refs[2 * na:]
        x, y, c = _my_place()

        def cp(a, j, slot):
            rc = h_refs[a].shape[1] // nch
            rows = pl.ds(j * rc, rc)
            return pltpu.make_async_remote_copy(src_ref=h_refs[a].at[slot, rows], dst_ref=o_refs[a].at[slot, rows],
                                                send_sem=send_sems.at[a * nch + j], recv_sem=recv_sems.at[a * nch + j],
                                                device_id=(x, y, 1 - c), device_id_type=MESH)

        copies = [cp(a, j, c) for a in range(na) for j in range(nch)]
        for t in copies:
            t.start()
        for a in range(na):
            for j in range(nch):
                cp(a, j, 1 - c).wait_recv()
        for t in copies:
            t.wait_send()

    for h in halves:
        assert h.shape[1] % (nch * 8) == 0, h.shape
    outs = [jax.ShapeDtypeStruct(h.shape, h.dtype) for h in halves]
    return _comm_call("sibling_share", body, halves, outs, [na * nch, na * nch], aliases={a: a for a in range(na)})


def _allreduce_small_call(part):
    R, C = part.shape

    def body(p_ref, o_ref, slots, send_sems, recv_sems):
        x, y, c = _my_place()
        me = 4 * x + 2 * y + c
        peers = []
        for k in range(1, 8):
            px = x ^ (k >> 2) if (k >> 2) else x
            py = y ^ ((k >> 1) & 1) if ((k >> 1) & 1) else y
            pc = c ^ (k & 1) if (k & 1) else c
            peers.append((px, py, pc))
        copies = []
        for k, (px, py, pc) in enumerate(peers):
            cp = pltpu.make_async_remote_copy(src_ref=p_ref, dst_ref=slots.at[me], send_sem=send_sems.at[k],
                                              recv_sem=recv_sems.at[k], device_id=(px, py, pc), device_id_type=MESH)
            cp.start()
            copies.append(cp)
        slots[me] = p_ref[...]
        for k, (px, py, pc) in enumerate(peers):
            pltpu.make_async_remote_copy(src_ref=p_ref, dst_ref=slots.at[4 * px + 2 * py + pc],
                                         send_sem=send_sems.at[k], recv_sem=recv_sems.at[k],
                                         device_id=(px, py, pc), device_id_type=MESH).wait_recv()
        for cp in copies:
            cp.wait_send()
        acc = slots[0]
        for s in range(1, 8):
            acc = acc + slots[s]
        o_ref[...] = acc

    vm = pl.BlockSpec(memory_space=pltpu.VMEM)
    return pl.pallas_call(
        body, name="allreduce_small", in_specs=[vm], out_specs=vm,
        out_shape=jax.ShapeDtypeStruct((R, C), F32),
        scratch_shapes=[pltpu.VMEM((8, R, C), F32), pltpu.SemaphoreType.DMA((7,)), pltpu.SemaphoreType.DMA((7,))],
        compiler_params=pltpu.CompilerParams(has_side_effects=True, vmem_limit_bytes=_VMEM_LIMIT),
    )(part)


def _adamw_call(name, w, g, m, v):
    shape = w.shape
    cols = shape[-1]
    rows = int(np.prod(shape[:-1])) if len(shape) > 1 else 1
    w2, g2, m2, v2 = (t.reshape(rows, cols) for t in (w, g, m, v))
    tr = _pick(rows, 256, 8)
    c1 = 1.0 / (1.0 - ADAM_B1 ** ADAM_STEP)
    c2 = 1.0 / (1.0 - ADAM_B2 ** ADAM_STEP)

    def body(w_ref, g_ref, m_ref, v_ref, d_ref, nm_ref, nv_ref):
        gr = g_ref[...]
        m_new = ADAM_B1 * m_ref[...] + (1.0 - ADAM_B1) * gr
        v_new = ADAM_B2 * v_ref[...] + (1.0 - ADAM_B2) * (gr * gr)
        m_hat = m_new / (1.0 - ADAM_B1 ** ADAM_STEP)
        v_hat = v_new / (1.0 - ADAM_B2 ** ADAM_STEP)
        d_ref[...] = -ADAM_LR * (m_hat / (jnp.sqrt(v_hat) + ADAM_EPS) + ADAM_WD * w_ref[...])
        nm_ref[...] = m_new
        nv_ref[...] = v_new

    spec = pl.BlockSpec((tr, cols), lambda i: (i, 0))
    d, nm, nv = pl.pallas_call(
        body, name=name, grid=(rows // tr,), in_specs=[spec] * 4, out_specs=[spec] * 3,
        out_shape=[jax.ShapeDtypeStruct((rows, cols), F32)] * 3, compiler_params=_params(("arbitrary",)),
    )(w2, g2, m2, v2)
    return d.reshape(shape), nm.reshape(shape), nv.reshape(shape)


def _pad_rows(flat, mult):
    n = flat.shape[0]
    unit = PACK_COLS * mult
    total = -(-n // unit) * unit
    return jnp.pad(flat, (0, total - n)).reshape(total // PACK_COLS, PACK_COLS)


def _pad_axis(arr, axis, mult):
    n = arr.shape[axis]
    total = -(-n // mult) * mult
    if total == n:
        return arr
    widths = [(0, 0)] * arr.ndim
    widths[axis] = (0, total - n)
    return jnp.pad(arr, widths)


def _shard_slice(arr, axis, blk, nblk=4):
    w = arr.shape[axis] // nblk
    return lax.slice_in_dim(arr, blk * w, (blk + 1) * w, axis=axis)


_NT = (((1,), (1,)), ((), ()))
_TN = (((0,), (0,)), ((), ()))


def _ffn_fwd(tag, x, h, wg, wu, wd, g_next):
    S, D = x.shape
    (wg, gi), (wu, ui), (wd, di) = wg, wu, wd
    NB, Fb = wg.shape[0], wg.shape[2]
    tm = _pick(S, 1024, 8)

    def gate_up(h_ref, wg_ref, wu_ref, a_ref, u_ref, s_ref):
        h_t = _mx(h_ref[...])
        a = jnp.dot(h_t, _mx(wg_ref[...]), preferred_element_type=F32)
        u = jnp.dot(h_t, _mx(wu_ref[...]), preferred_element_type=F32)
        sig = _sigmoid(a)
        silu = a * sig
        a_ref[...] = (u * (sig * (1.0 + a * (1.0 - sig)))).astype(a_ref.dtype)
        u_ref[...] = silu.astype(u_ref.dtype)
        s_ref[...] = (silu * u).astype(s_ref.dtype)

    hid = pl.BlockSpec((None, tm, Fb), lambda b, i: (b, i, 0))
    a, u, s = pl.pallas_call(
        gate_up, name=tag + "_gate_up", grid=(NB, S // tm),
        in_specs=[pl.BlockSpec((tm, D), lambda b, i: (i, 0)), pl.BlockSpec((None, D, Fb), lambda b, i: (b, gi, 0)),
                  pl.BlockSpec((None, D, Fb), lambda b, i: (b, ui, 0))], out_specs=[hid] * 3,
        out_shape=[jax.ShapeDtypeStruct((NB, S, Fb), BF16)] * 3, compiler_params=_params(("arbitrary", "arbitrary")),
    )(h, wg, wu)

    tm2 = _pick(S, 512, 8)

    def down(s_ref, wd_ref, x_ref, g_ref, xo_ref, ho_ref):
        acc = jnp.dot(_mx(s_ref[0]), _mx(wd_ref[0]), preferred_element_type=F32)
        for b in range(1, NB):
            acc = acc + jnp.dot(_mx(s_ref[b]), _mx(wd_ref[b]), preferred_element_type=F32)
        x_new = x_ref[...] + 0.5 * acc
        xo_ref[...] = x_new
        ho_ref[...] = (_rms_stats(x_new)[0] * g_ref[...]).astype(ho_ref.dtype)

    row = pl.BlockSpec((tm2, D), lambda i: (i, 0))
    x_new, h_next = pl.pallas_call(
        down, name=tag + "_down", grid=(S // tm2,),
        in_specs=[pl.BlockSpec((NB, tm2, Fb), lambda i: (0, i, 0)), pl.BlockSpec((NB, Fb, D), lambda i: (0, di, 0)),
                  row, pl.BlockSpec((1, D), lambda i: (0, 0))],
        out_specs=[row, row], out_shape=[jax.ShapeDtypeStruct((S, D), F32), jax.ShapeDtypeStruct((S, D), BF16)],
        compiler_params=_params(("arbitrary",)),
    )(s, wd, x, g_next)
    return x_new, h_next, (x, h, a, u, s)


def _ffn_bwd(tag, saved, dx_out, dxb, wg, wu, wd, gain):
    x, h, a, u, s = saved
    S, D = x.shape
    (wg, gi), (wu, ui), (wd, di) = wg, wu, wd
    NB, Fb = wg.shape[0], wg.shape[2]
    tm = _pick(S, 1024, 8)
    tk = _pick(S, 1024, 128)
    nk = S // tk

    def dgate_up(d_ref, wd_ref, a_ref, u_ref, da_ref, du_ref):
        ds = 0.5 * lax.dot_general(_mx(d_ref[...]), _mx(wd_ref[...]), _NT, preferred_element_type=F32)
        da_ref[...] = (ds * a_ref[...].astype(F32)).astype(da_ref.dtype)
        du_ref[...] = (ds * u_ref[...].astype(F32)).astype(du_ref.dtype)

    hid = pl.BlockSpec((None, tm, Fb), lambda b, i: (b, i, 0))
    da, du = pl.pallas_call(
        dgate_up, name=tag + "_dgate_up", grid=(NB, S // tm),
        in_specs=[pl.BlockSpec((tm, D), lambda b, i: (i, 0)), pl.BlockSpec((None, Fb, D), lambda b, i: (b, di, 0)),
                  hid, hid],
        out_specs=[hid, hid], out_shape=[jax.ShapeDtypeStruct((NB, S, Fb), BF16)] * 2,
        compiler_params=_params(("arbitrary", "arbitrary")),
    )(dxb, wd, a, u)

    def dw_down(s_ref, d_ref, o_ref, acc_ref):
        k = pl.program_id(1)
        p = lax.dot_general(_mx(s_ref[...]), _mx(d_ref[...]), _TN, preferred_element_type=F32)

        @pl.when(k == 0)
        def _():
            acc_ref[...] = p

        @pl.when(k != 0)
        def _():
            acc_ref[...] += p

        @pl.when(k == nk - 1)
        def _():
            o_ref[...] = (0.5 * acc_ref[...]).astype(o_ref.dtype)

    hk = pl.BlockSpec((None, tk, Fb), lambda b, k: (b, k, 0))
    dwd = pl.pallas_call(
        dw_down, name=tag + "_dw_down", grid=(NB, nk),
        in_specs=[hk, pl.BlockSpec((tk, D), lambda b, k: (k, 0))],
        out_specs=pl.BlockSpec((None, Fb, D), lambda b, k: (b, 0, 0)),
        out_shape=jax.ShapeDtypeStruct((NB, Fb, D), _WIRE_DTYPE), scratch_shapes=[pltpu.VMEM((Fb, D), F32)],
        compiler_params=_params(("arbitrary", "arbitrary")),
    )(s, dxb)

    def dw_gate_up(h_ref, da_ref, du_ref, og_ref, ou_ref, accg_ref, accu_ref):
        k = pl.program_id(1)
        h_t = _mx(h_ref[...])
        pg = lax.dot_general(h_t, _mx(da_ref[...]), _TN, preferred_element_type=F32)
        pu = lax.dot_general(h_t, _mx(du_ref[...]), _TN, preferred_element_type=F32)

        @pl.when(k == 0)
        def _():
            accg_ref[...] = pg
            accu_ref[...] = pu

        @pl.when(k != 0)
        def _():
            accg_ref[...] += pg
            accu_ref[...] += pu

        @pl.when(k == nk - 1)
        def _():
            og_ref[...] = accg_ref[...].astype(og_ref.dtype)
            ou_ref[...] = accu_ref[...].astype(ou_ref.dtype)

    wout = pl.BlockSpec((None, D, Fb), lambda b, k: (b, 0, 0))
    dwg, dwu = pl.pallas_call(
        dw_gate_up, name=tag + "_dw_gate_up", grid=(NB, nk),
        in_specs=[pl.BlockSpec((tk, D), lambda b, k: (k, 0)), hk, hk], out_specs=[wout, wout],
        out_shape=[jax.ShapeDtypeStruct((NB, D, Fb), _WIRE_DTYPE)] * 2,
        scratch_shapes=[pltpu.VMEM((D, Fb), F32)] * 2, compiler_params=_params(("arbitrary", "arbitrary")),
    )(h, da, du)

    tm2 = _pick(S, 512, 8)

    def dx_body(da_ref, du_ref, wg_hbm, wu_hbm, x_ref, dxo_ref, g_ref, dx_ref, dxb_ref, dg_ref, wg_v, wu_v, sem):
        i = pl.program_id(0)

        @pl.when(i == 0)
        def _():
            cg = pltpu.make_async_copy(wg_hbm.at[:, pl.ds(gi * D, D), :], wg_v, sem.at[0])
            cu = pltpu.make_async_copy(wu_hbm.at[:, pl.ds(ui * D, D), :], wu_v, sem.at[1])
            cg.start()
            cu.start()
            cg.wait()
            cu.wait()

        dh = None
        for b in range(NB):
            t = lax.dot_general(_mx(da_ref[b]), wg_v[b], _NT, preferred_element_type=F32)
            t = t + lax.dot_general(_mx(du_ref[b]), wu_v[b], _NT, preferred_element_type=F32)
            dh = t if dh is None else dh + t
        dx_n, dg = _rms_bwd(x_ref[...], g_ref[...], dh)
        dx = dxo_ref[...] + dx_n
        dx_ref[...] = dx
        dxb_ref[...] = dx.astype(dxb_ref.dtype)
        dg = jnp.sum(dg, axis=0, keepdims=True)

        @pl.when(i == 0)
        def _():
            dg_ref[...] = dg

        @pl.when(i != 0)
        def _():
            dg_ref[...] += dg

    row = pl.BlockSpec((tm2, D), lambda i: (i, 0))
    hid2 = pl.BlockSpec((NB, tm2, Fb), lambda i: (0, i, 0))
    anyspec = pl.BlockSpec(memory_space=pl.ANY)
    fixed = pl.BlockSpec((1, D), lambda i: (0, 0))
    dx, dxb_new, dgain = pl.pallas_call(
        dx_body, name=tag + "_dx", grid=(S // tm2,),
        in_specs=[hid2, hid2, anyspec, anyspec, row, row, fixed], out_specs=[row, row, fixed],
        out_shape=[jax.ShapeDtypeStruct((S, D), F32), jax.ShapeDtypeStruct((S, D), BF16),
                   jax.ShapeDtypeStruct((1, D), F32)],
        scratch_shapes=[pltpu.VMEM((NB, D, Fb), wg.dtype), pltpu.VMEM((NB, D, Fb), wu.dtype),
                        pltpu.SemaphoreType.DMA((2,))],
        compiler_params=_params(("arbitrary",)),
    )(da, du, wg, wu, x, dx_out, gain)
    return dx, dxb_new, dgain, dwg, dwu, dwd


def _conv_mixer_fwd(tag, x, h, w_in, w_taps, w_out, g_next):
    S, D = x.shape
    C3 = w_in.shape[1]
    tm = _pick(S, 512, 8)
    p, = _fused_matmul(tag + "_in", 'nn', [_op(h)], [_op(w_in)], [(0, 0, 0)], 1, _ident_epi(), [F32],
                       S, C3, D, tm, _pick(C3, 1024, 128), D)
    m = _conv_fwd_call(p, w_taps)
    x_new, h_next = _fused_matmul(tag + "_out", 'nn', [_op(m)], [_op(w_out)], [(0, 0, 0)], 1, _resid_norm_epi(1.0),
                                  [F32, BF16], S, D, D, tm, D, D, tile_extras=[x], row_extras=[g_next])
    return x_new, h_next, (x, h, p, m)


def _conv_mixer_bwd(tag, saved, dx_out, dxb, w_in, w_taps, w_out, gain):
    x, h, p, m = saved
    S, D = x.shape
    C3 = w_in.shape[1]
    tm = _pick(S, 512, 8)
    tk = _pick(S, 512, 128)
    dm, = _fused_matmul(tag + "_dm", 'nt', [_op(dxb)], [_op(w_out)], [(0, 0, 0)], 1, _ident_epi(), [F32],
                        S, D, D, tm, D, D)
    dw_out, = _fused_matmul(tag + "_dw_out", 'tn', [_op(m)], [_op(dxb)], [(0, 0, 0)], 1, _ident_epi(), [F32],
                            D, D, S, D, D, tk)
    dp, dtaps = _conv_bwd_call(p, w_taps, dm)
    dw_in, = _fused_matmul(tag + "_dw_in", 'tn', [_op(h)], [_op(dp)], [(0, 0, 0)], 1, _ident_epi(), [F32],
                           D, C3, S, D, _pick(C3, 1024, 128), tk)
    dx, dxb_new, dgain = _fused_matmul(tag + "_dx", 'nt', [_op(dp)], [_op(w_in)], [(0, 0, 0)], 1, _norm_bwd_epi,
                                       [F32, BF16], S, D, C3, tm, D, _pick(C3, 1024, 128),
                                       tile_extras=[x, dx_out], row_extras=[gain], n_colsum=1)
    return dx, dxb_new, dgain, dw_in, dtaps, dw_out


def _attn_scale():
    return np.float32(QK_DIM ** -0.5)


def _even_mixer_fwd(tag, x, h, wts, tables, g_next):
    S, D = x.shape
    cos, sa, sb = tables
    tm = _pick(S, 512, 8)
    AW = HEADS * HP
    proj, = _fused_matmul(tag + "_in", 'nn', [_op(h)], [_op(wts['w_in'])], [(0, 0, 0)], 1, _ident_epi(), [F32],
                          S, PROJ_W, D, tm, _pick(PROJ_W, 896, 128), D)
    cqn, ckvn, kr, u, vn = _even_prep_call(proj, wts['q_norm'], wts['kv_norm'], wts['sg_norm'], cos, sa, sb)
    scale = _attn_scale()

    def q_epi(accs, tiles, rows, mrows):
        c_t, a_t, b_t = mrows
        heads = [_rope(accs[0][:, hh * HP:(hh + 1) * HP], c_t, a_t, b_t) * scale for hh in range(HEADS)]
        return [jnp.concatenate(heads, axis=1)]

    q, = _fused_matmul(tag + "_q", 'nn', [_op(cqn)], [_op(wts['w_q'])], [(0, 0, 0)], 1, q_epi, [BF16],
                       S, AW, Q_LORA, tm, AW, Q_LORA, mrow_extras=[cos, sa, sb])

    def kv_epi(accs, tiles, rows, mrows):
        return [accs[0] + jnp.concatenate([mrows[0].astype(F32)] * HEADS, axis=1), accs[1]]

    k, v = _fused_matmul(tag + "_kv", 'nn', [_op(ckvn)], [_op(wts['w_k']), _op(wts['w_v'])],
                         [(0, 0, 0), (0, 1, 1)], 2, kv_epi, [BF16, BF16], S, AW, KV_LORA, tm, AW, KV_LORA,
                         mrow_extras=[kr])
    o, lse = _flash_fwd_call(q, k, v)
    mix = _sgu_fwd_call(vn, u, o, wts['sg_wst'], wts['sg_bexp'])
    x_new, h_next = _fused_matmul(tag + "_out", 'nn', [_op(mix)], [_op(wts['w_out'])], [(0, 0, 0)], 1,
                                  _resid_norm_epi(1.0), [F32, BF16], S, D, AW + SG_WIDTH, tm, D, AW + SG_WIDTH,
                                  tile_extras=[x], row_extras=[g_next])
    return x_new, h_next, (x, h, proj, cqn, ckvn, u, vn, q, k, v, o, lse, mix)


def _even_mixer_bwd(tag, saved, dx_out, dxb, wts, tables, gain):
    x, h, proj, cqn, ckvn, u, vn, q, k, v, o, lse, mix = saved
    S, D = x.shape
    cos, sa, sb = tables
    tm = _pick(S, 512, 8)
    tk = _pick(S, 512, 128)
    AW = HEADS * HP
    MW = AW + SG_WIDTH
    dmix, = _fused_matmul(tag + "_dmix", 'nt', [_op(dxb)], [_op(wts['w_out'])], [(0, 0, 0)], 1, _ident_epi(), [BF16],
                          S, MW, D, tm, _pick(MW, 768, 128), D)
    dw_out, = _fused_matmul(tag + "_dw_out", 'tn', [_op(mix)], [_op(dxb)], [(0, 0, 0)], 1, _ident_epi(), [F32],
                            MW, D, S, _pick(MW, 768, 128), D, tk)
    du, dvn, dsg_w, dsg_b = _sgu_bwd_call(dmix, vn, u, wts['sg_wst'], wts['sg_wst_t'], wts['sg_bexp'])
    delta = _attn_delta_call(o, dmix)
    dq, dk, dv = _flash_bwd_call(q, k, v, dmix, lse, delta)
    scale = _attn_scale()

    def dq_epi(accs, tiles, rows, mrows):
        return accs

    def dq_pre_call():
        tr = _pick(S, 256, 8)

        def body(d_ref, c_ref, a_ref, b_ref, o_ref):
            for hh in range(HEADS):
                t = _rope_t(d_ref[:, hh * HP:(hh + 1) * HP], c_ref[...], a_ref[...], b_ref[...]) * scale
                o_ref[:, hh * HP:(hh + 1) * HP] = t.astype(o_ref.dtype)

        row = lambda i: (i, 0)
        return pl.pallas_call(
            body, name=tag + "_dq_unrope", grid=(S // tr,),
            in_specs=[pl.BlockSpec((tr, AW), row)] + [pl.BlockSpec((tr, HP), row)] * 3,
            out_specs=pl.BlockSpec((tr, AW), row), out_shape=jax.ShapeDtypeStruct((S, AW), BF16),
            compiler_params=_params(("arbitrary",)),
        )(dq, cos, sa, sb)

    dqp = dq_pre_call()
    dw_q, = _fused_matmul(tag + "_dw_q", 'tn', [_op(cqn)], [_op(dqp)], [(0, 0, 0)], 1, _ident_epi(), [F32],
                          Q_LORA, AW, S, Q_LORA, AW, tk)
    dcqn, = _fused_matmul(tag + "_dcq", 'nt', [_op(dqp)], [_op(wts['w_q'])], [(0, 0, 0)], 1, dq_epi, [F32],
                          S, Q_LORA, AW, tm, Q_LORA, AW)
    dw_k, dw_v = _fused_matmul(tag + "_dw_kv", 'tn', [_op(ckvn)], [_op(dk), _op(dv)], [(0, 0, 0), (0, 1, 1)], 2,
                               _ident_epi(), [F32, F32], KV_LORA, AW, S, KV_LORA, AW, tk)
    dckvn, = _fused_matmul(tag + "_dckv", 'nt', [_op(dk), _op(dv)], [_op(wts['w_k']), _op(wts['w_v'])],
                           [(0, 0, 0), (1, 1, 0)], 1, dq_epi, [F32], S, KV_LORA, AW, tm, KV_LORA, AW)
    dproj, dqn, dkvn, dsgn = _even_prep_bwd_call(proj, wts['q_norm'], wts['kv_norm'], wts['sg_norm'], cos, sa, sb,
                                                 dcqn, dckvn, dk, du, dvn)
    dw_in, = _fused_matmul(tag + "_dw_in", 'tn', [_op(h)], [_op(dproj)], [(0, 0, 0)], 1, _ident_epi(), [F32],
                           D, PROJ_W, S, D, _pick(PROJ_W, 896, 128), tk)
    dx, dxb_new, dgain = _fused_matmul(tag + "_dx", 'nt', [_op(dproj)], [_op(wts['w_in'])], [(0, 0, 0)], 1,
                                       _norm_bwd_epi, [F32, BF16], S, D, PROJ_W, tm, D, _pick(PROJ_W, 896, 128),
                                       tile_extras=[x, dx_out], row_extras=[gain], n_colsum=1)
    grads = dict(w_in=dw_in, w_q=dw_q, w_k=dw_k, w_v=dw_v, w_out=dw_out, q_norm=dqn, kv_norm=dkvn, sg_norm=dsgn,
                 sg_w=dsg_w, sg_b=dsg_b)
    return dx, dxb_new, dgain, grads


def _even_weights(w_in, w_uq, w_ukv, w_out, q_norm, kv_norm, sg_norm, sg_w, sg_b):
    D = w_in.shape[0]
    kr_cols = jnp.pad(w_in[:, Q_LORA + KV_LORA:Q_LORA + KV_LORA + ROPE], ((0, 0), (NOPE, HP - QK_DIM)))
    w_in_p = jnp.concatenate([w_in[:, :Q_LORA + KV_LORA], kr_cols, w_in[:, Q_LORA + KV_LORA + ROPE:]], axis=1)
    wq = w_uq.reshape(Q_LORA, HEADS, QK_DIM)
    w_q = jnp.pad(wq, ((0, 0), (0, 0), (0, HP - QK_DIM))).reshape(Q_LORA, HEADS * HP)
    wkv = w_ukv.reshape(KV_LORA, HEADS, NOPE + VDIM)
    w_k = jnp.pad(wkv[:, :, :NOPE], ((0, 0), (0, 0), (0, HP - NOPE))).reshape(KV_LORA, HEADS * HP)
    w_v = jnp.pad(wkv[:, :, NOPE:], ((0, 0), (0, 0), (0, HP - VDIM))).reshape(KV_LORA, HEADS * HP)
    wo_a = w_out[:HEADS * VDIM].reshape(HEADS, VDIM, D)
    wo_a = jnp.pad(wo_a, ((0, 0), (0, HP - VDIM), (0, 0))).reshape(HEADS * HP, D)
    w_out_p = jnp.concatenate([wo_a, w_out[HEADS * VDIM:]], axis=0)
    tri = jnp.tril(jnp.ones((SG_CHUNK, SG_CHUNK), F32))
    wm = sg_w * tri
    wst = wm.reshape(SG_GROUPS // 2, 2 * SG_CHUNK, SG_CHUNK).astype(_MXU_DTYPE)
    wst_t = jnp.swapaxes(wm, 1, 2).reshape(SG_GROUPS // 2, 2 * SG_CHUNK, SG_CHUNK).astype(_MXU_DTYPE)
    bexp = jnp.repeat(sg_b.T, SG_GDIM, axis=1)
    return dict(w_in=w_in_p, w_q=w_q, w_k=w_k, w_v=w_v, w_out=w_out_p, sg_wst=wst, sg_wst_t=wst_t, sg_bexp=bexp,
                q_norm=q_norm.reshape(1, -1), kv_norm=kv_norm.reshape(1, -1), sg_norm=sg_norm.reshape(1, -1))


def _even_grads_unpad(g):
    d_in = g['w_in']
    kr0 = Q_LORA + KV_LORA
    dw_in = jnp.concatenate([d_in[:, :kr0], d_in[:, kr0 + NOPE:kr0 + QK_DIM], d_in[:, kr0 + HP:]], axis=1)
    dw_uq = g['w_q'].reshape(Q_LORA, HEADS, HP)[:, :, :QK_DIM].reshape(Q_LORA, HEADS * QK_DIM)
    dk = g['w_k'].reshape(KV_LORA, HEADS, HP)[:, :, :NOPE]
    dv = g['w_v'].reshape(KV_LORA, HEADS, HP)[:, :, :VDIM]
    dw_ukv = jnp.concatenate([dk, dv], axis=2).reshape(KV_LORA, HEADS * (NOPE + VDIM))
    D = d_in.shape[0]
    wo = g['w_out']
    wo_a = wo[:HEADS * HP].reshape(HEADS, HP, D)[:, :VDIM].reshape(HEADS * VDIM, D)
    dw_out = jnp.concatenate([wo_a, wo[HEADS * HP:]], axis=0)
    dsg_b = g['sg_b'][:, :SG_GROUPS].T
    return dict(even_w_in=dw_in, w_uq=dw_uq, w_ukv=dw_ukv, even_w_out=dw_out, q_norm=g['q_norm'][0],
                kv_norm=g['kv_norm'][0], sg_norm=g['sg_norm'][0], sg_w=g['sg_w'], sg_b=dsg_b)


def kernel(x, positions, ffn_pre_norm, ffn_pre_w_gate, ffn_pre_w_up, ffn_pre_w_down, mix_norm, ffn_post_norm, ffn_post_w_gate, ffn_post_w_up, ffn_post_w_down, even_w_in, q_norm, w_uq, kv_norm, w_ukv, sg_norm, sg_w, sg_b, even_w_out, conv_w_in, conv_w, conv_w_out, final_norm, loss_target, m_ffn_pre_norm, m_ffn_pre_w_gate, m_ffn_pre_w_up, m_ffn_pre_w_down, m_mix_norm, m_ffn_post_norm, m_ffn_post_w_gate, m_ffn_post_w_up, m_ffn_post_w_down, m_even_w_in, m_q_norm, m_w_uq, m_kv_norm, m_w_ukv, m_sg_norm, m_sg_w, m_sg_b, m_even_w_out, m_conv_w_in, m_conv_w, m_conv_w_out, m_final_norm, v_ffn_pre_norm, v_ffn_pre_w_gate, v_ffn_pre_w_up, v_ffn_pre_w_down, v_mix_norm, v_ffn_post_norm, v_ffn_post_w_gate, v_ffn_post_w_up, v_ffn_post_w_down, v_even_w_in, v_q_norm, v_w_uq, v_kv_norm, v_w_ukv, v_sg_norm, v_sg_w, v_sg_b, v_even_w_out, v_conv_w_in, v_conv_w, v_conv_w_out, v_final_norm):
    env = dict(locals())
    w_loc = {n: env[n] for n in WEIGHTS}
    m_loc = {n: env['m_' + n] for n in WEIGHTS}
    v_loc = {n: env['v_' + n] for n in WEIGHTS}
    S, D = x.shape[1], x.shape[2]
    depth = ffn_pre_norm.shape[0]
    xs = x.reshape(S, D)
    target = loss_target.reshape(S, D)

    Fb = ffn_pre_w_gate.shape[2]
    wire2d = lambda n, cols: w_loc[n].astype(_WIRE_DTYPE).reshape(-1, cols)
    shard_a = _pad_axis(jnp.concatenate([wire2d(n, Fb) for n in GROUP_A], axis=0), 0, PACK_ROW_MULT)
    shard_b = _pad_axis(jnp.concatenate([wire2d(n, D) for n in GROUP_B], axis=0), 0, PACK_ROW_MULT)
    shard_c = _pad_rows(jnp.concatenate([w_loc[n].astype(_WIRE_DTYPE).reshape(-1) for n in GROUP_C]), PACK_ROW_MULT)
    gat_a, gat_b, gat_c = _gather_halves_call([shard_a, shard_b, shard_c])
    taps = _gather_weights_call("gather_taps", _pad_rows(conv_w.reshape(-1), 8)).reshape(4, -1)
    taps = jnp.concatenate([taps[b, :conv_w.size].reshape(conv_w.shape) for b in range(4)], axis=2)
    full = {}
    for ia, n in enumerate(GROUP_A):
        full[n] = [(gat_a, ia * depth + l) for l in range(depth)]
    row = 0
    for n in GROUP_B:
        shp = w_loc[n].shape
        rows = shp[0] * shp[1]
        if n in FFN_WEIGHTS:
            full[n] = [(gat_b, row // shp[1] + l) for l in range(depth)]
        else:
            full[n] = jnp.concatenate([gat_b[b, row:row + rows].reshape(shp) for b in range(4)], axis=1)
        row += rows
    gflat = gat_c.reshape(4, -1)
    off = 0
    for n in GROUP_C:
        shp = w_loc[n].shape
        size = int(np.prod(shp))
        full[n] = jnp.concatenate([gflat[b, off:off + size].reshape(shp) for b in range(4)], axis=SHARD_AXIS[n])
        off += size

    inv_freq = ROPE_THETA ** (-jnp.arange(0, ROPE, 2, dtype=F32) / ROPE)
    half = ROPE // 2
    zeros = lambda n: jnp.zeros((n,), F32)
    ones = jnp.ones((half,), F32)
    invf = jnp.concatenate([zeros(NOPE), inv_freq, inv_freq, zeros(HP - QK_DIM)]).reshape(1, HP)
    mask_a = jnp.concatenate([zeros(NOPE), -ones, zeros(HP - NOPE - half)]).reshape(1, HP)
    mask_b = jnp.concatenate([zeros(NOPE + half), ones, zeros(HP - QK_DIM)]).reshape(1, HP)
    tables = _rope_tables_call(positions.reshape(S, 1), invf, mask_a, mask_b)

    even_w = []
    for e in range((depth + 1) // 2):
        even_w.append(_even_weights(full['even_w_in'][e], full['w_uq'][e], full['w_ukv'][e], full['even_w_out'][e],
                                    q_norm[e], kv_norm[e], sg_norm[e], sg_w[e], sg_b[e]))

    def gain_row(arr, l):
        return arr[l].reshape(1, D)

    saved = []
    h = _rmsnorm_call("first_norm", xs, gain_row(ffn_pre_norm, 0))
    xc = xs
    for l in range(depth):
        xc, h, s_pre = _ffn_fwd(f"l{l}_pre", xc, h, full['ffn_pre_w_gate'][l], full['ffn_pre_w_up'][l],
                                full['ffn_pre_w_down'][l], gain_row(mix_norm, l))
        if l % 2 == 0:
            xc, h, s_mix = _even_mixer_fwd(f"l{l}_mix", xc, h, even_w[l // 2], tables, gain_row(ffn_post_norm, l))
        else:
            o = l // 2
            xc, h, s_mix = _conv_mixer_fwd(f"l{l}_mix", xc, h, full['conv_w_in'][o], taps[o],
                                           full['conv_w_out'][o], gain_row(ffn_post_norm, l))
        g_next = gain_row(ffn_pre_norm, l + 1) if l + 1 < depth else final_norm.reshape(1, D)
        xc, h, s_post = _ffn_fwd(f"l{l}_post", xc, h, full['ffn_post_w_gate'][l], full['ffn_post_w_up'][l],
                                 full['ffn_post_w_down'][l], g_next)
        saved.append((s_pre, s_mix, s_post))

    dx, dxb, d_final, loss_part = _loss_call(xc, target, final_norm.reshape(1, D))
    loss = lax.psum(loss_part[0, 0], ("x", "y", "c"))

    gl = {n: [None] * w_loc[n].shape[0] for n in WEIGHTS if n != 'final_norm'}
    for l in reversed(range(depth)):
        s_pre, s_mix, s_post = saved[l]
        dx, dxb, dgain, dwg, dwu, dwd = _ffn_bwd(f"l{l}_post", s_post, dx, dxb, full['ffn_post_w_gate'][l],
                                                 full['ffn_post_w_up'][l], full['ffn_post_w_down'][l],
                                                 gain_row(ffn_post_norm, l))
        gl['ffn_post_norm'][l] = dgain[0]
        gl['ffn_post_w_gate'][l], gl['ffn_post_w_up'][l], gl['ffn_post_w_down'][l] = dwg, dwu, dwd
        if l % 2 == 0:
            e = l // 2
            dx, dxb, dgain, eg = _even_mixer_bwd(f"l{l}_mix", s_mix, dx, dxb, even_w[e], tables, gain_row(mix_norm, l))
            for n, val in _even_grads_unpad(eg).items():
                gl[n][e] = val
        else:
            o = l // 2
            dx, dxb, dgain, dw_in, dtaps, dw_out = _conv_mixer_bwd(f"l{l}_mix", s_mix, dx, dxb, full['conv_w_in'][o],
                                                                   taps[o], full['conv_w_out'][o],
                                                                   gain_row(mix_norm, l))
            gl['conv_w_in'][o], gl['conv_w'][o], gl['conv_w_out'][o] = dw_in, dtaps, dw_out
        gl['mix_norm'][l] = dgain[0]
        dx, dxb, dgain, dwg, dwu, dwd = _ffn_bwd(f"l{l}_pre", s_pre, dx, dxb, full['ffn_pre_w_gate'][l],
                                                 full['ffn_pre_w_up'][l], full['ffn_pre_w_down'][l],
                                                 gain_row(ffn_pre_norm, l))
        gl['ffn_pre_norm'][l] = dgain[0]
        gl['ffn_pre_w_gate'][l], gl['ffn_pre_w_up'][l], gl['ffn_pre_w_down'][l] = dwg, dwu, dwd
    grad_x = dx.reshape(x.shape)
    part = {n: jnp.stack(gl[n]) for n in gl if n not in FFN_WEIGHTS}
    part['final_norm'] = d_final[0]

    def row_blocked(n):
        g = part[n]
        L, r4, cols = g.shape
        return jnp.swapaxes(g.reshape(L, 4, r4 // 4, cols), 0, 1).reshape(4, L * (r4 // 4), cols).astype(_WIRE_DTYPE)

    pack_a = jnp.concatenate([gl[n][l] for n in GROUP_A for l in range(depth)], axis=1)
    pack_b = jnp.concatenate([gl[n][l] for n in GROUP_B if n in FFN_WEIGHTS for l in range(depth)]
                             + [row_blocked(n) for n in GROUP_B if n not in FFN_WEIGHTS], axis=1)
    pack_c = jnp.stack([_pad_rows(jnp.concatenate(
        [_shard_slice(part[n], SHARD_AXIS[n], b).astype(_WIRE_DTYPE).reshape(-1) for n in GROUP_C]), PACK_ROW_MULT)
        for b in range(4)])
    packs = [_pad_axis(p, 1, PACK_ROW_MULT) for p in (pack_a, pack_b, pack_c)]
    packs = [p.reshape(4, 2, p.shape[1] // 2, p.shape[2]) for p in packs]
    core = lax.axis_index("c").astype(jnp.int32).reshape(1)
    theirs = _pair_exchange_call(packs)
    pairs = [_pair_add_call(f"pair_add_{i}", p, t, core) for i, (p, t) in enumerate(zip(packs, theirs))]
    arrived = _chip_scatter_call(pairs)
    mine = [_sum_slots_call(f"sum_grad_slots_{i}", r, core) for i, r in enumerate(arrived)]
    red_a, red_b, red_c = [t.reshape(-1, t.shape[2]) for t in _sibling_share_call(mine)]
    grads = {}
    for group, red in ((GROUP_A, red_a), (GROUP_B, red_b)):
        row = 0
        for n in group:
            shp = w_loc[n].shape
            rows = shp[0] * shp[1]
            grads[n] = red[row:row + rows].reshape(shp)
            row += rows
    red_c = red_c.reshape(-1)
    off = 0
    for n in GROUP_C:
        shp = w_loc[n].shape
        size = int(np.prod(shp))
        grads[n] = red_c[off:off + size].reshape(shp)
        off += size

    small = _pad_rows(jnp.concatenate([part[n].reshape(-1) for n in REPLICATED]), 8)
    small_sum = _allreduce_small_call(small).reshape(-1)
    off = 0
    for n in REPLICATED:
        size = int(np.prod(w_loc[n].shape))
        grads[n] = small_sum[off:off + size].reshape(w_loc[n].shape)
        off += size

    deltas, new_m, new_v = {}, {}, {}
    for n in WEIGHTS:
        deltas[n], new_m[n], new_v[n] = _adamw_call("adamw_" + n, w_loc[n], grads[n], m_loc[n], v_loc[n])
    return (loss, grad_x, *[grads[n] for n in WEIGHTS], *[deltas[n] for n in WEIGHTS],
            *[new_m[n] for n in WEIGHTS], *[new_v[n] for n in WEIGHTS])
```

```python
import functools

import numpy as np
import jax
import jax.numpy as jnp
from jax import lax
from jax.experimental import pallas as pl
from jax.experimental.pallas import tpu as pltpu

F32 = jnp.float32
BF16 = jnp.bfloat16
_MXU_DTYPE = jnp.bfloat16
_WIRE_DTYPE = jnp.bfloat16
_VMEM_LIMIT = 52 * 1024 * 1024
_LANES = 128
_ATT_BLOCK = 512
_ROW_TILE = 512
_SG_TILE = 1024

NORM_EPS = 1e-6
HEADS = 8
NOPE = 64
ROPE = 32
VDIM = 64
QK_DIM = NOPE + ROPE
HP = 128
Q_LORA = 384
KV_LORA = 256
SG_WIDTH = 512
SG_GROUPS = 8
SG_GDIM = 64
SG_CHUNK = 128
ROPE_THETA = 10000.0
PROJ_W = Q_LORA + KV_LORA + HP + 2 * SG_WIDTH
ADAM_LR = 0.001
ADAM_B1 = 0.9
ADAM_B2 = 0.999
ADAM_EPS = 1e-08
ADAM_WD = 0.01
ADAM_STEP = 10
MESH = pl.DeviceIdType.MESH
PACK_COLS = 1024
PACK_ROW_MULT = 256

SHARDED = ['ffn_pre_w_gate', 'ffn_pre_w_up', 'ffn_pre_w_down', 'ffn_post_w_gate', 'ffn_post_w_up',
           'ffn_post_w_down', 'even_w_in', 'w_uq', 'w_ukv', 'even_w_out', 'conv_w_in', 'conv_w', 'conv_w_out']
SHARD_AXIS = {'ffn_pre_w_gate': 2, 'ffn_pre_w_up': 2, 'ffn_pre_w_down': 1, 'ffn_post_w_gate': 2,
              'ffn_post_w_up': 2, 'ffn_post_w_down': 1, 'even_w_in': 2, 'w_uq': 2, 'w_ukv': 2,
              'even_w_out': 1, 'conv_w_in': 2, 'conv_w': 2, 'conv_w_out': 1}
FFN_WEIGHTS = ['ffn_pre_w_gate', 'ffn_pre_w_up', 'ffn_pre_w_down', 'ffn_post_w_gate', 'ffn_post_w_up',
               'ffn_post_w_down']
GROUP_A = ['ffn_pre_w_gate', 'ffn_pre_w_up', 'ffn_post_w_gate', 'ffn_post_w_up']
GROUP_B = ['ffn_pre_w_down', 'ffn_post_w_down', 'even_w_out', 'conv_w_out']
GROUP_C = ['even_w_in', 'w_uq', 'w_ukv', 'conv_w_in', 'conv_w']
REPLICATED = ['ffn_pre_norm', 'mix_norm', 'ffn_post_norm', 'q_norm', 'kv_norm', 'sg_norm', 'sg_w', 'sg_b',
              'final_norm']
WEIGHTS = ['ffn_pre_norm', 'ffn_pre_w_gate', 'ffn_pre_w_up', 'ffn_pre_w_down', 'mix_norm', 'ffn_post_norm',
           'ffn_post_w_gate', 'ffn_post_w_up', 'ffn_post_w_down', 'even_w_in', 'q_norm', 'w_uq', 'kv_norm',
           'w_ukv', 'sg_norm', 'sg_w', 'sg_b', 'even_w_out', 'conv_w_in', 'conv_w', 'conv_w_out', 'final_norm']


def _params(sem=None):
    return pltpu.CompilerParams(vmem_limit_bytes=_VMEM_LIMIT,
                                **({} if sem is None else {'dimension_semantics': sem}))


def _pick(n, pref, mult):
    best = None
    t = mult
    while t <= min(n, pref):
        if n % t == 0:
            best = t
        t += mult
    return n if best is None else best


def _mx(v):
    return v if v.dtype == _MXU_DTYPE else v.astype(_MXU_DTYPE)


def _sigmoid(a):
    return 1.0 / (1.0 + jnp.exp(-a))


def _rms_stats(x):
    rstd = lax.rsqrt(jnp.mean(x * x, axis=-1, keepdims=True) + NORM_EPS)
    return x * rstd, rstd


def _rms_bwd(x, g, dh):
    xhat, rstd = _rms_stats(x)
    gdh = g * dh
    dx = rstd * (gdh - xhat * jnp.mean(gdh * xhat, axis=-1, keepdims=True))
    return dx, dh * xhat


def _fused_matmul(name, mode, lhs, rhs, prods, n_acc, epilogue, out_dtypes, M, N, K, tm, tn, tk,
                  tile_extras=(), row_extras=(), mrow_extras=(), n_colsum=0):
    gj, gi, gk = N // tn, M // tm, K // tk
    assert gj * tn == N and gi * tm == M and gk * tk == K, (name, M, N, K, tm, tn, tk)
    dims = {'nn': (((1,), (0,)), ((), ())), 'nt': (((1,), (1,)), ((), ())), 'tn': (((0,), (0,)), ((), ()))}[mode]

    def lhs_spec(roff, coff, kb):
        kb = tk if kb is None else kb
        if mode == 'tn':
            return pl.BlockSpec((kb, tm), lambda j, i, k: (k + roff, i + coff))
        return pl.BlockSpec((tm, kb), lambda j, i, k: (i + roff, k + coff))

    def rhs_spec(roff, coff, kb):
        kb = tk if kb is None else kb
        if mode == 'nt':
            return pl.BlockSpec((tn, kb), lambda j, i, k: (j + roff, k + coff))
        return pl.BlockSpec((kb, tn), lambda j, i, k: (k + roff, j + coff))

    in_specs = [lhs_spec(*a[1:]) for a in lhs] + [rhs_spec(*a[1:]) for a in rhs]
    in_specs += [pl.BlockSpec((tm, tn), lambda j, i, k: (i, j)) for _ in tile_extras]
    in_specs += [pl.BlockSpec((1, tn), lambda j, i, k: (0, j)) for _ in row_extras]
    in_specs += [pl.BlockSpec((tm, a.shape[1]), lambda j, i, k: (i, 0)) for a in mrow_extras]
    n_out = len(out_dtypes)
    out_shape = [jax.ShapeDtypeStruct((M, N), d) for d in out_dtypes]
    out_specs = [pl.BlockSpec((tm, tn), lambda j, i, k: (i, j)) for _ in out_dtypes]
    out_shape += [jax.ShapeDtypeStruct((1, N), F32) for _ in range(n_colsum)]
    out_specs += [pl.BlockSpec((1, tn), lambda j, i, k: (0, j)) for _ in range(n_colsum)]
    scratch = [pltpu.VMEM((tm, tn), F32) for _ in range(n_acc)] if gk > 1 else []
    nl, nr, nt, nrw, nm = len(lhs), len(rhs), len(tile_extras), len(row_extras), len(mrow_extras)

    def body(*refs):
        pos = 0
        lhs_refs = refs[pos:pos + nl]; pos += nl
        rhs_refs = refs[pos:pos + nr]; pos += nr
        tile_refs = refs[pos:pos + nt]; pos += nt
        row_refs = refs[pos:pos + nrw]; pos += nrw
        mrow_refs = refs[pos:pos + nm]; pos += nm
        out_refs = refs[pos:pos + n_out]; pos += n_out
        cs_refs = refs[pos:pos + n_colsum]; pos += n_colsum
        acc_refs = refs[pos:]
        i = pl.program_id(1)
        k = pl.program_id(2)

        def partials():
            res = [None] * n_acc
            for (li, ri, ai) in prods:
                d = lax.dot_general(_mx(lhs_refs[li][...]), _mx(rhs_refs[ri][...]), dims,
                                    preferred_element_type=F32)
                res[ai] = d if res[ai] is None else res[ai] + d
            return res

        def finish(accs):
            outs = epilogue(accs, [r[...] for r in tile_refs], [r[...] for r in row_refs],
                            [r[...] for r in mrow_refs])
            for r, o in zip(out_refs, outs[:n_out]):
                r[...] = o.astype(r.dtype)
            for r, c in zip(cs_refs, outs[n_out:]):
                c = jnp.sum(c, axis=0, keepdims=True)

                @pl.when(i == 0)
                def _():
                    r[...] = c

                @pl.when(i != 0)
                def _():
                    r[...] += c

        if gk == 1:
            finish(partials())
        else:
            p = partials()

            @pl.when(k == 0)
            def _():
                for r, v in zip(acc_refs, p):
                    r[...] = v

            @pl.when(k != 0)
            def _():
                for r, v in zip(acc_refs, p):
                    r[...] += v

            @pl.when(k == gk - 1)
            def _():
                finish([r[...] for r in acc_refs])

    res = pl.pallas_call(
        body, name=name, grid=(gj, gi, gk), in_specs=in_specs, out_specs=out_specs, out_shape=out_shape,
        scratch_shapes=scratch, compiler_params=_params(("arbitrary", "arbitrary", "arbitrary")),
    )(*[a[0] for a in lhs], *[a[0] for a in rhs], *tile_extras, *row_extras, *mrow_extras)
    return res


def _op(a, roff=0, coff=0, kb=None):
    return (a, roff, coff, kb)


def _ident_epi(scale=None):
    def epi(accs, tiles, rows, mrows):
        return [a if scale is None else a * scale for a in accs]
    return epi


def _resid_norm_epi(scale):
    def epi(accs, tiles, rows, mrows):
        x_new = tiles[0] + scale * accs[0]
        xhat, _ = _rms_stats(x_new)
        return [x_new, xhat * rows[0]]
    return epi


def _norm_bwd_epi(accs, tiles, rows, mrows):
    dx_n, dg = _rms_bwd(tiles[0], rows[0], accs[0])
    dx = tiles[1] + dx_n
    return [dx, dx, dg]


def _rmsnorm_call(name, x, g):
    S, D = x.shape
    tm = _pick(S, _ROW_TILE, 8)

    def body(x_ref, g_ref, h_ref):
        xhat, _ = _rms_stats(x_ref[...])
        h_ref[...] = (xhat * g_ref[...]).astype(h_ref.dtype)

    return pl.pallas_call(
        body, name=name, grid=(S // tm,),
        in_specs=[pl.BlockSpec((tm, D), lambda i: (i, 0)), pl.BlockSpec((1, D), lambda i: (0, 0))],
        out_specs=pl.BlockSpec((tm, D), lambda i: (i, 0)),
        out_shape=jax.ShapeDtypeStruct((S, D), BF16), compiler_params=_params(("arbitrary",)),
    )(x, g)


def _loss_call(x, target, g):
    S, D = x.shape
    tm = _pick(S, _ROW_TILE, 8)

    def body(x_ref, t_ref, g_ref, dx_ref, dxb_ref, dg_ref, loss_ref):
        i = pl.program_id(0)
        x_t = x_ref[...]
        gain = g_ref[...]
        xhat, _ = _rms_stats(x_t)
        diff = xhat * gain - t_ref[...]
        dy = diff * (1.0 / D)
        dx, dg = _rms_bwd(x_t, gain, dy)
        dx_ref[...] = dx
        dxb_ref[...] = dx.astype(BF16)
        dg = jnp.sum(dg, axis=0, keepdims=True)
        part = 0.5 * jnp.sum(jnp.sum(diff * diff, axis=1, keepdims=True), axis=0, keepdims=True) * (1.0 / D)
        part = jnp.broadcast_to(part, (1, _LANES))

        @pl.when(i == 0)
        def _():
            dg_ref[...] = dg
            loss_ref[...] = part

        @pl.when(i != 0)
        def _():
            dg_ref[...] += dg
            loss_ref[...] += part

    row = lambda i: (i, 0)
    fixed = lambda i: (0, 0)
    return pl.pallas_call(
        body, name="loss_head", grid=(S // tm,),
        in_specs=[pl.BlockSpec((tm, D), row), pl.BlockSpec((tm, D), row), pl.BlockSpec((1, D), fixed)],
        out_specs=[pl.BlockSpec((tm, D), row), pl.BlockSpec((tm, D), row), pl.BlockSpec((1, D), fixed),
                   pl.BlockSpec((1, _LANES), fixed)],
        out_shape=[jax.ShapeDtypeStruct((S, D), F32), jax.ShapeDtypeStruct((S, D), BF16),
                   jax.ShapeDtypeStruct((1, D), F32), jax.ShapeDtypeStruct((1, _LANES), F32)],
        compiler_params=_params(("arbitrary",)),
    )(x, target, g)


def _rope_tables_call(pos_col, invf, mask_a, mask_b):
    S = pos_col.shape[0]
    tm = _pick(S, _ROW_TILE, 8)

    def body(p_ref, f_ref, a_ref, b_ref, cos_ref, sa_ref, sb_ref):
        ang = p_ref[...].astype(F32) * f_ref[...]
        sn = jnp.sin(ang)
        cos_ref[...] = jnp.cos(ang)
        sa_ref[...] = sn * a_ref[...]
        sb_ref[...] = sn * b_ref[...]

    row = lambda i: (i, 0)
    fixed = lambda i: (0, 0)
    return pl.pallas_call(
        body, name="rope_tables", grid=(S // tm,),
        in_specs=[pl.BlockSpec((tm, 1), row)] + [pl.BlockSpec((1, HP), fixed)] * 3,
        out_specs=[pl.BlockSpec((tm, HP), row)] * 3,
        out_shape=[jax.ShapeDtypeStruct((S, HP), F32)] * 3, compiler_params=_params(("arbitrary",)),
    )(pos_col, invf, mask_a, mask_b)


def _rope(t, cos, sa, sb):
    return t * cos + pltpu.roll(t, HP - ROPE // 2, 1) * sa + pltpu.roll(t, ROPE // 2, 1) * sb


def _rope_t(d, cos, sa, sb):
    return d * cos + pltpu.roll(d * sa, ROPE // 2, 1) + pltpu.roll(d * sb, HP - ROPE // 2, 1)


def _gelu(z):
    return 0.5 * z * (1.0 + lax.erf(z * np.float32(1.0 / np.sqrt(2.0))))


def _gelu_grad(z):
    cdf = 0.5 * (1.0 + lax.erf(z * np.float32(1.0 / np.sqrt(2.0))))
    pdf = np.float32(1.0 / np.sqrt(2.0 * np.pi)) * jnp.exp(-0.5 * z * z)
    return cdf + z * pdf


_CQ0, _CKV0, _KR0, _Z0 = 0, Q_LORA, Q_LORA + KV_LORA, Q_LORA + KV_LORA + HP


def _even_prep_call(proj, qn, kvn, sgn, cos, sa, sb):
    S = proj.shape[0]
    tm = _pick(S, 256, 8)

    def body(p_ref, qn_ref, kvn_ref, sgn_ref, cos_ref, sa_ref, sb_ref, cq_ref, ckv_ref, kr_ref, u_ref, v_ref):
        cq = p_ref[:, _CQ0:_CQ0 + Q_LORA]
        cq_ref[...] = (_rms_stats(cq)[0] * qn_ref[...]).astype(BF16)
        ckv = p_ref[:, _CKV0:_CKV0 + KV_LORA]
        ckv_ref[...] = (_rms_stats(ckv)[0] * kvn_ref[...]).astype(BF16)
        kr = p_ref[:, _KR0:_KR0 + HP]
        kr_ref[...] = _rope(kr, cos_ref[...], sa_ref[...], sb_ref[...]).astype(BF16)
        u_ref[...] = _gelu(p_ref[:, _Z0:_Z0 + SG_WIDTH]).astype(BF16)
        zv = _gelu(p_ref[:, _Z0 + SG_WIDTH:_Z0 + 2 * SG_WIDTH])
        v_ref[...] = (_rms_stats(zv)[0] * sgn_ref[...]).astype(BF16)

    row = lambda i: (i, 0)
    fixed = lambda i: (0, 0)
    widths = [Q_LORA, KV_LORA, HP, SG_WIDTH, SG_WIDTH]
    return pl.pallas_call(
        body, name="even_prep", grid=(S // tm,),
        in_specs=[pl.BlockSpec((tm, PROJ_W), row), pl.BlockSpec((1, Q_LORA), fixed),
                  pl.BlockSpec((1, KV_LORA), fixed), pl.BlockSpec((1, SG_WIDTH), fixed)]
        + [pl.BlockSpec((tm, HP), row)] * 3,
        out_specs=[pl.BlockSpec((tm, w), row) for w in widths],
        out_shape=[jax.ShapeDtypeStruct((S, w), BF16) for w in widths],
        compiler_params=_params(("arbitrary",)),
    )(proj, qn, kvn, sgn, cos, sa, sb)


def _even_prep_bwd_call(proj, qn, kvn, sgn, cos, sa, sb, dcqn, dckvn, dk, du, dvn):
    S = proj.shape[0]
    tm = _pick(S, 256, 8)

    def body(p_ref, qn_ref, kvn_ref, sgn_ref, cos_ref, sa_ref, sb_ref, dcq_ref, dckv_ref, dk_ref, du_ref,
             dvn_ref, dp_ref, dqn_ref, dkvn_ref, dsgn_ref):
        i = pl.program_id(0)
        dcq, gq = _rms_bwd(p_ref[:, _CQ0:_CQ0 + Q_LORA], qn_ref[...], dcq_ref[...])
        dp_ref[:, _CQ0:_CQ0 + Q_LORA] = dcq.astype(BF16)
        dckv, gkv = _rms_bwd(p_ref[:, _CKV0:_CKV0 + KV_LORA], kvn_ref[...], dckv_ref[...])
        dp_ref[:, _CKV0:_CKV0 + KV_LORA] = dckv.astype(BF16)
        dkr = dk_ref[:, 0:HP].astype(F32)
        for h in range(1, HEADS):
            dkr = dkr + dk_ref[:, h * HP:(h + 1) * HP].astype(F32)
        lane = lax.broadcasted_iota(jnp.int32, dkr.shape, 1)
        dkr = jnp.where((lane >= NOPE) & (lane < QK_DIM), dkr, 0.0)
        dp_ref[:, _KR0:_KR0 + HP] = _rope_t(dkr, cos_ref[...], sa_ref[...], sb_ref[...]).astype(BF16)
        zu = p_ref[:, _Z0:_Z0 + SG_WIDTH]
        dp_ref[:, _Z0:_Z0 + SG_WIDTH] = (du_ref[...].astype(F32) * _gelu_grad(zu)).astype(BF16)
        zv = p_ref[:, _Z0 + SG_WIDTH:_Z0 + 2 * SG_WIDTH]
        dgv, gsg = _rms_bwd(_gelu(zv), sgn_ref[...], dvn_ref[...].astype(F32))
        dp_ref[:, _Z0 + SG_WIDTH:_Z0 + 2 * SG_WIDTH] = (dgv * _gelu_grad(zv)).astype(BF16)
        sums = [jnp.sum(t, axis=0, keepdims=True) for t in (gq, gkv, gsg)]

        @pl.when(i == 0)
        def _():
            for r, s in zip((dqn_ref, dkvn_ref, dsgn_ref), sums):
                r[...] = s

        @pl.when(i != 0)
        def _():
            for r, s in zip((dqn_ref, dkvn_ref, dsgn_ref), sums):
                r[...] += s

    row = lambda i: (i, 0)
    fixed = lambda i: (0, 0)
    return pl.pallas_call(
        body, name="even_prep_bwd", grid=(S // tm,),
        in_specs=[pl.BlockSpec((tm, PROJ_W), row), pl.BlockSpec((1, Q_LORA), fixed),
                  pl.BlockSpec((1, KV_LORA), fixed), pl.BlockSpec((1, SG_WIDTH), fixed)]
        + [pl.BlockSpec((tm, HP), row)] * 3
        + [pl.BlockSpec((tm, Q_LORA), row), pl.BlockSpec((tm, KV_LORA), row),
           pl.BlockSpec((tm, HEADS * HP), row), pl.BlockSpec((tm, SG_WIDTH), row),
           pl.BlockSpec((tm, SG_WIDTH), row)],
        out_specs=[pl.BlockSpec((tm, PROJ_W), row), pl.BlockSpec((1, Q_LORA), fixed),
                   pl.BlockSpec((1, KV_LORA), fixed), pl.BlockSpec((1, SG_WIDTH), fixed)],
        out_shape=[jax.ShapeDtypeStruct((S, PROJ_W), BF16), jax.ShapeDtypeStruct((1, Q_LORA), F32),
                   jax.ShapeDtypeStruct((1, KV_LORA), F32), jax.ShapeDtypeStruct((1, SG_WIDTH), F32)],
        compiler_params=_params(("arbitrary",)),
    )(proj, qn, kvn, sgn, cos, sa, sb, dcqn, dckvn, dk, du, dvn)


def _causal_mask(rows, cols):
    r = lax.broadcasted_iota(jnp.int32, (rows, cols), 0)
    c = lax.broadcasted_iota(jnp.int32, (rows, cols), 1)
    return c <= r


def _flash_fwd_call(q, k, v):
    S = q.shape[0]
    tb = _pick(S, _ATT_BLOCK, 128)
    nq = S // tb
    nt_dims = (((1,), (1,)), ((), ()))

    def body(q_ref, k_ref, v_ref, o_ref, lse_ref, s_a, s_b, m_ref, l_ref, acc_ref):
        i = pl.program_id(1)

        def scores(buf, j):
            k_t = k_ref[pl.ds(pl.multiple_of(j * tb, tb), tb), :]
            buf[...] = lax.dot_general(q_ref[...], k_t, nt_dims, preferred_element_type=F32)

        def update(buf, j, masked):
            v_t = v_ref[pl.ds(pl.multiple_of(j * tb, tb), tb), :]
            s = buf[...]
            if masked:
                s = jnp.where(_causal_mask(tb, tb), s, -1e30)
            m = m_ref[...]
            m_new = jnp.maximum(m, jnp.max(s, axis=1, keepdims=True))
            alpha = jnp.exp(m - m_new)
            p = jnp.exp(s - m_new)
            l_ref[...] = alpha * l_ref[...] + jnp.sum(p, axis=1, keepdims=True)
            acc_ref[...] = alpha * acc_ref[...] + jnp.dot(p.astype(v_t.dtype), v_t, preferred_element_type=F32)
            m_ref[...] = m_new

        m_ref[...] = jnp.full((tb, 1), -1e30, F32)
        l_ref[...] = jnp.zeros((tb, 1), F32)
        acc_ref[...] = jnp.zeros((tb, HP), F32)
        scores(s_a, 0)
        pairs = i // 2

        def two_blocks(t, carry):
            scores(s_b, 2 * t + 1)
            update(s_a, 2 * t, False)
            scores(s_a, 2 * t + 2)
            update(s_b, 2 * t + 1, False)
            return carry

        lax.fori_loop(0, pairs, two_blocks, 0)

        @pl.when(2 * pairs == i)
        def _():
            update(s_a, i, True)

        @pl.when(2 * pairs != i)
        def _():
            scores(s_b, i)
            update(s_a, i - 1, False)
            update(s_b, i, True)

        o_ref[...] = (acc_ref[...] / l_ref[...]).astype(o_ref.dtype)
        lse = jnp.broadcast_to(m_ref[...] + jnp.log(l_ref[...]), (tb, HP))
        lse_ref[0, 0] = jnp.transpose(lse)[0:8, :]

    return pl.pallas_call(
        body, name="flash_fwd", grid=(HEADS, nq),
        in_specs=[pl.BlockSpec((tb, HP), lambda h, i: (i, h)), pl.BlockSpec((S, HP), lambda h, i: (0, h)),
                  pl.BlockSpec((S, HP), lambda h, i: (0, h))],
        out_specs=[pl.BlockSpec((tb, HP), lambda h, i: (i, h)),
                   pl.BlockSpec((1, 1, 8, tb), lambda h, i: (h, i, 0, 0))],
        out_shape=[jax.ShapeDtypeStruct((S, HEADS * HP), q.dtype), jax.ShapeDtypeStruct((HEADS, nq, 8, tb), F32)],
        scratch_shapes=[pltpu.VMEM((tb, tb), F32), pltpu.VMEM((tb, tb), F32), pltpu.VMEM((tb, 1), F32),
                        pltpu.VMEM((tb, 1), F32), pltpu.VMEM((tb, HP), F32)],
        compiler_params=_params(("arbitrary", "arbitrary")),
    )(q, k, v)


def _attn_delta_call(o, do):
    S = o.shape[0]
    tb = _pick(S, _ATT_BLOCK, 128)
    nq = S // tb
    nb = _pick(nq, 4, 1)

    def body(o_ref, do_ref, d_ref):
        for r in range(nb):
            rows = slice(r * tb, (r + 1) * tb)
            d = jnp.sum(o_ref[rows, :].astype(F32) * do_ref[rows, :].astype(F32), axis=1, keepdims=True)
            d_ref[0, r] = jnp.transpose(jnp.broadcast_to(d, (tb, HP)))[0:8, :]

    return pl.pallas_call(
        body, name="attn_delta", grid=(HEADS, nq // nb),
        in_specs=[pl.BlockSpec((nb * tb, HP), lambda h, i: (i, h))] * 2,
        out_specs=pl.BlockSpec((1, nb, 8, tb), lambda h, i: (h, i, 0, 0)),
        out_shape=jax.ShapeDtypeStruct((HEADS, nq, 8, tb), F32), compiler_params=_params(("arbitrary", "arbitrary")),
    )(o, do)


def _flash_bwd_call(q, k, v, do, lse, delta):
    S = q.shape[0]
    tb = _pick(S, _ATT_BLOCK, 128)
    nq = S // tb
    nt_dims = (((1,), (1,)), ((), ()))
    tn_dims = (((0,), (0,)), ((), ()))

    def body(q_ref, do_ref, lse_ref, dl_ref, k_ref, v_ref, dq_ref, dk_ref, dv_ref, st_a, dp_a, st_b, dp_b, dk_acc,
             dv_acc):
        j = pl.program_id(1)

        @pl.when(j == 0)
        def _():
            dq_ref[...] = jnp.zeros_like(dq_ref)

        def rows_of(i):
            return pl.ds(pl.multiple_of(i * tb, tb), tb)

        def scores(st_buf, dp_buf, i):
            st_buf[...] = lax.dot_general(k_ref[...], q_ref[rows_of(i), :], nt_dims, preferred_element_type=F32)
            dp_buf[...] = lax.dot_general(v_ref[...], do_ref[rows_of(i), :], nt_dims, preferred_element_type=F32)

        def update(st_buf, dp_buf, i, masked):
            q_t = q_ref[rows_of(i), :]
            do_t = do_ref[rows_of(i), :]
            pt = jnp.exp(st_buf[...] - lse_ref[0, i, 0:1, :])
            if masked:
                pt = jnp.where(jnp.transpose(_causal_mask(tb, tb)), pt, 0.0)
            dst = (pt * (dp_buf[...] - dl_ref[0, i, 0:1, :])).astype(q_t.dtype)
            dv_acc[...] += jnp.dot(pt.astype(do_t.dtype), do_t, preferred_element_type=F32)
            dk_acc[...] += jnp.dot(dst, q_t, preferred_element_type=F32)
            dq_ref[rows_of(i), :] += lax.dot_general(dst, k_ref[...], tn_dims, preferred_element_type=F32)

        last = nq - 1
        dk_acc[...] = jnp.zeros((tb, HP), F32)
        dv_acc[...] = jnp.zeros((tb, HP), F32)
        scores(st_b, dp_b, j)
        scores(st_a, dp_a, jnp.minimum(j + 1, last))
        update(st_b, dp_b, j, True)
        rest = last - j
        pairs = rest // 2

        def two_blocks(t, carry):
            i0 = j + 1 + 2 * t
            scores(st_b, dp_b, i0 + 1)
            update(st_a, dp_a, i0, False)
            scores(st_a, dp_a, jnp.minimum(i0 + 2, last))
            update(st_b, dp_b, i0 + 1, False)
            return carry

        lax.fori_loop(0, pairs, two_blocks, 0)

        @pl.when(2 * pairs != rest)
        def _():
            update(st_a, dp_a, last, False)

        dk_ref[...] = dk_acc[...].astype(dk_ref.dtype)
        dv_ref[...] = dv_acc[...].astype(dv_ref.dtype)

    head = lambda h, j: (0, h)
    blk = lambda h, j: (j, h)
    rows = lambda h, j: (h, 0, 0, 0)
    return pl.pallas_call(
        body, name="flash_bwd", grid=(HEADS, nq),
        in_specs=[pl.BlockSpec((S, HP), head), pl.BlockSpec((S, HP), head), pl.BlockSpec((1, nq, 8, tb), rows),
                  pl.BlockSpec((1, nq, 8, tb), rows), pl.BlockSpec((tb, HP), blk), pl.BlockSpec((tb, HP), blk)],
        out_specs=[pl.BlockSpec((S, HP), head), pl.BlockSpec((tb, HP), blk), pl.BlockSpec((tb, HP), blk)],
        out_shape=[jax.ShapeDtypeStruct((S, HEADS * HP), F32), jax.ShapeDtypeStruct((S, HEADS * HP), BF16),
                   jax.ShapeDtypeStruct((S, HEADS * HP), BF16)],
        scratch_shapes=[pltpu.VMEM((tb, tb), F32)] * 4 + [pltpu.VMEM((tb, HP), F32)] * 2,
        compiler_params=_params(("arbitrary", "arbitrary")),
    )(q, do, lse, delta, k, v)


def _sg_mixed(w_ref, vch, lane_lo):
    blocks = []
    for jb in range(SG_WIDTH // _LANES):
        r = jnp.dot(w_ref[jb], vch[:, jb * _LANES:(jb + 1) * _LANES], preferred_element_type=F32)
        blocks.append(jnp.where(lane_lo, r[0:SG_CHUNK], r[SG_CHUNK:2 * SG_CHUNK]))
    return jnp.concatenate(blocks, axis=1)


def _sgu_fwd_call(vn, u, attn, wst, bexp):
    S = vn.shape[0]
    tm = _pick(S, _SG_TILE, SG_CHUNK)
    AW = HEADS * HP

    def body(v_ref, u_ref, a_ref, w_ref, b_ref, mix_ref):
        lane_lo = lax.broadcasted_iota(jnp.int32, (SG_CHUNK, _LANES), 1) < SG_GDIM
        mix_ref[:, 0:AW] = a_ref[...]
        for c in range(tm // SG_CHUNK):
            rs = slice(c * SG_CHUNK, (c + 1) * SG_CHUNK)
            mixed = _sg_mixed(w_ref, v_ref[rs, :], lane_lo) + b_ref[...]
            mix_ref[rs, AW:AW + SG_WIDTH] = (u_ref[rs, :].astype(F32) * mixed).astype(mix_ref.dtype)

    row = lambda i: (i, 0)
    return pl.pallas_call(
        body, name="sgu_fwd", grid=(S // tm,),
        in_specs=[pl.BlockSpec((tm, SG_WIDTH), row), pl.BlockSpec((tm, SG_WIDTH), row), pl.BlockSpec((tm, AW), row),
                  pl.BlockSpec((SG_WIDTH // _LANES, 2 * SG_CHUNK, SG_CHUNK), lambda i: (0, 0, 0)),
                  pl.BlockSpec((SG_CHUNK, SG_WIDTH), lambda i: (0, 0))],
        out_specs=pl.BlockSpec((tm, AW + SG_WIDTH), row),
        out_shape=jax.ShapeDtypeStruct((S, AW + SG_WIDTH), BF16), compiler_params=_params(("arbitrary",)),
    )(vn, u, attn, wst, bexp)


def _sgu_bwd_call(dmix, vn, u, wst, wst_t, bexp):
    S = vn.shape[0]
    tm = _pick(S, _SG_TILE, SG_CHUNK)
    nblk = SG_WIDTH // _LANES
    col0 = (HEADS * HP) // SG_WIDTH
    nt_dims = (((1,), (1,)), ((), ()))

    def body(d_ref, v_ref, u_ref, w_ref, wt_ref, b_ref, du_ref, dv_ref, dw_ref, db_ref, dbacc_ref):
        i = pl.program_id(0)
        lane_lo = lax.broadcasted_iota(jnp.int32, (SG_CHUNK, _LANES), 1) < SG_GDIM

        @pl.when(i == 0)
        def _():
            dw_ref[...] = jnp.zeros_like(dw_ref)
            dbacc_ref[...] = jnp.zeros_like(dbacc_ref)

        for c in range(tm // SG_CHUNK):
            rs = slice(c * SG_CHUNK, (c + 1) * SG_CHUNK)
            vch = v_ref[rs, :]
            dsg = d_ref[rs, :].astype(F32)
            mixed = _sg_mixed(w_ref, vch, lane_lo) + b_ref[...]
            du_ref[rs, :] = (dsg * mixed).astype(du_ref.dtype)
            dmixed = dsg * u_ref[rs, :].astype(F32)
            dbacc_ref[...] += dmixed
            dmx = dmixed.astype(vch.dtype)
            dv_ref[rs, :] = _sg_mixed(wt_ref, dmx, lane_lo).astype(dv_ref.dtype)
            for jb in range(nblk):
                dblk = dmx[:, jb * _LANES:(jb + 1) * _LANES]
                vblk = vch[:, jb * _LANES:(jb + 1) * _LANES]
                zero = jnp.zeros_like(dblk)
                dw_ref[2 * jb] += lax.dot_general(jnp.where(lane_lo, dblk, zero), vblk, nt_dims,
                                                  preferred_element_type=F32)
                dw_ref[2 * jb + 1] += lax.dot_general(jnp.where(lane_lo, zero, dblk), vblk, nt_dims,
                                                      preferred_element_type=F32)

        @pl.when(i == pl.num_programs(0) - 1)
        def _():
            tri = _causal_mask(SG_CHUNK, SG_CHUNK)
            for g in range(SG_GROUPS):
                dw_ref[g] = jnp.where(tri, dw_ref[g], 0.0)
            lane = lax.broadcasted_iota(jnp.int32, (SG_CHUNK, _LANES), 1)
            out = jnp.zeros((SG_CHUNK, _LANES), F32)
            for g in range(SG_GROUPS):
                blk = dbacc_ref[:, (g // 2) * _LANES:(g // 2 + 1) * _LANES]
                sel = lane_lo if g % 2 == 0 else jnp.logical_not(lane_lo)
                s = jnp.sum(jnp.where(sel, blk, 0.0), axis=1, keepdims=True)
                out = jnp.where(lane == g, s, out)
            db_ref[...] = out

    row = lambda i: (i, 0)
    wspec = pl.BlockSpec((nblk, 2 * SG_CHUNK, SG_CHUNK), lambda i: (0, 0, 0))
    return pl.pallas_call(
        body, name="sgu_bwd", grid=(S // tm,),
        in_specs=[pl.BlockSpec((tm, SG_WIDTH), lambda i: (i, col0)), pl.BlockSpec((tm, SG_WIDTH), row),
                  pl.BlockSpec((tm, SG_WIDTH), row), wspec, wspec,
                  pl.BlockSpec((SG_CHUNK, SG_WIDTH), lambda i: (0, 0))],
        out_specs=[pl.BlockSpec((tm, SG_WIDTH), row), pl.BlockSpec((tm, SG_WIDTH), row),
                   pl.BlockSpec((SG_GROUPS, SG_CHUNK, SG_CHUNK), lambda i: (0, 0, 0)),
                   pl.BlockSpec((SG_CHUNK, _LANES), lambda i: (0, 0))],
        out_shape=[jax.ShapeDtypeStruct((S, SG_WIDTH), BF16), jax.ShapeDtypeStruct((S, SG_WIDTH), BF16),
                   jax.ShapeDtypeStruct((SG_GROUPS, SG_CHUNK, SG_CHUNK), F32),
                   jax.ShapeDtypeStruct((SG_CHUNK, _LANES), F32)],
        scratch_shapes=[pltpu.VMEM((SG_CHUNK, SG_WIDTH), F32)],
        compiler_params=_params(("arbitrary",)),
    )(dmix, vn, u, wst, wst_t, bexp)


def _shift_down(t, halo, n):
    rows = lax.broadcasted_iota(jnp.int32, t.shape, 0)
    out = pltpu.roll(t, n, 0)
    for r in range(n):
        out = jnp.where(rows == r, halo[8 - n + r:8 - n + r + 1, :], out)
    return out


def _shift_up(t, halo, n):
    tm = t.shape[0]
    rows = lax.broadcasted_iota(jnp.int32, t.shape, 0)
    out = pltpu.roll(t, tm - n, 0)
    for r in range(n):
        out = jnp.where(rows == tm - n + r, halo[r:r + 1, :], out)
    return out


def _conv_fwd_call(p, w):
    S, C3 = p.shape
    C = C3 // 3
    tm = _pick(S, _ROW_TILE, 8)
    hb = tm // 8

    def body(p_ref, c_prev, z_prev, w_ref, m_ref):
        i = pl.program_id(0)
        cz = p_ref[:, C:2 * C] * p_ref[:, 2 * C:3 * C]
        czp = jnp.where(i > 0, c_prev[...] * z_prev[...], 0.0)
        y = w_ref[2:3, :] * cz + w_ref[1:2, :] * _shift_down(cz, czp, 1) + w_ref[0:1, :] * _shift_down(cz, czp, 2)
        m_ref[...] = (p_ref[:, 0:C] * y).astype(m_ref.dtype)

    prev = lambda col: (lambda i: (jnp.maximum(i * hb - 1, 0), col))
    return pl.pallas_call(
        body, name="conv_fwd", grid=(S // tm,),
        in_specs=[pl.BlockSpec((tm, C3), lambda i: (i, 0)), pl.BlockSpec((8, C), prev(1)),
                  pl.BlockSpec((8, C), prev(2)), pl.BlockSpec((3, C), lambda i: (0, 0))],
        out_specs=pl.BlockSpec((tm, C), lambda i: (i, 0)),
        out_shape=jax.ShapeDtypeStruct((S, C), BF16), compiler_params=_params(("arbitrary",)),
    )(p, p, p, w)


def _conv_bwd_call(p, w, dm):
    S, C3 = p.shape
    C = C3 // 3
    tm = _pick(S, 256, 8)
    hb = tm // 8
    n_tiles = S // tm

    def body(p_ref, c_prev, z_prev, b_next, dm_ref, dm_next, w_ref, dp_ref, dw_ref):
        i = pl.program_id(0)
        b = p_ref[:, 0:C]
        c = p_ref[:, C:2 * C]
        z = p_ref[:, 2 * C:3 * C]
        cz = c * z
        czp = jnp.where(i > 0, c_prev[...] * z_prev[...], 0.0)
        s1 = _shift_down(cz, czp, 1)
        s2 = _shift_down(cz, czp, 2)
        w0, w1, w2 = w_ref[0:1, :], w_ref[1:2, :], w_ref[2:3, :]
        y = w2 * cz + w1 * s1 + w0 * s2
        dm_t = dm_ref[...]
        dy = dm_t * b
        dyn = jnp.where(i < n_tiles - 1, dm_next[...] * b_next[...], 0.0)
        dcz = w2 * dy + w1 * _shift_up(dy, dyn, 1) + w0 * _shift_up(dy, dyn, 2)
        dp_ref[:, 0:C] = (dm_t * y).astype(dp_ref.dtype)
        dp_ref[:, C:2 * C] = (dcz * z).astype(dp_ref.dtype)
        dp_ref[:, 2 * C:3 * C] = (dcz * c).astype(dp_ref.dtype)
        dw = jnp.concatenate([jnp.sum(dy * s2, axis=0, keepdims=True), jnp.sum(dy * s1, axis=0, keepdims=True),
                              jnp.sum(dy * cz, axis=0, keepdims=True)], axis=0)

        @pl.when(i == 0)
        def _():
            dw_ref[...] = dw

        @pl.when(i != 0)
        def _():
            dw_ref[...] += dw

    prev = lambda col: (lambda i: (jnp.maximum(i * hb - 1, 0), col))
    nxt = lambda col: (lambda i: (jnp.minimum((i + 1) * hb, S // 8 - 1), col))
    return pl.pallas_call(
        body, name="conv_bwd", grid=(n_tiles,),
        in_specs=[pl.BlockSpec((tm, C3), lambda i: (i, 0)), pl.BlockSpec((8, C), prev(1)),
                  pl.BlockSpec((8, C), prev(2)), pl.BlockSpec((8, C), nxt(0)),
                  pl.BlockSpec((tm, C), lambda i: (i, 0)), pl.BlockSpec((8, C), nxt(0)),
                  pl.BlockSpec((3, C), lambda i: (0, 0))],
        out_specs=[pl.BlockSpec((tm, C3), lambda i: (i, 0)), pl.BlockSpec((3, C), lambda i: (0, 0))],
        out_shape=[jax.ShapeDtypeStruct((S, C3), BF16), jax.ShapeDtypeStruct((3, C), F32)],
        compiler_params=_params(("arbitrary",)),
    )(p, p, p, p, dm, dm, w)


def _my_place():
    return lax.axis_index("x"), lax.axis_index("y"), lax.axis_index("c")


def _gather_weights_call(name, shard):
    R, C = shard.shape

    def body(s_ref, o_ref, send_sems, recv_sems, local_sem):
        x, y, c = _my_place()
        mine = 2 * x + y
        local = pltpu.make_async_copy(s_ref, o_ref.at[mine], local_sem)
        local.start()
        peers = [(1 - x, y), (x, 1 - y), (1 - x, 1 - y)]
        copies = []
        for k, (px, py) in enumerate(peers):
            cp = pltpu.make_async_remote_copy(src_ref=s_ref, dst_ref=o_ref.at[mine], send_sem=send_sems.at[k],
                                              recv_sem=recv_sems.at[k], device_id=(px, py, c), device_id_type=MESH)
            cp.start()
            copies.append(cp)
        for k, (px, py) in enumerate(peers):
            pltpu.make_async_remote_copy(src_ref=s_ref, dst_ref=o_ref.at[2 * px + py], send_sem=send_sems.at[k],
                                         recv_sem=recv_sems.at[k], device_id=(px, py, c),
                                         device_id_type=MESH).wait_recv()
        for cp in copies:
            cp.wait_send()
        local.wait()

    any_spec = pl.BlockSpec(memory_space=pl.ANY)
    return pl.pallas_call(
        body, name=name, in_specs=[any_spec], out_specs=any_spec,
        out_shape=jax.ShapeDtypeStruct((4, R, C), shard.dtype),
        scratch_shapes=[pltpu.SemaphoreType.DMA((3,)), pltpu.SemaphoreType.DMA((3,)), pltpu.SemaphoreType.DMA],
        compiler_params=pltpu.CompilerParams(has_side_effects=True),
    )(shard)


_D2D_CHUNKS = 4


_LOCAL_CHUNKS = 8


def _local_copies(src_of, dst_of, rows, sems, base):
    rc = rows // _LOCAL_CHUNKS
    assert rc * _LOCAL_CHUNKS == rows and rc % 16 == 0, rows
    out = []
    for j in range(_LOCAL_CHUNKS):
        sl = pl.ds(j * rc, rc)
        out.append(pltpu.make_async_copy(src_of(sl), dst_of(sl), sems.at[base + j]))
    return out


def _comm_call(name, body, arrays, out_shapes, sem_counts, aliases=None):
    any_spec = pl.BlockSpec(memory_space=pl.ANY)
    return pl.pallas_call(
        body, name=name, in_specs=[any_spec] * len(arrays), out_specs=[any_spec] * len(out_shapes),
        out_shape=out_shapes, scratch_shapes=[pltpu.SemaphoreType.DMA((n,)) for n in sem_counts],
        input_output_aliases=aliases or {}, compiler_params=pltpu.CompilerParams(has_side_effects=True),
    )(*arrays)


def _gather_halves_call(shards):
    na = len(shards)
    for s in shards:
        assert s.shape[0] % (2 * _D2D_CHUNKS * 16) == 0, s.shape

    def body(*refs):
        s_refs, o_refs = refs[:na], refs[na:2 * na]
        ici_send, ici_recv, d2d_send, d2d_recv = refs[2 * na:]
        x, y, c = _my_place()
        mine = 2 * x + y
        chips = [(1 - x, y), (x, 1 - y), (1 - x, 1 - y)]

        def ici(a, k, chip, block):
            Rh = s_refs[a].shape[1] // 2
            my_half = pl.ds(pl.multiple_of(c * Rh, 16), Rh)
            return pltpu.make_async_remote_copy(src_ref=s_refs[a].at[mine, my_half],
                                                dst_ref=o_refs[a].at[block, my_half],
                                                send_sem=ici_send.at[3 * a + k], recv_sem=ici_recv.at[3 * a + k],
                                                device_id=(chip[0], chip[1], c), device_id_type=MESH)

        def d2d(a, k, j, block, half):
            Rh = s_refs[a].shape[1] // 2
            rc = Rh // _D2D_CHUNKS
            rows = pl.ds(pl.multiple_of(half * Rh + j * rc, 16), rc)
            idx = (3 * a + k) * _D2D_CHUNKS + j
            return pltpu.make_async_remote_copy(src_ref=o_refs[a].at[block, rows], dst_ref=o_refs[a].at[block, rows],
                                                send_sem=d2d_send.at[idx], recv_sem=d2d_recv.at[idx],
                                                device_id=(x, y, 1 - c), device_id_type=MESH)

        sends = [ici(a, k, chip, mine) for a in range(na) for k, chip in enumerate(chips)]
        for cp in sends:
            cp.start()
        for a in range(na):
            for k, chip in enumerate(chips):
                block = 2 * chip[0] + chip[1]
                ici(a, k, chip, block).wait_recv()
                for j in range(_D2D_CHUNKS):
                    cp = d2d(a, k, j, block, c)
                    cp.start()
                    sends.append(cp)
        for a in range(na):
            for k, chip in enumerate(chips):
                for j in range(_D2D_CHUNKS):
                    d2d(a, k, j, 2 * chip[0] + chip[1], 1 - c).wait_recv()
        for cp in sends:
            cp.wait_send()

    start = [jnp.broadcast_to(s[None], (4,) + tuple(s.shape)) for s in shards]
    outs = [jax.ShapeDtypeStruct(t.shape, t.dtype) for t in start]
    n_d2d = 3 * na * _D2D_CHUNKS
    return _comm_call("gather_weights", body, start, outs, [3 * na, 3 * na, n_d2d, n_d2d],
                      aliases={a: a for a in range(na)})


def _pair_exchange_call(packed):
    na = len(packed)

    def body(*refs):
        p_refs, o_refs = refs[:na], refs[na:2 * na]
        send_sems, recv_sems = refs[2 * na:]
        x, y, c = _my_place()
        copies = []
        for a in range(na):
            nb, _, Rh, _ = p_refs[a].shape
            rc = Rh // _D2D_CHUNKS
            assert rc * _D2D_CHUNKS == Rh and rc % 16 == 0
            for b in range(nb):
                for j in range(_D2D_CHUNKS):
                    rows = pl.ds(j * rc, rc)
                    idx = (a * nb + b) * _D2D_CHUNKS + j
                    copies.append(pltpu.make_async_remote_copy(
                        src_ref=p_refs[a].at[b, 1 - c, rows], dst_ref=o_refs[a].at[b, rows],
                        send_sem=send_sems.at[idx], recv_sem=recv_sems.at[idx],
                        device_id=(x, y, 1 - c), device_id_type=MESH))
        for t in copies:
            t.start()
        for t in copies:
            t.wait_recv()
        for t in copies:
            t.wait_send()

    outs = [jax.ShapeDtypeStruct((p.shape[0], p.shape[2], p.shape[3]), p.dtype) for p in packed]
    n = sum(p.shape[0] for p in packed) * _D2D_CHUNKS
    return _comm_call("pair_exchange", body, packed, outs, [n, n])


def _pair_add_call(name, packed, other, core):
    nb, _, Rh, C = packed.shape
    tr = _pick(Rh, 512, 16)

    def body(c_ref, p_ref, o_ref, q_ref):
        q_ref[...] = (p_ref[...].astype(F32) + o_ref[...].astype(F32)).astype(q_ref.dtype)

    grid_spec = pltpu.PrefetchScalarGridSpec(
        num_scalar_prefetch=1, grid=(nb, Rh // tr),
        in_specs=[pl.BlockSpec((None, None, tr, C), lambda b, r, c_ref: (b, c_ref[0], r, 0)),
                  pl.BlockSpec((None, tr, C), lambda b, r, c_ref: (b, r, 0))],
        out_specs=pl.BlockSpec((None, tr, C), lambda b, r, c_ref: (b, r, 0)))
    return pl.pallas_call(
        body, name=name, grid_spec=grid_spec, out_shape=jax.ShapeDtypeStruct((nb, Rh, C), packed.dtype),
        compiler_params=_params(("arbitrary", "arbitrary")),
    )(core, packed, other)


def _chip_scatter_call(pairs):
    na = len(pairs)

    def body(*refs):
        p_refs, o_refs = refs[:na], refs[na:2 * na]
        send_sems, recv_sems, local_sems = refs[2 * na:]
        x, y, c = _my_place()
        mine = 2 * x + y
        chips = [(1 - x, y), (x, 1 - y), (1 - x, 1 - y)]
        pending = []
        for a in range(na):
            p_ref, o_ref = p_refs[a], o_refs[a]
            pending += _local_copies(lambda sl: p_ref.at[mine, sl], lambda sl: o_ref.at[mine, sl], p_ref.shape[1],
                                     local_sems, a * _LOCAL_CHUNKS)
        for t in pending:
            t.start()
        copies = []
        for a in range(na):
            for k, (px, py) in enumerate(chips):
                t = pltpu.make_async_remote_copy(src_ref=p_refs[a].at[2 * px + py], dst_ref=o_refs[a].at[mine],
                                                 send_sem=send_sems.at[3 * a + k], recv_sem=recv_sems.at[3 * a + k],
                                                 device_id=(px, py, c), device_id_type=MESH)
                t.start()
                copies.append(t)
        for a in range(na):
            for k, (px, py) in enumerate(chips):
                pltpu.make_async_remote_copy(src_ref=p_refs[a].at[mine], dst_ref=o_refs[a].at[2 * px + py],
                                             send_sem=send_sems.at[3 * a + k], recv_sem=recv_sems.at[3 * a + k],
                                             device_id=(px, py, c), device_id_type=MESH).wait_recv()
        for t in copies:
            t.wait_send()
        for t in pending:
            t.wait()

    outs = [jax.ShapeDtypeStruct(p.shape, p.dtype) for p in pairs]
    return _comm_call("chip_scatter", body, pairs, outs, [3 * na, 3 * na, na * _LOCAL_CHUNKS])


def _sum_slots_call(name, parts, core):
    n, R, C = parts.shape
    tr = _pick(R, 256, 8)

    def body(c_ref, p_ref, o_ref):
        acc = p_ref[0].astype(F32)
        for s in range(1, n):
            acc = acc + p_ref[s].astype(F32)
        o_ref[...] = acc

    grid_spec = pltpu.PrefetchScalarGridSpec(
        num_scalar_prefetch=1, grid=(R // tr,),
        in_specs=[pl.BlockSpec((n, tr, C), lambda i, c_ref: (0, i, 0))],
        out_specs=pl.BlockSpec((None, tr, C), lambda i, c_ref: (c_ref[0], i, 0)))
    return pl.pallas_call(
        body, name=name, grid_spec=grid_spec, out_shape=jax.ShapeDtypeStruct((2, R, C), F32),
        compiler_params=_params(("arbitrary",)),
    )(core, parts)


def _sibling_share_call(halves):
    na = len(halves)
    nch = 2 * _D2D_CHUNKS

    def body(*refs):
        h_refs, o_refs = refs[:na], refs[na:2 * na]
        send_sems, recv_sems = refs[2 * na:]
        x, y, c = _my_place()

        def cp(a, j, slot):
            rc = h_refs[a].shape[1] // nch
            rows = pl.ds(j * rc, rc)
            return pltpu.make_async_remote_copy(src_ref=h_refs[a].at[slot, rows], dst_ref=o_refs[a].at[slot, rows],
                                                send_sem=send_sems.at[a * nch + j], recv_sem=recv_sems.at[a * nch + j],
                                                device_id=(x, y, 1 - c), device_id_type=MESH)

        copies = [cp(a, j, c) for a in range(na) for j in range(nch)]
        for t in copies:
            t.start()
        for a in range(na):
            for j in range(nch):
                cp(a, j, 1 - c).wait_recv()
        for t in copies:
            t.wait_send()

    for h in halves:
        assert h.shape[1] % (nch * 8) == 0, h.shape
    outs = [jax.ShapeDtypeStruct(h.shape, h.dtype) for h in halves]
    return _comm_call("sibling_share", body, halves, outs, [na * nch, na * nch], aliases={a: a for a in range(na)})


def _allreduce_small_call(part):
    R, C = part.shape

    def body(p_ref, o_ref, slots, send_sems, recv_sems):
        x, y, c = _my_place()
        me = 4 * x + 2 * y + c
        peers = []
        for k in range(1, 8):
            px = x ^ (k >> 2) if (k >> 2) else x
            py = y ^ ((k >> 1) & 1) if ((k >> 1) & 1) else y
            pc = c ^ (k & 1) if (k & 1) else c
            peers.append((px, py, pc))
        copies = []
        for k, (px, py, pc) in enumerate(peers):
            cp = pltpu.make_async_remote_copy(src_ref=p_ref, dst_ref=slots.at[me], send_sem=send_sems.at[k],
                                              recv_sem=recv_sems.at[k], device_id=(px, py, pc), device_id_type=MESH)
            cp.start()
            copies.append(cp)
        slots[me] = p_ref[...]
        for k, (px, py, pc) in enumerate(peers):
            pltpu.make_async_remote_copy(src_ref=p_ref, dst_ref=slots.at[4 * px + 2 * py + pc],
                                         send_sem=send_sems.at[k], recv_sem=recv_sems.at[k],
                                         device_id=(px, py, pc), device_id_type=MESH).wait_recv()
        for cp in copies:
            cp.wait_send()
        acc = slots[0]
        for s in range(1, 8):
            acc = acc + slots[s]
        o_ref[...] = acc

    vm = pl.BlockSpec(memory_space=pltpu.VMEM)
    return pl.pallas_call(
        body, name="allreduce_small", in_specs=[vm], out_specs=vm,
        out_shape=jax.ShapeDtypeStruct((R, C), F32),
        scratch_shapes=[pltpu.VMEM((8, R, C), F32), pltpu.SemaphoreType.DMA((7,)), pltpu.SemaphoreType.DMA((7,))],
        compiler_params=pltpu.CompilerParams(has_side_effects=True, vmem_limit_bytes=_VMEM_LIMIT),
    )(part)


def _adamw_call(name, w, g, m, v):
    shape = w.shape
    cols = shape[-1]
    rows = int(np.prod(shape[:-1])) if len(shape) > 1 else 1
    w2, g2, m2, v2 = (t.reshape(rows, cols) for t in (w, g, m, v))
    tr = _pick(rows, 256, 8)
    c1 = 1.0 / (1.0 - ADAM_B1 ** ADAM_STEP)
    c2 = 1.0 / (1.0 - ADAM_B2 ** ADAM_STEP)

    def body(w_ref, g_ref, m_ref, v_ref, d_ref, nm_ref, nv_ref):
        gr = g_ref[...]
        m_new = ADAM_B1 * m_ref[...] + (1.0 - ADAM_B1) * gr
        v_new = ADAM_B2 * v_ref[...] + (1.0 - ADAM_B2) * (gr * gr)
        m_hat = m_new / (1.0 - ADAM_B1 ** ADAM_STEP)
        v_hat = v_new / (1.0 - ADAM_B2 ** ADAM_STEP)
        d_ref[...] = -ADAM_LR * (m_hat / (jnp.sqrt(v_hat) + ADAM_EPS) + ADAM_WD * w_ref[...])
        nm_ref[...] = m_new
        nv_ref[...] = v_new

    spec = pl.BlockSpec((tr, cols), lambda i: (i, 0))
    d, nm, nv = pl.pallas_call(
        body, name=name, grid=(rows // tr,), in_specs=[spec] * 4, out_specs=[spec] * 3,
        out_shape=[jax.ShapeDtypeStruct((rows, cols), F32)] * 3, compiler_params=_params(("arbitrary",)),
    )(w2, g2, m2, v2)
    return d.reshape(shape), nm.reshape(shape), nv.reshape(shape)


def _pad_rows(flat, mult):
    n = flat.shape[0]
    unit = PACK_COLS * mult
    total = -(-n // unit) * unit
    return jnp.pad(flat, (0, total - n)).reshape(total // PACK_COLS, PACK_COLS)


def _pad_axis(arr, axis, mult):
    n = arr.shape[axis]
    total = -(-n // mult) * mult
    if total == n:
        return arr
    widths = [(0, 0)] * arr.ndim
    widths[axis] = (0, total - n)
    return jnp.pad(arr, widths)


def _shard_slice(arr, axis, blk, nblk=4):
    w = arr.shape[axis] // nblk
    return lax.slice_in_dim(arr, blk * w, (blk + 1) * w, axis=axis)


_NT = (((1,), (1,)), ((), ()))
_TN = (((0,), (0,)), ((), ()))


def _ffn_fwd(tag, x, h, wg, wu, wd, g_next):
    S, D = x.shape
    (wg, gi), (wu, ui), (wd, di) = wg, wu, wd
    NB, Fb = wg.shape[0], wg.shape[2]
    tm = _pick(S, 1024, 8)

    def gate_up(h_ref, wg_ref, wu_ref, a_ref, u_ref, s_ref):
        h_t = _mx(h_ref[...])
        a = jnp.dot(h_t, _mx(wg_ref[...]), preferred_element_type=F32)
        u = jnp.dot(h_t, _mx(wu_ref[...]), preferred_element_type=F32)
        sig = _sigmoid(a)
        silu = a * sig
        a_ref[...] = (u * (sig * (1.0 + a * (1.0 - sig)))).astype(a_ref.dtype)
        u_ref[...] = silu.astype(u_ref.dtype)
        s_ref[...] = (silu * u).astype(s_ref.dtype)

    hid = pl.BlockSpec((None, tm, Fb), lambda b, i: (b, i, 0))
    a, u, s = pl.pallas_call(
        gate_up, name=tag + "_gate_up", grid=(NB, S // tm),
        in_specs=[pl.BlockSpec((tm, D), lambda b, i: (i, 0)), pl.BlockSpec((None, D, Fb), lambda b, i: (b, gi, 0)),
                  pl.BlockSpec((None, D, Fb), lambda b, i: (b, ui, 0))], out_specs=[hid] * 3,
        out_shape=[jax.ShapeDtypeStruct((NB, S, Fb), BF16)] * 3, compiler_params=_params(("arbitrary", "arbitrary")),
    )(h, wg, wu)

    tm2 = _pick(S, 512, 8)

    def down(s_ref, wd_ref, x_ref, g_ref, xo_ref, ho_ref):
        acc = jnp.dot(_mx(s_ref[0]), _mx(wd_ref[0]), preferred_element_type=F32)
        for b in range(1, NB):
            acc = acc + jnp.dot(_mx(s_ref[b]), _mx(wd_ref[b]), preferred_element_type=F32)
        x_new = x_ref[...] + 0.5 * acc
        xo_ref[...] = x_new
        ho_ref[...] = (_rms_stats(x_new)[0] * g_ref[...]).astype(ho_ref.dtype)

    row = pl.BlockSpec((tm2, D), lambda i: (i, 0))
    x_new, h_next = pl.pallas_call(
        down, name=tag + "_down", grid=(S // tm2,),
        in_specs=[pl.BlockSpec((NB, tm2, Fb), lambda i: (0, i, 0)), pl.BlockSpec((NB, Fb, D), lambda i: (0, di, 0)),
                  row, pl.BlockSpec((1, D), lambda i: (0, 0))],
        out_specs=[row, row], out_shape=[jax.ShapeDtypeStruct((S, D), F32), jax.ShapeDtypeStruct((S, D), BF16)],
        compiler_params=_params(("arbitrary",)),
    )(s, wd, x, g_next)
    return x_new, h_next, (x, h, a, u, s)


def _ffn_bwd(tag, saved, dx_out, dxb, wg, wu, wd, gain):
    x, h, a, u, s = saved
    S, D = x.shape
    (wg, gi), (wu, ui), (wd, di) = wg, wu, wd
    NB, Fb = wg.shape[0], wg.shape[2]
    tm = _pick(S, 1024, 8)
    tk = _pick(S, 1024, 128)
    nk = S // tk

    def dgate_up(d_ref, wd_ref, a_ref, u_ref, da_ref, du_ref):
        ds = 0.5 * lax.dot_general(_mx(d_ref[...]), _mx(wd_ref[...]), _NT, preferred_element_type=F32)
        da_ref[...] = (ds * a_ref[...].astype(F32)).astype(da_ref.dtype)
        du_ref[...] = (ds * u_ref[...].astype(F32)).astype(du_ref.dtype)

    hid = pl.BlockSpec((None, tm, Fb), lambda b, i: (b, i, 0))
    da, du = pl.pallas_call(
        dgate_up, name=tag + "_dgate_up", grid=(NB, S // tm),
        in_specs=[pl.BlockSpec((tm, D), lambda b, i: (i, 0)), pl.BlockSpec((None, Fb, D), lambda b, i: (b, di, 0)),
                  hid, hid],
        out_specs=[hid, hid], out_shape=[jax.ShapeDtypeStruct((NB, S, Fb), BF16)] * 2,
        compiler_params=_params(("arbitrary", "arbitrary")),
    )(dxb, wd, a, u)

    def dw_down(s_ref, d_ref, o_ref, acc_ref):
        k = pl.program_id(1)
        p = lax.dot_general(_mx(s_ref[...]), _mx(d_ref[...]), _TN, preferred_element_type=F32)

        @pl.when(k == 0)
        def _():
            acc_ref[...] = p

        @pl.when(k != 0)
        def _():
            acc_ref[...] += p

        @pl.when(k == nk - 1)
        def _():
            o_ref[...] = (0.5 * acc_ref[...]).astype(o_ref.dtype)

    hk = pl.BlockSpec((None, tk, Fb), lambda b, k: (b, k, 0))
    dwd = pl.pallas_call(
        dw_down, name=tag + "_dw_down", grid=(NB, nk),
        in_specs=[hk, pl.BlockSpec((tk, D), lambda b, k: (k, 0))],
        out_specs=pl.BlockSpec((None, Fb, D), lambda b, k: (b, 0, 0)),
        out_shape=jax.ShapeDtypeStruct((NB, Fb, D), _WIRE_DTYPE), scratch_shapes=[pltpu.VMEM((Fb, D), F32)],
        compiler_params=_params(("arbitrary", "arbitrary")),
    )(s, dxb)

    def dw_gate_up(h_ref, da_ref, du_ref, og_ref, ou_ref, accg_ref, accu_ref):
        k = pl.program_id(1)
        h_t = _mx(h_ref[...])
        pg = lax.dot_general(h_t, _mx(da_ref[...]), _TN, preferred_element_type=F32)
        pu = lax.dot_general(h_t, _mx(du_ref[...]), _TN, preferred_element_type=F32)

        @pl.when(k == 0)
        def _():
            accg_ref[...] = pg
            accu_ref[...] = pu

        @pl.when(k != 0)
        def _():
            accg_ref[...] += pg
            accu_ref[...] += pu

        @pl.when(k == nk - 1)
        def _():
            og_ref[...] = accg_ref[...].astype(og_ref.dtype)
            ou_ref[...] = accu_ref[...].astype(ou_ref.dtype)

    wout = pl.BlockSpec((None, D, Fb), lambda b, k: (b, 0, 0))
    dwg, dwu = pl.pallas_call(
        dw_gate_up, name=tag + "_dw_gate_up", grid=(NB, nk),
        in_specs=[pl.BlockSpec((tk, D), lambda b, k: (k, 0)), hk, hk], out_specs=[wout, wout],
        out_shape=[jax.ShapeDtypeStruct((NB, D, Fb), _WIRE_DTYPE)] * 2,
        scratch_shapes=[pltpu.VMEM((D, Fb), F32)] * 2, compiler_params=_params(("arbitrary", "arbitrary")),
    )(h, da, du)

    tm2 = _pick(S, 512, 8)

    def dx_body(da_ref, du_ref, wg_hbm, wu_hbm, x_ref, dxo_ref, g_ref, dx_ref, dxb_ref, dg_ref, wg_v, wu_v, sem):
        i = pl.program_id(0)

        @pl.when(i == 0)
        def _():
            cg = pltpu.make_async_copy(wg_hbm.at[:, pl.ds(gi * D, D), :], wg_v, sem.at[0])
            cu = pltpu.make_async_copy(wu_hbm.at[:, pl.ds(ui * D, D), :], wu_v, sem.at[1])
            cg.start()
            cu.start()
            cg.wait()
            cu.wait()

        dh = None
        for b in range(NB):
            t = lax.dot_general(_mx(da_ref[b]), wg_v[b], _NT, preferred_element_type=F32)
            t = t + lax.dot_general(_mx(du_ref[b]), wu_v[b], _NT, preferred_element_type=F32)
            dh = t if dh is None else dh + t
        dx_n, dg = _rms_bwd(x_ref[...], g_ref[...], dh)
        dx = dxo_ref[...] + dx_n
        dx_ref[...] = dx
        dxb_ref[...] = dx.astype(dxb_ref.dtype)
        dg = jnp.sum(dg, axis=0, keepdims=True)

        @pl.when(i == 0)
        def _():
            dg_ref[...] = dg

        @pl.when(i != 0)
        def _():
            dg_ref[...] += dg

    row = pl.BlockSpec((tm2, D), lambda i: (i, 0))
    hid2 = pl.BlockSpec((NB, tm2, Fb), lambda i: (0, i, 0))
    anyspec = pl.BlockSpec(memory_space=pl.ANY)
    fixed = pl.BlockSpec((1, D), lambda i: (0, 0))
    dx, dxb_new, dgain = pl.pallas_call(
        dx_body, name=tag + "_dx", grid=(S // tm2,),
        in_specs=[hid2, hid2, anyspec, anyspec, row, row, fixed], out_specs=[row, row, fixed],
        out_shape=[jax.ShapeDtypeStruct((S, D), F32), jax.ShapeDtypeStruct((S, D), BF16),
                   jax.ShapeDtypeStruct((1, D), F32)],
        scratch_shapes=[pltpu.VMEM((NB, D, Fb), wg.dtype), pltpu.VMEM((NB, D, Fb), wu.dtype),
                        pltpu.SemaphoreType.DMA((2,))],
        compiler_params=_params(("arbitrary",)),
    )(da, du, wg, wu, x, dx_out, gain)
    return dx, dxb_new, dgain, dwg, dwu, dwd


def _conv_mixer_fwd(tag, x, h, w_in, w_taps, w_out, g_next):
    S, D = x.shape
    C3 = w_in.shape[1]
    tm = _pick(S, 512, 8)
    p, = _fused_matmul(tag + "_in", 'nn', [_op(h)], [_op(w_in)], [(0, 0, 0)], 1, _ident_epi(), [F32],
                       S, C3, D, tm, _pick(C3, 1024, 128), D)
    m = _conv_fwd_call(p, w_taps)
    x_new, h_next = _fused_matmul(tag + "_out", 'nn', [_op(m)], [_op(w_out)], [(0, 0, 0)], 1, _resid_norm_epi(1.0),
                                  [F32, BF16], S, D, D, tm, D, D, tile_extras=[x], row_extras=[g_next])
    return x_new, h_next, (x, h, p, m)


def _conv_mixer_bwd(tag, saved, dx_out, dxb, w_in, w_taps, w_out, gain):
    x, h, p, m = saved
    S, D = x.shape
    C3 = w_in.shape[1]
    tm = _pick(S, 512, 8)
    tk = _pick(S, 1024, 128)
    dm, = _fused_matmul(tag + "_dm", 'nt', [_op(dxb)], [_op(w_out)], [(0, 0, 0)], 1, _ident_epi(), [F32],
                        S, D, D, tm, D, D)
    dw_out, = _fused_matmul(tag + "_dw_out", 'tn', [_op(m)], [_op(dxb)], [(0, 0, 0)], 1, _ident_epi(), [F32],
                            D, D, S, D, D, tk)
    dp, dtaps = _conv_bwd_call(p, w_taps, dm)
    dw_in, = _fused_matmul(tag + "_dw_in", 'tn', [_op(h)], [_op(dp)], [(0, 0, 0)], 1, _ident_epi(), [F32],
                           D, C3, S, D, _pick(C3, 1024, 128), tk)
    dx, dxb_new, dgain = _fused_matmul(tag + "_dx", 'nt', [_op(dp)], [_op(w_in)], [(0, 0, 0)], 1, _norm_bwd_epi,
                                       [F32, BF16], S, D, C3, tm, D, _pick(C3, 1024, 128),
                                       tile_extras=[x, dx_out], row_extras=[gain], n_colsum=1)
    return dx, dxb_new, dgain, dw_in, dtaps, dw_out


def _attn_scale():
    return np.float32(QK_DIM ** -0.5)


def _even_mixer_fwd(tag, x, h, wts, tables, g_next):
    S, D = x.shape
    cos, sa, sb = tables
    tm = _pick(S, 512, 8)
    AW = HEADS * HP
    proj, = _fused_matmul(tag + "_in", 'nn', [_op(h)], [_op(wts['w_in'])], [(0, 0, 0)], 1, _ident_epi(), [F32],
                          S, PROJ_W, D, tm, _pick(PROJ_W, 896, 128), D)
    cqn, ckvn, kr, u, vn = _even_prep_call(proj, wts['q_norm'], wts['kv_norm'], wts['sg_norm'], cos, sa, sb)
    scale = _attn_scale()

    def q_epi(accs, tiles, rows, mrows):
        c_t, a_t, b_t = mrows
        heads = [_rope(accs[0][:, hh * HP:(hh + 1) * HP], c_t, a_t, b_t) * scale for hh in range(HEADS)]
        return [jnp.concatenate(heads, axis=1)]

    q, = _fused_matmul(tag + "_q", 'nn', [_op(cqn)], [_op(wts['w_q'])], [(0, 0, 0)], 1, q_epi, [BF16],
                       S, AW, Q_LORA, tm, AW, Q_LORA, mrow_extras=[cos, sa, sb])

    def kv_epi(accs, tiles, rows, mrows):
        return [accs[0] + jnp.concatenate([mrows[0].astype(F32)] * HEADS, axis=1), accs[1]]

    k, v = _fused_matmul(tag + "_kv", 'nn', [_op(ckvn)], [_op(wts['w_k']), _op(wts['w_v'])],
                         [(0, 0, 0), (0, 1, 1)], 2, kv_epi, [BF16, BF16], S, AW, KV_LORA, tm, AW, KV_LORA,
                         mrow_extras=[kr])
    o, lse = _flash_fwd_call(q, k, v)
    mix = _sgu_fwd_call(vn, u, o, wts['sg_wst'], wts['sg_bexp'])
    x_new, h_next = _fused_matmul(tag + "_out", 'nn', [_op(mix)], [_op(wts['w_out'])], [(0, 0, 0)], 1,
                                  _resid_norm_epi(1.0), [F32, BF16], S, D, AW + SG_WIDTH, tm, D, AW + SG_WIDTH,
                                  tile_extras=[x], row_extras=[g_next])
    return x_new, h_next, (x, h, proj, cqn, ckvn, u, vn, q, k, v, o, lse, mix)


def _even_mixer_bwd(tag, saved, dx_out, dxb, wts, tables, gain):
    x, h, proj, cqn, ckvn, u, vn, q, k, v, o, lse, mix = saved
    S, D = x.shape
    cos, sa, sb = tables
    tm = _pick(S, 512, 8)
    tk = _pick(S, 1024, 128)
    AW = HEADS * HP
    MW = AW + SG_WIDTH
    dmix, = _fused_matmul(tag + "_dmix", 'nt', [_op(dxb)], [_op(wts['w_out'])], [(0, 0, 0)], 1, _ident_epi(), [BF16],
                          S, MW, D, tm, _pick(MW, 768, 128), D)
    dw_out, = _fused_matmul(tag + "_dw_out", 'tn', [_op(mix)], [_op(dxb)], [(0, 0, 0)], 1, _ident_epi(), [F32],
                            MW, D, S, _pick(MW, 768, 128), D, tk)
    du, dvn, dsg_w, dsg_b = _sgu_bwd_call(dmix, vn, u, wts['sg_wst'], wts['sg_wst_t'], wts['sg_bexp'])
    delta = _attn_delta_call(o, dmix)
    dq, dk, dv = _flash_bwd_call(q, k, v, dmix, lse, delta)
    scale = _attn_scale()

    def dq_epi(accs, tiles, rows, mrows):
        return accs

    def dq_pre_call():
        tr = _pick(S, 256, 8)

        def body(d_ref, c_ref, a_ref, b_ref, o_ref):
            for hh in range(HEADS):
                t = _rope_t(d_ref[:, hh * HP:(hh + 1) * HP], c_ref[...], a_ref[...], b_ref[...]) * scale
                o_ref[:, hh * HP:(hh + 1) * HP] = t.astype(o_ref.dtype)

        row = lambda i: (i, 0)
        return pl.pallas_call(
            body, name=tag + "_dq_unrope", grid=(S // tr,),
            in_specs=[pl.BlockSpec((tr, AW), row)] + [pl.BlockSpec((tr, HP), row)] * 3,
            out_specs=pl.BlockSpec((tr, AW), row), out_shape=jax.ShapeDtypeStruct((S, AW), BF16),
            compiler_params=_params(("arbitrary",)),
        )(dq, cos, sa, sb)

    dqp = dq_pre_call()
    dw_q, = _fused_matmul(tag + "_dw_q", 'tn', [_op(cqn)], [_op(dqp)], [(0, 0, 0)], 1, _ident_epi(), [F32],
                          Q_LORA, AW, S, Q_LORA, AW, tk)
    dcqn, = _fused_matmul(tag + "_dcq", 'nt', [_op(dqp)], [_op(wts['w_q'])], [(0, 0, 0)], 1, dq_epi, [F32],
                          S, Q_LORA, AW, tm, Q_LORA, AW)
    dw_k, dw_v = _fused_matmul(tag + "_dw_kv", 'tn', [_op(ckvn)], [_op(dk), _op(dv)], [(0, 0, 0), (0, 1, 1)], 2,
                               _ident_epi(), [F32, F32], KV_LORA, AW, S, KV_LORA, AW, tk)
    dckvn, = _fused_matmul(tag + "_dckv", 'nt', [_op(dk), _op(dv)], [_op(wts['w_k']), _op(wts['w_v'])],
                           [(0, 0, 0), (1, 1, 0)], 1, dq_epi, [F32], S, KV_LORA, AW, tm, KV_LORA, AW)
    dproj, dqn, dkvn, dsgn = _even_prep_bwd_call(proj, wts['q_norm'], wts['kv_norm'], wts['sg_norm'], cos, sa, sb,
                                                 dcqn, dckvn, dk, du, dvn)
    dw_in, = _fused_matmul(tag + "_dw_in", 'tn', [_op(h)], [_op(dproj)], [(0, 0, 0)], 1, _ident_epi(), [F32],
                           D, PROJ_W, S, D, _pick(PROJ_W, 896, 128), tk)
    dx, dxb_new, dgain = _fused_matmul(tag + "_dx", 'nt', [_op(dproj)], [_op(wts['w_in'])], [(0, 0, 0)], 1,
                                       _norm_bwd_epi, [F32, BF16], S, D, PROJ_W, tm, D, _pick(PROJ_W, 896, 128),
                                       tile_extras=[x, dx_out], row_extras=[gain], n_colsum=1)
    grads = dict(w_in=dw_in, w_q=dw_q, w_k=dw_k, w_v=dw_v, w_out=dw_out, q_norm=dqn, kv_norm=dkvn, sg_norm=dsgn,
                 sg_w=dsg_w, sg_b=dsg_b)
    return dx, dxb_new, dgain, grads


def _even_weights(w_in, w_uq, w_ukv, w_out, q_norm, kv_norm, sg_norm, sg_w, sg_b):
    D = w_in.shape[0]
    kr_cols = jnp.pad(w_in[:, Q_LORA + KV_LORA:Q_LORA + KV_LORA + ROPE], ((0, 0), (NOPE, HP - QK_DIM)))
    w_in_p = jnp.concatenate([w_in[:, :Q_LORA + KV_LORA], kr_cols, w_in[:, Q_LORA + KV_LORA + ROPE:]], axis=1)
    wq = w_uq.reshape(Q_LORA, HEADS, QK_DIM)
    w_q = jnp.pad(wq, ((0, 0), (0, 0), (0, HP - QK_DIM))).reshape(Q_LORA, HEADS * HP)
    wkv = w_ukv.reshape(KV_LORA, HEADS, NOPE + VDIM)
    w_k = jnp.pad(wkv[:, :, :NOPE], ((0, 0), (0, 0), (0, HP - NOPE))).reshape(KV_LORA, HEADS * HP)
    w_v = jnp.pad(wkv[:, :, NOPE:], ((0, 0), (0, 0), (0, HP - VDIM))).reshape(KV_LORA, HEADS * HP)
    wo_a = w_out[:HEADS * VDIM].reshape(HEADS, VDIM, D)
    wo_a = jnp.pad(wo_a, ((0, 0), (0, HP - VDIM), (0, 0))).reshape(HEADS * HP, D)
    w_out_p = jnp.concatenate([wo_a, w_out[HEADS * VDIM:]], axis=0)
    tri = jnp.tril(jnp.ones((SG_CHUNK, SG_CHUNK), F32))
    wm = sg_w * tri
    wst = wm.reshape(SG_GROUPS // 2, 2 * SG_CHUNK, SG_CHUNK).astype(_MXU_DTYPE)
    wst_t = jnp.swapaxes(wm, 1, 2).reshape(SG_GROUPS // 2, 2 * SG_CHUNK, SG_CHUNK).astype(_MXU_DTYPE)
    bexp = jnp.repeat(sg_b.T, SG_GDIM, axis=1)
    return dict(w_in=w_in_p, w_q=w_q, w_k=w_k, w_v=w_v, w_out=w_out_p, sg_wst=wst, sg_wst_t=wst_t, sg_bexp=bexp,
                q_norm=q_norm.reshape(1, -1), kv_norm=kv_norm.reshape(1, -1), sg_norm=sg_norm.reshape(1, -1))


def _even_grads_unpad(g):
    d_in = g['w_in']
    kr0 = Q_LORA + KV_LORA
    dw_in = jnp.concatenate([d_in[:, :kr0], d_in[:, kr0 + NOPE:kr0 + QK_DIM], d_in[:, kr0 + HP:]], axis=1)
    dw_uq = g['w_q'].reshape(Q_LORA, HEADS, HP)[:, :, :QK_DIM].reshape(Q_LORA, HEADS * QK_DIM)
    dk = g['w_k'].reshape(KV_LORA, HEADS, HP)[:, :, :NOPE]
    dv = g['w_v'].reshape(KV_LORA, HEADS, HP)[:, :, :VDIM]
    dw_ukv = jnp.concatenate([dk, dv], axis=2).reshape(KV_LORA, HEADS * (NOPE + VDIM))
    D = d_in.shape[0]
    wo = g['w_out']
    wo_a = wo[:HEADS * HP].reshape(HEADS, HP, D)[:, :VDIM].reshape(HEADS * VDIM, D)
    dw_out = jnp.concatenate([wo_a, wo[HEADS * HP:]], axis=0)
    dsg_b = g['sg_b'][:, :SG_GROUPS].T
    return dict(even_w_in=dw_in, w_uq=dw_uq, w_ukv=dw_ukv, even_w_out=dw_out, q_norm=g['q_norm'][0],
                kv_norm=g['kv_norm'][0], sg_norm=g['sg_norm'][0], sg_w=g['sg_w'], sg_b=dsg_b)


def kernel(x, positions, ffn_pre_norm, ffn_pre_w_gate, ffn_pre_w_up, ffn_pre_w_down, mix_norm, ffn_post_norm, ffn_post_w_gate, ffn_post_w_up, ffn_post_w_down, even_w_in, q_norm, w_uq, kv_norm, w_ukv, sg_norm, sg_w, sg_b, even_w_out, conv_w_in, conv_w, conv_w_out, final_norm, loss_target, m_ffn_pre_norm, m_ffn_pre_w_gate, m_ffn_pre_w_up, m_ffn_pre_w_down, m_mix_norm, m_ffn_post_norm, m_ffn_post_w_gate, m_ffn_post_w_up, m_ffn_post_w_down, m_even_w_in, m_q_norm, m_w_uq, m_kv_norm, m_w_ukv, m_sg_norm, m_sg_w, m_sg_b, m_even_w_out, m_conv_w_in, m_conv_w, m_conv_w_out, m_final_norm, v_ffn_pre_norm, v_ffn_pre_w_gate, v_ffn_pre_w_up, v_ffn_pre_w_down, v_mix_norm, v_ffn_post_norm, v_ffn_post_w_gate, v_ffn_post_w_up, v_ffn_post_w_down, v_even_w_in, v_q_norm, v_w_uq, v_kv_norm, v_w_ukv, v_sg_norm, v_sg_w, v_sg_b, v_even_w_out, v_conv_w_in, v_conv_w, v_conv_w_out, v_final_norm):
    env = dict(locals())
    w_loc = {n: env[n] for n in WEIGHTS}
    m_loc = {n: env['m_' + n] for n in WEIGHTS}
    v_loc = {n: env['v_' + n] for n in WEIGHTS}
    S, D = x.shape[1], x.shape[2]
    depth = ffn_pre_norm.shape[0]
    xs = x.reshape(S, D)
    target = loss_target.reshape(S, D)

    Fb = ffn_pre_w_gate.shape[2]
    wire2d = lambda n, cols: w_loc[n].astype(_WIRE_DTYPE).reshape(-1, cols)
    shard_a = _pad_axis(jnp.concatenate([wire2d(n, Fb) for n in GROUP_A], axis=0), 0, PACK_ROW_MULT)
    shard_b = _pad_axis(jnp.concatenate([wire2d(n, D) for n in GROUP_B], axis=0), 0, PACK_ROW_MULT)
    shard_c = _pad_rows(jnp.concatenate([w_loc[n].astype(_WIRE_DTYPE).reshape(-1) for n in GROUP_C]), PACK_ROW_MULT)
    gat_a, gat_b, gat_c = _gather_halves_call([shard_a, shard_b, shard_c])
    taps = _gather_weights_call("gather_taps", _pad_rows(conv_w.reshape(-1), 8)).reshape(4, -1)
    taps = jnp.concatenate([taps[b, :conv_w.size].reshape(conv_w.shape) for b in range(4)], axis=2)
    full = {}
    for ia, n in enumerate(GROUP_A):
        full[n] = [(gat_a, ia * depth + l) for l in range(depth)]
    row = 0
    for n in GROUP_B:
        shp = w_loc[n].shape
        rows = shp[0] * shp[1]
        if n in FFN_WEIGHTS:
            full[n] = [(gat_b, row // shp[1] + l) for l in range(depth)]
        else:
            full[n] = jnp.concatenate([gat_b[b, row:row + rows].reshape(shp) for b in range(4)], axis=1)
        row += rows
    gflat = gat_c.reshape(4, -1)
    off = 0
    for n in GROUP_C:
        shp = w_loc[n].shape
        size = int(np.prod(shp))
        full[n] = jnp.concatenate([gflat[b, off:off + size].reshape(shp) for b in range(4)], axis=SHARD_AXIS[n])
        off += size

    inv_freq = ROPE_THETA ** (-jnp.arange(0, ROPE, 2, dtype=F32) / ROPE)
    half = ROPE // 2
    zeros = lambda n: jnp.zeros((n,), F32)
    ones = jnp.ones((half,), F32)
    invf = jnp.concatenate([zeros(NOPE), inv_freq, inv_freq, zeros(HP - QK_DIM)]).reshape(1, HP)
    mask_a = jnp.concatenate([zeros(NOPE), -ones, zeros(HP - NOPE - half)]).reshape(1, HP)
    mask_b = jnp.concatenate([zeros(NOPE + half), ones, zeros(HP - QK_DIM)]).reshape(1, HP)
    tables = _rope_tables_call(positions.reshape(S, 1), invf, mask_a, mask_b)

    even_w = []
    for e in range((depth + 1) // 2):
        even_w.append(_even_weights(full['even_w_in'][e], full['w_uq'][e], full['w_ukv'][e], full['even_w_out'][e],
                                    q_norm[e], kv_norm[e], sg_norm[e], sg_w[e], sg_b[e]))

    def gain_row(arr, l):
        return arr[l].reshape(1, D)

    saved = []
    h = _rmsnorm_call("first_norm", xs, gain_row(ffn_pre_norm, 0))
    xc = xs
    for l in range(depth):
        xc, h, s_pre = _ffn_fwd(f"l{l}_pre", xc, h, full['ffn_pre_w_gate'][l], full['ffn_pre_w_up'][l],
                                full['ffn_pre_w_down'][l], gain_row(mix_norm, l))
        if l % 2 == 0:
            xc, h, s_mix = _even_mixer_fwd(f"l{l}_mix", xc, h, even_w[l // 2], tables, gain_row(ffn_post_norm, l))
        else:
            o = l // 2
            xc, h, s_mix = _conv_mixer_fwd(f"l{l}_mix", xc, h, full['conv_w_in'][o], taps[o],
                                           full['conv_w_out'][o], gain_row(ffn_post_norm, l))
        g_next = gain_row(ffn_pre_norm, l + 1) if l + 1 < depth else final_norm.reshape(1, D)
        xc, h, s_post = _ffn_fwd(f"l{l}_post", xc, h, full['ffn_post_w_gate'][l], full['ffn_post_w_up'][l],
                                 full['ffn_post_w_down'][l], g_next)
        saved.append((s_pre, s_mix, s_post))

    dx, dxb, d_final, loss_part = _loss_call(xc, target, final_norm.reshape(1, D))
    loss = lax.psum(loss_part[0, 0], ("x", "y", "c"))

    gl = {n: [None] * w_loc[n].shape[0] for n in WEIGHTS if n != 'final_norm'}
    for l in reversed(range(depth)):
        s_pre, s_mix, s_post = saved[l]
        dx, dxb, dgain, dwg, dwu, dwd = _ffn_bwd(f"l{l}_post", s_post, dx, dxb, full['ffn_post_w_gate'][l],
                                                 full['ffn_post_w_up'][l], full['ffn_post_w_down'][l],
                                                 gain_row(ffn_post_norm, l))
        gl['ffn_post_norm'][l] = dgain[0]
        gl['ffn_post_w_gate'][l], gl['ffn_post_w_up'][l], gl['ffn_post_w_down'][l] = dwg, dwu, dwd
        if l % 2 == 0:
            e = l // 2
            dx, dxb, dgain, eg = _even_mixer_bwd(f"l{l}_mix", s_mix, dx, dxb, even_w[e], tables, gain_row(mix_norm, l))
            for n, val in _even_grads_unpad(eg).items():
                gl[n][e] = val
        else:
            o = l // 2
            dx, dxb, dgain, dw_in, dtaps, dw_out = _conv_mixer_bwd(f"l{l}_mix", s_mix, dx, dxb, full['conv_w_in'][o],
                                                                   taps[o], full['conv_w_out'][o],
                                                                   gain_row(mix_norm, l))
            gl['conv_w_in'][o], gl['conv_w'][o], gl['conv_w_out'][o] = dw_in, dtaps, dw_out
        gl['mix_norm'][l] = dgain[0]
        dx, dxb, dgain, dwg, dwu, dwd = _ffn_bwd(f"l{l}_pre", s_pre, dx, dxb, full['ffn_pre_w_gate'][l],
                                                 full['ffn_pre_w_up'][l], full['ffn_pre_w_down'][l],
                                                 gain_row(ffn_pre_norm, l))
        gl['ffn_pre_norm'][l] = dgain[0]
        gl['ffn_pre_w_gate'][l], gl['ffn_pre_w_up'][l], gl['ffn_pre_w_down'][l] = dwg, dwu, dwd
    grad_x = dx.reshape(x.shape)
    part = {n: jnp.stack(gl[n]) for n in gl if n not in FFN_WEIGHTS}
    part['final_norm'] = d_final[0]

    def row_blocked(n):
        g = part[n]
        L, r4, cols = g.shape
        return jnp.swapaxes(g.reshape(L, 4, r4 // 4, cols), 0, 1).reshape(4, L * (r4 // 4), cols).astype(_WIRE_DTYPE)

    pack_a = jnp.concatenate([gl[n][l] for n in GROUP_A for l in range(depth)], axis=1)
    pack_b = jnp.concatenate([gl[n][l] for n in GROUP_B if n in FFN_WEIGHTS for l in range(depth)]
                             + [row_blocked(n) for n in GROUP_B if n not in FFN_WEIGHTS], axis=1)
    pack_c = jnp.stack([_pad_rows(jnp.concatenate(
        [_shard_slice(part[n], SHARD_AXIS[n], b).astype(_WIRE_DTYPE).reshape(-1) for n in GROUP_C]), PACK_ROW_MULT)
        for b in range(4)])
    packs = [_pad_axis(p, 1, PACK_ROW_MULT) for p in (pack_a, pack_b, pack_c)]
    packs = [p.reshape(4, 2, p.shape[1] // 2, p.shape[2]) for p in packs]
    core = lax.axis_index("c").astype(jnp.int32).reshape(1)
    theirs = _pair_exchange_call(packs)
    pairs = [_pair_add_call(f"pair_add_{i}", p, t, core) for i, (p, t) in enumerate(zip(packs, theirs))]
    arrived = _chip_scatter_call(pairs)
    mine = [_sum_slots_call(f"sum_grad_slots_{i}", r, core) for i, r in enumerate(arrived)]
    red_a, red_b, red_c = [t.reshape(-1, t.shape[2]) for t in _sibling_share_call(mine)]
    grads = {}
    for group, red in ((GROUP_A, red_a), (GROUP_B, red_b)):
        row = 0
        for n in group:
            shp = w_loc[n].shape
            rows = shp[0] * shp[1]
            grads[n] = red[row:row + rows].reshape(shp)
            row += rows
    red_c = red_c.reshape(-1)
    off = 0
    for n in GROUP_C:
        shp = w_loc[n].shape
        size = int(np.prod(shp))
        grads[n] = red_c[off:off + size].reshape(shp)
        off += size

    small = _pad_rows(jnp.concatenate([part[n].reshape(-1) for n in REPLICATED]), 8)
    small_sum = _allreduce_small_call(small).reshape(-1)
    off = 0
    for n in REPLICATED:
        size = int(np.prod(w_loc[n].shape))
        grads[n] = small_sum[off:off + size].reshape(w_loc[n].shape)
        off += size

    deltas, new_m, new_v = {}, {}, {}
    for n in WEIGHTS:
        deltas[n], new_m[n], new_v[n] = _adamw_call("adamw_" + n, w_loc[n], grads[n], m_loc[n], v_loc[n])
    return (loss, grad_x, *[grads[n] for n in WEIGHTS], *[deltas[n] for n in WEIGHTS],
            *[new_m[n] for n in WEIGHTS], *[new_v[n] for n in WEIGHTS])
```

```python
import functools

import numpy as np
import jax
import jax.numpy as jnp
from jax import lax
from jax.experimental import pallas as pl
from jax.experimental.pallas import tpu as pltpu

F32 = jnp.float32
BF16 = jnp.bfloat16
_MXU_DTYPE = jnp.bfloat16
_WIRE_DTYPE = jnp.bfloat16
_VMEM_LIMIT = 52 * 1024 * 1024
_LANES = 128
_ATT_BLOCK = 512
_ROW_TILE = 512
_SG_TILE = 1024

NORM_EPS = 1e-6
HEADS = 8
NOPE = 64
ROPE = 32
VDIM = 64
QK_DIM = NOPE + ROPE
HP = 128
Q_LORA = 384
KV_LORA = 256
SG_WIDTH = 512
SG_GROUPS = 8
SG_GDIM = 64
SG_CHUNK = 128
ROPE_THETA = 10000.0
PROJ_W = Q_LORA + KV_LORA + HP + 2 * SG_WIDTH
ADAM_LR = 0.001
ADAM_B1 = 0.9
ADAM_B2 = 0.999
ADAM_EPS = 1e-08
ADAM_WD = 0.01
ADAM_STEP = 10
MESH = pl.DeviceIdType.MESH
PACK_COLS = 1024
PACK_ROW_MULT = 256

SHARDED = ['ffn_pre_w_gate', 'ffn_pre_w_up', 'ffn_pre_w_down', 'ffn_post_w_gate', 'ffn_post_w_up',
           'ffn_post_w_down', 'even_w_in', 'w_uq', 'w_ukv', 'even_w_out', 'conv_w_in', 'conv_w', 'conv_w_out']
SHARD_AXIS = {'ffn_pre_w_gate': 2, 'ffn_pre_w_up': 2, 'ffn_pre_w_down': 1, 'ffn_post_w_gate': 2,
              'ffn_post_w_up': 2, 'ffn_post_w_down': 1, 'even_w_in': 2, 'w_uq': 2, 'w_ukv': 2,
              'even_w_out': 1, 'conv_w_in': 2, 'conv_w': 2, 'conv_w_out': 1}
FFN_WEIGHTS = ['ffn_pre_w_gate', 'ffn_pre_w_up', 'ffn_pre_w_down', 'ffn_post_w_gate', 'ffn_post_w_up',
               'ffn_post_w_down']
GROUP_A = ['ffn_pre_w_gate', 'ffn_pre_w_up', 'ffn_post_w_gate', 'ffn_post_w_up']
GROUP_B = ['ffn_pre_w_down', 'ffn_post_w_down', 'even_w_out', 'conv_w_out']
GROUP_C = ['even_w_in', 'w_uq', 'w_ukv', 'conv_w_in', 'conv_w']
REPLICATED = ['ffn_pre_norm', 'mix_norm', 'ffn_post_norm', 'q_norm', 'kv_norm', 'sg_norm', 'sg_w', 'sg_b',
              'final_norm']
WEIGHTS = ['ffn_pre_norm', 'ffn_pre_w_gate', 'ffn_pre_w_up', 'ffn_pre_w_down', 'mix_norm', 'ffn_post_norm',
           'ffn_post_w_gate', 'ffn_post_w_up', 'ffn_post_w_down', 'even_w_in', 'q_norm', 'w_uq', 'kv_norm',
           'w_ukv', 'sg_norm', 'sg_w', 'sg_b', 'even_w_out', 'conv_w_in', 'conv_w', 'conv_w_out', 'final_norm']


def _params(sem=None):
    return pltpu.CompilerParams(vmem_limit_bytes=_VMEM_LIMIT,
                                **({} if sem is None else {'dimension_semantics': sem}))


def _pick(n, pref, mult):
    best = None
    t = mult
    while t <= min(n, pref):
        if n % t == 0:
            best = t
        t += mult
    return n if best is None else best


def _mx(v):
    return v if v.dtype == _MXU_DTYPE else v.astype(_MXU_DTYPE)


def _sigmoid(a):
    return 1.0 / (1.0 + jnp.exp(-a))


def _rms_stats(x):
    rstd = lax.rsqrt(jnp.mean(x * x, axis=-1, keepdims=True) + NORM_EPS)
    return x * rstd, rstd


def _rms_bwd(x, g, dh):
    xhat, rstd = _rms_stats(x)
    gdh = g * dh
    dx = rstd * (gdh - xhat * jnp.mean(gdh * xhat, axis=-1, keepdims=True))
    return dx, dh * xhat


def _fused_matmul(name, mode, lhs, rhs, prods, n_acc, epilogue, out_dtypes, M, N, K, tm, tn, tk,
                  tile_extras=(), row_extras=(), mrow_extras=(), n_colsum=0):
    gj, gi, gk = N // tn, M // tm, K // tk
    assert gj * tn == N and gi * tm == M and gk * tk == K, (name, M, N, K, tm, tn, tk)
    dims = {'nn': (((1,), (0,)), ((), ())), 'nt': (((1,), (1,)), ((), ())), 'tn': (((0,), (0,)), ((), ()))}[mode]

    def lhs_spec(roff, coff, kb):
        kb = tk if kb is None else kb
        if mode == 'tn':
            return pl.BlockSpec((kb, tm), lambda j, i, k: (k + roff, i + coff))
        return pl.BlockSpec((tm, kb), lambda j, i, k: (i + roff, k + coff))

    def rhs_spec(roff, coff, kb):
        kb = tk if kb is None else kb
        if mode == 'nt':
            return pl.BlockSpec((tn, kb), lambda j, i, k: (j + roff, k + coff))
        return pl.BlockSpec((kb, tn), lambda j, i, k: (k + roff, j + coff))

    in_specs = [lhs_spec(*a[1:]) for a in lhs] + [rhs_spec(*a[1:]) for a in rhs]
    in_specs += [pl.BlockSpec((tm, tn), lambda j, i, k: (i, j)) for _ in tile_extras]
    in_specs += [pl.BlockSpec((1, tn), lambda j, i, k: (0, j)) for _ in row_extras]
    in_specs += [pl.BlockSpec((tm, a.shape[1]), lambda j, i, k: (i, 0)) for a in mrow_extras]
    n_out = len(out_dtypes)
    out_shape = [jax.ShapeDtypeStruct((M, N), d) for d in out_dtypes]
    out_specs = [pl.BlockSpec((tm, tn), lambda j, i, k: (i, j)) for _ in out_dtypes]
    out_shape += [jax.ShapeDtypeStruct((1, N), F32) for _ in range(n_colsum)]
    out_specs += [pl.BlockSpec((1, tn), lambda j, i, k: (0, j)) for _ in range(n_colsum)]
    scratch = [pltpu.VMEM((tm, tn), F32) for _ in range(n_acc)] if gk > 1 else []
    nl, nr, nt, nrw, nm = len(lhs), len(rhs), len(tile_extras), len(row_extras), len(mrow_extras)

    def body(*refs):
        pos = 0
        lhs_refs = refs[pos:pos + nl]; pos += nl
        rhs_refs = refs[pos:pos + nr]; pos += nr
        tile_refs = refs[pos:pos + nt]; pos += nt
        row_refs = refs[pos:pos + nrw]; pos += nrw
        mrow_refs = refs[pos:pos + nm]; pos += nm
        out_refs = refs[pos:pos + n_out]; pos += n_out
        cs_refs = refs[pos:pos + n_colsum]; pos += n_colsum
        acc_refs = refs[pos:]
        i = pl.program_id(1)
        k = pl.program_id(2)

        def partials():
            res = [None] * n_acc
            for (li, ri, ai) in prods:
                d = lax.dot_general(_mx(lhs_refs[li][...]), _mx(rhs_refs[ri][...]), dims,
                                    preferred_element_type=F32)
                res[ai] = d if res[ai] is None else res[ai] + d
            return res

        def finish(accs):
            outs = epilogue(accs, [r[...] for r in tile_refs], [r[...] for r in row_refs],
                            [r[...] for r in mrow_refs])
            for r, o in zip(out_refs, outs[:n_out]):
                r[...] = o.astype(r.dtype)
            for r, c in zip(cs_refs, outs[n_out:]):
                c = jnp.sum(c, axis=0, keepdims=True)

                @pl.when(i == 0)
                def _():
                    r[...] = c

                @pl.when(i != 0)
                def _():
                    r[...] += c

        if gk == 1:
            finish(partials())
        else:
            p = partials()

            @pl.when(k == 0)
            def _():
                for r, v in zip(acc_refs, p):
                    r[...] = v

            @pl.when(k != 0)
            def _():
                for r, v in zip(acc_refs, p):
                    r[...] += v

            @pl.when(k == gk - 1)
            def _():
                finish([r[...] for r in acc_refs])

    res = pl.pallas_call(
        body, name=name, grid=(gj, gi, gk), in_specs=in_specs, out_specs=out_specs, out_shape=out_shape,
        scratch_shapes=scratch, compiler_params=_params(("arbitrary", "arbitrary", "arbitrary")),
    )(*[a[0] for a in lhs], *[a[0] for a in rhs], *tile_extras, *row_extras, *mrow_extras)
    return res


def _op(a, roff=0, coff=0, kb=None):
    return (a, roff, coff, kb)


def _ident_epi(scale=None):
    def epi(accs, tiles, rows, mrows):
        return [a if scale is None else a * scale for a in accs]
    return epi


def _resid_norm_epi(scale):
    def epi(accs, tiles, rows, mrows):
        x_new = tiles[0] + scale * accs[0]
        xhat, _ = _rms_stats(x_new)
        return [x_new, xhat * rows[0]]
    return epi


def _norm_bwd_epi(accs, tiles, rows, mrows):
    dx_n, dg = _rms_bwd(tiles[0], rows[0], accs[0])
    dx = tiles[1] + dx_n
    return [dx, dx, dg]


def _rmsnorm_call(name, x, g):
    S, D = x.shape
    tm = _pick(S, _ROW_TILE, 8)

    def body(x_ref, g_ref, h_ref):
        xhat, _ = _rms_stats(x_ref[...])
        h_ref[...] = (xhat * g_ref[...]).astype(h_ref.dtype)

    return pl.pallas_call(
        body, name=name, grid=(S // tm,),
        in_specs=[pl.BlockSpec((tm, D), lambda i: (i, 0)), pl.BlockSpec((1, D), lambda i: (0, 0))],
        out_specs=pl.BlockSpec((tm, D), lambda i: (i, 0)),
        out_shape=jax.ShapeDtypeStruct((S, D), BF16), compiler_params=_params(("arbitrary",)),
    )(x, g)


def _loss_call(x, target, g):
    S, D = x.shape
    tm = _pick(S, _ROW_TILE, 8)

    def body(x_ref, t_ref, g_ref, dx_ref, dxb_ref, dg_ref, loss_ref):
        i = pl.program_id(0)
        x_t = x_ref[...]
        gain = g_ref[...]
        xhat, _ = _rms_stats(x_t)
        diff = xhat * gain - t_ref[...]
        dy = diff * (1.0 / D)
        dx, dg = _rms_bwd(x_t, gain, dy)
        dx_ref[...] = dx
        dxb_ref[...] = dx.astype(BF16)
        dg = jnp.sum(dg, axis=0, keepdims=True)
        part = 0.5 * jnp.sum(jnp.sum(diff * diff, axis=1, keepdims=True), axis=0, keepdims=True) * (1.0 / D)
        part = jnp.broadcast_to(part, (1, _LANES))

        @pl.when(i == 0)
        def _():
            dg_ref[...] = dg
            loss_ref[...] = part

        @pl.when(i != 0)
        def _():
            dg_ref[...] += dg
            loss_ref[...] += part

    row = lambda i: (i, 0)
    fixed = lambda i: (0, 0)
    return pl.pallas_call(
        body, name="loss_head", grid=(S // tm,),
        in_specs=[pl.BlockSpec((tm, D), row), pl.BlockSpec((tm, D), row), pl.BlockSpec((1, D), fixed)],
        out_specs=[pl.BlockSpec((tm, D), row), pl.BlockSpec((tm, D), row), pl.BlockSpec((1, D), fixed),
                   pl.BlockSpec((1, _LANES), fixed)],
        out_shape=[jax.ShapeDtypeStruct((S, D), F32), jax.ShapeDtypeStruct((S, D), BF16),
                   jax.ShapeDtypeStruct((1, D), F32), jax.ShapeDtypeStruct((1, _LANES), F32)],
        compiler_params=_params(("arbitrary",)),
    )(x, target, g)


def _rope_tables_call(pos_col, invf, mask_a, mask_b):
    S = pos_col.shape[0]
    tm = _pick(S, _ROW_TILE, 8)

    def body(p_ref, f_ref, a_ref, b_ref, cos_ref, sa_ref, sb_ref):
        ang = p_ref[...].astype(F32) * f_ref[...]
        sn = jnp.sin(ang)
        cos_ref[...] = jnp.cos(ang)
        sa_ref[...] = sn * a_ref[...]
        sb_ref[...] = sn * b_ref[...]

    row = lambda i: (i, 0)
    fixed = lambda i: (0, 0)
    return pl.pallas_call(
        body, name="rope_tables", grid=(S // tm,),
        in_specs=[pl.BlockSpec((tm, 1), row)] + [pl.BlockSpec((1, HP), fixed)] * 3,
        out_specs=[pl.BlockSpec((tm, HP), row)] * 3,
        out_shape=[jax.ShapeDtypeStruct((S, HP), F32)] * 3, compiler_params=_params(("arbitrary",)),
    )(pos_col, invf, mask_a, mask_b)


def _rope(t, cos, sa, sb):
    return t * cos + pltpu.roll(t, HP - ROPE // 2, 1) * sa + pltpu.roll(t, ROPE // 2, 1) * sb


def _rope_t(d, cos, sa, sb):
    return d * cos + pltpu.roll(d * sa, ROPE // 2, 1) + pltpu.roll(d * sb, HP - ROPE // 2, 1)


def _gelu(z):
    return 0.5 * z * (1.0 + lax.erf(z * np.float32(1.0 / np.sqrt(2.0))))


def _gelu_grad(z):
    cdf = 0.5 * (1.0 + lax.erf(z * np.float32(1.0 / np.sqrt(2.0))))
    pdf = np.float32(1.0 / np.sqrt(2.0 * np.pi)) * jnp.exp(-0.5 * z * z)
    return cdf + z * pdf


_CQ0, _CKV0, _KR0, _Z0 = 0, Q_LORA, Q_LORA + KV_LORA, Q_LORA + KV_LORA + HP


def _even_prep_call(proj, qn, kvn, sgn, cos, sa, sb):
    S = proj.shape[0]
    tm = _pick(S, 256, 8)

    def body(p_ref, qn_ref, kvn_ref, sgn_ref, cos_ref, sa_ref, sb_ref, cq_ref, ckv_ref, kr_ref, u_ref, v_ref):
        cq = p_ref[:, _CQ0:_CQ0 + Q_LORA]
        cq_ref[...] = (_rms_stats(cq)[0] * qn_ref[...]).astype(BF16)
        ckv = p_ref[:, _CKV0:_CKV0 + KV_LORA]
        ckv_ref[...] = (_rms_stats(ckv)[0] * kvn_ref[...]).astype(BF16)
        kr = p_ref[:, _KR0:_KR0 + HP]
        kr_ref[...] = _rope(kr, cos_ref[...], sa_ref[...], sb_ref[...]).astype(BF16)
        u_ref[...] = _gelu(p_ref[:, _Z0:_Z0 + SG_WIDTH]).astype(BF16)
        zv = _gelu(p_ref[:, _Z0 + SG_WIDTH:_Z0 + 2 * SG_WIDTH])
        v_ref[...] = (_rms_stats(zv)[0] * sgn_ref[...]).astype(BF16)

    row = lambda i: (i, 0)
    fixed = lambda i: (0, 0)
    widths = [Q_LORA, KV_LORA, HP, SG_WIDTH, SG_WIDTH]
    return pl.pallas_call(
        body, name="even_prep", grid=(S // tm,),
        in_specs=[pl.BlockSpec((tm, PROJ_W), row), pl.BlockSpec((1, Q_LORA), fixed),
                  pl.BlockSpec((1, KV_LORA), fixed), pl.BlockSpec((1, SG_WIDTH), fixed)]
        + [pl.BlockSpec((tm, HP), row)] * 3,
        out_specs=[pl.BlockSpec((tm, w), row) for w in widths],
        out_shape=[jax.ShapeDtypeStruct((S, w), BF16) for w in widths],
        compiler_params=_params(("arbitrary",)),
    )(proj, qn, kvn, sgn, cos, sa, sb)


def _even_prep_bwd_call(proj, qn, kvn, sgn, cos, sa, sb, dcqn, dckvn, dk, du, dvn):
    S = proj.shape[0]
    tm = _pick(S, 256, 8)

    def body(p_ref, qn_ref, kvn_ref, sgn_ref, cos_ref, sa_ref, sb_ref, dcq_ref, dckv_ref, dk_ref, du_ref,
             dvn_ref, dp_ref, dqn_ref, dkvn_ref, dsgn_ref):
        i = pl.program_id(0)
        dcq, gq = _rms_bwd(p_ref[:, _CQ0:_CQ0 + Q_LORA], qn_ref[...], dcq_ref[...])
        dp_ref[:, _CQ0:_CQ0 + Q_LORA] = dcq.astype(BF16)
        dckv, gkv = _rms_bwd(p_ref[:, _CKV0:_CKV0 + KV_LORA], kvn_ref[...], dckv_ref[...])
        dp_ref[:, _CKV0:_CKV0 + KV_LORA] = dckv.astype(BF16)
        dkr = dk_ref[:, 0:HP].astype(F32)
        for h in range(1, HEADS):
            dkr = dkr + dk_ref[:, h * HP:(h + 1) * HP].astype(F32)
        lane = lax.broadcasted_iota(jnp.int32, dkr.shape, 1)
        dkr = jnp.where((lane >= NOPE) & (lane < QK_DIM), dkr, 0.0)
        dp_ref[:, _KR0:_KR0 + HP] = _rope_t(dkr, cos_ref[...], sa_ref[...], sb_ref[...]).astype(BF16)
        zu = p_ref[:, _Z0:_Z0 + SG_WIDTH]
        dp_ref[:, _Z0:_Z0 + SG_WIDTH] = (du_ref[...].astype(F32) * _gelu_grad(zu)).astype(BF16)
        zv = p_ref[:, _Z0 + SG_WIDTH:_Z0 + 2 * SG_WIDTH]
        dgv, gsg = _rms_bwd(_gelu(zv), sgn_ref[...], dvn_ref[...].astype(F32))
        dp_ref[:, _Z0 + SG_WIDTH:_Z0 + 2 * SG_WIDTH] = (dgv * _gelu_grad(zv)).astype(BF16)
        sums = [jnp.sum(t, axis=0, keepdims=True) for t in (gq, gkv, gsg)]

        @pl.when(i == 0)
        def _():
            for r, s in zip((dqn_ref, dkvn_ref, dsgn_ref), sums):
                r[...] = s

        @pl.when(i != 0)
        def _():
            for r, s in zip((dqn_ref, dkvn_ref, dsgn_ref), sums):
                r[...] += s

    row = lambda i: (i, 0)
    fixed = lambda i: (0, 0)
    return pl.pallas_call(
        body, name="even_prep_bwd", grid=(S // tm,),
        in_specs=[pl.BlockSpec((tm, PROJ_W), row), pl.BlockSpec((1, Q_LORA), fixed),
                  pl.BlockSpec((1, KV_LORA), fixed), pl.BlockSpec((1, SG_WIDTH), fixed)]
        + [pl.BlockSpec((tm, HP), row)] * 3
        + [pl.BlockSpec((tm, Q_LORA), row), pl.BlockSpec((tm, KV_LORA), row),
           pl.BlockSpec((tm, HEADS * HP), row), pl.BlockSpec((tm, SG_WIDTH), row),
           pl.BlockSpec((tm, SG_WIDTH), row)],
        out_specs=[pl.BlockSpec((tm, PROJ_W), row), pl.BlockSpec((1, Q_LORA), fixed),
                   pl.BlockSpec((1, KV_LORA), fixed), pl.BlockSpec((1, SG_WIDTH), fixed)],
        out_shape=[jax.ShapeDtypeStruct((S, PROJ_W), BF16), jax.ShapeDtypeStruct((1, Q_LORA), F32),
                   jax.ShapeDtypeStruct((1, KV_LORA), F32), jax.ShapeDtypeStruct((1, SG_WIDTH), F32)],
        compiler_params=_params(("arbitrary",)),
    )(proj, qn, kvn, sgn, cos, sa, sb, dcqn, dckvn, dk, du, dvn)


def _causal_mask(rows, cols):
    r = lax.broadcasted_iota(jnp.int32, (rows, cols), 0)
    c = lax.broadcasted_iota(jnp.int32, (rows, cols), 1)
    return c <= r


def _flash_fwd_call(q, k, v):
    S = q.shape[0]
    tb = _pick(S, _ATT_BLOCK, 128)
    nq = S // tb
    nt_dims = (((1,), (1,)), ((), ()))

    def body(q_ref, k_ref, v_ref, o_ref, lse_ref, s_a, s_b, m_ref, acc_ref):
        i = pl.program_id(1)

        def scores(buf, j):
            k_t = k_ref[pl.ds(pl.multiple_of(j * tb, tb), tb), :]
            buf[...] = lax.dot_general(q_ref[...], k_t, nt_dims, preferred_element_type=F32)

        def update(buf, j, masked):
            v_t = v_ref[pl.ds(pl.multiple_of(j * tb, tb), tb), :]
            s = buf[...]
            if masked:
                s = jnp.where(_causal_mask(tb, tb), s, -1e30)
            m = m_ref[...]
            m_new = jnp.maximum(m, jnp.max(s, axis=1, keepdims=True))
            alpha = jnp.exp(m - m_new)
            p = jnp.exp(s - m_new)
            acc_ref[...] = alpha * acc_ref[...] + jnp.dot(p.astype(v_t.dtype), v_t, preferred_element_type=F32)
            m_ref[...] = m_new

        m_ref[...] = jnp.full((tb, 1), -1e30, F32)
        acc_ref[...] = jnp.zeros((tb, HP), F32)
        scores(s_a, 0)
        pairs = i // 2

        def two_blocks(t, carry):
            scores(s_b, 2 * t + 1)
            update(s_a, 2 * t, False)
            scores(s_a, 2 * t + 2)
            update(s_b, 2 * t + 1, False)
            return carry

        lax.fori_loop(0, pairs, two_blocks, 0)

        @pl.when(2 * pairs == i)
        def _():
            update(s_a, i, True)

        @pl.when(2 * pairs != i)
        def _():
            scores(s_b, i)
            update(s_a, i - 1, False)
            update(s_b, i, True)

        acc = acc_ref[...]
        l = acc[:, VDIM:VDIM + 1]
        lane = lax.broadcasted_iota(jnp.int32, (tb, HP), 1)
        o_ref[...] = jnp.where(lane < VDIM, acc / l, 0.0).astype(o_ref.dtype)
        lse = jnp.broadcast_to(m_ref[...] + jnp.log(l), (tb, HP))
        lse_ref[0, 0] = jnp.transpose(lse)[0:8, :]

    return pl.pallas_call(
        body, name="flash_fwd", grid=(HEADS, nq),
        in_specs=[pl.BlockSpec((tb, HP), lambda h, i: (i, h)), pl.BlockSpec((S, HP), lambda h, i: (0, h)),
                  pl.BlockSpec((S, HP), lambda h, i: (0, h))],
        out_specs=[pl.BlockSpec((tb, HP), lambda h, i: (i, h)),
                   pl.BlockSpec((1, 1, 8, tb), lambda h, i: (h, i, 0, 0))],
        out_shape=[jax.ShapeDtypeStruct((S, HEADS * HP), q.dtype), jax.ShapeDtypeStruct((HEADS, nq, 8, tb), F32)],
        scratch_shapes=[pltpu.VMEM((tb, tb), F32), pltpu.VMEM((tb, tb), F32), pltpu.VMEM((tb, 1), F32),
                        pltpu.VMEM((tb, HP), F32)],
        compiler_params=_params(("arbitrary", "arbitrary")),
    )(q, k, v)


def _attn_delta_call(o, do):
    S = o.shape[0]
    tb = _pick(S, _ATT_BLOCK, 128)
    nq = S // tb
    nb = _pick(nq, 4, 1)

    def body(o_ref, do_ref, d_ref):
        for r in range(nb):
            rows = slice(r * tb, (r + 1) * tb)
            d = jnp.sum(o_ref[rows, :].astype(F32) * do_ref[rows, :].astype(F32), axis=1, keepdims=True)
            d_ref[0, r] = jnp.transpose(jnp.broadcast_to(d, (tb, HP)))[0:8, :]

    return pl.pallas_call(
        body, name="attn_delta", grid=(HEADS, nq // nb),
        in_specs=[pl.BlockSpec((nb * tb, HP), lambda h, i: (i, h))] * 2,
        out_specs=pl.BlockSpec((1, nb, 8, tb), lambda h, i: (h, i, 0, 0)),
        out_shape=jax.ShapeDtypeStruct((HEADS, nq, 8, tb), F32), compiler_params=_params(("arbitrary", "arbitrary")),
    )(o, do)


def _flash_bwd_call(q, k, v, do, lse, delta):
    S = q.shape[0]
    tb = _pick(S, _ATT_BLOCK, 128)
    nq = S // tb
    nt_dims = (((1,), (1,)), ((), ()))
    tn_dims = (((0,), (0,)), ((), ()))

    def body(q_ref, do_ref, lse_ref, dl_ref, k_ref, v_ref, dq_ref, dk_ref, dv_ref, st_a, dp_a, st_b, dp_b, dk_acc,
             dv_acc):
        j = pl.program_id(1)

        @pl.when(j == 0)
        def _():
            dq_ref[...] = jnp.zeros_like(dq_ref)

        def rows_of(i):
            return pl.ds(pl.multiple_of(i * tb, tb), tb)

        def scores(st_buf, dp_buf, i):
            st_buf[...] = lax.dot_general(k_ref[...], q_ref[rows_of(i), :], nt_dims, preferred_element_type=F32)
            dp_buf[...] = lax.dot_general(v_ref[...], do_ref[rows_of(i), :], nt_dims, preferred_element_type=F32)

        def update(st_buf, dp_buf, i, masked):
            q_t = q_ref[rows_of(i), :]
            do_t = do_ref[rows_of(i), :]
            pt = jnp.exp(st_buf[...] - lse_ref[0, i, 0:1, :])
            if masked:
                pt = jnp.where(jnp.transpose(_causal_mask(tb, tb)), pt, 0.0)
            dst = (pt * (dp_buf[...] - dl_ref[0, i, 0:1, :])).astype(q_t.dtype)
            dv_acc[...] += jnp.dot(pt.astype(do_t.dtype), do_t, preferred_element_type=F32)
            dk_acc[...] += jnp.dot(dst, q_t, preferred_element_type=F32)
            dq_ref[rows_of(i), :] += lax.dot_general(dst, k_ref[...], tn_dims, preferred_element_type=F32)

        last = nq - 1
        dk_acc[...] = jnp.zeros((tb, HP), F32)
        dv_acc[...] = jnp.zeros((tb, HP), F32)
        scores(st_b, dp_b, j)
        scores(st_a, dp_a, jnp.minimum(j + 1, last))
        update(st_b, dp_b, j, True)
        rest = last - j
        pairs = rest // 2

        def two_blocks(t, carry):
            i0 = j + 1 + 2 * t
            scores(st_b, dp_b, i0 + 1)
            update(st_a, dp_a, i0, False)
            scores(st_a, dp_a, jnp.minimum(i0 + 2, last))
            update(st_b, dp_b, i0 + 1, False)
            return carry

        lax.fori_loop(0, pairs, two_blocks, 0)

        @pl.when(2 * pairs != rest)
        def _():
            update(st_a, dp_a, last, False)

        dk_ref[...] = dk_acc[...].astype(dk_ref.dtype)
        dv_ref[...] = dv_acc[...].astype(dv_ref.dtype)

    head = lambda h, j: (0, h)
    blk = lambda h, j: (j, h)
    rows = lambda h, j: (h, 0, 0, 0)
    return pl.pallas_call(
        body, name="flash_bwd", grid=(HEADS, nq),
        in_specs=[pl.BlockSpec((S, HP), head), pl.BlockSpec((S, HP), head), pl.BlockSpec((1, nq, 8, tb), rows),
                  pl.BlockSpec((1, nq, 8, tb), rows), pl.BlockSpec((tb, HP), blk), pl.BlockSpec((tb, HP), blk)],
        out_specs=[pl.BlockSpec((S, HP), head), pl.BlockSpec((tb, HP), blk), pl.BlockSpec((tb, HP), blk)],
        out_shape=[jax.ShapeDtypeStruct((S, HEADS * HP), F32), jax.ShapeDtypeStruct((S, HEADS * HP), BF16),
                   jax.ShapeDtypeStruct((S, HEADS * HP), BF16)],
        scratch_shapes=[pltpu.VMEM((tb, tb), F32)] * 4 + [pltpu.VMEM((tb, HP), F32)] * 2,
        compiler_params=_params(("arbitrary", "arbitrary")),
    )(q, do, lse, delta, k, v)


def _sg_mixed(w_ref, vch, lane_lo):
    blocks = []
    for jb in range(SG_WIDTH // _LANES):
        r = jnp.dot(w_ref[jb], vch[:, jb * _LANES:(jb + 1) * _LANES], preferred_element_type=F32)
        blocks.append(jnp.where(lane_lo, r[0:SG_CHUNK], r[SG_CHUNK:2 * SG_CHUNK]))
    return jnp.concatenate(blocks, axis=1)


def _sgu_fwd_call(vn, u, attn, wst, bexp):
    S = vn.shape[0]
    tm = _pick(S, _SG_TILE, SG_CHUNK)
    AW = HEADS * HP

    def body(v_ref, u_ref, a_ref, w_ref, b_ref, mix_ref):
        lane_lo = lax.broadcasted_iota(jnp.int32, (SG_CHUNK, _LANES), 1) < SG_GDIM
        mix_ref[:, 0:AW] = a_ref[...]
        for c in range(tm // SG_CHUNK):
            rs = slice(c * SG_CHUNK, (c + 1) * SG_CHUNK)
            mixed = _sg_mixed(w_ref, v_ref[rs, :], lane_lo) + b_ref[...]
            mix_ref[rs, AW:AW + SG_WIDTH] = (u_ref[rs, :].astype(F32) * mixed).astype(mix_ref.dtype)

    row = lambda i: (i, 0)
    return pl.pallas_call(
        body, name="sgu_fwd", grid=(S // tm,),
        in_specs=[pl.BlockSpec((tm, SG_WIDTH), row), pl.BlockSpec((tm, SG_WIDTH), row), pl.BlockSpec((tm, AW), row),
                  pl.BlockSpec((SG_WIDTH // _LANES, 2 * SG_CHUNK, SG_CHUNK), lambda i: (0, 0, 0)),
                  pl.BlockSpec((SG_CHUNK, SG_WIDTH), lambda i: (0, 0))],
        out_specs=pl.BlockSpec((tm, AW + SG_WIDTH), row),
        out_shape=jax.ShapeDtypeStruct((S, AW + SG_WIDTH), BF16), compiler_params=_params(("arbitrary",)),
    )(vn, u, attn, wst, bexp)


def _sgu_bwd_call(dmix, vn, u, wst, wst_t, bexp):
    S = vn.shape[0]
    tm = _pick(S, _SG_TILE, SG_CHUNK)
    nblk = SG_WIDTH // _LANES
    col0 = (HEADS * HP) // SG_WIDTH
    nt_dims = (((1,), (1,)), ((), ()))

    def body(d_ref, v_ref, u_ref, w_ref, wt_ref, b_ref, du_ref, dv_ref, dw_ref, db_ref, dbacc_ref):
        i = pl.program_id(0)
        lane_lo = lax.broadcasted_iota(jnp.int32, (SG_CHUNK, _LANES), 1) < SG_GDIM

        @pl.when(i == 0)
        def _():
            dw_ref[...] = jnp.zeros_like(dw_ref)
            dbacc_ref[...] = jnp.zeros_like(dbacc_ref)

        for c in range(tm // SG_CHUNK):
            rs = slice(c * SG_CHUNK, (c + 1) * SG_CHUNK)
            vch = v_ref[rs, :]
            dsg = d_ref[rs, :].astype(F32)
            mixed = _sg_mixed(w_ref, vch, lane_lo) + b_ref[...]
            du_ref[rs, :] = (dsg * mixed).astype(du_ref.dtype)
            dmixed = dsg * u_ref[rs, :].astype(F32)
            dbacc_ref[...] += dmixed
            dmx = dmixed.astype(vch.dtype)
            dv_ref[rs, :] = _sg_mixed(wt_ref, dmx, lane_lo).astype(dv_ref.dtype)
            for jb in range(nblk):
                dblk = dmx[:, jb * _LANES:(jb + 1) * _LANES]
                vblk = vch[:, jb * _LANES:(jb + 1) * _LANES]
                zero = jnp.zeros_like(dblk)
                dw_ref[2 * jb] += lax.dot_general(jnp.where(lane_lo, dblk, zero), vblk, nt_dims,
                                                  preferred_element_type=F32)
                dw_ref[2 * jb + 1] += lax.dot_general(jnp.where(lane_lo, zero, dblk), vblk, nt_dims,
                                                      preferred_element_type=F32)

        @pl.when(i == pl.num_programs(0) - 1)
        def _():
            tri = _causal_mask(SG_CHUNK, SG_CHUNK)
            for g in range(SG_GROUPS):
                dw_ref[g] = jnp.where(tri, dw_ref[g], 0.0)
            lane = lax.broadcasted_iota(jnp.int32, (SG_CHUNK, _LANES), 1)
            out = jnp.zeros((SG_CHUNK, _LANES), F32)
            for g in range(SG_GROUPS):
                blk = dbacc_ref[:, (g // 2) * _LANES:(g // 2 + 1) * _LANES]
                sel = lane_lo if g % 2 == 0 else jnp.logical_not(lane_lo)
                s = jnp.sum(jnp.where(sel, blk, 0.0), axis=1, keepdims=True)
                out = jnp.where(lane == g, s, out)
            db_ref[...] = out

    row = lambda i: (i, 0)
    wspec = pl.BlockSpec((nblk, 2 * SG_CHUNK, SG_CHUNK), lambda i: (0, 0, 0))
    return pl.pallas_call(
        body, name="sgu_bwd", grid=(S // tm,),
        in_specs=[pl.BlockSpec((tm, SG_WIDTH), lambda i: (i, col0)), pl.BlockSpec((tm, SG_WIDTH), row),
                  pl.BlockSpec((tm, SG_WIDTH), row), wspec, wspec,
                  pl.BlockSpec((SG_CHUNK, SG_WIDTH), lambda i: (0, 0))],
        out_specs=[pl.BlockSpec((tm, SG_WIDTH), row), pl.BlockSpec((tm, SG_WIDTH), row),
                   pl.BlockSpec((SG_GROUPS, SG_CHUNK, SG_CHUNK), lambda i: (0, 0, 0)),
                   pl.BlockSpec((SG_CHUNK, _LANES), lambda i: (0, 0))],
        out_shape=[jax.ShapeDtypeStruct((S, SG_WIDTH), BF16), jax.ShapeDtypeStruct((S, SG_WIDTH), BF16),
                   jax.ShapeDtypeStruct((SG_GROUPS, SG_CHUNK, SG_CHUNK), F32),
                   jax.ShapeDtypeStruct((SG_CHUNK, _LANES), F32)],
        scratch_shapes=[pltpu.VMEM((SG_CHUNK, SG_WIDTH), F32)],
        compiler_params=_params(("arbitrary",)),
    )(dmix, vn, u, wst, wst_t, bexp)


def _shift_down(t, halo, n):
    rows = lax.broadcasted_iota(jnp.int32, t.shape, 0)
    out = pltpu.roll(t, n, 0)
    for r in range(n):
        out = jnp.where(rows == r, halo[8 - n + r:8 - n + r + 1, :], out)
    return out


def _shift_up(t, halo, n):
    tm = t.shape[0]
    rows = lax.broadcasted_iota(jnp.int32, t.shape, 0)
    out = pltpu.roll(t, tm - n, 0)
    for r in range(n):
        out = jnp.where(rows == tm - n + r, halo[r:r + 1, :], out)
    return out


def _conv_fwd_call(p, w):
    S, C3 = p.shape
    C = C3 // 3
    tm = _pick(S, _ROW_TILE, 8)
    hb = tm // 8

    def body(p_ref, c_prev, z_prev, w_ref, m_ref):
        i = pl.program_id(0)
        cz = p_ref[:, C:2 * C] * p_ref[:, 2 * C:3 * C]
        czp = jnp.where(i > 0, c_prev[...] * z_prev[...], 0.0)
        y = w_ref[2:3, :] * cz + w_ref[1:2, :] * _shift_down(cz, czp, 1) + w_ref[0:1, :] * _shift_down(cz, czp, 2)
        m_ref[...] = (p_ref[:, 0:C] * y).astype(m_ref.dtype)

    prev = lambda col: (lambda i: (jnp.maximum(i * hb - 1, 0), col))
    return pl.pallas_call(
        body, name="conv_fwd", grid=(S // tm,),
        in_specs=[pl.BlockSpec((tm, C3), lambda i: (i, 0)), pl.BlockSpec((8, C), prev(1)),
                  pl.BlockSpec((8, C), prev(2)), pl.BlockSpec((3, C), lambda i: (0, 0))],
        out_specs=pl.BlockSpec((tm, C), lambda i: (i, 0)),
        out_shape=jax.ShapeDtypeStruct((S, C), BF16), compiler_params=_params(("arbitrary",)),
    )(p, p, p, w)


def _conv_bwd_call(p, w, dm):
    S, C3 = p.shape
    C = C3 // 3
    tm = _pick(S, 256, 8)
    hb = tm // 8
    n_tiles = S // tm

    def body(p_ref, c_prev, z_prev, b_next, dm_ref, dm_next, w_ref, dp_ref, dw_ref):
        i = pl.program_id(0)
        b = p_ref[:, 0:C]
        c = p_ref[:, C:2 * C]
        z = p_ref[:, 2 * C:3 * C]
        cz = c * z
        czp = jnp.where(i > 0, c_prev[...] * z_prev[...], 0.0)
        s1 = _shift_down(cz, czp, 1)
        s2 = _shift_down(cz, czp, 2)
        w0, w1, w2 = w_ref[0:1, :], w_ref[1:2, :], w_ref[2:3, :]
        y = w2 * cz + w1 * s1 + w0 * s2
        dm_t = dm_ref[...]
        dy = dm_t * b
        dyn = jnp.where(i < n_tiles - 1, dm_next[...] * b_next[...], 0.0)
        dcz = w2 * dy + w1 * _shift_up(dy, dyn, 1) + w0 * _shift_up(dy, dyn, 2)
        dp_ref[:, 0:C] = (dm_t * y).astype(dp_ref.dtype)
        dp_ref[:, C:2 * C] = (dcz * z).astype(dp_ref.dtype)
        dp_ref[:, 2 * C:3 * C] = (dcz * c).astype(dp_ref.dtype)
        dw = jnp.concatenate([jnp.sum(dy * s2, axis=0, keepdims=True), jnp.sum(dy * s1, axis=0, keepdims=True),
                              jnp.sum(dy * cz, axis=0, keepdims=True)], axis=0)

        @pl.when(i == 0)
        def _():
            dw_ref[...] = dw

        @pl.when(i != 0)
        def _():
            dw_ref[...] += dw

    prev = lambda col: (lambda i: (jnp.maximum(i * hb - 1, 0), col))
    nxt = lambda col: (lambda i: (jnp.minimum((i + 1) * hb, S // 8 - 1), col))
    return pl.pallas_call(
        body, name="conv_bwd", grid=(n_tiles,),
        in_specs=[pl.BlockSpec((tm, C3), lambda i: (i, 0)), pl.BlockSpec((8, C), prev(1)),
                  pl.BlockSpec((8, C), prev(2)), pl.BlockSpec((8, C), nxt(0)),
                  pl.BlockSpec((tm, C), lambda i: (i, 0)), pl.BlockSpec((8, C), nxt(0)),
                  pl.BlockSpec((3, C), lambda i: (0, 0))],
        out_specs=[pl.BlockSpec((tm, C3), lambda i: (i, 0)), pl.BlockSpec((3, C), lambda i: (0, 0))],
        out_shape=[jax.ShapeDtypeStruct((S, C3), BF16), jax.ShapeDtypeStruct((3, C), F32)],
        compiler_params=_params(("arbitrary",)),
    )(p, p, p, p, dm, dm, w)


def _my_place():
    return lax.axis_index("x"), lax.axis_index("y"), lax.axis_index("c")


def _gather_weights_call(name, shard):
    R, C = shard.shape

    def body(s_ref, o_ref, send_sems, recv_sems, local_sem):
        x, y, c = _my_place()
        mine = 2 * x + y
        local = pltpu.make_async_copy(s_ref, o_ref.at[mine], local_sem)
        local.start()
        peers = [(1 - x, y), (x, 1 - y), (1 - x, 1 - y)]
        copies = []
        for k, (px, py) in enumerate(peers):
            cp = pltpu.make_async_remote_copy(src_ref=s_ref, dst_ref=o_ref.at[mine], send_sem=send_sems.at[k],
                                              recv_sem=recv_sems.at[k], device_id=(px, py, c), device_id_type=MESH)
            cp.start()
            copies.append(cp)
        for k, (px, py) in enumerate(peers):
            pltpu.make_async_remote_copy(src_ref=s_ref, dst_ref=o_ref.at[2 * px + py], send_sem=send_sems.at[k],
                                         recv_sem=recv_sems.at[k], device_id=(px, py, c),
                                         device_id_type=MESH).wait_recv()
        for cp in copies:
            cp.wait_send()
        local.wait()

    any_spec = pl.BlockSpec(memory_space=pl.ANY)
    return pl.pallas_call(
        body, name=name, in_specs=[any_spec], out_specs=any_spec,
        out_shape=jax.ShapeDtypeStruct((4, R, C), shard.dtype),
        scratch_shapes=[pltpu.SemaphoreType.DMA((3,)), pltpu.SemaphoreType.DMA((3,)), pltpu.SemaphoreType.DMA],
        compiler_params=pltpu.CompilerParams(has_side_effects=True),
    )(shard)


_D2D_CHUNKS = 4


_LOCAL_CHUNKS = 8


def _local_copies(src_of, dst_of, rows, sems, base):
    rc = rows // _LOCAL_CHUNKS
    assert rc * _LOCAL_CHUNKS == rows and rc % 16 == 0, rows
    out = []
    for j in range(_LOCAL_CHUNKS):
        sl = pl.ds(j * rc, rc)
        out.append(pltpu.make_async_copy(src_of(sl), dst_of(sl), sems.at[base + j]))
    return out


def _comm_call(name, body, arrays, out_shapes, sem_counts, aliases=None):
    any_spec = pl.BlockSpec(memory_space=pl.ANY)
    return pl.pallas_call(
        body, name=name, in_specs=[any_spec] * len(arrays), out_specs=[any_spec] * len(out_shapes),
        out_shape=out_shapes, scratch_shapes=[pltpu.SemaphoreType.DMA((n,)) for n in sem_counts],
        input_output_aliases=aliases or {}, compiler_params=pltpu.CompilerParams(has_side_effects=True),
    )(*arrays)


def _gather_halves_call(shards):
    na = len(shards)
    for s in shards:
        assert s.shape[0] % (2 * _D2D_CHUNKS * 16) == 0, s.shape

    def body(*refs):
        s_refs, o_refs = refs[:na], refs[na:2 * na]
        ici_send, ici_recv, d2d_send, d2d_recv = refs[2 * na:]
        x, y, c = _my_place()
        mine = 2 * x + y
        chips = [(1 - x, y), (x, 1 - y), (1 - x, 1 - y)]

        def ici(a, k, chip, block):
            Rh = s_refs[a].shape[1] // 2
            my_half = pl.ds(pl.multiple_of(c * Rh, 16), Rh)
            return pltpu.make_async_remote_copy(src_ref=s_refs[a].at[mine, my_half],
                                                dst_ref=o_refs[a].at[block, my_half],
                                                send_sem=ici_send.at[3 * a + k], recv_sem=ici_recv.at[3 * a + k],
                                                device_id=(chip[0], chip[1], c), device_id_type=MESH)

        def d2d(a, k, j, block, half):
            Rh = s_refs[a].shape[1] // 2
            rc = Rh // _D2D_CHUNKS
            rows = pl.ds(pl.multiple_of(half * Rh + j * rc, 16), rc)
            idx = (3 * a + k) * _D2D_CHUNKS + j
            return pltpu.make_async_remote_copy(src_ref=o_refs[a].at[block, rows], dst_ref=o_refs[a].at[block, rows],
                                                send_sem=d2d_send.at[idx], recv_sem=d2d_recv.at[idx],
                                                device_id=(x, y, 1 - c), device_id_type=MESH)

        sends = [ici(a, k, chip, mine) for a in range(na) for k, chip in enumerate(chips)]
        for cp in sends:
            cp.start()
        for a in range(na):
            for k, chip in enumerate(chips):
                block = 2 * chip[0] + chip[1]
                ici(a, k, chip, block).wait_recv()
                for j in range(_D2D_CHUNKS):
                    cp = d2d(a, k, j, block, c)
                    cp.start()
                    sends.append(cp)
        for a in range(na):
            for k, chip in enumerate(chips):
                for j in range(_D2D_CHUNKS):
                    d2d(a, k, j, 2 * chip[0] + chip[1], 1 - c).wait_recv()
        for cp in sends:
            cp.wait_send()

    start = [jnp.broadcast_to(s[None], (4,) + tuple(s.shape)) for s in shards]
    outs = [jax.ShapeDtypeStruct(t.shape, t.dtype) for t in start]
    n_d2d = 3 * na * _D2D_CHUNKS
    return _comm_call("gather_weights", body, start, outs, [3 * na, 3 * na, n_d2d, n_d2d],
                      aliases={a: a for a in range(na)})


def _pair_exchange_call(packed):
    na = len(packed)

    def body(*refs):
        p_refs, o_refs = refs[:na], refs[na:2 * na]
        send_sems, recv_sems = refs[2 * na:]
        x, y, c = _my_place()
        copies = []
        for a in range(na):
            nb, _, Rh, _ = p_refs[a].shape
            rc = Rh // _D2D_CHUNKS
            assert rc * _D2D_CHUNKS == Rh and rc % 16 == 0
            for b in range(nb):
                for j in range(_D2D_CHUNKS):
                    rows = pl.ds(j * rc, rc)
                    idx = (a * nb + b) * _D2D_CHUNKS + j
                    copies.append(pltpu.make_async_remote_copy(
                        src_ref=p_refs[a].at[b, 1 - c, rows], dst_ref=o_refs[a].at[b, rows],
                        send_sem=send_sems.at[idx], recv_sem=recv_sems.at[idx],
                        device_id=(x, y, 1 - c), device_id_type=MESH))
        for t in copies:
            t.start()
        for t in copies:
            t.wait_recv()
        for t in copies:
            t.wait_send()

    outs = [jax.ShapeDtypeStruct((p.shape[0], p.shape[2], p.shape[3]), p.dtype) for p in packed]
    n = sum(p.shape[0] for p in packed) * _D2D_CHUNKS
    return _comm_call("pair_exchange", body, packed, outs, [n, n])


def _pair_add_call(name, packed, other, core):
    nb, _, Rh, C = packed.shape
    tr = _pick(Rh, 512, 16)

    def body(c_ref, p_ref, o_ref, q_ref):
        q_ref[...] = (p_ref[...].astype(F32) + o_ref[...].astype(F32)).astype(q_ref.dtype)

    grid_spec = pltpu.PrefetchScalarGridSpec(
        num_scalar_prefetch=1, grid=(nb, Rh // tr),
        in_specs=[pl.BlockSpec((None, None, tr, C), lambda b, r, c_ref: (b, c_ref[0], r, 0)),
                  pl.BlockSpec((None, tr, C), lambda b, r, c_ref: (b, r, 0))],
        out_specs=pl.BlockSpec((None, tr, C), lambda b, r, c_ref: (b, r, 0)))
    return pl.pallas_call(
        body, name=name, grid_spec=grid_spec, out_shape=jax.ShapeDtypeStruct((nb, Rh, C), packed.dtype),
        compiler_params=_params(("arbitrary", "arbitrary")),
    )(core, packed, other)


def _chip_scatter_call(pairs):
    na = len(pairs)

    def body(*refs):
        p_refs, o_refs = refs[:na], refs[na:2 * na]
        send_sems, recv_sems, local_sems = refs[2 * na:]
        x, y, c = _my_place()
        mine = 2 * x + y
        chips = [(1 - x, y), (x, 1 - y), (1 - x, 1 - y)]
        pending = []
        for a in range(na):
            p_ref, o_ref = p_refs[a], o_refs[a]
            pending += _local_copies(lambda sl: p_ref.at[mine, sl], lambda sl: o_ref.at[mine, sl], p_ref.shape[1],
                                     local_sems, a * _LOCAL_CHUNKS)
        for t in pending:
            t.start()
        copies = []
        for a in range(na):
            for k, (px, py) in enumerate(chips):
                t = pltpu.make_async_remote_copy(src_ref=p_refs[a].at[2 * px + py], dst_ref=o_refs[a].at[mine],
                                                 send_sem=send_sems.at[3 * a + k], recv_sem=recv_sems.at[3 * a + k],
                                                 device_id=(px, py, c), device_id_type=MESH)
                t.start()
                copies.append(t)
        for a in range(na):
            for k, (px, py) in enumerate(chips):
                pltpu.make_async_remote_copy(src_ref=p_refs[a].at[mine], dst_ref=o_refs[a].at[2 * px + py],
                                             send_sem=send_sems.at[3 * a + k], recv_sem=recv_sems.at[3 * a + k],
                                             device_id=(px, py, c), device_id_type=MESH).wait_recv()
        for t in copies:
            t.wait_send()
        for t in pending:
            t.wait()

    outs = [jax.ShapeDtypeStruct(p.shape, p.dtype) for p in pairs]
    return _comm_call("chip_scatter", body, pairs, outs, [3 * na, 3 * na, na * _LOCAL_CHUNKS])


def _sum_slots_call(name, parts, core):
    n, R, C = parts.shape
    tr = _pick(R, 256, 8)

    def body(c_ref, p_ref, o_ref):
        acc = p_ref[0].astype(F32)
        for s in range(1, n):
            acc = acc + p_ref[s].astype(F32)
        o_ref[...] = acc

    grid_spec = pltpu.PrefetchScalarGridSpec(
        num_scalar_prefetch=1, grid=(R // tr,),
        in_specs=[pl.BlockSpec((n, tr, C), lambda i, c_ref: (0, i, 0))],
        out_specs=pl.BlockSpec((None, tr, C), lambda i, c_ref: (c_ref[0], i, 0)))
    return pl.pallas_call(
        body, name=name, grid_spec=grid_spec, out_shape=jax.ShapeDtypeStruct((2, R, C), F32),
        compiler_params=_params(("arbitrary",)),
    )(core, parts)


def _sibling_share_call(halves):
    na = len(halves)
    nch = 2 * _D2D_CHUNKS

    def body(*refs):
        h_refs, o_refs = refs[:na], refs[na:2 * na]
        send_sems, recv_sems = refs[2 * na:]
        x, y, c = _my_place()

        def cp(a, j, slot):
            rc = h_refs[a].shape[1] // nch
            rows = pl.ds(j * rc, rc)
            return pltpu.make_async_remote_copy(src_ref=h_refs[a].at[slot, rows], dst_ref=o_refs[a].at[slot, rows],
                                                send_sem=send_sems.at[a * nch + j], recv_sem=recv_sems.at[a * nch + j],
                                                device_id=(x, y, 1 - c), device_id_type=MESH)

        copies = [cp(a, j, c) for a in range(na) for j in range(nch)]
        for t in copies:
            t.start()
        for a in range(na):
            for j in range(nch):
                cp(a, j, 1 - c).wait_recv()
        for t in copies:
            t.wait_send()

    for h in halves:
        assert h.shape[1] % (nch * 8) == 0, h.shape
    outs = [jax.ShapeDtypeStruct(h.shape, h.dtype) for h in halves]
    return _comm_call("sibling_share", body, halves, outs, [na * nch, na * nch], aliases={a: a for a in range(na)})


def _allreduce_small_call(part):
    R, C = part.shape

    def body(p_ref, o_ref, slots, send_sems, recv_sems):
        x, y, c = _my_place()
        me = 4 * x + 2 * y + c
        peers = []
        for k in range(1, 8):
            px = x ^ (k >> 2) if (k >> 2) else x
            py = y ^ ((k >> 1) & 1) if ((k >> 1) & 1) else y
            pc = c ^ (k & 1) if (k & 1) else c
            peers.append((px, py, pc))
        copies = []
        for k, (px, py, pc) in enumerate(peers):
            cp = pltpu.make_async_remote_copy(src_ref=p_ref, dst_ref=slots.at[me], send_sem=send_sems.at[k],
                                              recv_sem=recv_sems.at[k], device_id=(px, py, pc), device_id_type=MESH)
            cp.start()
            copies.append(cp)
        slots[me] = p_ref[...]
        for k, (px, py, pc) in enumerate(peers):
            pltpu.make_async_remote_copy(src_ref=p_ref, dst_ref=slots.at[4 * px + 2 * py + pc],
                                         send_sem=send_sems.at[k], recv_sem=recv_sems.at[k],
                                         device_id=(px, py, pc), device_id_type=MESH).wait_recv()
        for cp in copies:
            cp.wait_send()
        acc = slots[0]
        for s in range(1, 8):
            acc = acc + slots[s]
        o_ref[...] = acc

    vm = pl.BlockSpec(memory_space=pltpu.VMEM)
    return pl.pallas_call(
        body, name="allreduce_small", in_specs=[vm], out_specs=vm,
        out_shape=jax.ShapeDtypeStruct((R, C), F32),
        scratch_shapes=[pltpu.VMEM((8, R, C), F32), pltpu.SemaphoreType.DMA((7,)), pltpu.SemaphoreType.DMA((7,))],
        compiler_params=pltpu.CompilerParams(has_side_effects=True, vmem_limit_bytes=_VMEM_LIMIT),
    )(part)


def _adamw_call(name, w, g, m, v):
    shape = w.shape
    cols = shape[-1]
    rows = int(np.prod(shape[:-1])) if len(shape) > 1 else 1
    w2, g2, m2, v2 = (t.reshape(rows, cols) for t in (w, g, m, v))
    tr = _pick(rows, 256, 8)
    c1 = 1.0 / (1.0 - ADAM_B1 ** ADAM_STEP)
    c2 = 1.0 / (1.0 - ADAM_B2 ** ADAM_STEP)

    def body(w_ref, g_ref, m_ref, v_ref, d_ref, nm_ref, nv_ref):
        gr = g_ref[...]
        m_new = ADAM_B1 * m_ref[...] + (1.0 - ADAM_B1) * gr
        v_new = ADAM_B2 * v_ref[...] + (1.0 - ADAM_B2) * (gr * gr)
        m_hat = m_new / (1.0 - ADAM_B1 ** ADAM_STEP)
        v_hat = v_new / (1.0 - ADAM_B2 ** ADAM_STEP)
        d_ref[...] = -ADAM_LR * (m_hat / (jnp.sqrt(v_hat) + ADAM_EPS) + ADAM_WD * w_ref[...])
        nm_ref[...] = m_new
        nv_ref[...] = v_new

    spec = pl.BlockSpec((tr, cols), lambda i: (i, 0))
    d, nm, nv = pl.pallas_call(
        body, name=name, grid=(rows // tr,), in_specs=[spec] * 4, out_specs=[spec] * 3,
        out_shape=[jax.ShapeDtypeStruct((rows, cols), F32)] * 3, compiler_params=_params(("arbitrary",)),
    )(w2, g2, m2, v2)
    return d.reshape(shape), nm.reshape(shape), nv.reshape(shape)


def _pad_rows(flat, mult):
    n = flat.shape[0]
    unit = PACK_COLS * mult
    total = -(-n // unit) * unit
    return jnp.pad(flat, (0, total - n)).reshape(total // PACK_COLS, PACK_COLS)


def _pad_axis(arr, axis, mult):
    n = arr.shape[axis]
    total = -(-n // mult) * mult
    if total == n:
        return arr
    widths = [(0, 0)] * arr.ndim
    widths[axis] = (0, total - n)
    return jnp.pad(arr, widths)


def _shard_slice(arr, axis, blk, nblk=4):
    w = arr.shape[axis] // nblk
    return lax.slice_in_dim(arr, blk * w, (blk + 1) * w, axis=axis)


_NT = (((1,), (1,)), ((), ()))
_TN = (((0,), (0,)), ((), ()))


def _ffn_fwd(tag, x, h, wg, wu, wd, g_next):
    S, D = x.shape
    (wg, gi), (wu, ui), (wd, di) = wg, wu, wd
    NB, Fb = wg.shape[0], wg.shape[2]
    tm = _pick(S, 1024, 8)

    def gate_up(h_ref, wg_ref, wu_ref, a_ref, u_ref, s_ref):
        h_t = _mx(h_ref[...])
        a = jnp.dot(h_t, _mx(wg_ref[...]), preferred_element_type=F32)
        u = jnp.dot(h_t, _mx(wu_ref[...]), preferred_element_type=F32)
        sig = _sigmoid(a)
        silu = a * sig
        a_ref[...] = (u * (sig * (1.0 + a * (1.0 - sig)))).astype(a_ref.dtype)
        u_ref[...] = silu.astype(u_ref.dtype)
        s_ref[...] = (silu * u).astype(s_ref.dtype)

    hid = pl.BlockSpec((None, tm, Fb), lambda b, i: (b, i, 0))
    a, u, s = pl.pallas_call(
        gate_up, name=tag + "_gate_up", grid=(NB, S // tm),
        in_specs=[pl.BlockSpec((tm, D), lambda b, i: (i, 0)), pl.BlockSpec((None, D, Fb), lambda b, i: (b, gi, 0)),
                  pl.BlockSpec((None, D, Fb), lambda b, i: (b, ui, 0))], out_specs=[hid] * 3,
        out_shape=[jax.ShapeDtypeStruct((NB, S, Fb), BF16)] * 3, compiler_params=_params(("arbitrary", "arbitrary")),
    )(h, wg, wu)

    tm2 = _pick(S, 512, 8)

    def down(s_ref, wd_ref, x_ref, g_ref, xo_ref, ho_ref):
        acc = jnp.dot(_mx(s_ref[0]), _mx(wd_ref[0]), preferred_element_type=F32)
        for b in range(1, NB):
            acc = acc + jnp.dot(_mx(s_ref[b]), _mx(wd_ref[b]), preferred_element_type=F32)
        x_new = x_ref[...] + 0.5 * acc
        xo_ref[...] = x_new
        ho_ref[...] = (_rms_stats(x_new)[0] * g_ref[...]).astype(ho_ref.dtype)

    row = pl.BlockSpec((tm2, D), lambda i: (i, 0))
    x_new, h_next = pl.pallas_call(
        down, name=tag + "_down", grid=(S // tm2,),
        in_specs=[pl.BlockSpec((NB, tm2, Fb), lambda i: (0, i, 0)), pl.BlockSpec((NB, Fb, D), lambda i: (0, di, 0)),
                  row, pl.BlockSpec((1, D), lambda i: (0, 0))],
        out_specs=[row, row], out_shape=[jax.ShapeDtypeStruct((S, D), F32), jax.ShapeDtypeStruct((S, D), BF16)],
        compiler_params=_params(("arbitrary",)),
    )(s, wd, x, g_next)
    return x_new, h_next, (x, h, a, u, s)


def _ffn_bwd(tag, saved, dx_out, dxb, wg, wu, wd, gain):
    x, h, a, u, s = saved
    S, D = x.shape
    (wg, gi), (wu, ui), (wd, di) = wg, wu, wd
    NB, Fb = wg.shape[0], wg.shape[2]
    tm = _pick(S, 1024, 8)
    tk = _pick(S, 1024, 128)
    nk = S // tk

    def dgate_up(d_ref, wd_ref, a_ref, u_ref, da_ref, du_ref):
        ds = 0.5 * lax.dot_general(_mx(d_ref[...]), _mx(wd_ref[...]), _NT, preferred_element_type=F32)
        da_ref[...] = (ds * a_ref[...].astype(F32)).astype(da_ref.dtype)
        du_ref[...] = (ds * u_ref[...].astype(F32)).astype(du_ref.dtype)

    hid = pl.BlockSpec((None, tm, Fb), lambda b, i: (b, i, 0))
    da, du = pl.pallas_call(
        dgate_up, name=tag + "_dgate_up", grid=(NB, S // tm),
        in_specs=[pl.BlockSpec((tm, D), lambda b, i: (i, 0)), pl.BlockSpec((None, Fb, D), lambda b, i: (b, di, 0)),
                  hid, hid],
        out_specs=[hid, hid], out_shape=[jax.ShapeDtypeStruct((NB, S, Fb), BF16)] * 2,
        compiler_params=_params(("arbitrary", "arbitrary")),
    )(dxb, wd, a, u)

    def dw_down(s_ref, d_ref, o_ref, acc_ref):
        k = pl.program_id(1)
        p = lax.dot_general(_mx(s_ref[...]), _mx(d_ref[...]), _TN, preferred_element_type=F32)

        @pl.when(k == 0)
        def _():
            acc_ref[...] = p

        @pl.when(k != 0)
        def _():
            acc_ref[...] += p

        @pl.when(k == nk - 1)
        def _():
            o_ref[...] = (0.5 * acc_ref[...]).astype(o_ref.dtype)

    hk = pl.BlockSpec((None, tk, Fb), lambda b, k: (b, k, 0))
    dwd = pl.pallas_call(
        dw_down, name=tag + "_dw_down", grid=(NB, nk),
        in_specs=[hk, pl.BlockSpec((tk, D), lambda b, k: (k, 0))],
        out_specs=pl.BlockSpec((None, Fb, D), lambda b, k: (b, 0, 0)),
        out_shape=jax.ShapeDtypeStruct((NB, Fb, D), _WIRE_DTYPE), scratch_shapes=[pltpu.VMEM((Fb, D), F32)],
        compiler_params=_params(("arbitrary", "arbitrary")),
    )(s, dxb)

    def dw_gate_up(h_ref, da_ref, du_ref, og_ref, ou_ref, accg_ref, accu_ref):
        k = pl.program_id(1)
        h_t = _mx(h_ref[...])
        pg = lax.dot_general(h_t, _mx(da_ref[...]), _TN, preferred_element_type=F32)
        pu = lax.dot_general(h_t, _mx(du_ref[...]), _TN, preferred_element_type=F32)

        @pl.when(k == 0)
        def _():
            accg_ref[...] = pg
            accu_ref[...] = pu

        @pl.when(k != 0)
        def _():
            accg_ref[...] += pg
            accu_ref[...] += pu

        @pl.when(k == nk - 1)
        def _():
            og_ref[...] = accg_ref[...].astype(og_ref.dtype)
            ou_ref[...] = accu_ref[...].astype(ou_ref.dtype)

    wout = pl.BlockSpec((None, D, Fb), lambda b, k: (b, 0, 0))
    dwg, dwu = pl.pallas_call(
        dw_gate_up, name=tag + "_dw_gate_up", grid=(NB, nk),
        in_specs=[pl.BlockSpec((tk, D), lambda b, k: (k, 0)), hk, hk], out_specs=[wout, wout],
        out_shape=[jax.ShapeDtypeStruct((NB, D, Fb), _WIRE_DTYPE)] * 2,
        scratch_shapes=[pltpu.VMEM((D, Fb), F32)] * 2, compiler_params=_params(("arbitrary", "arbitrary")),
    )(h, da, du)

    tm2 = _pick(S, 512, 8)

    def dx_body(da_ref, du_ref, wg_hbm, wu_hbm, x_ref, dxo_ref, g_ref, dx_ref, dxb_ref, dg_ref, wg_v, wu_v, sem):
        i = pl.program_id(0)

        @pl.when(i == 0)
        def _():
            cg = pltpu.make_async_copy(wg_hbm.at[:, pl.ds(gi * D, D), :], wg_v, sem.at[0])
            cu = pltpu.make_async_copy(wu_hbm.at[:, pl.ds(ui * D, D), :], wu_v, sem.at[1])
            cg.start()
            cu.start()
            cg.wait()
            cu.wait()

        dh = None
        for b in range(NB):
            t = lax.dot_general(_mx(da_ref[b]), wg_v[b], _NT, preferred_element_type=F32)
            t = t + lax.dot_general(_mx(du_ref[b]), wu_v[b], _NT, preferred_element_type=F32)
            dh = t if dh is None else dh + t
        dx_n, dg = _rms_bwd(x_ref[...], g_ref[...], dh)
        dx = dxo_ref[...] + dx_n
        dx_ref[...] = dx
        dxb_ref[...] = dx.astype(dxb_ref.dtype)
        dg = jnp.sum(dg, axis=0, keepdims=True)

        @pl.when(i == 0)
        def _():
            dg_ref[...] = dg

        @pl.when(i != 0)
        def _():
            dg_ref[...] += dg

    row = pl.BlockSpec((tm2, D), lambda i: (i, 0))
    hid2 = pl.BlockSpec((NB, tm2, Fb), lambda i: (0, i, 0))
    anyspec = pl.BlockSpec(memory_space=pl.ANY)
    fixed = pl.BlockSpec((1, D), lambda i: (0, 0))
    dx, dxb_new, dgain = pl.pallas_call(
        dx_body, name=tag + "_dx", grid=(S // tm2,),
        in_specs=[hid2, hid2, anyspec, anyspec, row, row, fixed], out_specs=[row, row, fixed],
        out_shape=[jax.ShapeDtypeStruct((S, D), F32), jax.ShapeDtypeStruct((S, D), BF16),
                   jax.ShapeDtypeStruct((1, D), F32)],
        scratch_shapes=[pltpu.VMEM((NB, D, Fb), wg.dtype), pltpu.VMEM((NB, D, Fb), wu.dtype),
                        pltpu.SemaphoreType.DMA((2,))],
        compiler_params=_params(("arbitrary",)),
    )(da, du, wg, wu, x, dx_out, gain)
    return dx, dxb_new, dgain, dwg, dwu, dwd


def _conv_mixer_fwd(tag, x, h, w_in, w_taps, w_out, g_next):
    S, D = x.shape
    C3 = w_in.shape[1]
    tm = _pick(S, 512, 8)
    p, = _fused_matmul(tag + "_in", 'nn', [_op(h)], [_op(w_in)], [(0, 0, 0)], 1, _ident_epi(), [F32],
                       S, C3, D, tm, _pick(C3, 1024, 128), D)
    m = _conv_fwd_call(p, w_taps)
    x_new, h_next = _fused_matmul(tag + "_out", 'nn', [_op(m)], [_op(w_out)], [(0, 0, 0)], 1, _resid_norm_epi(1.0),
                                  [F32, BF16], S, D, D, tm, D, D, tile_extras=[x], row_extras=[g_next])
    return x_new, h_next, (x, h, p, m)


def _conv_mixer_bwd(tag, saved, dx_out, dxb, w_in, w_taps, w_out, gain):
    x, h, p, m = saved
    S, D = x.shape
    C3 = w_in.shape[1]
    tm = _pick(S, 512, 8)
    tk = _pick(S, 1024, 128)
    dm, = _fused_matmul(tag + "_dm", 'nt', [_op(dxb)], [_op(w_out)], [(0, 0, 0)], 1, _ident_epi(), [F32],
                        S, D, D, tm, D, D)
    dw_out, = _fused_matmul(tag + "_dw_out", 'tn', [_op(m)], [_op(dxb)], [(0, 0, 0)], 1, _ident_epi(), [F32],
                            D, D, S, D, D, tk)
    dp, dtaps = _conv_bwd_call(p, w_taps, dm)
    dw_in, = _fused_matmul(tag + "_dw_in", 'tn', [_op(h)], [_op(dp)], [(0, 0, 0)], 1, _ident_epi(), [F32],
                           D, C3, S, D, _pick(C3, 1024, 128), tk)
    dx, dxb_new, dgain = _fused_matmul(tag + "_dx", 'nt', [_op(dp)], [_op(w_in)], [(0, 0, 0)], 1, _norm_bwd_epi,
                                       [F32, BF16], S, D, C3, tm, D, C3,
                                       tile_extras=[x, dx_out], row_extras=[gain], n_colsum=1)
    return dx, dxb_new, dgain, dw_in, dtaps, dw_out


def _attn_scale():
    return np.float32(QK_DIM ** -0.5)


def _even_mixer_fwd(tag, x, h, wts, tables, g_next):
    S, D = x.shape
    cos, sa, sb = tables
    tm = _pick(S, 512, 8)
    AW = HEADS * HP
    proj, = _fused_matmul(tag + "_in", 'nn', [_op(h)], [_op(wts['w_in'])], [(0, 0, 0)], 1, _ident_epi(), [F32],
                          S, PROJ_W, D, tm, _pick(PROJ_W, 896, 128), D)
    cqn, ckvn, kr, u, vn = _even_prep_call(proj, wts['q_norm'], wts['kv_norm'], wts['sg_norm'], cos, sa, sb)
    scale = _attn_scale()

    def q_epi(accs, tiles, rows, mrows):
        c_t, a_t, b_t = mrows
        heads = [_rope(accs[0][:, hh * HP:(hh + 1) * HP], c_t, a_t, b_t) * scale for hh in range(HEADS)]
        return [jnp.concatenate(heads, axis=1)]

    q, = _fused_matmul(tag + "_q", 'nn', [_op(cqn)], [_op(wts['w_q'])], [(0, 0, 0)], 1, q_epi, [BF16],
                       S, AW, Q_LORA, tm, AW, Q_LORA, mrow_extras=[cos, sa, sb])

    def kv_epi(accs, tiles, rows, mrows):
        lane = lax.broadcasted_iota(jnp.int32, accs[1].shape, 1)
        v_t = jnp.where((lane & (HP - 1)) == VDIM, 1.0, accs[1])
        return [accs[0] + jnp.concatenate([mrows[0].astype(F32)] * HEADS, axis=1), v_t]

    k, v = _fused_matmul(tag + "_kv", 'nn', [_op(ckvn)], [_op(wts['w_k']), _op(wts['w_v'])],
                         [(0, 0, 0), (0, 1, 1)], 2, kv_epi, [BF16, BF16], S, AW, KV_LORA, tm, AW, KV_LORA,
                         mrow_extras=[kr])
    o, lse = _flash_fwd_call(q, k, v)
    mix = _sgu_fwd_call(vn, u, o, wts['sg_wst'], wts['sg_bexp'])
    x_new, h_next = _fused_matmul(tag + "_out", 'nn', [_op(mix)], [_op(wts['w_out'])], [(0, 0, 0)], 1,
                                  _resid_norm_epi(1.0), [F32, BF16], S, D, AW + SG_WIDTH, tm, D, AW + SG_WIDTH,
                                  tile_extras=[x], row_extras=[g_next])
    return x_new, h_next, (x, h, proj, cqn, ckvn, u, vn, q, k, v, o, lse, mix)


def _even_mixer_bwd(tag, saved, dx_out, dxb, wts, tables, gain):
    x, h, proj, cqn, ckvn, u, vn, q, k, v, o, lse, mix = saved
    S, D = x.shape
    cos, sa, sb = tables
    tm = _pick(S, 512, 8)
    tk = _pick(S, 1024, 128)
    AW = HEADS * HP
    MW = AW + SG_WIDTH
    dmix, = _fused_matmul(tag + "_dmix", 'nt', [_op(dxb)], [_op(wts['w_out'])], [(0, 0, 0)], 1, _ident_epi(), [BF16],
                          S, MW, D, tm, _pick(MW, 768, 128), D)
    dw_out, = _fused_matmul(tag + "_dw_out", 'tn', [_op(mix)], [_op(dxb)], [(0, 0, 0)], 1, _ident_epi(), [F32],
                            MW, D, S, _pick(MW, 768, 128), D, tk)
    du, dvn, dsg_w, dsg_b = _sgu_bwd_call(dmix, vn, u, wts['sg_wst'], wts['sg_wst_t'], wts['sg_bexp'])
    delta = _attn_delta_call(o, dmix)
    dq, dk, dv = _flash_bwd_call(q, k, v, dmix, lse, delta)
    scale = _attn_scale()

    def dq_epi(accs, tiles, rows, mrows):
        return accs

    def dq_pre_call():
        tr = _pick(S, 256, 8)

        def body(d_ref, c_ref, a_ref, b_ref, o_ref):
            for hh in range(HEADS):
                t = _rope_t(d_ref[:, hh * HP:(hh + 1) * HP], c_ref[...], a_ref[...], b_ref[...]) * scale
                o_ref[:, hh * HP:(hh + 1) * HP] = t.astype(o_ref.dtype)

        row = lambda i: (i, 0)
        return pl.pallas_call(
            body, name=tag + "_dq_unrope", grid=(S // tr,),
            in_specs=[pl.BlockSpec((tr, AW), row)] + [pl.BlockSpec((tr, HP), row)] * 3,
            out_specs=pl.BlockSpec((tr, AW), row), out_shape=jax.ShapeDtypeStruct((S, AW), BF16),
            compiler_params=_params(("arbitrary",)),
        )(dq, cos, sa, sb)

    dqp = dq_pre_call()
    dw_q, = _fused_matmul(tag + "_dw_q", 'tn', [_op(cqn)], [_op(dqp)], [(0, 0, 0)], 1, _ident_epi(), [F32],
                          Q_LORA, AW, S, Q_LORA, AW, tk)
    dcqn, = _fused_matmul(tag + "_dcq", 'nt', [_op(dqp)], [_op(wts['w_q'])], [(0, 0, 0)], 1, dq_epi, [F32],
                          S, Q_LORA, AW, tm, Q_LORA, AW)
    dw_k, dw_v = _fused_matmul(tag + "_dw_kv", 'tn', [_op(ckvn)], [_op(dk), _op(dv)], [(0, 0, 0), (0, 1, 1)], 2,
                               _ident_epi(), [F32, F32], KV_LORA, AW, S, KV_LORA, AW, tk)
    dckvn, = _fused_matmul(tag + "_dckv", 'nt', [_op(dk), _op(dv)], [_op(wts['w_k']), _op(wts['w_v'])],
                           [(0, 0, 0), (1, 1, 0)], 1, dq_epi, [F32], S, KV_LORA, AW, tm, KV_LORA, AW)
    dproj, dqn, dkvn, dsgn = _even_prep_bwd_call(proj, wts['q_norm'], wts['kv_norm'], wts['sg_norm'], cos, sa, sb,
                                                 dcqn, dckvn, dk, du, dvn)
    dw_in, = _fused_matmul(tag + "_dw_in", 'tn', [_op(h)], [_op(dproj)], [(0, 0, 0)], 1, _ident_epi(), [F32],
                           D, PROJ_W, S, D, _pick(PROJ_W, 896, 128), tk)
    dx, dxb_new, dgain = _fused_matmul(tag + "_dx", 'nt', [_op(dproj)], [_op(wts['w_in'])], [(0, 0, 0)], 1,
                                       _norm_bwd_epi, [F32, BF16], S, D, PROJ_W, tm, D, PROJ_W,
                                       tile_extras=[x, dx_out], row_extras=[gain], n_colsum=1)
    grads = dict(w_in=dw_in, w_q=dw_q, w_k=dw_k, w_v=dw_v, w_out=dw_out, q_norm=dqn, kv_norm=dkvn, sg_norm=dsgn,
                 sg_w=dsg_w, sg_b=dsg_b)
    return dx, dxb_new, dgain, grads


def _even_weights(w_in, w_uq, w_ukv, w_out, q_norm, kv_norm, sg_norm, sg_w, sg_b):
    D = w_in.shape[0]
    kr_cols = jnp.pad(w_in[:, Q_LORA + KV_LORA:Q_LORA + KV_LORA + ROPE], ((0, 0), (NOPE, HP - QK_DIM)))
    w_in_p = jnp.concatenate([w_in[:, :Q_LORA + KV_LORA], kr_cols, w_in[:, Q_LORA + KV_LORA + ROPE:]], axis=1)
    wq = w_uq.reshape(Q_LORA, HEADS, QK_DIM)
    w_q = jnp.pad(wq, ((0, 0), (0, 0), (0, HP - QK_DIM))).reshape(Q_LORA, HEADS * HP)
    wkv = w_ukv.reshape(KV_LORA, HEADS, NOPE + VDIM)
    w_k = jnp.pad(wkv[:, :, :NOPE], ((0, 0), (0, 0), (0, HP - NOPE))).reshape(KV_LORA, HEADS * HP)
    w_v = jnp.pad(wkv[:, :, NOPE:], ((0, 0), (0, 0), (0, HP - VDIM))).reshape(KV_LORA, HEADS * HP)
    wo_a = w_out[:HEADS * VDIM].reshape(HEADS, VDIM, D)
    wo_a = jnp.pad(wo_a, ((0, 0), (0, HP - VDIM), (0, 0))).reshape(HEADS * HP, D)
    w_out_p = jnp.concatenate([wo_a, w_out[HEADS * VDIM:]], axis=0)
    tri = jnp.tril(jnp.ones((SG_CHUNK, SG_CHUNK), F32))
    wm = sg_w * tri
    wst = wm.reshape(SG_GROUPS // 2, 2 * SG_CHUNK, SG_CHUNK).astype(_MXU_DTYPE)
    wst_t = jnp.swapaxes(wm, 1, 2).reshape(SG_GROUPS // 2, 2 * SG_CHUNK, SG_CHUNK).astype(_MXU_DTYPE)
    bexp = jnp.repeat(sg_b.T, SG_GDIM, axis=1)
    return dict(w_in=w_in_p, w_q=w_q, w_k=w_k, w_v=w_v, w_out=w_out_p, sg_wst=wst, sg_wst_t=wst_t, sg_bexp=bexp,
                q_norm=q_norm.reshape(1, -1), kv_norm=kv_norm.reshape(1, -1), sg_norm=sg_norm.reshape(1, -1))


def _even_grads_unpad(g):
    d_in = g['w_in']
    kr0 = Q_LORA + KV_LORA
    dw_in = jnp.concatenate([d_in[:, :kr0], d_in[:, kr0 + NOPE:kr0 + QK_DIM], d_in[:, kr0 + HP:]], axis=1)
    dw_uq = g['w_q'].reshape(Q_LORA, HEADS, HP)[:, :, :QK_DIM].reshape(Q_LORA, HEADS * QK_DIM)
    dk = g['w_k'].reshape(KV_LORA, HEADS, HP)[:, :, :NOPE]
    dv = g['w_v'].reshape(KV_LORA, HEADS, HP)[:, :, :VDIM]
    dw_ukv = jnp.concatenate([dk, dv], axis=2).reshape(KV_LORA, HEADS * (NOPE + VDIM))
    D = d_in.shape[0]
    wo = g['w_out']
    wo_a = wo[:HEADS * HP].reshape(HEADS, HP, D)[:, :VDIM].reshape(HEADS * VDIM, D)
    dw_out = jnp.concatenate([wo_a, wo[HEADS * HP:]], axis=0)
    dsg_b = g['sg_b'][:, :SG_GROUPS].T
    return dict(even_w_in=dw_in, w_uq=dw_uq, w_ukv=dw_ukv, even_w_out=dw_out, q_norm=g['q_norm'][0],
                kv_norm=g['kv_norm'][0], sg_norm=g['sg_norm'][0], sg_w=g['sg_w'], sg_b=dsg_b)


def kernel(x, positions, ffn_pre_norm, ffn_pre_w_gate, ffn_pre_w_up, ffn_pre_w_down, mix_norm, ffn_post_norm, ffn_post_w_gate, ffn_post_w_up, ffn_post_w_down, even_w_in, q_norm, w_uq, kv_norm, w_ukv, sg_norm, sg_w, sg_b, even_w_out, conv_w_in, conv_w, conv_w_out, final_norm, loss_target, m_ffn_pre_norm, m_ffn_pre_w_gate, m_ffn_pre_w_up, m_ffn_pre_w_down, m_mix_norm, m_ffn_post_norm, m_ffn_post_w_gate, m_ffn_post_w_up, m_ffn_post_w_down, m_even_w_in, m_q_norm, m_w_uq, m_kv_norm, m_w_ukv, m_sg_norm, m_sg_w, m_sg_b, m_even_w_out, m_conv_w_in, m_conv_w, m_conv_w_out, m_final_norm, v_ffn_pre_norm, v_ffn_pre_w_gate, v_ffn_pre_w_up, v_ffn_pre_w_down, v_mix_norm, v_ffn_post_norm, v_ffn_post_w_gate, v_ffn_post_w_up, v_ffn_post_w_down, v_even_w_in, v_q_norm, v_w_uq, v_kv_norm, v_w_ukv, v_sg_norm, v_sg_w, v_sg_b, v_even_w_out, v_conv_w_in, v_conv_w, v_conv_w_out, v_final_norm):
    env = dict(locals())
    w_loc = {n: env[n] for n in WEIGHTS}
    m_loc = {n: env['m_' + n] for n in WEIGHTS}
    v_loc = {n: env['v_' + n] for n in WEIGHTS}
    S, D = x.shape[1], x.shape[2]
    depth = ffn_pre_norm.shape[0]
    xs = x.reshape(S, D)
    target = loss_target.reshape(S, D)

    Fb = ffn_pre_w_gate.shape[2]
    wire2d = lambda n, cols: w_loc[n].astype(_WIRE_DTYPE).reshape(-1, cols)
    shard_a = _pad_axis(jnp.concatenate([wire2d(n, Fb) for n in GROUP_A], axis=0), 0, PACK_ROW_MULT)
    shard_b = _pad_axis(jnp.concatenate([wire2d(n, D) for n in GROUP_B], axis=0), 0, PACK_ROW_MULT)
    shard_c = _pad_rows(jnp.concatenate([w_loc[n].astype(_WIRE_DTYPE).reshape(-1) for n in GROUP_C]), PACK_ROW_MULT)
    gat_a, gat_b, gat_c = _gather_halves_call([shard_a, shard_b, shard_c])
    taps = _gather_weights_call("gather_taps", _pad_rows(conv_w.reshape(-1), 8)).reshape(4, -1)
    taps = jnp.concatenate([taps[b, :conv_w.size].reshape(conv_w.shape) for b in range(4)], axis=2)
    full = {}
    for ia, n in enumerate(GROUP_A):
        full[n] = [(gat_a, ia * depth + l) for l in range(depth)]
    row = 0
    for n in GROUP_B:
        shp = w_loc[n].shape
        rows = shp[0] * shp[1]
        if n in FFN_WEIGHTS:
            full[n] = [(gat_b, row // shp[1] + l) for l in range(depth)]
        else:
            full[n] = jnp.concatenate([gat_b[b, row:row + rows].reshape(shp) for b in range(4)], axis=1)
        row += rows
    gflat = gat_c.reshape(4, -1)
    off = 0
    for n in GROUP_C:
        shp = w_loc[n].shape
        size = int(np.prod(shp))
        full[n] = jnp.concatenate([gflat[b, off:off + size].reshape(shp) for b in range(4)], axis=SHARD_AXIS[n])
        off += size

    inv_freq = ROPE_THETA ** (-jnp.arange(0, ROPE, 2, dtype=F32) / ROPE)
    half = ROPE // 2
    zeros = lambda n: jnp.zeros((n,), F32)
    ones = jnp.ones((half,), F32)
    invf = jnp.concatenate([zeros(NOPE), inv_freq, inv_freq, zeros(HP - QK_DIM)]).reshape(1, HP)
    mask_a = jnp.concatenate([zeros(NOPE), -ones, zeros(HP - NOPE - half)]).reshape(1, HP)
    mask_b = jnp.concatenate([zeros(NOPE + half), ones, zeros(HP - QK_DIM)]).reshape(1, HP)
    tables = _rope_tables_call(positions.reshape(S, 1), invf, mask_a, mask_b)

    even_w = []
    for e in range((depth + 1) // 2):
        even_w.append(_even_weights(full['even_w_in'][e], full['w_uq'][e], full['w_ukv'][e], full['even_w_out'][e],
                                    q_norm[e], kv_norm[e], sg_norm[e], sg_w[e], sg_b[e]))

    def gain_row(arr, l):
        return arr[l].reshape(1, D)

    saved = []
    h = _rmsnorm_call("first_norm", xs, gain_row(ffn_pre_norm, 0))
    xc = xs
    for l in range(depth):
        xc, h, s_pre = _ffn_fwd(f"l{l}_pre", xc, h, full['ffn_pre_w_gate'][l], full['ffn_pre_w_up'][l],
                                full['ffn_pre_w_down'][l], gain_row(mix_norm, l))
        if l % 2 == 0:
            xc, h, s_mix = _even_mixer_fwd(f"l{l}_mix", xc, h, even_w[l // 2], tables, gain_row(ffn_post_norm, l))
        else:
            o = l // 2
            xc, h, s_mix = _conv_mixer_fwd(f"l{l}_mix", xc, h, full['conv_w_in'][o], taps[o],
                                           full['conv_w_out'][o], gain_row(ffn_post_norm, l))
        g_next = gain_row(ffn_pre_norm, l + 1) if l + 1 < depth else final_norm.reshape(1, D)
        xc, h, s_post = _ffn_fwd(f"l{l}_post", xc, h, full['ffn_post_w_gate'][l], full['ffn_post_w_up'][l],
                                 full['ffn_post_w_down'][l], g_next)
        saved.append((s_pre, s_mix, s_post))

    dx, dxb, d_final, loss_part = _loss_call(xc, target, final_norm.reshape(1, D))
    loss = lax.psum(loss_part[0, 0], ("x", "y", "c"))

    gl = {n: [None] * w_loc[n].shape[0] for n in WEIGHTS if n != 'final_norm'}
    for l in reversed(range(depth)):
        s_pre, s_mix, s_post = saved[l]
        dx, dxb, dgain, dwg, dwu, dwd = _ffn_bwd(f"l{l}_post", s_post, dx, dxb, full['ffn_post_w_gate'][l],
                                                 full['ffn_post_w_up'][l], full['ffn_post_w_down'][l],
                                                 gain_row(ffn_post_norm, l))
        gl['ffn_post_norm'][l] = dgain[0]
        gl['ffn_post_w_gate'][l], gl['ffn_post_w_up'][l], gl['ffn_post_w_down'][l] = dwg, dwu, dwd
        if l % 2 == 0:
            e = l // 2
            dx, dxb, dgain, eg = _even_mixer_bwd(f"l{l}_mix", s_mix, dx, dxb, even_w[e], tables, gain_row(mix_norm, l))
            for n, val in _even_grads_unpad(eg).items():
                gl[n][e] = val
        else:
            o = l // 2
            dx, dxb, dgain, dw_in, dtaps, dw_out = _conv_mixer_bwd(f"l{l}_mix", s_mix, dx, dxb, full['conv_w_in'][o],
                                                                   taps[o], full['conv_w_out'][o],
                                                                   gain_row(mix_norm, l))
            gl['conv_w_in'][o], gl['conv_w'][o], gl['conv_w_out'][o] = dw_in, dtaps, dw_out
        gl['mix_norm'][l] = dgain[0]
        dx, dxb, dgain, dwg, dwu, dwd = _ffn_bwd(f"l{l}_pre", s_pre, dx, dxb, full['ffn_pre_w_gate'][l],
                                                 full['ffn_pre_w_up'][l], full['ffn_pre_w_down'][l],
                                                 gain_row(ffn_pre_norm, l))
        gl['ffn_pre_norm'][l] = dgain[0]
        gl['ffn_pre_w_gate'][l], gl['ffn_pre_w_up'][l], gl['ffn_pre_w_down'][l] = dwg, dwu, dwd
    grad_x = dx.reshape(x.shape)
    part = {n: jnp.stack(gl[n]) for n in gl if n not in FFN_WEIGHTS}
    part['final_norm'] = d_final[0]

    def row_blocked(n):
        g = part[n]
        L, r4, cols = g.shape
        return jnp.swapaxes(g.reshape(L, 4, r4 // 4, cols), 0, 1).reshape(4, L * (r4 // 4), cols).astype(_WIRE_DTYPE)

    pack_a = jnp.concatenate([gl[n][l] for n in GROUP_A for l in range(depth)], axis=1)
    pack_b = jnp.concatenate([gl[n][l] for n in GROUP_B if n in FFN_WEIGHTS for l in range(depth)]
                             + [row_blocked(n) for n in GROUP_B if n not in FFN_WEIGHTS], axis=1)
    pack_c = jnp.stack([_pad_rows(jnp.concatenate(
        [_shard_slice(part[n], SHARD_AXIS[n], b).astype(_WIRE_DTYPE).reshape(-1) for n in GROUP_C]), PACK_ROW_MULT)
        for b in range(4)])
    packs = [_pad_axis(p, 1, PACK_ROW_MULT) for p in (pack_a, pack_b, pack_c)]
    packs = [p.reshape(4, 2, p.shape[1] // 2, p.shape[2]) for p in packs]
    core = lax.axis_index("c").astype(jnp.int32).reshape(1)
    theirs = _pair_exchange_call(packs)
    pairs = [_pair_add_call(f"pair_add_{i}", p, t, core) for i, (p, t) in enumerate(zip(packs, theirs))]
    arrived = _chip_scatter_call(pairs)
    mine = [_sum_slots_call(f"sum_grad_slots_{i}", r, core) for i, r in enumerate(arrived)]
    red_a, red_b, red_c = [t.reshape(-1, t.shape[2]) for t in _sibling_share_call(mine)]
    grads = {}
    for group, red in ((GROUP_A, red_a), (GROUP_B, red_b)):
        row = 0
        for n in group:
            shp = w_loc[n].shape
            rows = shp[0] * shp[1]
            grads[n] = red[row:row + rows].reshape(shp)
            row += rows
    red_c = red_c.reshape(-1)
    off = 0
    for n in GROUP_C:
        shp = w_loc[n].shape
        size = int(np.prod(shp))
        grads[n] = red_c[off:off + size].reshape(shp)
        off += size

    small = _pad_rows(jnp.concatenate([part[n].reshape(-1) for n in REPLICATED]), 8)
    small_sum = _allreduce_small_call(small).reshape(-1)
    off = 0
    for n in REPLICATED:
        size = int(np.prod(w_loc[n].shape))
        grads[n] = small_sum[off:off + size].reshape(w_loc[n].shape)
        off += size

    deltas, new_m, new_v = {}, {}, {}
    for n in WEIGHTS:
        deltas[n], new_m[n], new_v[n] = _adamw_call("adamw_" + n, w_loc[n], grads[n], m_loc[n], v_loc[n])
    return (loss, grad_x, *[grads[n] for n in WEIGHTS], *[deltas[n] for n in WEIGHTS],
            *[new_m[n] for n in WEIGHTS], *[new_v[n] for n in WEIGHTS])
```

```python
import functools

import numpy as np
import jax
import jax.numpy as jnp
from jax import lax
from jax.experimental import pallas as pl
from jax.experimental.pallas import tpu as pltpu

F32 = jnp.float32
BF16 = jnp.bfloat16
_MXU_DTYPE = jnp.bfloat16
_WIRE_DTYPE = jnp.bfloat16
_VMEM_LIMIT = 52 * 1024 * 1024
_LANES = 128
_ATT_BLOCK = 512
_ROW_TILE = 512
_SG_TILE = 1024

NORM_EPS = 1e-6
HEADS = 8
NOPE = 64
ROPE = 32
VDIM = 64
QK_DIM = NOPE + ROPE
HP = 128
Q_LORA = 384
KV_LORA = 256
SG_WIDTH = 512
SG_GROUPS = 8
SG_GDIM = 64
SG_CHUNK = 128
ROPE_THETA = 10000.0
PROJ_W = Q_LORA + KV_LORA + HP + 2 * SG_WIDTH
ADAM_LR = 0.001
ADAM_B1 = 0.9
ADAM_B2 = 0.999
ADAM_EPS = 1e-08
ADAM_WD = 0.01
ADAM_STEP = 10
MESH = pl.DeviceIdType.MESH
PACK_COLS = 1024
PACK_ROW_MULT = 256

SHARDED = ['ffn_pre_w_gate', 'ffn_pre_w_up', 'ffn_pre_w_down', 'ffn_post_w_gate', 'ffn_post_w_up',
           'ffn_post_w_down', 'even_w_in', 'w_uq', 'w_ukv', 'even_w_out', 'conv_w_in', 'conv_w', 'conv_w_out']
SHARD_AXIS = {'ffn_pre_w_gate': 2, 'ffn_pre_w_up': 2, 'ffn_pre_w_down': 1, 'ffn_post_w_gate': 2,
              'ffn_post_w_up': 2, 'ffn_post_w_down': 1, 'even_w_in': 2, 'w_uq': 2, 'w_ukv': 2,
              'even_w_out': 1, 'conv_w_in': 2, 'conv_w': 2, 'conv_w_out': 1}
FFN_WEIGHTS = ['ffn_pre_w_gate', 'ffn_pre_w_up', 'ffn_pre_w_down', 'ffn_post_w_gate', 'ffn_post_w_up',
               'ffn_post_w_down']
GROUP_A = ['ffn_pre_w_gate', 'ffn_pre_w_up', 'ffn_post_w_gate', 'ffn_post_w_up']
GROUP_B = ['ffn_pre_w_down', 'ffn_post_w_down', 'even_w_out', 'conv_w_out']
GROUP_C = ['even_w_in', 'w_uq', 'w_ukv', 'conv_w_in', 'conv_w']
GATHER_C = ['even_w_in', 'w_uq', 'w_ukv', 'conv_w_in']
REPLICATED = ['ffn_pre_norm', 'mix_norm', 'ffn_post_norm', 'q_norm', 'kv_norm', 'sg_norm', 'sg_w', 'sg_b',
              'final_norm']
WEIGHTS = ['ffn_pre_norm', 'ffn_pre_w_gate', 'ffn_pre_w_up', 'ffn_pre_w_down', 'mix_norm', 'ffn_post_norm',
           'ffn_post_w_gate', 'ffn_post_w_up', 'ffn_post_w_down', 'even_w_in', 'q_norm', 'w_uq', 'kv_norm',
           'w_ukv', 'sg_norm', 'sg_w', 'sg_b', 'even_w_out', 'conv_w_in', 'conv_w', 'conv_w_out', 'final_norm']


def _params(sem=None):
    return pltpu.CompilerParams(vmem_limit_bytes=_VMEM_LIMIT,
                                **({} if sem is None else {'dimension_semantics': sem}))


def _pick(n, pref, mult):
    best = None
    t = mult
    while t <= min(n, pref):
        if n % t == 0:
            best = t
        t += mult
    return n if best is None else best


def _mx(v):
    return v if v.dtype == _MXU_DTYPE else v.astype(_MXU_DTYPE)


def _sigmoid(a):
    return 1.0 / (1.0 + jnp.exp(-a))


def _rms_stats(x):
    rstd = lax.rsqrt(jnp.mean(x * x, axis=-1, keepdims=True) + NORM_EPS)
    return x * rstd, rstd


def _rms_bwd(x, g, dh):
    xhat, rstd = _rms_stats(x)
    gdh = g * dh
    dx = rstd * (gdh - xhat * jnp.mean(gdh * xhat, axis=-1, keepdims=True))
    return dx, dh * xhat


def _fused_matmul(name, mode, lhs, rhs, prods, n_acc, epilogue, out_dtypes, M, N, K, tm, tn, tk,
                  tile_extras=(), row_extras=(), mrow_extras=(), n_colsum=0):
    gj, gi, gk = N // tn, M // tm, K // tk
    assert gj * tn == N and gi * tm == M and gk * tk == K, (name, M, N, K, tm, tn, tk)
    dims = {'nn': (((1,), (0,)), ((), ())), 'nt': (((1,), (1,)), ((), ())), 'tn': (((0,), (0,)), ((), ()))}[mode]

    def lhs_spec(roff, coff, kb):
        kb = tk if kb is None else kb
        if mode == 'tn':
            return pl.BlockSpec((kb, tm), lambda j, i, k: (k + roff, i + coff))
        return pl.BlockSpec((tm, kb), lambda j, i, k: (i + roff, k + coff))

    def rhs_spec(roff, coff, kb):
        kb = tk if kb is None else kb
        if mode == 'nt':
            return pl.BlockSpec((tn, kb), lambda j, i, k: (j + roff, k + coff))
        return pl.BlockSpec((kb, tn), lambda j, i, k: (k + roff, j + coff))

    in_specs = [lhs_spec(*a[1:]) for a in lhs] + [rhs_spec(*a[1:]) for a in rhs]
    in_specs += [pl.BlockSpec((tm, tn), lambda j, i, k: (i, j)) for _ in tile_extras]
    in_specs += [pl.BlockSpec((1, tn), lambda j, i, k: (0, j)) for _ in row_extras]
    in_specs += [pl.BlockSpec((tm, a.shape[1]), lambda j, i, k: (i, 0)) for a in mrow_extras]
    n_out = len(out_dtypes)
    out_shape = [jax.ShapeDtypeStruct((M, N), d) for d in out_dtypes]
    out_specs = [pl.BlockSpec((tm, tn), lambda j, i, k: (i, j)) for _ in out_dtypes]
    out_shape += [jax.ShapeDtypeStruct((1, N), F32) for _ in range(n_colsum)]
    out_specs += [pl.BlockSpec((1, tn), lambda j, i, k: (0, j)) for _ in range(n_colsum)]
    scratch = [pltpu.VMEM((tm, tn), F32) for _ in range(n_acc)] if gk > 1 else []
    nl, nr, nt, nrw, nm = len(lhs), len(rhs), len(tile_extras), len(row_extras), len(mrow_extras)

    def body(*refs):
        pos = 0
        lhs_refs = refs[pos:pos + nl]; pos += nl
        rhs_refs = refs[pos:pos + nr]; pos += nr
        tile_refs = refs[pos:pos + nt]; pos += nt
        row_refs = refs[pos:pos + nrw]; pos += nrw
        mrow_refs = refs[pos:pos + nm]; pos += nm
        out_refs = refs[pos:pos + n_out]; pos += n_out
        cs_refs = refs[pos:pos + n_colsum]; pos += n_colsum
        acc_refs = refs[pos:]
        i = pl.program_id(1)
        k = pl.program_id(2)

        def partials():
            res = [None] * n_acc
            for (li, ri, ai) in prods:
                d = lax.dot_general(_mx(lhs_refs[li][...]), _mx(rhs_refs[ri][...]), dims,
                                    preferred_element_type=F32)
                res[ai] = d if res[ai] is None else res[ai] + d
            return res

        def finish(accs):
            outs = epilogue(accs, [r[...] for r in tile_refs], [r[...] for r in row_refs],
                            [r[...] for r in mrow_refs])
            for r, o in zip(out_refs, outs[:n_out]):
                r[...] = o.astype(r.dtype)
            for r, c in zip(cs_refs, outs[n_out:]):
                c = jnp.sum(c, axis=0, keepdims=True)

                @pl.when(i == 0)
                def _():
                    r[...] = c

                @pl.when(i != 0)
                def _():
                    r[...] += c

        if gk == 1:
            finish(partials())
        else:
            p = partials()

            @pl.when(k == 0)
            def _():
                for r, v in zip(acc_refs, p):
                    r[...] = v

            @pl.when(k != 0)
            def _():
                for r, v in zip(acc_refs, p):
                    r[...] += v

            @pl.when(k == gk - 1)
            def _():
                finish([r[...] for r in acc_refs])

    res = pl.pallas_call(
        body, name=name, grid=(gj, gi, gk), in_specs=in_specs, out_specs=out_specs, out_shape=out_shape,
        scratch_shapes=scratch, compiler_params=_params(("arbitrary", "arbitrary", "arbitrary")),
    )(*[a[0] for a in lhs], *[a[0] for a in rhs], *tile_extras, *row_extras, *mrow_extras)
    return res


def _op(a, roff=0, coff=0, kb=None):
    return (a, roff, coff, kb)


def _ident_epi(scale=None):
    def epi(accs, tiles, rows, mrows):
        return [a if scale is None else a * scale for a in accs]
    return epi


def _resid_norm_epi(scale):
    def epi(accs, tiles, rows, mrows):
        x_new = tiles[0] + scale * accs[0]
        xhat, _ = _rms_stats(x_new)
        return [x_new, xhat * rows[0]]
    return epi


def _norm_bwd_epi(accs, tiles, rows, mrows):
    dx_n, dg = _rms_bwd(tiles[0], rows[0], accs[0])
    dx = tiles[1] + dx_n
    return [dx, dx, dg]


def _rmsnorm_call(name, x, g):
    S, D = x.shape
    tm = _pick(S, _ROW_TILE, 8)

    def body(x_ref, g_ref, h_ref):
        xhat, _ = _rms_stats(x_ref[...])
        h_ref[...] = (xhat * g_ref[...]).astype(h_ref.dtype)

    return pl.pallas_call(
        body, name=name, grid=(S // tm,),
        in_specs=[pl.BlockSpec((tm, D), lambda i: (i, 0)), pl.BlockSpec((1, D), lambda i: (0, 0))],
        out_specs=pl.BlockSpec((tm, D), lambda i: (i, 0)),
        out_shape=jax.ShapeDtypeStruct((S, D), BF16), compiler_params=_params(("arbitrary",)),
    )(x, g)


def _loss_call(x, target, g):
    S, D = x.shape
    tm = _pick(S, _ROW_TILE, 8)

    def body(x_ref, t_ref, g_ref, dx_ref, dxb_ref, dg_ref, loss_ref):
        i = pl.program_id(0)
        x_t = x_ref[...]
        gain = g_ref[...]
        xhat, _ = _rms_stats(x_t)
        diff = xhat * gain - t_ref[...]
        dy = diff * (1.0 / D)
        dx, dg = _rms_bwd(x_t, gain, dy)
        dx_ref[...] = dx
        dxb_ref[...] = dx.astype(BF16)
        dg = jnp.sum(dg, axis=0, keepdims=True)
        part = 0.5 * jnp.sum(jnp.sum(diff * diff, axis=1, keepdims=True), axis=0, keepdims=True) * (1.0 / D)
        part = jnp.broadcast_to(part, (1, _LANES))

        @pl.when(i == 0)
        def _():
            dg_ref[...] = dg
            loss_ref[...] = part

        @pl.when(i != 0)
        def _():
            dg_ref[...] += dg
            loss_ref[...] += part

    row = lambda i: (i, 0)
    fixed = lambda i: (0, 0)
    return pl.pallas_call(
        body, name="loss_head", grid=(S // tm,),
        in_specs=[pl.BlockSpec((tm, D), row), pl.BlockSpec((tm, D), row), pl.BlockSpec((1, D), fixed)],
        out_specs=[pl.BlockSpec((tm, D), row), pl.BlockSpec((tm, D), row), pl.BlockSpec((1, D), fixed),
                   pl.BlockSpec((1, _LANES), fixed)],
        out_shape=[jax.ShapeDtypeStruct((S, D), F32), jax.ShapeDtypeStruct((S, D), BF16),
                   jax.ShapeDtypeStruct((1, D), F32), jax.ShapeDtypeStruct((1, _LANES), F32)],
        compiler_params=_params(("arbitrary",)),
    )(x, target, g)


def _rope_tables_call(pos_col, invf, mask_a, mask_b):
    S = pos_col.shape[0]
    tm = _pick(S, _ROW_TILE, 8)

    def body(p_ref, f_ref, a_ref, b_ref, cos_ref, sa_ref, sb_ref):
        ang = p_ref[...].astype(F32) * f_ref[...]
        sn = jnp.sin(ang)
        cos_ref[...] = jnp.cos(ang)
        sa_ref[...] = sn * a_ref[...]
        sb_ref[...] = sn * b_ref[...]

    row = lambda i: (i, 0)
    fixed = lambda i: (0, 0)
    return pl.pallas_call(
        body, name="rope_tables", grid=(S // tm,),
        in_specs=[pl.BlockSpec((tm, 1), row)] + [pl.BlockSpec((1, HP), fixed)] * 3,
        out_specs=[pl.BlockSpec((tm, HP), row)] * 3,
        out_shape=[jax.ShapeDtypeStruct((S, HP), F32)] * 3, compiler_params=_params(("arbitrary",)),
    )(pos_col, invf, mask_a, mask_b)


def _rope(t, cos, sa, sb):
    return t * cos + pltpu.roll(t, HP - ROPE // 2, 1) * sa + pltpu.roll(t, ROPE // 2, 1) * sb


def _rope_t(d, cos, sa, sb):
    return d * cos + pltpu.roll(d * sa, ROPE // 2, 1) + pltpu.roll(d * sb, HP - ROPE // 2, 1)


def _gelu(z):
    return 0.5 * z * (1.0 + lax.erf(z * np.float32(1.0 / np.sqrt(2.0))))


def _gelu_grad(z):
    cdf = 0.5 * (1.0 + lax.erf(z * np.float32(1.0 / np.sqrt(2.0))))
    pdf = np.float32(1.0 / np.sqrt(2.0 * np.pi)) * jnp.exp(-0.5 * z * z)
    return cdf + z * pdf


_CQ0, _CKV0, _KR0, _Z0 = 0, Q_LORA, Q_LORA + KV_LORA, Q_LORA + KV_LORA + HP


def _even_prep_call(proj, qn, kvn, sgn, cos, sa, sb):
    S = proj.shape[0]
    tm = _pick(S, 256, 8)

    def body(p_ref, qn_ref, kvn_ref, sgn_ref, cos_ref, sa_ref, sb_ref, cq_ref, ckv_ref, kr_ref, u_ref, v_ref):
        cq = p_ref[:, _CQ0:_CQ0 + Q_LORA]
        cq_ref[...] = (_rms_stats(cq)[0] * qn_ref[...]).astype(BF16)
        ckv = p_ref[:, _CKV0:_CKV0 + KV_LORA]
        ckv_ref[...] = (_rms_stats(ckv)[0] * kvn_ref[...]).astype(BF16)
        kr = p_ref[:, _KR0:_KR0 + HP]
        kr_ref[...] = _rope(kr, cos_ref[...], sa_ref[...], sb_ref[...]).astype(BF16)
        u_ref[...] = _gelu(p_ref[:, _Z0:_Z0 + SG_WIDTH]).astype(BF16)
        zv = _gelu(p_ref[:, _Z0 + SG_WIDTH:_Z0 + 2 * SG_WIDTH])
        v_ref[...] = (_rms_stats(zv)[0] * sgn_ref[...]).astype(BF16)

    row = lambda i: (i, 0)
    fixed = lambda i: (0, 0)
    widths = [Q_LORA, KV_LORA, HP, SG_WIDTH, SG_WIDTH]
    return pl.pallas_call(
        body, name="even_prep", grid=(S // tm,),
        in_specs=[pl.BlockSpec((tm, PROJ_W), row), pl.BlockSpec((1, Q_LORA), fixed),
                  pl.BlockSpec((1, KV_LORA), fixed), pl.BlockSpec((1, SG_WIDTH), fixed)]
        + [pl.BlockSpec((tm, HP), row)] * 3,
        out_specs=[pl.BlockSpec((tm, w), row) for w in widths],
        out_shape=[jax.ShapeDtypeStruct((S, w), BF16) for w in widths],
        compiler_params=_params(("arbitrary",)),
    )(proj, qn, kvn, sgn, cos, sa, sb)


def _even_prep_bwd_call(proj, qn, kvn, sgn, cos, sa, sb, dcqn, dckvn, dk, du, dvn):
    S = proj.shape[0]
    tm = _pick(S, 256, 8)

    def body(p_ref, qn_ref, kvn_ref, sgn_ref, cos_ref, sa_ref, sb_ref, dcq_ref, dckv_ref, dk_ref, du_ref,
             dvn_ref, dp_ref, dqn_ref, dkvn_ref, dsgn_ref):
        i = pl.program_id(0)
        dcq, gq = _rms_bwd(p_ref[:, _CQ0:_CQ0 + Q_LORA], qn_ref[...], dcq_ref[...])
        dp_ref[:, _CQ0:_CQ0 + Q_LORA] = dcq.astype(BF16)
        dckv, gkv = _rms_bwd(p_ref[:, _CKV0:_CKV0 + KV_LORA], kvn_ref[...], dckv_ref[...])
        dp_ref[:, _CKV0:_CKV0 + KV_LORA] = dckv.astype(BF16)
        dkr = dk_ref[:, 0:HP].astype(F32)
        for h in range(1, HEADS):
            dkr = dkr + dk_ref[:, h * HP:(h + 1) * HP].astype(F32)
        lane = lax.broadcasted_iota(jnp.int32, dkr.shape, 1)
        dkr = jnp.where((lane >= NOPE) & (lane < QK_DIM), dkr, 0.0)
        dp_ref[:, _KR0:_KR0 + HP] = _rope_t(dkr, cos_ref[...], sa_ref[...], sb_ref[...]).astype(BF16)
        zu = p_ref[:, _Z0:_Z0 + SG_WIDTH]
        dp_ref[:, _Z0:_Z0 + SG_WIDTH] = (du_ref[...].astype(F32) * _gelu_grad(zu)).astype(BF16)
        zv = p_ref[:, _Z0 + SG_WIDTH:_Z0 + 2 * SG_WIDTH]
        dgv, gsg = _rms_bwd(_gelu(zv), sgn_ref[...], dvn_ref[...].astype(F32))
        dp_ref[:, _Z0 + SG_WIDTH:_Z0 + 2 * SG_WIDTH] = (dgv * _gelu_grad(zv)).astype(BF16)
        sums = [jnp.sum(t, axis=0, keepdims=True) for t in (gq, gkv, gsg)]

        @pl.when(i == 0)
        def _():
            for r, s in zip((dqn_ref, dkvn_ref, dsgn_ref), sums):
                r[...] = s

        @pl.when(i != 0)
        def _():
            for r, s in zip((dqn_ref, dkvn_ref, dsgn_ref), sums):
                r[...] += s

    row = lambda i: (i, 0)
    fixed = lambda i: (0, 0)
    return pl.pallas_call(
        body, name="even_prep_bwd", grid=(S // tm,),
        in_specs=[pl.BlockSpec((tm, PROJ_W), row), pl.BlockSpec((1, Q_LORA), fixed),
                  pl.BlockSpec((1, KV_LORA), fixed), pl.BlockSpec((1, SG_WIDTH), fixed)]
        + [pl.BlockSpec((tm, HP), row)] * 3
        + [pl.BlockSpec((tm, Q_LORA), row), pl.BlockSpec((tm, KV_LORA), row),
           pl.BlockSpec((tm, HEADS * HP), row), pl.BlockSpec((tm, SG_WIDTH), row),
           pl.BlockSpec((tm, SG_WIDTH), row)],
        out_specs=[pl.BlockSpec((tm, PROJ_W), row), pl.BlockSpec((1, Q_LORA), fixed),
                   pl.BlockSpec((1, KV_LORA), fixed), pl.BlockSpec((1, SG_WIDTH), fixed)],
        out_shape=[jax.ShapeDtypeStruct((S, PROJ_W), BF16), jax.ShapeDtypeStruct((1, Q_LORA), F32),
                   jax.ShapeDtypeStruct((1, KV_LORA), F32), jax.ShapeDtypeStruct((1, SG_WIDTH), F32)],
        compiler_params=_params(("arbitrary",)),
    )(proj, qn, kvn, sgn, cos, sa, sb, dcqn, dckvn, dk, du, dvn)


def _causal_mask(rows, cols):
    r = lax.broadcasted_iota(jnp.int32, (rows, cols), 0)
    c = lax.broadcasted_iota(jnp.int32, (rows, cols), 1)
    return c <= r


def _flash_fwd_call(q, k, v, carry=()):
    S = q.shape[0]
    tb = _pick(S, _ATT_BLOCK, 128)
    nq = S // tb
    nt_dims = (((1,), (1,)), ((), ()))

    nc = len(carry)

    def body(*refs):
        q_ref, k_ref, v_ref = refs[:3]
        o_ref, lse_ref = refs[3 + nc:5 + nc]
        s_a, s_b, m_ref, acc_ref = refs[5 + 2 * nc:9 + 2 * nc]
        h = pl.program_id(0)
        i = pl.program_id(1)
        if nc:
            send, forward, finish = _gather_phases(refs[3:3 + nc], refs[5 + nc:5 + 2 * nc], *refs[9 + 2 * nc:])
            pl.when((h == 0) & (i == 0))(send)
            pl.when((h == HEADS // 2) & (i == 0))(forward)

        def scores(buf, j):
            k_t = k_ref[pl.ds(pl.multiple_of(j * tb, tb), tb), :]
            buf[...] = lax.dot_general(q_ref[...], k_t, nt_dims, preferred_element_type=F32)

        def update(buf, j, masked):
            v_t = v_ref[pl.ds(pl.multiple_of(j * tb, tb), tb), :]
            s = buf[...]
            if masked:
                s = jnp.where(_causal_mask(tb, tb), s, -1e30)
            m = m_ref[...]
            m_new = jnp.maximum(m, jnp.max(s, axis=1, keepdims=True))
            alpha = jnp.exp(m - m_new)
            p = jnp.exp(s - m_new)
            acc_ref[...] = alpha * acc_ref[...] + jnp.dot(p.astype(v_t.dtype), v_t, preferred_element_type=F32)
            m_ref[...] = m_new

        m_ref[...] = jnp.full((tb, 1), -1e30, F32)
        acc_ref[...] = jnp.zeros((tb, HP), F32)
        scores(s_a, 0)
        pairs = i // 2

        def two_blocks(t, carry):
            scores(s_b, 2 * t + 1)
            update(s_a, 2 * t, False)
            scores(s_a, 2 * t + 2)
            update(s_b, 2 * t + 1, False)
            return carry

        lax.fori_loop(0, pairs, two_blocks, 0)

        @pl.when(2 * pairs == i)
        def _():
            update(s_a, i, True)

        @pl.when(2 * pairs != i)
        def _():
            scores(s_b, i)
            update(s_a, i - 1, False)
            update(s_b, i, True)

        acc = acc_ref[...]
        l = acc[:, VDIM:VDIM + 1]
        lane = lax.broadcasted_iota(jnp.int32, (tb, HP), 1)
        o_ref[...] = jnp.where(lane < VDIM, acc / l, 0.0).astype(o_ref.dtype)
        lse = jnp.broadcast_to(m_ref[...] + jnp.log(l), (tb, HP))
        lse_ref[0, 0] = jnp.transpose(lse)[0:8, :]
        if nc:
            pl.when((h == HEADS - 1) & (i == nq - 1))(finish)

    start = _gather_start(carry)
    any_spec = pl.BlockSpec(memory_space=pl.ANY)
    res = pl.pallas_call(
        body, name="flash_fwd_gather" if nc else "flash_fwd", grid=(HEADS, nq),
        in_specs=[pl.BlockSpec((tb, HP), lambda h, i: (i, h)), pl.BlockSpec((S, HP), lambda h, i: (0, h)),
                  pl.BlockSpec((S, HP), lambda h, i: (0, h))] + [any_spec] * nc,
        out_specs=[pl.BlockSpec((tb, HP), lambda h, i: (i, h)),
                   pl.BlockSpec((1, 1, 8, tb), lambda h, i: (h, i, 0, 0))] + [any_spec] * nc,
        out_shape=[jax.ShapeDtypeStruct((S, HEADS * HP), q.dtype), jax.ShapeDtypeStruct((HEADS, nq, 8, tb), F32)]
        + [jax.ShapeDtypeStruct(t.shape, t.dtype) for t in start],
        scratch_shapes=[pltpu.VMEM((tb, tb), F32), pltpu.VMEM((tb, tb), F32), pltpu.VMEM((tb, 1), F32),
                        pltpu.VMEM((tb, HP), F32)] + ([pltpu.SemaphoreType.DMA((n,)) for n in _gather_sems(nc)]
                                                      if nc else []),
        input_output_aliases={3 + a: 2 + a for a in range(nc)},
        compiler_params=pltpu.CompilerParams(vmem_limit_bytes=_VMEM_LIMIT, has_side_effects=bool(nc),
                                             dimension_semantics=("arbitrary", "arbitrary")),
    )(q, k, v, *start)
    return res[0], res[1], list(res[2:])


def _attn_delta_call(o, do):
    S = o.shape[0]
    tb = _pick(S, _ATT_BLOCK, 128)
    nq = S // tb
    nb = _pick(nq, 4, 1)

    def body(o_ref, do_ref, d_ref):
        for r in range(nb):
            rows = slice(r * tb, (r + 1) * tb)
            d = jnp.sum(o_ref[rows, :].astype(F32) * do_ref[rows, :].astype(F32), axis=1, keepdims=True)
            d_ref[0, r] = jnp.transpose(jnp.broadcast_to(d, (tb, HP)))[0:8, :]

    return pl.pallas_call(
        body, name="attn_delta", grid=(HEADS, nq // nb),
        in_specs=[pl.BlockSpec((nb * tb, HP), lambda h, i: (i, h))] * 2,
        out_specs=pl.BlockSpec((1, nb, 8, tb), lambda h, i: (h, i, 0, 0)),
        out_shape=jax.ShapeDtypeStruct((HEADS, nq, 8, tb), F32), compiler_params=_params(("arbitrary", "arbitrary")),
    )(o, do)


def _flash_bwd_call(q, k, v, do, lse, delta):
    S = q.shape[0]
    tb = _pick(S, _ATT_BLOCK, 128)
    nq = S // tb
    nt_dims = (((1,), (1,)), ((), ()))
    tn_dims = (((0,), (0,)), ((), ()))

    def body(q_ref, do_ref, lse_ref, dl_ref, k_ref, v_ref, dq_ref, dk_ref, dv_ref, st_a, dp_a, st_b, dp_b, dk_acc,
             dv_acc):
        j = pl.program_id(1)

        @pl.when(j == 0)
        def _():
            dq_ref[...] = jnp.zeros_like(dq_ref)

        def rows_of(i):
            return pl.ds(pl.multiple_of(i * tb, tb), tb)

        def scores(st_buf, dp_buf, i):
            st_buf[...] = lax.dot_general(k_ref[...], q_ref[rows_of(i), :], nt_dims, preferred_element_type=F32)
            dp_buf[...] = lax.dot_general(v_ref[...], do_ref[rows_of(i), :], nt_dims, preferred_element_type=F32)

        def update(st_buf, dp_buf, i, masked):
            q_t = q_ref[rows_of(i), :]
            do_t = do_ref[rows_of(i), :]
            pt = jnp.exp(st_buf[...] - lse_ref[0, i, 0:1, :])
            if masked:
                pt = jnp.where(jnp.transpose(_causal_mask(tb, tb)), pt, 0.0)
            dst = (pt * (dp_buf[...] - dl_ref[0, i, 0:1, :])).astype(q_t.dtype)
            dv_acc[...] += jnp.dot(pt.astype(do_t.dtype), do_t, preferred_element_type=F32)
            dk_acc[...] += jnp.dot(dst, q_t, preferred_element_type=F32)
            dq_ref[rows_of(i), :] += lax.dot_general(dst, k_ref[...], tn_dims, preferred_element_type=F32)

        last = nq - 1
        dk_acc[...] = jnp.zeros((tb, HP), F32)
        dv_acc[...] = jnp.zeros((tb, HP), F32)
        scores(st_b, dp_b, j)
        scores(st_a, dp_a, jnp.minimum(j + 1, last))
        update(st_b, dp_b, j, True)
        rest = last - j
        pairs = rest // 2

        def two_blocks(t, carry):
            i0 = j + 1 + 2 * t
            scores(st_b, dp_b, i0 + 1)
            update(st_a, dp_a, i0, False)
            scores(st_a, dp_a, jnp.minimum(i0 + 2, last))
            update(st_b, dp_b, i0 + 1, False)
            return carry

        lax.fori_loop(0, pairs, two_blocks, 0)

        @pl.when(2 * pairs != rest)
        def _():
            update(st_a, dp_a, last, False)

        dk_ref[...] = dk_acc[...].astype(dk_ref.dtype)
        dv_ref[...] = dv_acc[...].astype(dv_ref.dtype)

    head = lambda h, j: (0, h)
    blk = lambda h, j: (j, h)
    rows = lambda h, j: (h, 0, 0, 0)
    return pl.pallas_call(
        body, name="flash_bwd", grid=(HEADS, nq),
        in_specs=[pl.BlockSpec((S, HP), head), pl.BlockSpec((S, HP), head), pl.BlockSpec((1, nq, 8, tb), rows),
                  pl.BlockSpec((1, nq, 8, tb), rows), pl.BlockSpec((tb, HP), blk), pl.BlockSpec((tb, HP), blk)],
        out_specs=[pl.BlockSpec((S, HP), head), pl.BlockSpec((tb, HP), blk), pl.BlockSpec((tb, HP), blk)],
        out_shape=[jax.ShapeDtypeStruct((S, HEADS * HP), F32), jax.ShapeDtypeStruct((S, HEADS * HP), BF16),
                   jax.ShapeDtypeStruct((S, HEADS * HP), BF16)],
        scratch_shapes=[pltpu.VMEM((tb, tb), F32)] * 4 + [pltpu.VMEM((tb, HP), F32)] * 2,
        compiler_params=_params(("arbitrary", "arbitrary")),
    )(q, do, lse, delta, k, v)


def _sg_mixed(w_ref, vch, lane_lo):
    blocks = []
    for jb in range(SG_WIDTH // _LANES):
        r = jnp.dot(w_ref[jb], vch[:, jb * _LANES:(jb + 1) * _LANES], preferred_element_type=F32)
        blocks.append(jnp.where(lane_lo, r[0:SG_CHUNK], r[SG_CHUNK:2 * SG_CHUNK]))
    return jnp.concatenate(blocks, axis=1)


def _sgu_fwd_call(vn, u, attn, wst, bexp):
    S = vn.shape[0]
    tm = _pick(S, _SG_TILE, SG_CHUNK)
    AW = HEADS * HP

    def body(v_ref, u_ref, a_ref, w_ref, b_ref, mix_ref):
        lane_lo = lax.broadcasted_iota(jnp.int32, (SG_CHUNK, _LANES), 1) < SG_GDIM
        mix_ref[:, 0:AW] = a_ref[...]
        for c in range(tm // SG_CHUNK):
            rs = slice(c * SG_CHUNK, (c + 1) * SG_CHUNK)
            mixed = _sg_mixed(w_ref, v_ref[rs, :], lane_lo) + b_ref[...]
            mix_ref[rs, AW:AW + SG_WIDTH] = (u_ref[rs, :].astype(F32) * mixed).astype(mix_ref.dtype)

    row = lambda i: (i, 0)
    return pl.pallas_call(
        body, name="sgu_fwd", grid=(S // tm,),
        in_specs=[pl.BlockSpec((tm, SG_WIDTH), row), pl.BlockSpec((tm, SG_WIDTH), row), pl.BlockSpec((tm, AW), row),
                  pl.BlockSpec((SG_WIDTH // _LANES, 2 * SG_CHUNK, SG_CHUNK), lambda i: (0, 0, 0)),
                  pl.BlockSpec((SG_CHUNK, SG_WIDTH), lambda i: (0, 0))],
        out_specs=pl.BlockSpec((tm, AW + SG_WIDTH), row),
        out_shape=jax.ShapeDtypeStruct((S, AW + SG_WIDTH), BF16), compiler_params=_params(("arbitrary",)),
    )(vn, u, attn, wst, bexp)


def _sgu_bwd_call(dmix, vn, u, wst, wst_t, bexp):
    S = vn.shape[0]
    tm = _pick(S, _SG_TILE, SG_CHUNK)
    nblk = SG_WIDTH // _LANES
    col0 = (HEADS * HP) // SG_WIDTH
    nt_dims = (((1,), (1,)), ((), ()))

    def body(d_ref, v_ref, u_ref, w_ref, wt_ref, b_ref, du_ref, dv_ref, dw_ref, db_ref, dbacc_ref):
        i = pl.program_id(0)
        lane_lo = lax.broadcasted_iota(jnp.int32, (SG_CHUNK, _LANES), 1) < SG_GDIM

        @pl.when(i == 0)
        def _():
            dw_ref[...] = jnp.zeros_like(dw_ref)
            dbacc_ref[...] = jnp.zeros_like(dbacc_ref)

        for c in range(tm // SG_CHUNK):
            rs = slice(c * SG_CHUNK, (c + 1) * SG_CHUNK)
            vch = v_ref[rs, :]
            dsg = d_ref[rs, :].astype(F32)
            mixed = _sg_mixed(w_ref, vch, lane_lo) + b_ref[...]
            du_ref[rs, :] = (dsg * mixed).astype(du_ref.dtype)
            dmixed = dsg * u_ref[rs, :].astype(F32)
            dbacc_ref[...] += dmixed
            dmx = dmixed.astype(vch.dtype)
            dv_ref[rs, :] = _sg_mixed(wt_ref, dmx, lane_lo).astype(dv_ref.dtype)
            for jb in range(nblk):
                dblk = dmx[:, jb * _LANES:(jb + 1) * _LANES]
                vblk = vch[:, jb * _LANES:(jb + 1) * _LANES]
                zero = jnp.zeros_like(dblk)
                dw_ref[2 * jb] += lax.dot_general(jnp.where(lane_lo, dblk, zero), vblk, nt_dims,
                                                  preferred_element_type=F32)
                dw_ref[2 * jb + 1] += lax.dot_general(jnp.where(lane_lo, zero, dblk), vblk, nt_dims,
                                                      preferred_element_type=F32)

        @pl.when(i == pl.num_programs(0) - 1)
        def _():
            tri = _causal_mask(SG_CHUNK, SG_CHUNK)
            for g in range(SG_GROUPS):
                dw_ref[g] = jnp.where(tri, dw_ref[g], 0.0)
            lane = lax.broadcasted_iota(jnp.int32, (SG_CHUNK, _LANES), 1)
            out = jnp.zeros((SG_CHUNK, _LANES), F32)
            for g in range(SG_GROUPS):
                blk = dbacc_ref[:, (g // 2) * _LANES:(g // 2 + 1) * _LANES]
                sel = lane_lo if g % 2 == 0 else jnp.logical_not(lane_lo)
                s = jnp.sum(jnp.where(sel, blk, 0.0), axis=1, keepdims=True)
                out = jnp.where(lane == g, s, out)
            db_ref[...] = out

    row = lambda i: (i, 0)
    wspec = pl.BlockSpec((nblk, 2 * SG_CHUNK, SG_CHUNK), lambda i: (0, 0, 0))
    return pl.pallas_call(
        body, name="sgu_bwd", grid=(S // tm,),
        in_specs=[pl.BlockSpec((tm, SG_WIDTH), lambda i: (i, col0)), pl.BlockSpec((tm, SG_WIDTH), row),
                  pl.BlockSpec((tm, SG_WIDTH), row), wspec, wspec,
                  pl.BlockSpec((SG_CHUNK, SG_WIDTH), lambda i: (0, 0))],
        out_specs=[pl.BlockSpec((tm, SG_WIDTH), row), pl.BlockSpec((tm, SG_WIDTH), row),
                   pl.BlockSpec((SG_GROUPS, SG_CHUNK, SG_CHUNK), lambda i: (0, 0, 0)),
                   pl.BlockSpec((SG_CHUNK, _LANES), lambda i: (0, 0))],
        out_shape=[jax.ShapeDtypeStruct((S, SG_WIDTH), BF16), jax.ShapeDtypeStruct((S, SG_WIDTH), BF16),
                   jax.ShapeDtypeStruct((SG_GROUPS, SG_CHUNK, SG_CHUNK), F32),
                   jax.ShapeDtypeStruct((SG_CHUNK, _LANES), F32)],
        scratch_shapes=[pltpu.VMEM((SG_CHUNK, SG_WIDTH), F32)],
        compiler_params=_params(("arbitrary",)),
    )(dmix, vn, u, wst, wst_t, bexp)


def _shift_down(t, halo, n):
    rows = lax.broadcasted_iota(jnp.int32, t.shape, 0)
    out = pltpu.roll(t, n, 0)
    for r in range(n):
        out = jnp.where(rows == r, halo[8 - n + r:8 - n + r + 1, :], out)
    return out


def _shift_up(t, halo, n):
    tm = t.shape[0]
    rows = lax.broadcasted_iota(jnp.int32, t.shape, 0)
    out = pltpu.roll(t, tm - n, 0)
    for r in range(n):
        out = jnp.where(rows == tm - n + r, halo[r:r + 1, :], out)
    return out


def _conv_fwd_call(p, w):
    S, C3 = p.shape
    C = C3 // 3
    tm = _pick(S, _ROW_TILE, 8)
    hb = tm // 8

    def body(p_ref, c_prev, z_prev, w_ref, m_ref):
        i = pl.program_id(0)
        cz = p_ref[:, C:2 * C] * p_ref[:, 2 * C:3 * C]
        czp = jnp.where(i > 0, c_prev[...] * z_prev[...], 0.0)
        y = w_ref[2:3, :] * cz + w_ref[1:2, :] * _shift_down(cz, czp, 1) + w_ref[0:1, :] * _shift_down(cz, czp, 2)
        m_ref[...] = (p_ref[:, 0:C] * y).astype(m_ref.dtype)

    prev = lambda col: (lambda i: (jnp.maximum(i * hb - 1, 0), col))
    return pl.pallas_call(
        body, name="conv_fwd", grid=(S // tm,),
        in_specs=[pl.BlockSpec((tm, C3), lambda i: (i, 0)), pl.BlockSpec((8, C), prev(1)),
                  pl.BlockSpec((8, C), prev(2)), pl.BlockSpec((3, C), lambda i: (0, 0))],
        out_specs=pl.BlockSpec((tm, C), lambda i: (i, 0)),
        out_shape=jax.ShapeDtypeStruct((S, C), BF16), compiler_params=_params(("arbitrary",)),
    )(p, p, p, w)


def _conv_bwd_call(p, w, dm):
    S, C3 = p.shape
    C = C3 // 3
    tm = _pick(S, 256, 8)
    hb = tm // 8
    n_tiles = S // tm

    def body(p_ref, c_prev, z_prev, b_next, dm_ref, dm_next, w_ref, dp_ref, dw_ref):
        i = pl.program_id(0)
        b = p_ref[:, 0:C]
        c = p_ref[:, C:2 * C]
        z = p_ref[:, 2 * C:3 * C]
        cz = c * z
        czp = jnp.where(i > 0, c_prev[...] * z_prev[...], 0.0)
        s1 = _shift_down(cz, czp, 1)
        s2 = _shift_down(cz, czp, 2)
        w0, w1, w2 = w_ref[0:1, :], w_ref[1:2, :], w_ref[2:3, :]
        y = w2 * cz + w1 * s1 + w0 * s2
        dm_t = dm_ref[...]
        dy = dm_t * b
        dyn = jnp.where(i < n_tiles - 1, dm_next[...] * b_next[...], 0.0)
        dcz = w2 * dy + w1 * _shift_up(dy, dyn, 1) + w0 * _shift_up(dy, dyn, 2)
        dp_ref[:, 0:C] = (dm_t * y).astype(dp_ref.dtype)
        dp_ref[:, C:2 * C] = (dcz * z).astype(dp_ref.dtype)
        dp_ref[:, 2 * C:3 * C] = (dcz * c).astype(dp_ref.dtype)
        dw = jnp.concatenate([jnp.sum(dy * s2, axis=0, keepdims=True), jnp.sum(dy * s1, axis=0, keepdims=True),
                              jnp.sum(dy * cz, axis=0, keepdims=True)], axis=0)

        @pl.when(i == 0)
        def _():
            dw_ref[...] = dw

        @pl.when(i != 0)
        def _():
            dw_ref[...] += dw

    prev = lambda col: (lambda i: (jnp.maximum(i * hb - 1, 0), col))
    nxt = lambda col: (lambda i: (jnp.minimum((i + 1) * hb, S // 8 - 1), col))
    return pl.pallas_call(
        body, name="conv_bwd", grid=(n_tiles,),
        in_specs=[pl.BlockSpec((tm, C3), lambda i: (i, 0)), pl.BlockSpec((8, C), prev(1)),
                  pl.BlockSpec((8, C), prev(2)), pl.BlockSpec((8, C), nxt(0)),
                  pl.BlockSpec((tm, C), lambda i: (i, 0)), pl.BlockSpec((8, C), nxt(0)),
                  pl.BlockSpec((3, C), lambda i: (0, 0))],
        out_specs=[pl.BlockSpec((tm, C3), lambda i: (i, 0)), pl.BlockSpec((3, C), lambda i: (0, 0))],
        out_shape=[jax.ShapeDtypeStruct((S, C3), BF16), jax.ShapeDtypeStruct((3, C), F32)],
        compiler_params=_params(("arbitrary",)),
    )(p, p, p, p, dm, dm, w)


def _my_place():
    return lax.axis_index("x"), lax.axis_index("y"), lax.axis_index("c")


def _gather_weights_call(name, shard):
    R, C = shard.shape

    def body(s_ref, o_ref, send_sems, recv_sems, local_sem):
        x, y, c = _my_place()
        mine = 2 * x + y
        local = pltpu.make_async_copy(s_ref, o_ref.at[mine], local_sem)
        local.start()
        peers = [(1 - x, y), (x, 1 - y), (1 - x, 1 - y)]
        copies = []
        for k, (px, py) in enumerate(peers):
            cp = pltpu.make_async_remote_copy(src_ref=s_ref, dst_ref=o_ref.at[mine], send_sem=send_sems.at[k],
                                              recv_sem=recv_sems.at[k], device_id=(px, py, c), device_id_type=MESH)
            cp.start()
            copies.append(cp)
        for k, (px, py) in enumerate(peers):
            pltpu.make_async_remote_copy(src_ref=s_ref, dst_ref=o_ref.at[2 * px + py], send_sem=send_sems.at[k],
                                         recv_sem=recv_sems.at[k], device_id=(px, py, c),
                                         device_id_type=MESH).wait_recv()
        for cp in copies:
            cp.wait_send()
        local.wait()

    any_spec = pl.BlockSpec(memory_space=pl.ANY)
    return pl.pallas_call(
        body, name=name, in_specs=[any_spec], out_specs=any_spec,
        out_shape=jax.ShapeDtypeStruct((4, R, C), shard.dtype),
        scratch_shapes=[pltpu.SemaphoreType.DMA((3,)), pltpu.SemaphoreType.DMA((3,)), pltpu.SemaphoreType.DMA],
        compiler_params=pltpu.CompilerParams(has_side_effects=True),
    )(shard)


_D2D_CHUNKS = 4


_LOCAL_CHUNKS = 8


def _local_copies(src_of, dst_of, rows, sems, base):
    rc = rows // _LOCAL_CHUNKS
    assert rc * _LOCAL_CHUNKS == rows and rc % 16 == 0, rows
    out = []
    for j in range(_LOCAL_CHUNKS):
        sl = pl.ds(j * rc, rc)
        out.append(pltpu.make_async_copy(src_of(sl), dst_of(sl), sems.at[base + j]))
    return out


def _comm_call(name, body, arrays, out_shapes, sem_counts, aliases=None):
    any_spec = pl.BlockSpec(memory_space=pl.ANY)
    return pl.pallas_call(
        body, name=name, in_specs=[any_spec] * len(arrays), out_specs=[any_spec] * len(out_shapes),
        out_shape=out_shapes, scratch_shapes=[pltpu.SemaphoreType.DMA((n,)) for n in sem_counts],
        input_output_aliases=aliases or {}, compiler_params=pltpu.CompilerParams(has_side_effects=True),
    )(*arrays)


def _gather_halves_call(shards):
    na = len(shards)

    def body(*refs):
        send, forward, finish = _gather_phases(refs[:na], refs[na:2 * na], *refs[2 * na:])
        send()
        forward()
        finish()

    start = _gather_start(shards)
    outs = [jax.ShapeDtypeStruct(t.shape, t.dtype) for t in start]
    return _comm_call("gather_weights", body, start, outs, _gather_sems(na), aliases={a: a for a in range(na)})


def _gather_start(shards):
    for s in shards:
        assert s.shape[0] % (2 * _D2D_CHUNKS * 16) == 0, s.shape
    return [jnp.broadcast_to(s[None], (4,) + tuple(s.shape)) for s in shards]


def _gather_sems(na):
    return [3 * na, 3 * na, 3 * na * _D2D_CHUNKS, 3 * na * _D2D_CHUNKS]


def _gather_phases(s_refs, o_refs, ici_send, ici_recv, d2d_send, d2d_recv):
    na = len(s_refs)
    x, y, c = _my_place()
    mine = 2 * x + y
    chips = [(1 - x, y), (x, 1 - y), (1 - x, 1 - y)]

    def ici(a, k, chip, block):
        Rh = s_refs[a].shape[1] // 2
        my_half = pl.ds(pl.multiple_of(c * Rh, 16), Rh)
        return pltpu.make_async_remote_copy(src_ref=s_refs[a].at[mine, my_half], dst_ref=o_refs[a].at[block, my_half],
                                            send_sem=ici_send.at[3 * a + k], recv_sem=ici_recv.at[3 * a + k],
                                            device_id=(chip[0], chip[1], c), device_id_type=MESH)

    def d2d(a, k, j, block, half):
        Rh = s_refs[a].shape[1] // 2
        rc = Rh // _D2D_CHUNKS
        rows = pl.ds(pl.multiple_of(half * Rh + j * rc, 16), rc)
        idx = (3 * a + k) * _D2D_CHUNKS + j
        return pltpu.make_async_remote_copy(src_ref=o_refs[a].at[block, rows], dst_ref=o_refs[a].at[block, rows],
                                            send_sem=d2d_send.at[idx], recv_sem=d2d_recv.at[idx],
                                            device_id=(x, y, 1 - c), device_id_type=MESH)

    def send():
        for a in range(na):
            for k, chip in enumerate(chips):
                ici(a, k, chip, mine).start()

    def forward():
        for a in range(na):
            for k, chip in enumerate(chips):
                block = 2 * chip[0] + chip[1]
                ici(a, k, chip, block).wait_recv()
                for j in range(_D2D_CHUNKS):
                    d2d(a, k, j, block, c).start()

    def finish():
        for a in range(na):
            for k, chip in enumerate(chips):
                block = 2 * chip[0] + chip[1]
                for j in range(_D2D_CHUNKS):
                    d2d(a, k, j, block, 1 - c).wait_recv()
        for a in range(na):
            for k, chip in enumerate(chips):
                ici(a, k, chip, mine).wait_send()
                for j in range(_D2D_CHUNKS):
                    d2d(a, k, j, 2 * chip[0] + chip[1], c).wait_send()

    return send, forward, finish


def _pair_exchange_call(packed):
    na = len(packed)

    def body(*refs):
        p_refs, o_refs = refs[:na], refs[na:2 * na]
        send_sems, recv_sems = refs[2 * na:]
        x, y, c = _my_place()
        copies = []
        for a in range(na):
            nb, _, Rh, _ = p_refs[a].shape
            rc = Rh // _D2D_CHUNKS
            assert rc * _D2D_CHUNKS == Rh and rc % 16 == 0
            for b in range(nb):
                for j in range(_D2D_CHUNKS):
                    rows = pl.ds(j * rc, rc)
                    idx = (a * nb + b) * _D2D_CHUNKS + j
                    copies.append(pltpu.make_async_remote_copy(
                        src_ref=p_refs[a].at[b, 1 - c, rows], dst_ref=o_refs[a].at[b, rows],
                        send_sem=send_sems.at[idx], recv_sem=recv_sems.at[idx],
                        device_id=(x, y, 1 - c), device_id_type=MESH))
        for t in copies:
            t.start()
        for t in copies:
            t.wait_recv()
        for t in copies:
            t.wait_send()

    outs = [jax.ShapeDtypeStruct((p.shape[0], p.shape[2], p.shape[3]), p.dtype) for p in packed]
    n = sum(p.shape[0] for p in packed) * _D2D_CHUNKS
    return _comm_call("pair_exchange", body, packed, outs, [n, n])


def _pair_add_call(name, packed, other, core):
    nb, _, Rh, C = packed.shape
    tr = _pick(Rh, 512, 16)

    def body(c_ref, p_ref, o_ref, q_ref):
        q_ref[...] = (p_ref[...].astype(F32) + o_ref[...].astype(F32)).astype(q_ref.dtype)

    grid_spec = pltpu.PrefetchScalarGridSpec(
        num_scalar_prefetch=1, grid=(nb, Rh // tr),
        in_specs=[pl.BlockSpec((None, None, tr, C), lambda b, r, c_ref: (b, c_ref[0], r, 0)),
                  pl.BlockSpec((None, tr, C), lambda b, r, c_ref: (b, r, 0))],
        out_specs=pl.BlockSpec((None, tr, C), lambda b, r, c_ref: (b, r, 0)))
    return pl.pallas_call(
        body, name=name, grid_spec=grid_spec, out_shape=jax.ShapeDtypeStruct((nb, Rh, C), packed.dtype),
        compiler_params=_params(("arbitrary", "arbitrary")),
    )(core, packed, other)


def _chip_scatter_call(pairs):
    na = len(pairs)

    def body(*refs):
        p_refs, o_refs = refs[:na], refs[na:2 * na]
        send_sems, recv_sems, local_sems = refs[2 * na:]
        x, y, c = _my_place()
        mine = 2 * x + y
        chips = [(1 - x, y), (x, 1 - y), (1 - x, 1 - y)]
        pending = []
        for a in range(na):
            p_ref, o_ref = p_refs[a], o_refs[a]
            pending += _local_copies(lambda sl: p_ref.at[mine, sl], lambda sl: o_ref.at[mine, sl], p_ref.shape[1],
                                     local_sems, a * _LOCAL_CHUNKS)
        for t in pending:
            t.start()
        copies = []
        for a in range(na):
            for k, (px, py) in enumerate(chips):
                t = pltpu.make_async_remote_copy(src_ref=p_refs[a].at[2 * px + py], dst_ref=o_refs[a].at[mine],
                                                 send_sem=send_sems.at[3 * a + k], recv_sem=recv_sems.at[3 * a + k],
                                                 device_id=(px, py, c), device_id_type=MESH)
                t.start()
                copies.append(t)
        for a in range(na):
            for k, (px, py) in enumerate(chips):
                pltpu.make_async_remote_copy(src_ref=p_refs[a].at[mine], dst_ref=o_refs[a].at[2 * px + py],
                                             send_sem=send_sems.at[3 * a + k], recv_sem=recv_sems.at[3 * a + k],
                                             device_id=(px, py, c), device_id_type=MESH).wait_recv()
        for t in copies:
            t.wait_send()
        for t in pending:
            t.wait()

    outs = [jax.ShapeDtypeStruct(p.shape, p.dtype) for p in pairs]
    return _comm_call("chip_scatter", body, pairs, outs, [3 * na, 3 * na, na * _LOCAL_CHUNKS])


def _sum_slots_call(name, parts, core):
    n, R, C = parts.shape
    tr = _pick(R, 256, 8)

    def body(c_ref, p_ref, o_ref):
        acc = p_ref[0].astype(F32)
        for s in range(1, n):
            acc = acc + p_ref[s].astype(F32)
        o_ref[...] = acc

    grid_spec = pltpu.PrefetchScalarGridSpec(
        num_scalar_prefetch=1, grid=(R // tr,),
        in_specs=[pl.BlockSpec((n, tr, C), lambda i, c_ref: (0, i, 0))],
        out_specs=pl.BlockSpec((None, tr, C), lambda i, c_ref: (c_ref[0], i, 0)))
    return pl.pallas_call(
        body, name=name, grid_spec=grid_spec, out_shape=jax.ShapeDtypeStruct((2, R, C), F32),
        compiler_params=_params(("arbitrary",)),
    )(core, parts)


def _sibling_share_call(halves):
    na = len(halves)
    nch = 2 * _D2D_CHUNKS

    def body(*refs):
        h_refs, o_refs = refs[:na], refs[na:2 * na]
        send_sems, recv_sems = refs[2 * na:]
        x, y, c = _my_place()

        def cp(a, j, slot):
            rc = h_refs[a].shape[1] // nch
            rows = pl.ds(j * rc, rc)
            return pltpu.make_async_remote_copy(src_ref=h_refs[a].at[slot, rows], dst_ref=o_refs[a].at[slot, rows],
                                                send_sem=send_sems.at[a * nch + j], recv_sem=recv_sems.at[a * nch + j],
                                                device_id=(x, y, 1 - c), device_id_type=MESH)

        copies = [cp(a, j, c) for a in range(na) for j in range(nch)]
        for t in copies:
            t.start()
        for a in range(na):
            for j in range(nch):
                cp(a, j, 1 - c).wait_recv()
        for t in copies:
            t.wait_send()

    for h in halves:
        assert h.shape[1] % (nch * 8) == 0, h.shape
    outs = [jax.ShapeDtypeStruct(h.shape, h.dtype) for h in halves]
    return _comm_call("sibling_share", body, halves, outs, [na * nch, na * nch], aliases={a: a for a in range(na)})


def _allreduce_small_call(part):
    R, C = part.shape

    def body(p_ref, o_ref, slots, send_sems, recv_sems):
        x, y, c = _my_place()
        me = 4 * x + 2 * y + c
        peers = []
        for k in range(1, 8):
            px = x ^ (k >> 2) if (k >> 2) else x
            py = y ^ ((k >> 1) & 1) if ((k >> 1) & 1) else y
            pc = c ^ (k & 1) if (k & 1) else c
            peers.append((px, py, pc))
        copies = []
        for k, (px, py, pc) in enumerate(peers):
            cp = pltpu.make_async_remote_copy(src_ref=p_ref, dst_ref=slots.at[me], send_sem=send_sems.at[k],
                                              recv_sem=recv_sems.at[k], device_id=(px, py, pc), device_id_type=MESH)
            cp.start()
            copies.append(cp)
        slots[me] = p_ref[...]
        for k, (px, py, pc) in enumerate(peers):
            pltpu.make_async_remote_copy(src_ref=p_ref, dst_ref=slots.at[4 * px + 2 * py + pc],
                                         send_sem=send_sems.at[k], recv_sem=recv_sems.at[k],
                                         device_id=(px, py, pc), device_id_type=MESH).wait_recv()
        for cp in copies:
            cp.wait_send()
        acc = slots[0]
        for s in range(1, 8):
            acc = acc + slots[s]
        o_ref[...] = acc

    vm = pl.BlockSpec(memory_space=pltpu.VMEM)
    return pl.pallas_call(
        body, name="allreduce_small", in_specs=[vm], out_specs=vm,
        out_shape=jax.ShapeDtypeStruct((R, C), F32),
        scratch_shapes=[pltpu.VMEM((8, R, C), F32), pltpu.SemaphoreType.DMA((7,)), pltpu.SemaphoreType.DMA((7,))],
        compiler_params=pltpu.CompilerParams(has_side_effects=True, vmem_limit_bytes=_VMEM_LIMIT),
    )(part)


def _adamw_call(name, w, g, m, v):
    shape = w.shape
    cols = shape[-1]
    rows = int(np.prod(shape[:-1])) if len(shape) > 1 else 1
    w2, g2, m2, v2 = (t.reshape(rows, cols) for t in (w, g, m, v))
    tr = _pick(rows, 256, 8)
    c1 = 1.0 / (1.0 - ADAM_B1 ** ADAM_STEP)
    c2 = 1.0 / (1.0 - ADAM_B2 ** ADAM_STEP)

    def body(w_ref, g_ref, m_ref, v_ref, d_ref, nm_ref, nv_ref):
        gr = g_ref[...]
        m_new = ADAM_B1 * m_ref[...] + (1.0 - ADAM_B1) * gr
        v_new = ADAM_B2 * v_ref[...] + (1.0 - ADAM_B2) * (gr * gr)
        m_hat = m_new / (1.0 - ADAM_B1 ** ADAM_STEP)
        v_hat = v_new / (1.0 - ADAM_B2 ** ADAM_STEP)
        d_ref[...] = -ADAM_LR * (m_hat / (jnp.sqrt(v_hat) + ADAM_EPS) + ADAM_WD * w_ref[...])
        nm_ref[...] = m_new
        nv_ref[...] = v_new

    spec = pl.BlockSpec((tr, cols), lambda i: (i, 0))
    d, nm, nv = pl.pallas_call(
        body, name=name, grid=(rows // tr,), in_specs=[spec] * 4, out_specs=[spec] * 3,
        out_shape=[jax.ShapeDtypeStruct((rows, cols), F32)] * 3, compiler_params=_params(("arbitrary",)),
    )(w2, g2, m2, v2)
    return d.reshape(shape), nm.reshape(shape), nv.reshape(shape)


def _pad_rows(flat, mult):
    n = flat.shape[0]
    unit = PACK_COLS * mult
    total = -(-n // unit) * unit
    return jnp.pad(flat, (0, total - n)).reshape(total // PACK_COLS, PACK_COLS)


def _pad_axis(arr, axis, mult):
    n = arr.shape[axis]
    total = -(-n // mult) * mult
    if total == n:
        return arr
    widths = [(0, 0)] * arr.ndim
    widths[axis] = (0, total - n)
    return jnp.pad(arr, widths)


def _shard_slice(arr, axis, blk, nblk=4):
    w = arr.shape[axis] // nblk
    return lax.slice_in_dim(arr, blk * w, (blk + 1) * w, axis=axis)


_NT = (((1,), (1,)), ((), ()))
_TN = (((0,), (0,)), ((), ()))


def _ffn_fwd(tag, x, h, wg, wu, wd, g_next):
    S, D = x.shape
    (wg, gi), (wu, ui), (wd, di) = wg, wu, wd
    NB, Fb = wg.shape[0], wg.shape[2]
    tm = _pick(S, 1024, 8)

    def gate_up(h_ref, wg_ref, wu_ref, a_ref, u_ref, s_ref):
        h_t = _mx(h_ref[...])
        a = jnp.dot(h_t, _mx(wg_ref[...]), preferred_element_type=F32)
        u = jnp.dot(h_t, _mx(wu_ref[...]), preferred_element_type=F32)
        sig = _sigmoid(a)
        silu = a * sig
        a_ref[...] = (u * (sig * (1.0 + a * (1.0 - sig)))).astype(a_ref.dtype)
        u_ref[...] = silu.astype(u_ref.dtype)
        s_ref[...] = (silu * u).astype(s_ref.dtype)

    hid = pl.BlockSpec((None, tm, Fb), lambda b, i: (b, i, 0))
    a, u, s = pl.pallas_call(
        gate_up, name=tag + "_gate_up", grid=(NB, S // tm),
        in_specs=[pl.BlockSpec((tm, D), lambda b, i: (i, 0)), pl.BlockSpec((None, D, Fb), lambda b, i: (b, gi, 0)),
                  pl.BlockSpec((None, D, Fb), lambda b, i: (b, ui, 0))], out_specs=[hid] * 3,
        out_shape=[jax.ShapeDtypeStruct((NB, S, Fb), BF16)] * 3, compiler_params=_params(("arbitrary", "arbitrary")),
    )(h, wg, wu)

    tm2 = _pick(S, 512, 8)

    def down(s_ref, wd_ref, x_ref, g_ref, xo_ref, ho_ref):
        acc = jnp.dot(_mx(s_ref[0]), _mx(wd_ref[0]), preferred_element_type=F32)
        for b in range(1, NB):
            acc = acc + jnp.dot(_mx(s_ref[b]), _mx(wd_ref[b]), preferred_element_type=F32)
        x_new = x_ref[...] + 0.5 * acc
        xo_ref[...] = x_new
        ho_ref[...] = (_rms_stats(x_new)[0] * g_ref[...]).astype(ho_ref.dtype)

    row = pl.BlockSpec((tm2, D), lambda i: (i, 0))
    x_new, h_next = pl.pallas_call(
        down, name=tag + "_down", grid=(S // tm2,),
        in_specs=[pl.BlockSpec((NB, tm2, Fb), lambda i: (0, i, 0)), pl.BlockSpec((NB, Fb, D), lambda i: (0, di, 0)),
                  row, pl.BlockSpec((1, D), lambda i: (0, 0))],
        out_specs=[row, row], out_shape=[jax.ShapeDtypeStruct((S, D), F32), jax.ShapeDtypeStruct((S, D), BF16)],
        compiler_params=_params(("arbitrary",)),
    )(s, wd, x, g_next)
    return x_new, h_next, (x, h, a, u, s)


def _ffn_bwd(tag, saved, dx_out, dxb, wg, wu, wd, gain):
    x, h, a, u, s = saved
    S, D = x.shape
    (wg, gi), (wu, ui), (wd, di) = wg, wu, wd
    NB, Fb = wg.shape[0], wg.shape[2]
    tm = _pick(S, 1024, 8)
    tk = _pick(S, 1024, 128)
    nk = S // tk

    def dgate_up(d_ref, wd_ref, a_ref, u_ref, da_ref, du_ref):
        ds = 0.5 * lax.dot_general(_mx(d_ref[...]), _mx(wd_ref[...]), _NT, preferred_element_type=F32)
        da_ref[...] = (ds * a_ref[...].astype(F32)).astype(da_ref.dtype)
        du_ref[...] = (ds * u_ref[...].astype(F32)).astype(du_ref.dtype)

    hid = pl.BlockSpec((None, tm, Fb), lambda b, i: (b, i, 0))
    da, du = pl.pallas_call(
        dgate_up, name=tag + "_dgate_up", grid=(NB, S // tm),
        in_specs=[pl.BlockSpec((tm, D), lambda b, i: (i, 0)), pl.BlockSpec((None, Fb, D), lambda b, i: (b, di, 0)),
                  hid, hid],
        out_specs=[hid, hid], out_shape=[jax.ShapeDtypeStruct((NB, S, Fb), BF16)] * 2,
        compiler_params=_params(("arbitrary", "arbitrary")),
    )(dxb, wd, a, u)

    def dw_down(s_ref, d_ref, o_ref, acc_ref):
        k = pl.program_id(1)
        p = lax.dot_general(_mx(s_ref[...]), _mx(d_ref[...]), _TN, preferred_element_type=F32)

        @pl.when(k == 0)
        def _():
            acc_ref[...] = p

        @pl.when(k != 0)
        def _():
            acc_ref[...] += p

        @pl.when(k == nk - 1)
        def _():
            o_ref[...] = (0.5 * acc_ref[...]).astype(o_ref.dtype)

    hk = pl.BlockSpec((None, tk, Fb), lambda b, k: (b, k, 0))
    dwd = pl.pallas_call(
        dw_down, name=tag + "_dw_down", grid=(NB, nk),
        in_specs=[hk, pl.BlockSpec((tk, D), lambda b, k: (k, 0))],
        out_specs=pl.BlockSpec((None, Fb, D), lambda b, k: (b, 0, 0)),
        out_shape=jax.ShapeDtypeStruct((NB, Fb, D), _WIRE_DTYPE), scratch_shapes=[pltpu.VMEM((Fb, D), F32)],
        compiler_params=_params(("arbitrary", "arbitrary")),
    )(s, dxb)

    def dw_gate_up(h_ref, da_ref, du_ref, og_ref, ou_ref, accg_ref, accu_ref):
        k = pl.program_id(1)
        h_t = _mx(h_ref[...])
        pg = lax.dot_general(h_t, _mx(da_ref[...]), _TN, preferred_element_type=F32)
        pu = lax.dot_general(h_t, _mx(du_ref[...]), _TN, preferred_element_type=F32)

        @pl.when(k == 0)
        def _():
            accg_ref[...] = pg
            accu_ref[...] = pu

        @pl.when(k != 0)
        def _():
            accg_ref[...] += pg
            accu_ref[...] += pu

        @pl.when(k == nk - 1)
        def _():
            og_ref[...] = accg_ref[...].astype(og_ref.dtype)
            ou_ref[...] = accu_ref[...].astype(ou_ref.dtype)

    wout = pl.BlockSpec((None, D, Fb), lambda b, k: (b, 0, 0))
    dwg, dwu = pl.pallas_call(
        dw_gate_up, name=tag + "_dw_gate_up", grid=(NB, nk),
        in_specs=[pl.BlockSpec((tk, D), lambda b, k: (k, 0)), hk, hk], out_specs=[wout, wout],
        out_shape=[jax.ShapeDtypeStruct((NB, D, Fb), _WIRE_DTYPE)] * 2,
        scratch_shapes=[pltpu.VMEM((D, Fb), F32)] * 2, compiler_params=_params(("arbitrary", "arbitrary")),
    )(h, da, du)

    tm2 = _pick(S, 512, 8)

    def dx_body(da_ref, du_ref, wg_hbm, wu_hbm, x_ref, dxo_ref, g_ref, dx_ref, dxb_ref, dg_ref, wg_v, wu_v, sem):
        i = pl.program_id(0)

        @pl.when(i == 0)
        def _():
            cg = pltpu.make_async_copy(wg_hbm.at[:, pl.ds(gi * D, D), :], wg_v, sem.at[0])
            cu = pltpu.make_async_copy(wu_hbm.at[:, pl.ds(ui * D, D), :], wu_v, sem.at[1])
            cg.start()
            cu.start()
            cg.wait()
            cu.wait()

        dh = None
        for b in range(NB):
            t = lax.dot_general(_mx(da_ref[b]), wg_v[b], _NT, preferred_element_type=F32)
            t = t + lax.dot_general(_mx(du_ref[b]), wu_v[b], _NT, preferred_element_type=F32)
            dh = t if dh is None else dh + t
        dx_n, dg = _rms_bwd(x_ref[...], g_ref[...], dh)
        dx = dxo_ref[...] + dx_n
        dx_ref[...] = dx
        dxb_ref[...] = dx.astype(dxb_ref.dtype)
        dg = jnp.sum(dg, axis=0, keepdims=True)

        @pl.when(i == 0)
        def _():
            dg_ref[...] = dg

        @pl.when(i != 0)
        def _():
            dg_ref[...] += dg

    row = pl.BlockSpec((tm2, D), lambda i: (i, 0))
    hid2 = pl.BlockSpec((NB, tm2, Fb), lambda i: (0, i, 0))
    anyspec = pl.BlockSpec(memory_space=pl.ANY)
    fixed = pl.BlockSpec((1, D), lambda i: (0, 0))
    dx, dxb_new, dgain = pl.pallas_call(
        dx_body, name=tag + "_dx", grid=(S // tm2,),
        in_specs=[hid2, hid2, anyspec, anyspec, row, row, fixed], out_specs=[row, row, fixed],
        out_shape=[jax.ShapeDtypeStruct((S, D), F32), jax.ShapeDtypeStruct((S, D), BF16),
                   jax.ShapeDtypeStruct((1, D), F32)],
        scratch_shapes=[pltpu.VMEM((NB, D, Fb), wg.dtype), pltpu.VMEM((NB, D, Fb), wu.dtype),
                        pltpu.SemaphoreType.DMA((2,))],
        compiler_params=_params(("arbitrary",)),
    )(da, du, wg, wu, x, dx_out, gain)
    return dx, dxb_new, dgain, dwg, dwu, dwd


def _conv_mixer_fwd(tag, x, h, w_in, w_taps, w_out, g_next):
    S, D = x.shape
    C3 = w_in.shape[1]
    tm = _pick(S, 512, 8)
    p, = _fused_matmul(tag + "_in", 'nn', [_op(h)], [_op(w_in)], [(0, 0, 0)], 1, _ident_epi(), [F32],
                       S, C3, D, tm, _pick(C3, 1024, 128), D)
    m = _conv_fwd_call(p, w_taps)
    x_new, h_next = _fused_matmul(tag + "_out", 'nn', [_op(m)], [_op(w_out)], [(0, 0, 0)], 1, _resid_norm_epi(1.0),
                                  [F32, BF16], S, D, D, tm, D, D, tile_extras=[x], row_extras=[g_next])
    return x_new, h_next, (x, h, p, m)


def _conv_mixer_bwd(tag, saved, dx_out, dxb, w_in, w_taps, w_out, gain):
    x, h, p, m = saved
    S, D = x.shape
    C3 = w_in.shape[1]
    tm = _pick(S, 512, 8)
    tk = _pick(S, 1024, 128)
    dm, = _fused_matmul(tag + "_dm", 'nt', [_op(dxb)], [_op(w_out)], [(0, 0, 0)], 1, _ident_epi(), [F32],
                        S, D, D, tm, D, D)
    dw_out, = _fused_matmul(tag + "_dw_out", 'tn', [_op(m)], [_op(dxb)], [(0, 0, 0)], 1, _ident_epi(), [F32],
                            D, D, S, D, D, tk)
    dp, dtaps = _conv_bwd_call(p, w_taps, dm)
    dw_in, = _fused_matmul(tag + "_dw_in", 'tn', [_op(h)], [_op(dp)], [(0, 0, 0)], 1, _ident_epi(), [F32],
                           D, C3, S, D, _pick(C3, 1024, 128), tk)
    dx, dxb_new, dgain = _fused_matmul(tag + "_dx", 'nt', [_op(dp)], [_op(w_in)], [(0, 0, 0)], 1, _norm_bwd_epi,
                                       [F32, BF16], S, D, C3, tm, D, C3,
                                       tile_extras=[x, dx_out], row_extras=[gain], n_colsum=1)
    return dx, dxb_new, dgain, dw_in, dtaps, dw_out


def _attn_scale():
    return np.float32(QK_DIM ** -0.5)


def _even_mixer_fwd(tag, x, h, wts, tables, g_next, carry=()):
    S, D = x.shape
    cos, sa, sb = tables
    tm = _pick(S, 512, 8)
    AW = HEADS * HP
    proj, = _fused_matmul(tag + "_in", 'nn', [_op(h)], [_op(wts['w_in'])], [(0, 0, 0)], 1, _ident_epi(), [F32],
                          S, PROJ_W, D, tm, _pick(PROJ_W, 896, 128), D)
    cqn, ckvn, kr, u, vn = _even_prep_call(proj, wts['q_norm'], wts['kv_norm'], wts['sg_norm'], cos, sa, sb)
    scale = _attn_scale()

    def q_epi(accs, tiles, rows, mrows):
        c_t, a_t, b_t = mrows
        heads = [_rope(accs[0][:, hh * HP:(hh + 1) * HP], c_t, a_t, b_t) * scale for hh in range(HEADS)]
        return [jnp.concatenate(heads, axis=1)]

    q, = _fused_matmul(tag + "_q", 'nn', [_op(cqn)], [_op(wts['w_q'])], [(0, 0, 0)], 1, q_epi, [BF16],
                       S, AW, Q_LORA, tm, AW, Q_LORA, mrow_extras=[cos, sa, sb])

    def kv_epi(accs, tiles, rows, mrows):
        lane = lax.broadcasted_iota(jnp.int32, accs[1].shape, 1)
        v_t = jnp.where((lane & (HP - 1)) == VDIM, 1.0, accs[1])
        return [accs[0] + jnp.concatenate([mrows[0].astype(F32)] * HEADS, axis=1), v_t]

    k, v = _fused_matmul(tag + "_kv", 'nn', [_op(ckvn)], [_op(wts['w_k']), _op(wts['w_v'])],
                         [(0, 0, 0), (0, 1, 1)], 2, kv_epi, [BF16, BF16], S, AW, KV_LORA, tm, AW, KV_LORA,
                         mrow_extras=[kr])
    o, lse, gathered = _flash_fwd_call(q, k, v, carry)
    mix = _sgu_fwd_call(vn, u, o, wts['sg_wst'], wts['sg_bexp'])
    x_new, h_next = _fused_matmul(tag + "_out", 'nn', [_op(mix)], [_op(wts['w_out'])], [(0, 0, 0)], 1,
                                  _resid_norm_epi(1.0), [F32, BF16], S, D, AW + SG_WIDTH, tm, D, AW + SG_WIDTH,
                                  tile_extras=[x], row_extras=[g_next])
    return x_new, h_next, (x, h, proj, cqn, ckvn, u, vn, q, k, v, o, lse, mix), gathered


def _even_mixer_bwd(tag, saved, dx_out, dxb, wts, tables, gain):
    x, h, proj, cqn, ckvn, u, vn, q, k, v, o, lse, mix = saved
    S, D = x.shape
    cos, sa, sb = tables
    tm = _pick(S, 512, 8)
    tk = _pick(S, 1024, 128)
    AW = HEADS * HP
    MW = AW + SG_WIDTH
    dmix, = _fused_matmul(tag + "_dmix", 'nt', [_op(dxb)], [_op(wts['w_out'])], [(0, 0, 0)], 1, _ident_epi(), [BF16],
                          S, MW, D, tm, _pick(MW, 768, 128), D)
    dw_out, = _fused_matmul(tag + "_dw_out", 'tn', [_op(mix)], [_op(dxb)], [(0, 0, 0)], 1, _ident_epi(), [F32],
                            MW, D, S, _pick(MW, 768, 128), D, tk)
    du, dvn, dsg_w, dsg_b = _sgu_bwd_call(dmix, vn, u, wts['sg_wst'], wts['sg_wst_t'], wts['sg_bexp'])
    delta = _attn_delta_call(o, dmix)
    dq, dk, dv = _flash_bwd_call(q, k, v, dmix, lse, delta)
    scale = _attn_scale()

    def dq_epi(accs, tiles, rows, mrows):
        return accs

    def dq_pre_call():
        tr = _pick(S, 256, 8)

        def body(d_ref, c_ref, a_ref, b_ref, o_ref):
            for hh in range(HEADS):
                t = _rope_t(d_ref[:, hh * HP:(hh + 1) * HP], c_ref[...], a_ref[...], b_ref[...]) * scale
                o_ref[:, hh * HP:(hh + 1) * HP] = t.astype(o_ref.dtype)

        row = lambda i: (i, 0)
        return pl.pallas_call(
            body, name=tag + "_dq_unrope", grid=(S // tr,),
            in_specs=[pl.BlockSpec((tr, AW), row)] + [pl.BlockSpec((tr, HP), row)] * 3,
            out_specs=pl.BlockSpec((tr, AW), row), out_shape=jax.ShapeDtypeStruct((S, AW), BF16),
            compiler_params=_params(("arbitrary",)),
        )(dq, cos, sa, sb)

    dqp = dq_pre_call()
    dw_q, = _fused_matmul(tag + "_dw_q", 'tn', [_op(cqn)], [_op(dqp)], [(0, 0, 0)], 1, _ident_epi(), [F32],
                          Q_LORA, AW, S, Q_LORA, AW, tk)
    dcqn, = _fused_matmul(tag + "_dcq", 'nt', [_op(dqp)], [_op(wts['w_q'])], [(0, 0, 0)], 1, dq_epi, [F32],
                          S, Q_LORA, AW, tm, Q_LORA, AW)
    dw_k, dw_v = _fused_matmul(tag + "_dw_kv", 'tn', [_op(ckvn)], [_op(dk), _op(dv)], [(0, 0, 0), (0, 1, 1)], 2,
                               _ident_epi(), [F32, F32], KV_LORA, AW, S, KV_LORA, AW, tk)
    dckvn, = _fused_matmul(tag + "_dckv", 'nt', [_op(dk), _op(dv)], [_op(wts['w_k']), _op(wts['w_v'])],
                           [(0, 0, 0), (1, 1, 0)], 1, dq_epi, [F32], S, KV_LORA, AW, tm, KV_LORA, AW)
    dproj, dqn, dkvn, dsgn = _even_prep_bwd_call(proj, wts['q_norm'], wts['kv_norm'], wts['sg_norm'], cos, sa, sb,
                                                 dcqn, dckvn, dk, du, dvn)
    dw_in, = _fused_matmul(tag + "_dw_in", 'tn', [_op(h)], [_op(dproj)], [(0, 0, 0)], 1, _ident_epi(), [F32],
                           D, PROJ_W, S, D, _pick(PROJ_W, 896, 128), tk)
    dx, dxb_new, dgain = _fused_matmul(tag + "_dx", 'nt', [_op(dproj)], [_op(wts['w_in'])], [(0, 0, 0)], 1,
                                       _norm_bwd_epi, [F32, BF16], S, D, PROJ_W, tm, D, PROJ_W,
                                       tile_extras=[x, dx_out], row_extras=[gain], n_colsum=1)
    grads = dict(w_in=dw_in, w_q=dw_q, w_k=dw_k, w_v=dw_v, w_out=dw_out, q_norm=dqn, kv_norm=dkvn, sg_norm=dsgn,
                 sg_w=dsg_w, sg_b=dsg_b)
    return dx, dxb_new, dgain, grads


def _even_weights(w_in, w_uq, w_ukv, w_out, q_norm, kv_norm, sg_norm, sg_w, sg_b):
    D = w_in.shape[0]
    kr_cols = jnp.pad(w_in[:, Q_LORA + KV_LORA:Q_LORA + KV_LORA + ROPE], ((0, 0), (NOPE, HP - QK_DIM)))
    w_in_p = jnp.concatenate([w_in[:, :Q_LORA + KV_LORA], kr_cols, w_in[:, Q_LORA + KV_LORA + ROPE:]], axis=1)
    wq = w_uq.reshape(Q_LORA, HEADS, QK_DIM)
    w_q = jnp.pad(wq, ((0, 0), (0, 0), (0, HP - QK_DIM))).reshape(Q_LORA, HEADS * HP)
    wkv = w_ukv.reshape(KV_LORA, HEADS, NOPE + VDIM)
    w_k = jnp.pad(wkv[:, :, :NOPE], ((0, 0), (0, 0), (0, HP - NOPE))).reshape(KV_LORA, HEADS * HP)
    w_v = jnp.pad(wkv[:, :, NOPE:], ((0, 0), (0, 0), (0, HP - VDIM))).reshape(KV_LORA, HEADS * HP)
    wo_a = w_out[:HEADS * VDIM].reshape(HEADS, VDIM, D)
    wo_a = jnp.pad(wo_a, ((0, 0), (0, HP - VDIM), (0, 0))).reshape(HEADS * HP, D)
    w_out_p = jnp.concatenate([wo_a, w_out[HEADS * VDIM:]], axis=0)
    tri = jnp.tril(jnp.ones((SG_CHUNK, SG_CHUNK), F32))
    wm = sg_w * tri
    wst = wm.reshape(SG_GROUPS // 2, 2 * SG_CHUNK, SG_CHUNK).astype(_MXU_DTYPE)
    wst_t = jnp.swapaxes(wm, 1, 2).reshape(SG_GROUPS // 2, 2 * SG_CHUNK, SG_CHUNK).astype(_MXU_DTYPE)
    bexp = jnp.repeat(sg_b.T, SG_GDIM, axis=1)
    return dict(w_in=w_in_p, w_q=w_q, w_k=w_k, w_v=w_v, w_out=w_out_p, sg_wst=wst, sg_wst_t=wst_t, sg_bexp=bexp,
                q_norm=q_norm.reshape(1, -1), kv_norm=kv_norm.reshape(1, -1), sg_norm=sg_norm.reshape(1, -1))


def _even_grads_unpad(g):
    d_in = g['w_in']
    kr0 = Q_LORA + KV_LORA
    dw_in = jnp.concatenate([d_in[:, :kr0], d_in[:, kr0 + NOPE:kr0 + QK_DIM], d_in[:, kr0 + HP:]], axis=1)
    dw_uq = g['w_q'].reshape(Q_LORA, HEADS, HP)[:, :, :QK_DIM].reshape(Q_LORA, HEADS * QK_DIM)
    dk = g['w_k'].reshape(KV_LORA, HEADS, HP)[:, :, :NOPE]
    dv = g['w_v'].reshape(KV_LORA, HEADS, HP)[:, :, :VDIM]
    dw_ukv = jnp.concatenate([dk, dv], axis=2).reshape(KV_LORA, HEADS * (NOPE + VDIM))
    D = d_in.shape[0]
    wo = g['w_out']
    wo_a = wo[:HEADS * HP].reshape(HEADS, HP, D)[:, :VDIM].reshape(HEADS * VDIM, D)
    dw_out = jnp.concatenate([wo_a, wo[HEADS * HP:]], axis=0)
    dsg_b = g['sg_b'][:, :SG_GROUPS].T
    return dict(even_w_in=dw_in, w_uq=dw_uq, w_ukv=dw_ukv, even_w_out=dw_out, q_norm=g['q_norm'][0],
                kv_norm=g['kv_norm'][0], sg_norm=g['sg_norm'][0], sg_w=g['sg_w'], sg_b=dsg_b)


def kernel(x, positions, ffn_pre_norm, ffn_pre_w_gate, ffn_pre_w_up, ffn_pre_w_down, mix_norm, ffn_post_norm, ffn_post_w_gate, ffn_post_w_up, ffn_post_w_down, even_w_in, q_norm, w_uq, kv_norm, w_ukv, sg_norm, sg_w, sg_b, even_w_out, conv_w_in, conv_w, conv_w_out, final_norm, loss_target, m_ffn_pre_norm, m_ffn_pre_w_gate, m_ffn_pre_w_up, m_ffn_pre_w_down, m_mix_norm, m_ffn_post_norm, m_ffn_post_w_gate, m_ffn_post_w_up, m_ffn_post_w_down, m_even_w_in, m_q_norm, m_w_uq, m_kv_norm, m_w_ukv, m_sg_norm, m_sg_w, m_sg_b, m_even_w_out, m_conv_w_in, m_conv_w, m_conv_w_out, m_final_norm, v_ffn_pre_norm, v_ffn_pre_w_gate, v_ffn_pre_w_up, v_ffn_pre_w_down, v_mix_norm, v_ffn_post_norm, v_ffn_post_w_gate, v_ffn_post_w_up, v_ffn_post_w_down, v_even_w_in, v_q_norm, v_w_uq, v_kv_norm, v_w_ukv, v_sg_norm, v_sg_w, v_sg_b, v_even_w_out, v_conv_w_in, v_conv_w, v_conv_w_out, v_final_norm):
    env = dict(locals())
    w_loc = {n: env[n] for n in WEIGHTS}
    m_loc = {n: env['m_' + n] for n in WEIGHTS}
    v_loc = {n: env['v_' + n] for n in WEIGHTS}
    S, D = x.shape[1], x.shape[2]
    depth = ffn_pre_norm.shape[0]
    xs = x.reshape(S, D)
    target = loss_target.reshape(S, D)

    def layers_of(n, early):
        count = w_loc[n].shape[0]
        if n in ('conv_w_in', 'conv_w_out'):
            return [] if early else list(range(count))
        return [0] if early else list(range(1, count))

    def shards_of(early):
        wire = lambda n, l: w_loc[n][l].astype(_WIRE_DTYPE)
        keys = [[(n, l) for n in group for l in layers_of(n, early)] for group in (GROUP_A, GROUP_B, GATHER_C)]
        sa = _pad_axis(jnp.concatenate([wire(n, l) for n, l in keys[0]], axis=0), 0, PACK_ROW_MULT)
        sb = _pad_axis(jnp.concatenate([wire(n, l) for n, l in keys[1]], axis=0), 0, PACK_ROW_MULT)
        sc = _pad_rows(jnp.concatenate([wire(n, l).reshape(-1) for n, l in keys[2]]), PACK_ROW_MULT)
        return [sa, sb, sc], keys

    full = {n: {} for n in SHARDED}

    def unpack(gathered, keys):
        gat_a, gat_b, gat_c = gathered
        for idx, (n, l) in enumerate(keys[0]):
            full[n][l] = (gat_a, idx)
        row = 0
        for n, l in keys[1]:
            rows = w_loc[n].shape[1]
            if n in FFN_WEIGHTS:
                full[n][l] = (gat_b, row // rows)
            else:
                full[n][l] = jnp.concatenate([gat_b[b, row:row + rows] for b in range(4)], axis=0)
            row += rows
        gflat = gat_c.reshape(4, -1)
        off = 0
        for n, l in keys[2]:
            shp = w_loc[n].shape[1:]
            size = int(np.prod(shp))
            full[n][l] = jnp.concatenate([gflat[b, off:off + size].reshape(shp) for b in range(4)],
                                         axis=SHARD_AXIS[n] - 1)
            off += size

    early_shards, early_keys = shards_of(True)
    unpack(_gather_halves_call(early_shards), early_keys)
    late_shards, late_keys = shards_of(False)
    taps = _gather_weights_call("gather_taps", _pad_rows(conv_w.reshape(-1), 8)).reshape(4, -1)
    taps = jnp.concatenate([taps[b, :conv_w.size].reshape(conv_w.shape) for b in range(4)], axis=2)

    inv_freq = ROPE_THETA ** (-jnp.arange(0, ROPE, 2, dtype=F32) / ROPE)
    half = ROPE // 2
    zeros = lambda n: jnp.zeros((n,), F32)
    ones = jnp.ones((half,), F32)
    invf = jnp.concatenate([zeros(NOPE), inv_freq, inv_freq, zeros(HP - QK_DIM)]).reshape(1, HP)
    mask_a = jnp.concatenate([zeros(NOPE), -ones, zeros(HP - NOPE - half)]).reshape(1, HP)
    mask_b = jnp.concatenate([zeros(NOPE + half), ones, zeros(HP - QK_DIM)]).reshape(1, HP)
    tables = _rope_tables_call(positions.reshape(S, 1), invf, mask_a, mask_b)

    even_w = {}

    def even_weights_of(e):
        if e not in even_w:
            even_w[e] = _even_weights(full['even_w_in'][e], full['w_uq'][e], full['w_ukv'][e], full['even_w_out'][e],
                                      q_norm[e], kv_norm[e], sg_norm[e], sg_w[e], sg_b[e])
        return even_w[e]

    def gain_row(arr, l):
        return arr[l].reshape(1, D)

    saved = []
    h = _rmsnorm_call("first_norm", xs, gain_row(ffn_pre_norm, 0))
    xc = xs
    for l in range(depth):
        xc, h, s_pre = _ffn_fwd(f"l{l}_pre", xc, h, full['ffn_pre_w_gate'][l], full['ffn_pre_w_up'][l],
                                full['ffn_pre_w_down'][l], gain_row(mix_norm, l))
        if l % 2 == 0:
            xc, h, s_mix, gathered = _even_mixer_fwd(f"l{l}_mix", xc, h, even_weights_of(l // 2), tables,
                                                     gain_row(ffn_post_norm, l), late_shards if l == 0 else ())
            if l == 0:
                unpack(gathered, late_keys)
        else:
            o = l // 2
            xc, h, s_mix = _conv_mixer_fwd(f"l{l}_mix", xc, h, full['conv_w_in'][o], taps[o],
                                           full['conv_w_out'][o], gain_row(ffn_post_norm, l))
        g_next = gain_row(ffn_pre_norm, l + 1) if l + 1 < depth else final_norm.reshape(1, D)
        xc, h, s_post = _ffn_fwd(f"l{l}_post", xc, h, full['ffn_post_w_gate'][l], full['ffn_post_w_up'][l],
                                 full['ffn_post_w_down'][l], g_next)
        saved.append((s_pre, s_mix, s_post))

    dx, dxb, d_final, loss_part = _loss_call(xc, target, final_norm.reshape(1, D))
    loss = lax.psum(loss_part[0, 0], ("x", "y", "c"))

    gl = {n: [None] * w_loc[n].shape[0] for n in WEIGHTS if n != 'final_norm'}
    for l in reversed(range(depth)):
        s_pre, s_mix, s_post = saved[l]
        dx, dxb, dgain, dwg, dwu, dwd = _ffn_bwd(f"l{l}_post", s_post, dx, dxb, full['ffn_post_w_gate'][l],
                                                 full['ffn_post_w_up'][l], full['ffn_post_w_down'][l],
                                                 gain_row(ffn_post_norm, l))
        gl['ffn_post_norm'][l] = dgain[0]
        gl['ffn_post_w_gate'][l], gl['ffn_post_w_up'][l], gl['ffn_post_w_down'][l] = dwg, dwu, dwd
        if l % 2 == 0:
            e = l // 2
            dx, dxb, dgain, eg = _even_mixer_bwd(f"l{l}_mix", s_mix, dx, dxb, even_weights_of(e), tables,
                                                 gain_row(mix_norm, l))
            for n, val in _even_grads_unpad(eg).items():
                gl[n][e] = val
        else:
            o = l // 2
            dx, dxb, dgain, dw_in, dtaps, dw_out = _conv_mixer_bwd(f"l{l}_mix", s_mix, dx, dxb, full['conv_w_in'][o],
                                                                   taps[o], full['conv_w_out'][o],
                                                                   gain_row(mix_norm, l))
            gl['conv_w_in'][o], gl['conv_w'][o], gl['conv_w_out'][o] = dw_in, dtaps, dw_out
        gl['mix_norm'][l] = dgain[0]
        dx, dxb, dgain, dwg, dwu, dwd = _ffn_bwd(f"l{l}_pre", s_pre, dx, dxb, full['ffn_pre_w_gate'][l],
                                                 full['ffn_pre_w_up'][l], full['ffn_pre_w_down'][l],
                                                 gain_row(ffn_pre_norm, l))
        gl['ffn_pre_norm'][l] = dgain[0]
        gl['ffn_pre_w_gate'][l], gl['ffn_pre_w_up'][l], gl['ffn_pre_w_down'][l] = dwg, dwu, dwd
    grad_x = dx.reshape(x.shape)
    part = {n: jnp.stack(gl[n]) for n in gl if n not in FFN_WEIGHTS}
    part['final_norm'] = d_final[0]

    def row_blocked(n):
        g = part[n]
        L, r4, cols = g.shape
        return jnp.swapaxes(g.reshape(L, 4, r4 // 4, cols), 0, 1).reshape(4, L * (r4 // 4), cols).astype(_WIRE_DTYPE)

    pack_a = jnp.concatenate([gl[n][l] for n in GROUP_A for l in range(depth)], axis=1)
    pack_b = jnp.concatenate([gl[n][l] for n in GROUP_B if n in FFN_WEIGHTS for l in range(depth)]
                             + [row_blocked(n) for n in GROUP_B if n not in FFN_WEIGHTS], axis=1)
    pack_c = jnp.stack([_pad_rows(jnp.concatenate(
        [_shard_slice(part[n], SHARD_AXIS[n], b).astype(_WIRE_DTYPE).reshape(-1) for n in GROUP_C]), PACK_ROW_MULT)
        for b in range(4)])
    packs = [_pad_axis(p, 1, PACK_ROW_MULT) for p in (pack_a, pack_b, pack_c)]
    packs = [p.reshape(4, 2, p.shape[1] // 2, p.shape[2]) for p in packs]
    core = lax.axis_index("c").astype(jnp.int32).reshape(1)
    theirs = _pair_exchange_call(packs)
    pairs = [_pair_add_call(f"pair_add_{i}", p, t, core) for i, (p, t) in enumerate(zip(packs, theirs))]
    arrived = _chip_scatter_call(pairs)
    mine = [_sum_slots_call(f"sum_grad_slots_{i}", r, core) for i, r in enumerate(arrived)]
    red_a, red_b, red_c = [t.reshape(-1, t.shape[2]) for t in _sibling_share_call(mine)]
    grads = {}
    for group, red in ((GROUP_A, red_a), (GROUP_B, red_b)):
        row = 0
        for n in group:
            shp = w_loc[n].shape
            rows = shp[0] * shp[1]
            grads[n] = red[row:row + rows].reshape(shp)
            row += rows
    red_c = red_c.reshape(-1)
    off = 0
    for n in GROUP_C:
        shp = w_loc[n].shape
        size = int(np.prod(shp))
        grads[n] = red_c[off:off + size].reshape(shp)
        off += size

    small = _pad_rows(jnp.concatenate([part[n].reshape(-1) for n in REPLICATED]), 8)
    small_sum = _allreduce_small_call(small).reshape(-1)
    off = 0
    for n in REPLICATED:
        size = int(np.prod(w_loc[n].shape))
        grads[n] = small_sum[off:off + size].reshape(w_loc[n].shape)
        off += size

    deltas, new_m, new_v = {}, {}, {}
    for n in WEIGHTS:
        deltas[n], new_m[n], new_v[n] = _adamw_call("adamw_" + n, w_loc[n], grads[n], m_loc[n], v_loc[n])
    return (loss, grad_x, *[grads[n] for n in WEIGHTS], *[deltas[n] for n in WEIGHTS],
            *[new_m[n] for n in WEIGHTS], *[new_v[n] for n in WEIGHTS])
```

```python
import functools

import numpy as np
import jax
import jax.numpy as jnp
from jax import lax
from jax.experimental import pallas as pl
from jax.experimental.pallas import tpu as pltpu

F32 = jnp.float32
BF16 = jnp.bfloat16
_MXU_DTYPE = jnp.bfloat16
_WIRE_DTYPE = jnp.bfloat16
_VMEM_LIMIT = 52 * 1024 * 1024
_LANES = 128
_ATT_BLOCK = 512
_ROW_TILE = 512
_SG_TILE = 1024

NORM_EPS = 1e-6
HEADS = 8
NOPE = 64
ROPE = 32
VDIM = 64
QK_DIM = NOPE + ROPE
HP = 128
Q_LORA = 384
KV_LORA = 256
SG_WIDTH = 512
SG_GROUPS = 8
SG_GDIM = 64
SG_CHUNK = 128
ROPE_THETA = 10000.0
PROJ_W = Q_LORA + KV_LORA + HP + 2 * SG_WIDTH
ADAM_LR = 0.001
ADAM_B1 = 0.9
ADAM_B2 = 0.999
ADAM_EPS = 1e-08
ADAM_WD = 0.01
ADAM_STEP = 10
MESH = pl.DeviceIdType.MESH
PACK_COLS = 1024
PACK_ROW_MULT = 256

SHARDED = ['ffn_pre_w_gate', 'ffn_pre_w_up', 'ffn_pre_w_down', 'ffn_post_w_gate', 'ffn_post_w_up',
           'ffn_post_w_down', 'even_w_in', 'w_uq', 'w_ukv', 'even_w_out', 'conv_w_in', 'conv_w', 'conv_w_out']
SHARD_AXIS = {'ffn_pre_w_gate': 2, 'ffn_pre_w_up': 2, 'ffn_pre_w_down': 1, 'ffn_post_w_gate': 2,
              'ffn_post_w_up': 2, 'ffn_post_w_down': 1, 'even_w_in': 2, 'w_uq': 2, 'w_ukv': 2,
              'even_w_out': 1, 'conv_w_in': 2, 'conv_w': 2, 'conv_w_out': 1}
FFN_WEIGHTS = ['ffn_pre_w_gate', 'ffn_pre_w_up', 'ffn_pre_w_down', 'ffn_post_w_gate', 'ffn_post_w_up',
               'ffn_post_w_down']
GROUP_A = ['ffn_pre_w_gate', 'ffn_pre_w_up', 'ffn_post_w_gate', 'ffn_post_w_up']
GROUP_B = ['ffn_pre_w_down', 'ffn_post_w_down', 'even_w_out', 'conv_w_out']
GROUP_C = ['even_w_in', 'w_uq', 'w_ukv', 'conv_w_in', 'conv_w']
GATHER_C = ['even_w_in', 'w_uq', 'w_ukv', 'conv_w_in']
REPLICATED = ['ffn_pre_norm', 'mix_norm', 'ffn_post_norm', 'q_norm', 'kv_norm', 'sg_norm', 'sg_w', 'sg_b',
              'final_norm']
WEIGHTS = ['ffn_pre_norm', 'ffn_pre_w_gate', 'ffn_pre_w_up', 'ffn_pre_w_down', 'mix_norm', 'ffn_post_norm',
           'ffn_post_w_gate', 'ffn_post_w_up', 'ffn_post_w_down', 'even_w_in', 'q_norm', 'w_uq', 'kv_norm',
           'w_ukv', 'sg_norm', 'sg_w', 'sg_b', 'even_w_out', 'conv_w_in', 'conv_w', 'conv_w_out', 'final_norm']


def _params(sem=None):
    return pltpu.CompilerParams(vmem_limit_bytes=_VMEM_LIMIT,
                                **({} if sem is None else {'dimension_semantics': sem}))


def _pick(n, pref, mult):
    best = None
    t = mult
    while t <= min(n, pref):
        if n % t == 0:
            best = t
        t += mult
    return n if best is None else best


def _mx(v):
    return v if v.dtype == _MXU_DTYPE else v.astype(_MXU_DTYPE)


def _sigmoid(a):
    return 1.0 / (1.0 + jnp.exp(-a))


def _rms_stats(x):
    rstd = lax.rsqrt(jnp.mean(x * x, axis=-1, keepdims=True) + NORM_EPS)
    return x * rstd, rstd


def _rms_bwd(x, g, dh):
    xhat, rstd = _rms_stats(x)
    gdh = g * dh
    dx = rstd * (gdh - xhat * jnp.mean(gdh * xhat, axis=-1, keepdims=True))
    return dx, dh * xhat


def _fused_matmul(name, mode, lhs, rhs, prods, n_acc, epilogue, out_dtypes, M, N, K, tm, tn, tk,
                  tile_extras=(), row_extras=(), mrow_extras=(), n_colsum=0):
    gj, gi, gk = N // tn, M // tm, K // tk
    assert gj * tn == N and gi * tm == M and gk * tk == K, (name, M, N, K, tm, tn, tk)
    dims = {'nn': (((1,), (0,)), ((), ())), 'nt': (((1,), (1,)), ((), ())), 'tn': (((0,), (0,)), ((), ()))}[mode]

    def lhs_spec(roff, coff, kb):
        kb = tk if kb is None else kb
        if mode == 'tn':
            return pl.BlockSpec((kb, tm), lambda j, i, k: (k + roff, i + coff))
        return pl.BlockSpec((tm, kb), lambda j, i, k: (i + roff, k + coff))

    def rhs_spec(roff, coff, kb):
        kb = tk if kb is None else kb
        if mode == 'nt':
            return pl.BlockSpec((tn, kb), lambda j, i, k: (j + roff, k + coff))
        return pl.BlockSpec((kb, tn), lambda j, i, k: (k + roff, j + coff))

    in_specs = [lhs_spec(*a[1:]) for a in lhs] + [rhs_spec(*a[1:]) for a in rhs]
    in_specs += [pl.BlockSpec((tm, tn), lambda j, i, k: (i, j)) for _ in tile_extras]
    in_specs += [pl.BlockSpec((1, tn), lambda j, i, k: (0, j)) for _ in row_extras]
    in_specs += [pl.BlockSpec((tm, a.shape[1]), lambda j, i, k: (i, 0)) for a in mrow_extras]
    n_out = len(out_dtypes)
    out_shape = [jax.ShapeDtypeStruct((M, N), d) for d in out_dtypes]
    out_specs = [pl.BlockSpec((tm, tn), lambda j, i, k: (i, j)) for _ in out_dtypes]
    out_shape += [jax.ShapeDtypeStruct((1, N), F32) for _ in range(n_colsum)]
    out_specs += [pl.BlockSpec((1, tn), lambda j, i, k: (0, j)) for _ in range(n_colsum)]
    scratch = [pltpu.VMEM((tm, tn), F32) for _ in range(n_acc)] if gk > 1 else []
    nl, nr, nt, nrw, nm = len(lhs), len(rhs), len(tile_extras), len(row_extras), len(mrow_extras)

    def body(*refs):
        pos = 0
        lhs_refs = refs[pos:pos + nl]; pos += nl
        rhs_refs = refs[pos:pos + nr]; pos += nr
        tile_refs = refs[pos:pos + nt]; pos += nt
        row_refs = refs[pos:pos + nrw]; pos += nrw
        mrow_refs = refs[pos:pos + nm]; pos += nm
        out_refs = refs[pos:pos + n_out]; pos += n_out
        cs_refs = refs[pos:pos + n_colsum]; pos += n_colsum
        acc_refs = refs[pos:]
        i = pl.program_id(1)
        k = pl.program_id(2)

        def partials():
            res = [None] * n_acc
            for (li, ri, ai) in prods:
                d = lax.dot_general(_mx(lhs_refs[li][...]), _mx(rhs_refs[ri][...]), dims,
                                    preferred_element_type=F32)
                res[ai] = d if res[ai] is None else res[ai] + d
            return res

        def finish(accs):
            outs = epilogue(accs, [r[...] for r in tile_refs], [r[...] for r in row_refs],
                            [r[...] for r in mrow_refs])
            for r, o in zip(out_refs, outs[:n_out]):
                r[...] = o.astype(r.dtype)
            for r, c in zip(cs_refs, outs[n_out:]):
                c = jnp.sum(c, axis=0, keepdims=True)

                @pl.when(i == 0)
                def _():
                    r[...] = c

                @pl.when(i != 0)
                def _():
                    r[...] += c

        if gk == 1:
            finish(partials())
        else:
            p = partials()

            @pl.when(k == 0)
            def _():
                for r, v in zip(acc_refs, p):
                    r[...] = v

            @pl.when(k != 0)
            def _():
                for r, v in zip(acc_refs, p):
                    r[...] += v

            @pl.when(k == gk - 1)
            def _():
                finish([r[...] for r in acc_refs])

    res = pl.pallas_call(
        body, name=name, grid=(gj, gi, gk), in_specs=in_specs, out_specs=out_specs, out_shape=out_shape,
        scratch_shapes=scratch, compiler_params=_params(("arbitrary", "arbitrary", "arbitrary")),
    )(*[a[0] for a in lhs], *[a[0] for a in rhs], *tile_extras, *row_extras, *mrow_extras)
    return res


def _op(a, roff=0, coff=0, kb=None):
    return (a, roff, coff, kb)


def _ident_epi(scale=None):
    def epi(accs, tiles, rows, mrows):
        return [a if scale is None else a * scale for a in accs]
    return epi


def _resid_norm_epi(scale):
    def epi(accs, tiles, rows, mrows):
        x_new = tiles[0] + scale * accs[0]
        xhat, _ = _rms_stats(x_new)
        return [x_new, xhat * rows[0]]
    return epi


def _norm_bwd_epi(accs, tiles, rows, mrows):
    dx_n, dg = _rms_bwd(tiles[0], rows[0], accs[0])
    dx = tiles[1] + dx_n
    return [dx, dx, dg]


def _rmsnorm_call(name, x, g):
    S, D = x.shape
    tm = _pick(S, _ROW_TILE, 8)

    def body(x_ref, g_ref, h_ref):
        xhat, _ = _rms_stats(x_ref[...])
        h_ref[...] = (xhat * g_ref[...]).astype(h_ref.dtype)

    return pl.pallas_call(
        body, name=name, grid=(S // tm,),
        in_specs=[pl.BlockSpec((tm, D), lambda i: (i, 0)), pl.BlockSpec((1, D), lambda i: (0, 0))],
        out_specs=pl.BlockSpec((tm, D), lambda i: (i, 0)),
        out_shape=jax.ShapeDtypeStruct((S, D), BF16), compiler_params=_params(("arbitrary",)),
    )(x, g)


def _loss_call(x, target, g):
    S, D = x.shape
    tm = _pick(S, _ROW_TILE, 8)

    def body(x_ref, t_ref, g_ref, dx_ref, dxb_ref, dg_ref, loss_ref):
        i = pl.program_id(0)
        x_t = x_ref[...]
        gain = g_ref[...]
        xhat, _ = _rms_stats(x_t)
        diff = xhat * gain - t_ref[...]
        dy = diff * (1.0 / D)
        dx, dg = _rms_bwd(x_t, gain, dy)
        dx_ref[...] = dx
        dxb_ref[...] = dx.astype(BF16)
        dg = jnp.sum(dg, axis=0, keepdims=True)
        part = 0.5 * jnp.sum(jnp.sum(diff * diff, axis=1, keepdims=True), axis=0, keepdims=True) * (1.0 / D)
        part = jnp.broadcast_to(part, (1, _LANES))

        @pl.when(i == 0)
        def _():
            dg_ref[...] = dg
            loss_ref[...] = part

        @pl.when(i != 0)
        def _():
            dg_ref[...] += dg
            loss_ref[...] += part

    row = lambda i: (i, 0)
    fixed = lambda i: (0, 0)
    return pl.pallas_call(
        body, name="loss_head", grid=(S // tm,),
        in_specs=[pl.BlockSpec((tm, D), row), pl.BlockSpec((tm, D), row), pl.BlockSpec((1, D), fixed)],
        out_specs=[pl.BlockSpec((tm, D), row), pl.BlockSpec((tm, D), row), pl.BlockSpec((1, D), fixed),
                   pl.BlockSpec((1, _LANES), fixed)],
        out_shape=[jax.ShapeDtypeStruct((S, D), F32), jax.ShapeDtypeStruct((S, D), BF16),
                   jax.ShapeDtypeStruct((1, D), F32), jax.ShapeDtypeStruct((1, _LANES), F32)],
        compiler_params=_params(("arbitrary",)),
    )(x, target, g)


def _rope_tables_call(pos_col, invf, mask_a, mask_b):
    S = pos_col.shape[0]
    tm = _pick(S, _ROW_TILE, 8)

    def body(p_ref, f_ref, a_ref, b_ref, cos_ref, sa_ref, sb_ref):
        ang = p_ref[...].astype(F32) * f_ref[...]
        sn = jnp.sin(ang)
        cos_ref[...] = jnp.cos(ang)
        sa_ref[...] = sn * a_ref[...]
        sb_ref[...] = sn * b_ref[...]

    row = lambda i: (i, 0)
    fixed = lambda i: (0, 0)
    return pl.pallas_call(
        body, name="rope_tables", grid=(S // tm,),
        in_specs=[pl.BlockSpec((tm, 1), row)] + [pl.BlockSpec((1, HP), fixed)] * 3,
        out_specs=[pl.BlockSpec((tm, HP), row)] * 3,
        out_shape=[jax.ShapeDtypeStruct((S, HP), F32)] * 3, compiler_params=_params(("arbitrary",)),
    )(pos_col, invf, mask_a, mask_b)


def _rope(t, cos, sa, sb):
    return t * cos + pltpu.roll(t, HP - ROPE // 2, 1) * sa + pltpu.roll(t, ROPE // 2, 1) * sb


def _rope_t(d, cos, sa, sb):
    return d * cos + pltpu.roll(d * sa, ROPE // 2, 1) + pltpu.roll(d * sb, HP - ROPE // 2, 1)


def _gelu(z):
    return 0.5 * z * (1.0 + lax.erf(z * np.float32(1.0 / np.sqrt(2.0))))


def _gelu_grad(z):
    cdf = 0.5 * (1.0 + lax.erf(z * np.float32(1.0 / np.sqrt(2.0))))
    pdf = np.float32(1.0 / np.sqrt(2.0 * np.pi)) * jnp.exp(-0.5 * z * z)
    return cdf + z * pdf


_CQ0, _CKV0, _KR0, _Z0 = 0, Q_LORA, Q_LORA + KV_LORA, Q_LORA + KV_LORA + HP


def _even_prep_call(proj, qn, kvn, sgn, cos, sa, sb):
    S = proj.shape[0]
    tm = _pick(S, 256, 8)

    def body(p_ref, qn_ref, kvn_ref, sgn_ref, cos_ref, sa_ref, sb_ref, cq_ref, ckv_ref, kr_ref, u_ref, v_ref):
        cq = p_ref[:, _CQ0:_CQ0 + Q_LORA]
        cq_ref[...] = (_rms_stats(cq)[0] * qn_ref[...]).astype(BF16)
        ckv = p_ref[:, _CKV0:_CKV0 + KV_LORA]
        ckv_ref[...] = (_rms_stats(ckv)[0] * kvn_ref[...]).astype(BF16)
        kr = p_ref[:, _KR0:_KR0 + HP]
        kr_ref[...] = _rope(kr, cos_ref[...], sa_ref[...], sb_ref[...]).astype(BF16)
        u_ref[...] = _gelu(p_ref[:, _Z0:_Z0 + SG_WIDTH]).astype(BF16)
        zv = _gelu(p_ref[:, _Z0 + SG_WIDTH:_Z0 + 2 * SG_WIDTH])
        v_ref[...] = (_rms_stats(zv)[0] * sgn_ref[...]).astype(BF16)

    row = lambda i: (i, 0)
    fixed = lambda i: (0, 0)
    widths = [Q_LORA, KV_LORA, HP, SG_WIDTH, SG_WIDTH]
    return pl.pallas_call(
        body, name="even_prep", grid=(S // tm,),
        in_specs=[pl.BlockSpec((tm, PROJ_W), row), pl.BlockSpec((1, Q_LORA), fixed),
                  pl.BlockSpec((1, KV_LORA), fixed), pl.BlockSpec((1, SG_WIDTH), fixed)]
        + [pl.BlockSpec((tm, HP), row)] * 3,
        out_specs=[pl.BlockSpec((tm, w), row) for w in widths],
        out_shape=[jax.ShapeDtypeStruct((S, w), BF16) for w in widths],
        compiler_params=_params(("arbitrary",)),
    )(proj, qn, kvn, sgn, cos, sa, sb)


def _even_prep_bwd_call(proj, qn, kvn, sgn, cos, sa, sb, dcqn, dckvn, dk, du, dvn):
    S = proj.shape[0]
    tm = _pick(S, 256, 8)

    def body(p_ref, qn_ref, kvn_ref, sgn_ref, cos_ref, sa_ref, sb_ref, dcq_ref, dckv_ref, dk_ref, du_ref,
             dvn_ref, dp_ref, dqn_ref, dkvn_ref, dsgn_ref):
        i = pl.program_id(0)
        dcq, gq = _rms_bwd(p_ref[:, _CQ0:_CQ0 + Q_LORA], qn_ref[...], dcq_ref[...])
        dp_ref[:, _CQ0:_CQ0 + Q_LORA] = dcq.astype(BF16)
        dckv, gkv = _rms_bwd(p_ref[:, _CKV0:_CKV0 + KV_LORA], kvn_ref[...], dckv_ref[...])
        dp_ref[:, _CKV0:_CKV0 + KV_LORA] = dckv.astype(BF16)
        dkr = dk_ref[:, 0:HP].astype(F32)
        for h in range(1, HEADS):
            dkr = dkr + dk_ref[:, h * HP:(h + 1) * HP].astype(F32)
        lane = lax.broadcasted_iota(jnp.int32, dkr.shape, 1)
        dkr = jnp.where((lane >= NOPE) & (lane < QK_DIM), dkr, 0.0)
        dp_ref[:, _KR0:_KR0 + HP] = _rope_t(dkr, cos_ref[...], sa_ref[...], sb_ref[...]).astype(BF16)
        zu = p_ref[:, _Z0:_Z0 + SG_WIDTH]
        dp_ref[:, _Z0:_Z0 + SG_WIDTH] = (du_ref[...].astype(F32) * _gelu_grad(zu)).astype(BF16)
        zv = p_ref[:, _Z0 + SG_WIDTH:_Z0 + 2 * SG_WIDTH]
        dgv, gsg = _rms_bwd(_gelu(zv), sgn_ref[...], dvn_ref[...].astype(F32))
        dp_ref[:, _Z0 + SG_WIDTH:_Z0 + 2 * SG_WIDTH] = (dgv * _gelu_grad(zv)).astype(BF16)
        sums = [jnp.sum(t, axis=0, keepdims=True) for t in (gq, gkv, gsg)]

        @pl.when(i == 0)
        def _():
            for r, s in zip((dqn_ref, dkvn_ref, dsgn_ref), sums):
                r[...] = s

        @pl.when(i != 0)
        def _():
            for r, s in zip((dqn_ref, dkvn_ref, dsgn_ref), sums):
                r[...] += s

    row = lambda i: (i, 0)
    fixed = lambda i: (0, 0)
    return pl.pallas_call(
        body, name="even_prep_bwd", grid=(S // tm,),
        in_specs=[pl.BlockSpec((tm, PROJ_W), row), pl.BlockSpec((1, Q_LORA), fixed),
                  pl.BlockSpec((1, KV_LORA), fixed), pl.BlockSpec((1, SG_WIDTH), fixed)]
        + [pl.BlockSpec((tm, HP), row)] * 3
        + [pl.BlockSpec((tm, Q_LORA), row), pl.BlockSpec((tm, KV_LORA), row),
           pl.BlockSpec((tm, HEADS * HP), row), pl.BlockSpec((tm, SG_WIDTH), row),
           pl.BlockSpec((tm, SG_WIDTH), row)],
        out_specs=[pl.BlockSpec((tm, PROJ_W), row), pl.BlockSpec((1, Q_LORA), fixed),
                   pl.BlockSpec((1, KV_LORA), fixed), pl.BlockSpec((1, SG_WIDTH), fixed)],
        out_shape=[jax.ShapeDtypeStruct((S, PROJ_W), BF16), jax.ShapeDtypeStruct((1, Q_LORA), F32),
                   jax.ShapeDtypeStruct((1, KV_LORA), F32), jax.ShapeDtypeStruct((1, SG_WIDTH), F32)],
        compiler_params=_params(("arbitrary",)),
    )(proj, qn, kvn, sgn, cos, sa, sb, dcqn, dckvn, dk, du, dvn)


def _causal_mask(rows, cols):
    r = lax.broadcasted_iota(jnp.int32, (rows, cols), 0)
    c = lax.broadcasted_iota(jnp.int32, (rows, cols), 1)
    return c <= r


def _flash_fwd_call(q, k, v, carry=()):
    S = q.shape[0]
    tb = _pick(S, _ATT_BLOCK, 128)
    nq = S // tb
    nt_dims = (((1,), (1,)), ((), ()))

    nc = len(carry)

    def body(*refs):
        q_ref, k_ref, v_ref = refs[:3]
        o_ref, lse_ref = refs[3 + nc:5 + nc]
        s_a, s_b, m_ref, acc_ref = refs[5 + 2 * nc:9 + 2 * nc]
        h = pl.program_id(0)
        i = pl.program_id(1)
        if nc:
            send, forward, finish = _gather_phases(refs[3:3 + nc], refs[5 + nc:5 + 2 * nc], *refs[9 + 2 * nc:])
            pl.when((h == 0) & (i == 0))(send)
            pl.when((h == HEADS // 2) & (i == 0))(forward)

        def scores(buf, j):
            k_t = k_ref[pl.ds(pl.multiple_of(j * tb, tb), tb), :]
            buf[...] = lax.dot_general(q_ref[...], k_t, nt_dims, preferred_element_type=F32)

        def update(buf, j, masked):
            v_t = v_ref[pl.ds(pl.multiple_of(j * tb, tb), tb), :]
            s = buf[...]
            if masked:
                s = jnp.where(_causal_mask(tb, tb), s, -1e30)
            m = m_ref[...]
            m_new = jnp.maximum(m, jnp.max(s, axis=1, keepdims=True))
            alpha = jnp.exp(m - m_new)
            p = jnp.exp(s - m_new)
            acc_ref[...] = alpha * acc_ref[...] + jnp.dot(p.astype(v_t.dtype), v_t, preferred_element_type=F32)
            m_ref[...] = m_new

        m_ref[...] = jnp.full((tb, 1), -1e30, F32)
        acc_ref[...] = jnp.zeros((tb, HP), F32)
        scores(s_a, 0)
        pairs = i // 2

        def two_blocks(t, carry):
            scores(s_b, 2 * t + 1)
            update(s_a, 2 * t, False)
            scores(s_a, 2 * t + 2)
            update(s_b, 2 * t + 1, False)
            return carry

        lax.fori_loop(0, pairs, two_blocks, 0)

        @pl.when(2 * pairs == i)
        def _():
            update(s_a, i, True)

        @pl.when(2 * pairs != i)
        def _():
            scores(s_b, i)
            update(s_a, i - 1, False)
            update(s_b, i, True)

        acc = acc_ref[...]
        l = acc[:, VDIM:VDIM + 1]
        lane = lax.broadcasted_iota(jnp.int32, (tb, HP), 1)
        o_ref[...] = jnp.where(lane < VDIM, acc / l, 0.0).astype(o_ref.dtype)
        lse = jnp.broadcast_to(m_ref[...] + jnp.log(l), (tb, HP))
        lse_ref[0, 0] = jnp.transpose(lse)[0:8, :]
        if nc:
            pl.when((h == HEADS - 1) & (i == nq - 1))(finish)

    start = _gather_start(carry)
    any_spec = pl.BlockSpec(memory_space=pl.ANY)
    res = pl.pallas_call(
        body, name="flash_fwd_gather" if nc else "flash_fwd", grid=(HEADS, nq),
        in_specs=[pl.BlockSpec((tb, HP), lambda h, i: (i, h)), pl.BlockSpec((S, HP), lambda h, i: (0, h)),
                  pl.BlockSpec((S, HP), lambda h, i: (0, h))] + [any_spec] * nc,
        out_specs=[pl.BlockSpec((tb, HP), lambda h, i: (i, h)),
                   pl.BlockSpec((1, 1, 8, tb), lambda h, i: (h, i, 0, 0))] + [any_spec] * nc,
        out_shape=[jax.ShapeDtypeStruct((S, HEADS * HP), q.dtype), jax.ShapeDtypeStruct((HEADS, nq, 8, tb), F32)]
        + [jax.ShapeDtypeStruct(t.shape, t.dtype) for t in start],
        scratch_shapes=[pltpu.VMEM((tb, tb), F32), pltpu.VMEM((tb, tb), F32), pltpu.VMEM((tb, 1), F32),
                        pltpu.VMEM((tb, HP), F32)] + ([pltpu.SemaphoreType.DMA((n,)) for n in _gather_sems(nc)]
                                                      if nc else []),
        input_output_aliases={3 + a: 2 + a for a in range(nc)},
        compiler_params=pltpu.CompilerParams(vmem_limit_bytes=_VMEM_LIMIT, has_side_effects=bool(nc),
                                             dimension_semantics=("arbitrary", "arbitrary")),
    )(q, k, v, *start)
    return res[0], res[1], list(res[2:])


def _attn_delta_call(o, do):
    S = o.shape[0]
    tb = _pick(S, _ATT_BLOCK, 128)
    nq = S // tb
    nb = _pick(nq, 4, 1)

    def body(o_ref, do_ref, d_ref):
        for r in range(nb):
            rows = slice(r * tb, (r + 1) * tb)
            d = jnp.sum(o_ref[rows, :].astype(F32) * do_ref[rows, :].astype(F32), axis=1, keepdims=True)
            d_ref[0, r] = jnp.transpose(jnp.broadcast_to(d, (tb, HP)))[0:8, :]

    return pl.pallas_call(
        body, name="attn_delta", grid=(HEADS, nq // nb),
        in_specs=[pl.BlockSpec((nb * tb, HP), lambda h, i: (i, h))] * 2,
        out_specs=pl.BlockSpec((1, nb, 8, tb), lambda h, i: (h, i, 0, 0)),
        out_shape=jax.ShapeDtypeStruct((HEADS, nq, 8, tb), F32), compiler_params=_params(("arbitrary", "arbitrary")),
    )(o, do)


def _flash_bwd_call(q, k, v, do, lse, delta, carry=()):
    S = q.shape[0]
    tb = _pick(S, _ATT_BLOCK, 128)
    nq = S // tb
    nt_dims = (((1,), (1,)), ((), ()))
    tn_dims = (((0,), (0,)), ((), ()))
    nc = len(carry)

    def body(*refs):
        q_ref, do_ref, lse_ref, dl_ref, k_ref, v_ref = refs[:6]
        dq_ref, dk_ref, dv_ref = refs[6 + nc:9 + nc]
        st_a, dp_a, st_b, dp_b, dk_acc, dv_acc = refs[9 + 2 * nc:15 + 2 * nc]
        h = pl.program_id(0)
        j = pl.program_id(1)
        if nc:
            send, finish = _scatter_phases(refs[6:6 + nc], refs[9 + nc:9 + 2 * nc], *refs[15 + 2 * nc:])
            pl.when((h == 0) & (j == 0))(send)

        @pl.when(j == 0)
        def _():
            dq_ref[...] = jnp.zeros_like(dq_ref)

        def rows_of(i):
            return pl.ds(pl.multiple_of(i * tb, tb), tb)

        def scores(st_buf, dp_buf, i):
            st_buf[...] = lax.dot_general(k_ref[...], q_ref[rows_of(i), :], nt_dims, preferred_element_type=F32)
            dp_buf[...] = lax.dot_general(v_ref[...], do_ref[rows_of(i), :], nt_dims, preferred_element_type=F32)

        def update(st_buf, dp_buf, i, masked):
            q_t = q_ref[rows_of(i), :]
            do_t = do_ref[rows_of(i), :]
            pt = jnp.exp(st_buf[...] - lse_ref[0, i, 0:1, :])
            if masked:
                pt = jnp.where(jnp.transpose(_causal_mask(tb, tb)), pt, 0.0)
            dst = (pt * (dp_buf[...] - dl_ref[0, i, 0:1, :])).astype(q_t.dtype)
            dv_acc[...] += jnp.dot(pt.astype(do_t.dtype), do_t, preferred_element_type=F32)
            dk_acc[...] += jnp.dot(dst, q_t, preferred_element_type=F32)
            dq_ref[rows_of(i), :] += lax.dot_general(dst, k_ref[...], tn_dims, preferred_element_type=F32)

        last = nq - 1
        dk_acc[...] = jnp.zeros((tb, HP), F32)
        dv_acc[...] = jnp.zeros((tb, HP), F32)
        scores(st_b, dp_b, j)
        scores(st_a, dp_a, jnp.minimum(j + 1, last))
        update(st_b, dp_b, j, True)
        rest = last - j
        pairs = rest // 2

        def two_blocks(t, carry):
            i0 = j + 1 + 2 * t
            scores(st_b, dp_b, i0 + 1)
            update(st_a, dp_a, i0, False)
            scores(st_a, dp_a, jnp.minimum(i0 + 2, last))
            update(st_b, dp_b, i0 + 1, False)
            return carry

        lax.fori_loop(0, pairs, two_blocks, 0)

        @pl.when(2 * pairs != rest)
        def _():
            update(st_a, dp_a, last, False)

        dk_ref[...] = dk_acc[...].astype(dk_ref.dtype)
        dv_ref[...] = dv_acc[...].astype(dv_ref.dtype)
        if nc:
            pl.when((h == HEADS - 1) & (j == nq - 1))(finish)

    head = lambda h, j: (0, h)
    blk = lambda h, j: (j, h)
    rows = lambda h, j: (h, 0, 0, 0)
    any_spec = pl.BlockSpec(memory_space=pl.ANY)
    res = pl.pallas_call(
        body, name="flash_bwd_scatter" if nc else "flash_bwd", grid=(HEADS, nq),
        in_specs=[pl.BlockSpec((S, HP), head), pl.BlockSpec((S, HP), head), pl.BlockSpec((1, nq, 8, tb), rows),
                  pl.BlockSpec((1, nq, 8, tb), rows), pl.BlockSpec((tb, HP), blk), pl.BlockSpec((tb, HP), blk)]
        + [any_spec] * nc,
        out_specs=[pl.BlockSpec((S, HP), head), pl.BlockSpec((tb, HP), blk), pl.BlockSpec((tb, HP), blk)]
        + [any_spec] * nc,
        out_shape=[jax.ShapeDtypeStruct((S, HEADS * HP), F32), jax.ShapeDtypeStruct((S, HEADS * HP), BF16),
                   jax.ShapeDtypeStruct((S, HEADS * HP), BF16)]
        + [jax.ShapeDtypeStruct(p.shape, p.dtype) for p in carry],
        scratch_shapes=[pltpu.VMEM((tb, tb), F32)] * 4 + [pltpu.VMEM((tb, HP), F32)] * 2
        + ([pltpu.SemaphoreType.DMA((n,)) for n in _scatter_sems(nc)] if nc else []),
        compiler_params=pltpu.CompilerParams(vmem_limit_bytes=_VMEM_LIMIT, has_side_effects=bool(nc),
                                             dimension_semantics=("arbitrary", "arbitrary")),
    )(q, do, lse, delta, k, v, *carry)
    return res[0], res[1], res[2], list(res[3:])


def _sg_mixed(w_ref, vch, lane_lo):
    blocks = []
    for jb in range(SG_WIDTH // _LANES):
        r = jnp.dot(w_ref[jb], vch[:, jb * _LANES:(jb + 1) * _LANES], preferred_element_type=F32)
        blocks.append(jnp.where(lane_lo, r[0:SG_CHUNK], r[SG_CHUNK:2 * SG_CHUNK]))
    return jnp.concatenate(blocks, axis=1)


def _sgu_fwd_call(vn, u, attn, wst, bexp):
    S = vn.shape[0]
    tm = _pick(S, _SG_TILE, SG_CHUNK)
    AW = HEADS * HP

    def body(v_ref, u_ref, a_ref, w_ref, b_ref, mix_ref):
        lane_lo = lax.broadcasted_iota(jnp.int32, (SG_CHUNK, _LANES), 1) < SG_GDIM
        mix_ref[:, 0:AW] = a_ref[...]
        for c in range(tm // SG_CHUNK):
            rs = slice(c * SG_CHUNK, (c + 1) * SG_CHUNK)
            mixed = _sg_mixed(w_ref, v_ref[rs, :], lane_lo) + b_ref[...]
            mix_ref[rs, AW:AW + SG_WIDTH] = (u_ref[rs, :].astype(F32) * mixed).astype(mix_ref.dtype)

    row = lambda i: (i, 0)
    return pl.pallas_call(
        body, name="sgu_fwd", grid=(S // tm,),
        in_specs=[pl.BlockSpec((tm, SG_WIDTH), row), pl.BlockSpec((tm, SG_WIDTH), row), pl.BlockSpec((tm, AW), row),
                  pl.BlockSpec((SG_WIDTH // _LANES, 2 * SG_CHUNK, SG_CHUNK), lambda i: (0, 0, 0)),
                  pl.BlockSpec((SG_CHUNK, SG_WIDTH), lambda i: (0, 0))],
        out_specs=pl.BlockSpec((tm, AW + SG_WIDTH), row),
        out_shape=jax.ShapeDtypeStruct((S, AW + SG_WIDTH), BF16), compiler_params=_params(("arbitrary",)),
    )(vn, u, attn, wst, bexp)


def _sgu_bwd_call(dmix, vn, u, wst, wst_t, bexp):
    S = vn.shape[0]
    tm = _pick(S, _SG_TILE, SG_CHUNK)
    nblk = SG_WIDTH // _LANES
    col0 = (HEADS * HP) // SG_WIDTH
    nt_dims = (((1,), (1,)), ((), ()))

    def body(d_ref, v_ref, u_ref, w_ref, wt_ref, b_ref, du_ref, dv_ref, dw_ref, db_ref, dbacc_ref):
        i = pl.program_id(0)
        lane_lo = lax.broadcasted_iota(jnp.int32, (SG_CHUNK, _LANES), 1) < SG_GDIM

        @pl.when(i == 0)
        def _():
            dw_ref[...] = jnp.zeros_like(dw_ref)
            dbacc_ref[...] = jnp.zeros_like(dbacc_ref)

        for c in range(tm // SG_CHUNK):
            rs = slice(c * SG_CHUNK, (c + 1) * SG_CHUNK)
            vch = v_ref[rs, :]
            dsg = d_ref[rs, :].astype(F32)
            mixed = _sg_mixed(w_ref, vch, lane_lo) + b_ref[...]
            du_ref[rs, :] = (dsg * mixed).astype(du_ref.dtype)
            dmixed = dsg * u_ref[rs, :].astype(F32)
            dbacc_ref[...] += dmixed
            dmx = dmixed.astype(vch.dtype)
            dv_ref[rs, :] = _sg_mixed(wt_ref, dmx, lane_lo).astype(dv_ref.dtype)
            for jb in range(nblk):
                dblk = dmx[:, jb * _LANES:(jb + 1) * _LANES]
                vblk = vch[:, jb * _LANES:(jb + 1) * _LANES]
                zero = jnp.zeros_like(dblk)
                dw_ref[2 * jb] += lax.dot_general(jnp.where(lane_lo, dblk, zero), vblk, nt_dims,
                                                  preferred_element_type=F32)
                dw_ref[2 * jb + 1] += lax.dot_general(jnp.where(lane_lo, zero, dblk), vblk, nt_dims,
                                                      preferred_element_type=F32)

        @pl.when(i == pl.num_programs(0) - 1)
        def _():
            tri = _causal_mask(SG_CHUNK, SG_CHUNK)
            for g in range(SG_GROUPS):
                dw_ref[g] = jnp.where(tri, dw_ref[g], 0.0)
            lane = lax.broadcasted_iota(jnp.int32, (SG_CHUNK, _LANES), 1)
            out = jnp.zeros((SG_CHUNK, _LANES), F32)
            for g in range(SG_GROUPS):
                blk = dbacc_ref[:, (g // 2) * _LANES:(g // 2 + 1) * _LANES]
                sel = lane_lo if g % 2 == 0 else jnp.logical_not(lane_lo)
                s = jnp.sum(jnp.where(sel, blk, 0.0), axis=1, keepdims=True)
                out = jnp.where(lane == g, s, out)
            db_ref[...] = out

    row = lambda i: (i, 0)
    wspec = pl.BlockSpec((nblk, 2 * SG_CHUNK, SG_CHUNK), lambda i: (0, 0, 0))
    return pl.pallas_call(
        body, name="sgu_bwd", grid=(S // tm,),
        in_specs=[pl.BlockSpec((tm, SG_WIDTH), lambda i: (i, col0)), pl.BlockSpec((tm, SG_WIDTH), row),
                  pl.BlockSpec((tm, SG_WIDTH), row), wspec, wspec,
                  pl.BlockSpec((SG_CHUNK, SG_WIDTH), lambda i: (0, 0))],
        out_specs=[pl.BlockSpec((tm, SG_WIDTH), row), pl.BlockSpec((tm, SG_WIDTH), row),
                   pl.BlockSpec((SG_GROUPS, SG_CHUNK, SG_CHUNK), lambda i: (0, 0, 0)),
                   pl.BlockSpec((SG_CHUNK, _LANES), lambda i: (0, 0))],
        out_shape=[jax.ShapeDtypeStruct((S, SG_WIDTH), BF16), jax.ShapeDtypeStruct((S, SG_WIDTH), BF16),
                   jax.ShapeDtypeStruct((SG_GROUPS, SG_CHUNK, SG_CHUNK), F32),
                   jax.ShapeDtypeStruct((SG_CHUNK, _LANES), F32)],
        scratch_shapes=[pltpu.VMEM((SG_CHUNK, SG_WIDTH), F32)],
        compiler_params=_params(("arbitrary",)),
    )(dmix, vn, u, wst, wst_t, bexp)


def _shift_down(t, halo, n):
    rows = lax.broadcasted_iota(jnp.int32, t.shape, 0)
    out = pltpu.roll(t, n, 0)
    for r in range(n):
        out = jnp.where(rows == r, halo[8 - n + r:8 - n + r + 1, :], out)
    return out


def _shift_up(t, halo, n):
    tm = t.shape[0]
    rows = lax.broadcasted_iota(jnp.int32, t.shape, 0)
    out = pltpu.roll(t, tm - n, 0)
    for r in range(n):
        out = jnp.where(rows == tm - n + r, halo[r:r + 1, :], out)
    return out


def _conv_fwd_call(p, w):
    S, C3 = p.shape
    C = C3 // 3
    tm = _pick(S, _ROW_TILE, 8)
    hb = tm // 8

    def body(p_ref, c_prev, z_prev, w_ref, m_ref):
        i = pl.program_id(0)
        cz = p_ref[:, C:2 * C] * p_ref[:, 2 * C:3 * C]
        czp = jnp.where(i > 0, c_prev[...] * z_prev[...], 0.0)
        y = w_ref[2:3, :] * cz + w_ref[1:2, :] * _shift_down(cz, czp, 1) + w_ref[0:1, :] * _shift_down(cz, czp, 2)
        m_ref[...] = (p_ref[:, 0:C] * y).astype(m_ref.dtype)

    prev = lambda col: (lambda i: (jnp.maximum(i * hb - 1, 0), col))
    return pl.pallas_call(
        body, name="conv_fwd", grid=(S // tm,),
        in_specs=[pl.BlockSpec((tm, C3), lambda i: (i, 0)), pl.BlockSpec((8, C), prev(1)),
                  pl.BlockSpec((8, C), prev(2)), pl.BlockSpec((3, C), lambda i: (0, 0))],
        out_specs=pl.BlockSpec((tm, C), lambda i: (i, 0)),
        out_shape=jax.ShapeDtypeStruct((S, C), BF16), compiler_params=_params(("arbitrary",)),
    )(p, p, p, w)


def _conv_bwd_call(p, w, dm):
    S, C3 = p.shape
    C = C3 // 3
    tm = _pick(S, 256, 8)
    hb = tm // 8
    n_tiles = S // tm

    def body(p_ref, c_prev, z_prev, b_next, dm_ref, dm_next, w_ref, dp_ref, dw_ref):
        i = pl.program_id(0)
        b = p_ref[:, 0:C]
        c = p_ref[:, C:2 * C]
        z = p_ref[:, 2 * C:3 * C]
        cz = c * z
        czp = jnp.where(i > 0, c_prev[...] * z_prev[...], 0.0)
        s1 = _shift_down(cz, czp, 1)
        s2 = _shift_down(cz, czp, 2)
        w0, w1, w2 = w_ref[0:1, :], w_ref[1:2, :], w_ref[2:3, :]
        y = w2 * cz + w1 * s1 + w0 * s2
        dm_t = dm_ref[...]
        dy = dm_t * b
        dyn = jnp.where(i < n_tiles - 1, dm_next[...] * b_next[...], 0.0)
        dcz = w2 * dy + w1 * _shift_up(dy, dyn, 1) + w0 * _shift_up(dy, dyn, 2)
        dp_ref[:, 0:C] = (dm_t * y).astype(dp_ref.dtype)
        dp_ref[:, C:2 * C] = (dcz * z).astype(dp_ref.dtype)
        dp_ref[:, 2 * C:3 * C] = (dcz * c).astype(dp_ref.dtype)
        dw = jnp.concatenate([jnp.sum(dy * s2, axis=0, keepdims=True), jnp.sum(dy * s1, axis=0, keepdims=True),
                              jnp.sum(dy * cz, axis=0, keepdims=True)], axis=0)

        @pl.when(i == 0)
        def _():
            dw_ref[...] = dw

        @pl.when(i != 0)
        def _():
            dw_ref[...] += dw

    prev = lambda col: (lambda i: (jnp.maximum(i * hb - 1, 0), col))
    nxt = lambda col: (lambda i: (jnp.minimum((i + 1) * hb, S // 8 - 1), col))
    return pl.pallas_call(
        body, name="conv_bwd", grid=(n_tiles,),
        in_specs=[pl.BlockSpec((tm, C3), lambda i: (i, 0)), pl.BlockSpec((8, C), prev(1)),
                  pl.BlockSpec((8, C), prev(2)), pl.BlockSpec((8, C), nxt(0)),
                  pl.BlockSpec((tm, C), lambda i: (i, 0)), pl.BlockSpec((8, C), nxt(0)),
                  pl.BlockSpec((3, C), lambda i: (0, 0))],
        out_specs=[pl.BlockSpec((tm, C3), lambda i: (i, 0)), pl.BlockSpec((3, C), lambda i: (0, 0))],
        out_shape=[jax.ShapeDtypeStruct((S, C3), BF16), jax.ShapeDtypeStruct((3, C), F32)],
        compiler_params=_params(("arbitrary",)),
    )(p, p, p, p, dm, dm, w)


def _my_place():
    return lax.axis_index("x"), lax.axis_index("y"), lax.axis_index("c")


def _gather_weights_call(name, shard):
    R, C = shard.shape

    def body(s_ref, o_ref, send_sems, recv_sems, local_sem):
        x, y, c = _my_place()
        mine = 2 * x + y
        local = pltpu.make_async_copy(s_ref, o_ref.at[mine], local_sem)
        local.start()
        peers = [(1 - x, y), (x, 1 - y), (1 - x, 1 - y)]
        copies = []
        for k, (px, py) in enumerate(peers):
            cp = pltpu.make_async_remote_copy(src_ref=s_ref, dst_ref=o_ref.at[mine], send_sem=send_sems.at[k],
                                              recv_sem=recv_sems.at[k], device_id=(px, py, c), device_id_type=MESH)
            cp.start()
            copies.append(cp)
        for k, (px, py) in enumerate(peers):
            pltpu.make_async_remote_copy(src_ref=s_ref, dst_ref=o_ref.at[2 * px + py], send_sem=send_sems.at[k],
                                         recv_sem=recv_sems.at[k], device_id=(px, py, c),
                                         device_id_type=MESH).wait_recv()
        for cp in copies:
            cp.wait_send()
        local.wait()

    any_spec = pl.BlockSpec(memory_space=pl.ANY)
    return pl.pallas_call(
        body, name=name, in_specs=[any_spec], out_specs=any_spec,
        out_shape=jax.ShapeDtypeStruct((4, R, C), shard.dtype),
        scratch_shapes=[pltpu.SemaphoreType.DMA((3,)), pltpu.SemaphoreType.DMA((3,)), pltpu.SemaphoreType.DMA],
        compiler_params=pltpu.CompilerParams(has_side_effects=True),
    )(shard)


_D2D_CHUNKS = 4


_LOCAL_CHUNKS = 8


def _local_copies(src_of, dst_of, rows, sems, base):
    rc = rows // _LOCAL_CHUNKS
    assert rc * _LOCAL_CHUNKS == rows and rc % 16 == 0, rows
    out = []
    for j in range(_LOCAL_CHUNKS):
        sl = pl.ds(j * rc, rc)
        out.append(pltpu.make_async_copy(src_of(sl), dst_of(sl), sems.at[base + j]))
    return out


def _comm_call(name, body, arrays, out_shapes, sem_counts, aliases=None):
    any_spec = pl.BlockSpec(memory_space=pl.ANY)
    return pl.pallas_call(
        body, name=name, in_specs=[any_spec] * len(arrays), out_specs=[any_spec] * len(out_shapes),
        out_shape=out_shapes, scratch_shapes=[pltpu.SemaphoreType.DMA((n,)) for n in sem_counts],
        input_output_aliases=aliases or {}, compiler_params=pltpu.CompilerParams(has_side_effects=True),
    )(*arrays)


def _gather_halves_call(shards):
    na = len(shards)

    def body(*refs):
        send, forward, finish = _gather_phases(refs[:na], refs[na:2 * na], *refs[2 * na:])
        send()
        forward()
        finish()

    start = _gather_start(shards)
    outs = [jax.ShapeDtypeStruct(t.shape, t.dtype) for t in start]
    return _comm_call("gather_weights", body, start, outs, _gather_sems(na), aliases={a: a for a in range(na)})


def _gather_start(shards):
    for s in shards:
        assert s.shape[0] % (2 * _D2D_CHUNKS * 16) == 0, s.shape
    return [jnp.broadcast_to(s[None], (4,) + tuple(s.shape)) for s in shards]


def _gather_sems(na):
    return [3 * na, 3 * na, 3 * na * _D2D_CHUNKS, 3 * na * _D2D_CHUNKS]


def _gather_phases(s_refs, o_refs, ici_send, ici_recv, d2d_send, d2d_recv):
    na = len(s_refs)
    x, y, c = _my_place()
    mine = 2 * x + y
    chips = [(1 - x, y), (x, 1 - y), (1 - x, 1 - y)]

    def ici(a, k, chip, block):
        Rh = s_refs[a].shape[1] // 2
        my_half = pl.ds(pl.multiple_of(c * Rh, 16), Rh)
        return pltpu.make_async_remote_copy(src_ref=s_refs[a].at[mine, my_half], dst_ref=o_refs[a].at[block, my_half],
                                            send_sem=ici_send.at[3 * a + k], recv_sem=ici_recv.at[3 * a + k],
                                            device_id=(chip[0], chip[1], c), device_id_type=MESH)

    def d2d(a, k, j, block, half):
        Rh = s_refs[a].shape[1] // 2
        rc = Rh // _D2D_CHUNKS
        rows = pl.ds(pl.multiple_of(half * Rh + j * rc, 16), rc)
        idx = (3 * a + k) * _D2D_CHUNKS + j
        return pltpu.make_async_remote_copy(src_ref=o_refs[a].at[block, rows], dst_ref=o_refs[a].at[block, rows],
                                            send_sem=d2d_send.at[idx], recv_sem=d2d_recv.at[idx],
                                            device_id=(x, y, 1 - c), device_id_type=MESH)

    def send():
        for a in range(na):
            for k, chip in enumerate(chips):
                ici(a, k, chip, mine).start()

    def forward():
        for a in range(na):
            for k, chip in enumerate(chips):
                block = 2 * chip[0] + chip[1]
                ici(a, k, chip, block).wait_recv()
                for j in range(_D2D_CHUNKS):
                    d2d(a, k, j, block, c).start()

    def finish():
        for a in range(na):
            for k, chip in enumerate(chips):
                block = 2 * chip[0] + chip[1]
                for j in range(_D2D_CHUNKS):
                    d2d(a, k, j, block, 1 - c).wait_recv()
        for a in range(na):
            for k, chip in enumerate(chips):
                ici(a, k, chip, mine).wait_send()
                for j in range(_D2D_CHUNKS):
                    d2d(a, k, j, 2 * chip[0] + chip[1], c).wait_send()

    return send, forward, finish


def _pair_exchange_call(packed, tag):
    na = len(packed)

    def body(*refs):
        p_refs, o_refs = refs[:na], refs[na:2 * na]
        send_sems, recv_sems = refs[2 * na:]
        x, y, c = _my_place()
        copies = []
        for a in range(na):
            nb, _, Rh, _ = p_refs[a].shape
            rc = Rh // _D2D_CHUNKS
            assert rc * _D2D_CHUNKS == Rh and rc % 16 == 0
            for b in range(nb):
                for j in range(_D2D_CHUNKS):
                    rows = pl.ds(j * rc, rc)
                    idx = (a * nb + b) * _D2D_CHUNKS + j
                    copies.append(pltpu.make_async_remote_copy(
                        src_ref=p_refs[a].at[b, 1 - c, rows], dst_ref=o_refs[a].at[b, rows],
                        send_sem=send_sems.at[idx], recv_sem=recv_sems.at[idx],
                        device_id=(x, y, 1 - c), device_id_type=MESH))
        for t in copies:
            t.start()
        for t in copies:
            t.wait_recv()
        for t in copies:
            t.wait_send()

    outs = [jax.ShapeDtypeStruct((p.shape[0], p.shape[2], p.shape[3]), p.dtype) for p in packed]
    n = sum(p.shape[0] for p in packed) * _D2D_CHUNKS
    return _comm_call("pair_exchange_" + tag, body, packed, outs, [n, n])


def _pair_add_call(name, packed, other, core):
    nb, _, Rh, C = packed.shape
    tr = _pick(Rh, 512, 16)

    def body(c_ref, p_ref, o_ref, q_ref):
        q_ref[...] = (p_ref[...].astype(F32) + o_ref[...].astype(F32)).astype(q_ref.dtype)

    grid_spec = pltpu.PrefetchScalarGridSpec(
        num_scalar_prefetch=1, grid=(nb, Rh // tr),
        in_specs=[pl.BlockSpec((None, None, tr, C), lambda b, r, c_ref: (b, c_ref[0], r, 0)),
                  pl.BlockSpec((None, tr, C), lambda b, r, c_ref: (b, r, 0))],
        out_specs=pl.BlockSpec((None, tr, C), lambda b, r, c_ref: (b, r, 0)))
    return pl.pallas_call(
        body, name=name, grid_spec=grid_spec, out_shape=jax.ShapeDtypeStruct((nb, Rh, C), packed.dtype),
        compiler_params=_params(("arbitrary", "arbitrary")),
    )(core, packed, other)


def _chip_scatter_call(pairs):
    na = len(pairs)

    def body(*refs):
        send, finish = _scatter_phases(refs[:na], refs[na:2 * na], *refs[2 * na:])
        send()
        finish()

    outs = [jax.ShapeDtypeStruct(p.shape, p.dtype) for p in pairs]
    return _comm_call("chip_scatter", body, pairs, outs, _scatter_sems(na))


def _scatter_sems(na):
    return [3 * na, 3 * na, na * _LOCAL_CHUNKS]


def _scatter_phases(p_refs, o_refs, send_sems, recv_sems, local_sems):
    na = len(p_refs)
    x, y, c = _my_place()
    mine = 2 * x + y
    chips = [(1 - x, y), (x, 1 - y), (1 - x, 1 - y)]

    def local(a):
        p_ref, o_ref = p_refs[a], o_refs[a]
        return _local_copies(lambda sl: p_ref.at[mine, sl], lambda sl: o_ref.at[mine, sl], p_ref.shape[1],
                             local_sems, a * _LOCAL_CHUNKS)

    def remote(a, k, src_block, dst_block):
        px, py = chips[k]
        return pltpu.make_async_remote_copy(src_ref=p_refs[a].at[src_block], dst_ref=o_refs[a].at[dst_block],
                                            send_sem=send_sems.at[3 * a + k], recv_sem=recv_sems.at[3 * a + k],
                                            device_id=(px, py, c), device_id_type=MESH)

    def send():
        for a in range(na):
            for t in local(a):
                t.start()
            for k, (px, py) in enumerate(chips):
                remote(a, k, 2 * px + py, mine).start()

    def finish():
        for a in range(na):
            for k, (px, py) in enumerate(chips):
                remote(a, k, mine, 2 * px + py).wait_recv()
        for a in range(na):
            for k, (px, py) in enumerate(chips):
                remote(a, k, 2 * px + py, mine).wait_send()
            for t in local(a):
                t.wait()

    return send, finish


def _sum_slots_call(name, parts, core):
    n, R, C = parts.shape
    tr = _pick(R, 256, 8)

    def body(c_ref, p_ref, o_ref):
        acc = p_ref[0].astype(F32)
        for s in range(1, n):
            acc = acc + p_ref[s].astype(F32)
        o_ref[...] = acc

    grid_spec = pltpu.PrefetchScalarGridSpec(
        num_scalar_prefetch=1, grid=(R // tr,),
        in_specs=[pl.BlockSpec((n, tr, C), lambda i, c_ref: (0, i, 0))],
        out_specs=pl.BlockSpec((None, tr, C), lambda i, c_ref: (c_ref[0], i, 0)))
    return pl.pallas_call(
        body, name=name, grid_spec=grid_spec, out_shape=jax.ShapeDtypeStruct((2, R, C), F32),
        compiler_params=_params(("arbitrary",)),
    )(core, parts)


def _sibling_share_call(halves):
    na = len(halves)
    nch = 2 * _D2D_CHUNKS

    def body(*refs):
        h_refs, o_refs = refs[:na], refs[na:2 * na]
        send_sems, recv_sems = refs[2 * na:]
        x, y, c = _my_place()

        def cp(a, j, slot):
            rc = h_refs[a].shape[1] // nch
            rows = pl.ds(j * rc, rc)
            return pltpu.make_async_remote_copy(src_ref=h_refs[a].at[slot, rows], dst_ref=o_refs[a].at[slot, rows],
                                                send_sem=send_sems.at[a * nch + j], recv_sem=recv_sems.at[a * nch + j],
                                                device_id=(x, y, 1 - c), device_id_type=MESH)

        copies = [cp(a, j, c) for a in range(na) for j in range(nch)]
        for t in copies:
            t.start()
        for a in range(na):
            for j in range(nch):
                cp(a, j, 1 - c).wait_recv()
        for t in copies:
            t.wait_send()

    for h in halves:
        assert h.shape[1] % (nch * 8) == 0, h.shape
    outs = [jax.ShapeDtypeStruct(h.shape, h.dtype) for h in halves]
    return _comm_call("sibling_share", body, halves, outs, [na * nch, na * nch], aliases={a: a for a in range(na)})


def _allreduce_small_call(part):
    R, C = part.shape

    def body(p_ref, o_ref, slots, send_sems, recv_sems):
        x, y, c = _my_place()
        me = 4 * x + 2 * y + c
        peers = []
        for k in range(1, 8):
            px = x ^ (k >> 2) if (k >> 2) else x
            py = y ^ ((k >> 1) & 1) if ((k >> 1) & 1) else y
            pc = c ^ (k & 1) if (k & 1) else c
            peers.append((px, py, pc))
        copies = []
        for k, (px, py, pc) in enumerate(peers):
            cp = pltpu.make_async_remote_copy(src_ref=p_ref, dst_ref=slots.at[me], send_sem=send_sems.at[k],
                                              recv_sem=recv_sems.at[k], device_id=(px, py, pc), device_id_type=MESH)
            cp.start()
            copies.append(cp)
        slots[me] = p_ref[...]
        for k, (px, py, pc) in enumerate(peers):
            pltpu.make_async_remote_copy(src_ref=p_ref, dst_ref=slots.at[4 * px + 2 * py + pc],
                                         send_sem=send_sems.at[k], recv_sem=recv_sems.at[k],
                                         device_id=(px, py, pc), device_id_type=MESH).wait_recv()
        for cp in copies:
            cp.wait_send()
        acc = slots[0]
        for s in range(1, 8):
            acc = acc + slots[s]
        o_ref[...] = acc

    vm = pl.BlockSpec(memory_space=pltpu.VMEM)
    return pl.pallas_call(
        body, name="allreduce_small", in_specs=[vm], out_specs=vm,
        out_shape=jax.ShapeDtypeStruct((R, C), F32),
        scratch_shapes=[pltpu.VMEM((8, R, C), F32), pltpu.SemaphoreType.DMA((7,)), pltpu.SemaphoreType.DMA((7,))],
        compiler_params=pltpu.CompilerParams(has_side_effects=True, vmem_limit_bytes=_VMEM_LIMIT),
    )(part)


def _adamw_call(name, w, g, m, v):
    shape = w.shape
    cols = shape[-1]
    rows = int(np.prod(shape[:-1])) if len(shape) > 1 else 1
    w2, g2, m2, v2 = (t.reshape(rows, cols) for t in (w, g, m, v))
    tr = _pick(rows, 256, 8)
    c1 = 1.0 / (1.0 - ADAM_B1 ** ADAM_STEP)
    c2 = 1.0 / (1.0 - ADAM_B2 ** ADAM_STEP)

    def body(w_ref, g_ref, m_ref, v_ref, d_ref, nm_ref, nv_ref):
        gr = g_ref[...]
        m_new = ADAM_B1 * m_ref[...] + (1.0 - ADAM_B1) * gr
        v_new = ADAM_B2 * v_ref[...] + (1.0 - ADAM_B2) * (gr * gr)
        m_hat = m_new / (1.0 - ADAM_B1 ** ADAM_STEP)
        v_hat = v_new / (1.0 - ADAM_B2 ** ADAM_STEP)
        d_ref[...] = -ADAM_LR * (m_hat / (jnp.sqrt(v_hat) + ADAM_EPS) + ADAM_WD * w_ref[...])
        nm_ref[...] = m_new
        nv_ref[...] = v_new

    spec = pl.BlockSpec((tr, cols), lambda i: (i, 0))
    d, nm, nv = pl.pallas_call(
        body, name=name, grid=(rows // tr,), in_specs=[spec] * 4, out_specs=[spec] * 3,
        out_shape=[jax.ShapeDtypeStruct((rows, cols), F32)] * 3, compiler_params=_params(("arbitrary",)),
    )(w2, g2, m2, v2)
    return d.reshape(shape), nm.reshape(shape), nv.reshape(shape)


def _pad_rows(flat, mult):
    n = flat.shape[0]
    unit = PACK_COLS * mult
    total = -(-n // unit) * unit
    return jnp.pad(flat, (0, total - n)).reshape(total // PACK_COLS, PACK_COLS)


def _pad_axis(arr, axis, mult):
    n = arr.shape[axis]
    total = -(-n // mult) * mult
    if total == n:
        return arr
    widths = [(0, 0)] * arr.ndim
    widths[axis] = (0, total - n)
    return jnp.pad(arr, widths)


def _shard_slice(arr, axis, blk, nblk=4):
    w = arr.shape[axis] // nblk
    return lax.slice_in_dim(arr, blk * w, (blk + 1) * w, axis=axis)


_NT = (((1,), (1,)), ((), ()))
_TN = (((0,), (0,)), ((), ()))


def _ffn_fwd(tag, x, h, wg, wu, wd, g_next):
    S, D = x.shape
    (wg, gi), (wu, ui), (wd, di) = wg, wu, wd
    NB, Fb = wg.shape[0], wg.shape[2]
    tm = _pick(S, 1024, 8)

    def gate_up(h_ref, wg_ref, wu_ref, a_ref, u_ref, s_ref):
        h_t = _mx(h_ref[...])
        a = jnp.dot(h_t, _mx(wg_ref[...]), preferred_element_type=F32)
        u = jnp.dot(h_t, _mx(wu_ref[...]), preferred_element_type=F32)
        sig = _sigmoid(a)
        silu = a * sig
        a_ref[...] = (u * (sig * (1.0 + a * (1.0 - sig)))).astype(a_ref.dtype)
        u_ref[...] = silu.astype(u_ref.dtype)
        s_ref[...] = (silu * u).astype(s_ref.dtype)

    hid = pl.BlockSpec((None, tm, Fb), lambda b, i: (b, i, 0))
    a, u, s = pl.pallas_call(
        gate_up, name=tag + "_gate_up", grid=(NB, S // tm),
        in_specs=[pl.BlockSpec((tm, D), lambda b, i: (i, 0)), pl.BlockSpec((None, D, Fb), lambda b, i: (b, gi, 0)),
                  pl.BlockSpec((None, D, Fb), lambda b, i: (b, ui, 0))], out_specs=[hid] * 3,
        out_shape=[jax.ShapeDtypeStruct((NB, S, Fb), BF16)] * 3, compiler_params=_params(("arbitrary", "arbitrary")),
    )(h, wg, wu)

    tm2 = _pick(S, 512, 8)

    def down(s_ref, wd_ref, x_ref, g_ref, xo_ref, ho_ref):
        acc = jnp.dot(_mx(s_ref[0]), _mx(wd_ref[0]), preferred_element_type=F32)
        for b in range(1, NB):
            acc = acc + jnp.dot(_mx(s_ref[b]), _mx(wd_ref[b]), preferred_element_type=F32)
        x_new = x_ref[...] + 0.5 * acc
        xo_ref[...] = x_new
        ho_ref[...] = (_rms_stats(x_new)[0] * g_ref[...]).astype(ho_ref.dtype)

    row = pl.BlockSpec((tm2, D), lambda i: (i, 0))
    x_new, h_next = pl.pallas_call(
        down, name=tag + "_down", grid=(S // tm2,),
        in_specs=[pl.BlockSpec((NB, tm2, Fb), lambda i: (0, i, 0)), pl.BlockSpec((NB, Fb, D), lambda i: (0, di, 0)),
                  row, pl.BlockSpec((1, D), lambda i: (0, 0))],
        out_specs=[row, row], out_shape=[jax.ShapeDtypeStruct((S, D), F32), jax.ShapeDtypeStruct((S, D), BF16)],
        compiler_params=_params(("arbitrary",)),
    )(s, wd, x, g_next)
    return x_new, h_next, (x, h, a, u, s)


def _ffn_bwd(tag, saved, dx_out, dxb, wg, wu, wd, gain):
    x, h, a, u, s = saved
    S, D = x.shape
    (wg, gi), (wu, ui), (wd, di) = wg, wu, wd
    NB, Fb = wg.shape[0], wg.shape[2]
    tm = _pick(S, 1024, 8)
    tk = _pick(S, 1024, 128)
    nk = S // tk

    def dgate_up(d_ref, wd_ref, a_ref, u_ref, da_ref, du_ref):
        ds = 0.5 * lax.dot_general(_mx(d_ref[...]), _mx(wd_ref[...]), _NT, preferred_element_type=F32)
        da_ref[...] = (ds * a_ref[...].astype(F32)).astype(da_ref.dtype)
        du_ref[...] = (ds * u_ref[...].astype(F32)).astype(du_ref.dtype)

    hid = pl.BlockSpec((None, tm, Fb), lambda b, i: (b, i, 0))
    da, du = pl.pallas_call(
        dgate_up, name=tag + "_dgate_up", grid=(NB, S // tm),
        in_specs=[pl.BlockSpec((tm, D), lambda b, i: (i, 0)), pl.BlockSpec((None, Fb, D), lambda b, i: (b, di, 0)),
                  hid, hid],
        out_specs=[hid, hid], out_shape=[jax.ShapeDtypeStruct((NB, S, Fb), BF16)] * 2,
        compiler_params=_params(("arbitrary", "arbitrary")),
    )(dxb, wd, a, u)

    def dw_down(s_ref, d_ref, o_ref, acc_ref):
        k = pl.program_id(1)
        p = lax.dot_general(_mx(s_ref[...]), _mx(d_ref[...]), _TN, preferred_element_type=F32)

        @pl.when(k == 0)
        def _():
            acc_ref[...] = p

        @pl.when(k != 0)
        def _():
            acc_ref[...] += p

        @pl.when(k == nk - 1)
        def _():
            o_ref[...] = (0.5 * acc_ref[...]).astype(o_ref.dtype)

    hk = pl.BlockSpec((None, tk, Fb), lambda b, k: (b, k, 0))
    dwd = pl.pallas_call(
        dw_down, name=tag + "_dw_down", grid=(NB, nk),
        in_specs=[hk, pl.BlockSpec((tk, D), lambda b, k: (k, 0))],
        out_specs=pl.BlockSpec((None, Fb, D), lambda b, k: (b, 0, 0)),
        out_shape=jax.ShapeDtypeStruct((NB, Fb, D), _WIRE_DTYPE), scratch_shapes=[pltpu.VMEM((Fb, D), F32)],
        compiler_params=_params(("arbitrary", "arbitrary")),
    )(s, dxb)

    def dw_gate_up(h_ref, da_ref, du_ref, og_ref, ou_ref, accg_ref, accu_ref):
        k = pl.program_id(1)
        h_t = _mx(h_ref[...])
        pg = lax.dot_general(h_t, _mx(da_ref[...]), _TN, preferred_element_type=F32)
        pu = lax.dot_general(h_t, _mx(du_ref[...]), _TN, preferred_element_type=F32)

        @pl.when(k == 0)
        def _():
            accg_ref[...] = pg
            accu_ref[...] = pu

        @pl.when(k != 0)
        def _():
            accg_ref[...] += pg
            accu_ref[...] += pu

        @pl.when(k == nk - 1)
        def _():
            og_ref[...] = accg_ref[...].astype(og_ref.dtype)
            ou_ref[...] = accu_ref[...].astype(ou_ref.dtype)

    wout = pl.BlockSpec((None, D, Fb), lambda b, k: (b, 0, 0))
    dwg, dwu = pl.pallas_call(
        dw_gate_up, name=tag + "_dw_gate_up", grid=(NB, nk),
        in_specs=[pl.BlockSpec((tk, D), lambda b, k: (k, 0)), hk, hk], out_specs=[wout, wout],
        out_shape=[jax.ShapeDtypeStruct((NB, D, Fb), _WIRE_DTYPE)] * 2,
        scratch_shapes=[pltpu.VMEM((D, Fb), F32)] * 2, compiler_params=_params(("arbitrary", "arbitrary")),
    )(h, da, du)

    tm2 = _pick(S, 512, 8)

    def dx_body(da_ref, du_ref, wg_hbm, wu_hbm, x_ref, dxo_ref, g_ref, dx_ref, dxb_ref, dg_ref, wg_v, wu_v, sem):
        i = pl.program_id(0)

        @pl.when(i == 0)
        def _():
            cg = pltpu.make_async_copy(wg_hbm.at[:, pl.ds(gi * D, D), :], wg_v, sem.at[0])
            cu = pltpu.make_async_copy(wu_hbm.at[:, pl.ds(ui * D, D), :], wu_v, sem.at[1])
            cg.start()
            cu.start()
            cg.wait()
            cu.wait()

        dh = None
        for b in range(NB):
            t = lax.dot_general(_mx(da_ref[b]), wg_v[b], _NT, preferred_element_type=F32)
            t = t + lax.dot_general(_mx(du_ref[b]), wu_v[b], _NT, preferred_element_type=F32)
            dh = t if dh is None else dh + t
        dx_n, dg = _rms_bwd(x_ref[...], g_ref[...], dh)
        dx = dxo_ref[...] + dx_n
        dx_ref[...] = dx
        dxb_ref[...] = dx.astype(dxb_ref.dtype)
        dg = jnp.sum(dg, axis=0, keepdims=True)

        @pl.when(i == 0)
        def _():
            dg_ref[...] = dg

        @pl.when(i != 0)
        def _():
            dg_ref[...] += dg

    row = pl.BlockSpec((tm2, D), lambda i: (i, 0))
    hid2 = pl.BlockSpec((NB, tm2, Fb), lambda i: (0, i, 0))
    anyspec = pl.BlockSpec(memory_space=pl.ANY)
    fixed = pl.BlockSpec((1, D), lambda i: (0, 0))
    dx, dxb_new, dgain = pl.pallas_call(
        dx_body, name=tag + "_dx", grid=(S // tm2,),
        in_specs=[hid2, hid2, anyspec, anyspec, row, row, fixed], out_specs=[row, row, fixed],
        out_shape=[jax.ShapeDtypeStruct((S, D), F32), jax.ShapeDtypeStruct((S, D), BF16),
                   jax.ShapeDtypeStruct((1, D), F32)],
        scratch_shapes=[pltpu.VMEM((NB, D, Fb), wg.dtype), pltpu.VMEM((NB, D, Fb), wu.dtype),
                        pltpu.SemaphoreType.DMA((2,))],
        compiler_params=_params(("arbitrary",)),
    )(da, du, wg, wu, x, dx_out, gain)
    return dx, dxb_new, dgain, dwg, dwu, dwd


def _conv_mixer_fwd(tag, x, h, w_in, w_taps, w_out, g_next):
    S, D = x.shape
    C3 = w_in.shape[1]
    tm = _pick(S, 512, 8)
    p, = _fused_matmul(tag + "_in", 'nn', [_op(h)], [_op(w_in)], [(0, 0, 0)], 1, _ident_epi(), [F32],
                       S, C3, D, tm, _pick(C3, 1024, 128), D)
    m = _conv_fwd_call(p, w_taps)
    x_new, h_next = _fused_matmul(tag + "_out", 'nn', [_op(m)], [_op(w_out)], [(0, 0, 0)], 1, _resid_norm_epi(1.0),
                                  [F32, BF16], S, D, D, tm, D, D, tile_extras=[x], row_extras=[g_next])
    return x_new, h_next, (x, h, p, m)


def _conv_mixer_bwd(tag, saved, dx_out, dxb, w_in, w_taps, w_out, gain):
    x, h, p, m = saved
    S, D = x.shape
    C3 = w_in.shape[1]
    tm = _pick(S, 512, 8)
    tk = _pick(S, 1024, 128)
    dm, = _fused_matmul(tag + "_dm", 'nt', [_op(dxb)], [_op(w_out)], [(0, 0, 0)], 1, _ident_epi(), [F32],
                        S, D, D, tm, D, D)
    dw_out, = _fused_matmul(tag + "_dw_out", 'tn', [_op(m)], [_op(dxb)], [(0, 0, 0)], 1, _ident_epi(), [F32],
                            D, D, S, D, D, tk)
    dp, dtaps = _conv_bwd_call(p, w_taps, dm)
    dw_in, = _fused_matmul(tag + "_dw_in", 'tn', [_op(h)], [_op(dp)], [(0, 0, 0)], 1, _ident_epi(), [F32],
                           D, C3, S, D, _pick(C3, 1024, 128), tk)
    dx, dxb_new, dgain = _fused_matmul(tag + "_dx", 'nt', [_op(dp)], [_op(w_in)], [(0, 0, 0)], 1, _norm_bwd_epi,
                                       [F32, BF16], S, D, C3, tm, D, C3,
                                       tile_extras=[x, dx_out], row_extras=[gain], n_colsum=1)
    return dx, dxb_new, dgain, dw_in, dtaps, dw_out


def _attn_scale():
    return np.float32(QK_DIM ** -0.5)


def _even_mixer_fwd(tag, x, h, wts, tables, g_next, carry=()):
    S, D = x.shape
    cos, sa, sb = tables
    tm = _pick(S, 512, 8)
    AW = HEADS * HP
    proj, = _fused_matmul(tag + "_in", 'nn', [_op(h)], [_op(wts['w_in'])], [(0, 0, 0)], 1, _ident_epi(), [F32],
                          S, PROJ_W, D, tm, _pick(PROJ_W, 896, 128), D)
    cqn, ckvn, kr, u, vn = _even_prep_call(proj, wts['q_norm'], wts['kv_norm'], wts['sg_norm'], cos, sa, sb)
    scale = _attn_scale()

    def q_epi(accs, tiles, rows, mrows):
        c_t, a_t, b_t = mrows
        heads = [_rope(accs[0][:, hh * HP:(hh + 1) * HP], c_t, a_t, b_t) * scale for hh in range(HEADS)]
        return [jnp.concatenate(heads, axis=1)]

    q, = _fused_matmul(tag + "_q", 'nn', [_op(cqn)], [_op(wts['w_q'])], [(0, 0, 0)], 1, q_epi, [BF16],
                       S, AW, Q_LORA, tm, AW, Q_LORA, mrow_extras=[cos, sa, sb])

    def kv_epi(accs, tiles, rows, mrows):
        lane = lax.broadcasted_iota(jnp.int32, accs[1].shape, 1)
        v_t = jnp.where((lane & (HP - 1)) == VDIM, 1.0, accs[1])
        return [accs[0] + jnp.concatenate([mrows[0].astype(F32)] * HEADS, axis=1), v_t]

    k, v = _fused_matmul(tag + "_kv", 'nn', [_op(ckvn)], [_op(wts['w_k']), _op(wts['w_v'])],
                         [(0, 0, 0), (0, 1, 1)], 2, kv_epi, [BF16, BF16], S, AW, KV_LORA, tm, AW, KV_LORA,
                         mrow_extras=[kr])
    o, lse, gathered = _flash_fwd_call(q, k, v, carry)
    mix = _sgu_fwd_call(vn, u, o, wts['sg_wst'], wts['sg_bexp'])
    x_new, h_next = _fused_matmul(tag + "_out", 'nn', [_op(mix)], [_op(wts['w_out'])], [(0, 0, 0)], 1,
                                  _resid_norm_epi(1.0), [F32, BF16], S, D, AW + SG_WIDTH, tm, D, AW + SG_WIDTH,
                                  tile_extras=[x], row_extras=[g_next])
    return x_new, h_next, (x, h, proj, cqn, ckvn, u, vn, q, k, v, o, lse, mix), gathered


def _even_mixer_bwd(tag, saved, dx_out, dxb, wts, tables, gain, carry=()):
    x, h, proj, cqn, ckvn, u, vn, q, k, v, o, lse, mix = saved
    S, D = x.shape
    cos, sa, sb = tables
    tm = _pick(S, 512, 8)
    tk = _pick(S, 1024, 128)
    AW = HEADS * HP
    MW = AW + SG_WIDTH
    dmix, = _fused_matmul(tag + "_dmix", 'nt', [_op(dxb)], [_op(wts['w_out'])], [(0, 0, 0)], 1, _ident_epi(), [BF16],
                          S, MW, D, tm, _pick(MW, 768, 128), D)
    dw_out, = _fused_matmul(tag + "_dw_out", 'tn', [_op(mix)], [_op(dxb)], [(0, 0, 0)], 1, _ident_epi(), [F32],
                            MW, D, S, _pick(MW, 768, 128), D, tk)
    du, dvn, dsg_w, dsg_b = _sgu_bwd_call(dmix, vn, u, wts['sg_wst'], wts['sg_wst_t'], wts['sg_bexp'])
    delta = _attn_delta_call(o, dmix)
    dq, dk, dv, arrived = _flash_bwd_call(q, k, v, dmix, lse, delta, carry)
    scale = _attn_scale()

    def dq_epi(accs, tiles, rows, mrows):
        return accs

    def dq_pre_call():
        tr = _pick(S, 256, 8)

        def body(d_ref, c_ref, a_ref, b_ref, o_ref):
            for hh in range(HEADS):
                t = _rope_t(d_ref[:, hh * HP:(hh + 1) * HP], c_ref[...], a_ref[...], b_ref[...]) * scale
                o_ref[:, hh * HP:(hh + 1) * HP] = t.astype(o_ref.dtype)

        row = lambda i: (i, 0)
        return pl.pallas_call(
            body, name=tag + "_dq_unrope", grid=(S // tr,),
            in_specs=[pl.BlockSpec((tr, AW), row)] + [pl.BlockSpec((tr, HP), row)] * 3,
            out_specs=pl.BlockSpec((tr, AW), row), out_shape=jax.ShapeDtypeStruct((S, AW), BF16),
            compiler_params=_params(("arbitrary",)),
        )(dq, cos, sa, sb)

    dqp = dq_pre_call()
    dw_q, = _fused_matmul(tag + "_dw_q", 'tn', [_op(cqn)], [_op(dqp)], [(0, 0, 0)], 1, _ident_epi(), [F32],
                          Q_LORA, AW, S, Q_LORA, AW, tk)
    dcqn, = _fused_matmul(tag + "_dcq", 'nt', [_op(dqp)], [_op(wts['w_q'])], [(0, 0, 0)], 1, dq_epi, [F32],
                          S, Q_LORA, AW, tm, Q_LORA, AW)
    dw_k, dw_v = _fused_matmul(tag + "_dw_kv", 'tn', [_op(ckvn)], [_op(dk), _op(dv)], [(0, 0, 0), (0, 1, 1)], 2,
                               _ident_epi(), [F32, F32], KV_LORA, AW, S, KV_LORA, AW, tk)
    dckvn, = _fused_matmul(tag + "_dckv", 'nt', [_op(dk), _op(dv)], [_op(wts['w_k']), _op(wts['w_v'])],
                           [(0, 0, 0), (1, 1, 0)], 1, dq_epi, [F32], S, KV_LORA, AW, tm, KV_LORA, AW)
    dproj, dqn, dkvn, dsgn = _even_prep_bwd_call(proj, wts['q_norm'], wts['kv_norm'], wts['sg_norm'], cos, sa, sb,
                                                 dcqn, dckvn, dk, du, dvn)
    dw_in, = _fused_matmul(tag + "_dw_in", 'tn', [_op(h)], [_op(dproj)], [(0, 0, 0)], 1, _ident_epi(), [F32],
                           D, PROJ_W, S, D, _pick(PROJ_W, 896, 128), tk)
    dx, dxb_new, dgain = _fused_matmul(tag + "_dx", 'nt', [_op(dproj)], [_op(wts['w_in'])], [(0, 0, 0)], 1,
                                       _norm_bwd_epi, [F32, BF16], S, D, PROJ_W, tm, D, PROJ_W,
                                       tile_extras=[x, dx_out], row_extras=[gain], n_colsum=1)
    grads = dict(w_in=dw_in, w_q=dw_q, w_k=dw_k, w_v=dw_v, w_out=dw_out, q_norm=dqn, kv_norm=dkvn, sg_norm=dsgn,
                 sg_w=dsg_w, sg_b=dsg_b)
    return dx, dxb_new, dgain, grads, arrived


def _even_weights(w_in, w_uq, w_ukv, w_out, q_norm, kv_norm, sg_norm, sg_w, sg_b):
    D = w_in.shape[0]
    kr_cols = jnp.pad(w_in[:, Q_LORA + KV_LORA:Q_LORA + KV_LORA + ROPE], ((0, 0), (NOPE, HP - QK_DIM)))
    w_in_p = jnp.concatenate([w_in[:, :Q_LORA + KV_LORA], kr_cols, w_in[:, Q_LORA + KV_LORA + ROPE:]], axis=1)
    wq = w_uq.reshape(Q_LORA, HEADS, QK_DIM)
    w_q = jnp.pad(wq, ((0, 0), (0, 0), (0, HP - QK_DIM))).reshape(Q_LORA, HEADS * HP)
    wkv = w_ukv.reshape(KV_LORA, HEADS, NOPE + VDIM)
    w_k = jnp.pad(wkv[:, :, :NOPE], ((0, 0), (0, 0), (0, HP - NOPE))).reshape(KV_LORA, HEADS * HP)
    w_v = jnp.pad(wkv[:, :, NOPE:], ((0, 0), (0, 0), (0, HP - VDIM))).reshape(KV_LORA, HEADS * HP)
    wo_a = w_out[:HEADS * VDIM].reshape(HEADS, VDIM, D)
    wo_a = jnp.pad(wo_a, ((0, 0), (0, HP - VDIM), (0, 0))).reshape(HEADS * HP, D)
    w_out_p = jnp.concatenate([wo_a, w_out[HEADS * VDIM:]], axis=0)
    tri = jnp.tril(jnp.ones((SG_CHUNK, SG_CHUNK), F32))
    wm = sg_w * tri
    wst = wm.reshape(SG_GROUPS // 2, 2 * SG_CHUNK, SG_CHUNK).astype(_MXU_DTYPE)
    wst_t = jnp.swapaxes(wm, 1, 2).reshape(SG_GROUPS // 2, 2 * SG_CHUNK, SG_CHUNK).astype(_MXU_DTYPE)
    bexp = jnp.repeat(sg_b.T, SG_GDIM, axis=1)
    return dict(w_in=w_in_p, w_q=w_q, w_k=w_k, w_v=w_v, w_out=w_out_p, sg_wst=wst, sg_wst_t=wst_t, sg_bexp=bexp,
                q_norm=q_norm.reshape(1, -1), kv_norm=kv_norm.reshape(1, -1), sg_norm=sg_norm.reshape(1, -1))


def _even_grads_unpad(g):
    d_in = g['w_in']
    kr0 = Q_LORA + KV_LORA
    dw_in = jnp.concatenate([d_in[:, :kr0], d_in[:, kr0 + NOPE:kr0 + QK_DIM], d_in[:, kr0 + HP:]], axis=1)
    dw_uq = g['w_q'].reshape(Q_LORA, HEADS, HP)[:, :, :QK_DIM].reshape(Q_LORA, HEADS * QK_DIM)
    dk = g['w_k'].reshape(KV_LORA, HEADS, HP)[:, :, :NOPE]
    dv = g['w_v'].reshape(KV_LORA, HEADS, HP)[:, :, :VDIM]
    dw_ukv = jnp.concatenate([dk, dv], axis=2).reshape(KV_LORA, HEADS * (NOPE + VDIM))
    D = d_in.shape[0]
    wo = g['w_out']
    wo_a = wo[:HEADS * HP].reshape(HEADS, HP, D)[:, :VDIM].reshape(HEADS * VDIM, D)
    dw_out = jnp.concatenate([wo_a, wo[HEADS * HP:]], axis=0)
    dsg_b = g['sg_b'][:, :SG_GROUPS].T
    return dict(even_w_in=dw_in, w_uq=dw_uq, w_ukv=dw_ukv, even_w_out=dw_out, q_norm=g['q_norm'][0],
                kv_norm=g['kv_norm'][0], sg_norm=g['sg_norm'][0], sg_w=g['sg_w'], sg_b=dsg_b)


def kernel(x, positions, ffn_pre_norm, ffn_pre_w_gate, ffn_pre_w_up, ffn_pre_w_down, mix_norm, ffn_post_norm, ffn_post_w_gate, ffn_post_w_up, ffn_post_w_down, even_w_in, q_norm, w_uq, kv_norm, w_ukv, sg_norm, sg_w, sg_b, even_w_out, conv_w_in, conv_w, conv_w_out, final_norm, loss_target, m_ffn_pre_norm, m_ffn_pre_w_gate, m_ffn_pre_w_up, m_ffn_pre_w_down, m_mix_norm, m_ffn_post_norm, m_ffn_post_w_gate, m_ffn_post_w_up, m_ffn_post_w_down, m_even_w_in, m_q_norm, m_w_uq, m_kv_norm, m_w_ukv, m_sg_norm, m_sg_w, m_sg_b, m_even_w_out, m_conv_w_in, m_conv_w, m_conv_w_out, m_final_norm, v_ffn_pre_norm, v_ffn_pre_w_gate, v_ffn_pre_w_up, v_ffn_pre_w_down, v_mix_norm, v_ffn_post_norm, v_ffn_post_w_gate, v_ffn_post_w_up, v_ffn_post_w_down, v_even_w_in, v_q_norm, v_w_uq, v_kv_norm, v_w_ukv, v_sg_norm, v_sg_w, v_sg_b, v_even_w_out, v_conv_w_in, v_conv_w, v_conv_w_out, v_final_norm):
    env = dict(locals())
    w_loc = {n: env[n] for n in WEIGHTS}
    m_loc = {n: env['m_' + n] for n in WEIGHTS}
    v_loc = {n: env['v_' + n] for n in WEIGHTS}
    S, D = x.shape[1], x.shape[2]
    depth = ffn_pre_norm.shape[0]
    xs = x.reshape(S, D)
    target = loss_target.reshape(S, D)

    def layers_of(n, early):
        count = w_loc[n].shape[0]
        if n in ('conv_w_in', 'conv_w', 'conv_w_out'):
            return [] if early else list(range(count))
        return [0] if early else list(range(1, count))

    def shards_of(early):
        wire = lambda n, l: w_loc[n][l].astype(_WIRE_DTYPE)
        keys = [[(n, l) for n in group for l in layers_of(n, early)] for group in (GROUP_A, GROUP_B, GATHER_C)]
        sa = _pad_axis(jnp.concatenate([wire(n, l) for n, l in keys[0]], axis=0), 0, PACK_ROW_MULT)
        sb = _pad_axis(jnp.concatenate([wire(n, l) for n, l in keys[1]], axis=0), 0, PACK_ROW_MULT)
        sc = _pad_rows(jnp.concatenate([wire(n, l).reshape(-1) for n, l in keys[2]]), PACK_ROW_MULT)
        return [sa, sb, sc], keys

    full = {n: {} for n in SHARDED}

    def unpack(gathered, keys):
        gat_a, gat_b, gat_c = gathered
        for idx, (n, l) in enumerate(keys[0]):
            full[n][l] = (gat_a, idx)
        row = 0
        for n, l in keys[1]:
            rows = w_loc[n].shape[1]
            if n in FFN_WEIGHTS:
                full[n][l] = (gat_b, row // rows)
            else:
                full[n][l] = jnp.concatenate([gat_b[b, row:row + rows] for b in range(4)], axis=0)
            row += rows
        gflat = gat_c.reshape(4, -1)
        off = 0
        for n, l in keys[2]:
            shp = w_loc[n].shape[1:]
            size = int(np.prod(shp))
            full[n][l] = jnp.concatenate([gflat[b, off:off + size].reshape(shp) for b in range(4)],
                                         axis=SHARD_AXIS[n] - 1)
            off += size

    early_shards, early_keys = shards_of(True)
    unpack(_gather_halves_call(early_shards), early_keys)
    late_shards, late_keys = shards_of(False)
    taps = _gather_weights_call("gather_taps", _pad_rows(conv_w.reshape(-1), 8)).reshape(4, -1)
    taps = jnp.concatenate([taps[b, :conv_w.size].reshape(conv_w.shape) for b in range(4)], axis=2)

    inv_freq = ROPE_THETA ** (-jnp.arange(0, ROPE, 2, dtype=F32) / ROPE)
    half = ROPE // 2
    zeros = lambda n: jnp.zeros((n,), F32)
    ones = jnp.ones((half,), F32)
    invf = jnp.concatenate([zeros(NOPE), inv_freq, inv_freq, zeros(HP - QK_DIM)]).reshape(1, HP)
    mask_a = jnp.concatenate([zeros(NOPE), -ones, zeros(HP - NOPE - half)]).reshape(1, HP)
    mask_b = jnp.concatenate([zeros(NOPE + half), ones, zeros(HP - QK_DIM)]).reshape(1, HP)
    tables = _rope_tables_call(positions.reshape(S, 1), invf, mask_a, mask_b)

    even_w = {}

    def even_weights_of(e):
        if e not in even_w:
            even_w[e] = _even_weights(full['even_w_in'][e], full['w_uq'][e], full['w_ukv'][e], full['even_w_out'][e],
                                      q_norm[e], kv_norm[e], sg_norm[e], sg_w[e], sg_b[e])
        return even_w[e]

    def gain_row(arr, l):
        return arr[l].reshape(1, D)

    saved = []
    h = _rmsnorm_call("first_norm", xs, gain_row(ffn_pre_norm, 0))
    xc = xs
    for l in range(depth):
        xc, h, s_pre = _ffn_fwd(f"l{l}_pre", xc, h, full['ffn_pre_w_gate'][l], full['ffn_pre_w_up'][l],
                                full['ffn_pre_w_down'][l], gain_row(mix_norm, l))
        if l % 2 == 0:
            xc, h, s_mix, gathered = _even_mixer_fwd(f"l{l}_mix", xc, h, even_weights_of(l // 2), tables,
                                                     gain_row(ffn_post_norm, l), late_shards if l == 0 else ())
            if l == 0:
                unpack(gathered, late_keys)
        else:
            o = l // 2
            xc, h, s_mix = _conv_mixer_fwd(f"l{l}_mix", xc, h, full['conv_w_in'][o], taps[o],
                                           full['conv_w_out'][o], gain_row(ffn_post_norm, l))
        g_next = gain_row(ffn_pre_norm, l + 1) if l + 1 < depth else final_norm.reshape(1, D)
        xc, h, s_post = _ffn_fwd(f"l{l}_post", xc, h, full['ffn_post_w_gate'][l], full['ffn_post_w_up'][l],
                                 full['ffn_post_w_down'][l], g_next)
        saved.append((s_pre, s_mix, s_post))

    dx, dxb, d_final, loss_part = _loss_call(xc, target, final_norm.reshape(1, D))
    loss = lax.psum(loss_part[0, 0], ("x", "y", "c"))

    gl = {n: [None] * w_loc[n].shape[0] for n in WEIGHTS if n != 'final_norm'}
    core = lax.axis_index("c").astype(jnp.int32).reshape(1)

    def pair_sums(first, tag):
        keys = [[(n, l) for n in group for l in layers_of(n, first)] for group in (GROUP_A, GROUP_B, GROUP_C)]

        def rows_blocked(n, l):
            g = gl[n][l]
            return g.reshape(4, g.shape[0] // 4, g.shape[1]).astype(_WIRE_DTYPE)

        pack_a = jnp.concatenate([gl[n][l] for n, l in keys[0]], axis=1)
        pack_b = jnp.concatenate([gl[n][l] if n in FFN_WEIGHTS else rows_blocked(n, l) for n, l in keys[1]], axis=1)
        pack_c = jnp.stack([_pad_rows(jnp.concatenate(
            [_shard_slice(gl[n][l], SHARD_AXIS[n] - 1, b).astype(_WIRE_DTYPE).reshape(-1) for n, l in keys[2]]),
            PACK_ROW_MULT) for b in range(4)])
        packs = [_pad_axis(p, 1, PACK_ROW_MULT) for p in (pack_a, pack_b, pack_c)]
        packs = [p.reshape(4, 2, p.shape[1] // 2, p.shape[2]) for p in packs]
        theirs = _pair_exchange_call(packs, tag)
        return [_pair_add_call(f"pair_add_{tag}_{i}", p, t, core)
                for i, (p, t) in enumerate(zip(packs, theirs))], keys
    for l in reversed(range(depth)):
        s_pre, s_mix, s_post = saved[l]
        dx, dxb, dgain, dwg, dwu, dwd = _ffn_bwd(f"l{l}_post", s_post, dx, dxb, full['ffn_post_w_gate'][l],
                                                 full['ffn_post_w_up'][l], full['ffn_post_w_down'][l],
                                                 gain_row(ffn_post_norm, l))
        gl['ffn_post_norm'][l] = dgain[0]
        gl['ffn_post_w_gate'][l], gl['ffn_post_w_up'][l], gl['ffn_post_w_down'][l] = dwg, dwu, dwd
        if l % 2 == 0:
            e = l // 2
            if l == 0:
                pairs_rest, keys_rest = pair_sums(False, "rest")
            dx, dxb, dgain, eg, arrived = _even_mixer_bwd(f"l{l}_mix", s_mix, dx, dxb, even_weights_of(e), tables,
                                                          gain_row(mix_norm, l), pairs_rest if l == 0 else ())
            if l == 0:
                arrived_rest = arrived
            for n, val in _even_grads_unpad(eg).items():
                gl[n][e] = val
        else:
            o = l // 2
            dx, dxb, dgain, dw_in, dtaps, dw_out = _conv_mixer_bwd(f"l{l}_mix", s_mix, dx, dxb, full['conv_w_in'][o],
                                                                   taps[o], full['conv_w_out'][o],
                                                                   gain_row(mix_norm, l))
            gl['conv_w_in'][o], gl['conv_w'][o], gl['conv_w_out'][o] = dw_in, dtaps, dw_out
        gl['mix_norm'][l] = dgain[0]
        dx, dxb, dgain, dwg, dwu, dwd = _ffn_bwd(f"l{l}_pre", s_pre, dx, dxb, full['ffn_pre_w_gate'][l],
                                                 full['ffn_pre_w_up'][l], full['ffn_pre_w_down'][l],
                                                 gain_row(ffn_pre_norm, l))
        gl['ffn_pre_norm'][l] = dgain[0]
        gl['ffn_pre_w_gate'][l], gl['ffn_pre_w_up'][l], gl['ffn_pre_w_down'][l] = dwg, dwu, dwd
    grad_x = dx.reshape(x.shape)
    part = {n: jnp.stack(gl[n]) for n in gl if n not in FFN_WEIGHTS}
    part['final_norm'] = d_final[0]

    pairs, keys_first = pair_sums(True, "first")
    arrived_first = _chip_scatter_call(pairs)
    mine = [_sum_slots_call(f"sum_grad_slots_{i}", r, core) for i, r in enumerate(list(arrived_rest) + list(arrived_first))]
    reduced = [t.reshape(-1, t.shape[2]) for t in _sibling_share_call(mine)]
    per_layer = {n: {} for n in SHARDED}
    for (red_a, red_b, red_c), keys in ((reduced[:3], keys_rest), (reduced[3:], keys_first)):
        for idx, (n, l) in enumerate(keys[0]):
            per_layer[n][l] = red_a[idx * D:(idx + 1) * D]
        row = 0
        for n, l in keys[1]:
            rows = w_loc[n].shape[1]
            per_layer[n][l] = red_b[row:row + rows]
            row += rows
        red_c = red_c.reshape(-1)
        off = 0
        for n, l in keys[2]:
            shp = w_loc[n].shape[1:]
            size = int(np.prod(shp))
            per_layer[n][l] = red_c[off:off + size].reshape(shp)
            off += size
    grads = {n: jnp.stack([per_layer[n][l] for l in range(w_loc[n].shape[0])]) for n in SHARDED}

    small = _pad_rows(jnp.concatenate([part[n].reshape(-1) for n in REPLICATED]), 8)
    small_sum = _allreduce_small_call(small).reshape(-1)
    off = 0
    for n in REPLICATED:
        size = int(np.prod(w_loc[n].shape))
        grads[n] = small_sum[off:off + size].reshape(w_loc[n].shape)
        off += size

    deltas, new_m, new_v = {}, {}, {}
    for n in WEIGHTS:
        deltas[n], new_m[n], new_v[n] = _adamw_call("adamw_" + n, w_loc[n], grads[n], m_loc[n], v_loc[n])
    return (loss, grad_x, *[grads[n] for n in WEIGHTS], *[deltas[n] for n in WEIGHTS],
            *[new_m[n] for n in WEIGHTS], *[new_v[n] for n in WEIGHTS])
```

```python
import functools

import numpy as np
import jax
import jax.numpy as jnp
from jax import lax
from jax.experimental import pallas as pl
from jax.experimental.pallas import tpu as pltpu

F32 = jnp.float32
BF16 = jnp.bfloat16
_MXU_DTYPE = jnp.bfloat16
_WIRE_DTYPE = jnp.bfloat16
_VMEM_LIMIT = 52 * 1024 * 1024
_LANES = 128
_ATT_BLOCK = 512
_ROW_TILE = 512
_SG_TILE = 1024

NORM_EPS = 1e-6
HEADS = 8
NOPE = 64
ROPE = 32
VDIM = 64
QK_DIM = NOPE + ROPE
HP = 128
Q_LORA = 384
KV_LORA = 256
SG_WIDTH = 512
SG_GROUPS = 8
SG_GDIM = 64
SG_CHUNK = 128
ROPE_THETA = 10000.0
PROJ_W = Q_LORA + KV_LORA + HP + 2 * SG_WIDTH
ADAM_LR = 0.001
ADAM_B1 = 0.9
ADAM_B2 = 0.999
ADAM_EPS = 1e-08
ADAM_WD = 0.01
ADAM_STEP = 10
MESH = pl.DeviceIdType.MESH
PACK_COLS = 1024
PACK_ROW_MULT = 256

SHARDED = ['ffn_pre_w_gate', 'ffn_pre_w_up', 'ffn_pre_w_down', 'ffn_post_w_gate', 'ffn_post_w_up',
           'ffn_post_w_down', 'even_w_in', 'w_uq', 'w_ukv', 'even_w_out', 'conv_w_in', 'conv_w', 'conv_w_out']
SHARD_AXIS = {'ffn_pre_w_gate': 2, 'ffn_pre_w_up': 2, 'ffn_pre_w_down': 1, 'ffn_post_w_gate': 2,
              'ffn_post_w_up': 2, 'ffn_post_w_down': 1, 'even_w_in': 2, 'w_uq': 2, 'w_ukv': 2,
              'even_w_out': 1, 'conv_w_in': 2, 'conv_w': 2, 'conv_w_out': 1}
FFN_WEIGHTS = ['ffn_pre_w_gate', 'ffn_pre_w_up', 'ffn_pre_w_down', 'ffn_post_w_gate', 'ffn_post_w_up',
               'ffn_post_w_down']
GROUP_A = ['ffn_pre_w_gate', 'ffn_pre_w_up', 'ffn_post_w_gate', 'ffn_post_w_up']
GROUP_B = ['ffn_pre_w_down', 'ffn_post_w_down', 'even_w_out', 'conv_w_out']
GROUP_C = ['even_w_in', 'w_uq', 'w_ukv', 'conv_w_in', 'conv_w']
GATHER_C = ['even_w_in', 'w_uq', 'w_ukv', 'conv_w_in']
REPLICATED = ['ffn_pre_norm', 'mix_norm', 'ffn_post_norm', 'q_norm', 'kv_norm', 'sg_norm', 'sg_w', 'sg_b',
              'final_norm']
WEIGHTS = ['ffn_pre_norm', 'ffn_pre_w_gate', 'ffn_pre_w_up', 'ffn_pre_w_down', 'mix_norm', 'ffn_post_norm',
           'ffn_post_w_gate', 'ffn_post_w_up', 'ffn_post_w_down', 'even_w_in', 'q_norm', 'w_uq', 'kv_norm',
           'w_ukv', 'sg_norm', 'sg_w', 'sg_b', 'even_w_out', 'conv_w_in', 'conv_w', 'conv_w_out', 'final_norm']


def _params(sem=None):
    return pltpu.CompilerParams(vmem_limit_bytes=_VMEM_LIMIT,
                                **({} if sem is None else {'dimension_semantics': sem}))


def _pick(n, pref, mult):
    best = None
    t = mult
    while t <= min(n, pref):
        if n % t == 0:
            best = t
        t += mult
    return n if best is None else best


def _mx(v):
    return v if v.dtype == _MXU_DTYPE else v.astype(_MXU_DTYPE)


def _sigmoid(a):
    return 1.0 / (1.0 + jnp.exp(-a))


def _rms_stats(x):
    rstd = lax.rsqrt(jnp.mean(x * x, axis=-1, keepdims=True) + NORM_EPS)
    return x * rstd, rstd


def _rms_bwd(x, g, dh):
    xhat, rstd = _rms_stats(x)
    gdh = g * dh
    dx = rstd * (gdh - xhat * jnp.mean(gdh * xhat, axis=-1, keepdims=True))
    return dx, dh * xhat


def _fused_matmul(name, mode, lhs, rhs, prods, n_acc, epilogue, out_dtypes, M, N, K, tm, tn, tk,
                  tile_extras=(), row_extras=(), mrow_extras=(), n_colsum=0):
    gj, gi, gk = N // tn, M // tm, K // tk
    assert gj * tn == N and gi * tm == M and gk * tk == K, (name, M, N, K, tm, tn, tk)
    dims = {'nn': (((1,), (0,)), ((), ())), 'nt': (((1,), (1,)), ((), ())), 'tn': (((0,), (0,)), ((), ()))}[mode]

    def lhs_spec(roff, coff, kb):
        kb = tk if kb is None else kb
        if mode == 'tn':
            return pl.BlockSpec((kb, tm), lambda j, i, k: (k + roff, i + coff))
        return pl.BlockSpec((tm, kb), lambda j, i, k: (i + roff, k + coff))

    def rhs_spec(roff, coff, kb):
        kb = tk if kb is None else kb
        if mode == 'nt':
            return pl.BlockSpec((tn, kb), lambda j, i, k: (j + roff, k + coff))
        return pl.BlockSpec((kb, tn), lambda j, i, k: (k + roff, j + coff))

    in_specs = [lhs_spec(*a[1:]) for a in lhs] + [rhs_spec(*a[1:]) for a in rhs]
    in_specs += [pl.BlockSpec((tm, tn), lambda j, i, k: (i, j)) for _ in tile_extras]
    in_specs += [pl.BlockSpec((1, tn), lambda j, i, k: (0, j)) for _ in row_extras]
    in_specs += [pl.BlockSpec((tm, a.shape[1]), lambda j, i, k: (i, 0)) for a in mrow_extras]
    n_out = len(out_dtypes)
    out_shape = [jax.ShapeDtypeStruct((M, N), d) for d in out_dtypes]
    out_specs = [pl.BlockSpec((tm, tn), lambda j, i, k: (i, j)) for _ in out_dtypes]
    out_shape += [jax.ShapeDtypeStruct((1, N), F32) for _ in range(n_colsum)]
    out_specs += [pl.BlockSpec((1, tn), lambda j, i, k: (0, j)) for _ in range(n_colsum)]
    scratch = [pltpu.VMEM((tm, tn), F32) for _ in range(n_acc)] if gk > 1 else []
    nl, nr, nt, nrw, nm = len(lhs), len(rhs), len(tile_extras), len(row_extras), len(mrow_extras)

    def body(*refs):
        pos = 0
        lhs_refs = refs[pos:pos + nl]; pos += nl
        rhs_refs = refs[pos:pos + nr]; pos += nr
        tile_refs = refs[pos:pos + nt]; pos += nt
        row_refs = refs[pos:pos + nrw]; pos += nrw
        mrow_refs = refs[pos:pos + nm]; pos += nm
        out_refs = refs[pos:pos + n_out]; pos += n_out
        cs_refs = refs[pos:pos + n_colsum]; pos += n_colsum
        acc_refs = refs[pos:]
        i = pl.program_id(1)
        k = pl.program_id(2)

        def partials():
            res = [None] * n_acc
            for (li, ri, ai) in prods:
                d = lax.dot_general(_mx(lhs_refs[li][...]), _mx(rhs_refs[ri][...]), dims,
                                    preferred_element_type=F32)
                res[ai] = d if res[ai] is None else res[ai] + d
            return res

        def finish(accs):
            outs = epilogue(accs, [r[...] for r in tile_refs], [r[...] for r in row_refs],
                            [r[...] for r in mrow_refs])
            for r, o in zip(out_refs, outs[:n_out]):
                r[...] = o.astype(r.dtype)
            for r, c in zip(cs_refs, outs[n_out:]):
                c = jnp.sum(c, axis=0, keepdims=True)

                @pl.when(i == 0)
                def _():
                    r[...] = c

                @pl.when(i != 0)
                def _():
                    r[...] += c

        if gk == 1:
            finish(partials())
        else:
            p = partials()

            @pl.when(k == 0)
            def _():
                for r, v in zip(acc_refs, p):
                    r[...] = v

            @pl.when(k != 0)
            def _():
                for r, v in zip(acc_refs, p):
                    r[...] += v

            @pl.when(k == gk - 1)
            def _():
                finish([r[...] for r in acc_refs])

    res = pl.pallas_call(
        body, name=name, grid=(gj, gi, gk), in_specs=in_specs, out_specs=out_specs, out_shape=out_shape,
        scratch_shapes=scratch, compiler_params=_params(("arbitrary", "arbitrary", "arbitrary")),
    )(*[a[0] for a in lhs], *[a[0] for a in rhs], *tile_extras, *row_extras, *mrow_extras)
    return res


def _op(a, roff=0, coff=0, kb=None):
    return (a, roff, coff, kb)


def _ident_epi(scale=None):
    def epi(accs, tiles, rows, mrows):
        return [a if scale is None else a * scale for a in accs]
    return epi


def _resid_norm_epi(scale):
    def epi(accs, tiles, rows, mrows):
        x_new = tiles[0] + scale * accs[0]
        xhat, _ = _rms_stats(x_new)
        return [x_new, xhat * rows[0]]
    return epi


def _norm_bwd_epi(accs, tiles, rows, mrows):
    dx_n, dg = _rms_bwd(tiles[0], rows[0], accs[0])
    dx = tiles[1] + dx_n
    return [dx, dx, dg]


def _rmsnorm_call(name, x, g):
    S, D = x.shape
    tm = _pick(S, _ROW_TILE, 128)

    def body(x_ref, g_ref, h_ref, ht_ref):
        h = _rms_stats(x_ref[...])[0] * g_ref[...]
        h_ref[...] = h.astype(h_ref.dtype)
        ht_ref[...] = jnp.transpose(h).astype(ht_ref.dtype)

    return pl.pallas_call(
        body, name=name, grid=(S // tm,),
        in_specs=[pl.BlockSpec((tm, D), lambda i: (i, 0)), pl.BlockSpec((1, D), lambda i: (0, 0))],
        out_specs=[pl.BlockSpec((tm, D), lambda i: (i, 0)), pl.BlockSpec((D, tm), lambda i: (0, i))],
        out_shape=[jax.ShapeDtypeStruct((S, D), BF16), jax.ShapeDtypeStruct((D, S), BF16)],
        compiler_params=_params(("arbitrary",)),
    )(x, g)


def _loss_call(x, target, g):
    S, D = x.shape
    tm = _pick(S, _ROW_TILE, 8)

    def body(x_ref, t_ref, g_ref, dx_ref, dxb_ref, dg_ref, loss_ref):
        i = pl.program_id(0)
        x_t = x_ref[...]
        gain = g_ref[...]
        xhat, _ = _rms_stats(x_t)
        diff = xhat * gain - t_ref[...]
        dy = diff * (1.0 / D)
        dx, dg = _rms_bwd(x_t, gain, dy)
        dx_ref[...] = dx
        dxb_ref[...] = dx.astype(BF16)
        dg = jnp.sum(dg, axis=0, keepdims=True)
        part = 0.5 * jnp.sum(jnp.sum(diff * diff, axis=1, keepdims=True), axis=0, keepdims=True) * (1.0 / D)
        part = jnp.broadcast_to(part, (1, _LANES))

        @pl.when(i == 0)
        def _():
            dg_ref[...] = dg
            loss_ref[...] = part

        @pl.when(i != 0)
        def _():
            dg_ref[...] += dg
            loss_ref[...] += part

    row = lambda i: (i, 0)
    fixed = lambda i: (0, 0)
    return pl.pallas_call(
        body, name="loss_head", grid=(S // tm,),
        in_specs=[pl.BlockSpec((tm, D), row), pl.BlockSpec((tm, D), row), pl.BlockSpec((1, D), fixed)],
        out_specs=[pl.BlockSpec((tm, D), row), pl.BlockSpec((tm, D), row), pl.BlockSpec((1, D), fixed),
                   pl.BlockSpec((1, _LANES), fixed)],
        out_shape=[jax.ShapeDtypeStruct((S, D), F32), jax.ShapeDtypeStruct((S, D), BF16),
                   jax.ShapeDtypeStruct((1, D), F32), jax.ShapeDtypeStruct((1, _LANES), F32)],
        compiler_params=_params(("arbitrary",)),
    )(x, target, g)


def _rope_tables_call(pos_col, invf, mask_a, mask_b):
    S = pos_col.shape[0]
    tm = _pick(S, _ROW_TILE, 8)

    def body(p_ref, f_ref, a_ref, b_ref, cos_ref, sa_ref, sb_ref):
        ang = p_ref[...].astype(F32) * f_ref[...]
        sn = jnp.sin(ang)
        cos_ref[...] = jnp.cos(ang)
        sa_ref[...] = sn * a_ref[...]
        sb_ref[...] = sn * b_ref[...]

    row = lambda i: (i, 0)
    fixed = lambda i: (0, 0)
    return pl.pallas_call(
        body, name="rope_tables", grid=(S // tm,),
        in_specs=[pl.BlockSpec((tm, 1), row)] + [pl.BlockSpec((1, HP), fixed)] * 3,
        out_specs=[pl.BlockSpec((tm, HP), row)] * 3,
        out_shape=[jax.ShapeDtypeStruct((S, HP), F32)] * 3, compiler_params=_params(("arbitrary",)),
    )(pos_col, invf, mask_a, mask_b)


def _rope(t, cos, sa, sb):
    return t * cos + pltpu.roll(t, HP - ROPE // 2, 1) * sa + pltpu.roll(t, ROPE // 2, 1) * sb


def _rope_t(d, cos, sa, sb):
    return d * cos + pltpu.roll(d * sa, ROPE // 2, 1) + pltpu.roll(d * sb, HP - ROPE // 2, 1)


def _gelu(z):
    return 0.5 * z * (1.0 + lax.erf(z * np.float32(1.0 / np.sqrt(2.0))))


def _gelu_grad(z):
    cdf = 0.5 * (1.0 + lax.erf(z * np.float32(1.0 / np.sqrt(2.0))))
    pdf = np.float32(1.0 / np.sqrt(2.0 * np.pi)) * jnp.exp(-0.5 * z * z)
    return cdf + z * pdf


_CQ0, _CKV0, _KR0, _Z0 = 0, Q_LORA, Q_LORA + KV_LORA, Q_LORA + KV_LORA + HP


def _even_prep_call(proj, qn, kvn, sgn, cos, sa, sb):
    S = proj.shape[0]
    tm = _pick(S, 256, 8)

    def body(p_ref, qn_ref, kvn_ref, sgn_ref, cos_ref, sa_ref, sb_ref, cq_ref, ckv_ref, kr_ref, u_ref, v_ref):
        cq = p_ref[:, _CQ0:_CQ0 + Q_LORA]
        cq_ref[...] = (_rms_stats(cq)[0] * qn_ref[...]).astype(BF16)
        ckv = p_ref[:, _CKV0:_CKV0 + KV_LORA]
        ckv_ref[...] = (_rms_stats(ckv)[0] * kvn_ref[...]).astype(BF16)
        kr = p_ref[:, _KR0:_KR0 + HP]
        kr_ref[...] = _rope(kr, cos_ref[...], sa_ref[...], sb_ref[...]).astype(BF16)
        u_ref[...] = _gelu(p_ref[:, _Z0:_Z0 + SG_WIDTH]).astype(BF16)
        zv = _gelu(p_ref[:, _Z0 + SG_WIDTH:_Z0 + 2 * SG_WIDTH])
        v_ref[...] = (_rms_stats(zv)[0] * sgn_ref[...]).astype(BF16)

    row = lambda i: (i, 0)
    fixed = lambda i: (0, 0)
    widths = [Q_LORA, KV_LORA, HP, SG_WIDTH, SG_WIDTH]
    return pl.pallas_call(
        body, name="even_prep", grid=(S // tm,),
        in_specs=[pl.BlockSpec((tm, PROJ_W), row), pl.BlockSpec((1, Q_LORA), fixed),
                  pl.BlockSpec((1, KV_LORA), fixed), pl.BlockSpec((1, SG_WIDTH), fixed)]
        + [pl.BlockSpec((tm, HP), row)] * 3,
        out_specs=[pl.BlockSpec((tm, w), row) for w in widths],
        out_shape=[jax.ShapeDtypeStruct((S, w), BF16) for w in widths],
        compiler_params=_params(("arbitrary",)),
    )(proj, qn, kvn, sgn, cos, sa, sb)


def _even_prep_bwd_call(proj, qn, kvn, sgn, cos, sa, sb, dcqn, dckvn, dk, du, dvn):
    S = proj.shape[0]
    tm = _pick(S, 256, 8)

    def body(p_ref, qn_ref, kvn_ref, sgn_ref, cos_ref, sa_ref, sb_ref, dcq_ref, dckv_ref, dk_ref, du_ref,
             dvn_ref, dp_ref, dqn_ref, dkvn_ref, dsgn_ref):
        i = pl.program_id(0)
        dcq, gq = _rms_bwd(p_ref[:, _CQ0:_CQ0 + Q_LORA], qn_ref[...], dcq_ref[...])
        dp_ref[:, _CQ0:_CQ0 + Q_LORA] = dcq.astype(BF16)
        dckv, gkv = _rms_bwd(p_ref[:, _CKV0:_CKV0 + KV_LORA], kvn_ref[...], dckv_ref[...])
        dp_ref[:, _CKV0:_CKV0 + KV_LORA] = dckv.astype(BF16)
        dkr = dk_ref[:, 0:HP].astype(F32)
        for h in range(1, HEADS):
            dkr = dkr + dk_ref[:, h * HP:(h + 1) * HP].astype(F32)
        lane = lax.broadcasted_iota(jnp.int32, dkr.shape, 1)
        dkr = jnp.where((lane >= NOPE) & (lane < QK_DIM), dkr, 0.0)
        dp_ref[:, _KR0:_KR0 + HP] = _rope_t(dkr, cos_ref[...], sa_ref[...], sb_ref[...]).astype(BF16)
        zu = p_ref[:, _Z0:_Z0 + SG_WIDTH]
        dp_ref[:, _Z0:_Z0 + SG_WIDTH] = (du_ref[...].astype(F32) * _gelu_grad(zu)).astype(BF16)
        zv = p_ref[:, _Z0 + SG_WIDTH:_Z0 + 2 * SG_WIDTH]
        dgv, gsg = _rms_bwd(_gelu(zv), sgn_ref[...], dvn_ref[...].astype(F32))
        dp_ref[:, _Z0 + SG_WIDTH:_Z0 + 2 * SG_WIDTH] = (dgv * _gelu_grad(zv)).astype(BF16)
        sums = [jnp.sum(t, axis=0, keepdims=True) for t in (gq, gkv, gsg)]

        @pl.when(i == 0)
        def _():
            for r, s in zip((dqn_ref, dkvn_ref, dsgn_ref), sums):
                r[...] = s

        @pl.when(i != 0)
        def _():
            for r, s in zip((dqn_ref, dkvn_ref, dsgn_ref), sums):
                r[...] += s

    row = lambda i: (i, 0)
    fixed = lambda i: (0, 0)
    return pl.pallas_call(
        body, name="even_prep_bwd", grid=(S // tm,),
        in_specs=[pl.BlockSpec((tm, PROJ_W), row), pl.BlockSpec((1, Q_LORA), fixed),
                  pl.BlockSpec((1, KV_LORA), fixed), pl.BlockSpec((1, SG_WIDTH), fixed)]
        + [pl.BlockSpec((tm, HP), row)] * 3
        + [pl.BlockSpec((tm, Q_LORA), row), pl.BlockSpec((tm, KV_LORA), row),
           pl.BlockSpec((tm, HEADS * HP), row), pl.BlockSpec((tm, SG_WIDTH), row),
           pl.BlockSpec((tm, SG_WIDTH), row)],
        out_specs=[pl.BlockSpec((tm, PROJ_W), row), pl.BlockSpec((1, Q_LORA), fixed),
                   pl.BlockSpec((1, KV_LORA), fixed), pl.BlockSpec((1, SG_WIDTH), fixed)],
        out_shape=[jax.ShapeDtypeStruct((S, PROJ_W), BF16), jax.ShapeDtypeStruct((1, Q_LORA), F32),
                   jax.ShapeDtypeStruct((1, KV_LORA), F32), jax.ShapeDtypeStruct((1, SG_WIDTH), F32)],
        compiler_params=_params(("arbitrary",)),
    )(proj, qn, kvn, sgn, cos, sa, sb, dcqn, dckvn, dk, du, dvn)


def _causal_mask(rows, cols):
    r = lax.broadcasted_iota(jnp.int32, (rows, cols), 0)
    c = lax.broadcasted_iota(jnp.int32, (rows, cols), 1)
    return c <= r


def _flash_fwd_call(q, k, v, carry=()):
    S = q.shape[0]
    tb = _pick(S, _ATT_BLOCK, 128)
    nq = S // tb
    nt_dims = (((1,), (1,)), ((), ()))

    nc = len(carry)

    def body(*refs):
        q_ref, k_ref, v_ref = refs[:3]
        o_ref, lse_ref = refs[3 + nc:5 + nc]
        s_a, s_b, m_ref, acc_ref = refs[5 + 2 * nc:9 + 2 * nc]
        h = pl.program_id(0)
        i = pl.program_id(1)
        if nc:
            send, forward, finish = _gather_phases(refs[3:3 + nc], refs[5 + nc:5 + 2 * nc], *refs[9 + 2 * nc:])
            pl.when((h == 0) & (i == 0))(send)
            pl.when((h == HEADS // 2) & (i == 0))(forward)

        def scores(buf, j):
            k_t = k_ref[pl.ds(pl.multiple_of(j * tb, tb), tb), :]
            buf[...] = lax.dot_general(q_ref[...], k_t, nt_dims, preferred_element_type=F32)

        def update(buf, j, masked):
            v_t = v_ref[pl.ds(pl.multiple_of(j * tb, tb), tb), :]
            s = buf[...]
            if masked:
                s = jnp.where(_causal_mask(tb, tb), s, -1e30)
            m = m_ref[...]
            m_new = jnp.maximum(m, jnp.max(s, axis=1, keepdims=True))
            alpha = jnp.exp(m - m_new)
            p = jnp.exp(s - m_new)
            acc_ref[...] = alpha * acc_ref[...] + jnp.dot(p.astype(v_t.dtype), v_t, preferred_element_type=F32)
            m_ref[...] = m_new

        m_ref[...] = jnp.full((tb, 1), -1e30, F32)
        acc_ref[...] = jnp.zeros((tb, HP), F32)
        scores(s_a, 0)
        pairs = i // 2

        def two_blocks(t, carry):
            scores(s_b, 2 * t + 1)
            update(s_a, 2 * t, False)
            scores(s_a, 2 * t + 2)
            update(s_b, 2 * t + 1, False)
            return carry

        lax.fori_loop(0, pairs, two_blocks, 0)

        @pl.when(2 * pairs == i)
        def _():
            update(s_a, i, True)

        @pl.when(2 * pairs != i)
        def _():
            scores(s_b, i)
            update(s_a, i - 1, False)
            update(s_b, i, True)

        acc = acc_ref[...]
        l = acc[:, VDIM:VDIM + 1]
        lane = lax.broadcasted_iota(jnp.int32, (tb, HP), 1)
        o_ref[...] = jnp.where(lane < VDIM, acc / l, 0.0).astype(o_ref.dtype)
        lse = jnp.broadcast_to(m_ref[...] + jnp.log(l), (tb, HP))
        lse_ref[0, 0] = jnp.transpose(lse)[0:8, :]
        if nc:
            pl.when((h == HEADS - 1) & (i == nq - 1))(finish)

    start = _gather_start(carry)
    any_spec = pl.BlockSpec(memory_space=pl.ANY)
    res = pl.pallas_call(
        body, name="flash_fwd_gather" if nc else "flash_fwd", grid=(HEADS, nq),
        in_specs=[pl.BlockSpec((tb, HP), lambda h, i: (i, h)), pl.BlockSpec((S, HP), lambda h, i: (0, h)),
                  pl.BlockSpec((S, HP), lambda h, i: (0, h))] + [any_spec] * nc,
        out_specs=[pl.BlockSpec((tb, HP), lambda h, i: (i, h)),
                   pl.BlockSpec((1, 1, 8, tb), lambda h, i: (h, i, 0, 0))] + [any_spec] * nc,
        out_shape=[jax.ShapeDtypeStruct((S, HEADS * HP), q.dtype), jax.ShapeDtypeStruct((HEADS, nq, 8, tb), F32)]
        + [jax.ShapeDtypeStruct(t.shape, t.dtype) for t in start],
        scratch_shapes=[pltpu.VMEM((tb, tb), F32), pltpu.VMEM((tb, tb), F32), pltpu.VMEM((tb, 1), F32),
                        pltpu.VMEM((tb, HP), F32)] + ([pltpu.SemaphoreType.DMA((n,)) for n in _gather_sems(nc)]
                                                      if nc else []),
        input_output_aliases={3 + a: 2 + a for a in range(nc)},
        compiler_params=pltpu.CompilerParams(vmem_limit_bytes=_VMEM_LIMIT, has_side_effects=bool(nc),
                                             dimension_semantics=("arbitrary", "arbitrary")),
    )(q, k, v, *start)
    return res[0], res[1], list(res[2:])


def _attn_delta_call(o, do):
    S = o.shape[0]
    tb = _pick(S, _ATT_BLOCK, 128)
    nq = S // tb
    nb = _pick(nq, 4, 1)

    def body(o_ref, do_ref, d_ref):
        for r in range(nb):
            rows = slice(r * tb, (r + 1) * tb)
            d = jnp.sum(o_ref[rows, :].astype(F32) * do_ref[rows, :].astype(F32), axis=1, keepdims=True)
            d_ref[0, r] = jnp.transpose(jnp.broadcast_to(d, (tb, HP)))[0:8, :]

    return pl.pallas_call(
        body, name="attn_delta", grid=(HEADS, nq // nb),
        in_specs=[pl.BlockSpec((nb * tb, HP), lambda h, i: (i, h))] * 2,
        out_specs=pl.BlockSpec((1, nb, 8, tb), lambda h, i: (h, i, 0, 0)),
        out_shape=jax.ShapeDtypeStruct((HEADS, nq, 8, tb), F32), compiler_params=_params(("arbitrary", "arbitrary")),
    )(o, do)


def _flash_bwd_call(q, k, v, do, lse, delta, carry=()):
    S = q.shape[0]
    tb = _pick(S, _ATT_BLOCK, 128)
    nq = S // tb
    nt_dims = (((1,), (1,)), ((), ()))
    tn_dims = (((0,), (0,)), ((), ()))
    nc = len(carry)

    def body(*refs):
        q_ref, do_ref, lse_ref, dl_ref, k_ref, v_ref = refs[:6]
        dq_ref, dk_ref, dv_ref = refs[6 + nc:9 + nc]
        st_a, dp_a, st_b, dp_b, dk_acc, dv_acc = refs[9 + 2 * nc:15 + 2 * nc]
        h = pl.program_id(0)
        j = pl.program_id(1)
        if nc:
            send, finish = _scatter_phases(refs[6:6 + nc], refs[9 + nc:9 + 2 * nc], *refs[15 + 2 * nc:])
            pl.when((h == 0) & (j == 0))(send)

        @pl.when(j == 0)
        def _():
            dq_ref[...] = jnp.zeros_like(dq_ref)

        def rows_of(i):
            return pl.ds(pl.multiple_of(i * tb, tb), tb)

        def scores(st_buf, dp_buf, i):
            st_buf[...] = lax.dot_general(k_ref[...], q_ref[rows_of(i), :], nt_dims, preferred_element_type=F32)
            dp_buf[...] = lax.dot_general(v_ref[...], do_ref[rows_of(i), :], nt_dims, preferred_element_type=F32)

        def update(st_buf, dp_buf, i, masked):
            q_t = q_ref[rows_of(i), :]
            do_t = do_ref[rows_of(i), :]
            pt = jnp.exp(st_buf[...] - lse_ref[0, i, 0:1, :])
            if masked:
                pt = jnp.where(jnp.transpose(_causal_mask(tb, tb)), pt, 0.0)
            dst = (pt * (dp_buf[...] - dl_ref[0, i, 0:1, :])).astype(q_t.dtype)
            dv_acc[...] += jnp.dot(pt.astype(do_t.dtype), do_t, preferred_element_type=F32)
            dk_acc[...] += jnp.dot(dst, q_t, preferred_element_type=F32)
            dq_ref[rows_of(i), :] += lax.dot_general(dst, k_ref[...], tn_dims, preferred_element_type=F32)

        last = nq - 1
        dk_acc[...] = jnp.zeros((tb, HP), F32)
        dv_acc[...] = jnp.zeros((tb, HP), F32)
        scores(st_b, dp_b, j)
        scores(st_a, dp_a, jnp.minimum(j + 1, last))
        update(st_b, dp_b, j, True)
        rest = last - j
        pairs = rest // 2

        def two_blocks(t, carry):
            i0 = j + 1 + 2 * t
            scores(st_b, dp_b, i0 + 1)
            update(st_a, dp_a, i0, False)
            scores(st_a, dp_a, jnp.minimum(i0 + 2, last))
            update(st_b, dp_b, i0 + 1, False)
            return carry

        lax.fori_loop(0, pairs, two_blocks, 0)

        @pl.when(2 * pairs != rest)
        def _():
            update(st_a, dp_a, last, False)

        dk_ref[...] = dk_acc[...].astype(dk_ref.dtype)
        dv_ref[...] = dv_acc[...].astype(dv_ref.dtype)
        if nc:
            pl.when((h == HEADS - 1) & (j == nq - 1))(finish)

    head = lambda h, j: (0, h)
    blk = lambda h, j: (j, h)
    rows = lambda h, j: (h, 0, 0, 0)
    any_spec = pl.BlockSpec(memory_space=pl.ANY)
    res = pl.pallas_call(
        body, name="flash_bwd_scatter" if nc else "flash_bwd", grid=(HEADS, nq),
        in_specs=[pl.BlockSpec((S, HP), head), pl.BlockSpec((S, HP), head), pl.BlockSpec((1, nq, 8, tb), rows),
                  pl.BlockSpec((1, nq, 8, tb), rows), pl.BlockSpec((tb, HP), blk), pl.BlockSpec((tb, HP), blk)]
        + [any_spec] * nc,
        out_specs=[pl.BlockSpec((S, HP), head), pl.BlockSpec((tb, HP), blk), pl.BlockSpec((tb, HP), blk)]
        + [any_spec] * nc,
        out_shape=[jax.ShapeDtypeStruct((S, HEADS * HP), F32), jax.ShapeDtypeStruct((S, HEADS * HP), BF16),
                   jax.ShapeDtypeStruct((S, HEADS * HP), BF16)]
        + [jax.ShapeDtypeStruct(p.shape, p.dtype) for p in carry],
        scratch_shapes=[pltpu.VMEM((tb, tb), F32)] * 4 + [pltpu.VMEM((tb, HP), F32)] * 2
        + ([pltpu.SemaphoreType.DMA((n,)) for n in _scatter_sems(nc)] if nc else []),
        compiler_params=pltpu.CompilerParams(vmem_limit_bytes=_VMEM_LIMIT, has_side_effects=bool(nc),
                                             dimension_semantics=("arbitrary", "arbitrary")),
    )(q, do, lse, delta, k, v, *carry)
    return res[0], res[1], res[2], list(res[3:])


def _sg_mixed(w_ref, vch, lane_lo):
    blocks = []
    for jb in range(SG_WIDTH // _LANES):
        r = jnp.dot(w_ref[jb], vch[:, jb * _LANES:(jb + 1) * _LANES], preferred_element_type=F32)
        blocks.append(jnp.where(lane_lo, r[0:SG_CHUNK], r[SG_CHUNK:2 * SG_CHUNK]))
    return jnp.concatenate(blocks, axis=1)


def _sgu_fwd_call(vn, u, attn, wst, bexp):
    S = vn.shape[0]
    tm = _pick(S, _SG_TILE, SG_CHUNK)
    AW = HEADS * HP

    def body(v_ref, u_ref, a_ref, w_ref, b_ref, mix_ref):
        lane_lo = lax.broadcasted_iota(jnp.int32, (SG_CHUNK, _LANES), 1) < SG_GDIM
        mix_ref[:, 0:AW] = a_ref[...]
        for c in range(tm // SG_CHUNK):
            rs = slice(c * SG_CHUNK, (c + 1) * SG_CHUNK)
            mixed = _sg_mixed(w_ref, v_ref[rs, :], lane_lo) + b_ref[...]
            mix_ref[rs, AW:AW + SG_WIDTH] = (u_ref[rs, :].astype(F32) * mixed).astype(mix_ref.dtype)

    row = lambda i: (i, 0)
    return pl.pallas_call(
        body, name="sgu_fwd", grid=(S // tm,),
        in_specs=[pl.BlockSpec((tm, SG_WIDTH), row), pl.BlockSpec((tm, SG_WIDTH), row), pl.BlockSpec((tm, AW), row),
                  pl.BlockSpec((SG_WIDTH // _LANES, 2 * SG_CHUNK, SG_CHUNK), lambda i: (0, 0, 0)),
                  pl.BlockSpec((SG_CHUNK, SG_WIDTH), lambda i: (0, 0))],
        out_specs=pl.BlockSpec((tm, AW + SG_WIDTH), row),
        out_shape=jax.ShapeDtypeStruct((S, AW + SG_WIDTH), BF16), compiler_params=_params(("arbitrary",)),
    )(vn, u, attn, wst, bexp)


def _sgu_bwd_call(dmix, vn, u, wst, wst_t, bexp):
    S = vn.shape[0]
    tm = _pick(S, _SG_TILE, SG_CHUNK)
    nblk = SG_WIDTH // _LANES
    col0 = (HEADS * HP) // SG_WIDTH
    nt_dims = (((1,), (1,)), ((), ()))

    def body(d_ref, v_ref, u_ref, w_ref, wt_ref, b_ref, du_ref, dv_ref, dw_ref, db_ref, dbacc_ref):
        i = pl.program_id(0)
        lane_lo = lax.broadcasted_iota(jnp.int32, (SG_CHUNK, _LANES), 1) < SG_GDIM

        @pl.when(i == 0)
        def _():
            dw_ref[...] = jnp.zeros_like(dw_ref)
            dbacc_ref[...] = jnp.zeros_like(dbacc_ref)

        for c in range(tm // SG_CHUNK):
            rs = slice(c * SG_CHUNK, (c + 1) * SG_CHUNK)
            vch = v_ref[rs, :]
            dsg = d_ref[rs, :].astype(F32)
            mixed = _sg_mixed(w_ref, vch, lane_lo) + b_ref[...]
            du_ref[rs, :] = (dsg * mixed).astype(du_ref.dtype)
            dmixed = dsg * u_ref[rs, :].astype(F32)
            dbacc_ref[...] += dmixed
            dmx = dmixed.astype(vch.dtype)
            dv_ref[rs, :] = _sg_mixed(wt_ref, dmx, lane_lo).astype(dv_ref.dtype)
            for jb in range(nblk):
                dblk = dmx[:, jb * _LANES:(jb + 1) * _LANES]
                vblk = vch[:, jb * _LANES:(jb + 1) * _LANES]
                zero = jnp.zeros_like(dblk)
                dw_ref[2 * jb] += lax.dot_general(jnp.where(lane_lo, dblk, zero), vblk, nt_dims,
                                                  preferred_element_type=F32)
                dw_ref[2 * jb + 1] += lax.dot_general(jnp.where(lane_lo, zero, dblk), vblk, nt_dims,
                                                      preferred_element_type=F32)

        @pl.when(i == pl.num_programs(0) - 1)
        def _():
            tri = _causal_mask(SG_CHUNK, SG_CHUNK)
            for g in range(SG_GROUPS):
                dw_ref[g] = jnp.where(tri, dw_ref[g], 0.0)
            lane = lax.broadcasted_iota(jnp.int32, (SG_CHUNK, _LANES), 1)
            out = jnp.zeros((SG_CHUNK, _LANES), F32)
            for g in range(SG_GROUPS):
                blk = dbacc_ref[:, (g // 2) * _LANES:(g // 2 + 1) * _LANES]
                sel = lane_lo if g % 2 == 0 else jnp.logical_not(lane_lo)
                s = jnp.sum(jnp.where(sel, blk, 0.0), axis=1, keepdims=True)
                out = jnp.where(lane == g, s, out)
            db_ref[...] = out

    row = lambda i: (i, 0)
    wspec = pl.BlockSpec((nblk, 2 * SG_CHUNK, SG_CHUNK), lambda i: (0, 0, 0))
    return pl.pallas_call(
        body, name="sgu_bwd", grid=(S // tm,),
        in_specs=[pl.BlockSpec((tm, SG_WIDTH), lambda i: (i, col0)), pl.BlockSpec((tm, SG_WIDTH), row),
                  pl.BlockSpec((tm, SG_WIDTH), row), wspec, wspec,
                  pl.BlockSpec((SG_CHUNK, SG_WIDTH), lambda i: (0, 0))],
        out_specs=[pl.BlockSpec((tm, SG_WIDTH), row), pl.BlockSpec((tm, SG_WIDTH), row),
                   pl.BlockSpec((SG_GROUPS, SG_CHUNK, SG_CHUNK), lambda i: (0, 0, 0)),
                   pl.BlockSpec((SG_CHUNK, _LANES), lambda i: (0, 0))],
        out_shape=[jax.ShapeDtypeStruct((S, SG_WIDTH), BF16), jax.ShapeDtypeStruct((S, SG_WIDTH), BF16),
                   jax.ShapeDtypeStruct((SG_GROUPS, SG_CHUNK, SG_CHUNK), F32),
                   jax.ShapeDtypeStruct((SG_CHUNK, _LANES), F32)],
        scratch_shapes=[pltpu.VMEM((SG_CHUNK, SG_WIDTH), F32)],
        compiler_params=_params(("arbitrary",)),
    )(dmix, vn, u, wst, wst_t, bexp)


def _shift_down(t, halo, n):
    rows = lax.broadcasted_iota(jnp.int32, t.shape, 0)
    out = pltpu.roll(t, n, 0)
    for r in range(n):
        out = jnp.where(rows == r, halo[8 - n + r:8 - n + r + 1, :], out)
    return out


def _shift_up(t, halo, n):
    tm = t.shape[0]
    rows = lax.broadcasted_iota(jnp.int32, t.shape, 0)
    out = pltpu.roll(t, tm - n, 0)
    for r in range(n):
        out = jnp.where(rows == tm - n + r, halo[r:r + 1, :], out)
    return out


def _conv_fwd_call(p, w):
    S, C3 = p.shape
    C = C3 // 3
    tm = _pick(S, _ROW_TILE, 8)
    hb = tm // 8

    def body(p_ref, c_prev, z_prev, w_ref, m_ref):
        i = pl.program_id(0)
        cz = p_ref[:, C:2 * C] * p_ref[:, 2 * C:3 * C]
        czp = jnp.where(i > 0, c_prev[...] * z_prev[...], 0.0)
        y = w_ref[2:3, :] * cz + w_ref[1:2, :] * _shift_down(cz, czp, 1) + w_ref[0:1, :] * _shift_down(cz, czp, 2)
        m_ref[...] = (p_ref[:, 0:C] * y).astype(m_ref.dtype)

    prev = lambda col: (lambda i: (jnp.maximum(i * hb - 1, 0), col))
    return pl.pallas_call(
        body, name="conv_fwd", grid=(S // tm,),
        in_specs=[pl.BlockSpec((tm, C3), lambda i: (i, 0)), pl.BlockSpec((8, C), prev(1)),
                  pl.BlockSpec((8, C), prev(2)), pl.BlockSpec((3, C), lambda i: (0, 0))],
        out_specs=pl.BlockSpec((tm, C), lambda i: (i, 0)),
        out_shape=jax.ShapeDtypeStruct((S, C), BF16), compiler_params=_params(("arbitrary",)),
    )(p, p, p, w)


def _conv_bwd_call(p, w, dm):
    S, C3 = p.shape
    C = C3 // 3
    tm = _pick(S, 256, 8)
    hb = tm // 8
    n_tiles = S // tm

    def body(p_ref, c_prev, z_prev, b_next, dm_ref, dm_next, w_ref, dp_ref, dw_ref):
        i = pl.program_id(0)
        b = p_ref[:, 0:C]
        c = p_ref[:, C:2 * C]
        z = p_ref[:, 2 * C:3 * C]
        cz = c * z
        czp = jnp.where(i > 0, c_prev[...] * z_prev[...], 0.0)
        s1 = _shift_down(cz, czp, 1)
        s2 = _shift_down(cz, czp, 2)
        w0, w1, w2 = w_ref[0:1, :], w_ref[1:2, :], w_ref[2:3, :]
        y = w2 * cz + w1 * s1 + w0 * s2
        dm_t = dm_ref[...]
        dy = dm_t * b
        dyn = jnp.where(i < n_tiles - 1, dm_next[...] * b_next[...], 0.0)
        dcz = w2 * dy + w1 * _shift_up(dy, dyn, 1) + w0 * _shift_up(dy, dyn, 2)
        dp_ref[:, 0:C] = (dm_t * y).astype(dp_ref.dtype)
        dp_ref[:, C:2 * C] = (dcz * z).astype(dp_ref.dtype)
        dp_ref[:, 2 * C:3 * C] = (dcz * c).astype(dp_ref.dtype)
        dw = jnp.concatenate([jnp.sum(dy * s2, axis=0, keepdims=True), jnp.sum(dy * s1, axis=0, keepdims=True),
                              jnp.sum(dy * cz, axis=0, keepdims=True)], axis=0)

        @pl.when(i == 0)
        def _():
            dw_ref[...] = dw

        @pl.when(i != 0)
        def _():
            dw_ref[...] += dw

    prev = lambda col: (lambda i: (jnp.maximum(i * hb - 1, 0), col))
    nxt = lambda col: (lambda i: (jnp.minimum((i + 1) * hb, S // 8 - 1), col))
    return pl.pallas_call(
        body, name="conv_bwd", grid=(n_tiles,),
        in_specs=[pl.BlockSpec((tm, C3), lambda i: (i, 0)), pl.BlockSpec((8, C), prev(1)),
                  pl.BlockSpec((8, C), prev(2)), pl.BlockSpec((8, C), nxt(0)),
                  pl.BlockSpec((tm, C), lambda i: (i, 0)), pl.BlockSpec((8, C), nxt(0)),
                  pl.BlockSpec((3, C), lambda i: (0, 0))],
        out_specs=[pl.BlockSpec((tm, C3), lambda i: (i, 0)), pl.BlockSpec((3, C), lambda i: (0, 0))],
        out_shape=[jax.ShapeDtypeStruct((S, C3), BF16), jax.ShapeDtypeStruct((3, C), F32)],
        compiler_params=_params(("arbitrary",)),
    )(p, p, p, p, dm, dm, w)


def _my_place():
    return lax.axis_index("x"), lax.axis_index("y"), lax.axis_index("c")


def _gather_weights_call(name, shard):
    R, C = shard.shape

    def body(s_ref, o_ref, send_sems, recv_sems, local_sem):
        x, y, c = _my_place()
        mine = 2 * x + y
        local = pltpu.make_async_copy(s_ref, o_ref.at[mine], local_sem)
        local.start()
        peers = [(1 - x, y), (x, 1 - y), (1 - x, 1 - y)]
        copies = []
        for k, (px, py) in enumerate(peers):
            cp = pltpu.make_async_remote_copy(src_ref=s_ref, dst_ref=o_ref.at[mine], send_sem=send_sems.at[k],
                                              recv_sem=recv_sems.at[k], device_id=(px, py, c), device_id_type=MESH)
            cp.start()
            copies.append(cp)
        for k, (px, py) in enumerate(peers):
            pltpu.make_async_remote_copy(src_ref=s_ref, dst_ref=o_ref.at[2 * px + py], send_sem=send_sems.at[k],
                                         recv_sem=recv_sems.at[k], device_id=(px, py, c),
                                         device_id_type=MESH).wait_recv()
        for cp in copies:
            cp.wait_send()
        local.wait()

    any_spec = pl.BlockSpec(memory_space=pl.ANY)
    return pl.pallas_call(
        body, name=name, in_specs=[any_spec], out_specs=any_spec,
        out_shape=jax.ShapeDtypeStruct((4, R, C), shard.dtype),
        scratch_shapes=[pltpu.SemaphoreType.DMA((3,)), pltpu.SemaphoreType.DMA((3,)), pltpu.SemaphoreType.DMA],
        compiler_params=pltpu.CompilerParams(has_side_effects=True),
    )(shard)


_D2D_CHUNKS = 4


_LOCAL_CHUNKS = 8


def _local_copies(src_of, dst_of, rows, sems, base):
    rc = rows // _LOCAL_CHUNKS
    assert rc * _LOCAL_CHUNKS == rows and rc % 16 == 0, rows
    out = []
    for j in range(_LOCAL_CHUNKS):
        sl = pl.ds(j * rc, rc)
        out.append(pltpu.make_async_copy(src_of(sl), dst_of(sl), sems.at[base + j]))
    return out


def _comm_call(name, body, arrays, out_shapes, sem_counts, aliases=None):
    any_spec = pl.BlockSpec(memory_space=pl.ANY)
    return pl.pallas_call(
        body, name=name, in_specs=[any_spec] * len(arrays), out_specs=[any_spec] * len(out_shapes),
        out_shape=out_shapes, scratch_shapes=[pltpu.SemaphoreType.DMA((n,)) for n in sem_counts],
        input_output_aliases=aliases or {}, compiler_params=pltpu.CompilerParams(has_side_effects=True),
    )(*arrays)


def _gather_halves_call(shards):
    na = len(shards)

    def body(*refs):
        send, forward, finish = _gather_phases(refs[:na], refs[na:2 * na], *refs[2 * na:])
        send()
        forward()
        finish()

    start = _gather_start(shards)
    outs = [jax.ShapeDtypeStruct(t.shape, t.dtype) for t in start]
    return _comm_call("gather_weights", body, start, outs, _gather_sems(na), aliases={a: a for a in range(na)})


def _gather_start(shards):
    for s in shards:
        assert s.shape[0] % (2 * _D2D_CHUNKS * 16) == 0, s.shape
    return [jnp.broadcast_to(s[None], (4,) + tuple(s.shape)) for s in shards]


def _gather_sems(na):
    return [3 * na, 3 * na, 3 * na * _D2D_CHUNKS, 3 * na * _D2D_CHUNKS]


def _gather_phases(s_refs, o_refs, ici_send, ici_recv, d2d_send, d2d_recv):
    na = len(s_refs)
    x, y, c = _my_place()
    mine = 2 * x + y
    chips = [(1 - x, y), (x, 1 - y), (1 - x, 1 - y)]

    def ici(a, k, chip, block):
        Rh = s_refs[a].shape[1] // 2
        my_half = pl.ds(pl.multiple_of(c * Rh, 16), Rh)
        return pltpu.make_async_remote_copy(src_ref=s_refs[a].at[mine, my_half], dst_ref=o_refs[a].at[block, my_half],
                                            send_sem=ici_send.at[3 * a + k], recv_sem=ici_recv.at[3 * a + k],
                                            device_id=(chip[0], chip[1], c), device_id_type=MESH)

    def d2d(a, k, j, block, half):
        Rh = s_refs[a].shape[1] // 2
        rc = Rh // _D2D_CHUNKS
        rows = pl.ds(pl.multiple_of(half * Rh + j * rc, 16), rc)
        idx = (3 * a + k) * _D2D_CHUNKS + j
        return pltpu.make_async_remote_copy(src_ref=o_refs[a].at[block, rows], dst_ref=o_refs[a].at[block, rows],
                                            send_sem=d2d_send.at[idx], recv_sem=d2d_recv.at[idx],
                                            device_id=(x, y, 1 - c), device_id_type=MESH)

    def send():
        for a in range(na):
            for k, chip in enumerate(chips):
                ici(a, k, chip, mine).start()

    def forward():
        for a in range(na):
            for k, chip in enumerate(chips):
                block = 2 * chip[0] + chip[1]
                ici(a, k, chip, block).wait_recv()
                for j in range(_D2D_CHUNKS):
                    d2d(a, k, j, block, c).start()

    def finish():
        for a in range(na):
            for k, chip in enumerate(chips):
                block = 2 * chip[0] + chip[1]
                for j in range(_D2D_CHUNKS):
                    d2d(a, k, j, block, 1 - c).wait_recv()
        for a in range(na):
            for k, chip in enumerate(chips):
                ici(a, k, chip, mine).wait_send()
                for j in range(_D2D_CHUNKS):
                    d2d(a, k, j, 2 * chip[0] + chip[1], c).wait_send()

    return send, forward, finish


def _pair_exchange_call(packed, tag):
    na = len(packed)

    def body(*refs):
        p_refs, o_refs = refs[:na], refs[na:2 * na]
        send_sems, recv_sems = refs[2 * na:]
        x, y, c = _my_place()
        copies = []
        for a in range(na):
            nb, _, Rh, _ = p_refs[a].shape
            rc = Rh // _D2D_CHUNKS
            assert rc * _D2D_CHUNKS == Rh and rc % 16 == 0
            for b in range(nb):
                for j in range(_D2D_CHUNKS):
                    rows = pl.ds(j * rc, rc)
                    idx = (a * nb + b) * _D2D_CHUNKS + j
                    copies.append(pltpu.make_async_remote_copy(
                        src_ref=p_refs[a].at[b, 1 - c, rows], dst_ref=o_refs[a].at[b, rows],
                        send_sem=send_sems.at[idx], recv_sem=recv_sems.at[idx],
                        device_id=(x, y, 1 - c), device_id_type=MESH))
        for t in copies:
            t.start()
        for t in copies:
            t.wait_recv()
        for t in copies:
            t.wait_send()

    outs = [jax.ShapeDtypeStruct((p.shape[0], p.shape[2], p.shape[3]), p.dtype) for p in packed]
    n = sum(p.shape[0] for p in packed) * _D2D_CHUNKS
    return _comm_call("pair_exchange_" + tag, body, packed, outs, [n, n])


def _pair_add_call(name, packed, other, core):
    nb, _, Rh, C = packed.shape
    tr = _pick(Rh, 512, 16)

    def body(c_ref, p_ref, o_ref, q_ref):
        q_ref[...] = (p_ref[...].astype(F32) + o_ref[...].astype(F32)).astype(q_ref.dtype)

    grid_spec = pltpu.PrefetchScalarGridSpec(
        num_scalar_prefetch=1, grid=(nb, Rh // tr),
        in_specs=[pl.BlockSpec((None, None, tr, C), lambda b, r, c_ref: (b, c_ref[0], r, 0)),
                  pl.BlockSpec((None, tr, C), lambda b, r, c_ref: (b, r, 0))],
        out_specs=pl.BlockSpec((None, tr, C), lambda b, r, c_ref: (b, r, 0)))
    return pl.pallas_call(
        body, name=name, grid_spec=grid_spec, out_shape=jax.ShapeDtypeStruct((nb, Rh, C), packed.dtype),
        compiler_params=_params(("arbitrary", "arbitrary")),
    )(core, packed, other)


def _chip_scatter_call(pairs):
    na = len(pairs)

    def body(*refs):
        send, finish = _scatter_phases(refs[:na], refs[na:2 * na], *refs[2 * na:])
        send()
        finish()

    outs = [jax.ShapeDtypeStruct(p.shape, p.dtype) for p in pairs]
    return _comm_call("chip_scatter", body, pairs, outs, _scatter_sems(na))


def _scatter_sems(na):
    return [3 * na, 3 * na, na * _LOCAL_CHUNKS]


def _scatter_phases(p_refs, o_refs, send_sems, recv_sems, local_sems):
    na = len(p_refs)
    x, y, c = _my_place()
    mine = 2 * x + y
    chips = [(1 - x, y), (x, 1 - y), (1 - x, 1 - y)]

    def local(a):
        p_ref, o_ref = p_refs[a], o_refs[a]
        return _local_copies(lambda sl: p_ref.at[mine, sl], lambda sl: o_ref.at[mine, sl], p_ref.shape[1],
                             local_sems, a * _LOCAL_CHUNKS)

    def remote(a, k, src_block, dst_block):
        px, py = chips[k]
        return pltpu.make_async_remote_copy(src_ref=p_refs[a].at[src_block], dst_ref=o_refs[a].at[dst_block],
                                            send_sem=send_sems.at[3 * a + k], recv_sem=recv_sems.at[3 * a + k],
                                            device_id=(px, py, c), device_id_type=MESH)

    def send():
        for a in range(na):
            for t in local(a):
                t.start()
            for k, (px, py) in enumerate(chips):
                remote(a, k, 2 * px + py, mine).start()

    def finish():
        for a in range(na):
            for k, (px, py) in enumerate(chips):
                remote(a, k, mine, 2 * px + py).wait_recv()
        for a in range(na):
            for k, (px, py) in enumerate(chips):
                remote(a, k, 2 * px + py, mine).wait_send()
            for t in local(a):
                t.wait()

    return send, finish


def _sum_slots_call(name, parts, core):
    n, R, C = parts.shape
    tr = _pick(R, 256, 8)

    def body(c_ref, p_ref, o_ref):
        acc = p_ref[0].astype(F32)
        for s in range(1, n):
            acc = acc + p_ref[s].astype(F32)
        o_ref[...] = acc

    grid_spec = pltpu.PrefetchScalarGridSpec(
        num_scalar_prefetch=1, grid=(R // tr,),
        in_specs=[pl.BlockSpec((n, tr, C), lambda i, c_ref: (0, i, 0))],
        out_specs=pl.BlockSpec((None, tr, C), lambda i, c_ref: (c_ref[0], i, 0)))
    return pl.pallas_call(
        body, name=name, grid_spec=grid_spec, out_shape=jax.ShapeDtypeStruct((2, R, C), F32),
        compiler_params=_params(("arbitrary",)),
    )(core, parts)


def _sibling_share_call(halves):
    na = len(halves)
    nch = 2 * _D2D_CHUNKS

    def body(*refs):
        h_refs, o_refs = refs[:na], refs[na:2 * na]
        send_sems, recv_sems = refs[2 * na:]
        x, y, c = _my_place()

        def cp(a, j, slot):
            rc = h_refs[a].shape[1] // nch
            rows = pl.ds(j * rc, rc)
            return pltpu.make_async_remote_copy(src_ref=h_refs[a].at[slot, rows], dst_ref=o_refs[a].at[slot, rows],
                                                send_sem=send_sems.at[a * nch + j], recv_sem=recv_sems.at[a * nch + j],
                                                device_id=(x, y, 1 - c), device_id_type=MESH)

        copies = [cp(a, j, c) for a in range(na) for j in range(nch)]
        for t in copies:
            t.start()
        for a in range(na):
            for j in range(nch):
                cp(a, j, 1 - c).wait_recv()
        for t in copies:
            t.wait_send()

    for h in halves:
        assert h.shape[1] % (nch * 8) == 0, h.shape
    outs = [jax.ShapeDtypeStruct(h.shape, h.dtype) for h in halves]
    return _comm_call("sibling_share", body, halves, outs, [na * nch, na * nch], aliases={a: a for a in range(na)})


def _allreduce_small_call(part):
    R, C = part.shape

    def body(p_ref, o_ref, slots, send_sems, recv_sems):
        x, y, c = _my_place()
        me = 4 * x + 2 * y + c
        peers = []
        for k in range(1, 8):
            px = x ^ (k >> 2) if (k >> 2) else x
            py = y ^ ((k >> 1) & 1) if ((k >> 1) & 1) else y
            pc = c ^ (k & 1) if (k & 1) else c
            peers.append((px, py, pc))
        copies = []
        for k, (px, py, pc) in enumerate(peers):
            cp = pltpu.make_async_remote_copy(src_ref=p_ref, dst_ref=slots.at[me], send_sem=send_sems.at[k],
                                              recv_sem=recv_sems.at[k], device_id=(px, py, pc), device_id_type=MESH)
            cp.start()
            copies.append(cp)
        slots[me] = p_ref[...]
        for k, (px, py, pc) in enumerate(peers):
            pltpu.make_async_remote_copy(src_ref=p_ref, dst_ref=slots.at[4 * px + 2 * py + pc],
                                         send_sem=send_sems.at[k], recv_sem=recv_sems.at[k],
                                         device_id=(px, py, pc), device_id_type=MESH).wait_recv()
        for cp in copies:
            cp.wait_send()
        acc = slots[0]
        for s in range(1, 8):
            acc = acc + slots[s]
        o_ref[...] = acc

    vm = pl.BlockSpec(memory_space=pltpu.VMEM)
    return pl.pallas_call(
        body, name="allreduce_small", in_specs=[vm], out_specs=vm,
        out_shape=jax.ShapeDtypeStruct((R, C), F32),
        scratch_shapes=[pltpu.VMEM((8, R, C), F32), pltpu.SemaphoreType.DMA((7,)), pltpu.SemaphoreType.DMA((7,))],
        compiler_params=pltpu.CompilerParams(has_side_effects=True, vmem_limit_bytes=_VMEM_LIMIT),
    )(part)


def _adamw_call(name, w, g, m, v):
    shape = w.shape
    cols = shape[-1]
    rows = int(np.prod(shape[:-1])) if len(shape) > 1 else 1
    w2, g2, m2, v2 = (t.reshape(rows, cols) for t in (w, g, m, v))
    tr = _pick(rows, 256, 8)
    c1 = 1.0 / (1.0 - ADAM_B1 ** ADAM_STEP)
    c2 = 1.0 / (1.0 - ADAM_B2 ** ADAM_STEP)

    def body(w_ref, g_ref, m_ref, v_ref, d_ref, nm_ref, nv_ref):
        gr = g_ref[...]
        m_new = ADAM_B1 * m_ref[...] + (1.0 - ADAM_B1) * gr
        v_new = ADAM_B2 * v_ref[...] + (1.0 - ADAM_B2) * (gr * gr)
        m_hat = m_new / (1.0 - ADAM_B1 ** ADAM_STEP)
        v_hat = v_new / (1.0 - ADAM_B2 ** ADAM_STEP)
        d_ref[...] = -ADAM_LR * (m_hat / (jnp.sqrt(v_hat) + ADAM_EPS) + ADAM_WD * w_ref[...])
        nm_ref[...] = m_new
        nv_ref[...] = v_new

    spec = pl.BlockSpec((tr, cols), lambda i: (i, 0))
    d, nm, nv = pl.pallas_call(
        body, name=name, grid=(rows // tr,), in_specs=[spec] * 4, out_specs=[spec] * 3,
        out_shape=[jax.ShapeDtypeStruct((rows, cols), F32)] * 3, compiler_params=_params(("arbitrary",)),
    )(w2, g2, m2, v2)
    return d.reshape(shape), nm.reshape(shape), nv.reshape(shape)


def _pad_rows(flat, mult):
    n = flat.shape[0]
    unit = PACK_COLS * mult
    total = -(-n // unit) * unit
    return jnp.pad(flat, (0, total - n)).reshape(total // PACK_COLS, PACK_COLS)


def _pad_axis(arr, axis, mult):
    n = arr.shape[axis]
    total = -(-n // mult) * mult
    if total == n:
        return arr
    widths = [(0, 0)] * arr.ndim
    widths[axis] = (0, total - n)
    return jnp.pad(arr, widths)


def _shard_slice(arr, axis, blk, nblk=4):
    w = arr.shape[axis] // nblk
    return lax.slice_in_dim(arr, blk * w, (blk + 1) * w, axis=axis)


_NT = (((1,), (1,)), ((), ()))
_TN = (((0,), (0,)), ((), ()))


def _ffn_fwd(tag, x, h, ht, wg, wu, wd, g_next):
    S, D = x.shape
    (wg, gi), (wu, ui), (wd, di) = wg, wu, wd
    NB, Fb = wg.shape[0], wg.shape[2]
    tm = _pick(S, 1024, 8)

    def gate_up(h_ref, wg_ref, wu_ref, a_ref, u_ref, s_ref):
        h_t = _mx(h_ref[...])
        a = jnp.dot(h_t, _mx(wg_ref[...]), preferred_element_type=F32)
        u = jnp.dot(h_t, _mx(wu_ref[...]), preferred_element_type=F32)
        sig = _sigmoid(a)
        silu = a * sig
        a_ref[...] = (u * (sig * (1.0 + a * (1.0 - sig)))).astype(a_ref.dtype)
        u_ref[...] = silu.astype(u_ref.dtype)
        s_ref[...] = (silu * u).astype(s_ref.dtype)

    hid = pl.BlockSpec((None, tm, Fb), lambda b, i: (b, i, 0))
    a, u, s = pl.pallas_call(
        gate_up, name=tag + "_gate_up", grid=(NB, S // tm),
        in_specs=[pl.BlockSpec((tm, D), lambda b, i: (i, 0)), pl.BlockSpec((None, D, Fb), lambda b, i: (b, gi, 0)),
                  pl.BlockSpec((None, D, Fb), lambda b, i: (b, ui, 0))], out_specs=[hid] * 3,
        out_shape=[jax.ShapeDtypeStruct((NB, S, Fb), BF16)] * 3, compiler_params=_params(("arbitrary", "arbitrary")),
    )(h, wg, wu)

    tm2 = _pick(S, 512, 128)

    def down(s_ref, wd_ref, x_ref, g_ref, xo_ref, ho_ref, hto_ref):
        acc = jnp.dot(_mx(s_ref[0]), _mx(wd_ref[0]), preferred_element_type=F32)
        for b in range(1, NB):
            acc = acc + jnp.dot(_mx(s_ref[b]), _mx(wd_ref[b]), preferred_element_type=F32)
        x_new = x_ref[...] + 0.5 * acc
        xo_ref[...] = x_new
        h_new = _rms_stats(x_new)[0] * g_ref[...]
        ho_ref[...] = h_new.astype(ho_ref.dtype)
        hto_ref[...] = jnp.transpose(h_new).astype(hto_ref.dtype)

    row = pl.BlockSpec((tm2, D), lambda i: (i, 0))
    x_new, h_next, ht_next = pl.pallas_call(
        down, name=tag + "_down", grid=(S // tm2,),
        in_specs=[pl.BlockSpec((NB, tm2, Fb), lambda i: (0, i, 0)), pl.BlockSpec((NB, Fb, D), lambda i: (0, di, 0)),
                  row, pl.BlockSpec((1, D), lambda i: (0, 0))],
        out_specs=[row, row, pl.BlockSpec((D, tm2), lambda i: (0, i))],
        out_shape=[jax.ShapeDtypeStruct((S, D), F32), jax.ShapeDtypeStruct((S, D), BF16),
                   jax.ShapeDtypeStruct((D, S), BF16)],
        compiler_params=_params(("arbitrary",)),
    )(s, wd, x, g_next)
    return x_new, h_next, ht_next, (x, h, ht, a, u, s)


def _ffn_bwd(tag, saved, dx_out, dxb, wg, wu, wd, gain):
    x, h, ht, a, u, s = saved
    S, D = x.shape
    (wg, gi), (wu, ui), (wd, di) = wg, wu, wd
    NB, Fb = wg.shape[0], wg.shape[2]
    tm = _pick(S, 1024, 8)
    tk = _pick(S, 1024, 128)
    nk = S // tk

    def dgate_up(d_ref, wd_ref, a_ref, u_ref, da_ref, du_ref):
        ds = 0.5 * lax.dot_general(_mx(d_ref[...]), _mx(wd_ref[...]), _NT, preferred_element_type=F32)
        da_ref[...] = (ds * a_ref[...].astype(F32)).astype(da_ref.dtype)
        du_ref[...] = (ds * u_ref[...].astype(F32)).astype(du_ref.dtype)

    hid = pl.BlockSpec((None, tm, Fb), lambda b, i: (b, i, 0))
    da, du = pl.pallas_call(
        dgate_up, name=tag + "_dgate_up", grid=(NB, S // tm),
        in_specs=[pl.BlockSpec((tm, D), lambda b, i: (i, 0)), pl.BlockSpec((None, Fb, D), lambda b, i: (b, di, 0)),
                  hid, hid],
        out_specs=[hid, hid], out_shape=[jax.ShapeDtypeStruct((NB, S, Fb), BF16)] * 2,
        compiler_params=_params(("arbitrary", "arbitrary")),
    )(dxb, wd, a, u)

    def dw_down(s_ref, d_ref, o_ref, acc_ref):
        k = pl.program_id(1)
        p = lax.dot_general(_mx(s_ref[...]), _mx(d_ref[...]), _TN, preferred_element_type=F32)

        @pl.when(k == 0)
        def _():
            acc_ref[...] = p

        @pl.when(k != 0)
        def _():
            acc_ref[...] += p

        @pl.when(k == nk - 1)
        def _():
            o_ref[...] = (0.5 * acc_ref[...]).astype(o_ref.dtype)

    hk = pl.BlockSpec((None, tk, Fb), lambda b, k: (b, k, 0))
    dwd = pl.pallas_call(
        dw_down, name=tag + "_dw_down", grid=(NB, nk),
        in_specs=[hk, pl.BlockSpec((tk, D), lambda b, k: (k, 0))],
        out_specs=pl.BlockSpec((None, Fb, D), lambda b, k: (b, 0, 0)),
        out_shape=jax.ShapeDtypeStruct((NB, Fb, D), _WIRE_DTYPE), scratch_shapes=[pltpu.VMEM((Fb, D), F32)],
        compiler_params=_params(("arbitrary", "arbitrary")),
    )(s, dxb)

    def dw_gate_up(h_ref, da_ref, du_ref, og_ref, ou_ref, accg_ref, accu_ref):
        k = pl.program_id(1)
        h_t = _mx(h_ref[...])
        dims = _TN if ht is None else (((1,), (0,)), ((), ()))
        pg = lax.dot_general(h_t, _mx(da_ref[...]), dims, preferred_element_type=F32)
        pu = lax.dot_general(h_t, _mx(du_ref[...]), dims, preferred_element_type=F32)

        @pl.when(k == 0)
        def _():
            accg_ref[...] = pg
            accu_ref[...] = pu

        @pl.when(k != 0)
        def _():
            accg_ref[...] += pg
            accu_ref[...] += pu

        @pl.when(k == nk - 1)
        def _():
            og_ref[...] = accg_ref[...].astype(og_ref.dtype)
            ou_ref[...] = accu_ref[...].astype(ou_ref.dtype)

    wout = pl.BlockSpec((None, D, Fb), lambda b, k: (b, 0, 0))
    dwg, dwu = pl.pallas_call(
        dw_gate_up, name=tag + "_dw_gate_up", grid=(NB, nk),
        in_specs=[pl.BlockSpec((tk, D), lambda b, k: (k, 0)) if ht is None
                  else pl.BlockSpec((D, tk), lambda b, k: (0, k)), hk, hk], out_specs=[wout, wout],
        out_shape=[jax.ShapeDtypeStruct((NB, D, Fb), _WIRE_DTYPE)] * 2,
        scratch_shapes=[pltpu.VMEM((D, Fb), F32)] * 2, compiler_params=_params(("arbitrary", "arbitrary")),
    )(h if ht is None else ht, da, du)

    tm2 = _pick(S, 512, 8)

    def dx_body(da_ref, du_ref, wg_hbm, wu_hbm, x_ref, dxo_ref, g_ref, dx_ref, dxb_ref, dg_ref, wg_v, wu_v, sem):
        i = pl.program_id(0)

        @pl.when(i == 0)
        def _():
            cg = pltpu.make_async_copy(wg_hbm.at[:, pl.ds(gi * D, D), :], wg_v, sem.at[0])
            cu = pltpu.make_async_copy(wu_hbm.at[:, pl.ds(ui * D, D), :], wu_v, sem.at[1])
            cg.start()
            cu.start()
            cg.wait()
            cu.wait()

        dh = None
        for b in range(NB):
            t = lax.dot_general(_mx(da_ref[b]), wg_v[b], _NT, preferred_element_type=F32)
            t = t + lax.dot_general(_mx(du_ref[b]), wu_v[b], _NT, preferred_element_type=F32)
            dh = t if dh is None else dh + t
        dx_n, dg = _rms_bwd(x_ref[...], g_ref[...], dh)
        dx = dxo_ref[...] + dx_n
        dx_ref[...] = dx
        dxb_ref[...] = dx.astype(dxb_ref.dtype)
        dg = jnp.sum(dg, axis=0, keepdims=True)

        @pl.when(i == 0)
        def _():
            dg_ref[...] = dg

        @pl.when(i != 0)
        def _():
            dg_ref[...] += dg

    row = pl.BlockSpec((tm2, D), lambda i: (i, 0))
    hid2 = pl.BlockSpec((NB, tm2, Fb), lambda i: (0, i, 0))
    anyspec = pl.BlockSpec(memory_space=pl.ANY)
    fixed = pl.BlockSpec((1, D), lambda i: (0, 0))
    dx, dxb_new, dgain = pl.pallas_call(
        dx_body, name=tag + "_dx", grid=(S // tm2,),
        in_specs=[hid2, hid2, anyspec, anyspec, row, row, fixed], out_specs=[row, row, fixed],
        out_shape=[jax.ShapeDtypeStruct((S, D), F32), jax.ShapeDtypeStruct((S, D), BF16),
                   jax.ShapeDtypeStruct((1, D), F32)],
        scratch_shapes=[pltpu.VMEM((NB, D, Fb), wg.dtype), pltpu.VMEM((NB, D, Fb), wu.dtype),
                        pltpu.SemaphoreType.DMA((2,))],
        compiler_params=_params(("arbitrary",)),
    )(da, du, wg, wu, x, dx_out, gain)
    return dx, dxb_new, dgain, dwg, dwu, dwd


def _conv_mixer_fwd(tag, x, h, ht, w_in, w_taps, w_out, g_next):
    S, D = x.shape
    C3 = w_in.shape[1]
    tm = _pick(S, 512, 8)
    p, = _fused_matmul(tag + "_in", 'nn', [_op(h)], [_op(w_in)], [(0, 0, 0)], 1, _ident_epi(), [F32],
                       S, C3, D, tm, _pick(C3, 1024, 128), D)
    m = _conv_fwd_call(p, w_taps)
    x_new, h_next = _fused_matmul(tag + "_out", 'nn', [_op(m)], [_op(w_out)], [(0, 0, 0)], 1, _resid_norm_epi(1.0),
                                  [F32, BF16], S, D, D, tm, D, D, tile_extras=[x], row_extras=[g_next])
    return x_new, h_next, (x, h, ht, p, m)


def _conv_mixer_bwd(tag, saved, dx_out, dxb, w_in, w_taps, w_out, gain):
    x, h, ht, p, m = saved
    S, D = x.shape
    C3 = w_in.shape[1]
    tm = _pick(S, 512, 8)
    tk = _pick(S, 1024, 128)
    dm, = _fused_matmul(tag + "_dm", 'nt', [_op(dxb)], [_op(w_out)], [(0, 0, 0)], 1, _ident_epi(), [F32],
                        S, D, D, tm, D, D)
    dw_out, = _fused_matmul(tag + "_dw_out", 'tn', [_op(m)], [_op(dxb)], [(0, 0, 0)], 1, _ident_epi(), [F32],
                            D, D, S, D, D, tk)
    dp, dtaps = _conv_bwd_call(p, w_taps, dm)
    dw_in, = _fused_matmul(tag + "_dw_in", 'nn', [_op(ht)], [_op(dp)], [(0, 0, 0)], 1, _ident_epi(), [F32],
                           D, C3, S, D, _pick(C3, 1024, 128), tk)
    dx, dxb_new, dgain = _fused_matmul(tag + "_dx", 'nt', [_op(dp)], [_op(w_in)], [(0, 0, 0)], 1, _norm_bwd_epi,
                                       [F32, BF16], S, D, C3, tm, D, C3,
                                       tile_extras=[x, dx_out], row_extras=[gain], n_colsum=1)
    return dx, dxb_new, dgain, dw_in, dtaps, dw_out


def _attn_scale():
    return np.float32(QK_DIM ** -0.5)


def _even_mixer_fwd(tag, x, h, ht, wts, tables, g_next, carry=()):
    S, D = x.shape
    cos, sa, sb = tables
    tm = _pick(S, 512, 8)
    AW = HEADS * HP
    proj, = _fused_matmul(tag + "_in", 'nn', [_op(h)], [_op(wts['w_in'])], [(0, 0, 0)], 1, _ident_epi(), [F32],
                          S, PROJ_W, D, tm, _pick(PROJ_W, 896, 128), D)
    cqn, ckvn, kr, u, vn = _even_prep_call(proj, wts['q_norm'], wts['kv_norm'], wts['sg_norm'], cos, sa, sb)
    scale = _attn_scale()

    def q_epi(accs, tiles, rows, mrows):
        c_t, a_t, b_t = mrows
        heads = [_rope(accs[0][:, hh * HP:(hh + 1) * HP], c_t, a_t, b_t) * scale for hh in range(HEADS)]
        return [jnp.concatenate(heads, axis=1)]

    q, = _fused_matmul(tag + "_q", 'nn', [_op(cqn)], [_op(wts['w_q'])], [(0, 0, 0)], 1, q_epi, [BF16],
                       S, AW, Q_LORA, tm, AW, Q_LORA, mrow_extras=[cos, sa, sb])

    def kv_epi(accs, tiles, rows, mrows):
        lane = lax.broadcasted_iota(jnp.int32, accs[1].shape, 1)
        v_t = jnp.where((lane & (HP - 1)) == VDIM, 1.0, accs[1])
        return [accs[0] + jnp.concatenate([mrows[0].astype(F32)] * HEADS, axis=1), v_t]

    k, v = _fused_matmul(tag + "_kv", 'nn', [_op(ckvn)], [_op(wts['w_k']), _op(wts['w_v'])],
                         [(0, 0, 0), (0, 1, 1)], 2, kv_epi, [BF16, BF16], S, AW, KV_LORA, tm, AW, KV_LORA,
                         mrow_extras=[kr])
    o, lse, gathered = _flash_fwd_call(q, k, v, carry)
    mix = _sgu_fwd_call(vn, u, o, wts['sg_wst'], wts['sg_bexp'])
    x_new, h_next = _fused_matmul(tag + "_out", 'nn', [_op(mix)], [_op(wts['w_out'])], [(0, 0, 0)], 1,
                                  _resid_norm_epi(1.0), [F32, BF16], S, D, AW + SG_WIDTH, tm, D, AW + SG_WIDTH,
                                  tile_extras=[x], row_extras=[g_next])
    return x_new, h_next, (x, h, ht, proj, cqn, ckvn, u, vn, q, k, v, o, lse, mix), gathered


def _even_mixer_bwd(tag, saved, dx_out, dxb, wts, tables, gain, carry=()):
    x, h, ht, proj, cqn, ckvn, u, vn, q, k, v, o, lse, mix = saved
    S, D = x.shape
    cos, sa, sb = tables
    tm = _pick(S, 512, 8)
    tk = _pick(S, 1024, 128)
    AW = HEADS * HP
    MW = AW + SG_WIDTH
    dmix, = _fused_matmul(tag + "_dmix", 'nt', [_op(dxb)], [_op(wts['w_out'])], [(0, 0, 0)], 1, _ident_epi(), [BF16],
                          S, MW, D, tm, _pick(MW, 768, 128), D)
    dw_out, = _fused_matmul(tag + "_dw_out", 'tn', [_op(mix)], [_op(dxb)], [(0, 0, 0)], 1, _ident_epi(), [F32],
                            MW, D, S, _pick(MW, 768, 128), D, tk)
    du, dvn, dsg_w, dsg_b = _sgu_bwd_call(dmix, vn, u, wts['sg_wst'], wts['sg_wst_t'], wts['sg_bexp'])
    delta = _attn_delta_call(o, dmix)
    dq, dk, dv, arrived = _flash_bwd_call(q, k, v, dmix, lse, delta, carry)
    scale = _attn_scale()

    def dq_epi(accs, tiles, rows, mrows):
        return accs

    def dq_pre_call():
        tr = _pick(S, 256, 8)

        def body(d_ref, c_ref, a_ref, b_ref, o_ref):
            for hh in range(HEADS):
                t = _rope_t(d_ref[:, hh * HP:(hh + 1) * HP], c_ref[...], a_ref[...], b_ref[...]) * scale
                o_ref[:, hh * HP:(hh + 1) * HP] = t.astype(o_ref.dtype)

        row = lambda i: (i, 0)
        return pl.pallas_call(
            body, name=tag + "_dq_unrope", grid=(S // tr,),
            in_specs=[pl.BlockSpec((tr, AW), row)] + [pl.BlockSpec((tr, HP), row)] * 3,
            out_specs=pl.BlockSpec((tr, AW), row), out_shape=jax.ShapeDtypeStruct((S, AW), BF16),
            compiler_params=_params(("arbitrary",)),
        )(dq, cos, sa, sb)

    dqp = dq_pre_call()
    dw_q, = _fused_matmul(tag + "_dw_q", 'tn', [_op(cqn)], [_op(dqp)], [(0, 0, 0)], 1, _ident_epi(), [F32],
                          Q_LORA, AW, S, Q_LORA, AW, tk)
    dcqn, = _fused_matmul(tag + "_dcq", 'nt', [_op(dqp)], [_op(wts['w_q'])], [(0, 0, 0)], 1, dq_epi, [F32],
                          S, Q_LORA, AW, tm, Q_LORA, AW)
    dw_k, dw_v = _fused_matmul(tag + "_dw_kv", 'tn', [_op(ckvn)], [_op(dk), _op(dv)], [(0, 0, 0), (0, 1, 1)], 2,
                               _ident_epi(), [F32, F32], KV_LORA, AW, S, KV_LORA, AW, tk)
    dckvn, = _fused_matmul(tag + "_dckv", 'nt', [_op(dk), _op(dv)], [_op(wts['w_k']), _op(wts['w_v'])],
                           [(0, 0, 0), (1, 1, 0)], 1, dq_epi, [F32], S, KV_LORA, AW, tm, KV_LORA, AW)
    dproj, dqn, dkvn, dsgn = _even_prep_bwd_call(proj, wts['q_norm'], wts['kv_norm'], wts['sg_norm'], cos, sa, sb,
                                                 dcqn, dckvn, dk, du, dvn)
    dw_in, = _fused_matmul(tag + "_dw_in", 'nn', [_op(ht)], [_op(dproj)], [(0, 0, 0)], 1, _ident_epi(), [F32],
                           D, PROJ_W, S, D, _pick(PROJ_W, 896, 128), tk)
    dx, dxb_new, dgain = _fused_matmul(tag + "_dx", 'nt', [_op(dproj)], [_op(wts['w_in'])], [(0, 0, 0)], 1,
                                       _norm_bwd_epi, [F32, BF16], S, D, PROJ_W, tm, D, PROJ_W,
                                       tile_extras=[x, dx_out], row_extras=[gain], n_colsum=1)
    grads = dict(w_in=dw_in, w_q=dw_q, w_k=dw_k, w_v=dw_v, w_out=dw_out, q_norm=dqn, kv_norm=dkvn, sg_norm=dsgn,
                 sg_w=dsg_w, sg_b=dsg_b)
    return dx, dxb_new, dgain, grads, arrived


def _even_weights(w_in, w_uq, w_ukv, w_out, q_norm, kv_norm, sg_norm, sg_w, sg_b):
    D = w_in.shape[0]
    kr_cols = jnp.pad(w_in[:, Q_LORA + KV_LORA:Q_LORA + KV_LORA + ROPE], ((0, 0), (NOPE, HP - QK_DIM)))
    w_in_p = jnp.concatenate([w_in[:, :Q_LORA + KV_LORA], kr_cols, w_in[:, Q_LORA + KV_LORA + ROPE:]], axis=1)
    wq = w_uq.reshape(Q_LORA, HEADS, QK_DIM)
    w_q = jnp.pad(wq, ((0, 0), (0, 0), (0, HP - QK_DIM))).reshape(Q_LORA, HEADS * HP)
    wkv = w_ukv.reshape(KV_LORA, HEADS, NOPE + VDIM)
    w_k = jnp.pad(wkv[:, :, :NOPE], ((0, 0), (0, 0), (0, HP - NOPE))).reshape(KV_LORA, HEADS * HP)
    w_v = jnp.pad(wkv[:, :, NOPE:], ((0, 0), (0, 0), (0, HP - VDIM))).reshape(KV_LORA, HEADS * HP)
    wo_a = w_out[:HEADS * VDIM].reshape(HEADS, VDIM, D)
    wo_a = jnp.pad(wo_a, ((0, 0), (0, HP - VDIM), (0, 0))).reshape(HEADS * HP, D)
    w_out_p = jnp.concatenate([wo_a, w_out[HEADS * VDIM:]], axis=0)
    tri = jnp.tril(jnp.ones((SG_CHUNK, SG_CHUNK), F32))
    wm = sg_w * tri
    wst = wm.reshape(SG_GROUPS // 2, 2 * SG_CHUNK, SG_CHUNK).astype(_MXU_DTYPE)
    wst_t = jnp.swapaxes(wm, 1, 2).reshape(SG_GROUPS // 2, 2 * SG_CHUNK, SG_CHUNK).astype(_MXU_DTYPE)
    bexp = jnp.repeat(sg_b.T, SG_GDIM, axis=1)
    return dict(w_in=w_in_p, w_q=w_q, w_k=w_k, w_v=w_v, w_out=w_out_p, sg_wst=wst, sg_wst_t=wst_t, sg_bexp=bexp,
                q_norm=q_norm.reshape(1, -1), kv_norm=kv_norm.reshape(1, -1), sg_norm=sg_norm.reshape(1, -1))


def _even_grads_unpad(g):
    d_in = g['w_in']
    kr0 = Q_LORA + KV_LORA
    dw_in = jnp.concatenate([d_in[:, :kr0], d_in[:, kr0 + NOPE:kr0 + QK_DIM], d_in[:, kr0 + HP:]], axis=1)
    dw_uq = g['w_q'].reshape(Q_LORA, HEADS, HP)[:, :, :QK_DIM].reshape(Q_LORA, HEADS * QK_DIM)
    dk = g['w_k'].reshape(KV_LORA, HEADS, HP)[:, :, :NOPE]
    dv = g['w_v'].reshape(KV_LORA, HEADS, HP)[:, :, :VDIM]
    dw_ukv = jnp.concatenate([dk, dv], axis=2).reshape(KV_LORA, HEADS * (NOPE + VDIM))
    D = d_in.shape[0]
    wo = g['w_out']
    wo_a = wo[:HEADS * HP].reshape(HEADS, HP, D)[:, :VDIM].reshape(HEADS * VDIM, D)
    dw_out = jnp.concatenate([wo_a, wo[HEADS * HP:]], axis=0)
    dsg_b = g['sg_b'][:, :SG_GROUPS].T
    return dict(even_w_in=dw_in, w_uq=dw_uq, w_ukv=dw_ukv, even_w_out=dw_out, q_norm=g['q_norm'][0],
                kv_norm=g['kv_norm'][0], sg_norm=g['sg_norm'][0], sg_w=g['sg_w'], sg_b=dsg_b)


def kernel(x, positions, ffn_pre_norm, ffn_pre_w_gate, ffn_pre_w_up, ffn_pre_w_down, mix_norm, ffn_post_norm, ffn_post_w_gate, ffn_post_w_up, ffn_post_w_down, even_w_in, q_norm, w_uq, kv_norm, w_ukv, sg_norm, sg_w, sg_b, even_w_out, conv_w_in, conv_w, conv_w_out, final_norm, loss_target, m_ffn_pre_norm, m_ffn_pre_w_gate, m_ffn_pre_w_up, m_ffn_pre_w_down, m_mix_norm, m_ffn_post_norm, m_ffn_post_w_gate, m_ffn_post_w_up, m_ffn_post_w_down, m_even_w_in, m_q_norm, m_w_uq, m_kv_norm, m_w_ukv, m_sg_norm, m_sg_w, m_sg_b, m_even_w_out, m_conv_w_in, m_conv_w, m_conv_w_out, m_final_norm, v_ffn_pre_norm, v_ffn_pre_w_gate, v_ffn_pre_w_up, v_ffn_pre_w_down, v_mix_norm, v_ffn_post_norm, v_ffn_post_w_gate, v_ffn_post_w_up, v_ffn_post_w_down, v_even_w_in, v_q_norm, v_w_uq, v_kv_norm, v_w_ukv, v_sg_norm, v_sg_w, v_sg_b, v_even_w_out, v_conv_w_in, v_conv_w, v_conv_w_out, v_final_norm):
    env = dict(locals())
    w_loc = {n: env[n] for n in WEIGHTS}
    m_loc = {n: env['m_' + n] for n in WEIGHTS}
    v_loc = {n: env['v_' + n] for n in WEIGHTS}
    S, D = x.shape[1], x.shape[2]
    depth = ffn_pre_norm.shape[0]
    xs = x.reshape(S, D)
    target = loss_target.reshape(S, D)

    def layers_of(n, early):
        count = w_loc[n].shape[0]
        if n in ('conv_w_in', 'conv_w', 'conv_w_out'):
            return [] if early else list(range(count))
        return [0] if early else list(range(1, count))

    def shards_of(early):
        wire = lambda n, l: w_loc[n][l].astype(_WIRE_DTYPE)
        keys = [[(n, l) for n in group for l in layers_of(n, early)] for group in (GROUP_A, GROUP_B, GATHER_C)]
        sa = _pad_axis(jnp.concatenate([wire(n, l) for n, l in keys[0]], axis=0), 0, PACK_ROW_MULT)
        sb = _pad_axis(jnp.concatenate([wire(n, l) for n, l in keys[1]], axis=0), 0, PACK_ROW_MULT)
        sc = _pad_rows(jnp.concatenate([wire(n, l).reshape(-1) for n, l in keys[2]]), PACK_ROW_MULT)
        return [sa, sb, sc], keys

    full = {n: {} for n in SHARDED}

    def unpack(gathered, keys):
        gat_a, gat_b, gat_c = gathered
        for idx, (n, l) in enumerate(keys[0]):
            full[n][l] = (gat_a, idx)
        row = 0
        for n, l in keys[1]:
            rows = w_loc[n].shape[1]
            if n in FFN_WEIGHTS:
                full[n][l] = (gat_b, row // rows)
            else:
                full[n][l] = jnp.concatenate([gat_b[b, row:row + rows] for b in range(4)], axis=0)
            row += rows
        gflat = gat_c.reshape(4, -1)
        off = 0
        for n, l in keys[2]:
            shp = w_loc[n].shape[1:]
            size = int(np.prod(shp))
            full[n][l] = jnp.concatenate([gflat[b, off:off + size].reshape(shp) for b in range(4)],
                                         axis=SHARD_AXIS[n] - 1)
            off += size

    early_shards, early_keys = shards_of(True)
    unpack(_gather_halves_call(early_shards), early_keys)
    late_shards, late_keys = shards_of(False)
    taps = _gather_weights_call("gather_taps", _pad_rows(conv_w.reshape(-1), 8)).reshape(4, -1)
    taps = jnp.concatenate([taps[b, :conv_w.size].reshape(conv_w.shape) for b in range(4)], axis=2)

    inv_freq = ROPE_THETA ** (-jnp.arange(0, ROPE, 2, dtype=F32) / ROPE)
    half = ROPE // 2
    zeros = lambda n: jnp.zeros((n,), F32)
    ones = jnp.ones((half,), F32)
    invf = jnp.concatenate([zeros(NOPE), inv_freq, inv_freq, zeros(HP - QK_DIM)]).reshape(1, HP)
    mask_a = jnp.concatenate([zeros(NOPE), -ones, zeros(HP - NOPE - half)]).reshape(1, HP)
    mask_b = jnp.concatenate([zeros(NOPE + half), ones, zeros(HP - QK_DIM)]).reshape(1, HP)
    tables = _rope_tables_call(positions.reshape(S, 1), invf, mask_a, mask_b)

    even_w = {}

    def even_weights_of(e):
        if e not in even_w:
            even_w[e] = _even_weights(full['even_w_in'][e], full['w_uq'][e], full['w_ukv'][e], full['even_w_out'][e],
                                      q_norm[e], kv_norm[e], sg_norm[e], sg_w[e], sg_b[e])
        return even_w[e]

    def gain_row(arr, l):
        return arr[l].reshape(1, D)

    saved = []
    h, ht = _rmsnorm_call("first_norm", xs, gain_row(ffn_pre_norm, 0))
    xc = xs
    for l in range(depth):
        xc, h, ht, s_pre = _ffn_fwd(f"l{l}_pre", xc, h, ht, full['ffn_pre_w_gate'][l], full['ffn_pre_w_up'][l],
                                    full['ffn_pre_w_down'][l], gain_row(mix_norm, l))
        if l % 2 == 0:
            xc, h, s_mix, gathered = _even_mixer_fwd(f"l{l}_mix", xc, h, ht, even_weights_of(l // 2), tables,
                                                     gain_row(ffn_post_norm, l), late_shards if l == 0 else ())
            if l == 0:
                unpack(gathered, late_keys)
        else:
            o = l // 2
            xc, h, s_mix = _conv_mixer_fwd(f"l{l}_mix", xc, h, ht, full['conv_w_in'][o], taps[o],
                                           full['conv_w_out'][o], gain_row(ffn_post_norm, l))
        g_next = gain_row(ffn_pre_norm, l + 1) if l + 1 < depth else final_norm.reshape(1, D)
        xc, h, ht, s_post = _ffn_fwd(f"l{l}_post", xc, h, None, full['ffn_post_w_gate'][l], full['ffn_post_w_up'][l],
                                     full['ffn_post_w_down'][l], g_next)
        saved.append((s_pre, s_mix, s_post))

    dx, dxb, d_final, loss_part = _loss_call(xc, target, final_norm.reshape(1, D))
    loss = lax.psum(loss_part[0, 0], ("x", "y", "c"))

    gl = {n: [None] * w_loc[n].shape[0] for n in WEIGHTS if n != 'final_norm'}
    core = lax.axis_index("c").astype(jnp.int32).reshape(1)

    def pair_sums(first, tag):
        keys = [[(n, l) for n in group for l in layers_of(n, first)] for group in (GROUP_A, GROUP_B, GROUP_C)]

        def rows_blocked(n, l):
            g = gl[n][l]
            return g.reshape(4, g.shape[0] // 4, g.shape[1]).astype(_WIRE_DTYPE)

        pack_a = jnp.concatenate([gl[n][l] for n, l in keys[0]], axis=1)
        pack_b = jnp.concatenate([gl[n][l] if n in FFN_WEIGHTS else rows_blocked(n, l) for n, l in keys[1]], axis=1)
        pack_c = jnp.stack([_pad_rows(jnp.concatenate(
            [_shard_slice(gl[n][l], SHARD_AXIS[n] - 1, b).astype(_WIRE_DTYPE).reshape(-1) for n, l in keys[2]]),
            PACK_ROW_MULT) for b in range(4)])
        packs = [_pad_axis(p, 1, PACK_ROW_MULT) for p in (pack_a, pack_b, pack_c)]
        packs = [p.reshape(4, 2, p.shape[1] // 2, p.shape[2]) for p in packs]
        theirs = _pair_exchange_call(packs, tag)
        return [_pair_add_call(f"pair_add_{tag}_{i}", p, t, core)
                for i, (p, t) in enumerate(zip(packs, theirs))], keys
    for l in reversed(range(depth)):
        s_pre, s_mix, s_post = saved[l]
        dx, dxb, dgain, dwg, dwu, dwd = _ffn_bwd(f"l{l}_post", s_post, dx, dxb, full['ffn_post_w_gate'][l],
                                                 full['ffn_post_w_up'][l], full['ffn_post_w_down'][l],
                                                 gain_row(ffn_post_norm, l))
        gl['ffn_post_norm'][l] = dgain[0]
        gl['ffn_post_w_gate'][l], gl['ffn_post_w_up'][l], gl['ffn_post_w_down'][l] = dwg, dwu, dwd
        if l % 2 == 0:
            e = l // 2
            if l == 0:
                pairs_rest, keys_rest = pair_sums(False, "rest")
            dx, dxb, dgain, eg, arrived = _even_mixer_bwd(f"l{l}_mix", s_mix, dx, dxb, even_weights_of(e), tables,
                                                          gain_row(mix_norm, l), pairs_rest if l == 0 else ())
            if l == 0:
                arrived_rest = arrived
            for n, val in _even_grads_unpad(eg).items():
                gl[n][e] = val
        else:
            o = l // 2
            dx, dxb, dgain, dw_in, dtaps, dw_out = _conv_mixer_bwd(f"l{l}_mix", s_mix, dx, dxb, full['conv_w_in'][o],
                                                                   taps[o], full['conv_w_out'][o],
                                                                   gain_row(mix_norm, l))
            gl['conv_w_in'][o], gl['conv_w'][o], gl['conv_w_out'][o] = dw_in, dtaps, dw_out
        gl['mix_norm'][l] = dgain[0]
        dx, dxb, dgain, dwg, dwu, dwd = _ffn_bwd(f"l{l}_pre", s_pre, dx, dxb, full['ffn_pre_w_gate'][l],
                                                 full['ffn_pre_w_up'][l], full['ffn_pre_w_down'][l],
                                                 gain_row(ffn_pre_norm, l))
        gl['ffn_pre_norm'][l] = dgain[0]
        gl['ffn_pre_w_gate'][l], gl['ffn_pre_w_up'][l], gl['ffn_pre_w_down'][l] = dwg, dwu, dwd
    grad_x = dx.reshape(x.shape)
    part = {n: jnp.stack(gl[n]) for n in gl if n not in FFN_WEIGHTS}
    part['final_norm'] = d_final[0]

    pairs, keys_first = pair_sums(True, "first")
    arrived_first = _chip_scatter_call(pairs)
    mine = [_sum_slots_call(f"sum_grad_slots_{i}", r, core) for i, r in enumerate(list(arrived_rest) + list(arrived_first))]
    reduced = [t.reshape(-1, t.shape[2]) for t in _sibling_share_call(mine)]
    per_layer = {n: {} for n in SHARDED}
    for (red_a, red_b, red_c), keys in ((reduced[:3], keys_rest), (reduced[3:], keys_first)):
        for idx, (n, l) in enumerate(keys[0]):
            per_layer[n][l] = red_a[idx * D:(idx + 1) * D]
        row = 0
        for n, l in keys[1]:
            rows = w_loc[n].shape[1]
            per_layer[n][l] = red_b[row:row + rows]
            row += rows
        red_c = red_c.reshape(-1)
        off = 0
        for n, l in keys[2]:
            shp = w_loc[n].shape[1:]
            size = int(np.prod(shp))
            per_layer[n][l] = red_c[off:off + size].reshape(shp)
            off += size
    grads = {n: jnp.stack([per_layer[n][l] for l in range(w_loc[n].shape[0])]) for n in SHARDED}

    small = _pad_rows(jnp.concatenate([part[n].reshape(-1) for n in REPLICATED]), 8)
    small_sum = _allreduce_small_call(small).reshape(-1)
    off = 0
    for n in REPLICATED:
        size = int(np.prod(w_loc[n].shape))
        grads[n] = small_sum[off:off + size].reshape(w_loc[n].shape)
        off += size

    deltas, new_m, new_v = {}, {}, {}
    for n in WEIGHTS:
        deltas[n], new_m[n], new_v[n] = _adamw_call("adamw_" + n, w_loc[n], grads[n], m_loc[n], v_loc[n])
    return (loss, grad_x, *[grads[n] for n in WEIGHTS], *[deltas[n] for n in WEIGHTS],
            *[new_m[n] for n in WEIGHTS], *[new_v[n] for n in WEIGHTS])
```

```python
import functools

import numpy as np
import jax
import jax.numpy as jnp
from jax import lax
from jax.experimental import pallas as pl
from jax.experimental.pallas import tpu as pltpu

F32 = jnp.float32
BF16 = jnp.bfloat16
_MXU_DTYPE = jnp.bfloat16
_WIRE_DTYPE = jnp.bfloat16
_VMEM_LIMIT = 52 * 1024 * 1024
_LANES = 128
_ATT_BLOCK = 512
_ROW_TILE = 512
_SG_TILE = 1024

NORM_EPS = 1e-6
HEADS = 8
NOPE = 64
ROPE = 32
VDIM = 64
QK_DIM = NOPE + ROPE
HP = 128
Q_LORA = 384
KV_LORA = 256
SG_WIDTH = 512
SG_GROUPS = 8
SG_GDIM = 64
SG_CHUNK = 128
ROPE_THETA = 10000.0
PROJ_W = Q_LORA + KV_LORA + HP + 2 * SG_WIDTH
ADAM_LR = 0.001
ADAM_B1 = 0.9
ADAM_B2 = 0.999
ADAM_EPS = 1e-08
ADAM_WD = 0.01
ADAM_STEP = 10
MESH = pl.DeviceIdType.MESH
PACK_COLS = 1024
PACK_ROW_MULT = 256

SHARDED = ['ffn_pre_w_gate', 'ffn_pre_w_up', 'ffn_pre_w_down', 'ffn_post_w_gate', 'ffn_post_w_up',
           'ffn_post_w_down', 'even_w_in', 'w_uq', 'w_ukv', 'even_w_out', 'conv_w_in', 'conv_w', 'conv_w_out']
SHARD_AXIS = {'ffn_pre_w_gate': 2, 'ffn_pre_w_up': 2, 'ffn_pre_w_down': 1, 'ffn_post_w_gate': 2,
              'ffn_post_w_up': 2, 'ffn_post_w_down': 1, 'even_w_in': 2, 'w_uq': 2, 'w_ukv': 2,
              'even_w_out': 1, 'conv_w_in': 2, 'conv_w': 2, 'conv_w_out': 1}
FFN_WEIGHTS = ['ffn_pre_w_gate', 'ffn_pre_w_up', 'ffn_pre_w_down', 'ffn_post_w_gate', 'ffn_post_w_up',
               'ffn_post_w_down']
GROUP_A = ['ffn_pre_w_gate', 'ffn_pre_w_up', 'ffn_post_w_gate', 'ffn_post_w_up']
GROUP_B = ['ffn_pre_w_down', 'ffn_post_w_down', 'even_w_out', 'conv_w_out']
GROUP_C = ['even_w_in', 'w_uq', 'w_ukv', 'conv_w_in', 'conv_w']
GATHER_C = ['even_w_in', 'w_uq', 'w_ukv', 'conv_w_in']
REPLICATED = ['ffn_pre_norm', 'mix_norm', 'ffn_post_norm', 'q_norm', 'kv_norm', 'sg_norm', 'sg_w', 'sg_b',
              'final_norm']
WEIGHTS = ['ffn_pre_norm', 'ffn_pre_w_gate', 'ffn_pre_w_up', 'ffn_pre_w_down', 'mix_norm', 'ffn_post_norm',
           'ffn_post_w_gate', 'ffn_post_w_up', 'ffn_post_w_down', 'even_w_in', 'q_norm', 'w_uq', 'kv_norm',
           'w_ukv', 'sg_norm', 'sg_w', 'sg_b', 'even_w_out', 'conv_w_in', 'conv_w', 'conv_w_out', 'final_norm']


def _params(sem=None):
    return pltpu.CompilerParams(vmem_limit_bytes=_VMEM_LIMIT,
                                **({} if sem is None else {'dimension_semantics': sem}))


def _pick(n, pref, mult):
    best = None
    t = mult
    while t <= min(n, pref):
        if n % t == 0:
            best = t
        t += mult
    return n if best is None else best


def _mx(v):
    return v if v.dtype == _MXU_DTYPE else v.astype(_MXU_DTYPE)


def _sigmoid(a):
    return 1.0 / (1.0 + jnp.exp(-a))


def _rms_stats(x):
    rstd = lax.rsqrt(jnp.mean(x * x, axis=-1, keepdims=True) + NORM_EPS)
    return x * rstd, rstd


def _rms_bwd(x, g, dh):
    xhat, rstd = _rms_stats(x)
    gdh = g * dh
    dx = rstd * (gdh - xhat * jnp.mean(gdh * xhat, axis=-1, keepdims=True))
    return dx, dh * xhat


def _fused_matmul(name, mode, lhs, rhs, prods, n_acc, epilogue, out_dtypes, M, N, K, tm, tn, tk,
                  tile_extras=(), row_extras=(), mrow_extras=(), n_colsum=0):
    gj, gi, gk = N // tn, M // tm, K // tk
    assert gj * tn == N and gi * tm == M and gk * tk == K, (name, M, N, K, tm, tn, tk)
    dims = {'nn': (((1,), (0,)), ((), ())), 'nt': (((1,), (1,)), ((), ())), 'tn': (((0,), (0,)), ((), ()))}[mode]

    def lhs_spec(roff, coff, kb):
        kb = tk if kb is None else kb
        if mode == 'tn':
            return pl.BlockSpec((kb, tm), lambda j, i, k: (k + roff, i + coff))
        return pl.BlockSpec((tm, kb), lambda j, i, k: (i + roff, k + coff))

    def rhs_spec(roff, coff, kb):
        kb = tk if kb is None else kb
        if mode == 'nt':
            return pl.BlockSpec((tn, kb), lambda j, i, k: (j + roff, k + coff))
        return pl.BlockSpec((kb, tn), lambda j, i, k: (k + roff, j + coff))

    in_specs = [lhs_spec(*a[1:]) for a in lhs] + [rhs_spec(*a[1:]) for a in rhs]
    in_specs += [pl.BlockSpec((tm, tn), lambda j, i, k: (i, j)) for _ in tile_extras]
    in_specs += [pl.BlockSpec((1, tn), lambda j, i, k: (0, j)) for _ in row_extras]
    in_specs += [pl.BlockSpec((tm, a.shape[1]), lambda j, i, k: (i, 0)) for a in mrow_extras]
    n_out = len(out_dtypes)
    out_shape = [jax.ShapeDtypeStruct((M, N), d) for d in out_dtypes]
    out_specs = [pl.BlockSpec((tm, tn), lambda j, i, k: (i, j)) for _ in out_dtypes]
    out_shape += [jax.ShapeDtypeStruct((1, N), F32) for _ in range(n_colsum)]
    out_specs += [pl.BlockSpec((1, tn), lambda j, i, k: (0, j)) for _ in range(n_colsum)]
    scratch = [pltpu.VMEM((tm, tn), F32) for _ in range(n_acc)] if gk > 1 else []
    nl, nr, nt, nrw, nm = len(lhs), len(rhs), len(tile_extras), len(row_extras), len(mrow_extras)

    def body(*refs):
        pos = 0
        lhs_refs = refs[pos:pos + nl]; pos += nl
        rhs_refs = refs[pos:pos + nr]; pos += nr
        tile_refs = refs[pos:pos + nt]; pos += nt
        row_refs = refs[pos:pos + nrw]; pos += nrw
        mrow_refs = refs[pos:pos + nm]; pos += nm
        out_refs = refs[pos:pos + n_out]; pos += n_out
        cs_refs = refs[pos:pos + n_colsum]; pos += n_colsum
        acc_refs = refs[pos:]
        i = pl.program_id(1)
        k = pl.program_id(2)

        def partials():
            res = [None] * n_acc
            for (li, ri, ai) in prods:
                d = lax.dot_general(_mx(lhs_refs[li][...]), _mx(rhs_refs[ri][...]), dims,
                                    preferred_element_type=F32)
                res[ai] = d if res[ai] is None else res[ai] + d
            return res

        def finish(accs):
            outs = epilogue(accs, [r[...] for r in tile_refs], [r[...] for r in row_refs],
                            [r[...] for r in mrow_refs])
            for r, o in zip(out_refs, outs[:n_out]):
                r[...] = o.astype(r.dtype)
            for r, c in zip(cs_refs, outs[n_out:]):
                c = jnp.sum(c, axis=0, keepdims=True)

                @pl.when(i == 0)
                def _():
                    r[...] = c

                @pl.when(i != 0)
                def _():
                    r[...] += c

        if gk == 1:
            finish(partials())
        else:
            p = partials()

            @pl.when(k == 0)
            def _():
                for r, v in zip(acc_refs, p):
                    r[...] = v

            @pl.when(k != 0)
            def _():
                for r, v in zip(acc_refs, p):
                    r[...] += v

            @pl.when(k == gk - 1)
            def _():
                finish([r[...] for r in acc_refs])

    res = pl.pallas_call(
        body, name=name, grid=(gj, gi, gk), in_specs=in_specs, out_specs=out_specs, out_shape=out_shape,
        scratch_shapes=scratch, compiler_params=_params(("arbitrary", "arbitrary", "arbitrary")),
    )(*[a[0] for a in lhs], *[a[0] for a in rhs], *tile_extras, *row_extras, *mrow_extras)
    return res


def _op(a, roff=0, coff=0, kb=None):
    return (a, roff, coff, kb)


def _ident_epi(scale=None):
    def epi(accs, tiles, rows, mrows):
        return [a if scale is None else a * scale for a in accs]
    return epi


def _resid_norm_epi(scale):
    def epi(accs, tiles, rows, mrows):
        x_new = tiles[0] + scale * accs[0]
        xhat, _ = _rms_stats(x_new)
        return [x_new, xhat * rows[0]]
    return epi


def _norm_bwd_epi(accs, tiles, rows, mrows):
    dx_n, dg = _rms_bwd(tiles[0], rows[0], accs[0])
    dx = tiles[1] + dx_n
    return [dx, dx, dg]


def _rmsnorm_call(name, x, g):
    S, D = x.shape
    tm = _pick(S, _ROW_TILE, 128)

    def body(x_ref, g_ref, h_ref, ht_ref):
        h = _rms_stats(x_ref[...])[0] * g_ref[...]
        h_ref[...] = h.astype(h_ref.dtype)
        ht_ref[...] = jnp.transpose(h).astype(ht_ref.dtype)

    return pl.pallas_call(
        body, name=name, grid=(S // tm,),
        in_specs=[pl.BlockSpec((tm, D), lambda i: (i, 0)), pl.BlockSpec((1, D), lambda i: (0, 0))],
        out_specs=[pl.BlockSpec((tm, D), lambda i: (i, 0)), pl.BlockSpec((D, tm), lambda i: (0, i))],
        out_shape=[jax.ShapeDtypeStruct((S, D), BF16), jax.ShapeDtypeStruct((D, S), BF16)],
        compiler_params=_params(("arbitrary",)),
    )(x, g)


def _loss_call(x, target, g):
    S, D = x.shape
    tm = _pick(S, _ROW_TILE, 8)

    def body(x_ref, t_ref, g_ref, dx_ref, dxb_ref, dg_ref, loss_ref):
        i = pl.program_id(0)
        x_t = x_ref[...]
        gain = g_ref[...]
        xhat, _ = _rms_stats(x_t)
        diff = xhat * gain - t_ref[...]
        dy = diff * (1.0 / D)
        dx, dg = _rms_bwd(x_t, gain, dy)
        dx_ref[...] = dx
        dxb_ref[...] = dx.astype(BF16)
        dg = jnp.sum(dg, axis=0, keepdims=True)
        part = 0.5 * jnp.sum(jnp.sum(diff * diff, axis=1, keepdims=True), axis=0, keepdims=True) * (1.0 / D)
        part = jnp.broadcast_to(part, (1, _LANES))

        @pl.when(i == 0)
        def _():
            dg_ref[...] = dg
            loss_ref[...] = part

        @pl.when(i != 0)
        def _():
            dg_ref[...] += dg
            loss_ref[...] += part

    row = lambda i: (i, 0)
    fixed = lambda i: (0, 0)
    return pl.pallas_call(
        body, name="loss_head", grid=(S // tm,),
        in_specs=[pl.BlockSpec((tm, D), row), pl.BlockSpec((tm, D), row), pl.BlockSpec((1, D), fixed)],
        out_specs=[pl.BlockSpec((tm, D), row), pl.BlockSpec((tm, D), row), pl.BlockSpec((1, D), fixed),
                   pl.BlockSpec((1, _LANES), fixed)],
        out_shape=[jax.ShapeDtypeStruct((S, D), F32), jax.ShapeDtypeStruct((S, D), BF16),
                   jax.ShapeDtypeStruct((1, D), F32), jax.ShapeDtypeStruct((1, _LANES), F32)],
        compiler_params=_params(("arbitrary",)),
    )(x, target, g)


def _rope_tables_call(pos_col, invf, mask_a, mask_b):
    S = pos_col.shape[0]
    tm = _pick(S, _ROW_TILE, 8)

    def body(p_ref, f_ref, a_ref, b_ref, cos_ref, sa_ref, sb_ref):
        ang = p_ref[...].astype(F32) * f_ref[...]
        sn = jnp.sin(ang)
        cos_ref[...] = jnp.cos(ang)
        sa_ref[...] = sn * a_ref[...]
        sb_ref[...] = sn * b_ref[...]

    row = lambda i: (i, 0)
    fixed = lambda i: (0, 0)
    return pl.pallas_call(
        body, name="rope_tables", grid=(S // tm,),
        in_specs=[pl.BlockSpec((tm, 1), row)] + [pl.BlockSpec((1, HP), fixed)] * 3,
        out_specs=[pl.BlockSpec((tm, HP), row)] * 3,
        out_shape=[jax.ShapeDtypeStruct((S, HP), F32)] * 3, compiler_params=_params(("arbitrary",)),
    )(pos_col, invf, mask_a, mask_b)


def _rope(t, cos, sa, sb):
    return t * cos + pltpu.roll(t, HP - ROPE // 2, 1) * sa + pltpu.roll(t, ROPE // 2, 1) * sb


def _rope_t(d, cos, sa, sb):
    return d * cos + pltpu.roll(d * sa, ROPE // 2, 1) + pltpu.roll(d * sb, HP - ROPE // 2, 1)


def _gelu(z):
    return 0.5 * z * (1.0 + lax.erf(z * np.float32(1.0 / np.sqrt(2.0))))


def _gelu_grad(z):
    cdf = 0.5 * (1.0 + lax.erf(z * np.float32(1.0 / np.sqrt(2.0))))
    pdf = np.float32(1.0 / np.sqrt(2.0 * np.pi)) * jnp.exp(-0.5 * z * z)
    return cdf + z * pdf


_CQ0, _CKV0, _KR0, _Z0 = 0, Q_LORA, Q_LORA + KV_LORA, Q_LORA + KV_LORA + HP


def _even_prep_call(proj, qn, kvn, sgn, cos, sa, sb):
    S = proj.shape[0]
    tm = _pick(S, 256, 8)

    def body(p_ref, qn_ref, kvn_ref, sgn_ref, cos_ref, sa_ref, sb_ref, cq_ref, ckv_ref, kr_ref, u_ref, v_ref):
        cq = p_ref[:, _CQ0:_CQ0 + Q_LORA]
        cq_ref[...] = (_rms_stats(cq)[0] * qn_ref[...]).astype(BF16)
        ckv = p_ref[:, _CKV0:_CKV0 + KV_LORA]
        ckv_ref[...] = (_rms_stats(ckv)[0] * kvn_ref[...]).astype(BF16)
        kr = p_ref[:, _KR0:_KR0 + HP]
        kr_ref[...] = _rope(kr, cos_ref[...], sa_ref[...], sb_ref[...]).astype(BF16)
        u_ref[...] = _gelu(p_ref[:, _Z0:_Z0 + SG_WIDTH]).astype(BF16)
        zv = _gelu(p_ref[:, _Z0 + SG_WIDTH:_Z0 + 2 * SG_WIDTH])
        v_ref[...] = (_rms_stats(zv)[0] * sgn_ref[...]).astype(BF16)

    row = lambda i: (i, 0)
    fixed = lambda i: (0, 0)
    widths = [Q_LORA, KV_LORA, HP, SG_WIDTH, SG_WIDTH]
    return pl.pallas_call(
        body, name="even_prep", grid=(S // tm,),
        in_specs=[pl.BlockSpec((tm, PROJ_W), row), pl.BlockSpec((1, Q_LORA), fixed),
                  pl.BlockSpec((1, KV_LORA), fixed), pl.BlockSpec((1, SG_WIDTH), fixed)]
        + [pl.BlockSpec((tm, HP), row)] * 3,
        out_specs=[pl.BlockSpec((tm, w), row) for w in widths],
        out_shape=[jax.ShapeDtypeStruct((S, w), BF16) for w in widths],
        compiler_params=_params(("arbitrary",)),
    )(proj, qn, kvn, sgn, cos, sa, sb)


def _even_prep_bwd_call(proj, qn, kvn, sgn, cos, sa, sb, dcqn, dckvn, dk, du, dvn):
    S = proj.shape[0]
    tm = _pick(S, 256, 8)

    def body(p_ref, qn_ref, kvn_ref, sgn_ref, cos_ref, sa_ref, sb_ref, dcq_ref, dckv_ref, dk_ref, du_ref,
             dvn_ref, dp_ref, dqn_ref, dkvn_ref, dsgn_ref):
        i = pl.program_id(0)
        dcq, gq = _rms_bwd(p_ref[:, _CQ0:_CQ0 + Q_LORA], qn_ref[...], dcq_ref[...])
        dp_ref[:, _CQ0:_CQ0 + Q_LORA] = dcq.astype(BF16)
        dckv, gkv = _rms_bwd(p_ref[:, _CKV0:_CKV0 + KV_LORA], kvn_ref[...], dckv_ref[...])
        dp_ref[:, _CKV0:_CKV0 + KV_LORA] = dckv.astype(BF16)
        dkr = dk_ref[:, 0:HP].astype(F32)
        for h in range(1, HEADS):
            dkr = dkr + dk_ref[:, h * HP:(h + 1) * HP].astype(F32)
        lane = lax.broadcasted_iota(jnp.int32, dkr.shape, 1)
        dkr = jnp.where((lane >= NOPE) & (lane < QK_DIM), dkr, 0.0)
        dp_ref[:, _KR0:_KR0 + HP] = _rope_t(dkr, cos_ref[...], sa_ref[...], sb_ref[...]).astype(BF16)
        zu = p_ref[:, _Z0:_Z0 + SG_WIDTH]
        dp_ref[:, _Z0:_Z0 + SG_WIDTH] = (du_ref[...].astype(F32) * _gelu_grad(zu)).astype(BF16)
        zv = p_ref[:, _Z0 + SG_WIDTH:_Z0 + 2 * SG_WIDTH]
        dgv, gsg = _rms_bwd(_gelu(zv), sgn_ref[...], dvn_ref[...].astype(F32))
        dp_ref[:, _Z0 + SG_WIDTH:_Z0 + 2 * SG_WIDTH] = (dgv * _gelu_grad(zv)).astype(BF16)
        sums = [jnp.sum(t, axis=0, keepdims=True) for t in (gq, gkv, gsg)]

        @pl.when(i == 0)
        def _():
            for r, s in zip((dqn_ref, dkvn_ref, dsgn_ref), sums):
                r[...] = s

        @pl.when(i != 0)
        def _():
            for r, s in zip((dqn_ref, dkvn_ref, dsgn_ref), sums):
                r[...] += s

    row = lambda i: (i, 0)
    fixed = lambda i: (0, 0)
    return pl.pallas_call(
        body, name="even_prep_bwd", grid=(S // tm,),
        in_specs=[pl.BlockSpec((tm, PROJ_W), row), pl.BlockSpec((1, Q_LORA), fixed),
                  pl.BlockSpec((1, KV_LORA), fixed), pl.BlockSpec((1, SG_WIDTH), fixed)]
        + [pl.BlockSpec((tm, HP), row)] * 3
        + [pl.BlockSpec((tm, Q_LORA), row), pl.BlockSpec((tm, KV_LORA), row),
           pl.BlockSpec((tm, HEADS * HP), row), pl.BlockSpec((tm, SG_WIDTH), row),
           pl.BlockSpec((tm, SG_WIDTH), row)],
        out_specs=[pl.BlockSpec((tm, PROJ_W), row), pl.BlockSpec((1, Q_LORA), fixed),
                   pl.BlockSpec((1, KV_LORA), fixed), pl.BlockSpec((1, SG_WIDTH), fixed)],
        out_shape=[jax.ShapeDtypeStruct((S, PROJ_W), BF16), jax.ShapeDtypeStruct((1, Q_LORA), F32),
                   jax.ShapeDtypeStruct((1, KV_LORA), F32), jax.ShapeDtypeStruct((1, SG_WIDTH), F32)],
        compiler_params=_params(("arbitrary",)),
    )(proj, qn, kvn, sgn, cos, sa, sb, dcqn, dckvn, dk, du, dvn)


def _causal_mask(rows, cols):
    r = lax.broadcasted_iota(jnp.int32, (rows, cols), 0)
    c = lax.broadcasted_iota(jnp.int32, (rows, cols), 1)
    return c <= r


def _flash_fwd_call(q, k, v, carry=()):
    S = q.shape[0]
    tb = _pick(S, _ATT_BLOCK, 128)
    nq = S // tb
    nt_dims = (((1,), (1,)), ((), ()))

    nc = len(carry)

    def body(*refs):
        q_ref, k_ref, v_ref = refs[:3]
        o_ref, lse_ref = refs[3 + nc:5 + nc]
        s_a, s_b, m_ref, acc_ref = refs[5 + 2 * nc:9 + 2 * nc]
        h = pl.program_id(0)
        i = pl.program_id(1)
        if nc:
            send, forward, finish = _gather_phases(refs[3:3 + nc], refs[5 + nc:5 + 2 * nc], *refs[9 + 2 * nc:])
            pl.when((h == 0) & (i == 0))(send)
            pl.when((h == HEADS // 2) & (i == 0))(forward)

        def scores(buf, j):
            k_t = k_ref[pl.ds(pl.multiple_of(j * tb, tb), tb), :]
            buf[...] = lax.dot_general(q_ref[...], k_t, nt_dims, preferred_element_type=F32)

        def update(buf, j, masked):
            v_t = v_ref[pl.ds(pl.multiple_of(j * tb, tb), tb), :]
            s = buf[...]
            if masked:
                s = jnp.where(_causal_mask(tb, tb), s, -1e30)
            m = m_ref[...]
            m_new = jnp.maximum(m, jnp.max(s, axis=1, keepdims=True))
            alpha = jnp.exp(m - m_new)
            p = jnp.exp(s - m_new)
            acc_ref[...] = alpha * acc_ref[...] + jnp.dot(p.astype(v_t.dtype), v_t, preferred_element_type=F32)
            m_ref[...] = m_new

        m_ref[...] = jnp.full((tb, 1), -1e30, F32)
        acc_ref[...] = jnp.zeros((tb, HP), F32)
        scores(s_a, 0)
        pairs = i // 2

        def two_blocks(t, carry):
            scores(s_b, 2 * t + 1)
            update(s_a, 2 * t, False)
            scores(s_a, 2 * t + 2)
            update(s_b, 2 * t + 1, False)
            return carry

        lax.fori_loop(0, pairs, two_blocks, 0)

        @pl.when(2 * pairs == i)
        def _():
            update(s_a, i, True)

        @pl.when(2 * pairs != i)
        def _():
            scores(s_b, i)
            update(s_a, i - 1, False)
            update(s_b, i, True)

        acc = acc_ref[...]
        l = acc[:, VDIM:VDIM + 1]
        lane = lax.broadcasted_iota(jnp.int32, (tb, HP), 1)
        o_ref[...] = jnp.where(lane < VDIM, acc / l, 0.0).astype(o_ref.dtype)
        lse = jnp.broadcast_to(m_ref[...] + jnp.log(l), (tb, HP))
        lse_ref[0, 0] = jnp.transpose(lse)[0:8, :]
        if nc:
            pl.when((h == HEADS - 1) & (i == nq - 1))(finish)

    start = _gather_start(carry)
    any_spec = pl.BlockSpec(memory_space=pl.ANY)
    res = pl.pallas_call(
        body, name="flash_fwd_gather" if nc else "flash_fwd", grid=(HEADS, nq),
        in_specs=[pl.BlockSpec((tb, HP), lambda h, i: (i, h)), pl.BlockSpec((S, HP), lambda h, i: (0, h)),
                  pl.BlockSpec((S, HP), lambda h, i: (0, h))] + [any_spec] * nc,
        out_specs=[pl.BlockSpec((tb, HP), lambda h, i: (i, h)),
                   pl.BlockSpec((1, 1, 8, tb), lambda h, i: (h, i, 0, 0))] + [any_spec] * nc,
        out_shape=[jax.ShapeDtypeStruct((S, HEADS * HP), q.dtype), jax.ShapeDtypeStruct((HEADS, nq, 8, tb), F32)]
        + [jax.ShapeDtypeStruct(t.shape, t.dtype) for t in start],
        scratch_shapes=[pltpu.VMEM((tb, tb), F32), pltpu.VMEM((tb, tb), F32), pltpu.VMEM((tb, 1), F32),
                        pltpu.VMEM((tb, HP), F32)] + ([pltpu.SemaphoreType.DMA((n,)) for n in _gather_sems(nc)]
                                                      if nc else []),
        input_output_aliases={3 + a: 2 + a for a in range(nc)},
        compiler_params=pltpu.CompilerParams(vmem_limit_bytes=_VMEM_LIMIT, has_side_effects=bool(nc),
                                             dimension_semantics=("arbitrary", "arbitrary")),
    )(q, k, v, *start)
    return res[0], res[1], list(res[2:])


def _attn_delta_call(o, do):
    S = o.shape[0]
    tb = _pick(S, _ATT_BLOCK, 128)
    nq = S // tb
    nb = _pick(nq, 4, 1)

    def body(o_ref, do_ref, d_ref):
        for r in range(nb):
            rows = slice(r * tb, (r + 1) * tb)
            d = jnp.sum(o_ref[rows, :].astype(F32) * do_ref[rows, :].astype(F32), axis=1, keepdims=True)
            d_ref[0, r] = jnp.transpose(jnp.broadcast_to(d, (tb, HP)))[0:8, :]

    return pl.pallas_call(
        body, name="attn_delta", grid=(HEADS, nq // nb),
        in_specs=[pl.BlockSpec((nb * tb, HP), lambda h, i: (i, h))] * 2,
        out_specs=pl.BlockSpec((1, nb, 8, tb), lambda h, i: (h, i, 0, 0)),
        out_shape=jax.ShapeDtypeStruct((HEADS, nq, 8, tb), F32), compiler_params=_params(("arbitrary", "arbitrary")),
    )(o, do)


def _flash_bwd_call(q, k, v, do, lse, delta, carry=()):
    S = q.shape[0]
    tb = _pick(S, _ATT_BLOCK, 128)
    nq = S // tb
    nt_dims = (((1,), (1,)), ((), ()))
    tn_dims = (((0,), (0,)), ((), ()))
    nc = len(carry)

    def body(*refs):
        q_ref, do_ref, lse_ref, dl_ref, k_ref, v_ref = refs[:6]
        dq_ref, dk_ref, dv_ref = refs[6 + nc:9 + nc]
        st_a, dp_a, st_b, dp_b, dk_acc, dv_acc = refs[9 + 2 * nc:15 + 2 * nc]
        h = pl.program_id(0)
        j = pl.program_id(1)
        if nc:
            send, finish = _scatter_phases(refs[6:6 + nc], refs[9 + nc:9 + 2 * nc], *refs[15 + 2 * nc:])
            pl.when((h == 0) & (j == 0))(send)

        @pl.when(j == 0)
        def _():
            dq_ref[...] = jnp.zeros_like(dq_ref)

        def rows_of(i):
            return pl.ds(pl.multiple_of(i * tb, tb), tb)

        def scores(st_buf, dp_buf, i):
            st_buf[...] = lax.dot_general(k_ref[...], q_ref[rows_of(i), :], nt_dims, preferred_element_type=F32)
            dp_buf[...] = lax.dot_general(v_ref[...], do_ref[rows_of(i), :], nt_dims, preferred_element_type=F32)

        def update(st_buf, dp_buf, i, masked):
            q_t = q_ref[rows_of(i), :]
            do_t = do_ref[rows_of(i), :]
            pt = jnp.exp(st_buf[...] - lse_ref[0, i, 0:1, :])
            if masked:
                pt = jnp.where(jnp.transpose(_causal_mask(tb, tb)), pt, 0.0)
            dst = (pt * (dp_buf[...] - dl_ref[0, i, 0:1, :])).astype(q_t.dtype)
            dv_acc[...] += jnp.dot(pt.astype(do_t.dtype), do_t, preferred_element_type=F32)
            dk_acc[...] += jnp.dot(dst, q_t, preferred_element_type=F32)
            dq_ref[rows_of(i), :] += lax.dot_general(dst, k_ref[...], tn_dims, preferred_element_type=F32)

        last = nq - 1
        dk_acc[...] = jnp.zeros((tb, HP), F32)
        dv_acc[...] = jnp.zeros((tb, HP), F32)
        scores(st_b, dp_b, j)
        scores(st_a, dp_a, jnp.minimum(j + 1, last))
        update(st_b, dp_b, j, True)
        rest = last - j
        pairs = rest // 2

        def two_blocks(t, carry):
            i0 = j + 1 + 2 * t
            scores(st_b, dp_b, i0 + 1)
            update(st_a, dp_a, i0, False)
            scores(st_a, dp_a, jnp.minimum(i0 + 2, last))
            update(st_b, dp_b, i0 + 1, False)
            return carry

        lax.fori_loop(0, pairs, two_blocks, 0)

        @pl.when(2 * pairs != rest)
        def _():
            update(st_a, dp_a, last, False)

        dk_ref[...] = dk_acc[...].astype(dk_ref.dtype)
        dv_ref[...] = dv_acc[...].astype(dv_ref.dtype)
        if nc:
            pl.when((h == HEADS - 1) & (j == nq - 1))(finish)

    head = lambda h, j: (0, h)
    blk = lambda h, j: (j, h)
    rows = lambda h, j: (h, 0, 0, 0)
    any_spec = pl.BlockSpec(memory_space=pl.ANY)
    res = pl.pallas_call(
        body, name="flash_bwd_scatter" if nc else "flash_bwd", grid=(HEADS, nq),
        in_specs=[pl.BlockSpec((S, HP), head), pl.BlockSpec((S, HP), head), pl.BlockSpec((1, nq, 8, tb), rows),
                  pl.BlockSpec((1, nq, 8, tb), rows), pl.BlockSpec((tb, HP), blk), pl.BlockSpec((tb, HP), blk)]
        + [any_spec] * nc,
        out_specs=[pl.BlockSpec((S, HP), head), pl.BlockSpec((tb, HP), blk), pl.BlockSpec((tb, HP), blk)]
        + [any_spec] * nc,
        out_shape=[jax.ShapeDtypeStruct((S, HEADS * HP), F32), jax.ShapeDtypeStruct((S, HEADS * HP), BF16),
                   jax.ShapeDtypeStruct((S, HEADS * HP), BF16)]
        + [jax.ShapeDtypeStruct(p.shape, p.dtype) for p in carry],
        scratch_shapes=[pltpu.VMEM((tb, tb), F32)] * 4 + [pltpu.VMEM((tb, HP), F32)] * 2
        + ([pltpu.SemaphoreType.DMA((n,)) for n in _scatter_sems(nc)] if nc else []),
        compiler_params=pltpu.CompilerParams(vmem_limit_bytes=_VMEM_LIMIT, has_side_effects=bool(nc),
                                             dimension_semantics=("arbitrary", "arbitrary")),
    )(q, do, lse, delta, k, v, *carry)
    return res[0], res[1], res[2], list(res[3:])


def _sg_mixed(w_ref, vch, lane_lo):
    blocks = []
    for jb in range(SG_WIDTH // _LANES):
        r = jnp.dot(w_ref[jb], vch[:, jb * _LANES:(jb + 1) * _LANES], preferred_element_type=F32)
        blocks.append(jnp.where(lane_lo, r[0:SG_CHUNK], r[SG_CHUNK:2 * SG_CHUNK]))
    return jnp.concatenate(blocks, axis=1)


def _sgu_fwd_call(vn, u, attn, wst, bexp):
    S = vn.shape[0]
    tm = _pick(S, _SG_TILE, SG_CHUNK)
    AW = HEADS * HP

    def body(v_ref, u_ref, a_ref, w_ref, b_ref, mix_ref):
        lane_lo = lax.broadcasted_iota(jnp.int32, (SG_CHUNK, _LANES), 1) < SG_GDIM
        mix_ref[:, 0:AW] = a_ref[...]
        for c in range(tm // SG_CHUNK):
            rs = slice(c * SG_CHUNK, (c + 1) * SG_CHUNK)
            mixed = _sg_mixed(w_ref, v_ref[rs, :], lane_lo) + b_ref[...]
            mix_ref[rs, AW:AW + SG_WIDTH] = (u_ref[rs, :].astype(F32) * mixed).astype(mix_ref.dtype)

    row = lambda i: (i, 0)
    return pl.pallas_call(
        body, name="sgu_fwd", grid=(S // tm,),
        in_specs=[pl.BlockSpec((tm, SG_WIDTH), row), pl.BlockSpec((tm, SG_WIDTH), row), pl.BlockSpec((tm, AW), row),
                  pl.BlockSpec((SG_WIDTH // _LANES, 2 * SG_CHUNK, SG_CHUNK), lambda i: (0, 0, 0)),
                  pl.BlockSpec((SG_CHUNK, SG_WIDTH), lambda i: (0, 0))],
        out_specs=pl.BlockSpec((tm, AW + SG_WIDTH), row),
        out_shape=jax.ShapeDtypeStruct((S, AW + SG_WIDTH), BF16), compiler_params=_params(("arbitrary",)),
    )(vn, u, attn, wst, bexp)


def _sgu_bwd_call(dmix, vn, u, wst, wst_t, bexp):
    S = vn.shape[0]
    tm = _pick(S, _SG_TILE, SG_CHUNK)
    nblk = SG_WIDTH // _LANES
    col0 = (HEADS * HP) // SG_WIDTH
    nt_dims = (((1,), (1,)), ((), ()))

    def body(d_ref, v_ref, u_ref, w_ref, wt_ref, b_ref, du_ref, dv_ref, dw_ref, db_ref, dbacc_ref):
        i = pl.program_id(0)
        lane_lo = lax.broadcasted_iota(jnp.int32, (SG_CHUNK, _LANES), 1) < SG_GDIM

        @pl.when(i == 0)
        def _():
            dw_ref[...] = jnp.zeros_like(dw_ref)
            dbacc_ref[...] = jnp.zeros_like(dbacc_ref)

        for c in range(tm // SG_CHUNK):
            rs = slice(c * SG_CHUNK, (c + 1) * SG_CHUNK)
            vch = v_ref[rs, :]
            dsg = d_ref[rs, :].astype(F32)
            mixed = _sg_mixed(w_ref, vch, lane_lo) + b_ref[...]
            du_ref[rs, :] = (dsg * mixed).astype(du_ref.dtype)
            dmixed = dsg * u_ref[rs, :].astype(F32)
            dbacc_ref[...] += dmixed
            dmx = dmixed.astype(vch.dtype)
            dv_ref[rs, :] = _sg_mixed(wt_ref, dmx, lane_lo).astype(dv_ref.dtype)
            for jb in range(nblk):
                dblk = dmx[:, jb * _LANES:(jb + 1) * _LANES]
                vblk = vch[:, jb * _LANES:(jb + 1) * _LANES]
                zero = jnp.zeros_like(dblk)
                dw_ref[2 * jb] += lax.dot_general(jnp.where(lane_lo, dblk, zero), vblk, nt_dims,
                                                  preferred_element_type=F32)
                dw_ref[2 * jb + 1] += lax.dot_general(jnp.where(lane_lo, zero, dblk), vblk, nt_dims,
                                                      preferred_element_type=F32)

        @pl.when(i == pl.num_programs(0) - 1)
        def _():
            tri = _causal_mask(SG_CHUNK, SG_CHUNK)
            for g in range(SG_GROUPS):
                dw_ref[g] = jnp.where(tri, dw_ref[g], 0.0)
            lane = lax.broadcasted_iota(jnp.int32, (SG_CHUNK, _LANES), 1)
            out = jnp.zeros((SG_CHUNK, _LANES), F32)
            for g in range(SG_GROUPS):
                blk = dbacc_ref[:, (g // 2) * _LANES:(g // 2 + 1) * _LANES]
                sel = lane_lo if g % 2 == 0 else jnp.logical_not(lane_lo)
                s = jnp.sum(jnp.where(sel, blk, 0.0), axis=1, keepdims=True)
                out = jnp.where(lane == g, s, out)
            db_ref[...] = out

    row = lambda i: (i, 0)
    wspec = pl.BlockSpec((nblk, 2 * SG_CHUNK, SG_CHUNK), lambda i: (0, 0, 0))
    return pl.pallas_call(
        body, name="sgu_bwd", grid=(S // tm,),
        in_specs=[pl.BlockSpec((tm, SG_WIDTH), lambda i: (i, col0)), pl.BlockSpec((tm, SG_WIDTH), row),
                  pl.BlockSpec((tm, SG_WIDTH), row), wspec, wspec,
                  pl.BlockSpec((SG_CHUNK, SG_WIDTH), lambda i: (0, 0))],
        out_specs=[pl.BlockSpec((tm, SG_WIDTH), row), pl.BlockSpec((tm, SG_WIDTH), row),
                   pl.BlockSpec((SG_GROUPS, SG_CHUNK, SG_CHUNK), lambda i: (0, 0, 0)),
                   pl.BlockSpec((SG_CHUNK, _LANES), lambda i: (0, 0))],
        out_shape=[jax.ShapeDtypeStruct((S, SG_WIDTH), BF16), jax.ShapeDtypeStruct((S, SG_WIDTH), BF16),
                   jax.ShapeDtypeStruct((SG_GROUPS, SG_CHUNK, SG_CHUNK), F32),
                   jax.ShapeDtypeStruct((SG_CHUNK, _LANES), F32)],
        scratch_shapes=[pltpu.VMEM((SG_CHUNK, SG_WIDTH), F32)],
        compiler_params=_params(("arbitrary",)),
    )(dmix, vn, u, wst, wst_t, bexp)


def _shift_down(t, halo, n):
    rows = lax.broadcasted_iota(jnp.int32, t.shape, 0)
    out = pltpu.roll(t, n, 0)
    for r in range(n):
        out = jnp.where(rows == r, halo[8 - n + r:8 - n + r + 1, :], out)
    return out


def _shift_up(t, halo, n):
    tm = t.shape[0]
    rows = lax.broadcasted_iota(jnp.int32, t.shape, 0)
    out = pltpu.roll(t, tm - n, 0)
    for r in range(n):
        out = jnp.where(rows == tm - n + r, halo[r:r + 1, :], out)
    return out


def _conv_fwd_call(p, w):
    S, C3 = p.shape
    C = C3 // 3
    tm = _pick(S, _ROW_TILE, 8)
    hb = tm // 8

    def body(p_ref, c_prev, z_prev, w_ref, m_ref):
        i = pl.program_id(0)
        cz = p_ref[:, C:2 * C] * p_ref[:, 2 * C:3 * C]
        czp = jnp.where(i > 0, c_prev[...] * z_prev[...], 0.0)
        y = w_ref[2:3, :] * cz + w_ref[1:2, :] * _shift_down(cz, czp, 1) + w_ref[0:1, :] * _shift_down(cz, czp, 2)
        m_ref[...] = (p_ref[:, 0:C] * y).astype(m_ref.dtype)

    prev = lambda col: (lambda i: (jnp.maximum(i * hb - 1, 0), col))
    return pl.pallas_call(
        body, name="conv_fwd", grid=(S // tm,),
        in_specs=[pl.BlockSpec((tm, C3), lambda i: (i, 0)), pl.BlockSpec((8, C), prev(1)),
                  pl.BlockSpec((8, C), prev(2)), pl.BlockSpec((3, C), lambda i: (0, 0))],
        out_specs=pl.BlockSpec((tm, C), lambda i: (i, 0)),
        out_shape=jax.ShapeDtypeStruct((S, C), BF16), compiler_params=_params(("arbitrary",)),
    )(p, p, p, w)


def _conv_bwd_call(p, w, dm):
    S, C3 = p.shape
    C = C3 // 3
    tm = _pick(S, 256, 8)
    hb = tm // 8
    n_tiles = S // tm

    def body(p_ref, c_prev, z_prev, b_next, dm_ref, dm_next, w_ref, dp_ref, dw_ref):
        i = pl.program_id(0)
        b = p_ref[:, 0:C]
        c = p_ref[:, C:2 * C]
        z = p_ref[:, 2 * C:3 * C]
        cz = c * z
        czp = jnp.where(i > 0, c_prev[...] * z_prev[...], 0.0)
        s1 = _shift_down(cz, czp, 1)
        s2 = _shift_down(cz, czp, 2)
        w0, w1, w2 = w_ref[0:1, :], w_ref[1:2, :], w_ref[2:3, :]
        y = w2 * cz + w1 * s1 + w0 * s2
        dm_t = dm_ref[...]
        dy = dm_t * b
        dyn = jnp.where(i < n_tiles - 1, dm_next[...] * b_next[...], 0.0)
        dcz = w2 * dy + w1 * _shift_up(dy, dyn, 1) + w0 * _shift_up(dy, dyn, 2)
        dp_ref[:, 0:C] = (dm_t * y).astype(dp_ref.dtype)
        dp_ref[:, C:2 * C] = (dcz * z).astype(dp_ref.dtype)
        dp_ref[:, 2 * C:3 * C] = (dcz * c).astype(dp_ref.dtype)
        dw = jnp.concatenate([jnp.sum(dy * s2, axis=0, keepdims=True), jnp.sum(dy * s1, axis=0, keepdims=True),
                              jnp.sum(dy * cz, axis=0, keepdims=True)], axis=0)

        @pl.when(i == 0)
        def _():
            dw_ref[...] = dw

        @pl.when(i != 0)
        def _():
            dw_ref[...] += dw

    prev = lambda col: (lambda i: (jnp.maximum(i * hb - 1, 0), col))
    nxt = lambda col: (lambda i: (jnp.minimum((i + 1) * hb, S // 8 - 1), col))
    return pl.pallas_call(
        body, name="conv_bwd", grid=(n_tiles,),
        in_specs=[pl.BlockSpec((tm, C3), lambda i: (i, 0)), pl.BlockSpec((8, C), prev(1)),
                  pl.BlockSpec((8, C), prev(2)), pl.BlockSpec((8, C), nxt(0)),
                  pl.BlockSpec((tm, C), lambda i: (i, 0)), pl.BlockSpec((8, C), nxt(0)),
                  pl.BlockSpec((3, C), lambda i: (0, 0))],
        out_specs=[pl.BlockSpec((tm, C3), lambda i: (i, 0)), pl.BlockSpec((3, C), lambda i: (0, 0))],
        out_shape=[jax.ShapeDtypeStruct((S, C3), BF16), jax.ShapeDtypeStruct((3, C), F32)],
        compiler_params=_params(("arbitrary",)),
    )(p, p, p, p, dm, dm, w)


def _my_place():
    return lax.axis_index("x"), lax.axis_index("y"), lax.axis_index("c")


def _gather_weights_call(name, shard):
    R, C = shard.shape

    def body(s_ref, o_ref, send_sems, recv_sems, local_sem):
        x, y, c = _my_place()
        mine = 2 * x + y
        local = pltpu.make_async_copy(s_ref, o_ref.at[mine], local_sem)
        local.start()
        peers = [(1 - x, y), (x, 1 - y), (1 - x, 1 - y)]
        copies = []
        for k, (px, py) in enumerate(peers):
            cp = pltpu.make_async_remote_copy(src_ref=s_ref, dst_ref=o_ref.at[mine], send_sem=send_sems.at[k],
                                              recv_sem=recv_sems.at[k], device_id=(px, py, c), device_id_type=MESH)
            cp.start()
            copies.append(cp)
        for k, (px, py) in enumerate(peers):
            pltpu.make_async_remote_copy(src_ref=s_ref, dst_ref=o_ref.at[2 * px + py], send_sem=send_sems.at[k],
                                         recv_sem=recv_sems.at[k], device_id=(px, py, c),
                                         device_id_type=MESH).wait_recv()
        for cp in copies:
            cp.wait_send()
        local.wait()

    any_spec = pl.BlockSpec(memory_space=pl.ANY)
    return pl.pallas_call(
        body, name=name, in_specs=[any_spec], out_specs=any_spec,
        out_shape=jax.ShapeDtypeStruct((4, R, C), shard.dtype),
        scratch_shapes=[pltpu.SemaphoreType.DMA((3,)), pltpu.SemaphoreType.DMA((3,)), pltpu.SemaphoreType.DMA],
        compiler_params=pltpu.CompilerParams(has_side_effects=True),
    )(shard)


_D2D_CHUNKS = 4


_LOCAL_CHUNKS = 8


def _local_copies(src_of, dst_of, rows, sems, base):
    rc = rows // _LOCAL_CHUNKS
    assert rc * _LOCAL_CHUNKS == rows and rc % 16 == 0, rows
    out = []
    for j in range(_LOCAL_CHUNKS):
        sl = pl.ds(j * rc, rc)
        out.append(pltpu.make_async_copy(src_of(sl), dst_of(sl), sems.at[base + j]))
    return out


def _comm_call(name, body, arrays, out_shapes, sem_counts, aliases=None):
    any_spec = pl.BlockSpec(memory_space=pl.ANY)
    return pl.pallas_call(
        body, name=name, in_specs=[any_spec] * len(arrays), out_specs=[any_spec] * len(out_shapes),
        out_shape=out_shapes, scratch_shapes=[pltpu.SemaphoreType.DMA((n,)) for n in sem_counts],
        input_output_aliases=aliases or {}, compiler_params=pltpu.CompilerParams(has_side_effects=True),
    )(*arrays)


def _gather_halves_call(shards):
    na = len(shards)

    def body(*refs):
        send, forward, finish = _gather_phases(refs[:na], refs[na:2 * na], *refs[2 * na:])
        send()
        forward()
        finish()

    start = _gather_start(shards)
    outs = [jax.ShapeDtypeStruct(t.shape, t.dtype) for t in start]
    return _comm_call("gather_weights", body, start, outs, _gather_sems(na), aliases={a: a for a in range(na)})


def _gather_start(shards):
    for s in shards:
        assert s.shape[0] % (2 * _D2D_CHUNKS * 16) == 0, s.shape
    return [jnp.broadcast_to(s[None], (4,) + tuple(s.shape)) for s in shards]


def _gather_sems(na):
    return [3 * na, 3 * na, 3 * na * _D2D_CHUNKS, 3 * na * _D2D_CHUNKS]


def _gather_phases(s_refs, o_refs, ici_send, ici_recv, d2d_send, d2d_recv):
    na = len(s_refs)
    x, y, c = _my_place()
    mine = 2 * x + y
    chips = [(1 - x, y), (x, 1 - y), (1 - x, 1 - y)]

    def ici(a, k, chip, block):
        Rh = s_refs[a].shape[1] // 2
        my_half = pl.ds(pl.multiple_of(c * Rh, 16), Rh)
        return pltpu.make_async_remote_copy(src_ref=s_refs[a].at[mine, my_half], dst_ref=o_refs[a].at[block, my_half],
                                            send_sem=ici_send.at[3 * a + k], recv_sem=ici_recv.at[3 * a + k],
                                            device_id=(chip[0], chip[1], c), device_id_type=MESH)

    def d2d(a, k, j, block, half):
        Rh = s_refs[a].shape[1] // 2
        rc = Rh // _D2D_CHUNKS
        rows = pl.ds(pl.multiple_of(half * Rh + j * rc, 16), rc)
        idx = (3 * a + k) * _D2D_CHUNKS + j
        return pltpu.make_async_remote_copy(src_ref=o_refs[a].at[block, rows], dst_ref=o_refs[a].at[block, rows],
                                            send_sem=d2d_send.at[idx], recv_sem=d2d_recv.at[idx],
                                            device_id=(x, y, 1 - c), device_id_type=MESH)

    def send():
        for a in range(na):
            for k, chip in enumerate(chips):
                ici(a, k, chip, mine).start()

    def forward():
        for a in range(na):
            for k, chip in enumerate(chips):
                block = 2 * chip[0] + chip[1]
                ici(a, k, chip, block).wait_recv()
                for j in range(_D2D_CHUNKS):
                    d2d(a, k, j, block, c).start()

    def finish():
        for a in range(na):
            for k, chip in enumerate(chips):
                block = 2 * chip[0] + chip[1]
                for j in range(_D2D_CHUNKS):
                    d2d(a, k, j, block, 1 - c).wait_recv()
        for a in range(na):
            for k, chip in enumerate(chips):
                ici(a, k, chip, mine).wait_send()
                for j in range(_D2D_CHUNKS):
                    d2d(a, k, j, 2 * chip[0] + chip[1], c).wait_send()

    return send, forward, finish


def _pair_exchange_call(packed, tag):
    na = len(packed)

    def body(*refs):
        p_refs, o_refs = refs[:na], refs[na:2 * na]
        send_sems, recv_sems = refs[2 * na:]
        x, y, c = _my_place()
        copies = []
        for a in range(na):
            nb, _, Rh, _ = p_refs[a].shape
            rc = Rh // _D2D_CHUNKS
            assert rc * _D2D_CHUNKS == Rh and rc % 16 == 0
            for b in range(nb):
                for j in range(_D2D_CHUNKS):
                    rows = pl.ds(j * rc, rc)
                    idx = (a * nb + b) * _D2D_CHUNKS + j
                    copies.append(pltpu.make_async_remote_copy(
                        src_ref=p_refs[a].at[b, 1 - c, rows], dst_ref=o_refs[a].at[b, rows],
                        send_sem=send_sems.at[idx], recv_sem=recv_sems.at[idx],
                        device_id=(x, y, 1 - c), device_id_type=MESH))
        for t in copies:
            t.start()
        for t in copies:
            t.wait_recv()
        for t in copies:
            t.wait_send()

    outs = [jax.ShapeDtypeStruct((p.shape[0], p.shape[2], p.shape[3]), p.dtype) for p in packed]
    n = sum(p.shape[0] for p in packed) * _D2D_CHUNKS
    return _comm_call("pair_exchange_" + tag, body, packed, outs, [n, n])


def _pair_add_call(name, packed, other, core):
    nb, _, Rh, C = packed.shape
    tr = _pick(Rh, 512, 16)

    def body(c_ref, p_ref, o_ref, q_ref):
        q_ref[...] = (p_ref[...].astype(F32) + o_ref[...].astype(F32)).astype(q_ref.dtype)

    grid_spec = pltpu.PrefetchScalarGridSpec(
        num_scalar_prefetch=1, grid=(nb, Rh // tr),
        in_specs=[pl.BlockSpec((None, None, tr, C), lambda b, r, c_ref: (b, c_ref[0], r, 0)),
                  pl.BlockSpec((None, tr, C), lambda b, r, c_ref: (b, r, 0))],
        out_specs=pl.BlockSpec((None, tr, C), lambda b, r, c_ref: (b, r, 0)))
    return pl.pallas_call(
        body, name=name, grid_spec=grid_spec, out_shape=jax.ShapeDtypeStruct((nb, Rh, C), packed.dtype),
        compiler_params=_params(("arbitrary", "arbitrary")),
    )(core, packed, other)


def _chip_scatter_call(pairs):
    na = len(pairs)

    def body(*refs):
        send, finish = _scatter_phases(refs[:na], refs[na:2 * na], *refs[2 * na:])
        send()
        finish()

    outs = [jax.ShapeDtypeStruct(p.shape, p.dtype) for p in pairs]
    return _comm_call("chip_scatter", body, pairs, outs, _scatter_sems(na))


def _scatter_sems(na):
    return [3 * na, 3 * na, na * _LOCAL_CHUNKS]


def _scatter_phases(p_refs, o_refs, send_sems, recv_sems, local_sems):
    na = len(p_refs)
    x, y, c = _my_place()
    mine = 2 * x + y
    chips = [(1 - x, y), (x, 1 - y), (1 - x, 1 - y)]

    def local(a):
        p_ref, o_ref = p_refs[a], o_refs[a]
        return _local_copies(lambda sl: p_ref.at[mine, sl], lambda sl: o_ref.at[mine, sl], p_ref.shape[1],
                             local_sems, a * _LOCAL_CHUNKS)

    def remote(a, k, src_block, dst_block):
        px, py = chips[k]
        return pltpu.make_async_remote_copy(src_ref=p_refs[a].at[src_block], dst_ref=o_refs[a].at[dst_block],
                                            send_sem=send_sems.at[3 * a + k], recv_sem=recv_sems.at[3 * a + k],
                                            device_id=(px, py, c), device_id_type=MESH)

    def send():
        for a in range(na):
            for t in local(a):
                t.start()
            for k, (px, py) in enumerate(chips):
                remote(a, k, 2 * px + py, mine).start()

    def finish():
        for a in range(na):
            for k, (px, py) in enumerate(chips):
                remote(a, k, mine, 2 * px + py).wait_recv()
        for a in range(na):
            for k, (px, py) in enumerate(chips):
                remote(a, k, 2 * px + py, mine).wait_send()
            for t in local(a):
                t.wait()

    return send, finish


def _sum_slots_call(name, parts, core):
    n, R, C = parts.shape
    tr = _pick(R, 256, 8)

    def body(c_ref, p_ref, o_ref):
        acc = p_ref[0].astype(F32)
        for s in range(1, n):
            acc = acc + p_ref[s].astype(F32)
        o_ref[...] = acc

    grid_spec = pltpu.PrefetchScalarGridSpec(
        num_scalar_prefetch=1, grid=(R // tr,),
        in_specs=[pl.BlockSpec((n, tr, C), lambda i, c_ref: (0, i, 0))],
        out_specs=pl.BlockSpec((None, tr, C), lambda i, c_ref: (c_ref[0], i, 0)))
    return pl.pallas_call(
        body, name=name, grid_spec=grid_spec, out_shape=jax.ShapeDtypeStruct((2, R, C), F32),
        compiler_params=_params(("arbitrary",)),
    )(core, parts)


def _sibling_share_call(halves):
    na = len(halves)
    nch = 2 * _D2D_CHUNKS

    def body(*refs):
        h_refs, o_refs = refs[:na], refs[na:2 * na]
        send_sems, recv_sems = refs[2 * na:]
        x, y, c = _my_place()

        def cp(a, j, slot):
            rc = h_refs[a].shape[1] // nch
            rows = pl.ds(j * rc, rc)
            return pltpu.make_async_remote_copy(src_ref=h_refs[a].at[slot, rows], dst_ref=o_refs[a].at[slot, rows],
                                                send_sem=send_sems.at[a * nch + j], recv_sem=recv_sems.at[a * nch + j],
                                                device_id=(x, y, 1 - c), device_id_type=MESH)

        copies = [cp(a, j, c) for a in range(na) for j in range(nch)]
        for t in copies:
            t.start()
        for a in range(na):
            for j in range(nch):
                cp(a, j, 1 - c).wait_recv()
        for t in copies:
            t.wait_send()

    for h in halves:
        assert h.shape[1] % (nch * 8) == 0, h.shape
    outs = [jax.ShapeDtypeStruct(h.shape, h.dtype) for h in halves]
    return _comm_call("sibling_share", body, halves, outs, [na * nch, na * nch], aliases={a: a for a in range(na)})


def _allreduce_small_call(part):
    R, C = part.shape

    def body(p_ref, o_ref, slots, send_sems, recv_sems):
        x, y, c = _my_place()
        me = 4 * x + 2 * y + c
        peers = []
        for k in range(1, 8):
            px = x ^ (k >> 2) if (k >> 2) else x
            py = y ^ ((k >> 1) & 1) if ((k >> 1) & 1) else y
            pc = c ^ (k & 1) if (k & 1) else c
            peers.append((px, py, pc))
        copies = []
        for k, (px, py, pc) in enumerate(peers):
            cp = pltpu.make_async_remote_copy(src_ref=p_ref, dst_ref=slots.at[me], send_sem=send_sems.at[k],
                                              recv_sem=recv_sems.at[k], device_id=(px, py, pc), device_id_type=MESH)
            cp.start()
            copies.append(cp)
        slots[me] = p_ref[...]
        for k, (px, py, pc) in enumerate(peers):
            pltpu.make_async_remote_copy(src_ref=p_ref, dst_ref=slots.at[4 * px + 2 * py + pc],
                                         send_sem=send_sems.at[k], recv_sem=recv_sems.at[k],
                                         device_id=(px, py, pc), device_id_type=MESH).wait_recv()
        for cp in copies:
            cp.wait_send()
        acc = slots[0]
        for s in range(1, 8):
            acc = acc + slots[s]
        o_ref[...] = acc

    vm = pl.BlockSpec(memory_space=pltpu.VMEM)
    return pl.pallas_call(
        body, name="allreduce_small", in_specs=[vm], out_specs=vm,
        out_shape=jax.ShapeDtypeStruct((R, C), F32),
        scratch_shapes=[pltpu.VMEM((8, R, C), F32), pltpu.SemaphoreType.DMA((7,)), pltpu.SemaphoreType.DMA((7,))],
        compiler_params=pltpu.CompilerParams(has_side_effects=True, vmem_limit_bytes=_VMEM_LIMIT),
    )(part)


def _adamw_call(name, w, g, m, v):
    shape = w.shape
    cols = shape[-1]
    rows = int(np.prod(shape[:-1])) if len(shape) > 1 else 1
    w2, g2, m2, v2 = (t.reshape(rows, cols) for t in (w, g, m, v))
    tr = _pick(rows, 256, 8)
    c1 = 1.0 / (1.0 - ADAM_B1 ** ADAM_STEP)
    c2 = 1.0 / (1.0 - ADAM_B2 ** ADAM_STEP)

    def body(w_ref, g_ref, m_ref, v_ref, d_ref, nm_ref, nv_ref):
        gr = g_ref[...]
        m_new = ADAM_B1 * m_ref[...] + (1.0 - ADAM_B1) * gr
        v_new = ADAM_B2 * v_ref[...] + (1.0 - ADAM_B2) * (gr * gr)
        m_hat = m_new / (1.0 - ADAM_B1 ** ADAM_STEP)
        v_hat = v_new / (1.0 - ADAM_B2 ** ADAM_STEP)
        d_ref[...] = -ADAM_LR * (m_hat / (jnp.sqrt(v_hat) + ADAM_EPS) + ADAM_WD * w_ref[...])
        nm_ref[...] = m_new
        nv_ref[...] = v_new

    spec = pl.BlockSpec((tr, cols), lambda i: (i, 0))
    d, nm, nv = pl.pallas_call(
        body, name=name, grid=(rows // tr,), in_specs=[spec] * 4, out_specs=[spec] * 3,
        out_shape=[jax.ShapeDtypeStruct((rows, cols), F32)] * 3, compiler_params=_params(("arbitrary",)),
    )(w2, g2, m2, v2)
    return d.reshape(shape), nm.reshape(shape), nv.reshape(shape)


def _pad_rows(flat, mult):
    n = flat.shape[0]
    unit = PACK_COLS * mult
    total = -(-n // unit) * unit
    return jnp.pad(flat, (0, total - n)).reshape(total // PACK_COLS, PACK_COLS)


def _pad_axis(arr, axis, mult):
    n = arr.shape[axis]
    total = -(-n // mult) * mult
    if total == n:
        return arr
    widths = [(0, 0)] * arr.ndim
    widths[axis] = (0, total - n)
    return jnp.pad(arr, widths)


def _shard_slice(arr, axis, blk, nblk=4):
    w = arr.shape[axis] // nblk
    return lax.slice_in_dim(arr, blk * w, (blk + 1) * w, axis=axis)


_NT = (((1,), (1,)), ((), ()))
_TN = (((0,), (0,)), ((), ()))


def _ffn_fwd(tag, x, h, ht, wg, wu, wd, g_next):
    S, D = x.shape
    (wg, gi), (wu, ui), (wd, di) = wg, wu, wd
    NB, Fb = wg.shape[0], wg.shape[2]
    tm = _pick(S, 1024, 8)

    def gate_up(h_ref, wg_ref, wu_ref, a_ref, u_ref, s_ref):
        h_t = _mx(h_ref[...])
        a = jnp.dot(h_t, _mx(wg_ref[...]), preferred_element_type=F32)
        u = jnp.dot(h_t, _mx(wu_ref[...]), preferred_element_type=F32)
        sig = _sigmoid(a)
        silu = a * sig
        a_ref[...] = (u * (sig * (1.0 + a * (1.0 - sig)))).astype(a_ref.dtype)
        u_ref[...] = silu.astype(u_ref.dtype)
        s_ref[...] = (silu * u).astype(s_ref.dtype)

    hid = pl.BlockSpec((None, tm, Fb), lambda b, i: (b, i, 0))
    a, u, s = pl.pallas_call(
        gate_up, name=tag + "_gate_up", grid=(NB, S // tm),
        in_specs=[pl.BlockSpec((tm, D), lambda b, i: (i, 0)), pl.BlockSpec((None, D, Fb), lambda b, i: (b, gi, 0)),
                  pl.BlockSpec((None, D, Fb), lambda b, i: (b, ui, 0))], out_specs=[hid] * 3,
        out_shape=[jax.ShapeDtypeStruct((NB, S, Fb), BF16)] * 3, compiler_params=_params(("arbitrary", "arbitrary")),
    )(h, wg, wu)

    tm2 = _pick(S, 512, 128)

    def down(s_ref, wd_ref, x_ref, g_ref, xo_ref, ho_ref, hto_ref):
        acc = jnp.dot(_mx(s_ref[0]), _mx(wd_ref[0]), preferred_element_type=F32)
        for b in range(1, NB):
            acc = acc + jnp.dot(_mx(s_ref[b]), _mx(wd_ref[b]), preferred_element_type=F32)
        x_new = x_ref[...] + 0.5 * acc
        xo_ref[...] = x_new
        h_new = _rms_stats(x_new)[0] * g_ref[...]
        ho_ref[...] = h_new.astype(ho_ref.dtype)
        hto_ref[...] = jnp.transpose(h_new).astype(hto_ref.dtype)

    row = pl.BlockSpec((tm2, D), lambda i: (i, 0))
    x_new, h_next, ht_next = pl.pallas_call(
        down, name=tag + "_down", grid=(S // tm2,),
        in_specs=[pl.BlockSpec((NB, tm2, Fb), lambda i: (0, i, 0)), pl.BlockSpec((NB, Fb, D), lambda i: (0, di, 0)),
                  row, pl.BlockSpec((1, D), lambda i: (0, 0))],
        out_specs=[row, row, pl.BlockSpec((D, tm2), lambda i: (0, i))],
        out_shape=[jax.ShapeDtypeStruct((S, D), F32), jax.ShapeDtypeStruct((S, D), BF16),
                   jax.ShapeDtypeStruct((D, S), BF16)],
        compiler_params=_params(("arbitrary",)),
    )(s, wd, x, g_next)
    return x_new, h_next, ht_next, (x, h, ht, a, u, s)


def _ffn_bwd(tag, saved, dx_out, dxb, wg, wu, wd, gain):
    x, h, ht, a, u, s = saved
    S, D = x.shape
    (wg, gi), (wu, ui), (wd, di) = wg, wu, wd
    NB, Fb = wg.shape[0], wg.shape[2]
    tm = _pick(S, 1024, 8)
    tk = _pick(S, 2048, 128)
    nk = S // tk

    def dgate_up(d_ref, wd_ref, a_ref, u_ref, da_ref, du_ref):
        ds = 0.5 * lax.dot_general(_mx(d_ref[...]), _mx(wd_ref[...]), _NT, preferred_element_type=F32)
        da_ref[...] = (ds * a_ref[...].astype(F32)).astype(da_ref.dtype)
        du_ref[...] = (ds * u_ref[...].astype(F32)).astype(du_ref.dtype)

    hid = pl.BlockSpec((None, tm, Fb), lambda b, i: (b, i, 0))
    da, du = pl.pallas_call(
        dgate_up, name=tag + "_dgate_up", grid=(NB, S // tm),
        in_specs=[pl.BlockSpec((tm, D), lambda b, i: (i, 0)), pl.BlockSpec((None, Fb, D), lambda b, i: (b, di, 0)),
                  hid, hid],
        out_specs=[hid, hid], out_shape=[jax.ShapeDtypeStruct((NB, S, Fb), BF16)] * 2,
        compiler_params=_params(("arbitrary", "arbitrary")),
    )(dxb, wd, a, u)

    def dw_down(s_ref, d_ref, o_ref, acc_ref):
        k = pl.program_id(1)
        p = lax.dot_general(_mx(s_ref[...]), _mx(d_ref[...]), _TN, preferred_element_type=F32)

        @pl.when(k == 0)
        def _():
            acc_ref[...] = p

        @pl.when(k != 0)
        def _():
            acc_ref[...] += p

        @pl.when(k == nk - 1)
        def _():
            o_ref[...] = (0.5 * acc_ref[...]).astype(o_ref.dtype)

    hk = pl.BlockSpec((None, tk, Fb), lambda b, k: (b, k, 0))
    dwd = pl.pallas_call(
        dw_down, name=tag + "_dw_down", grid=(NB, nk),
        in_specs=[hk, pl.BlockSpec((tk, D), lambda b, k: (k, 0))],
        out_specs=pl.BlockSpec((None, Fb, D), lambda b, k: (b, 0, 0)),
        out_shape=jax.ShapeDtypeStruct((NB, Fb, D), _WIRE_DTYPE), scratch_shapes=[pltpu.VMEM((Fb, D), F32)],
        compiler_params=_params(("arbitrary", "arbitrary")),
    )(s, dxb)

    def dw_gate_up(h_ref, da_ref, du_ref, og_ref, ou_ref, accg_ref, accu_ref):
        k = pl.program_id(1)
        h_t = _mx(h_ref[...])
        dims = _TN if ht is None else (((1,), (0,)), ((), ()))
        pg = lax.dot_general(h_t, _mx(da_ref[...]), dims, preferred_element_type=F32)
        pu = lax.dot_general(h_t, _mx(du_ref[...]), dims, preferred_element_type=F32)

        @pl.when(k == 0)
        def _():
            accg_ref[...] = pg
            accu_ref[...] = pu

        @pl.when(k != 0)
        def _():
            accg_ref[...] += pg
            accu_ref[...] += pu

        @pl.when(k == nk - 1)
        def _():
            og_ref[...] = accg_ref[...].astype(og_ref.dtype)
            ou_ref[...] = accu_ref[...].astype(ou_ref.dtype)

    wout = pl.BlockSpec((None, D, Fb), lambda b, k: (b, 0, 0))
    dwg, dwu = pl.pallas_call(
        dw_gate_up, name=tag + "_dw_gate_up", grid=(NB, nk),
        in_specs=[pl.BlockSpec((tk, D), lambda b, k: (k, 0)) if ht is None
                  else pl.BlockSpec((D, tk), lambda b, k: (0, k)), hk, hk], out_specs=[wout, wout],
        out_shape=[jax.ShapeDtypeStruct((NB, D, Fb), _WIRE_DTYPE)] * 2,
        scratch_shapes=[pltpu.VMEM((D, Fb), F32)] * 2, compiler_params=_params(("arbitrary", "arbitrary")),
    )(h if ht is None else ht, da, du)

    tm2 = _pick(S, 512, 8)

    def dx_body(da_ref, du_ref, wg_hbm, wu_hbm, x_ref, dxo_ref, g_ref, dx_ref, dxb_ref, dg_ref, wg_v, wu_v, sem):
        i = pl.program_id(0)

        @pl.when(i == 0)
        def _():
            cg = pltpu.make_async_copy(wg_hbm.at[:, pl.ds(gi * D, D), :], wg_v, sem.at[0])
            cu = pltpu.make_async_copy(wu_hbm.at[:, pl.ds(ui * D, D), :], wu_v, sem.at[1])
            cg.start()
            cu.start()
            cg.wait()
            cu.wait()

        dh = None
        for b in range(NB):
            t = lax.dot_general(_mx(da_ref[b]), wg_v[b], _NT, preferred_element_type=F32)
            t = t + lax.dot_general(_mx(du_ref[b]), wu_v[b], _NT, preferred_element_type=F32)
            dh = t if dh is None else dh + t
        dx_n, dg = _rms_bwd(x_ref[...], g_ref[...], dh)
        dx = dxo_ref[...] + dx_n
        dx_ref[...] = dx
        dxb_ref[...] = dx.astype(dxb_ref.dtype)
        dg = jnp.sum(dg, axis=0, keepdims=True)

        @pl.when(i == 0)
        def _():
            dg_ref[...] = dg

        @pl.when(i != 0)
        def _():
            dg_ref[...] += dg

    row = pl.BlockSpec((tm2, D), lambda i: (i, 0))
    hid2 = pl.BlockSpec((NB, tm2, Fb), lambda i: (0, i, 0))
    anyspec = pl.BlockSpec(memory_space=pl.ANY)
    fixed = pl.BlockSpec((1, D), lambda i: (0, 0))
    dx, dxb_new, dgain = pl.pallas_call(
        dx_body, name=tag + "_dx", grid=(S // tm2,),
        in_specs=[hid2, hid2, anyspec, anyspec, row, row, fixed], out_specs=[row, row, fixed],
        out_shape=[jax.ShapeDtypeStruct((S, D), F32), jax.ShapeDtypeStruct((S, D), BF16),
                   jax.ShapeDtypeStruct((1, D), F32)],
        scratch_shapes=[pltpu.VMEM((NB, D, Fb), wg.dtype), pltpu.VMEM((NB, D, Fb), wu.dtype),
                        pltpu.SemaphoreType.DMA((2,))],
        compiler_params=_params(("arbitrary",)),
    )(da, du, wg, wu, x, dx_out, gain)
    return dx, dxb_new, dgain, dwg, dwu, dwd


def _conv_mixer_fwd(tag, x, h, ht, w_in, w_taps, w_out, g_next):
    S, D = x.shape
    C3 = w_in.shape[1]
    tm = _pick(S, 512, 8)
    p, = _fused_matmul(tag + "_in", 'nn', [_op(h)], [_op(w_in)], [(0, 0, 0)], 1, _ident_epi(), [F32],
                       S, C3, D, tm, _pick(C3, 1024, 128), D)
    m = _conv_fwd_call(p, w_taps)
    x_new, h_next = _fused_matmul(tag + "_out", 'nn', [_op(m)], [_op(w_out)], [(0, 0, 0)], 1, _resid_norm_epi(1.0),
                                  [F32, BF16], S, D, D, tm, D, D, tile_extras=[x], row_extras=[g_next])
    return x_new, h_next, (x, h, ht, p, m)


def _conv_mixer_bwd(tag, saved, dx_out, dxb, w_in, w_taps, w_out, gain):
    x, h, ht, p, m = saved
    S, D = x.shape
    C3 = w_in.shape[1]
    tm = _pick(S, 512, 8)
    tk = _pick(S, 1024, 128)
    dm, = _fused_matmul(tag + "_dm", 'nt', [_op(dxb)], [_op(w_out)], [(0, 0, 0)], 1, _ident_epi(), [F32],
                        S, D, D, tm, D, D)
    dw_out, = _fused_matmul(tag + "_dw_out", 'tn', [_op(m)], [_op(dxb)], [(0, 0, 0)], 1, _ident_epi(), [F32],
                            D, D, S, D, D, tk)
    dp, dtaps = _conv_bwd_call(p, w_taps, dm)
    dw_in, = _fused_matmul(tag + "_dw_in", 'nn', [_op(ht)], [_op(dp)], [(0, 0, 0)], 1, _ident_epi(), [F32],
                           D, C3, S, D, _pick(C3, 1024, 128), tk)
    dx, dxb_new, dgain = _fused_matmul(tag + "_dx", 'nt', [_op(dp)], [_op(w_in)], [(0, 0, 0)], 1, _norm_bwd_epi,
                                       [F32, BF16], S, D, C3, tm, D, C3,
                                       tile_extras=[x, dx_out], row_extras=[gain], n_colsum=1)
    return dx, dxb_new, dgain, dw_in, dtaps, dw_out


def _attn_scale():
    return np.float32(QK_DIM ** -0.5)


def _even_mixer_fwd(tag, x, h, ht, wts, tables, g_next, carry=()):
    S, D = x.shape
    cos, sa, sb = tables
    tm = _pick(S, 512, 8)
    AW = HEADS * HP
    proj, = _fused_matmul(tag + "_in", 'nn', [_op(h)], [_op(wts['w_in'])], [(0, 0, 0)], 1, _ident_epi(), [F32],
                          S, PROJ_W, D, tm, _pick(PROJ_W, 896, 128), D)
    cqn, ckvn, kr, u, vn = _even_prep_call(proj, wts['q_norm'], wts['kv_norm'], wts['sg_norm'], cos, sa, sb)
    scale = _attn_scale()

    def q_epi(accs, tiles, rows, mrows):
        c_t, a_t, b_t = mrows
        heads = [_rope(accs[0][:, hh * HP:(hh + 1) * HP], c_t, a_t, b_t) * scale for hh in range(HEADS)]
        return [jnp.concatenate(heads, axis=1)]

    q, = _fused_matmul(tag + "_q", 'nn', [_op(cqn)], [_op(wts['w_q'])], [(0, 0, 0)], 1, q_epi, [BF16],
                       S, AW, Q_LORA, tm, AW, Q_LORA, mrow_extras=[cos, sa, sb])

    def kv_epi(accs, tiles, rows, mrows):
        lane = lax.broadcasted_iota(jnp.int32, accs[1].shape, 1)
        v_t = jnp.where((lane & (HP - 1)) == VDIM, 1.0, accs[1])
        return [accs[0] + jnp.concatenate([mrows[0].astype(F32)] * HEADS, axis=1), v_t]

    k, v = _fused_matmul(tag + "_kv", 'nn', [_op(ckvn)], [_op(wts['w_k']), _op(wts['w_v'])],
                         [(0, 0, 0), (0, 1, 1)], 2, kv_epi, [BF16, BF16], S, AW, KV_LORA, tm, AW, KV_LORA,
                         mrow_extras=[kr])
    o, lse, gathered = _flash_fwd_call(q, k, v, carry)
    mix = _sgu_fwd_call(vn, u, o, wts['sg_wst'], wts['sg_bexp'])
    x_new, h_next = _fused_matmul(tag + "_out", 'nn', [_op(mix)], [_op(wts['w_out'])], [(0, 0, 0)], 1,
                                  _resid_norm_epi(1.0), [F32, BF16], S, D, AW + SG_WIDTH, tm, D, AW + SG_WIDTH,
                                  tile_extras=[x], row_extras=[g_next])
    return x_new, h_next, (x, h, ht, proj, cqn, ckvn, u, vn, q, k, v, o, lse, mix), gathered


def _even_mixer_bwd(tag, saved, dx_out, dxb, wts, tables, gain, carry=()):
    x, h, ht, proj, cqn, ckvn, u, vn, q, k, v, o, lse, mix = saved
    S, D = x.shape
    cos, sa, sb = tables
    tm = _pick(S, 512, 8)
    tk = _pick(S, 1024, 128)
    AW = HEADS * HP
    MW = AW + SG_WIDTH
    dmix, = _fused_matmul(tag + "_dmix", 'nt', [_op(dxb)], [_op(wts['w_out'])], [(0, 0, 0)], 1, _ident_epi(), [BF16],
                          S, MW, D, tm, _pick(MW, 768, 128), D)
    dw_out, = _fused_matmul(tag + "_dw_out", 'tn', [_op(mix)], [_op(dxb)], [(0, 0, 0)], 1, _ident_epi(), [F32],
                            MW, D, S, _pick(MW, 768, 128), D, tk)
    du, dvn, dsg_w, dsg_b = _sgu_bwd_call(dmix, vn, u, wts['sg_wst'], wts['sg_wst_t'], wts['sg_bexp'])
    delta = _attn_delta_call(o, dmix)
    dq, dk, dv, arrived = _flash_bwd_call(q, k, v, dmix, lse, delta, carry)
    scale = _attn_scale()

    def dq_epi(accs, tiles, rows, mrows):
        return accs

    def dq_pre_call():
        tr = _pick(S, 256, 8)

        def body(d_ref, c_ref, a_ref, b_ref, o_ref):
            for hh in range(HEADS):
                t = _rope_t(d_ref[:, hh * HP:(hh + 1) * HP], c_ref[...], a_ref[...], b_ref[...]) * scale
                o_ref[:, hh * HP:(hh + 1) * HP] = t.astype(o_ref.dtype)

        row = lambda i: (i, 0)
        return pl.pallas_call(
            body, name=tag + "_dq_unrope", grid=(S // tr,),
            in_specs=[pl.BlockSpec((tr, AW), row)] + [pl.BlockSpec((tr, HP), row)] * 3,
            out_specs=pl.BlockSpec((tr, AW), row), out_shape=jax.ShapeDtypeStruct((S, AW), BF16),
            compiler_params=_params(("arbitrary",)),
        )(dq, cos, sa, sb)

    dqp = dq_pre_call()
    dw_q, = _fused_matmul(tag + "_dw_q", 'tn', [_op(cqn)], [_op(dqp)], [(0, 0, 0)], 1, _ident_epi(), [F32],
                          Q_LORA, AW, S, Q_LORA, AW, tk)
    dcqn, = _fused_matmul(tag + "_dcq", 'nt', [_op(dqp)], [_op(wts['w_q'])], [(0, 0, 0)], 1, dq_epi, [F32],
                          S, Q_LORA, AW, tm, Q_LORA, AW)
    dw_k, dw_v = _fused_matmul(tag + "_dw_kv", 'tn', [_op(ckvn)], [_op(dk), _op(dv)], [(0, 0, 0), (0, 1, 1)], 2,
                               _ident_epi(), [F32, F32], KV_LORA, AW, S, KV_LORA, AW, tk)
    dckvn, = _fused_matmul(tag + "_dckv", 'nt', [_op(dk), _op(dv)], [_op(wts['w_k']), _op(wts['w_v'])],
                           [(0, 0, 0), (1, 1, 0)], 1, dq_epi, [F32], S, KV_LORA, AW, tm, KV_LORA, AW)
    dproj, dqn, dkvn, dsgn = _even_prep_bwd_call(proj, wts['q_norm'], wts['kv_norm'], wts['sg_norm'], cos, sa, sb,
                                                 dcqn, dckvn, dk, du, dvn)
    dw_in, = _fused_matmul(tag + "_dw_in", 'nn', [_op(ht)], [_op(dproj)], [(0, 0, 0)], 1, _ident_epi(), [F32],
                           D, PROJ_W, S, D, _pick(PROJ_W, 896, 128), tk)
    dx, dxb_new, dgain = _fused_matmul(tag + "_dx", 'nt', [_op(dproj)], [_op(wts['w_in'])], [(0, 0, 0)], 1,
                                       _norm_bwd_epi, [F32, BF16], S, D, PROJ_W, tm, D, PROJ_W,
                                       tile_extras=[x, dx_out], row_extras=[gain], n_colsum=1)
    grads = dict(w_in=dw_in, w_q=dw_q, w_k=dw_k, w_v=dw_v, w_out=dw_out, q_norm=dqn, kv_norm=dkvn, sg_norm=dsgn,
                 sg_w=dsg_w, sg_b=dsg_b)
    return dx, dxb_new, dgain, grads, arrived


def _even_weights(w_in, w_uq, w_ukv, w_out, q_norm, kv_norm, sg_norm, sg_w, sg_b):
    D = w_in.shape[0]
    kr_cols = jnp.pad(w_in[:, Q_LORA + KV_LORA:Q_LORA + KV_LORA + ROPE], ((0, 0), (NOPE, HP - QK_DIM)))
    w_in_p = jnp.concatenate([w_in[:, :Q_LORA + KV_LORA], kr_cols, w_in[:, Q_LORA + KV_LORA + ROPE:]], axis=1)
    wq = w_uq.reshape(Q_LORA, HEADS, QK_DIM)
    w_q = jnp.pad(wq, ((0, 0), (0, 0), (0, HP - QK_DIM))).reshape(Q_LORA, HEADS * HP)
    wkv = w_ukv.reshape(KV_LORA, HEADS, NOPE + VDIM)
    w_k = jnp.pad(wkv[:, :, :NOPE], ((0, 0), (0, 0), (0, HP - NOPE))).reshape(KV_LORA, HEADS * HP)
    w_v = jnp.pad(wkv[:, :, NOPE:], ((0, 0), (0, 0), (0, HP - VDIM))).reshape(KV_LORA, HEADS * HP)
    wo_a = w_out[:HEADS * VDIM].reshape(HEADS, VDIM, D)
    wo_a = jnp.pad(wo_a, ((0, 0), (0, HP - VDIM), (0, 0))).reshape(HEADS * HP, D)
    w_out_p = jnp.concatenate([wo_a, w_out[HEADS * VDIM:]], axis=0)
    tri = jnp.tril(jnp.ones((SG_CHUNK, SG_CHUNK), F32))
    wm = sg_w * tri
    wst = wm.reshape(SG_GROUPS // 2, 2 * SG_CHUNK, SG_CHUNK).astype(_MXU_DTYPE)
    wst_t = jnp.swapaxes(wm, 1, 2).reshape(SG_GROUPS // 2, 2 * SG_CHUNK, SG_CHUNK).astype(_MXU_DTYPE)
    bexp = jnp.repeat(sg_b.T, SG_GDIM, axis=1)
    return dict(w_in=w_in_p, w_q=w_q, w_k=w_k, w_v=w_v, w_out=w_out_p, sg_wst=wst, sg_wst_t=wst_t, sg_bexp=bexp,
                q_norm=q_norm.reshape(1, -1), kv_norm=kv_norm.reshape(1, -1), sg_norm=sg_norm.reshape(1, -1))


def _even_grads_unpad(g):
    d_in = g['w_in']
    kr0 = Q_LORA + KV_LORA
    dw_in = jnp.concatenate([d_in[:, :kr0], d_in[:, kr0 + NOPE:kr0 + QK_DIM], d_in[:, kr0 + HP:]], axis=1)
    dw_uq = g['w_q'].reshape(Q_LORA, HEADS, HP)[:, :, :QK_DIM].reshape(Q_LORA, HEADS * QK_DIM)
    dk = g['w_k'].reshape(KV_LORA, HEADS, HP)[:, :, :NOPE]
    dv = g['w_v'].reshape(KV_LORA, HEADS, HP)[:, :, :VDIM]
    dw_ukv = jnp.concatenate([dk, dv], axis=2).reshape(KV_LORA, HEADS * (NOPE + VDIM))
    D = d_in.shape[0]
    wo = g['w_out']
    wo_a = wo[:HEADS * HP].reshape(HEADS, HP, D)[:, :VDIM].reshape(HEADS * VDIM, D)
    dw_out = jnp.concatenate([wo_a, wo[HEADS * HP:]], axis=0)
    dsg_b = g['sg_b'][:, :SG_GROUPS].T
    return dict(even_w_in=dw_in, w_uq=dw_uq, w_ukv=dw_ukv, even_w_out=dw_out, q_norm=g['q_norm'][0],
                kv_norm=g['kv_norm'][0], sg_norm=g['sg_norm'][0], sg_w=g['sg_w'], sg_b=dsg_b)


def kernel(x, positions, ffn_pre_norm, ffn_pre_w_gate, ffn_pre_w_up, ffn_pre_w_down, mix_norm, ffn_post_norm, ffn_post_w_gate, ffn_post_w_up, ffn_post_w_down, even_w_in, q_norm, w_uq, kv_norm, w_ukv, sg_norm, sg_w, sg_b, even_w_out, conv_w_in, conv_w, conv_w_out, final_norm, loss_target, m_ffn_pre_norm, m_ffn_pre_w_gate, m_ffn_pre_w_up, m_ffn_pre_w_down, m_mix_norm, m_ffn_post_norm, m_ffn_post_w_gate, m_ffn_post_w_up, m_ffn_post_w_down, m_even_w_in, m_q_norm, m_w_uq, m_kv_norm, m_w_ukv, m_sg_norm, m_sg_w, m_sg_b, m_even_w_out, m_conv_w_in, m_conv_w, m_conv_w_out, m_final_norm, v_ffn_pre_norm, v_ffn_pre_w_gate, v_ffn_pre_w_up, v_ffn_pre_w_down, v_mix_norm, v_ffn_post_norm, v_ffn_post_w_gate, v_ffn_post_w_up, v_ffn_post_w_down, v_even_w_in, v_q_norm, v_w_uq, v_kv_norm, v_w_ukv, v_sg_norm, v_sg_w, v_sg_b, v_even_w_out, v_conv_w_in, v_conv_w, v_conv_w_out, v_final_norm):
    env = dict(locals())
    w_loc = {n: env[n] for n in WEIGHTS}
    m_loc = {n: env['m_' + n] for n in WEIGHTS}
    v_loc = {n: env['v_' + n] for n in WEIGHTS}
    S, D = x.shape[1], x.shape[2]
    depth = ffn_pre_norm.shape[0]
    xs = x.reshape(S, D)
    target = loss_target.reshape(S, D)

    def layers_of(n, early):
        count = w_loc[n].shape[0]
        if n in ('conv_w_in', 'conv_w', 'conv_w_out'):
            return [] if early else list(range(count))
        return [0] if early else list(range(1, count))

    def shards_of(early):
        wire = lambda n, l: w_loc[n][l].astype(_WIRE_DTYPE)
        keys = [[(n, l) for n in group for l in layers_of(n, early)] for group in (GROUP_A, GROUP_B, GATHER_C)]
        sa = _pad_axis(jnp.concatenate([wire(n, l) for n, l in keys[0]], axis=0), 0, PACK_ROW_MULT)
        sb = _pad_axis(jnp.concatenate([wire(n, l) for n, l in keys[1]], axis=0), 0, PACK_ROW_MULT)
        sc = _pad_rows(jnp.concatenate([wire(n, l).reshape(-1) for n, l in keys[2]]), PACK_ROW_MULT)
        return [sa, sb, sc], keys

    full = {n: {} for n in SHARDED}

    def unpack(gathered, keys):
        gat_a, gat_b, gat_c = gathered
        for idx, (n, l) in enumerate(keys[0]):
            full[n][l] = (gat_a, idx)
        row = 0
        for n, l in keys[1]:
            rows = w_loc[n].shape[1]
            if n in FFN_WEIGHTS:
                full[n][l] = (gat_b, row // rows)
            else:
                full[n][l] = jnp.concatenate([gat_b[b, row:row + rows] for b in range(4)], axis=0)
            row += rows
        gflat = gat_c.reshape(4, -1)
        off = 0
        for n, l in keys[2]:
            shp = w_loc[n].shape[1:]
            size = int(np.prod(shp))
            full[n][l] = jnp.concatenate([gflat[b, off:off + size].reshape(shp) for b in range(4)],
                                         axis=SHARD_AXIS[n] - 1)
            off += size

    early_shards, early_keys = shards_of(True)
    unpack(_gather_halves_call(early_shards), early_keys)
    late_shards, late_keys = shards_of(False)
    taps = _gather_weights_call("gather_taps", _pad_rows(conv_w.reshape(-1), 8)).reshape(4, -1)
    taps = jnp.concatenate([taps[b, :conv_w.size].reshape(conv_w.shape) for b in range(4)], axis=2)

    inv_freq = ROPE_THETA ** (-jnp.arange(0, ROPE, 2, dtype=F32) / ROPE)
    half = ROPE // 2
    zeros = lambda n: jnp.zeros((n,), F32)
    ones = jnp.ones((half,), F32)
    invf = jnp.concatenate([zeros(NOPE), inv_freq, inv_freq, zeros(HP - QK_DIM)]).reshape(1, HP)
    mask_a = jnp.concatenate([zeros(NOPE), -ones, zeros(HP - NOPE - half)]).reshape(1, HP)
    mask_b = jnp.concatenate([zeros(NOPE + half), ones, zeros(HP - QK_DIM)]).reshape(1, HP)
    tables = _rope_tables_call(positions.reshape(S, 1), invf, mask_a, mask_b)

    even_w = {}

    def even_weights_of(e):
        if e not in even_w:
            even_w[e] = _even_weights(full['even_w_in'][e], full['w_uq'][e], full['w_ukv'][e], full['even_w_out'][e],
                                      q_norm[e], kv_norm[e], sg_norm[e], sg_w[e], sg_b[e])
        return even_w[e]

    def gain_row(arr, l):
        return arr[l].reshape(1, D)

    saved = []
    h, ht = _rmsnorm_call("first_norm", xs, gain_row(ffn_pre_norm, 0))
    xc = xs
    for l in range(depth):
        xc, h, ht, s_pre = _ffn_fwd(f"l{l}_pre", xc, h, ht, full['ffn_pre_w_gate'][l], full['ffn_pre_w_up'][l],
                                    full['ffn_pre_w_down'][l], gain_row(mix_norm, l))
        if l % 2 == 0:
            xc, h, s_mix, gathered = _even_mixer_fwd(f"l{l}_mix", xc, h, ht, even_weights_of(l // 2), tables,
                                                     gain_row(ffn_post_norm, l), late_shards if l == 0 else ())
            if l == 0:
                unpack(gathered, late_keys)
        else:
            o = l // 2
            xc, h, s_mix = _conv_mixer_fwd(f"l{l}_mix", xc, h, ht, full['conv_w_in'][o], taps[o],
                                           full['conv_w_out'][o], gain_row(ffn_post_norm, l))
        g_next = gain_row(ffn_pre_norm, l + 1) if l + 1 < depth else final_norm.reshape(1, D)
        xc, h, ht, s_post = _ffn_fwd(f"l{l}_post", xc, h, None, full['ffn_post_w_gate'][l], full['ffn_post_w_up'][l],
                                     full['ffn_post_w_down'][l], g_next)
        saved.append((s_pre, s_mix, s_post))

    dx, dxb, d_final, loss_part = _loss_call(xc, target, final_norm.reshape(1, D))
    loss = lax.psum(loss_part[0, 0], ("x", "y", "c"))

    gl = {n: [None] * w_loc[n].shape[0] for n in WEIGHTS if n != 'final_norm'}
    core = lax.axis_index("c").astype(jnp.int32).reshape(1)

    def pair_sums(first, tag):
        keys = [[(n, l) for n in group for l in layers_of(n, first)] for group in (GROUP_A, GROUP_B, GROUP_C)]

        def rows_blocked(n, l):
            g = gl[n][l]
            return g.reshape(4, g.shape[0] // 4, g.shape[1]).astype(_WIRE_DTYPE)

        pack_a = jnp.concatenate([gl[n][l] for n, l in keys[0]], axis=1)
        pack_b = jnp.concatenate([gl[n][l] if n in FFN_WEIGHTS else rows_blocked(n, l) for n, l in keys[1]], axis=1)
        pack_c = jnp.stack([_pad_rows(jnp.concatenate(
            [_shard_slice(gl[n][l], SHARD_AXIS[n] - 1, b).astype(_WIRE_DTYPE).reshape(-1) for n, l in keys[2]]),
            PACK_ROW_MULT) for b in range(4)])
        packs = [_pad_axis(p, 1, PACK_ROW_MULT) for p in (pack_a, pack_b, pack_c)]
        packs = [p.reshape(4, 2, p.shape[1] // 2, p.shape[2]) for p in packs]
        theirs = _pair_exchange_call(packs, tag)
        return [_pair_add_call(f"pair_add_{tag}_{i}", p, t, core)
                for i, (p, t) in enumerate(zip(packs, theirs))], keys
    for l in reversed(range(depth)):
        s_pre, s_mix, s_post = saved[l]
        dx, dxb, dgain, dwg, dwu, dwd = _ffn_bwd(f"l{l}_post", s_post, dx, dxb, full['ffn_post_w_gate'][l],
                                                 full['ffn_post_w_up'][l], full['ffn_post_w_down'][l],
                                                 gain_row(ffn_post_norm, l))
        gl['ffn_post_norm'][l] = dgain[0]
        gl['ffn_post_w_gate'][l], gl['ffn_post_w_up'][l], gl['ffn_post_w_down'][l] = dwg, dwu, dwd
        if l % 2 == 0:
            e = l // 2
            if l == 0:
                pairs_rest, keys_rest = pair_sums(False, "rest")
            dx, dxb, dgain, eg, arrived = _even_mixer_bwd(f"l{l}_mix", s_mix, dx, dxb, even_weights_of(e), tables,
                                                          gain_row(mix_norm, l), pairs_rest if l == 0 else ())
            if l == 0:
                arrived_rest = arrived
            for n, val in _even_grads_unpad(eg).items():
                gl[n][e] = val
        else:
            o = l // 2
            dx, dxb, dgain, dw_in, dtaps, dw_out = _conv_mixer_bwd(f"l{l}_mix", s_mix, dx, dxb, full['conv_w_in'][o],
                                                                   taps[o], full['conv_w_out'][o],
                                                                   gain_row(mix_norm, l))
            gl['conv_w_in'][o], gl['conv_w'][o], gl['conv_w_out'][o] = dw_in, dtaps, dw_out
        gl['mix_norm'][l] = dgain[0]
        dx, dxb, dgain, dwg, dwu, dwd = _ffn_bwd(f"l{l}_pre", s_pre, dx, dxb, full['ffn_pre_w_gate'][l],
                                                 full['ffn_pre_w_up'][l], full['ffn_pre_w_down'][l],
                                                 gain_row(ffn_pre_norm, l))
        gl['ffn_pre_norm'][l] = dgain[0]
        gl['ffn_pre_w_gate'][l], gl['ffn_pre_w_up'][l], gl['ffn_pre_w_down'][l] = dwg, dwu, dwd
    grad_x = dx.reshape(x.shape)
    part = {n: jnp.stack(gl[n]) for n in gl if n not in FFN_WEIGHTS}
    part['final_norm'] = d_final[0]

    pairs, keys_first = pair_sums(True, "first")
    arrived_first = _chip_scatter_call(pairs)
    mine = [_sum_slots_call(f"sum_grad_slots_{i}", r, core) for i, r in enumerate(list(arrived_rest) + list(arrived_first))]
    reduced = [t.reshape(-1, t.shape[2]) for t in _sibling_share_call(mine)]
    per_layer = {n: {} for n in SHARDED}
    for (red_a, red_b, red_c), keys in ((reduced[:3], keys_rest), (reduced[3:], keys_first)):
        for idx, (n, l) in enumerate(keys[0]):
            per_layer[n][l] = red_a[idx * D:(idx + 1) * D]
        row = 0
        for n, l in keys[1]:
            rows = w_loc[n].shape[1]
            per_layer[n][l] = red_b[row:row + rows]
            row += rows
        red_c = red_c.reshape(-1)
        off = 0
        for n, l in keys[2]:
            shp = w_loc[n].shape[1:]
            size = int(np.prod(shp))
            per_layer[n][l] = red_c[off:off + size].reshape(shp)
            off += size
    grads = {n: jnp.stack([per_layer[n][l] for l in range(w_loc[n].shape[0])]) for n in SHARDED}

    small = _pad_rows(jnp.concatenate([part[n].reshape(-1) for n in REPLICATED]), 8)
    small_sum = _allreduce_small_call(small).reshape(-1)
    off = 0
    for n in REPLICATED:
        size = int(np.prod(w_loc[n].shape))
        grads[n] = small_sum[off:off + size].reshape(w_loc[n].shape)
        off += size

    deltas, new_m, new_v = {}, {}, {}
    for n in WEIGHTS:
        deltas[n], new_m[n], new_v[n] = _adamw_call("adamw_" + n, w_loc[n], grads[n], m_loc[n], v_loc[n])
    return (loss, grad_x, *[grads[n] for n in WEIGHTS], *[deltas[n] for n in WEIGHTS],
            *[new_m[n] for n in WEIGHTS], *[new_v[n] for n in WEIGHTS])
```

```python
import functools

import numpy as np
import jax
import jax.numpy as jnp
from jax import lax
from jax.experimental import pallas as pl
from jax.experimental.pallas import tpu as pltpu

F32 = jnp.float32
BF16 = jnp.bfloat16
_MXU_DTYPE = jnp.bfloat16
_WIRE_DTYPE = jnp.bfloat16
_VMEM_LIMIT = 52 * 1024 * 1024
_LANES = 128
_ATT_BLOCK = 512
_ROW_TILE = 512
_SG_TILE = 1024

NORM_EPS = 1e-6
HEADS = 8
NOPE = 64
ROPE = 32
VDIM = 64
QK_DIM = NOPE + ROPE
HP = 128
Q_LORA = 384
KV_LORA = 256
SG_WIDTH = 512
SG_GROUPS = 8
SG_GDIM = 64
SG_CHUNK = 128
ROPE_THETA = 10000.0
PROJ_W = Q_LORA + KV_LORA + HP + 2 * SG_WIDTH
ADAM_LR = 0.001
ADAM_B1 = 0.9
ADAM_B2 = 0.999
ADAM_EPS = 1e-08
ADAM_WD = 0.01
ADAM_STEP = 10
MESH = pl.DeviceIdType.MESH
PACK_COLS = 1024
PACK_ROW_MULT = 256

SHARDED = ['ffn_pre_w_gate', 'ffn_pre_w_up', 'ffn_pre_w_down', 'ffn_post_w_gate', 'ffn_post_w_up',
           'ffn_post_w_down', 'even_w_in', 'w_uq', 'w_ukv', 'even_w_out', 'conv_w_in', 'conv_w', 'conv_w_out']
SHARD_AXIS = {'ffn_pre_w_gate': 2, 'ffn_pre_w_up': 2, 'ffn_pre_w_down': 1, 'ffn_post_w_gate': 2,
              'ffn_post_w_up': 2, 'ffn_post_w_down': 1, 'even_w_in': 2, 'w_uq': 2, 'w_ukv': 2,
              'even_w_out': 1, 'conv_w_in': 2, 'conv_w': 2, 'conv_w_out': 1}
FFN_WEIGHTS = ['ffn_pre_w_gate', 'ffn_pre_w_up', 'ffn_pre_w_down', 'ffn_post_w_gate', 'ffn_post_w_up',
               'ffn_post_w_down']
GROUP_A = ['ffn_pre_w_gate', 'ffn_pre_w_up', 'ffn_post_w_gate', 'ffn_post_w_up']
GROUP_B = ['ffn_pre_w_down', 'ffn_post_w_down', 'even_w_out', 'conv_w_out']
GROUP_C = ['even_w_in', 'w_uq', 'w_ukv', 'conv_w_in', 'conv_w']
GATHER_C = ['even_w_in', 'w_uq', 'w_ukv', 'conv_w_in']
REPLICATED = ['ffn_pre_norm', 'mix_norm', 'ffn_post_norm', 'q_norm', 'kv_norm', 'sg_norm', 'sg_w', 'sg_b',
              'final_norm']
WEIGHTS = ['ffn_pre_norm', 'ffn_pre_w_gate', 'ffn_pre_w_up', 'ffn_pre_w_down', 'mix_norm', 'ffn_post_norm',
           'ffn_post_w_gate', 'ffn_post_w_up', 'ffn_post_w_down', 'even_w_in', 'q_norm', 'w_uq', 'kv_norm',
           'w_ukv', 'sg_norm', 'sg_w', 'sg_b', 'even_w_out', 'conv_w_in', 'conv_w', 'conv_w_out', 'final_norm']


def _params(sem=None):
    return pltpu.CompilerParams(vmem_limit_bytes=_VMEM_LIMIT,
                                **({} if sem is None else {'dimension_semantics': sem}))


def _pick(n, pref, mult):
    best = None
    t = mult
    while t <= min(n, pref):
        if n % t == 0:
            best = t
        t += mult
    return n if best is None else best


def _mx(v):
    return v if v.dtype == _MXU_DTYPE else v.astype(_MXU_DTYPE)


def _sigmoid(a):
    return 1.0 / (1.0 + jnp.exp(-a))


def _rms_stats(x):
    rstd = lax.rsqrt(jnp.mean(x * x, axis=-1, keepdims=True) + NORM_EPS)
    return x * rstd, rstd


def _rms_bwd(x, g, dh):
    xhat, rstd = _rms_stats(x)
    gdh = g * dh
    dx = rstd * (gdh - xhat * jnp.mean(gdh * xhat, axis=-1, keepdims=True))
    return dx, dh * xhat


def _fused_matmul(name, mode, lhs, rhs, prods, n_acc, epilogue, out_dtypes, M, N, K, tm, tn, tk,
                  tile_extras=(), row_extras=(), mrow_extras=(), n_colsum=0):
    gj, gi, gk = N // tn, M // tm, K // tk
    assert gj * tn == N and gi * tm == M and gk * tk == K, (name, M, N, K, tm, tn, tk)
    dims = {'nn': (((1,), (0,)), ((), ())), 'nt': (((1,), (1,)), ((), ())), 'tn': (((0,), (0,)), ((), ()))}[mode]

    def lhs_spec(roff, coff, kb):
        kb = tk if kb is None else kb
        if mode == 'tn':
            return pl.BlockSpec((kb, tm), lambda j, i, k: (k + roff, i + coff))
        return pl.BlockSpec((tm, kb), lambda j, i, k: (i + roff, k + coff))

    def rhs_spec(roff, coff, kb):
        kb = tk if kb is None else kb
        if mode == 'nt':
            return pl.BlockSpec((tn, kb), lambda j, i, k: (j + roff, k + coff))
        return pl.BlockSpec((kb, tn), lambda j, i, k: (k + roff, j + coff))

    in_specs = [lhs_spec(*a[1:]) for a in lhs] + [rhs_spec(*a[1:]) for a in rhs]
    in_specs += [pl.BlockSpec((tm, tn), lambda j, i, k: (i, j)) for _ in tile_extras]
    in_specs += [pl.BlockSpec((1, tn), lambda j, i, k: (0, j)) for _ in row_extras]
    in_specs += [pl.BlockSpec((tm, a.shape[1]), lambda j, i, k: (i, 0)) for a in mrow_extras]
    n_out = len(out_dtypes)
    out_shape = [jax.ShapeDtypeStruct((M, N), d) for d in out_dtypes]
    out_specs = [pl.BlockSpec((tm, tn), lambda j, i, k: (i, j)) for _ in out_dtypes]
    out_shape += [jax.ShapeDtypeStruct((1, N), F32) for _ in range(n_colsum)]
    out_specs += [pl.BlockSpec((1, tn), lambda j, i, k: (0, j)) for _ in range(n_colsum)]
    scratch = [pltpu.VMEM((tm, tn), F32) for _ in range(n_acc)] if gk > 1 else []
    nl, nr, nt, nrw, nm = len(lhs), len(rhs), len(tile_extras), len(row_extras), len(mrow_extras)

    def body(*refs):
        pos = 0
        lhs_refs = refs[pos:pos + nl]; pos += nl
        rhs_refs = refs[pos:pos + nr]; pos += nr
        tile_refs = refs[pos:pos + nt]; pos += nt
        row_refs = refs[pos:pos + nrw]; pos += nrw
        mrow_refs = refs[pos:pos + nm]; pos += nm
        out_refs = refs[pos:pos + n_out]; pos += n_out
        cs_refs = refs[pos:pos + n_colsum]; pos += n_colsum
        acc_refs = refs[pos:]
        i = pl.program_id(1)
        k = pl.program_id(2)

        def partials():
            res = [None] * n_acc
            for (li, ri, ai) in prods:
                d = lax.dot_general(_mx(lhs_refs[li][...]), _mx(rhs_refs[ri][...]), dims,
                                    preferred_element_type=F32)
                res[ai] = d if res[ai] is None else res[ai] + d
            return res

        def finish(accs):
            outs = epilogue(accs, [r[...] for r in tile_refs], [r[...] for r in row_refs],
                            [r[...] for r in mrow_refs])
            for r, o in zip(out_refs, outs[:n_out]):
                r[...] = o.astype(r.dtype)
            for r, c in zip(cs_refs, outs[n_out:]):
                c = jnp.sum(c, axis=0, keepdims=True)

                @pl.when(i == 0)
                def _():
                    r[...] = c

                @pl.when(i != 0)
                def _():
                    r[...] += c

        if gk == 1:
            finish(partials())
        else:
            p = partials()

            @pl.when(k == 0)
            def _():
                for r, v in zip(acc_refs, p):
                    r[...] = v

            @pl.when(k != 0)
            def _():
                for r, v in zip(acc_refs, p):
                    r[...] += v

            @pl.when(k == gk - 1)
            def _():
                finish([r[...] for r in acc_refs])

    res = pl.pallas_call(
        body, name=name, grid=(gj, gi, gk), in_specs=in_specs, out_specs=out_specs, out_shape=out_shape,
        scratch_shapes=scratch, compiler_params=_params(("arbitrary", "arbitrary", "arbitrary")),
    )(*[a[0] for a in lhs], *[a[0] for a in rhs], *tile_extras, *row_extras, *mrow_extras)
    return res


def _op(a, roff=0, coff=0, kb=None):
    return (a, roff, coff, kb)


def _ident_epi(scale=None):
    def epi(accs, tiles, rows, mrows):
        return [a if scale is None else a * scale for a in accs]
    return epi


def _resid_norm_epi(scale):
    def epi(accs, tiles, rows, mrows):
        x_new = tiles[0] + scale * accs[0]
        xhat, _ = _rms_stats(x_new)
        return [x_new, xhat * rows[0]]
    return epi


def _norm_bwd_epi(accs, tiles, rows, mrows):
    dx_n, dg = _rms_bwd(tiles[0], rows[0], accs[0])
    dx = tiles[1] + dx_n
    return [dx, dx, dg]


def _rmsnorm_call(name, x, g):
    S, D = x.shape
    tm = _pick(S, _ROW_TILE, 128)

    def body(x_ref, g_ref, h_ref, ht_ref):
        h = _rms_stats(x_ref[...])[0] * g_ref[...]
        h_ref[...] = h.astype(h_ref.dtype)
        ht_ref[...] = jnp.transpose(h).astype(ht_ref.dtype)

    return pl.pallas_call(
        body, name=name, grid=(S // tm,),
        in_specs=[pl.BlockSpec((tm, D), lambda i: (i, 0)), pl.BlockSpec((1, D), lambda i: (0, 0))],
        out_specs=[pl.BlockSpec((tm, D), lambda i: (i, 0)), pl.BlockSpec((D, tm), lambda i: (0, i))],
        out_shape=[jax.ShapeDtypeStruct((S, D), BF16), jax.ShapeDtypeStruct((D, S), BF16)],
        compiler_params=_params(("arbitrary",)),
    )(x, g)


def _loss_call(x, target, g):
    S, D = x.shape
    tm = _pick(S, _ROW_TILE, 8)

    def body(x_ref, t_ref, g_ref, dx_ref, dxb_ref, dg_ref, loss_ref):
        i = pl.program_id(0)
        x_t = x_ref[...]
        gain = g_ref[...]
        xhat, _ = _rms_stats(x_t)
        diff = xhat * gain - t_ref[...]
        dy = diff * (1.0 / D)
        dx, dg = _rms_bwd(x_t, gain, dy)
        dx_ref[...] = dx
        dxb_ref[...] = dx.astype(BF16)
        dg = jnp.sum(dg, axis=0, keepdims=True)
        part = 0.5 * jnp.sum(jnp.sum(diff * diff, axis=1, keepdims=True), axis=0, keepdims=True) * (1.0 / D)
        part = jnp.broadcast_to(part, (1, _LANES))

        @pl.when(i == 0)
        def _():
            dg_ref[...] = dg
            loss_ref[...] = part

        @pl.when(i != 0)
        def _():
            dg_ref[...] += dg
            loss_ref[...] += part

    row = lambda i: (i, 0)
    fixed = lambda i: (0, 0)
    return pl.pallas_call(
        body, name="loss_head", grid=(S // tm,),
        in_specs=[pl.BlockSpec((tm, D), row), pl.BlockSpec((tm, D), row), pl.BlockSpec((1, D), fixed)],
        out_specs=[pl.BlockSpec((tm, D), row), pl.BlockSpec((tm, D), row), pl.BlockSpec((1, D), fixed),
                   pl.BlockSpec((1, _LANES), fixed)],
        out_shape=[jax.ShapeDtypeStruct((S, D), F32), jax.ShapeDtypeStruct((S, D), BF16),
                   jax.ShapeDtypeStruct((1, D), F32), jax.ShapeDtypeStruct((1, _LANES), F32)],
        compiler_params=_params(("arbitrary",)),
    )(x, target, g)


def _rope_tables_call(pos_col, invf, mask_a, mask_b):
    S = pos_col.shape[0]
    tm = _pick(S, _ROW_TILE, 8)

    def body(p_ref, f_ref, a_ref, b_ref, cos_ref, sa_ref, sb_ref):
        ang = p_ref[...].astype(F32) * f_ref[...]
        sn = jnp.sin(ang)
        cos_ref[...] = jnp.cos(ang)
        sa_ref[...] = sn * a_ref[...]
        sb_ref[...] = sn * b_ref[...]

    row = lambda i: (i, 0)
    fixed = lambda i: (0, 0)
    return pl.pallas_call(
        body, name="rope_tables", grid=(S // tm,),
        in_specs=[pl.BlockSpec((tm, 1), row)] + [pl.BlockSpec((1, HP), fixed)] * 3,
        out_specs=[pl.BlockSpec((tm, HP), row)] * 3,
        out_shape=[jax.ShapeDtypeStruct((S, HP), F32)] * 3, compiler_params=_params(("arbitrary",)),
    )(pos_col, invf, mask_a, mask_b)


def _rope(t, cos, sa, sb):
    return t * cos + pltpu.roll(t, HP - ROPE // 2, 1) * sa + pltpu.roll(t, ROPE // 2, 1) * sb


def _rope_t(d, cos, sa, sb):
    return d * cos + pltpu.roll(d * sa, ROPE // 2, 1) + pltpu.roll(d * sb, HP - ROPE // 2, 1)


def _gelu(z):
    return 0.5 * z * (1.0 + lax.erf(z * np.float32(1.0 / np.sqrt(2.0))))


def _gelu_grad(z):
    cdf = 0.5 * (1.0 + lax.erf(z * np.float32(1.0 / np.sqrt(2.0))))
    pdf = np.float32(1.0 / np.sqrt(2.0 * np.pi)) * jnp.exp(-0.5 * z * z)
    return cdf + z * pdf


_CQ0, _CKV0, _KR0, _Z0 = 0, Q_LORA, Q_LORA + KV_LORA, Q_LORA + KV_LORA + HP


def _even_prep_call(proj, qn, kvn, sgn, cos, sa, sb):
    S = proj.shape[0]
    tm = _pick(S, 256, 8)

    def body(p_ref, qn_ref, kvn_ref, sgn_ref, cos_ref, sa_ref, sb_ref, cq_ref, ckv_ref, kr_ref, u_ref, v_ref):
        cq = p_ref[:, _CQ0:_CQ0 + Q_LORA]
        cq_ref[...] = (_rms_stats(cq)[0] * qn_ref[...]).astype(BF16)
        ckv = p_ref[:, _CKV0:_CKV0 + KV_LORA]
        ckv_ref[...] = (_rms_stats(ckv)[0] * kvn_ref[...]).astype(BF16)
        kr = p_ref[:, _KR0:_KR0 + HP]
        kr_ref[...] = _rope(kr, cos_ref[...], sa_ref[...], sb_ref[...]).astype(BF16)
        u_ref[...] = _gelu(p_ref[:, _Z0:_Z0 + SG_WIDTH]).astype(BF16)
        zv = _gelu(p_ref[:, _Z0 + SG_WIDTH:_Z0 + 2 * SG_WIDTH])
        v_ref[...] = (_rms_stats(zv)[0] * sgn_ref[...]).astype(BF16)

    row = lambda i: (i, 0)
    fixed = lambda i: (0, 0)
    widths = [Q_LORA, KV_LORA, HP, SG_WIDTH, SG_WIDTH]
    return pl.pallas_call(
        body, name="even_prep", grid=(S // tm,),
        in_specs=[pl.BlockSpec((tm, PROJ_W), row), pl.BlockSpec((1, Q_LORA), fixed),
                  pl.BlockSpec((1, KV_LORA), fixed), pl.BlockSpec((1, SG_WIDTH), fixed)]
        + [pl.BlockSpec((tm, HP), row)] * 3,
        out_specs=[pl.BlockSpec((tm, w), row) for w in widths],
        out_shape=[jax.ShapeDtypeStruct((S, w), BF16) for w in widths],
        compiler_params=_params(("arbitrary",)),
    )(proj, qn, kvn, sgn, cos, sa, sb)


def _even_prep_bwd_call(proj, qn, kvn, sgn, cos, sa, sb, dcqn, dckvn, dk, du, dvn):
    S = proj.shape[0]
    tm = _pick(S, 256, 8)

    def body(p_ref, qn_ref, kvn_ref, sgn_ref, cos_ref, sa_ref, sb_ref, dcq_ref, dckv_ref, dk_ref, du_ref,
             dvn_ref, dp_ref, dqn_ref, dkvn_ref, dsgn_ref):
        i = pl.program_id(0)
        dcq, gq = _rms_bwd(p_ref[:, _CQ0:_CQ0 + Q_LORA], qn_ref[...], dcq_ref[...])
        dp_ref[:, _CQ0:_CQ0 + Q_LORA] = dcq.astype(BF16)
        dckv, gkv = _rms_bwd(p_ref[:, _CKV0:_CKV0 + KV_LORA], kvn_ref[...], dckv_ref[...])
        dp_ref[:, _CKV0:_CKV0 + KV_LORA] = dckv.astype(BF16)
        dkr = dk_ref[:, 0:HP].astype(F32)
        for h in range(1, HEADS):
            dkr = dkr + dk_ref[:, h * HP:(h + 1) * HP].astype(F32)
        lane = lax.broadcasted_iota(jnp.int32, dkr.shape, 1)
        dkr = jnp.where((lane >= NOPE) & (lane < QK_DIM), dkr, 0.0)
        dp_ref[:, _KR0:_KR0 + HP] = _rope_t(dkr, cos_ref[...], sa_ref[...], sb_ref[...]).astype(BF16)
        zu = p_ref[:, _Z0:_Z0 + SG_WIDTH]
        dp_ref[:, _Z0:_Z0 + SG_WIDTH] = (du_ref[...].astype(F32) * _gelu_grad(zu)).astype(BF16)
        zv = p_ref[:, _Z0 + SG_WIDTH:_Z0 + 2 * SG_WIDTH]
        dgv, gsg = _rms_bwd(_gelu(zv), sgn_ref[...], dvn_ref[...].astype(F32))
        dp_ref[:, _Z0 + SG_WIDTH:_Z0 + 2 * SG_WIDTH] = (dgv * _gelu_grad(zv)).astype(BF16)
        sums = [jnp.sum(t, axis=0, keepdims=True) for t in (gq, gkv, gsg)]

        @pl.when(i == 0)
        def _():
            for r, s in zip((dqn_ref, dkvn_ref, dsgn_ref), sums):
                r[...] = s

        @pl.when(i != 0)
        def _():
            for r, s in zip((dqn_ref, dkvn_ref, dsgn_ref), sums):
                r[...] += s

    row = lambda i: (i, 0)
    fixed = lambda i: (0, 0)
    return pl.pallas_call(
        body, name="even_prep_bwd", grid=(S // tm,),
        in_specs=[pl.BlockSpec((tm, PROJ_W), row), pl.BlockSpec((1, Q_LORA), fixed),
                  pl.BlockSpec((1, KV_LORA), fixed), pl.BlockSpec((1, SG_WIDTH), fixed)]
        + [pl.BlockSpec((tm, HP), row)] * 3
        + [pl.BlockSpec((tm, Q_LORA), row), pl.BlockSpec((tm, KV_LORA), row),
           pl.BlockSpec((tm, HEADS * HP), row), pl.BlockSpec((tm, SG_WIDTH), row),
           pl.BlockSpec((tm, SG_WIDTH), row)],
        out_specs=[pl.BlockSpec((tm, PROJ_W), row), pl.BlockSpec((1, Q_LORA), fixed),
                   pl.BlockSpec((1, KV_LORA), fixed), pl.BlockSpec((1, SG_WIDTH), fixed)],
        out_shape=[jax.ShapeDtypeStruct((S, PROJ_W), BF16), jax.ShapeDtypeStruct((1, Q_LORA), F32),
                   jax.ShapeDtypeStruct((1, KV_LORA), F32), jax.ShapeDtypeStruct((1, SG_WIDTH), F32)],
        compiler_params=_params(("arbitrary",)),
    )(proj, qn, kvn, sgn, cos, sa, sb, dcqn, dckvn, dk, du, dvn)


def _causal_mask(rows, cols):
    r = lax.broadcasted_iota(jnp.int32, (rows, cols), 0)
    c = lax.broadcasted_iota(jnp.int32, (rows, cols), 1)
    return c <= r


def _flash_fwd_call(q, k, v, carry=()):
    S = q.shape[0]
    tb = _pick(S, _ATT_BLOCK, 128)
    nq = S // tb
    nt_dims = (((1,), (1,)), ((), ()))

    nc = len(carry)

    def body(*refs):
        q_ref, k_ref, v_ref = refs[:3]
        o_ref, lse_ref = refs[3 + nc:5 + nc]
        s_a, s_b, m_ref, acc_ref = refs[5 + 2 * nc:9 + 2 * nc]
        h = pl.program_id(0)
        i = pl.program_id(1)
        if nc:
            send, forward, finish = _gather_phases(refs[3:3 + nc], refs[5 + nc:5 + 2 * nc], *refs[9 + 2 * nc:])
            pl.when((h == 0) & (i == 0))(send)
            pl.when((h == HEADS // 2) & (i == 0))(forward)

        def scores(buf, j):
            k_t = k_ref[pl.ds(pl.multiple_of(j * tb, tb), tb), :]
            buf[...] = lax.dot_general(q_ref[...], k_t, nt_dims, preferred_element_type=F32)

        def update(buf, j, masked):
            v_t = v_ref[pl.ds(pl.multiple_of(j * tb, tb), tb), :]
            s = buf[...]
            if masked:
                s = jnp.where(_causal_mask(tb, tb), s, -1e30)
            m = m_ref[...]
            m_new = jnp.maximum(m, jnp.max(s, axis=1, keepdims=True))
            alpha = jnp.exp(m - m_new)
            p = jnp.exp(s - m_new)
            acc_ref[...] = alpha * acc_ref[...] + jnp.dot(p.astype(v_t.dtype), v_t, preferred_element_type=F32)
            m_ref[...] = m_new

        m_ref[...] = jnp.full((tb, 1), -1e30, F32)
        acc_ref[...] = jnp.zeros((tb, HP), F32)
        scores(s_a, 0)
        pairs = i // 2

        def two_blocks(t, carry):
            scores(s_b, 2 * t + 1)
            update(s_a, 2 * t, False)
            scores(s_a, 2 * t + 2)
            update(s_b, 2 * t + 1, False)
            return carry

        lax.fori_loop(0, pairs, two_blocks, 0)

        @pl.when(2 * pairs == i)
        def _():
            update(s_a, i, True)

        @pl.when(2 * pairs != i)
        def _():
            scores(s_b, i)
            update(s_a, i - 1, False)
            update(s_b, i, True)

        acc = acc_ref[...]
        l = acc[:, VDIM:VDIM + 1]
        lane = lax.broadcasted_iota(jnp.int32, (tb, HP), 1)
        o_ref[...] = jnp.where(lane < VDIM, acc / l, 0.0).astype(o_ref.dtype)
        lse = jnp.broadcast_to(m_ref[...] + jnp.log(l), (tb, HP))
        lse_ref[0, 0] = jnp.transpose(lse)[0:8, :]
        if nc:
            pl.when((h == HEADS - 1) & (i == nq - 1))(finish)

    start = _gather_start(carry)
    any_spec = pl.BlockSpec(memory_space=pl.ANY)
    res = pl.pallas_call(
        body, name="flash_fwd_gather" if nc else "flash_fwd", grid=(HEADS, nq),
        in_specs=[pl.BlockSpec((tb, HP), lambda h, i: (i, h)), pl.BlockSpec((S, HP), lambda h, i: (0, h)),
                  pl.BlockSpec((S, HP), lambda h, i: (0, h))] + [any_spec] * nc,
        out_specs=[pl.BlockSpec((tb, HP), lambda h, i: (i, h)),
                   pl.BlockSpec((1, 1, 8, tb), lambda h, i: (h, i, 0, 0))] + [any_spec] * nc,
        out_shape=[jax.ShapeDtypeStruct((S, HEADS * HP), q.dtype), jax.ShapeDtypeStruct((HEADS, nq, 8, tb), F32)]
        + [jax.ShapeDtypeStruct(t.shape, t.dtype) for t in start],
        scratch_shapes=[pltpu.VMEM((tb, tb), F32), pltpu.VMEM((tb, tb), F32), pltpu.VMEM((tb, 1), F32),
                        pltpu.VMEM((tb, HP), F32)] + ([pltpu.SemaphoreType.DMA((n,)) for n in _gather_sems(nc)]
                                                      if nc else []),
        input_output_aliases={3 + a: 2 + a for a in range(nc)},
        compiler_params=pltpu.CompilerParams(vmem_limit_bytes=_VMEM_LIMIT, has_side_effects=bool(nc),
                                             dimension_semantics=("arbitrary", "arbitrary")),
    )(q, k, v, *start)
    return res[0], res[1], list(res[2:])


def _attn_delta_call(o, do):
    S = o.shape[0]
    tb = _pick(S, _ATT_BLOCK, 128)
    nq = S // tb
    nb = _pick(nq, 4, 1)

    def body(o_ref, do_ref, d_ref):
        for r in range(nb):
            rows = slice(r * tb, (r + 1) * tb)
            d = jnp.sum(o_ref[rows, :].astype(F32) * do_ref[rows, :].astype(F32), axis=1, keepdims=True)
            d_ref[0, r] = jnp.transpose(jnp.broadcast_to(d, (tb, HP)))[0:8, :]

    return pl.pallas_call(
        body, name="attn_delta", grid=(HEADS, nq // nb),
        in_specs=[pl.BlockSpec((nb * tb, HP), lambda h, i: (i, h))] * 2,
        out_specs=pl.BlockSpec((1, nb, 8, tb), lambda h, i: (h, i, 0, 0)),
        out_shape=jax.ShapeDtypeStruct((HEADS, nq, 8, tb), F32), compiler_params=_params(("arbitrary", "arbitrary")),
    )(o, do)


def _flash_bwd_call(q, k, v, do, lse, delta, carry=()):
    S = q.shape[0]
    tb = _pick(S, _ATT_BLOCK, 128)
    nq = S // tb
    nt_dims = (((1,), (1,)), ((), ()))
    tn_dims = (((0,), (0,)), ((), ()))
    nc = len(carry)

    def body(*refs):
        q_ref, do_ref, lse_ref, dl_ref, k_ref, v_ref = refs[:6]
        dq_ref, dk_ref, dv_ref = refs[6 + nc:9 + nc]
        st_a, dp_a, st_b, dp_b, dk_acc, dv_acc = refs[9 + 2 * nc:15 + 2 * nc]
        h = pl.program_id(0)
        j = pl.program_id(1)
        if nc:
            send, finish = _scatter_phases(refs[6:6 + nc], refs[9 + nc:9 + 2 * nc], *refs[15 + 2 * nc:])
            pl.when((h == 0) & (j == 0))(send)

        @pl.when(j == 0)
        def _():
            dq_ref[...] = jnp.zeros_like(dq_ref)

        def rows_of(i):
            return pl.ds(pl.multiple_of(i * tb, tb), tb)

        def scores(st_buf, dp_buf, i):
            st_buf[...] = lax.dot_general(k_ref[...], q_ref[rows_of(i), :], nt_dims, preferred_element_type=F32)
            dp_buf[...] = lax.dot_general(v_ref[...], do_ref[rows_of(i), :], nt_dims, preferred_element_type=F32)

        def update(st_buf, dp_buf, i, masked):
            q_t = q_ref[rows_of(i), :]
            do_t = do_ref[rows_of(i), :]
            pt = jnp.exp(st_buf[...] - lse_ref[0, i, 0:1, :])
            if masked:
                pt = jnp.where(jnp.transpose(_causal_mask(tb, tb)), pt, 0.0)
            dst = (pt * (dp_buf[...] - dl_ref[0, i, 0:1, :])).astype(q_t.dtype)
            dv_acc[...] += jnp.dot(pt.astype(do_t.dtype), do_t, preferred_element_type=F32)
            dk_acc[...] += jnp.dot(dst, q_t, preferred_element_type=F32)
            dq_ref[rows_of(i), :] += lax.dot_general(dst, k_ref[...], tn_dims, preferred_element_type=F32)

        last = nq - 1
        dk_acc[...] = jnp.zeros((tb, HP), F32)
        dv_acc[...] = jnp.zeros((tb, HP), F32)
        scores(st_b, dp_b, j)
        scores(st_a, dp_a, jnp.minimum(j + 1, last))
        update(st_b, dp_b, j, True)
        rest = last - j
        pairs = rest // 2

        def two_blocks(t, carry):
            i0 = j + 1 + 2 * t
            scores(st_b, dp_b, i0 + 1)
            update(st_a, dp_a, i0, False)
            scores(st_a, dp_a, jnp.minimum(i0 + 2, last))
            update(st_b, dp_b, i0 + 1, False)
            return carry

        lax.fori_loop(0, pairs, two_blocks, 0)

        @pl.when(2 * pairs != rest)
        def _():
            update(st_a, dp_a, last, False)

        dk_ref[...] = dk_acc[...].astype(dk_ref.dtype)
        dv_ref[...] = dv_acc[...].astype(dv_ref.dtype)
        if nc:
            pl.when((h == HEADS - 1) & (j == nq - 1))(finish)

    head = lambda h, j: (0, h)
    blk = lambda h, j: (j, h)
    rows = lambda h, j: (h, 0, 0, 0)
    any_spec = pl.BlockSpec(memory_space=pl.ANY)
    res = pl.pallas_call(
        body, name="flash_bwd_scatter" if nc else "flash_bwd", grid=(HEADS, nq),
        in_specs=[pl.BlockSpec((S, HP), head), pl.BlockSpec((S, HP), head), pl.BlockSpec((1, nq, 8, tb), rows),
                  pl.BlockSpec((1, nq, 8, tb), rows), pl.BlockSpec((tb, HP), blk), pl.BlockSpec((tb, HP), blk)]
        + [any_spec] * nc,
        out_specs=[pl.BlockSpec((S, HP), head), pl.BlockSpec((tb, HP), blk), pl.BlockSpec((tb, HP), blk)]
        + [any_spec] * nc,
        out_shape=[jax.ShapeDtypeStruct((S, HEADS * HP), F32), jax.ShapeDtypeStruct((S, HEADS * HP), BF16),
                   jax.ShapeDtypeStruct((S, HEADS * HP), BF16)]
        + [jax.ShapeDtypeStruct(p.shape, p.dtype) for p in carry],
        scratch_shapes=[pltpu.VMEM((tb, tb), F32)] * 4 + [pltpu.VMEM((tb, HP), F32)] * 2
        + ([pltpu.SemaphoreType.DMA((n,)) for n in _scatter_sems(nc)] if nc else []),
        compiler_params=pltpu.CompilerParams(vmem_limit_bytes=_VMEM_LIMIT, has_side_effects=bool(nc),
                                             dimension_semantics=("arbitrary", "arbitrary")),
    )(q, do, lse, delta, k, v, *carry)
    return res[0], res[1], res[2], list(res[3:])


def _sg_mixed(w_ref, vch, lane_lo):
    blocks = []
    for jb in range(SG_WIDTH // _LANES):
        r = jnp.dot(w_ref[jb], vch[:, jb * _LANES:(jb + 1) * _LANES], preferred_element_type=F32)
        blocks.append(jnp.where(lane_lo, r[0:SG_CHUNK], r[SG_CHUNK:2 * SG_CHUNK]))
    return jnp.concatenate(blocks, axis=1)


def _sgu_fwd_call(vn, u, attn, wst, bexp):
    S = vn.shape[0]
    tm = _pick(S, _SG_TILE, SG_CHUNK)
    AW = HEADS * HP

    def body(v_ref, u_ref, a_ref, w_ref, b_ref, mix_ref):
        lane_lo = lax.broadcasted_iota(jnp.int32, (SG_CHUNK, _LANES), 1) < SG_GDIM
        mix_ref[:, 0:AW] = a_ref[...]
        for c in range(tm // SG_CHUNK):
            rs = slice(c * SG_CHUNK, (c + 1) * SG_CHUNK)
            mixed = _sg_mixed(w_ref, v_ref[rs, :], lane_lo) + b_ref[...]
            mix_ref[rs, AW:AW + SG_WIDTH] = (u_ref[rs, :].astype(F32) * mixed).astype(mix_ref.dtype)

    row = lambda i: (i, 0)
    return pl.pallas_call(
        body, name="sgu_fwd", grid=(S // tm,),
        in_specs=[pl.BlockSpec((tm, SG_WIDTH), row), pl.BlockSpec((tm, SG_WIDTH), row), pl.BlockSpec((tm, AW), row),
                  pl.BlockSpec((SG_WIDTH // _LANES, 2 * SG_CHUNK, SG_CHUNK), lambda i: (0, 0, 0)),
                  pl.BlockSpec((SG_CHUNK, SG_WIDTH), lambda i: (0, 0))],
        out_specs=pl.BlockSpec((tm, AW + SG_WIDTH), row),
        out_shape=jax.ShapeDtypeStruct((S, AW + SG_WIDTH), BF16), compiler_params=_params(("arbitrary",)),
    )(vn, u, attn, wst, bexp)


def _sgu_bwd_call(dmix, vn, u, wst, wst_t, bexp):
    S = vn.shape[0]
    tm = _pick(S, _SG_TILE, SG_CHUNK)
    nblk = SG_WIDTH // _LANES
    col0 = (HEADS * HP) // SG_WIDTH
    nt_dims = (((1,), (1,)), ((), ()))

    def body(d_ref, v_ref, u_ref, w_ref, wt_ref, b_ref, du_ref, dv_ref, dw_ref, db_ref, dbacc_ref):
        i = pl.program_id(0)
        lane_lo = lax.broadcasted_iota(jnp.int32, (SG_CHUNK, _LANES), 1) < SG_GDIM

        @pl.when(i == 0)
        def _():
            dw_ref[...] = jnp.zeros_like(dw_ref)
            dbacc_ref[...] = jnp.zeros_like(dbacc_ref)

        for c in range(tm // SG_CHUNK):
            rs = slice(c * SG_CHUNK, (c + 1) * SG_CHUNK)
            vch = v_ref[rs, :]
            dsg = d_ref[rs, :].astype(F32)
            mixed = _sg_mixed(w_ref, vch, lane_lo) + b_ref[...]
            du_ref[rs, :] = (dsg * mixed).astype(du_ref.dtype)
            dmixed = dsg * u_ref[rs, :].astype(F32)
            dbacc_ref[...] += dmixed
            dmx = dmixed.astype(vch.dtype)
            dv_ref[rs, :] = _sg_mixed(wt_ref, dmx, lane_lo).astype(dv_ref.dtype)
            for jb in range(nblk):
                dblk = dmx[:, jb * _LANES:(jb + 1) * _LANES]
                vblk = vch[:, jb * _LANES:(jb + 1) * _LANES]
                zero = jnp.zeros_like(dblk)
                dw_ref[2 * jb] += lax.dot_general(jnp.where(lane_lo, dblk, zero), vblk, nt_dims,
                                                  preferred_element_type=F32)
                dw_ref[2 * jb + 1] += lax.dot_general(jnp.where(lane_lo, zero, dblk), vblk, nt_dims,
                                                      preferred_element_type=F32)

        @pl.when(i == pl.num_programs(0) - 1)
        def _():
            tri = _causal_mask(SG_CHUNK, SG_CHUNK)
            for g in range(SG_GROUPS):
                dw_ref[g] = jnp.where(tri, dw_ref[g], 0.0)
            lane = lax.broadcasted_iota(jnp.int32, (SG_CHUNK, _LANES), 1)
            out = jnp.zeros((SG_CHUNK, _LANES), F32)
            for g in range(SG_GROUPS):
                blk = dbacc_ref[:, (g // 2) * _LANES:(g // 2 + 1) * _LANES]
                sel = lane_lo if g % 2 == 0 else jnp.logical_not(lane_lo)
                s = jnp.sum(jnp.where(sel, blk, 0.0), axis=1, keepdims=True)
                out = jnp.where(lane == g, s, out)
            db_ref[...] = out

    row = lambda i: (i, 0)
    wspec = pl.BlockSpec((nblk, 2 * SG_CHUNK, SG_CHUNK), lambda i: (0, 0, 0))
    return pl.pallas_call(
        body, name="sgu_bwd", grid=(S // tm,),
        in_specs=[pl.BlockSpec((tm, SG_WIDTH), lambda i: (i, col0)), pl.BlockSpec((tm, SG_WIDTH), row),
                  pl.BlockSpec((tm, SG_WIDTH), row), wspec, wspec,
                  pl.BlockSpec((SG_CHUNK, SG_WIDTH), lambda i: (0, 0))],
        out_specs=[pl.BlockSpec((tm, SG_WIDTH), row), pl.BlockSpec((tm, SG_WIDTH), row),
                   pl.BlockSpec((SG_GROUPS, SG_CHUNK, SG_CHUNK), lambda i: (0, 0, 0)),
                   pl.BlockSpec((SG_CHUNK, _LANES), lambda i: (0, 0))],
        out_shape=[jax.ShapeDtypeStruct((S, SG_WIDTH), BF16), jax.ShapeDtypeStruct((S, SG_WIDTH), BF16),
                   jax.ShapeDtypeStruct((SG_GROUPS, SG_CHUNK, SG_CHUNK), F32),
                   jax.ShapeDtypeStruct((SG_CHUNK, _LANES), F32)],
        scratch_shapes=[pltpu.VMEM((SG_CHUNK, SG_WIDTH), F32)],
        compiler_params=_params(("arbitrary",)),
    )(dmix, vn, u, wst, wst_t, bexp)


def _shift_down(t, halo, n):
    rows = lax.broadcasted_iota(jnp.int32, t.shape, 0)
    out = pltpu.roll(t, n, 0)
    for r in range(n):
        out = jnp.where(rows == r, halo[8 - n + r:8 - n + r + 1, :], out)
    return out


def _shift_up(t, halo, n):
    tm = t.shape[0]
    rows = lax.broadcasted_iota(jnp.int32, t.shape, 0)
    out = pltpu.roll(t, tm - n, 0)
    for r in range(n):
        out = jnp.where(rows == tm - n + r, halo[r:r + 1, :], out)
    return out


def _conv_fwd_call(p, w):
    S, C3 = p.shape
    C = C3 // 3
    tm = _pick(S, _ROW_TILE, 8)
    hb = tm // 8

    def body(p_ref, c_prev, z_prev, w_ref, m_ref):
        i = pl.program_id(0)
        cz = p_ref[:, C:2 * C] * p_ref[:, 2 * C:3 * C]
        czp = jnp.where(i > 0, c_prev[...] * z_prev[...], 0.0)
        y = w_ref[2:3, :] * cz + w_ref[1:2, :] * _shift_down(cz, czp, 1) + w_ref[0:1, :] * _shift_down(cz, czp, 2)
        m_ref[...] = (p_ref[:, 0:C] * y).astype(m_ref.dtype)

    prev = lambda col: (lambda i: (jnp.maximum(i * hb - 1, 0), col))
    return pl.pallas_call(
        body, name="conv_fwd", grid=(S // tm,),
        in_specs=[pl.BlockSpec((tm, C3), lambda i: (i, 0)), pl.BlockSpec((8, C), prev(1)),
                  pl.BlockSpec((8, C), prev(2)), pl.BlockSpec((3, C), lambda i: (0, 0))],
        out_specs=pl.BlockSpec((tm, C), lambda i: (i, 0)),
        out_shape=jax.ShapeDtypeStruct((S, C), BF16), compiler_params=_params(("arbitrary",)),
    )(p, p, p, w)


def _conv_bwd_call(p, w, dm):
    S, C3 = p.shape
    C = C3 // 3
    tm = _pick(S, 256, 8)
    hb = tm // 8
    n_tiles = S // tm

    def body(p_ref, c_prev, z_prev, b_next, dm_ref, dm_next, w_ref, dp_ref, dw_ref):
        i = pl.program_id(0)
        b = p_ref[:, 0:C]
        c = p_ref[:, C:2 * C]
        z = p_ref[:, 2 * C:3 * C]
        cz = c * z
        czp = jnp.where(i > 0, c_prev[...] * z_prev[...], 0.0)
        s1 = _shift_down(cz, czp, 1)
        s2 = _shift_down(cz, czp, 2)
        w0, w1, w2 = w_ref[0:1, :], w_ref[1:2, :], w_ref[2:3, :]
        y = w2 * cz + w1 * s1 + w0 * s2
        dm_t = dm_ref[...]
        dy = dm_t * b
        dyn = jnp.where(i < n_tiles - 1, dm_next[...] * b_next[...], 0.0)
        dcz = w2 * dy + w1 * _shift_up(dy, dyn, 1) + w0 * _shift_up(dy, dyn, 2)
        dp_ref[:, 0:C] = (dm_t * y).astype(dp_ref.dtype)
        dp_ref[:, C:2 * C] = (dcz * z).astype(dp_ref.dtype)
        dp_ref[:, 2 * C:3 * C] = (dcz * c).astype(dp_ref.dtype)
        dw = jnp.concatenate([jnp.sum(dy * s2, axis=0, keepdims=True), jnp.sum(dy * s1, axis=0, keepdims=True),
                              jnp.sum(dy * cz, axis=0, keepdims=True)], axis=0)

        @pl.when(i == 0)
        def _():
            dw_ref[...] = dw

        @pl.when(i != 0)
        def _():
            dw_ref[...] += dw

    prev = lambda col: (lambda i: (jnp.maximum(i * hb - 1, 0), col))
    nxt = lambda col: (lambda i: (jnp.minimum((i + 1) * hb, S // 8 - 1), col))
    return pl.pallas_call(
        body, name="conv_bwd", grid=(n_tiles,),
        in_specs=[pl.BlockSpec((tm, C3), lambda i: (i, 0)), pl.BlockSpec((8, C), prev(1)),
                  pl.BlockSpec((8, C), prev(2)), pl.BlockSpec((8, C), nxt(0)),
                  pl.BlockSpec((tm, C), lambda i: (i, 0)), pl.BlockSpec((8, C), nxt(0)),
                  pl.BlockSpec((3, C), lambda i: (0, 0))],
        out_specs=[pl.BlockSpec((tm, C3), lambda i: (i, 0)), pl.BlockSpec((3, C), lambda i: (0, 0))],
        out_shape=[jax.ShapeDtypeStruct((S, C3), BF16), jax.ShapeDtypeStruct((3, C), F32)],
        compiler_params=_params(("arbitrary",)),
    )(p, p, p, p, dm, dm, w)


def _my_place():
    return lax.axis_index("x"), lax.axis_index("y"), lax.axis_index("c")


def _gather_weights_call(name, shard):
    R, C = shard.shape

    def body(s_ref, o_ref, send_sems, recv_sems, local_sem):
        x, y, c = _my_place()
        mine = 2 * x + y
        local = pltpu.make_async_copy(s_ref, o_ref.at[mine], local_sem)
        local.start()
        peers = [(1 - x, y), (x, 1 - y), (1 - x, 1 - y)]
        copies = []
        for k, (px, py) in enumerate(peers):
            cp = pltpu.make_async_remote_copy(src_ref=s_ref, dst_ref=o_ref.at[mine], send_sem=send_sems.at[k],
                                              recv_sem=recv_sems.at[k], device_id=(px, py, c), device_id_type=MESH)
            cp.start()
            copies.append(cp)
        for k, (px, py) in enumerate(peers):
            pltpu.make_async_remote_copy(src_ref=s_ref, dst_ref=o_ref.at[2 * px + py], send_sem=send_sems.at[k],
                                         recv_sem=recv_sems.at[k], device_id=(px, py, c),
                                         device_id_type=MESH).wait_recv()
        for cp in copies:
            cp.wait_send()
        local.wait()

    any_spec = pl.BlockSpec(memory_space=pl.ANY)
    return pl.pallas_call(
        body, name=name, in_specs=[any_spec], out_specs=any_spec,
        out_shape=jax.ShapeDtypeStruct((4, R, C), shard.dtype),
        scratch_shapes=[pltpu.SemaphoreType.DMA((3,)), pltpu.SemaphoreType.DMA((3,)), pltpu.SemaphoreType.DMA],
        compiler_params=pltpu.CompilerParams(has_side_effects=True),
    )(shard)


_D2D_CHUNKS = 4


_LOCAL_CHUNKS = 8


def _local_copies(src_of, dst_of, rows, sems, base):
    rc = rows // _LOCAL_CHUNKS
    assert rc * _LOCAL_CHUNKS == rows and rc % 16 == 0, rows
    out = []
    for j in range(_LOCAL_CHUNKS):
        sl = pl.ds(j * rc, rc)
        out.append(pltpu.make_async_copy(src_of(sl), dst_of(sl), sems.at[base + j]))
    return out


def _comm_call(name, body, arrays, out_shapes, sem_counts, aliases=None):
    any_spec = pl.BlockSpec(memory_space=pl.ANY)
    return pl.pallas_call(
        body, name=name, in_specs=[any_spec] * len(arrays), out_specs=[any_spec] * len(out_shapes),
        out_shape=out_shapes, scratch_shapes=[pltpu.SemaphoreType.DMA((n,)) for n in sem_counts],
        input_output_aliases=aliases or {}, compiler_params=pltpu.CompilerParams(has_side_effects=True),
    )(*arrays)


def _gather_halves_call(shards):
    na = len(shards)

    def body(*refs):
        send, forward, finish = _gather_phases(refs[:na], refs[na:2 * na], *refs[2 * na:])
        send()
        forward()
        finish()

    start = _gather_start(shards)
    outs = [jax.ShapeDtypeStruct(t.shape, t.dtype) for t in start]
    return _comm_call("gather_weights", body, start, outs, _gather_sems(na), aliases={a: a for a in range(na)})


def _gather_start(shards):
    for s in shards:
        assert s.shape[0] % (2 * _D2D_CHUNKS * 16) == 0, s.shape
    return [jnp.broadcast_to(s[None], (4,) + tuple(s.shape)) for s in shards]


def _gather_sems(na):
    return [3 * na, 3 * na, 3 * na * _D2D_CHUNKS, 3 * na * _D2D_CHUNKS]


def _gather_phases(s_refs, o_refs, ici_send, ici_recv, d2d_send, d2d_recv):
    na = len(s_refs)
    x, y, c = _my_place()
    mine = 2 * x + y
    chips = [(1 - x, y), (x, 1 - y), (1 - x, 1 - y)]

    def ici(a, k, chip, block):
        Rh = s_refs[a].shape[1] // 2
        my_half = pl.ds(pl.multiple_of(c * Rh, 16), Rh)
        return pltpu.make_async_remote_copy(src_ref=s_refs[a].at[mine, my_half], dst_ref=o_refs[a].at[block, my_half],
                                            send_sem=ici_send.at[3 * a + k], recv_sem=ici_recv.at[3 * a + k],
                                            device_id=(chip[0], chip[1], c), device_id_type=MESH)

    def d2d(a, k, j, block, half):
        Rh = s_refs[a].shape[1] // 2
        rc = Rh // _D2D_CHUNKS
        rows = pl.ds(pl.multiple_of(half * Rh + j * rc, 16), rc)
        idx = (3 * a + k) * _D2D_CHUNKS + j
        return pltpu.make_async_remote_copy(src_ref=o_refs[a].at[block, rows], dst_ref=o_refs[a].at[block, rows],
                                            send_sem=d2d_send.at[idx], recv_sem=d2d_recv.at[idx],
                                            device_id=(x, y, 1 - c), device_id_type=MESH)

    def send():
        for a in range(na):
            for k, chip in enumerate(chips):
                ici(a, k, chip, mine).start()

    def forward():
        for a in range(na):
            for k, chip in enumerate(chips):
                block = 2 * chip[0] + chip[1]
                ici(a, k, chip, block).wait_recv()
                for j in range(_D2D_CHUNKS):
                    d2d(a, k, j, block, c).start()

    def finish():
        for a in range(na):
            for k, chip in enumerate(chips):
                block = 2 * chip[0] + chip[1]
                for j in range(_D2D_CHUNKS):
                    d2d(a, k, j, block, 1 - c).wait_recv()
        for a in range(na):
            for k, chip in enumerate(chips):
                ici(a, k, chip, mine).wait_send()
                for j in range(_D2D_CHUNKS):
                    d2d(a, k, j, 2 * chip[0] + chip[1], c).wait_send()

    return send, forward, finish


def _pair_exchange_call(packed, tag):
    na = len(packed)

    def body(*refs):
        p_refs, o_refs = refs[:na], refs[na:2 * na]
        send_sems, recv_sems = refs[2 * na:]
        x, y, c = _my_place()
        copies = []
        for a in range(na):
            nb, _, Rh, _ = p_refs[a].shape
            rc = Rh // _D2D_CHUNKS
            assert rc * _D2D_CHUNKS == Rh and rc % 16 == 0
            for b in range(nb):
                for j in range(_D2D_CHUNKS):
                    rows = pl.ds(j * rc, rc)
                    idx = (a * nb + b) * _D2D_CHUNKS + j
                    copies.append(pltpu.make_async_remote_copy(
                        src_ref=p_refs[a].at[b, 1 - c, rows], dst_ref=o_refs[a].at[b, rows],
                        send_sem=send_sems.at[idx], recv_sem=recv_sems.at[idx],
                        device_id=(x, y, 1 - c), device_id_type=MESH))
        for t in copies:
            t.start()
        for t in copies:
            t.wait_recv()
        for t in copies:
            t.wait_send()

    outs = [jax.ShapeDtypeStruct((p.shape[0], p.shape[2], p.shape[3]), p.dtype) for p in packed]
    n = sum(p.shape[0] for p in packed) * _D2D_CHUNKS
    return _comm_call("pair_exchange_" + tag, body, packed, outs, [n, n])


def _pair_add_call(name, packed, other, core):
    nb, _, Rh, C = packed.shape
    tr = _pick(Rh, 512, 16)

    def body(c_ref, p_ref, o_ref, q_ref):
        q_ref[...] = (p_ref[...].astype(F32) + o_ref[...].astype(F32)).astype(q_ref.dtype)

    grid_spec = pltpu.PrefetchScalarGridSpec(
        num_scalar_prefetch=1, grid=(nb, Rh // tr),
        in_specs=[pl.BlockSpec((None, None, tr, C), lambda b, r, c_ref: (b, c_ref[0], r, 0)),
                  pl.BlockSpec((None, tr, C), lambda b, r, c_ref: (b, r, 0))],
        out_specs=pl.BlockSpec((None, tr, C), lambda b, r, c_ref: (b, r, 0)))
    return pl.pallas_call(
        body, name=name, grid_spec=grid_spec, out_shape=jax.ShapeDtypeStruct((nb, Rh, C), packed.dtype),
        compiler_params=_params(("arbitrary", "arbitrary")),
    )(core, packed, other)


def _chip_scatter_call(pairs):
    na = len(pairs)

    def body(*refs):
        send, finish = _scatter_phases(refs[:na], refs[na:2 * na], *refs[2 * na:])
        send()
        finish()

    outs = [jax.ShapeDtypeStruct(p.shape, p.dtype) for p in pairs]
    return _comm_call("chip_scatter", body, pairs, outs, _scatter_sems(na))


def _scatter_sems(na):
    return [3 * na, 3 * na, na * _LOCAL_CHUNKS]


def _scatter_phases(p_refs, o_refs, send_sems, recv_sems, local_sems):
    na = len(p_refs)
    x, y, c = _my_place()
    mine = 2 * x + y
    chips = [(1 - x, y), (x, 1 - y), (1 - x, 1 - y)]

    def local(a):
        p_ref, o_ref = p_refs[a], o_refs[a]
        return _local_copies(lambda sl: p_ref.at[mine, sl], lambda sl: o_ref.at[mine, sl], p_ref.shape[1],
                             local_sems, a * _LOCAL_CHUNKS)

    def remote(a, k, src_block, dst_block):
        px, py = chips[k]
        return pltpu.make_async_remote_copy(src_ref=p_refs[a].at[src_block], dst_ref=o_refs[a].at[dst_block],
                                            send_sem=send_sems.at[3 * a + k], recv_sem=recv_sems.at[3 * a + k],
                                            device_id=(px, py, c), device_id_type=MESH)

    def send():
        for a in range(na):
            for t in local(a):
                t.start()
            for k, (px, py) in enumerate(chips):
                remote(a, k, 2 * px + py, mine).start()

    def finish():
        for a in range(na):
            for k, (px, py) in enumerate(chips):
                remote(a, k, mine, 2 * px + py).wait_recv()
        for a in range(na):
            for k, (px, py) in enumerate(chips):
                remote(a, k, 2 * px + py, mine).wait_send()
            for t in local(a):
                t.wait()

    return send, finish


def _sum_slots_call(name, parts, core):
    n, R, C = parts.shape
    tr = _pick(R, 256, 8)

    def body(c_ref, p_ref, o_ref):
        acc = p_ref[0].astype(F32)
        for s in range(1, n):
            acc = acc + p_ref[s].astype(F32)
        o_ref[...] = acc

    grid_spec = pltpu.PrefetchScalarGridSpec(
        num_scalar_prefetch=1, grid=(R // tr,),
        in_specs=[pl.BlockSpec((n, tr, C), lambda i, c_ref: (0, i, 0))],
        out_specs=pl.BlockSpec((None, tr, C), lambda i, c_ref: (c_ref[0], i, 0)))
    return pl.pallas_call(
        body, name=name, grid_spec=grid_spec, out_shape=jax.ShapeDtypeStruct((2, R, C), F32),
        compiler_params=_params(("arbitrary",)),
    )(core, parts)


def _sibling_share_call(halves):
    na = len(halves)
    nch = 2 * _D2D_CHUNKS

    def body(*refs):
        h_refs, o_refs = refs[:na], refs[na:2 * na]
        send_sems, recv_sems = refs[2 * na:]
        x, y, c = _my_place()

        def cp(a, j, slot):
            rc = h_refs[a].shape[1] // nch
            rows = pl.ds(j * rc, rc)
            return pltpu.make_async_remote_copy(src_ref=h_refs[a].at[slot, rows], dst_ref=o_refs[a].at[slot, rows],
                                                send_sem=send_sems.at[a * nch + j], recv_sem=recv_sems.at[a * nch + j],
                                                device_id=(x, y, 1 - c), device_id_type=MESH)

        copies = [cp(a, j, c) for a in range(na) for j in range(nch)]
        for t in copies:
            t.start()
        for a in range(na):
            for j in range(nch):
                cp(a, j, 1 - c).wait_recv()
        for t in copies:
            t.wait_send()

    for h in halves:
        assert h.shape[1] % (nch * 8) == 0, h.shape
    outs = [jax.ShapeDtypeStruct(h.shape, h.dtype) for h in halves]
    return _comm_call("sibling_share", body, halves, outs, [na * nch, na * nch], aliases={a: a for a in range(na)})


def _allreduce_small_call(part):
    R, C = part.shape

    def body(p_ref, o_ref, slots, send_sems, recv_sems):
        x, y, c = _my_place()
        me = 4 * x + 2 * y + c
        peers = []
        for k in range(1, 8):
            px = x ^ (k >> 2) if (k >> 2) else x
            py = y ^ ((k >> 1) & 1) if ((k >> 1) & 1) else y
            pc = c ^ (k & 1) if (k & 1) else c
            peers.append((px, py, pc))
        copies = []
        for k, (px, py, pc) in enumerate(peers):
            cp = pltpu.make_async_remote_copy(src_ref=p_ref, dst_ref=slots.at[me], send_sem=send_sems.at[k],
                                              recv_sem=recv_sems.at[k], device_id=(px, py, pc), device_id_type=MESH)
            cp.start()
            copies.append(cp)
        slots[me] = p_ref[...]
        for k, (px, py, pc) in enumerate(peers):
            pltpu.make_async_remote_copy(src_ref=p_ref, dst_ref=slots.at[4 * px + 2 * py + pc],
                                         send_sem=send_sems.at[k], recv_sem=recv_sems.at[k],
                                         device_id=(px, py, pc), device_id_type=MESH).wait_recv()
        for cp in copies:
            cp.wait_send()
        acc = slots[0]
        for s in range(1, 8):
            acc = acc + slots[s]
        o_ref[...] = acc

    vm = pl.BlockSpec(memory_space=pltpu.VMEM)
    return pl.pallas_call(
        body, name="allreduce_small", in_specs=[vm], out_specs=vm,
        out_shape=jax.ShapeDtypeStruct((R, C), F32),
        scratch_shapes=[pltpu.VMEM((8, R, C), F32), pltpu.SemaphoreType.DMA((7,)), pltpu.SemaphoreType.DMA((7,))],
        compiler_params=pltpu.CompilerParams(has_side_effects=True, vmem_limit_bytes=_VMEM_LIMIT),
    )(part)


def _adamw_call(name, w, g, m, v):
    shape = w.shape
    cols = shape[-1]
    rows = int(np.prod(shape[:-1])) if len(shape) > 1 else 1
    w2, g2, m2, v2 = (t.reshape(rows, cols) for t in (w, g, m, v))
    tr = _pick(rows, 256, 8)
    c1 = 1.0 / (1.0 - ADAM_B1 ** ADAM_STEP)
    c2 = 1.0 / (1.0 - ADAM_B2 ** ADAM_STEP)

    def body(w_ref, g_ref, m_ref, v_ref, d_ref, nm_ref, nv_ref):
        gr = g_ref[...]
        m_new = ADAM_B1 * m_ref[...] + (1.0 - ADAM_B1) * gr
        v_new = ADAM_B2 * v_ref[...] + (1.0 - ADAM_B2) * (gr * gr)
        m_hat = m_new / (1.0 - ADAM_B1 ** ADAM_STEP)
        v_hat = v_new / (1.0 - ADAM_B2 ** ADAM_STEP)
        d_ref[...] = -ADAM_LR * (m_hat / (jnp.sqrt(v_hat) + ADAM_EPS) + ADAM_WD * w_ref[...])
        nm_ref[...] = m_new
        nv_ref[...] = v_new

    spec = pl.BlockSpec((tr, cols), lambda i: (i, 0))
    d, nm, nv = pl.pallas_call(
        body, name=name, grid=(rows // tr,), in_specs=[spec] * 4, out_specs=[spec] * 3,
        out_shape=[jax.ShapeDtypeStruct((rows, cols), F32)] * 3, compiler_params=_params(("arbitrary",)),
    )(w2, g2, m2, v2)
    return d.reshape(shape), nm.reshape(shape), nv.reshape(shape)


def _pad_rows(flat, mult):
    n = flat.shape[0]
    unit = PACK_COLS * mult
    total = -(-n // unit) * unit
    return jnp.pad(flat, (0, total - n)).reshape(total // PACK_COLS, PACK_COLS)


def _pad_axis(arr, axis, mult):
    n = arr.shape[axis]
    total = -(-n // mult) * mult
    if total == n:
        return arr
    widths = [(0, 0)] * arr.ndim
    widths[axis] = (0, total - n)
    return jnp.pad(arr, widths)


def _shard_slice(arr, axis, blk, nblk=4):
    w = arr.shape[axis] // nblk
    return lax.slice_in_dim(arr, blk * w, (blk + 1) * w, axis=axis)


_NT = (((1,), (1,)), ((), ()))
_TN = (((0,), (0,)), ((), ()))


def _ffn_fwd(tag, x, h, ht, wg, wu, wd, g_next):
    S, D = x.shape
    (wg, gi), (wu, ui), (wd, di) = wg, wu, wd
    NB, Fb = wg.shape[0], wg.shape[2]
    tm = _pick(S, 1024, 8)

    def gate_up(h_ref, wg_ref, wu_ref, a_ref, u_ref, s_ref):
        h_t = _mx(h_ref[...])
        a = jnp.dot(h_t, _mx(wg_ref[...]), preferred_element_type=F32)
        u = jnp.dot(h_t, _mx(wu_ref[...]), preferred_element_type=F32)
        sig = _sigmoid(a)
        silu = a * sig
        a_ref[...] = (u * (sig * (1.0 + a * (1.0 - sig)))).astype(a_ref.dtype)
        u_ref[...] = silu.astype(u_ref.dtype)
        s_ref[...] = (silu * u).astype(s_ref.dtype)

    hid = pl.BlockSpec((None, tm, Fb), lambda b, i: (b, i, 0))
    a, u, s = pl.pallas_call(
        gate_up, name=tag + "_gate_up", grid=(NB, S // tm),
        in_specs=[pl.BlockSpec((tm, D), lambda b, i: (i, 0)), pl.BlockSpec((None, D, Fb), lambda b, i: (b, gi, 0)),
                  pl.BlockSpec((None, D, Fb), lambda b, i: (b, ui, 0))], out_specs=[hid] * 3,
        out_shape=[jax.ShapeDtypeStruct((NB, S, Fb), BF16)] * 3, compiler_params=_params(("arbitrary", "arbitrary")),
    )(h, wg, wu)

    tm2 = _pick(S, 512, 128)

    def down(s_ref, wd_ref, x_ref, g_ref, xo_ref, ho_ref):
        acc = jnp.dot(_mx(s_ref[0]), _mx(wd_ref[0]), preferred_element_type=F32)
        for b in range(1, NB):
            acc = acc + jnp.dot(_mx(s_ref[b]), _mx(wd_ref[b]), preferred_element_type=F32)
        x_new = x_ref[...] + 0.5 * acc
        xo_ref[...] = x_new
        ho_ref[...] = (_rms_stats(x_new)[0] * g_ref[...]).astype(ho_ref.dtype)

    row = pl.BlockSpec((tm2, D), lambda i: (i, 0))
    x_new, h_next = pl.pallas_call(
        down, name=tag + "_down", grid=(S // tm2,),
        in_specs=[pl.BlockSpec((NB, tm2, Fb), lambda i: (0, i, 0)), pl.BlockSpec((NB, Fb, D), lambda i: (0, di, 0)),
                  row, pl.BlockSpec((1, D), lambda i: (0, 0))],
        out_specs=[row, row], out_shape=[jax.ShapeDtypeStruct((S, D), F32), jax.ShapeDtypeStruct((S, D), BF16)],
        compiler_params=_params(("arbitrary",)),
    )(s, wd, x, g_next)
    return x_new, h_next, None, (x, h, ht, a, u, s)


def _ffn_bwd(tag, saved, dx_out, dxb, wg, wu, wd, gain):
    x, h, ht, a, u, s = saved
    S, D = x.shape
    (wg, gi), (wu, ui), (wd, di) = wg, wu, wd
    NB, Fb = wg.shape[0], wg.shape[2]
    tm = _pick(S, 1024, 8)
    tk = _pick(S, 2048, 128)
    nk = S // tk

    def dgate_up(d_ref, wd_ref, a_ref, u_ref, da_ref, du_ref):
        ds = 0.5 * lax.dot_general(_mx(d_ref[...]), _mx(wd_ref[...]), _NT, preferred_element_type=F32)
        da_ref[...] = (ds * a_ref[...].astype(F32)).astype(da_ref.dtype)
        du_ref[...] = (ds * u_ref[...].astype(F32)).astype(du_ref.dtype)

    hid = pl.BlockSpec((None, tm, Fb), lambda b, i: (b, i, 0))
    da, du = pl.pallas_call(
        dgate_up, name=tag + "_dgate_up", grid=(NB, S // tm),
        in_specs=[pl.BlockSpec((tm, D), lambda b, i: (i, 0)), pl.BlockSpec((None, Fb, D), lambda b, i: (b, di, 0)),
                  hid, hid],
        out_specs=[hid, hid], out_shape=[jax.ShapeDtypeStruct((NB, S, Fb), BF16)] * 2,
        compiler_params=_params(("arbitrary", "arbitrary")),
    )(dxb, wd, a, u)

    def dw_down(s_ref, d_ref, o_ref, acc_ref):
        k = pl.program_id(1)
        p = lax.dot_general(_mx(s_ref[...]), _mx(d_ref[...]), _TN, preferred_element_type=F32)

        @pl.when(k == 0)
        def _():
            acc_ref[...] = p

        @pl.when(k != 0)
        def _():
            acc_ref[...] += p

        @pl.when(k == nk - 1)
        def _():
            o_ref[...] = (0.5 * acc_ref[...]).astype(o_ref.dtype)

    hk = pl.BlockSpec((None, tk, Fb), lambda b, k: (b, k, 0))
    dwd = pl.pallas_call(
        dw_down, name=tag + "_dw_down", grid=(NB, nk),
        in_specs=[hk, pl.BlockSpec((tk, D), lambda b, k: (k, 0))],
        out_specs=pl.BlockSpec((None, Fb, D), lambda b, k: (b, 0, 0)),
        out_shape=jax.ShapeDtypeStruct((NB, Fb, D), _WIRE_DTYPE), scratch_shapes=[pltpu.VMEM((Fb, D), F32)],
        compiler_params=_params(("arbitrary", "arbitrary")),
    )(s, dxb)

    def dw_gate_up(h_ref, da_ref, du_ref, og_ref, ou_ref, accg_ref, accu_ref):
        k = pl.program_id(1)
        h_t = _mx(h_ref[...])
        dims = _TN if ht is None else (((1,), (0,)), ((), ()))
        pg = lax.dot_general(h_t, _mx(da_ref[...]), dims, preferred_element_type=F32)
        pu = lax.dot_general(h_t, _mx(du_ref[...]), dims, preferred_element_type=F32)

        @pl.when(k == 0)
        def _():
            accg_ref[...] = pg
            accu_ref[...] = pu

        @pl.when(k != 0)
        def _():
            accg_ref[...] += pg
            accu_ref[...] += pu

        @pl.when(k == nk - 1)
        def _():
            og_ref[...] = accg_ref[...].astype(og_ref.dtype)
            ou_ref[...] = accu_ref[...].astype(ou_ref.dtype)

    wout = pl.BlockSpec((None, D, Fb), lambda b, k: (b, 0, 0))
    dwg, dwu = pl.pallas_call(
        dw_gate_up, name=tag + "_dw_gate_up", grid=(NB, nk),
        in_specs=[pl.BlockSpec((tk, D), lambda b, k: (k, 0)) if ht is None
                  else pl.BlockSpec((D, tk), lambda b, k: (0, k)), hk, hk], out_specs=[wout, wout],
        out_shape=[jax.ShapeDtypeStruct((NB, D, Fb), _WIRE_DTYPE)] * 2,
        scratch_shapes=[pltpu.VMEM((D, Fb), F32)] * 2, compiler_params=_params(("arbitrary", "arbitrary")),
    )(h if ht is None else ht, da, du)

    tm2 = _pick(S, 512, 8)

    def dx_body(da_ref, du_ref, wg_hbm, wu_hbm, x_ref, dxo_ref, g_ref, dx_ref, dxb_ref, dg_ref, wg_v, wu_v, sem):
        i = pl.program_id(0)

        @pl.when(i == 0)
        def _():
            cg = pltpu.make_async_copy(wg_hbm.at[:, pl.ds(gi * D, D), :], wg_v, sem.at[0])
            cu = pltpu.make_async_copy(wu_hbm.at[:, pl.ds(ui * D, D), :], wu_v, sem.at[1])
            cg.start()
            cu.start()
            cg.wait()
            cu.wait()

        dh = None
        for b in range(NB):
            t = lax.dot_general(_mx(da_ref[b]), wg_v[b], _NT, preferred_element_type=F32)
            t = t + lax.dot_general(_mx(du_ref[b]), wu_v[b], _NT, preferred_element_type=F32)
            dh = t if dh is None else dh + t
        dx_n, dg = _rms_bwd(x_ref[...], g_ref[...], dh)
        dx = dxo_ref[...] + dx_n
        dx_ref[...] = dx
        dxb_ref[...] = dx.astype(dxb_ref.dtype)
        dg = jnp.sum(dg, axis=0, keepdims=True)

        @pl.when(i == 0)
        def _():
            dg_ref[...] = dg

        @pl.when(i != 0)
        def _():
            dg_ref[...] += dg

    row = pl.BlockSpec((tm2, D), lambda i: (i, 0))
    hid2 = pl.BlockSpec((NB, tm2, Fb), lambda i: (0, i, 0))
    anyspec = pl.BlockSpec(memory_space=pl.ANY)
    fixed = pl.BlockSpec((1, D), lambda i: (0, 0))
    dx, dxb_new, dgain = pl.pallas_call(
        dx_body, name=tag + "_dx", grid=(S // tm2,),
        in_specs=[hid2, hid2, anyspec, anyspec, row, row, fixed], out_specs=[row, row, fixed],
        out_shape=[jax.ShapeDtypeStruct((S, D), F32), jax.ShapeDtypeStruct((S, D), BF16),
                   jax.ShapeDtypeStruct((1, D), F32)],
        scratch_shapes=[pltpu.VMEM((NB, D, Fb), wg.dtype), pltpu.VMEM((NB, D, Fb), wu.dtype),
                        pltpu.SemaphoreType.DMA((2,))],
        compiler_params=_params(("arbitrary",)),
    )(da, du, wg, wu, x, dx_out, gain)
    return dx, dxb_new, dgain, dwg, dwu, dwd


def _conv_mixer_fwd(tag, x, h, ht, w_in, w_taps, w_out, g_next):
    S, D = x.shape
    C3 = w_in.shape[1]
    tm = _pick(S, 512, 8)
    p, = _fused_matmul(tag + "_in", 'nn', [_op(h)], [_op(w_in)], [(0, 0, 0)], 1, _ident_epi(), [F32],
                       S, C3, D, tm, _pick(C3, 1024, 128), D)
    m = _conv_fwd_call(p, w_taps)
    x_new, h_next = _fused_matmul(tag + "_out", 'nn', [_op(m)], [_op(w_out)], [(0, 0, 0)], 1, _resid_norm_epi(1.0),
                                  [F32, BF16], S, D, D, tm, D, D, tile_extras=[x], row_extras=[g_next])
    return x_new, h_next, (x, h, ht, p, m)


def _conv_mixer_bwd(tag, saved, dx_out, dxb, w_in, w_taps, w_out, gain):
    x, h, ht, p, m = saved
    S, D = x.shape
    C3 = w_in.shape[1]
    tm = _pick(S, 512, 8)
    tk = _pick(S, 1024, 128)
    dm, = _fused_matmul(tag + "_dm", 'nt', [_op(dxb)], [_op(w_out)], [(0, 0, 0)], 1, _ident_epi(), [F32],
                        S, D, D, tm, D, D)
    dw_out, = _fused_matmul(tag + "_dw_out", 'tn', [_op(m)], [_op(dxb)], [(0, 0, 0)], 1, _ident_epi(), [F32],
                            D, D, S, D, D, tk)
    dp, dtaps = _conv_bwd_call(p, w_taps, dm)
    dw_in, = _fused_matmul(tag + "_dw_in", 'tn' if ht is None else 'nn', [_op(h if ht is None else ht)], [_op(dp)],
                           [(0, 0, 0)], 1, _ident_epi(), [F32],
                           D, C3, S, D, _pick(C3, 1024, 128), tk)
    dx, dxb_new, dgain = _fused_matmul(tag + "_dx", 'nt', [_op(dp)], [_op(w_in)], [(0, 0, 0)], 1, _norm_bwd_epi,
                                       [F32, BF16], S, D, C3, tm, D, C3,
                                       tile_extras=[x, dx_out], row_extras=[gain], n_colsum=1)
    return dx, dxb_new, dgain, dw_in, dtaps, dw_out


def _attn_scale():
    return np.float32(QK_DIM ** -0.5)


def _even_mixer_fwd(tag, x, h, ht, wts, tables, g_next, carry=()):
    S, D = x.shape
    cos, sa, sb = tables
    tm = _pick(S, 512, 8)
    AW = HEADS * HP
    proj, = _fused_matmul(tag + "_in", 'nn', [_op(h)], [_op(wts['w_in'])], [(0, 0, 0)], 1, _ident_epi(), [F32],
                          S, PROJ_W, D, tm, _pick(PROJ_W, 896, 128), D)
    cqn, ckvn, kr, u, vn = _even_prep_call(proj, wts['q_norm'], wts['kv_norm'], wts['sg_norm'], cos, sa, sb)
    scale = _attn_scale()

    def q_epi(accs, tiles, rows, mrows):
        c_t, a_t, b_t = mrows
        heads = [_rope(accs[0][:, hh * HP:(hh + 1) * HP], c_t, a_t, b_t) * scale for hh in range(HEADS)]
        return [jnp.concatenate(heads, axis=1)]

    q, = _fused_matmul(tag + "_q", 'nn', [_op(cqn)], [_op(wts['w_q'])], [(0, 0, 0)], 1, q_epi, [BF16],
                       S, AW, Q_LORA, tm, AW, Q_LORA, mrow_extras=[cos, sa, sb])

    def kv_epi(accs, tiles, rows, mrows):
        lane = lax.broadcasted_iota(jnp.int32, accs[1].shape, 1)
        v_t = jnp.where((lane & (HP - 1)) == VDIM, 1.0, accs[1])
        return [accs[0] + jnp.concatenate([mrows[0].astype(F32)] * HEADS, axis=1), v_t]

    k, v = _fused_matmul(tag + "_kv", 'nn', [_op(ckvn)], [_op(wts['w_k']), _op(wts['w_v'])],
                         [(0, 0, 0), (0, 1, 1)], 2, kv_epi, [BF16, BF16], S, AW, KV_LORA, tm, AW, KV_LORA,
                         mrow_extras=[kr])
    o, lse, gathered = _flash_fwd_call(q, k, v, carry)
    mix = _sgu_fwd_call(vn, u, o, wts['sg_wst'], wts['sg_bexp'])
    x_new, h_next = _fused_matmul(tag + "_out", 'nn', [_op(mix)], [_op(wts['w_out'])], [(0, 0, 0)], 1,
                                  _resid_norm_epi(1.0), [F32, BF16], S, D, AW + SG_WIDTH, tm, D, AW + SG_WIDTH,
                                  tile_extras=[x], row_extras=[g_next])
    return x_new, h_next, (x, h, ht, proj, cqn, ckvn, u, vn, q, k, v, o, lse, mix), gathered


def _even_mixer_bwd(tag, saved, dx_out, dxb, wts, tables, gain, carry=()):
    x, h, ht, proj, cqn, ckvn, u, vn, q, k, v, o, lse, mix = saved
    S, D = x.shape
    cos, sa, sb = tables
    tm = _pick(S, 512, 8)
    tk = _pick(S, 1024, 128)
    AW = HEADS * HP
    MW = AW + SG_WIDTH
    dmix, = _fused_matmul(tag + "_dmix", 'nt', [_op(dxb)], [_op(wts['w_out'])], [(0, 0, 0)], 1, _ident_epi(), [BF16],
                          S, MW, D, tm, _pick(MW, 768, 128), D)
    dw_out, = _fused_matmul(tag + "_dw_out", 'tn', [_op(mix)], [_op(dxb)], [(0, 0, 0)], 1, _ident_epi(), [F32],
                            MW, D, S, _pick(MW, 768, 128), D, tk)
    du, dvn, dsg_w, dsg_b = _sgu_bwd_call(dmix, vn, u, wts['sg_wst'], wts['sg_wst_t'], wts['sg_bexp'])
    delta = _attn_delta_call(o, dmix)
    dq, dk, dv, arrived = _flash_bwd_call(q, k, v, dmix, lse, delta, carry)
    scale = _attn_scale()

    def dq_epi(accs, tiles, rows, mrows):
        return accs

    def dq_pre_call():
        tr = _pick(S, 256, 8)

        def body(d_ref, c_ref, a_ref, b_ref, o_ref):
            for hh in range(HEADS):
                t = _rope_t(d_ref[:, hh * HP:(hh + 1) * HP], c_ref[...], a_ref[...], b_ref[...]) * scale
                o_ref[:, hh * HP:(hh + 1) * HP] = t.astype(o_ref.dtype)

        row = lambda i: (i, 0)
        return pl.pallas_call(
            body, name=tag + "_dq_unrope", grid=(S // tr,),
            in_specs=[pl.BlockSpec((tr, AW), row)] + [pl.BlockSpec((tr, HP), row)] * 3,
            out_specs=pl.BlockSpec((tr, AW), row), out_shape=jax.ShapeDtypeStruct((S, AW), BF16),
            compiler_params=_params(("arbitrary",)),
        )(dq, cos, sa, sb)

    dqp = dq_pre_call()
    dw_q, = _fused_matmul(tag + "_dw_q", 'tn', [_op(cqn)], [_op(dqp)], [(0, 0, 0)], 1, _ident_epi(), [F32],
                          Q_LORA, AW, S, Q_LORA, AW, tk)
    dcqn, = _fused_matmul(tag + "_dcq", 'nt', [_op(dqp)], [_op(wts['w_q'])], [(0, 0, 0)], 1, dq_epi, [F32],
                          S, Q_LORA, AW, tm, Q_LORA, AW)
    dw_k, dw_v = _fused_matmul(tag + "_dw_kv", 'tn', [_op(ckvn)], [_op(dk), _op(dv)], [(0, 0, 0), (0, 1, 1)], 2,
                               _ident_epi(), [F32, F32], KV_LORA, AW, S, KV_LORA, AW, tk)
    dckvn, = _fused_matmul(tag + "_dckv", 'nt', [_op(dk), _op(dv)], [_op(wts['w_k']), _op(wts['w_v'])],
                           [(0, 0, 0), (1, 1, 0)], 1, dq_epi, [F32], S, KV_LORA, AW, tm, KV_LORA, AW)
    dproj, dqn, dkvn, dsgn = _even_prep_bwd_call(proj, wts['q_norm'], wts['kv_norm'], wts['sg_norm'], cos, sa, sb,
                                                 dcqn, dckvn, dk, du, dvn)
    dw_in, = _fused_matmul(tag + "_dw_in", 'tn' if ht is None else 'nn', [_op(h if ht is None else ht)],
                           [_op(dproj)], [(0, 0, 0)], 1, _ident_epi(), [F32],
                           D, PROJ_W, S, D, _pick(PROJ_W, 896, 128), tk)
    dx, dxb_new, dgain = _fused_matmul(tag + "_dx", 'nt', [_op(dproj)], [_op(wts['w_in'])], [(0, 0, 0)], 1,
                                       _norm_bwd_epi, [F32, BF16], S, D, PROJ_W, tm, D, PROJ_W,
                                       tile_extras=[x, dx_out], row_extras=[gain], n_colsum=1)
    grads = dict(w_in=dw_in, w_q=dw_q, w_k=dw_k, w_v=dw_v, w_out=dw_out, q_norm=dqn, kv_norm=dkvn, sg_norm=dsgn,
                 sg_w=dsg_w, sg_b=dsg_b)
    return dx, dxb_new, dgain, grads, arrived


def _even_weights(w_in, w_uq, w_ukv, w_out, q_norm, kv_norm, sg_norm, sg_w, sg_b):
    D = w_in.shape[0]
    kr_cols = jnp.pad(w_in[:, Q_LORA + KV_LORA:Q_LORA + KV_LORA + ROPE], ((0, 0), (NOPE, HP - QK_DIM)))
    w_in_p = jnp.concatenate([w_in[:, :Q_LORA + KV_LORA], kr_cols, w_in[:, Q_LORA + KV_LORA + ROPE:]], axis=1)
    wq = w_uq.reshape(Q_LORA, HEADS, QK_DIM)
    w_q = jnp.pad(wq, ((0, 0), (0, 0), (0, HP - QK_DIM))).reshape(Q_LORA, HEADS * HP)
    wkv = w_ukv.reshape(KV_LORA, HEADS, NOPE + VDIM)
    w_k = jnp.pad(wkv[:, :, :NOPE], ((0, 0), (0, 0), (0, HP - NOPE))).reshape(KV_LORA, HEADS * HP)
    w_v = jnp.pad(wkv[:, :, NOPE:], ((0, 0), (0, 0), (0, HP - VDIM))).reshape(KV_LORA, HEADS * HP)
    wo_a = w_out[:HEADS * VDIM].reshape(HEADS, VDIM, D)
    wo_a = jnp.pad(wo_a, ((0, 0), (0, HP - VDIM), (0, 0))).reshape(HEADS * HP, D)
    w_out_p = jnp.concatenate([wo_a, w_out[HEADS * VDIM:]], axis=0)
    tri = jnp.tril(jnp.ones((SG_CHUNK, SG_CHUNK), F32))
    wm = sg_w * tri
    wst = wm.reshape(SG_GROUPS // 2, 2 * SG_CHUNK, SG_CHUNK).astype(_MXU_DTYPE)
    wst_t = jnp.swapaxes(wm, 1, 2).reshape(SG_GROUPS // 2, 2 * SG_CHUNK, SG_CHUNK).astype(_MXU_DTYPE)
    bexp = jnp.repeat(sg_b.T, SG_GDIM, axis=1)
    return dict(w_in=w_in_p, w_q=w_q, w_k=w_k, w_v=w_v, w_out=w_out_p, sg_wst=wst, sg_wst_t=wst_t, sg_bexp=bexp,
                q_norm=q_norm.reshape(1, -1), kv_norm=kv_norm.reshape(1, -1), sg_norm=sg_norm.reshape(1, -1))


def _even_grads_unpad(g):
    d_in = g['w_in']
    kr0 = Q_LORA + KV_LORA
    dw_in = jnp.concatenate([d_in[:, :kr0], d_in[:, kr0 + NOPE:kr0 + QK_DIM], d_in[:, kr0 + HP:]], axis=1)
    dw_uq = g['w_q'].reshape(Q_LORA, HEADS, HP)[:, :, :QK_DIM].reshape(Q_LORA, HEADS * QK_DIM)
    dk = g['w_k'].reshape(KV_LORA, HEADS, HP)[:, :, :NOPE]
    dv = g['w_v'].reshape(KV_LORA, HEADS, HP)[:, :, :VDIM]
    dw_ukv = jnp.concatenate([dk, dv], axis=2).reshape(KV_LORA, HEADS * (NOPE + VDIM))
    D = d_in.shape[0]
    wo = g['w_out']
    wo_a = wo[:HEADS * HP].reshape(HEADS, HP, D)[:, :VDIM].reshape(HEADS * VDIM, D)
    dw_out = jnp.concatenate([wo_a, wo[HEADS * HP:]], axis=0)
    dsg_b = g['sg_b'][:, :SG_GROUPS].T
    return dict(even_w_in=dw_in, w_uq=dw_uq, w_ukv=dw_ukv, even_w_out=dw_out, q_norm=g['q_norm'][0],
                kv_norm=g['kv_norm'][0], sg_norm=g['sg_norm'][0], sg_w=g['sg_w'], sg_b=dsg_b)


def kernel(x, positions, ffn_pre_norm, ffn_pre_w_gate, ffn_pre_w_up, ffn_pre_w_down, mix_norm, ffn_post_norm, ffn_post_w_gate, ffn_post_w_up, ffn_post_w_down, even_w_in, q_norm, w_uq, kv_norm, w_ukv, sg_norm, sg_w, sg_b, even_w_out, conv_w_in, conv_w, conv_w_out, final_norm, loss_target, m_ffn_pre_norm, m_ffn_pre_w_gate, m_ffn_pre_w_up, m_ffn_pre_w_down, m_mix_norm, m_ffn_post_norm, m_ffn_post_w_gate, m_ffn_post_w_up, m_ffn_post_w_down, m_even_w_in, m_q_norm, m_w_uq, m_kv_norm, m_w_ukv, m_sg_norm, m_sg_w, m_sg_b, m_even_w_out, m_conv_w_in, m_conv_w, m_conv_w_out, m_final_norm, v_ffn_pre_norm, v_ffn_pre_w_gate, v_ffn_pre_w_up, v_ffn_pre_w_down, v_mix_norm, v_ffn_post_norm, v_ffn_post_w_gate, v_ffn_post_w_up, v_ffn_post_w_down, v_even_w_in, v_q_norm, v_w_uq, v_kv_norm, v_w_ukv, v_sg_norm, v_sg_w, v_sg_b, v_even_w_out, v_conv_w_in, v_conv_w, v_conv_w_out, v_final_norm):
    env = dict(locals())
    w_loc = {n: env[n] for n in WEIGHTS}
    m_loc = {n: env['m_' + n] for n in WEIGHTS}
    v_loc = {n: env['v_' + n] for n in WEIGHTS}
    S, D = x.shape[1], x.shape[2]
    depth = ffn_pre_norm.shape[0]
    xs = x.reshape(S, D)
    target = loss_target.reshape(S, D)

    def layers_of(n, early):
        count = w_loc[n].shape[0]
        if n in ('conv_w_in', 'conv_w', 'conv_w_out'):
            return [] if early else list(range(count))
        return [0] if early else list(range(1, count))

    def shards_of(early):
        wire = lambda n, l: w_loc[n][l].astype(_WIRE_DTYPE)
        keys = [[(n, l) for n in group for l in layers_of(n, early)] for group in (GROUP_A, GROUP_B, GATHER_C)]
        sa = _pad_axis(jnp.concatenate([wire(n, l) for n, l in keys[0]], axis=0), 0, PACK_ROW_MULT)
        sb = _pad_axis(jnp.concatenate([wire(n, l) for n, l in keys[1]], axis=0), 0, PACK_ROW_MULT)
        sc = _pad_rows(jnp.concatenate([wire(n, l).reshape(-1) for n, l in keys[2]]), PACK_ROW_MULT)
        return [sa, sb, sc], keys

    full = {n: {} for n in SHARDED}

    def unpack(gathered, keys):
        gat_a, gat_b, gat_c = gathered
        for idx, (n, l) in enumerate(keys[0]):
            full[n][l] = (gat_a, idx)
        row = 0
        for n, l in keys[1]:
            rows = w_loc[n].shape[1]
            if n in FFN_WEIGHTS:
                full[n][l] = (gat_b, row // rows)
            else:
                full[n][l] = jnp.concatenate([gat_b[b, row:row + rows] for b in range(4)], axis=0)
            row += rows
        gflat = gat_c.reshape(4, -1)
        off = 0
        for n, l in keys[2]:
            shp = w_loc[n].shape[1:]
            size = int(np.prod(shp))
            full[n][l] = jnp.concatenate([gflat[b, off:off + size].reshape(shp) for b in range(4)],
                                         axis=SHARD_AXIS[n] - 1)
            off += size

    early_shards, early_keys = shards_of(True)
    unpack(_gather_halves_call(early_shards), early_keys)
    late_shards, late_keys = shards_of(False)
    taps = _gather_weights_call("gather_taps", _pad_rows(conv_w.reshape(-1), 8)).reshape(4, -1)
    taps = jnp.concatenate([taps[b, :conv_w.size].reshape(conv_w.shape) for b in range(4)], axis=2)

    inv_freq = ROPE_THETA ** (-jnp.arange(0, ROPE, 2, dtype=F32) / ROPE)
    half = ROPE // 2
    zeros = lambda n: jnp.zeros((n,), F32)
    ones = jnp.ones((half,), F32)
    invf = jnp.concatenate([zeros(NOPE), inv_freq, inv_freq, zeros(HP - QK_DIM)]).reshape(1, HP)
    mask_a = jnp.concatenate([zeros(NOPE), -ones, zeros(HP - NOPE - half)]).reshape(1, HP)
    mask_b = jnp.concatenate([zeros(NOPE + half), ones, zeros(HP - QK_DIM)]).reshape(1, HP)
    tables = _rope_tables_call(positions.reshape(S, 1), invf, mask_a, mask_b)

    even_w = {}

    def even_weights_of(e):
        if e not in even_w:
            even_w[e] = _even_weights(full['even_w_in'][e], full['w_uq'][e], full['w_ukv'][e], full['even_w_out'][e],
                                      q_norm[e], kv_norm[e], sg_norm[e], sg_w[e], sg_b[e])
        return even_w[e]

    def gain_row(arr, l):
        return arr[l].reshape(1, D)

    saved = []
    h, ht = _rmsnorm_call("first_norm", xs, gain_row(ffn_pre_norm, 0))
    xc = xs
    for l in range(depth):
        xc, h, ht, s_pre = _ffn_fwd(f"l{l}_pre", xc, h, ht, full['ffn_pre_w_gate'][l], full['ffn_pre_w_up'][l],
                                    full['ffn_pre_w_down'][l], gain_row(mix_norm, l))
        if l % 2 == 0:
            xc, h, s_mix, gathered = _even_mixer_fwd(f"l{l}_mix", xc, h, ht, even_weights_of(l // 2), tables,
                                                     gain_row(ffn_post_norm, l), late_shards if l == 0 else ())
            if l == 0:
                unpack(gathered, late_keys)
        else:
            o = l // 2
            xc, h, s_mix = _conv_mixer_fwd(f"l{l}_mix", xc, h, ht, full['conv_w_in'][o], taps[o],
                                           full['conv_w_out'][o], gain_row(ffn_post_norm, l))
        g_next = gain_row(ffn_pre_norm, l + 1) if l + 1 < depth else final_norm.reshape(1, D)
        xc, h, ht, s_post = _ffn_fwd(f"l{l}_post", xc, h, None, full['ffn_post_w_gate'][l], full['ffn_post_w_up'][l],
                                     full['ffn_post_w_down'][l], g_next)
        saved.append((s_pre, s_mix, s_post))

    dx, dxb, d_final, loss_part = _loss_call(xc, target, final_norm.reshape(1, D))
    loss = lax.psum(loss_part[0, 0], ("x", "y", "c"))

    gl = {n: [None] * w_loc[n].shape[0] for n in WEIGHTS if n != 'final_norm'}
    core = lax.axis_index("c").astype(jnp.int32).reshape(1)

    def pair_sums(first, tag):
        keys = [[(n, l) for n in group for l in layers_of(n, first)] for group in (GROUP_A, GROUP_B, GROUP_C)]

        def rows_blocked(n, l):
            g = gl[n][l]
            return g.reshape(4, g.shape[0] // 4, g.shape[1]).astype(_WIRE_DTYPE)

        pack_a = jnp.concatenate([gl[n][l] for n, l in keys[0]], axis=1)
        pack_b = jnp.concatenate([gl[n][l] if n in FFN_WEIGHTS else rows_blocked(n, l) for n, l in keys[1]], axis=1)
        pack_c = jnp.stack([_pad_rows(jnp.concatenate(
            [_shard_slice(gl[n][l], SHARD_AXIS[n] - 1, b).astype(_WIRE_DTYPE).reshape(-1) for n, l in keys[2]]),
            PACK_ROW_MULT) for b in range(4)])
        packs = [_pad_axis(p, 1, PACK_ROW_MULT) for p in (pack_a, pack_b, pack_c)]
        packs = [p.reshape(4, 2, p.shape[1] // 2, p.shape[2]) for p in packs]
        theirs = _pair_exchange_call(packs, tag)
        return [_pair_add_call(f"pair_add_{tag}_{i}", p, t, core)
                for i, (p, t) in enumerate(zip(packs, theirs))], keys
    for l in reversed(range(depth)):
        s_pre, s_mix, s_post = saved[l]
        dx, dxb, dgain, dwg, dwu, dwd = _ffn_bwd(f"l{l}_post", s_post, dx, dxb, full['ffn_post_w_gate'][l],
                                                 full['ffn_post_w_up'][l], full['ffn_post_w_down'][l],
                                                 gain_row(ffn_post_norm, l))
        gl['ffn_post_norm'][l] = dgain[0]
        gl['ffn_post_w_gate'][l], gl['ffn_post_w_up'][l], gl['ffn_post_w_down'][l] = dwg, dwu, dwd
        if l % 2 == 0:
            e = l // 2
            if l == 0:
                pairs_rest, keys_rest = pair_sums(False, "rest")
            dx, dxb, dgain, eg, arrived = _even_mixer_bwd(f"l{l}_mix", s_mix, dx, dxb, even_weights_of(e), tables,
                                                          gain_row(mix_norm, l), pairs_rest if l == 0 else ())
            if l == 0:
                arrived_rest = arrived
            for n, val in _even_grads_unpad(eg).items():
                gl[n][e] = val
        else:
            o = l // 2
            dx, dxb, dgain, dw_in, dtaps, dw_out = _conv_mixer_bwd(f"l{l}_mix", s_mix, dx, dxb, full['conv_w_in'][o],
                                                                   taps[o], full['conv_w_out'][o],
                                                                   gain_row(mix_norm, l))
            gl['conv_w_in'][o], gl['conv_w'][o], gl['conv_w_out'][o] = dw_in, dtaps, dw_out
        gl['mix_norm'][l] = dgain[0]
        dx, dxb, dgain, dwg, dwu, dwd = _ffn_bwd(f"l{l}_pre", s_pre, dx, dxb, full['ffn_pre_w_gate'][l],
                                                 full['ffn_pre_w_up'][l], full['ffn_pre_w_down'][l],
                                                 gain_row(ffn_pre_norm, l))
        gl['ffn_pre_norm'][l] = dgain[0]
        gl['ffn_pre_w_gate'][l], gl['ffn_pre_w_up'][l], gl['ffn_pre_w_down'][l] = dwg, dwu, dwd
    grad_x = dx.reshape(x.shape)
    part = {n: jnp.stack(gl[n]) for n in gl if n not in FFN_WEIGHTS}
    part['final_norm'] = d_final[0]

    pairs, keys_first = pair_sums(True, "first")
    arrived_first = _chip_scatter_call(pairs)
    mine = [_sum_slots_call(f"sum_grad_slots_{i}", r, core) for i, r in enumerate(list(arrived_rest) + list(arrived_first))]
    reduced = [t.reshape(-1, t.shape[2]) for t in _sibling_share_call(mine)]
    per_layer = {n: {} for n in SHARDED}
    for (red_a, red_b, red_c), keys in ((reduced[:3], keys_rest), (reduced[3:], keys_first)):
        for idx, (n, l) in enumerate(keys[0]):
            per_layer[n][l] = red_a[idx * D:(idx + 1) * D]
        row = 0
        for n, l in keys[1]:
            rows = w_loc[n].shape[1]
            per_layer[n][l] = red_b[row:row + rows]
            row += rows
        red_c = red_c.reshape(-1)
        off = 0
        for n, l in keys[2]:
            shp = w_loc[n].shape[1:]
            size = int(np.prod(shp))
            per_layer[n][l] = red_c[off:off + size].reshape(shp)
            off += size
    grads = {n: jnp.stack([per_layer[n][l] for l in range(w_loc[n].shape[0])]) for n in SHARDED}

    small = _pad_rows(jnp.concatenate([part[n].reshape(-1) for n in REPLICATED]), 8)
    small_sum = _allreduce_small_call(small).reshape(-1)
    off = 0
    for n in REPLICATED:
        size = int(np.prod(w_loc[n].shape))
        grads[n] = small_sum[off:off + size].reshape(w_loc[n].shape)
        off += size

    deltas, new_m, new_v = {}, {}, {}
    for n in WEIGHTS:
        deltas[n], new_m[n], new_v[n] = _adamw_call("adamw_" + n, w_loc[n], grads[n], m_loc[n], v_loc[n])
    return (loss, grad_x, *[grads[n] for n in WEIGHTS], *[deltas[n] for n in WEIGHTS],
            *[new_m[n] for n in WEIGHTS], *[new_v[n] for n in WEIGHTS])
```

```python
import functools

import numpy as np
import jax
import jax.numpy as jnp
from jax import lax
from jax.experimental import pallas as pl
from jax.experimental.pallas import tpu as pltpu

F32 = jnp.float32
BF16 = jnp.bfloat16
_MXU_DTYPE = jnp.bfloat16
_WIRE_DTYPE = jnp.bfloat16
_VMEM_LIMIT = 52 * 1024 * 1024
_LANES = 128
_ATT_BLOCK = 512
_ROW_TILE = 512
_SG_TILE = 1024

NORM_EPS = 1e-6
HEADS = 8
NOPE = 64
ROPE = 32
VDIM = 64
QK_DIM = NOPE + ROPE
HP = 128
Q_LORA = 384
KV_LORA = 256
SG_WIDTH = 512
SG_GROUPS = 8
SG_GDIM = 64
SG_CHUNK = 128
ROPE_THETA = 10000.0
PROJ_W = Q_LORA + KV_LORA + HP + 2 * SG_WIDTH
ADAM_LR = 0.001
ADAM_B1 = 0.9
ADAM_B2 = 0.999
ADAM_EPS = 1e-08
ADAM_WD = 0.01
ADAM_STEP = 10
MESH = pl.DeviceIdType.MESH
PACK_COLS = 1024
PACK_ROW_MULT = 256

SHARDED = ['ffn_pre_w_gate', 'ffn_pre_w_up', 'ffn_pre_w_down', 'ffn_post_w_gate', 'ffn_post_w_up',
           'ffn_post_w_down', 'even_w_in', 'w_uq', 'w_ukv', 'even_w_out', 'conv_w_in', 'conv_w', 'conv_w_out']
SHARD_AXIS = {'ffn_pre_w_gate': 2, 'ffn_pre_w_up': 2, 'ffn_pre_w_down': 1, 'ffn_post_w_gate': 2,
              'ffn_post_w_up': 2, 'ffn_post_w_down': 1, 'even_w_in': 2, 'w_uq': 2, 'w_ukv': 2,
              'even_w_out': 1, 'conv_w_in': 2, 'conv_w': 2, 'conv_w_out': 1}
FFN_WEIGHTS = ['ffn_pre_w_gate', 'ffn_pre_w_up', 'ffn_pre_w_down', 'ffn_post_w_gate', 'ffn_post_w_up',
               'ffn_post_w_down']
GROUP_A = ['ffn_pre_w_gate', 'ffn_pre_w_up', 'ffn_post_w_gate', 'ffn_post_w_up']
GROUP_B = ['ffn_pre_w_down', 'ffn_post_w_down', 'even_w_out', 'conv_w_out']
GROUP_C = ['even_w_in', 'w_uq', 'w_ukv', 'conv_w_in', 'conv_w']
GATHER_C = ['even_w_in', 'w_uq', 'w_ukv', 'conv_w_in']
REPLICATED = ['ffn_pre_norm', 'mix_norm', 'ffn_post_norm', 'q_norm', 'kv_norm', 'sg_norm', 'sg_w', 'sg_b',
              'final_norm']
WEIGHTS = ['ffn_pre_norm', 'ffn_pre_w_gate', 'ffn_pre_w_up', 'ffn_pre_w_down', 'mix_norm', 'ffn_post_norm',
           'ffn_post_w_gate', 'ffn_post_w_up', 'ffn_post_w_down', 'even_w_in', 'q_norm', 'w_uq', 'kv_norm',
           'w_ukv', 'sg_norm', 'sg_w', 'sg_b', 'even_w_out', 'conv_w_in', 'conv_w', 'conv_w_out', 'final_norm']


def _params(sem=None):
    return pltpu.CompilerParams(vmem_limit_bytes=_VMEM_LIMIT,
                                **({} if sem is None else {'dimension_semantics': sem}))


def _pick(n, pref, mult):
    best = None
    t = mult
    while t <= min(n, pref):
        if n % t == 0:
            best = t
        t += mult
    return n if best is None else best


def _mx(v):
    return v if v.dtype == _MXU_DTYPE else v.astype(_MXU_DTYPE)


def _sigmoid(a):
    return 1.0 / (1.0 + jnp.exp(-a))


def _rms_stats(x):
    rstd = lax.rsqrt(jnp.mean(x * x, axis=-1, keepdims=True) + NORM_EPS)
    return x * rstd, rstd


def _rms_bwd(x, g, dh):
    xhat, rstd = _rms_stats(x)
    gdh = g * dh
    dx = rstd * (gdh - xhat * jnp.mean(gdh * xhat, axis=-1, keepdims=True))
    return dx, dh * xhat


def _fused_matmul(name, mode, lhs, rhs, prods, n_acc, epilogue, out_dtypes, M, N, K, tm, tn, tk,
                  tile_extras=(), row_extras=(), mrow_extras=(), n_colsum=0):
    gj, gi, gk = N // tn, M // tm, K // tk
    assert gj * tn == N and gi * tm == M and gk * tk == K, (name, M, N, K, tm, tn, tk)
    dims = {'nn': (((1,), (0,)), ((), ())), 'nt': (((1,), (1,)), ((), ())), 'tn': (((0,), (0,)), ((), ()))}[mode]

    def lhs_spec(roff, coff, kb):
        kb = tk if kb is None else kb
        if mode == 'tn':
            return pl.BlockSpec((kb, tm), lambda j, i, k: (k + roff, i + coff))
        return pl.BlockSpec((tm, kb), lambda j, i, k: (i + roff, k + coff))

    def rhs_spec(roff, coff, kb):
        kb = tk if kb is None else kb
        if mode == 'nt':
            return pl.BlockSpec((tn, kb), lambda j, i, k: (j + roff, k + coff))
        return pl.BlockSpec((kb, tn), lambda j, i, k: (k + roff, j + coff))

    in_specs = [lhs_spec(*a[1:]) for a in lhs] + [rhs_spec(*a[1:]) for a in rhs]
    in_specs += [pl.BlockSpec((tm, tn), lambda j, i, k: (i, j)) for _ in tile_extras]
    in_specs += [pl.BlockSpec((1, tn), lambda j, i, k: (0, j)) for _ in row_extras]
    in_specs += [pl.BlockSpec((tm, a.shape[1]), lambda j, i, k: (i, 0)) for a in mrow_extras]
    n_out = len(out_dtypes)
    out_shape = [jax.ShapeDtypeStruct((M, N), d) for d in out_dtypes]
    out_specs = [pl.BlockSpec((tm, tn), lambda j, i, k: (i, j)) for _ in out_dtypes]
    out_shape += [jax.ShapeDtypeStruct((1, N), F32) for _ in range(n_colsum)]
    out_specs += [pl.BlockSpec((1, tn), lambda j, i, k: (0, j)) for _ in range(n_colsum)]
    scratch = [pltpu.VMEM((tm, tn), F32) for _ in range(n_acc)] if gk > 1 else []
    nl, nr, nt, nrw, nm = len(lhs), len(rhs), len(tile_extras), len(row_extras), len(mrow_extras)

    def body(*refs):
        pos = 0
        lhs_refs = refs[pos:pos + nl]; pos += nl
        rhs_refs = refs[pos:pos + nr]; pos += nr
        tile_refs = refs[pos:pos + nt]; pos += nt
        row_refs = refs[pos:pos + nrw]; pos += nrw
        mrow_refs = refs[pos:pos + nm]; pos += nm
        out_refs = refs[pos:pos + n_out]; pos += n_out
        cs_refs = refs[pos:pos + n_colsum]; pos += n_colsum
        acc_refs = refs[pos:]
        i = pl.program_id(1)
        k = pl.program_id(2)

        def partials():
            res = [None] * n_acc
            for (li, ri, ai) in prods:
                d = lax.dot_general(_mx(lhs_refs[li][...]), _mx(rhs_refs[ri][...]), dims,
                                    preferred_element_type=F32)
                res[ai] = d if res[ai] is None else res[ai] + d
            return res

        def finish(accs):
            outs = epilogue(accs, [r[...] for r in tile_refs], [r[...] for r in row_refs],
                            [r[...] for r in mrow_refs])
            for r, o in zip(out_refs, outs[:n_out]):
                r[...] = o.astype(r.dtype)
            for r, c in zip(cs_refs, outs[n_out:]):
                c = jnp.sum(c, axis=0, keepdims=True)

                @pl.when(i == 0)
                def _():
                    r[...] = c

                @pl.when(i != 0)
                def _():
                    r[...] += c

        if gk == 1:
            finish(partials())
        else:
            p = partials()

            @pl.when(k == 0)
            def _():
                for r, v in zip(acc_refs, p):
                    r[...] = v

            @pl.when(k != 0)
            def _():
                for r, v in zip(acc_refs, p):
                    r[...] += v

            @pl.when(k == gk - 1)
            def _():
                finish([r[...] for r in acc_refs])

    res = pl.pallas_call(
        body, name=name, grid=(gj, gi, gk), in_specs=in_specs, out_specs=out_specs, out_shape=out_shape,
        scratch_shapes=scratch, compiler_params=_params(("arbitrary", "arbitrary", "arbitrary")),
    )(*[a[0] for a in lhs], *[a[0] for a in rhs], *tile_extras, *row_extras, *mrow_extras)
    return res


def _op(a, roff=0, coff=0, kb=None):
    return (a, roff, coff, kb)


def _ident_epi(scale=None):
    def epi(accs, tiles, rows, mrows):
        return [a if scale is None else a * scale for a in accs]
    return epi


def _resid_norm_epi(scale):
    def epi(accs, tiles, rows, mrows):
        x_new = tiles[0] + scale * accs[0]
        xhat, _ = _rms_stats(x_new)
        return [x_new, xhat * rows[0]]
    return epi


def _norm_bwd_epi(accs, tiles, rows, mrows):
    dx_n, dg = _rms_bwd(tiles[0], rows[0], accs[0])
    dx = tiles[1] + dx_n
    return [dx, dx, dg]


def _rmsnorm_call(name, x, g):
    S, D = x.shape
    tm = _pick(S, _ROW_TILE, 128)

    def body(x_ref, g_ref, h_ref, ht_ref):
        h = _rms_stats(x_ref[...])[0] * g_ref[...]
        h_ref[...] = h.astype(h_ref.dtype)
        ht_ref[...] = jnp.transpose(h).astype(ht_ref.dtype)

    return pl.pallas_call(
        body, name=name, grid=(S // tm,),
        in_specs=[pl.BlockSpec((tm, D), lambda i: (i, 0)), pl.BlockSpec((1, D), lambda i: (0, 0))],
        out_specs=[pl.BlockSpec((tm, D), lambda i: (i, 0)), pl.BlockSpec((D, tm), lambda i: (0, i))],
        out_shape=[jax.ShapeDtypeStruct((S, D), BF16), jax.ShapeDtypeStruct((D, S), BF16)],
        compiler_params=_params(("arbitrary",)),
    )(x, g)


def _loss_call(x, target, g):
    S, D = x.shape
    tm = _pick(S, _ROW_TILE, 8)

    def body(x_ref, t_ref, g_ref, dx_ref, dxb_ref, dg_ref, loss_ref):
        i = pl.program_id(0)
        x_t = x_ref[...]
        gain = g_ref[...]
        xhat, _ = _rms_stats(x_t)
        diff = xhat * gain - t_ref[...]
        dy = diff * (1.0 / D)
        dx, dg = _rms_bwd(x_t, gain, dy)
        dx_ref[...] = dx
        dxb_ref[...] = dx.astype(BF16)
        dg = jnp.sum(dg, axis=0, keepdims=True)
        part = 0.5 * jnp.sum(jnp.sum(diff * diff, axis=1, keepdims=True), axis=0, keepdims=True) * (1.0 / D)
        part = jnp.broadcast_to(part, (1, _LANES))

        @pl.when(i == 0)
        def _():
            dg_ref[...] = dg
            loss_ref[...] = part

        @pl.when(i != 0)
        def _():
            dg_ref[...] += dg
            loss_ref[...] += part

    row = lambda i: (i, 0)
    fixed = lambda i: (0, 0)
    return pl.pallas_call(
        body, name="loss_head", grid=(S // tm,),
        in_specs=[pl.BlockSpec((tm, D), row), pl.BlockSpec((tm, D), row), pl.BlockSpec((1, D), fixed)],
        out_specs=[pl.BlockSpec((tm, D), row), pl.BlockSpec((tm, D), row), pl.BlockSpec((1, D), fixed),
                   pl.BlockSpec((1, _LANES), fixed)],
        out_shape=[jax.ShapeDtypeStruct((S, D), F32), jax.ShapeDtypeStruct((S, D), BF16),
                   jax.ShapeDtypeStruct((1, D), F32), jax.ShapeDtypeStruct((1, _LANES), F32)],
        compiler_params=_params(("arbitrary",)),
    )(x, target, g)


def _rope_tables_call(pos_col, invf, mask_a, mask_b):
    S = pos_col.shape[0]
    tm = _pick(S, _ROW_TILE, 8)

    def body(p_ref, f_ref, a_ref, b_ref, cos_ref, sa_ref, sb_ref):
        ang = p_ref[...].astype(F32) * f_ref[...]
        sn = jnp.sin(ang)
        cos_ref[...] = jnp.cos(ang)
        sa_ref[...] = sn * a_ref[...]
        sb_ref[...] = sn * b_ref[...]

    row = lambda i: (i, 0)
    fixed = lambda i: (0, 0)
    return pl.pallas_call(
        body, name="rope_tables", grid=(S // tm,),
        in_specs=[pl.BlockSpec((tm, 1), row)] + [pl.BlockSpec((1, HP), fixed)] * 3,
        out_specs=[pl.BlockSpec((tm, HP), row)] * 3,
        out_shape=[jax.ShapeDtypeStruct((S, HP), F32)] * 3, compiler_params=_params(("arbitrary",)),
    )(pos_col, invf, mask_a, mask_b)


def _rope(t, cos, sa, sb):
    return t * cos + pltpu.roll(t, HP - ROPE // 2, 1) * sa + pltpu.roll(t, ROPE // 2, 1) * sb


def _rope_t(d, cos, sa, sb):
    return d * cos + pltpu.roll(d * sa, ROPE // 2, 1) + pltpu.roll(d * sb, HP - ROPE // 2, 1)


def _gelu(z):
    return 0.5 * z * (1.0 + lax.erf(z * np.float32(1.0 / np.sqrt(2.0))))


def _gelu_grad(z):
    cdf = 0.5 * (1.0 + lax.erf(z * np.float32(1.0 / np.sqrt(2.0))))
    pdf = np.float32(1.0 / np.sqrt(2.0 * np.pi)) * jnp.exp(-0.5 * z * z)
    return cdf + z * pdf


_CQ0, _CKV0, _KR0, _Z0 = 0, Q_LORA, Q_LORA + KV_LORA, Q_LORA + KV_LORA + HP


def _even_prep_call(proj, qn, kvn, sgn, cos, sa, sb):
    S = proj.shape[0]
    tm = _pick(S, 256, 8)

    def body(p_ref, qn_ref, kvn_ref, sgn_ref, cos_ref, sa_ref, sb_ref, cq_ref, ckv_ref, kr_ref, u_ref, v_ref):
        cq = p_ref[:, _CQ0:_CQ0 + Q_LORA]
        cq_ref[...] = (_rms_stats(cq)[0] * qn_ref[...]).astype(BF16)
        ckv = p_ref[:, _CKV0:_CKV0 + KV_LORA]
        ckv_ref[...] = (_rms_stats(ckv)[0] * kvn_ref[...]).astype(BF16)
        kr = p_ref[:, _KR0:_KR0 + HP]
        kr_ref[...] = _rope(kr, cos_ref[...], sa_ref[...], sb_ref[...]).astype(BF16)
        u_ref[...] = _gelu(p_ref[:, _Z0:_Z0 + SG_WIDTH]).astype(BF16)
        zv = _gelu(p_ref[:, _Z0 + SG_WIDTH:_Z0 + 2 * SG_WIDTH])
        v_ref[...] = (_rms_stats(zv)[0] * sgn_ref[...]).astype(BF16)

    row = lambda i: (i, 0)
    fixed = lambda i: (0, 0)
    widths = [Q_LORA, KV_LORA, HP, SG_WIDTH, SG_WIDTH]
    return pl.pallas_call(
        body, name="even_prep", grid=(S // tm,),
        in_specs=[pl.BlockSpec((tm, PROJ_W), row), pl.BlockSpec((1, Q_LORA), fixed),
                  pl.BlockSpec((1, KV_LORA), fixed), pl.BlockSpec((1, SG_WIDTH), fixed)]
        + [pl.BlockSpec((tm, HP), row)] * 3,
        out_specs=[pl.BlockSpec((tm, w), row) for w in widths],
        out_shape=[jax.ShapeDtypeStruct((S, w), BF16) for w in widths],
        compiler_params=_params(("arbitrary",)),
    )(proj, qn, kvn, sgn, cos, sa, sb)


def _even_prep_bwd_call(proj, qn, kvn, sgn, cos, sa, sb, dcqn, dckvn, dk, du, dvn):
    S = proj.shape[0]
    tm = _pick(S, 256, 8)

    def body(p_ref, qn_ref, kvn_ref, sgn_ref, cos_ref, sa_ref, sb_ref, dcq_ref, dckv_ref, dk_ref, du_ref,
             dvn_ref, dp_ref, dqn_ref, dkvn_ref, dsgn_ref):
        i = pl.program_id(0)
        dcq, gq = _rms_bwd(p_ref[:, _CQ0:_CQ0 + Q_LORA], qn_ref[...], dcq_ref[...])
        dp_ref[:, _CQ0:_CQ0 + Q_LORA] = dcq.astype(BF16)
        dckv, gkv = _rms_bwd(p_ref[:, _CKV0:_CKV0 + KV_LORA], kvn_ref[...], dckv_ref[...])
        dp_ref[:, _CKV0:_CKV0 + KV_LORA] = dckv.astype(BF16)
        dkr = dk_ref[:, 0:HP].astype(F32)
        for h in range(1, HEADS):
            dkr = dkr + dk_ref[:, h * HP:(h + 1) * HP].astype(F32)
        lane = lax.broadcasted_iota(jnp.int32, dkr.shape, 1)
        dkr = jnp.where((lane >= NOPE) & (lane < QK_DIM), dkr, 0.0)
        dp_ref[:, _KR0:_KR0 + HP] = _rope_t(dkr, cos_ref[...], sa_ref[...], sb_ref[...]).astype(BF16)
        zu = p_ref[:, _Z0:_Z0 + SG_WIDTH]
        dp_ref[:, _Z0:_Z0 + SG_WIDTH] = (du_ref[...].astype(F32) * _gelu_grad(zu)).astype(BF16)
        zv = p_ref[:, _Z0 + SG_WIDTH:_Z0 + 2 * SG_WIDTH]
        dgv, gsg = _rms_bwd(_gelu(zv), sgn_ref[...], dvn_ref[...].astype(F32))
        dp_ref[:, _Z0 + SG_WIDTH:_Z0 + 2 * SG_WIDTH] = (dgv * _gelu_grad(zv)).astype(BF16)
        sums = [jnp.sum(t, axis=0, keepdims=True) for t in (gq, gkv, gsg)]

        @pl.when(i == 0)
        def _():
            for r, s in zip((dqn_ref, dkvn_ref, dsgn_ref), sums):
                r[...] = s

        @pl.when(i != 0)
        def _():
            for r, s in zip((dqn_ref, dkvn_ref, dsgn_ref), sums):
                r[...] += s

    row = lambda i: (i, 0)
    fixed = lambda i: (0, 0)
    return pl.pallas_call(
        body, name="even_prep_bwd", grid=(S // tm,),
        in_specs=[pl.BlockSpec((tm, PROJ_W), row), pl.BlockSpec((1, Q_LORA), fixed),
                  pl.BlockSpec((1, KV_LORA), fixed), pl.BlockSpec((1, SG_WIDTH), fixed)]
        + [pl.BlockSpec((tm, HP), row)] * 3
        + [pl.BlockSpec((tm, Q_LORA), row), pl.BlockSpec((tm, KV_LORA), row),
           pl.BlockSpec((tm, HEADS * HP), row), pl.BlockSpec((tm, SG_WIDTH), row),
           pl.BlockSpec((tm, SG_WIDTH), row)],
        out_specs=[pl.BlockSpec((tm, PROJ_W), row), pl.BlockSpec((1, Q_LORA), fixed),
                   pl.BlockSpec((1, KV_LORA), fixed), pl.BlockSpec((1, SG_WIDTH), fixed)],
        out_shape=[jax.ShapeDtypeStruct((S, PROJ_W), BF16), jax.ShapeDtypeStruct((1, Q_LORA), F32),
                   jax.ShapeDtypeStruct((1, KV_LORA), F32), jax.ShapeDtypeStruct((1, SG_WIDTH), F32)],
        compiler_params=_params(("arbitrary",)),
    )(proj, qn, kvn, sgn, cos, sa, sb, dcqn, dckvn, dk, du, dvn)


def _causal_mask(rows, cols):
    r = lax.broadcasted_iota(jnp.int32, (rows, cols), 0)
    c = lax.broadcasted_iota(jnp.int32, (rows, cols), 1)
    return c <= r


def _flash_fwd_call(q, k, v, carry=()):
    S = q.shape[0]
    tb = _pick(S, _ATT_BLOCK, 128)
    nq = S // tb
    nt_dims = (((1,), (1,)), ((), ()))

    nc = len(carry)

    def body(*refs):
        q_ref, k_ref, v_ref = refs[:3]
        o_ref, lse_ref = refs[3 + nc:5 + nc]
        s_a, s_b, m_ref, acc_ref = refs[5 + 2 * nc:9 + 2 * nc]
        h = pl.program_id(0)
        i = pl.program_id(1)
        if nc:
            send, forward, finish = _gather_phases(refs[3:3 + nc], refs[5 + nc:5 + 2 * nc], *refs[9 + 2 * nc:])
            pl.when((h == 0) & (i == 0))(send)
            pl.when((h == HEADS // 2) & (i == 0))(forward)

        def scores(buf, j):
            k_t = k_ref[pl.ds(pl.multiple_of(j * tb, tb), tb), :]
            buf[...] = lax.dot_general(q_ref[...], k_t, nt_dims, preferred_element_type=F32)

        def update(buf, j, masked):
            v_t = v_ref[pl.ds(pl.multiple_of(j * tb, tb), tb), :]
            s = buf[...]
            if masked:
                s = jnp.where(_causal_mask(tb, tb), s, -1e30)
            m = m_ref[...]
            m_new = jnp.maximum(m, jnp.max(s, axis=1, keepdims=True))
            alpha = jnp.exp(m - m_new)
            p = jnp.exp(s - m_new)
            acc_ref[...] = alpha * acc_ref[...] + jnp.dot(p.astype(v_t.dtype), v_t, preferred_element_type=F32)
            m_ref[...] = m_new

        m_ref[...] = jnp.full((tb, 1), -1e30, F32)
        acc_ref[...] = jnp.zeros((tb, HP), F32)
        scores(s_a, 0)
        pairs = i // 2

        def two_blocks(t, carry):
            scores(s_b, 2 * t + 1)
            update(s_a, 2 * t, False)
            scores(s_a, 2 * t + 2)
            update(s_b, 2 * t + 1, False)
            return carry

        lax.fori_loop(0, pairs, two_blocks, 0)

        @pl.when(2 * pairs == i)
        def _():
            update(s_a, i, True)

        @pl.when(2 * pairs != i)
        def _():
            scores(s_b, i)
            update(s_a, i - 1, False)
            update(s_b, i, True)

        acc = acc_ref[...]
        l = acc[:, VDIM:VDIM + 1]
        lane = lax.broadcasted_iota(jnp.int32, (tb, HP), 1)
        o_ref[...] = jnp.where(lane < VDIM, acc / l, 0.0).astype(o_ref.dtype)
        lse = jnp.broadcast_to(m_ref[...] + jnp.log(l), (tb, HP))
        lse_ref[0, 0] = jnp.transpose(lse)[0:8, :]
        if nc:
            pl.when((h == HEADS - 1) & (i == nq - 1))(finish)

    start = _gather_start(carry)
    any_spec = pl.BlockSpec(memory_space=pl.ANY)
    res = pl.pallas_call(
        body, name="flash_fwd_gather" if nc else "flash_fwd", grid=(HEADS, nq),
        in_specs=[pl.BlockSpec((tb, HP), lambda h, i: (i, h)), pl.BlockSpec((S, HP), lambda h, i: (0, h)),
                  pl.BlockSpec((S, HP), lambda h, i: (0, h))] + [any_spec] * nc,
        out_specs=[pl.BlockSpec((tb, HP), lambda h, i: (i, h)),
                   pl.BlockSpec((1, 1, 8, tb), lambda h, i: (h, i, 0, 0))] + [any_spec] * nc,
        out_shape=[jax.ShapeDtypeStruct((S, HEADS * HP), q.dtype), jax.ShapeDtypeStruct((HEADS, nq, 8, tb), F32)]
        + [jax.ShapeDtypeStruct(t.shape, t.dtype) for t in start],
        scratch_shapes=[pltpu.VMEM((tb, tb), F32), pltpu.VMEM((tb, tb), F32), pltpu.VMEM((tb, 1), F32),
                        pltpu.VMEM((tb, HP), F32)] + ([pltpu.SemaphoreType.DMA((n,)) for n in _gather_sems(nc)]
                                                      if nc else []),
        input_output_aliases={3 + a: 2 + a for a in range(nc)},
        compiler_params=pltpu.CompilerParams(vmem_limit_bytes=_VMEM_LIMIT, has_side_effects=bool(nc),
                                             dimension_semantics=("arbitrary", "arbitrary")),
    )(q, k, v, *start)
    return res[0], res[1], list(res[2:])


def _attn_delta_call(o, do):
    S = o.shape[0]
    tb = _pick(S, _ATT_BLOCK, 128)
    nq = S // tb
    nb = _pick(nq, 4, 1)

    def body(o_ref, do_ref, d_ref):
        for r in range(nb):
            rows = slice(r * tb, (r + 1) * tb)
            d = jnp.sum(o_ref[rows, :].astype(F32) * do_ref[rows, :].astype(F32), axis=1, keepdims=True)
            d_ref[0, r] = jnp.transpose(jnp.broadcast_to(d, (tb, HP)))[0:8, :]

    return pl.pallas_call(
        body, name="attn_delta", grid=(HEADS, nq // nb),
        in_specs=[pl.BlockSpec((nb * tb, HP), lambda h, i: (i, h))] * 2,
        out_specs=pl.BlockSpec((1, nb, 8, tb), lambda h, i: (h, i, 0, 0)),
        out_shape=jax.ShapeDtypeStruct((HEADS, nq, 8, tb), F32), compiler_params=_params(("arbitrary", "arbitrary")),
    )(o, do)


def _flash_bwd_call(q, k, v, do, lse, delta, carry=()):
    S = q.shape[0]
    tb = _pick(S, _ATT_BLOCK, 128)
    nq = S // tb
    nt_dims = (((1,), (1,)), ((), ()))
    tn_dims = (((0,), (0,)), ((), ()))
    nc = len(carry)

    def body(*refs):
        q_ref, do_ref, lse_ref, dl_ref, k_ref, v_ref = refs[:6]
        dq_ref, dk_ref, dv_ref = refs[6 + nc:9 + nc]
        st_a, dp_a, st_b, dp_b, dk_acc, dv_acc = refs[9 + 2 * nc:15 + 2 * nc]
        h = pl.program_id(0)
        j = pl.program_id(1)
        if nc:
            send, finish = _scatter_phases(refs[6:6 + nc], refs[9 + nc:9 + 2 * nc], *refs[15 + 2 * nc:])
            pl.when((h == 0) & (j == 0))(send)

        @pl.when(j == 0)
        def _():
            dq_ref[...] = jnp.zeros_like(dq_ref)

        def rows_of(i):
            return pl.ds(pl.multiple_of(i * tb, tb), tb)

        def scores(st_buf, dp_buf, i):
            st_buf[...] = lax.dot_general(k_ref[...], q_ref[rows_of(i), :], nt_dims, preferred_element_type=F32)
            dp_buf[...] = lax.dot_general(v_ref[...], do_ref[rows_of(i), :], nt_dims, preferred_element_type=F32)

        def update(st_buf, dp_buf, i, masked):
            q_t = q_ref[rows_of(i), :]
            do_t = do_ref[rows_of(i), :]
            pt = jnp.exp(st_buf[...] - lse_ref[0, i, 0:1, :])
            if masked:
                pt = jnp.where(jnp.transpose(_causal_mask(tb, tb)), pt, 0.0)
            dst = (pt * (dp_buf[...] - dl_ref[0, i, 0:1, :])).astype(q_t.dtype)
            dv_acc[...] += jnp.dot(pt.astype(do_t.dtype), do_t, preferred_element_type=F32)
            dk_acc[...] += jnp.dot(dst, q_t, preferred_element_type=F32)
            dq_ref[rows_of(i), :] += lax.dot_general(dst, k_ref[...], tn_dims, preferred_element_type=F32)

        last = nq - 1
        dk_acc[...] = jnp.zeros((tb, HP), F32)
        dv_acc[...] = jnp.zeros((tb, HP), F32)
        scores(st_b, dp_b, j)
        scores(st_a, dp_a, jnp.minimum(j + 1, last))
        update(st_b, dp_b, j, True)
        rest = last - j
        pairs = rest // 2

        def two_blocks(t, carry):
            i0 = j + 1 + 2 * t
            scores(st_b, dp_b, i0 + 1)
            update(st_a, dp_a, i0, False)
            scores(st_a, dp_a, jnp.minimum(i0 + 2, last))
            update(st_b, dp_b, i0 + 1, False)
            return carry

        lax.fori_loop(0, pairs, two_blocks, 0)

        @pl.when(2 * pairs != rest)
        def _():
            update(st_a, dp_a, last, False)

        dk_ref[...] = dk_acc[...].astype(dk_ref.dtype)
        dv_ref[...] = dv_acc[...].astype(dv_ref.dtype)
        if nc:
            pl.when((h == HEADS - 1) & (j == nq - 1))(finish)

    head = lambda h, j: (0, h)
    blk = lambda h, j: (j, h)
    rows = lambda h, j: (h, 0, 0, 0)
    any_spec = pl.BlockSpec(memory_space=pl.ANY)
    res = pl.pallas_call(
        body, name="flash_bwd_scatter" if nc else "flash_bwd", grid=(HEADS, nq),
        in_specs=[pl.BlockSpec((S, HP), head), pl.BlockSpec((S, HP), head), pl.BlockSpec((1, nq, 8, tb), rows),
                  pl.BlockSpec((1, nq, 8, tb), rows), pl.BlockSpec((tb, HP), blk), pl.BlockSpec((tb, HP), blk)]
        + [any_spec] * nc,
        out_specs=[pl.BlockSpec((S, HP), head), pl.BlockSpec((tb, HP), blk), pl.BlockSpec((tb, HP), blk)]
        + [any_spec] * nc,
        out_shape=[jax.ShapeDtypeStruct((S, HEADS * HP), F32), jax.ShapeDtypeStruct((S, HEADS * HP), BF16),
                   jax.ShapeDtypeStruct((S, HEADS * HP), BF16)]
        + [jax.ShapeDtypeStruct(p.shape, p.dtype) for p in carry],
        scratch_shapes=[pltpu.VMEM((tb, tb), F32)] * 4 + [pltpu.VMEM((tb, HP), F32)] * 2
        + ([pltpu.SemaphoreType.DMA((n,)) for n in _scatter_sems(nc)] if nc else []),
        compiler_params=pltpu.CompilerParams(vmem_limit_bytes=_VMEM_LIMIT, has_side_effects=bool(nc),
                                             dimension_semantics=("arbitrary", "arbitrary")),
    )(q, do, lse, delta, k, v, *carry)
    return res[0], res[1], res[2], list(res[3:])


def _sg_mixed(w_ref, vch, lane_lo):
    blocks = []
    for jb in range(SG_WIDTH // _LANES):
        r = jnp.dot(w_ref[jb], vch[:, jb * _LANES:(jb + 1) * _LANES], preferred_element_type=F32)
        blocks.append(jnp.where(lane_lo, r[0:SG_CHUNK], r[SG_CHUNK:2 * SG_CHUNK]))
    return jnp.concatenate(blocks, axis=1)


def _sgu_fwd_call(vn, u, attn, wst, bexp):
    S = vn.shape[0]
    tm = _pick(S, _SG_TILE, SG_CHUNK)
    AW = HEADS * HP

    def body(v_ref, u_ref, a_ref, w_ref, b_ref, mix_ref):
        lane_lo = lax.broadcasted_iota(jnp.int32, (SG_CHUNK, _LANES), 1) < SG_GDIM
        mix_ref[:, 0:AW] = a_ref[...]
        for c in range(tm // SG_CHUNK):
            rs = slice(c * SG_CHUNK, (c + 1) * SG_CHUNK)
            mixed = _sg_mixed(w_ref, v_ref[rs, :], lane_lo) + b_ref[...]
            mix_ref[rs, AW:AW + SG_WIDTH] = (u_ref[rs, :].astype(F32) * mixed).astype(mix_ref.dtype)

    row = lambda i: (i, 0)
    return pl.pallas_call(
        body, name="sgu_fwd", grid=(S // tm,),
        in_specs=[pl.BlockSpec((tm, SG_WIDTH), row), pl.BlockSpec((tm, SG_WIDTH), row), pl.BlockSpec((tm, AW), row),
                  pl.BlockSpec((SG_WIDTH // _LANES, 2 * SG_CHUNK, SG_CHUNK), lambda i: (0, 0, 0)),
                  pl.BlockSpec((SG_CHUNK, SG_WIDTH), lambda i: (0, 0))],
        out_specs=pl.BlockSpec((tm, AW + SG_WIDTH), row),
        out_shape=jax.ShapeDtypeStruct((S, AW + SG_WIDTH), BF16), compiler_params=_params(("arbitrary",)),
    )(vn, u, attn, wst, bexp)


def _sgu_bwd_call(dmix, vn, u, wst, wst_t, bexp):
    S = vn.shape[0]
    tm = _pick(S, _SG_TILE, SG_CHUNK)
    nblk = SG_WIDTH // _LANES
    col0 = (HEADS * HP) // SG_WIDTH
    nt_dims = (((1,), (1,)), ((), ()))

    def body(d_ref, v_ref, u_ref, w_ref, wt_ref, b_ref, du_ref, dv_ref, dw_ref, db_ref, dbacc_ref):
        i = pl.program_id(0)
        lane_lo = lax.broadcasted_iota(jnp.int32, (SG_CHUNK, _LANES), 1) < SG_GDIM

        @pl.when(i == 0)
        def _():
            dw_ref[...] = jnp.zeros_like(dw_ref)
            dbacc_ref[...] = jnp.zeros_like(dbacc_ref)

        for c in range(tm // SG_CHUNK):
            rs = slice(c * SG_CHUNK, (c + 1) * SG_CHUNK)
            vch = v_ref[rs, :]
            dsg = d_ref[rs, :].astype(F32)
            mixed = _sg_mixed(w_ref, vch, lane_lo) + b_ref[...]
            du_ref[rs, :] = (dsg * mixed).astype(du_ref.dtype)
            dmixed = dsg * u_ref[rs, :].astype(F32)
            dbacc_ref[...] += dmixed
            dmx = dmixed.astype(vch.dtype)
            dv_ref[rs, :] = _sg_mixed(wt_ref, dmx, lane_lo).astype(dv_ref.dtype)
            for jb in range(nblk):
                dblk = dmx[:, jb * _LANES:(jb + 1) * _LANES]
                vblk = vch[:, jb * _LANES:(jb + 1) * _LANES]
                zero = jnp.zeros_like(dblk)
                dw_ref[2 * jb] += lax.dot_general(jnp.where(lane_lo, dblk, zero), vblk, nt_dims,
                                                  preferred_element_type=F32)
                dw_ref[2 * jb + 1] += lax.dot_general(jnp.where(lane_lo, zero, dblk), vblk, nt_dims,
                                                      preferred_element_type=F32)

        @pl.when(i == pl.num_programs(0) - 1)
        def _():
            tri = _causal_mask(SG_CHUNK, SG_CHUNK)
            for g in range(SG_GROUPS):
                dw_ref[g] = jnp.where(tri, dw_ref[g], 0.0)
            lane = lax.broadcasted_iota(jnp.int32, (SG_CHUNK, _LANES), 1)
            out = jnp.zeros((SG_CHUNK, _LANES), F32)
            for g in range(SG_GROUPS):
                blk = dbacc_ref[:, (g // 2) * _LANES:(g // 2 + 1) * _LANES]
                sel = lane_lo if g % 2 == 0 else jnp.logical_not(lane_lo)
                s = jnp.sum(jnp.where(sel, blk, 0.0), axis=1, keepdims=True)
                out = jnp.where(lane == g, s, out)
            db_ref[...] = out

    row = lambda i: (i, 0)
    wspec = pl.BlockSpec((nblk, 2 * SG_CHUNK, SG_CHUNK), lambda i: (0, 0, 0))
    return pl.pallas_call(
        body, name="sgu_bwd", grid=(S // tm,),
        in_specs=[pl.BlockSpec((tm, SG_WIDTH), lambda i: (i, col0)), pl.BlockSpec((tm, SG_WIDTH), row),
                  pl.BlockSpec((tm, SG_WIDTH), row), wspec, wspec,
                  pl.BlockSpec((SG_CHUNK, SG_WIDTH), lambda i: (0, 0))],
        out_specs=[pl.BlockSpec((tm, SG_WIDTH), row), pl.BlockSpec((tm, SG_WIDTH), row),
                   pl.BlockSpec((SG_GROUPS, SG_CHUNK, SG_CHUNK), lambda i: (0, 0, 0)),
                   pl.BlockSpec((SG_CHUNK, _LANES), lambda i: (0, 0))],
        out_shape=[jax.ShapeDtypeStruct((S, SG_WIDTH), BF16), jax.ShapeDtypeStruct((S, SG_WIDTH), BF16),
                   jax.ShapeDtypeStruct((SG_GROUPS, SG_CHUNK, SG_CHUNK), F32),
                   jax.ShapeDtypeStruct((SG_CHUNK, _LANES), F32)],
        scratch_shapes=[pltpu.VMEM((SG_CHUNK, SG_WIDTH), F32)],
        compiler_params=_params(("arbitrary",)),
    )(dmix, vn, u, wst, wst_t, bexp)


def _shift_down(t, halo, n):
    rows = lax.broadcasted_iota(jnp.int32, t.shape, 0)
    out = pltpu.roll(t, n, 0)
    for r in range(n):
        out = jnp.where(rows == r, halo[8 - n + r:8 - n + r + 1, :], out)
    return out


def _shift_up(t, halo, n):
    tm = t.shape[0]
    rows = lax.broadcasted_iota(jnp.int32, t.shape, 0)
    out = pltpu.roll(t, tm - n, 0)
    for r in range(n):
        out = jnp.where(rows == tm - n + r, halo[r:r + 1, :], out)
    return out


def _conv_fwd_call(p, w):
    S, C3 = p.shape
    C = C3 // 3
    tm = _pick(S, _ROW_TILE, 8)
    hb = tm // 8

    def body(p_ref, c_prev, z_prev, w_ref, m_ref):
        i = pl.program_id(0)
        cz = p_ref[:, C:2 * C] * p_ref[:, 2 * C:3 * C]
        czp = jnp.where(i > 0, c_prev[...] * z_prev[...], 0.0)
        y = w_ref[2:3, :] * cz + w_ref[1:2, :] * _shift_down(cz, czp, 1) + w_ref[0:1, :] * _shift_down(cz, czp, 2)
        m_ref[...] = (p_ref[:, 0:C] * y).astype(m_ref.dtype)

    prev = lambda col: (lambda i: (jnp.maximum(i * hb - 1, 0), col))
    return pl.pallas_call(
        body, name="conv_fwd", grid=(S // tm,),
        in_specs=[pl.BlockSpec((tm, C3), lambda i: (i, 0)), pl.BlockSpec((8, C), prev(1)),
                  pl.BlockSpec((8, C), prev(2)), pl.BlockSpec((3, C), lambda i: (0, 0))],
        out_specs=pl.BlockSpec((tm, C), lambda i: (i, 0)),
        out_shape=jax.ShapeDtypeStruct((S, C), BF16), compiler_params=_params(("arbitrary",)),
    )(p, p, p, w)


def _conv_bwd_call(p, w, dm):
    S, C3 = p.shape
    C = C3 // 3
    tm = _pick(S, 256, 8)
    hb = tm // 8
    n_tiles = S // tm

    def body(p_ref, c_prev, z_prev, b_next, dm_ref, dm_next, w_ref, dp_ref, dw_ref):
        i = pl.program_id(0)
        b = p_ref[:, 0:C]
        c = p_ref[:, C:2 * C]
        z = p_ref[:, 2 * C:3 * C]
        cz = c * z
        czp = jnp.where(i > 0, c_prev[...] * z_prev[...], 0.0)
        s1 = _shift_down(cz, czp, 1)
        s2 = _shift_down(cz, czp, 2)
        w0, w1, w2 = w_ref[0:1, :], w_ref[1:2, :], w_ref[2:3, :]
        y = w2 * cz + w1 * s1 + w0 * s2
        dm_t = dm_ref[...]
        dy = dm_t * b
        dyn = jnp.where(i < n_tiles - 1, dm_next[...] * b_next[...], 0.0)
        dcz = w2 * dy + w1 * _shift_up(dy, dyn, 1) + w0 * _shift_up(dy, dyn, 2)
        dp_ref[:, 0:C] = (dm_t * y).astype(dp_ref.dtype)
        dp_ref[:, C:2 * C] = (dcz * z).astype(dp_ref.dtype)
        dp_ref[:, 2 * C:3 * C] = (dcz * c).astype(dp_ref.dtype)
        dw = jnp.concatenate([jnp.sum(dy * s2, axis=0, keepdims=True), jnp.sum(dy * s1, axis=0, keepdims=True),
                              jnp.sum(dy * cz, axis=0, keepdims=True)], axis=0)

        @pl.when(i == 0)
        def _():
            dw_ref[...] = dw

        @pl.when(i != 0)
        def _():
            dw_ref[...] += dw

    prev = lambda col: (lambda i: (jnp.maximum(i * hb - 1, 0), col))
    nxt = lambda col: (lambda i: (jnp.minimum((i + 1) * hb, S // 8 - 1), col))
    return pl.pallas_call(
        body, name="conv_bwd", grid=(n_tiles,),
        in_specs=[pl.BlockSpec((tm, C3), lambda i: (i, 0)), pl.BlockSpec((8, C), prev(1)),
                  pl.BlockSpec((8, C), prev(2)), pl.BlockSpec((8, C), nxt(0)),
                  pl.BlockSpec((tm, C), lambda i: (i, 0)), pl.BlockSpec((8, C), nxt(0)),
                  pl.BlockSpec((3, C), lambda i: (0, 0))],
        out_specs=[pl.BlockSpec((tm, C3), lambda i: (i, 0)), pl.BlockSpec((3, C), lambda i: (0, 0))],
        out_shape=[jax.ShapeDtypeStruct((S, C3), BF16), jax.ShapeDtypeStruct((3, C), F32)],
        compiler_params=_params(("arbitrary",)),
    )(p, p, p, p, dm, dm, w)


def _my_place():
    return lax.axis_index("x"), lax.axis_index("y"), lax.axis_index("c")


def _gather_weights_call(name, shard):
    R, C = shard.shape

    def body(s_ref, o_ref, send_sems, recv_sems, local_sem):
        x, y, c = _my_place()
        mine = 2 * x + y
        local = pltpu.make_async_copy(s_ref, o_ref.at[mine], local_sem)
        local.start()
        peers = [(1 - x, y), (x, 1 - y), (1 - x, 1 - y)]
        copies = []
        for k, (px, py) in enumerate(peers):
            cp = pltpu.make_async_remote_copy(src_ref=s_ref, dst_ref=o_ref.at[mine], send_sem=send_sems.at[k],
                                              recv_sem=recv_sems.at[k], device_id=(px, py, c), device_id_type=MESH)
            cp.start()
            copies.append(cp)
        for k, (px, py) in enumerate(peers):
            pltpu.make_async_remote_copy(src_ref=s_ref, dst_ref=o_ref.at[2 * px + py], send_sem=send_sems.at[k],
                                         recv_sem=recv_sems.at[k], device_id=(px, py, c),
                                         device_id_type=MESH).wait_recv()
        for cp in copies:
            cp.wait_send()
        local.wait()

    any_spec = pl.BlockSpec(memory_space=pl.ANY)
    return pl.pallas_call(
        body, name=name, in_specs=[any_spec], out_specs=any_spec,
        out_shape=jax.ShapeDtypeStruct((4, R, C), shard.dtype),
        scratch_shapes=[pltpu.SemaphoreType.DMA((3,)), pltpu.SemaphoreType.DMA((3,)), pltpu.SemaphoreType.DMA],
        compiler_params=pltpu.CompilerParams(has_side_effects=True),
    )(shard)


_D2D_CHUNKS = 4


_LOCAL_CHUNKS = 8


def _local_copies(src_of, dst_of, rows, sems, base):
    rc = rows // _LOCAL_CHUNKS
    assert rc * _LOCAL_CHUNKS == rows and rc % 16 == 0, rows
    out = []
    for j in range(_LOCAL_CHUNKS):
        sl = pl.ds(j * rc, rc)
        out.append(pltpu.make_async_copy(src_of(sl), dst_of(sl), sems.at[base + j]))
    return out


def _comm_call(name, body, arrays, out_shapes, sem_counts, aliases=None):
    any_spec = pl.BlockSpec(memory_space=pl.ANY)
    return pl.pallas_call(
        body, name=name, in_specs=[any_spec] * len(arrays), out_specs=[any_spec] * len(out_shapes),
        out_shape=out_shapes, scratch_shapes=[pltpu.SemaphoreType.DMA((n,)) for n in sem_counts],
        input_output_aliases=aliases or {}, compiler_params=pltpu.CompilerParams(has_side_effects=True),
    )(*arrays)


def _gather_halves_call(shards):
    na = len(shards)

    def body(*refs):
        send, forward, finish = _gather_phases(refs[:na], refs[na:2 * na], *refs[2 * na:])
        send()
        forward()
        finish()

    start = _gather_start(shards)
    outs = [jax.ShapeDtypeStruct(t.shape, t.dtype) for t in start]
    return _comm_call("gather_weights", body, start, outs, _gather_sems(na), aliases={a: a for a in range(na)})


def _gather_start(shards):
    for s in shards:
        assert s.shape[0] % (2 * _D2D_CHUNKS * 16) == 0, s.shape
    return [jnp.broadcast_to(s[None], (4,) + tuple(s.shape)) for s in shards]


def _gather_sems(na):
    return [3 * na, 3 * na, 3 * na * _D2D_CHUNKS, 3 * na * _D2D_CHUNKS]


def _gather_phases(s_refs, o_refs, ici_send, ici_recv, d2d_send, d2d_recv):
    na = len(s_refs)
    x, y, c = _my_place()
    mine = 2 * x + y
    chips = [(1 - x, y), (x, 1 - y), (1 - x, 1 - y)]

    def ici(a, k, chip, block):
        Rh = s_refs[a].shape[1] // 2
        my_half = pl.ds(pl.multiple_of(c * Rh, 16), Rh)
        return pltpu.make_async_remote_copy(src_ref=s_refs[a].at[mine, my_half], dst_ref=o_refs[a].at[block, my_half],
                                            send_sem=ici_send.at[3 * a + k], recv_sem=ici_recv.at[3 * a + k],
                                            device_id=(chip[0], chip[1], c), device_id_type=MESH)

    def d2d(a, k, j, block, half):
        Rh = s_refs[a].shape[1] // 2
        rc = Rh // _D2D_CHUNKS
        rows = pl.ds(pl.multiple_of(half * Rh + j * rc, 16), rc)
        idx = (3 * a + k) * _D2D_CHUNKS + j
        return pltpu.make_async_remote_copy(src_ref=o_refs[a].at[block, rows], dst_ref=o_refs[a].at[block, rows],
                                            send_sem=d2d_send.at[idx], recv_sem=d2d_recv.at[idx],
                                            device_id=(x, y, 1 - c), device_id_type=MESH)

    def send():
        for a in range(na):
            for k, chip in enumerate(chips):
                ici(a, k, chip, mine).start()

    def forward():
        for a in range(na):
            for k, chip in enumerate(chips):
                block = 2 * chip[0] + chip[1]
                ici(a, k, chip, block).wait_recv()
                for j in range(_D2D_CHUNKS):
                    d2d(a, k, j, block, c).start()

    def finish():
        for a in range(na):
            for k, chip in enumerate(chips):
                block = 2 * chip[0] + chip[1]
                for j in range(_D2D_CHUNKS):
                    d2d(a, k, j, block, 1 - c).wait_recv()
        for a in range(na):
            for k, chip in enumerate(chips):
                ici(a, k, chip, mine).wait_send()
                for j in range(_D2D_CHUNKS):
                    d2d(a, k, j, 2 * chip[0] + chip[1], c).wait_send()

    return send, forward, finish


def _pair_exchange_call(packed, tag):
    na = len(packed)

    def body(*refs):
        p_refs, o_refs = refs[:na], refs[na:2 * na]
        send_sems, recv_sems = refs[2 * na:]
        x, y, c = _my_place()
        copies = []
        for a in range(na):
            nb, _, Rh, _ = p_refs[a].shape
            rc = Rh // _D2D_CHUNKS
            assert rc * _D2D_CHUNKS == Rh and rc % 16 == 0
            for b in range(nb):
                for j in range(_D2D_CHUNKS):
                    rows = pl.ds(j * rc, rc)
                    idx = (a * nb + b) * _D2D_CHUNKS + j
                    copies.append(pltpu.make_async_remote_copy(
                        src_ref=p_refs[a].at[b, 1 - c, rows], dst_ref=o_refs[a].at[b, rows],
                        send_sem=send_sems.at[idx], recv_sem=recv_sems.at[idx],
                        device_id=(x, y, 1 - c), device_id_type=MESH))
        for t in copies:
            t.start()
        for t in copies:
            t.wait_recv()
        for t in copies:
            t.wait_send()

    outs = [jax.ShapeDtypeStruct((p.shape[0], p.shape[2], p.shape[3]), p.dtype) for p in packed]
    n = sum(p.shape[0] for p in packed) * _D2D_CHUNKS
    return _comm_call("pair_exchange_" + tag, body, packed, outs, [n, n])


def _pair_add_call(name, packed, other, core):
    nb, _, Rh, C = packed.shape
    tr = _pick(Rh, 512, 16)

    def body(c_ref, p_ref, o_ref, q_ref):
        q_ref[...] = (p_ref[...].astype(F32) + o_ref[...].astype(F32)).astype(q_ref.dtype)

    grid_spec = pltpu.PrefetchScalarGridSpec(
        num_scalar_prefetch=1, grid=(nb, Rh // tr),
        in_specs=[pl.BlockSpec((None, None, tr, C), lambda b, r, c_ref: (b, c_ref[0], r, 0)),
                  pl.BlockSpec((None, tr, C), lambda b, r, c_ref: (b, r, 0))],
        out_specs=pl.BlockSpec((None, tr, C), lambda b, r, c_ref: (b, r, 0)))
    return pl.pallas_call(
        body, name=name, grid_spec=grid_spec, out_shape=jax.ShapeDtypeStruct((nb, Rh, C), packed.dtype),
        compiler_params=_params(("arbitrary", "arbitrary")),
    )(core, packed, other)


def _chip_scatter_call(pairs):
    na = len(pairs)

    def body(*refs):
        send, finish = _scatter_phases(refs[:na], refs[na:2 * na], *refs[2 * na:])
        send()
        finish()

    outs = [jax.ShapeDtypeStruct(p.shape, p.dtype) for p in pairs]
    return _comm_call("chip_scatter", body, pairs, outs, _scatter_sems(na))


def _scatter_sems(na):
    return [3 * na, 3 * na, na * _LOCAL_CHUNKS]


def _scatter_phases(p_refs, o_refs, send_sems, recv_sems, local_sems):
    na = len(p_refs)
    x, y, c = _my_place()
    mine = 2 * x + y
    chips = [(1 - x, y), (x, 1 - y), (1 - x, 1 - y)]

    def local(a):
        p_ref, o_ref = p_refs[a], o_refs[a]
        return _local_copies(lambda sl: p_ref.at[mine, sl], lambda sl: o_ref.at[mine, sl], p_ref.shape[1],
                             local_sems, a * _LOCAL_CHUNKS)

    def remote(a, k, src_block, dst_block):
        px, py = chips[k]
        return pltpu.make_async_remote_copy(src_ref=p_refs[a].at[src_block], dst_ref=o_refs[a].at[dst_block],
                                            send_sem=send_sems.at[3 * a + k], recv_sem=recv_sems.at[3 * a + k],
                                            device_id=(px, py, c), device_id_type=MESH)

    def send():
        for a in range(na):
            for t in local(a):
                t.start()
            for k, (px, py) in enumerate(chips):
                remote(a, k, 2 * px + py, mine).start()

    def finish():
        for a in range(na):
            for k, (px, py) in enumerate(chips):
                remote(a, k, mine, 2 * px + py).wait_recv()
        for a in range(na):
            for k, (px, py) in enumerate(chips):
                remote(a, k, 2 * px + py, mine).wait_send()
            for t in local(a):
                t.wait()

    return send, finish


def _sum_slots_call(name, parts, core):
    n, R, C = parts.shape
    tr = _pick(R, 256, 8)

    def body(c_ref, p_ref, o_ref):
        acc = p_ref[0].astype(F32)
        for s in range(1, n):
            acc = acc + p_ref[s].astype(F32)
        o_ref[...] = acc

    grid_spec = pltpu.PrefetchScalarGridSpec(
        num_scalar_prefetch=1, grid=(R // tr,),
        in_specs=[pl.BlockSpec((n, tr, C), lambda i, c_ref: (0, i, 0))],
        out_specs=pl.BlockSpec((None, tr, C), lambda i, c_ref: (c_ref[0], i, 0)))
    return pl.pallas_call(
        body, name=name, grid_spec=grid_spec, out_shape=jax.ShapeDtypeStruct((2, R, C), F32),
        compiler_params=_params(("arbitrary",)),
    )(core, parts)


def _sibling_share_call(halves):
    na = len(halves)
    nch = 2 * _D2D_CHUNKS

    def body(*refs):
        h_refs, o_refs = refs[:na], refs[na:2 * na]
        send_sems, recv_sems = refs[2 * na:]
        x, y, c = _my_place()

        def cp(a, j, slot):
            rc = h_refs[a].shape[1] // nch
            rows = pl.ds(j * rc, rc)
            return pltpu.make_async_remote_copy(src_ref=h_refs[a].at[slot, rows], dst_ref=o_refs[a].at[slot, rows],
                                                send_sem=send_sems.at[a * nch + j], recv_sem=recv_sems.at[a * nch + j],
                                                device_id=(x, y, 1 - c), device_id_type=MESH)

        copies = [cp(a, j, c) for a in range(na) for j in range(nch)]
        for t in copies:
            t.start()
        for a in range(na):
            for j in range(nch):
                cp(a, j, 1 - c).wait_recv()
        for t in copies:
            t.wait_send()

    for h in halves:
        assert h.shape[1] % (nch * 8) == 0, h.shape
    outs = [jax.ShapeDtypeStruct(h.shape, h.dtype) for h in halves]
    return _comm_call("sibling_share", body, halves, outs, [na * nch, na * nch], aliases={a: a for a in range(na)})


def _allreduce_small_call(part):
    R, C = part.shape

    def body(p_ref, o_ref, slots, send_sems, recv_sems):
        x, y, c = _my_place()
        me = 4 * x + 2 * y + c
        peers = []
        for k in range(1, 8):
            px = x ^ (k >> 2) if (k >> 2) else x
            py = y ^ ((k >> 1) & 1) if ((k >> 1) & 1) else y
            pc = c ^ (k & 1) if (k & 1) else c
            peers.append((px, py, pc))
        copies = []
        for k, (px, py, pc) in enumerate(peers):
            cp = pltpu.make_async_remote_copy(src_ref=p_ref, dst_ref=slots.at[me], send_sem=send_sems.at[k],
                                              recv_sem=recv_sems.at[k], device_id=(px, py, pc), device_id_type=MESH)
            cp.start()
            copies.append(cp)
        slots[me] = p_ref[...]
        for k, (px, py, pc) in enumerate(peers):
            pltpu.make_async_remote_copy(src_ref=p_ref, dst_ref=slots.at[4 * px + 2 * py + pc],
                                         send_sem=send_sems.at[k], recv_sem=recv_sems.at[k],
                                         device_id=(px, py, pc), device_id_type=MESH).wait_recv()
        for cp in copies:
            cp.wait_send()
        acc = slots[0]
        for s in range(1, 8):
            acc = acc + slots[s]
        o_ref[...] = acc

    vm = pl.BlockSpec(memory_space=pltpu.VMEM)
    return pl.pallas_call(
        body, name="allreduce_small", in_specs=[vm], out_specs=vm,
        out_shape=jax.ShapeDtypeStruct((R, C), F32),
        scratch_shapes=[pltpu.VMEM((8, R, C), F32), pltpu.SemaphoreType.DMA((7,)), pltpu.SemaphoreType.DMA((7,))],
        compiler_params=pltpu.CompilerParams(has_side_effects=True, vmem_limit_bytes=_VMEM_LIMIT),
    )(part)


def _adamw_call(name, w, g, m, v):
    shape = w.shape
    cols = shape[-1]
    rows = int(np.prod(shape[:-1])) if len(shape) > 1 else 1
    w2, g2, m2, v2 = (t.reshape(rows, cols) for t in (w, g, m, v))
    tr = _pick(rows, 256, 8)
    c1 = 1.0 / (1.0 - ADAM_B1 ** ADAM_STEP)
    c2 = 1.0 / (1.0 - ADAM_B2 ** ADAM_STEP)

    def body(w_ref, g_ref, m_ref, v_ref, d_ref, nm_ref, nv_ref):
        gr = g_ref[...]
        m_new = ADAM_B1 * m_ref[...] + (1.0 - ADAM_B1) * gr
        v_new = ADAM_B2 * v_ref[...] + (1.0 - ADAM_B2) * (gr * gr)
        m_hat = m_new / (1.0 - ADAM_B1 ** ADAM_STEP)
        v_hat = v_new / (1.0 - ADAM_B2 ** ADAM_STEP)
        d_ref[...] = -ADAM_LR * (m_hat / (jnp.sqrt(v_hat) + ADAM_EPS) + ADAM_WD * w_ref[...])
        nm_ref[...] = m_new
        nv_ref[...] = v_new

    spec = pl.BlockSpec((tr, cols), lambda i: (i, 0))
    d, nm, nv = pl.pallas_call(
        body, name=name, grid=(rows // tr,), in_specs=[spec] * 4, out_specs=[spec] * 3,
        out_shape=[jax.ShapeDtypeStruct((rows, cols), F32)] * 3, compiler_params=_params(("arbitrary",)),
    )(w2, g2, m2, v2)
    return d.reshape(shape), nm.reshape(shape), nv.reshape(shape)


def _pad_rows(flat, mult):
    n = flat.shape[0]
    unit = PACK_COLS * mult
    total = -(-n // unit) * unit
    return jnp.pad(flat, (0, total - n)).reshape(total // PACK_COLS, PACK_COLS)


def _pad_axis(arr, axis, mult):
    n = arr.shape[axis]
    total = -(-n // mult) * mult
    if total == n:
        return arr
    widths = [(0, 0)] * arr.ndim
    widths[axis] = (0, total - n)
    return jnp.pad(arr, widths)


def _shard_slice(arr, axis, blk, nblk=4):
    w = arr.shape[axis] // nblk
    return lax.slice_in_dim(arr, blk * w, (blk + 1) * w, axis=axis)


_NT = (((1,), (1,)), ((), ()))
_TN = (((0,), (0,)), ((), ()))


def _ffn_fwd(tag, x, h, ht, wg, wu, wd, g_next):
    S, D = x.shape
    (wg, gi), (wu, ui), (wd, di) = wg, wu, wd
    NB, Fb = wg.shape[0], wg.shape[2]
    tm = _pick(S, 1024, 8)

    def gate_up(h_ref, wg_ref, wu_ref, a_ref, u_ref, s_ref):
        h_t = _mx(h_ref[...])
        a = jnp.dot(h_t, _mx(wg_ref[...]), preferred_element_type=F32)
        u = jnp.dot(h_t, _mx(wu_ref[...]), preferred_element_type=F32)
        sig = _sigmoid(a)
        silu = a * sig
        a_ref[...] = (u * (sig * (1.0 + a * (1.0 - sig)))).astype(a_ref.dtype)
        u_ref[...] = silu.astype(u_ref.dtype)
        s_ref[...] = (silu * u).astype(s_ref.dtype)

    hid = pl.BlockSpec((None, tm, Fb), lambda b, i: (b, i, 0))
    a, u, s = pl.pallas_call(
        gate_up, name=tag + "_gate_up", grid=(NB, S // tm),
        in_specs=[pl.BlockSpec((tm, D), lambda b, i: (i, 0)), pl.BlockSpec((None, D, Fb), lambda b, i: (b, gi, 0)),
                  pl.BlockSpec((None, D, Fb), lambda b, i: (b, ui, 0))], out_specs=[hid] * 3,
        out_shape=[jax.ShapeDtypeStruct((NB, S, Fb), BF16)] * 3, compiler_params=_params(("arbitrary", "arbitrary")),
    )(h, wg, wu)

    tm2 = _pick(S, 512, 128)

    def down(s_ref, wd_ref, x_ref, g_ref, xo_ref, ho_ref):
        acc = jnp.dot(_mx(s_ref[0]), _mx(wd_ref[0]), preferred_element_type=F32)
        for b in range(1, NB):
            acc = acc + jnp.dot(_mx(s_ref[b]), _mx(wd_ref[b]), preferred_element_type=F32)
        x_new = x_ref[...] + 0.5 * acc
        xo_ref[...] = x_new
        ho_ref[...] = (_rms_stats(x_new)[0] * g_ref[...]).astype(ho_ref.dtype)

    row = pl.BlockSpec((tm2, D), lambda i: (i, 0))
    x_new, h_next = pl.pallas_call(
        down, name=tag + "_down", grid=(S // tm2,),
        in_specs=[pl.BlockSpec((NB, tm2, Fb), lambda i: (0, i, 0)), pl.BlockSpec((NB, Fb, D), lambda i: (0, di, 0)),
                  row, pl.BlockSpec((1, D), lambda i: (0, 0))],
        out_specs=[row, row], out_shape=[jax.ShapeDtypeStruct((S, D), F32), jax.ShapeDtypeStruct((S, D), BF16)],
        compiler_params=_params(("arbitrary",)),
    )(s, wd, x, g_next)
    return x_new, h_next, None, (x, h, ht, a, u, s)


def _ffn_bwd(tag, saved, dx_out, dxb, wg, wu, wd, gain):
    x, h, ht, a, u, s = saved
    S, D = x.shape
    (wg, gi), (wu, ui), (wd, di) = wg, wu, wd
    NB, Fb = wg.shape[0], wg.shape[2]
    tm = _pick(S, 1024, 8)
    tk = _pick(S, 2048, 128)
    nk = S // tk

    def dgate_up(d_ref, wd_ref, a_ref, u_ref, da_ref, du_ref):
        ds = 0.5 * lax.dot_general(_mx(d_ref[...]), _mx(wd_ref[...]), _NT, preferred_element_type=F32)
        da_ref[...] = (ds * a_ref[...].astype(F32)).astype(da_ref.dtype)
        du_ref[...] = (ds * u_ref[...].astype(F32)).astype(du_ref.dtype)

    hid = pl.BlockSpec((None, tm, Fb), lambda b, i: (b, i, 0))
    da, du = pl.pallas_call(
        dgate_up, name=tag + "_dgate_up", grid=(NB, S // tm),
        in_specs=[pl.BlockSpec((tm, D), lambda b, i: (i, 0)), pl.BlockSpec((None, Fb, D), lambda b, i: (b, di, 0)),
                  hid, hid],
        out_specs=[hid, hid], out_shape=[jax.ShapeDtypeStruct((NB, S, Fb), BF16)] * 2,
        compiler_params=_params(("arbitrary", "arbitrary")),
    )(dxb, wd, a, u)

    def dw_down(s_ref, d_ref, o_ref, acc_ref):
        k = pl.program_id(1)
        p = lax.dot_general(_mx(s_ref[...]), _mx(d_ref[...]), _TN, preferred_element_type=F32)

        @pl.when(k == 0)
        def _():
            acc_ref[...] = p

        @pl.when(k != 0)
        def _():
            acc_ref[...] += p

        @pl.when(k == nk - 1)
        def _():
            o_ref[...] = (0.5 * acc_ref[...]).astype(o_ref.dtype)

    hk = pl.BlockSpec((None, tk, Fb), lambda b, k: (b, k, 0))
    dwd = pl.pallas_call(
        dw_down, name=tag + "_dw_down", grid=(NB, nk),
        in_specs=[hk, pl.BlockSpec((tk, D), lambda b, k: (k, 0))],
        out_specs=pl.BlockSpec((None, Fb, D), lambda b, k: (b, 0, 0)),
        out_shape=jax.ShapeDtypeStruct((NB, Fb, D), _WIRE_DTYPE), scratch_shapes=[pltpu.VMEM((Fb, D), F32)],
        compiler_params=_params(("arbitrary", "arbitrary")),
    )(s, dxb)

    def dw_gate_up(h_ref, da_ref, du_ref, og_ref, ou_ref, accg_ref, accu_ref):
        k = pl.program_id(1)
        h_t = _mx(h_ref[...])
        dims = _TN if ht is None else (((1,), (0,)), ((), ()))
        pg = lax.dot_general(h_t, _mx(da_ref[...]), dims, preferred_element_type=F32)
        pu = lax.dot_general(h_t, _mx(du_ref[...]), dims, preferred_element_type=F32)

        @pl.when(k == 0)
        def _():
            accg_ref[...] = pg
            accu_ref[...] = pu

        @pl.when(k != 0)
        def _():
            accg_ref[...] += pg
            accu_ref[...] += pu

        @pl.when(k == nk - 1)
        def _():
            og_ref[...] = accg_ref[...].astype(og_ref.dtype)
            ou_ref[...] = accu_ref[...].astype(ou_ref.dtype)

    wout = pl.BlockSpec((None, D, Fb), lambda b, k: (b, 0, 0))
    dwg, dwu = pl.pallas_call(
        dw_gate_up, name=tag + "_dw_gate_up", grid=(NB, nk),
        in_specs=[pl.BlockSpec((tk, D), lambda b, k: (k, 0)) if ht is None
                  else pl.BlockSpec((D, tk), lambda b, k: (0, k)), hk, hk], out_specs=[wout, wout],
        out_shape=[jax.ShapeDtypeStruct((NB, D, Fb), _WIRE_DTYPE)] * 2,
        scratch_shapes=[pltpu.VMEM((D, Fb), F32)] * 2, compiler_params=_params(("arbitrary", "arbitrary")),
    )(h if ht is None else ht, da, du)

    tm2 = _pick(S, 512, 8)

    def dx_body(da_ref, du_ref, wg_hbm, wu_hbm, x_ref, dxo_ref, g_ref, dx_ref, dxb_ref, dg_ref, wg_v, wu_v, sem):
        i = pl.program_id(0)

        @pl.when(i == 0)
        def _():
            cg = pltpu.make_async_copy(wg_hbm.at[:, pl.ds(gi * D, D), :], wg_v, sem.at[0])
            cu = pltpu.make_async_copy(wu_hbm.at[:, pl.ds(ui * D, D), :], wu_v, sem.at[1])
            cg.start()
            cu.start()
            cg.wait()
            cu.wait()

        dh = None
        for b in range(NB):
            t = lax.dot_general(_mx(da_ref[b]), wg_v[b], _NT, preferred_element_type=F32)
            t = t + lax.dot_general(_mx(du_ref[b]), wu_v[b], _NT, preferred_element_type=F32)
            dh = t if dh is None else dh + t
        dx_n, dg = _rms_bwd(x_ref[...], g_ref[...], dh)
        dx = dxo_ref[...] + dx_n
        dx_ref[...] = dx
        dxb_ref[...] = dx.astype(dxb_ref.dtype)
        dg = jnp.sum(dg, axis=0, keepdims=True)

        @pl.when(i == 0)
        def _():
            dg_ref[...] = dg

        @pl.when(i != 0)
        def _():
            dg_ref[...] += dg

    row = pl.BlockSpec((tm2, D), lambda i: (i, 0))
    hid2 = pl.BlockSpec((NB, tm2, Fb), lambda i: (0, i, 0))
    anyspec = pl.BlockSpec(memory_space=pl.ANY)
    fixed = pl.BlockSpec((1, D), lambda i: (0, 0))
    dx, dxb_new, dgain = pl.pallas_call(
        dx_body, name=tag + "_dx", grid=(S // tm2,),
        in_specs=[hid2, hid2, anyspec, anyspec, row, row, fixed], out_specs=[row, row, fixed],
        out_shape=[jax.ShapeDtypeStruct((S, D), F32), jax.ShapeDtypeStruct((S, D), BF16),
                   jax.ShapeDtypeStruct((1, D), F32)],
        scratch_shapes=[pltpu.VMEM((NB, D, Fb), wg.dtype), pltpu.VMEM((NB, D, Fb), wu.dtype),
                        pltpu.SemaphoreType.DMA((2,))],
        compiler_params=_params(("arbitrary",)),
    )(da, du, wg, wu, x, dx_out, gain)
    return dx, dxb_new, dgain, dwg, dwu, dwd


def _conv_mixer_fwd(tag, x, h, ht, w_in, w_taps, w_out, g_next):
    S, D = x.shape
    C3 = w_in.shape[1]
    tm = _pick(S, 512, 8)
    p, = _fused_matmul(tag + "_in", 'nn', [_op(h)], [_op(w_in)], [(0, 0, 0)], 1, _ident_epi(), [F32],
                       S, C3, D, tm, _pick(C3, 1024, 128), D)
    m = _conv_fwd_call(p, w_taps)
    x_new, h_next = _fused_matmul(tag + "_out", 'nn', [_op(m)], [_op(w_out)], [(0, 0, 0)], 1, _resid_norm_epi(1.0),
                                  [F32, BF16], S, D, D, tm, D, D, tile_extras=[x], row_extras=[g_next])
    return x_new, h_next, (x, h, ht, p, m)


def _conv_mixer_bwd(tag, saved, dx_out, dxb, w_in, w_taps, w_out, gain):
    x, h, ht, p, m = saved
    S, D = x.shape
    C3 = w_in.shape[1]
    tm = _pick(S, 512, 8)
    tk = _pick(S, 2048, 128)
    dm, = _fused_matmul(tag + "_dm", 'nt', [_op(dxb)], [_op(w_out)], [(0, 0, 0)], 1, _ident_epi(), [F32],
                        S, D, D, tm, D, D)
    dw_out, = _fused_matmul(tag + "_dw_out", 'tn', [_op(m)], [_op(dxb)], [(0, 0, 0)], 1, _ident_epi(), [F32],
                            D, D, S, D, D, tk)
    dp, dtaps = _conv_bwd_call(p, w_taps, dm)
    dw_in, = _fused_matmul(tag + "_dw_in", 'tn' if ht is None else 'nn', [_op(h if ht is None else ht)], [_op(dp)],
                           [(0, 0, 0)], 1, _ident_epi(), [F32],
                           D, C3, S, D, _pick(C3, 1024, 128), tk)
    dx, dxb_new, dgain = _fused_matmul(tag + "_dx", 'nt', [_op(dp)], [_op(w_in)], [(0, 0, 0)], 1, _norm_bwd_epi,
                                       [F32, BF16], S, D, C3, tm, D, C3,
                                       tile_extras=[x, dx_out], row_extras=[gain], n_colsum=1)
    return dx, dxb_new, dgain, dw_in, dtaps, dw_out


def _attn_scale():
    return np.float32(QK_DIM ** -0.5)


def _even_mixer_fwd(tag, x, h, ht, wts, tables, g_next, carry=()):
    S, D = x.shape
    cos, sa, sb = tables
    tm = _pick(S, 512, 8)
    AW = HEADS * HP
    proj, = _fused_matmul(tag + "_in", 'nn', [_op(h)], [_op(wts['w_in'])], [(0, 0, 0)], 1, _ident_epi(), [F32],
                          S, PROJ_W, D, tm, _pick(PROJ_W, 896, 128), D)
    cqn, ckvn, kr, u, vn = _even_prep_call(proj, wts['q_norm'], wts['kv_norm'], wts['sg_norm'], cos, sa, sb)
    scale = _attn_scale()

    def q_epi(accs, tiles, rows, mrows):
        c_t, a_t, b_t = mrows
        heads = [_rope(accs[0][:, hh * HP:(hh + 1) * HP], c_t, a_t, b_t) * scale for hh in range(HEADS)]
        return [jnp.concatenate(heads, axis=1)]

    q, = _fused_matmul(tag + "_q", 'nn', [_op(cqn)], [_op(wts['w_q'])], [(0, 0, 0)], 1, q_epi, [BF16],
                       S, AW, Q_LORA, tm, AW, Q_LORA, mrow_extras=[cos, sa, sb])

    def kv_epi(accs, tiles, rows, mrows):
        lane = lax.broadcasted_iota(jnp.int32, accs[1].shape, 1)
        v_t = jnp.where((lane & (HP - 1)) == VDIM, 1.0, accs[1])
        return [accs[0] + jnp.concatenate([mrows[0].astype(F32)] * HEADS, axis=1), v_t]

    k, v = _fused_matmul(tag + "_kv", 'nn', [_op(ckvn)], [_op(wts['w_k']), _op(wts['w_v'])],
                         [(0, 0, 0), (0, 1, 1)], 2, kv_epi, [BF16, BF16], S, AW, KV_LORA, tm, AW, KV_LORA,
                         mrow_extras=[kr])
    o, lse, gathered = _flash_fwd_call(q, k, v, carry)
    mix = _sgu_fwd_call(vn, u, o, wts['sg_wst'], wts['sg_bexp'])
    x_new, h_next = _fused_matmul(tag + "_out", 'nn', [_op(mix)], [_op(wts['w_out'])], [(0, 0, 0)], 1,
                                  _resid_norm_epi(1.0), [F32, BF16], S, D, AW + SG_WIDTH, tm, D, AW + SG_WIDTH,
                                  tile_extras=[x], row_extras=[g_next])
    return x_new, h_next, (x, h, ht, proj, cqn, ckvn, u, vn, q, k, v, o, lse, mix), gathered


def _even_mixer_bwd(tag, saved, dx_out, dxb, wts, tables, gain, carry=()):
    x, h, ht, proj, cqn, ckvn, u, vn, q, k, v, o, lse, mix = saved
    S, D = x.shape
    cos, sa, sb = tables
    tm = _pick(S, 512, 8)
    tk = _pick(S, 2048, 128)
    AW = HEADS * HP
    MW = AW + SG_WIDTH
    dmix, = _fused_matmul(tag + "_dmix", 'nt', [_op(dxb)], [_op(wts['w_out'])], [(0, 0, 0)], 1, _ident_epi(), [BF16],
                          S, MW, D, tm, _pick(MW, 768, 128), D)
    dw_out, = _fused_matmul(tag + "_dw_out", 'tn', [_op(mix)], [_op(dxb)], [(0, 0, 0)], 1, _ident_epi(), [F32],
                            MW, D, S, _pick(MW, 768, 128), D, tk)
    du, dvn, dsg_w, dsg_b = _sgu_bwd_call(dmix, vn, u, wts['sg_wst'], wts['sg_wst_t'], wts['sg_bexp'])
    delta = _attn_delta_call(o, dmix)
    dq, dk, dv, arrived = _flash_bwd_call(q, k, v, dmix, lse, delta, carry)
    scale = _attn_scale()

    def dq_epi(accs, tiles, rows, mrows):
        return accs

    def dq_pre_call():
        tr = _pick(S, 256, 8)

        def body(d_ref, c_ref, a_ref, b_ref, o_ref):
            for hh in range(HEADS):
                t = _rope_t(d_ref[:, hh * HP:(hh + 1) * HP], c_ref[...], a_ref[...], b_ref[...]) * scale
                o_ref[:, hh * HP:(hh + 1) * HP] = t.astype(o_ref.dtype)

        row = lambda i: (i, 0)
        return pl.pallas_call(
            body, name=tag + "_dq_unrope", grid=(S // tr,),
            in_specs=[pl.BlockSpec((tr, AW), row)] + [pl.BlockSpec((tr, HP), row)] * 3,
            out_specs=pl.BlockSpec((tr, AW), row), out_shape=jax.ShapeDtypeStruct((S, AW), BF16),
            compiler_params=_params(("arbitrary",)),
        )(dq, cos, sa, sb)

    dqp = dq_pre_call()
    dw_q, = _fused_matmul(tag + "_dw_q", 'tn', [_op(cqn)], [_op(dqp)], [(0, 0, 0)], 1, _ident_epi(), [F32],
                          Q_LORA, AW, S, Q_LORA, AW, tk)
    dcqn, = _fused_matmul(tag + "_dcq", 'nt', [_op(dqp)], [_op(wts['w_q'])], [(0, 0, 0)], 1, dq_epi, [F32],
                          S, Q_LORA, AW, tm, Q_LORA, AW)
    dw_k, dw_v = _fused_matmul(tag + "_dw_kv", 'tn', [_op(ckvn)], [_op(dk), _op(dv)], [(0, 0, 0), (0, 1, 1)], 2,
                               _ident_epi(), [F32, F32], KV_LORA, AW, S, KV_LORA, AW, tk)
    dckvn, = _fused_matmul(tag + "_dckv", 'nt', [_op(dk), _op(dv)], [_op(wts['w_k']), _op(wts['w_v'])],
                           [(0, 0, 0), (1, 1, 0)], 1, dq_epi, [F32], S, KV_LORA, AW, tm, KV_LORA, AW)
    dproj, dqn, dkvn, dsgn = _even_prep_bwd_call(proj, wts['q_norm'], wts['kv_norm'], wts['sg_norm'], cos, sa, sb,
                                                 dcqn, dckvn, dk, du, dvn)
    dw_in, = _fused_matmul(tag + "_dw_in", 'tn' if ht is None else 'nn', [_op(h if ht is None else ht)],
                           [_op(dproj)], [(0, 0, 0)], 1, _ident_epi(), [F32],
                           D, PROJ_W, S, D, _pick(PROJ_W, 896, 128), tk)
    dx, dxb_new, dgain = _fused_matmul(tag + "_dx", 'nt', [_op(dproj)], [_op(wts['w_in'])], [(0, 0, 0)], 1,
                                       _norm_bwd_epi, [F32, BF16], S, D, PROJ_W, tm, D, PROJ_W,
                                       tile_extras=[x, dx_out], row_extras=[gain], n_colsum=1)
    grads = dict(w_in=dw_in, w_q=dw_q, w_k=dw_k, w_v=dw_v, w_out=dw_out, q_norm=dqn, kv_norm=dkvn, sg_norm=dsgn,
                 sg_w=dsg_w, sg_b=dsg_b)
    return dx, dxb_new, dgain, grads, arrived


def _even_weights(w_in, w_uq, w_ukv, w_out, q_norm, kv_norm, sg_norm, sg_w, sg_b):
    D = w_in.shape[0]
    kr_cols = jnp.pad(w_in[:, Q_LORA + KV_LORA:Q_LORA + KV_LORA + ROPE], ((0, 0), (NOPE, HP - QK_DIM)))
    w_in_p = jnp.concatenate([w_in[:, :Q_LORA + KV_LORA], kr_cols, w_in[:, Q_LORA + KV_LORA + ROPE:]], axis=1)
    wq = w_uq.reshape(Q_LORA, HEADS, QK_DIM)
    w_q = jnp.pad(wq, ((0, 0), (0, 0), (0, HP - QK_DIM))).reshape(Q_LORA, HEADS * HP)
    wkv = w_ukv.reshape(KV_LORA, HEADS, NOPE + VDIM)
    w_k = jnp.pad(wkv[:, :, :NOPE], ((0, 0), (0, 0), (0, HP - NOPE))).reshape(KV_LORA, HEADS * HP)
    w_v = jnp.pad(wkv[:, :, NOPE:], ((0, 0), (0, 0), (0, HP - VDIM))).reshape(KV_LORA, HEADS * HP)
    wo_a = w_out[:HEADS * VDIM].reshape(HEADS, VDIM, D)
    wo_a = jnp.pad(wo_a, ((0, 0), (0, HP - VDIM), (0, 0))).reshape(HEADS * HP, D)
    w_out_p = jnp.concatenate([wo_a, w_out[HEADS * VDIM:]], axis=0)
    tri = jnp.tril(jnp.ones((SG_CHUNK, SG_CHUNK), F32))
    wm = sg_w * tri
    wst = wm.reshape(SG_GROUPS // 2, 2 * SG_CHUNK, SG_CHUNK).astype(_MXU_DTYPE)
    wst_t = jnp.swapaxes(wm, 1, 2).reshape(SG_GROUPS // 2, 2 * SG_CHUNK, SG_CHUNK).astype(_MXU_DTYPE)
    bexp = jnp.repeat(sg_b.T, SG_GDIM, axis=1)
    return dict(w_in=w_in_p, w_q=w_q, w_k=w_k, w_v=w_v, w_out=w_out_p, sg_wst=wst, sg_wst_t=wst_t, sg_bexp=bexp,
                q_norm=q_norm.reshape(1, -1), kv_norm=kv_norm.reshape(1, -1), sg_norm=sg_norm.reshape(1, -1))


def _even_grads_unpad(g):
    d_in = g['w_in']
    kr0 = Q_LORA + KV_LORA
    dw_in = jnp.concatenate([d_in[:, :kr0], d_in[:, kr0 + NOPE:kr0 + QK_DIM], d_in[:, kr0 + HP:]], axis=1)
    dw_uq = g['w_q'].reshape(Q_LORA, HEADS, HP)[:, :, :QK_DIM].reshape(Q_LORA, HEADS * QK_DIM)
    dk = g['w_k'].reshape(KV_LORA, HEADS, HP)[:, :, :NOPE]
    dv = g['w_v'].reshape(KV_LORA, HEADS, HP)[:, :, :VDIM]
    dw_ukv = jnp.concatenate([dk, dv], axis=2).reshape(KV_LORA, HEADS * (NOPE + VDIM))
    D = d_in.shape[0]
    wo = g['w_out']
    wo_a = wo[:HEADS * HP].reshape(HEADS, HP, D)[:, :VDIM].reshape(HEADS * VDIM, D)
    dw_out = jnp.concatenate([wo_a, wo[HEADS * HP:]], axis=0)
    dsg_b = g['sg_b'][:, :SG_GROUPS].T
    return dict(even_w_in=dw_in, w_uq=dw_uq, w_ukv=dw_ukv, even_w_out=dw_out, q_norm=g['q_norm'][0],
                kv_norm=g['kv_norm'][0], sg_norm=g['sg_norm'][0], sg_w=g['sg_w'], sg_b=dsg_b)


def kernel(x, positions, ffn_pre_norm, ffn_pre_w_gate, ffn_pre_w_up, ffn_pre_w_down, mix_norm, ffn_post_norm, ffn_post_w_gate, ffn_post_w_up, ffn_post_w_down, even_w_in, q_norm, w_uq, kv_norm, w_ukv, sg_norm, sg_w, sg_b, even_w_out, conv_w_in, conv_w, conv_w_out, final_norm, loss_target, m_ffn_pre_norm, m_ffn_pre_w_gate, m_ffn_pre_w_up, m_ffn_pre_w_down, m_mix_norm, m_ffn_post_norm, m_ffn_post_w_gate, m_ffn_post_w_up, m_ffn_post_w_down, m_even_w_in, m_q_norm, m_w_uq, m_kv_norm, m_w_ukv, m_sg_norm, m_sg_w, m_sg_b, m_even_w_out, m_conv_w_in, m_conv_w, m_conv_w_out, m_final_norm, v_ffn_pre_norm, v_ffn_pre_w_gate, v_ffn_pre_w_up, v_ffn_pre_w_down, v_mix_norm, v_ffn_post_norm, v_ffn_post_w_gate, v_ffn_post_w_up, v_ffn_post_w_down, v_even_w_in, v_q_norm, v_w_uq, v_kv_norm, v_w_ukv, v_sg_norm, v_sg_w, v_sg_b, v_even_w_out, v_conv_w_in, v_conv_w, v_conv_w_out, v_final_norm):
    env = dict(locals())
    w_loc = {n: env[n] for n in WEIGHTS}
    m_loc = {n: env['m_' + n] for n in WEIGHTS}
    v_loc = {n: env['v_' + n] for n in WEIGHTS}
    S, D = x.shape[1], x.shape[2]
    depth = ffn_pre_norm.shape[0]
    xs = x.reshape(S, D)
    target = loss_target.reshape(S, D)

    def layers_of(n, early):
        count = w_loc[n].shape[0]
        if n in ('conv_w_in', 'conv_w', 'conv_w_out'):
            return [] if early else list(range(count))
        return [0] if early else list(range(1, count))

    def shards_of(early):
        wire = lambda n, l: w_loc[n][l].astype(_WIRE_DTYPE)
        keys = [[(n, l) for n in group for l in layers_of(n, early)] for group in (GROUP_A, GROUP_B, GATHER_C)]
        sa = _pad_axis(jnp.concatenate([wire(n, l) for n, l in keys[0]], axis=0), 0, PACK_ROW_MULT)
        sb = _pad_axis(jnp.concatenate([wire(n, l) for n, l in keys[1]], axis=0), 0, PACK_ROW_MULT)
        sc = _pad_rows(jnp.concatenate([wire(n, l).reshape(-1) for n, l in keys[2]]), PACK_ROW_MULT)
        return [sa, sb, sc], keys

    full = {n: {} for n in SHARDED}

    def unpack(gathered, keys):
        gat_a, gat_b, gat_c = gathered
        for idx, (n, l) in enumerate(keys[0]):
            full[n][l] = (gat_a, idx)
        row = 0
        for n, l in keys[1]:
            rows = w_loc[n].shape[1]
            if n in FFN_WEIGHTS:
                full[n][l] = (gat_b, row // rows)
            else:
                full[n][l] = jnp.concatenate([gat_b[b, row:row + rows] for b in range(4)], axis=0)
            row += rows
        gflat = gat_c.reshape(4, -1)
        off = 0
        for n, l in keys[2]:
            shp = w_loc[n].shape[1:]
            size = int(np.prod(shp))
            full[n][l] = jnp.concatenate([gflat[b, off:off + size].reshape(shp) for b in range(4)],
                                         axis=SHARD_AXIS[n] - 1)
            off += size

    early_shards, early_keys = shards_of(True)
    unpack(_gather_halves_call(early_shards), early_keys)
    late_shards, late_keys = shards_of(False)
    taps = _gather_weights_call("gather_taps", _pad_rows(conv_w.reshape(-1), 8)).reshape(4, -1)
    taps = jnp.concatenate([taps[b, :conv_w.size].reshape(conv_w.shape) for b in range(4)], axis=2)

    inv_freq = ROPE_THETA ** (-jnp.arange(0, ROPE, 2, dtype=F32) / ROPE)
    half = ROPE // 2
    zeros = lambda n: jnp.zeros((n,), F32)
    ones = jnp.ones((half,), F32)
    invf = jnp.concatenate([zeros(NOPE), inv_freq, inv_freq, zeros(HP - QK_DIM)]).reshape(1, HP)
    mask_a = jnp.concatenate([zeros(NOPE), -ones, zeros(HP - NOPE - half)]).reshape(1, HP)
    mask_b = jnp.concatenate([zeros(NOPE + half), ones, zeros(HP - QK_DIM)]).reshape(1, HP)
    tables = _rope_tables_call(positions.reshape(S, 1), invf, mask_a, mask_b)

    even_w = {}

    def even_weights_of(e):
        if e not in even_w:
            even_w[e] = _even_weights(full['even_w_in'][e], full['w_uq'][e], full['w_ukv'][e], full['even_w_out'][e],
                                      q_norm[e], kv_norm[e], sg_norm[e], sg_w[e], sg_b[e])
        return even_w[e]

    def gain_row(arr, l):
        return arr[l].reshape(1, D)

    saved = []
    h, ht = _rmsnorm_call("first_norm", xs, gain_row(ffn_pre_norm, 0))
    xc = xs
    for l in range(depth):
        xc, h, ht, s_pre = _ffn_fwd(f"l{l}_pre", xc, h, ht, full['ffn_pre_w_gate'][l], full['ffn_pre_w_up'][l],
                                    full['ffn_pre_w_down'][l], gain_row(mix_norm, l))
        if l % 2 == 0:
            xc, h, s_mix, gathered = _even_mixer_fwd(f"l{l}_mix", xc, h, ht, even_weights_of(l // 2), tables,
                                                     gain_row(ffn_post_norm, l), late_shards if l == 0 else ())
            if l == 0:
                unpack(gathered, late_keys)
        else:
            o = l // 2
            xc, h, s_mix = _conv_mixer_fwd(f"l{l}_mix", xc, h, ht, full['conv_w_in'][o], taps[o],
                                           full['conv_w_out'][o], gain_row(ffn_post_norm, l))
        g_next = gain_row(ffn_pre_norm, l + 1) if l + 1 < depth else final_norm.reshape(1, D)
        xc, h, ht, s_post = _ffn_fwd(f"l{l}_post", xc, h, None, full['ffn_post_w_gate'][l], full['ffn_post_w_up'][l],
                                     full['ffn_post_w_down'][l], g_next)
        saved.append((s_pre, s_mix, s_post))

    dx, dxb, d_final, loss_part = _loss_call(xc, target, final_norm.reshape(1, D))
    loss = lax.psum(loss_part[0, 0], ("x", "y", "c"))

    gl = {n: [None] * w_loc[n].shape[0] for n in WEIGHTS if n != 'final_norm'}
    core = lax.axis_index("c").astype(jnp.int32).reshape(1)

    def pair_sums(first, tag):
        keys = [[(n, l) for n in group for l in layers_of(n, first)] for group in (GROUP_A, GROUP_B, GROUP_C)]

        def rows_blocked(n, l):
            g = gl[n][l]
            return g.reshape(4, g.shape[0] // 4, g.shape[1]).astype(_WIRE_DTYPE)

        pack_a = jnp.concatenate([gl[n][l] for n, l in keys[0]], axis=1)
        pack_b = jnp.concatenate([gl[n][l] if n in FFN_WEIGHTS else rows_blocked(n, l) for n, l in keys[1]], axis=1)
        pack_c = jnp.stack([_pad_rows(jnp.concatenate(
            [_shard_slice(gl[n][l], SHARD_AXIS[n] - 1, b).astype(_WIRE_DTYPE).reshape(-1) for n, l in keys[2]]),
            PACK_ROW_MULT) for b in range(4)])
        packs = [_pad_axis(p, 1, PACK_ROW_MULT) for p in (pack_a, pack_b, pack_c)]
        packs = [p.reshape(4, 2, p.shape[1] // 2, p.shape[2]) for p in packs]
        theirs = _pair_exchange_call(packs, tag)
        return [_pair_add_call(f"pair_add_{tag}_{i}", p, t, core)
                for i, (p, t) in enumerate(zip(packs, theirs))], keys
    for l in reversed(range(depth)):
        s_pre, s_mix, s_post = saved[l]
        dx, dxb, dgain, dwg, dwu, dwd = _ffn_bwd(f"l{l}_post", s_post, dx, dxb, full['ffn_post_w_gate'][l],
                                                 full['ffn_post_w_up'][l], full['ffn_post_w_down'][l],
                                                 gain_row(ffn_post_norm, l))
        gl['ffn_post_norm'][l] = dgain[0]
        gl['ffn_post_w_gate'][l], gl['ffn_post_w_up'][l], gl['ffn_post_w_down'][l] = dwg, dwu, dwd
        if l % 2 == 0:
            e = l // 2
            if l == 0:
                pairs_rest, keys_rest = pair_sums(False, "rest")
            dx, dxb, dgain, eg, arrived = _even_mixer_bwd(f"l{l}_mix", s_mix, dx, dxb, even_weights_of(e), tables,
                                                          gain_row(mix_norm, l), pairs_rest if l == 0 else ())
            if l == 0:
                arrived_rest = arrived
            for n, val in _even_grads_unpad(eg).items():
                gl[n][e] = val
        else:
            o = l // 2
            dx, dxb, dgain, dw_in, dtaps, dw_out = _conv_mixer_bwd(f"l{l}_mix", s_mix, dx, dxb, full['conv_w_in'][o],
                                                                   taps[o], full['conv_w_out'][o],
                                                                   gain_row(mix_norm, l))
            gl['conv_w_in'][o], gl['conv_w'][o], gl['conv_w_out'][o] = dw_in, dtaps, dw_out
        gl['mix_norm'][l] = dgain[0]
        dx, dxb, dgain, dwg, dwu, dwd = _ffn_bwd(f"l{l}_pre", s_pre, dx, dxb, full['ffn_pre_w_gate'][l],
                                                 full['ffn_pre_w_up'][l], full['ffn_pre_w_down'][l],
                                                 gain_row(ffn_pre_norm, l))
        gl['ffn_pre_norm'][l] = dgain[0]
        gl['ffn_pre_w_gate'][l], gl['ffn_pre_w_up'][l], gl['ffn_pre_w_down'][l] = dwg, dwu, dwd
    grad_x = dx.reshape(x.shape)
    part = {n: jnp.stack(gl[n]) for n in gl if n not in FFN_WEIGHTS}
    part['final_norm'] = d_final[0]

    pairs, keys_first = pair_sums(True, "first")
    arrived_first = _chip_scatter_call(pairs)
    mine = [_sum_slots_call(f"sum_grad_slots_{i}", r, core) for i, r in enumerate(list(arrived_rest) + list(arrived_first))]
    reduced = [t.reshape(-1, t.shape[2]) for t in _sibling_share_call(mine)]
    per_layer = {n: {} for n in SHARDED}
    for (red_a, red_b, red_c), keys in ((reduced[:3], keys_rest), (reduced[3:], keys_first)):
        for idx, (n, l) in enumerate(keys[0]):
            per_layer[n][l] = red_a[idx * D:(idx + 1) * D]
        row = 0
        for n, l in keys[1]:
            rows = w_loc[n].shape[1]
            per_layer[n][l] = red_b[row:row + rows]
            row += rows
        red_c = red_c.reshape(-1)
        off = 0
        for n, l in keys[2]:
            shp = w_loc[n].shape[1:]
            size = int(np.prod(shp))
            per_layer[n][l] = red_c[off:off + size].reshape(shp)
            off += size
    grads = {n: jnp.stack([per_layer[n][l] for l in range(w_loc[n].shape[0])]) for n in SHARDED}

    small = _pad_rows(jnp.concatenate([part[n].reshape(-1) for n in REPLICATED]), 8)
    small_sum = _allreduce_small_call(small).reshape(-1)
    off = 0
    for n in REPLICATED:
        size = int(np.prod(w_loc[n].shape))
        grads[n] = small_sum[off:off + size].reshape(w_loc[n].shape)
        off += size

    deltas, new_m, new_v = {}, {}, {}
    for n in WEIGHTS:
        deltas[n], new_m[n], new_v[n] = _adamw_call("adamw_" + n, w_loc[n], grads[n], m_loc[n], v_loc[n])
    return (loss, grad_x, *[grads[n] for n in WEIGHTS], *[deltas[n] for n in WEIGHTS],
            *[new_m[n] for n in WEIGHTS], *[new_v[n] for n in WEIGHTS])
```
